```python
import math
import jax
import jax.numpy as jnp
from jax import lax
import numpy as np

D_MODEL = 2048
BATCH = 1
SEQ = 8192
DEPTH = 1
DEC_BATCH = 128
DEC_SEQ = 4
PAST_LEN = 8192
PAGE_SIZE = 128

D_FF = 5632
SWA_HEADS = 16
SWA_KV_HEADS = 4
SWA_GROUP = SWA_HEADS // SWA_KV_HEADS
SWA_HEAD_DIM = 64
SWA_Q_DIM = SWA_HEADS * SWA_HEAD_DIM
SWA_KV_DIM = SWA_KV_HEADS * SWA_HEAD_DIM
WINDOW = 128
N_BUCKETS = 32
MAX_DISTANCE = 128
RWKV_HEADS = 8
RWKV_HEAD_DIM = 64
RWKV_DIM = RWKV_HEADS * RWKV_HEAD_DIM
LORA_W = 64
LORA_A = 64
LORA_G = 128
RWKV_IN = 3 * RWKV_DIM + LORA_W + LORA_A + LORA_G
N_MEM = 256
MEM_HEADS = 4
MEM_HEAD_DIM = 128
MEM_DIM = MEM_HEADS * MEM_HEAD_DIM
N_BRANCH = 3
IN_DIM = SWA_Q_DIM + 2 * SWA_KV_DIM + RWKV_IN + MEM_DIM + N_BRANCH * D_MODEL
IN_SPLITS = (SWA_Q_DIM, SWA_Q_DIM + SWA_KV_DIM, SWA_Q_DIM + 2 * SWA_KV_DIM,
             SWA_Q_DIM + 2 * SWA_KV_DIM + RWKV_IN, SWA_Q_DIM + 2 * SWA_KV_DIM + RWKV_IN + MEM_DIM)
RWKV_SPLITS = (RWKV_DIM, 2 * RWKV_DIM, 3 * RWKV_DIM, 3 * RWKV_DIM + LORA_W, 3 * RWKV_DIM + LORA_W + LORA_A)
NORM_EPS = 1e-6
GN_EPS = 64e-5
NEG_INF = -1e30

kernel_name = 'hybrid_swa_rwkv7_memxattn_macaron_step'


def rmsnorm(x, g):
    xf = x.astype(jnp.float32)
    y = xf * lax.rsqrt(jnp.mean(xf * xf, axis=-1, keepdims=True) + NORM_EPS)
    return (y * g.astype(jnp.float32)).astype(x.dtype)


def swiglu(x, wi, wo):
    gate, up = jnp.split(x @ wi, 2, axis=-1)
    return (jax.nn.silu(gate) * up) @ wo


def t5_bucket(dist):
    max_exact = N_BUCKETS // 2
    d = jnp.maximum(dist, 0)
    log_ratio = jnp.log(jnp.maximum(d, 1).astype(jnp.float32) / max_exact) / math.log(MAX_DISTANCE / max_exact)
    large = jnp.minimum(max_exact + (log_ratio * (N_BUCKETS - max_exact)).astype(jnp.int32), N_BUCKETS - 1)
    return jnp.where(d < max_exact, d, large)


def rel_bias(dist, table):
    b = table.astype(jnp.float32)[t5_bucket(dist)]
    return jnp.moveaxis(b, -1, 0).reshape(SWA_KV_HEADS, SWA_GROUP, *dist.shape)


def sink_attend(q, k, v, bias, mask, sinks):
    logits = jnp.einsum('...qhgd,...jhd->...hgqj', q, k, preferred_element_type=jnp.float32)
    logits = jnp.where(mask, logits * (SWA_HEAD_DIM ** -0.5) + bias, NEG_INF)
    s = sinks.astype(jnp.float32)[:, :, None, None]
    m = jnp.maximum(jnp.max(logits, axis=-1, keepdims=True), s)
    p = jnp.exp(logits - m)
    p = (p / (jnp.sum(p, axis=-1, keepdims=True) + jnp.exp(s - m))).astype(v.dtype)
    return jnp.einsum('...hgqj,...jhd->...qhgd', p, v)


def swa_prompt(q, k, v, sinks, table):
    B, T = q.shape[:2]
    nb = T // WINDOW
    qb = q.reshape(B, nb, WINDOW, SWA_KV_HEADS, SWA_GROUP, SWA_HEAD_DIM)

    def band(t):
        tb = t.reshape(B, nb, WINDOW, SWA_KV_HEADS, SWA_HEAD_DIM)
        prev = jnp.concatenate([jnp.zeros_like(tb[:, :1]), tb[:, :-1]], axis=1)
        return jnp.concatenate([prev, tb], axis=2)

    i = jnp.arange(WINDOW)[:, None]
    j = jnp.arange(2 * WINDOW)[None, :]
    dist = i + WINDOW - j
    blk = jnp.arange(nb)[:, None, None]
    mask = (dist >= 0) & (dist < WINDOW) & (blk * WINDOW + j - WINDOW >= 0)
    o = sink_attend(qb, band(k), band(v), rel_bias(dist, table), mask[:, None, None], sinks)
    return o.reshape(B, T, SWA_Q_DIM)


def swa_sample(q, k, v, k_buf, v_buf, sinks, table):
    Bd, T = q.shape[:2]
    w_buf = k_buf.shape[1]
    keys = jnp.concatenate([k_buf.astype(k.dtype), k], axis=1)
    vals = jnp.concatenate([v_buf.astype(v.dtype), v], axis=1)
    dist = jnp.arange(T)[:, None] + w_buf - jnp.arange(w_buf + T)[None, :]
    mask = (dist >= 0) & (dist < WINDOW)
    o = sink_attend(q, keys, vals, rel_bias(dist, table), mask, sinks)
    return o.reshape(Bd, T, SWA_Q_DIM), keys[:, T:], vals[:, T:]


def mem_attend(q, mem_k, mem_v):
    B, T = q.shape[:2]
    qh = q.reshape(B, T, MEM_HEADS, MEM_HEAD_DIM)
    logits = jnp.einsum('bthd,bmhd->bhtm', qh, mem_k.astype(q.dtype), preferred_element_type=jnp.float32)
    p = jax.nn.softmax(logits * (MEM_HEAD_DIM ** -0.5), axis=-1).astype(q.dtype)
    return jnp.einsum('bhtm,bmhd->bthd', p, mem_v.astype(q.dtype)).reshape(B, T, MEM_DIM)


def rwkv7_scan(s0, r, decay, k, v, kk, a):
    def step(S, inp):
        r_t, w_t, k_t, v_t, kk_t, a_t = inp
        sa = jnp.einsum('bhvk,bhk->bhv', S, -kk_t)
        S = S * w_t[:, :, None, :] + sa[..., None] * (kk_t * a_t)[:, :, None, :] + v_t[..., None] * k_t[:, :, None, :]
        return S, jnp.einsum('bhvk,bhk->bhv', S, r_t)

    xs = tuple(jnp.moveaxis(t, 1, 0) for t in (r, decay, k, v, kk, a))
    S, ys = lax.scan(step, s0, xs)
    return S, jnp.moveaxis(ys, 0, 1)


def rwkv7(xr, shift0, s0, lp):
    B, T, _ = xr.shape
    f32 = jnp.float32
    prev = jnp.concatenate([shift0.astype(xr.dtype), xr[:, :-1]], axis=1)
    xs = xr + lp['rwkv_mu'] * (prev - xr)
    r, k, v, lw, la, lg = jnp.split(xs, RWKV_SPLITS, axis=-1)
    w = -jax.nn.softplus(-(lp['rwkv_w0'] + jnp.tanh(lw) @ lp['rwkv_w_w2']).astype(f32)) - 0.5
    decay = jnp.exp(-jnp.exp(w))
    a = jax.nn.sigmoid((lp['rwkv_a0'] + la @ lp['rwkv_a_w2']).astype(f32))
    g = jax.nn.sigmoid(lg) @ lp['rwkv_g_w2']

    def heads(t):
        return t.astype(f32).reshape(B, T, RWKV_HEADS, RWKV_HEAD_DIM)

    kf = k.astype(f32)
    kk = heads(kf * lp['rwkv_k_k'].astype(f32))
    kk = kk / jnp.maximum(jnp.sqrt(jnp.sum(kk * kk, axis=-1, keepdims=True)), 1e-12)
    kh = heads(kf * (1.0 + (a - 1.0) * lp['rwkv_k_a'].astype(f32)))
    rh, vh, ah, wh = heads(r), heads(v), heads(a), heads(decay)
    S, y = rwkv7_scan(s0.astype(f32), rh, wh, kh, vh, kk, ah)
    mu = jnp.mean(y, axis=-1, keepdims=True)
    var = jnp.mean(jnp.square(y - mu), axis=-1, keepdims=True)
    yn = ((y - mu) * lax.rsqrt(var + GN_EPS)).reshape(B, T, RWKV_DIM)
    yn = yn * lp['rwkv_ln_w'].astype(f32) + lp['rwkv_ln_b'].astype(f32)
    r_k = lp['rwkv_r_k'].astype(f32).reshape(RWKV_HEADS, RWKV_HEAD_DIM)
    bonus = (jnp.sum(rh * kh * r_k, axis=-1, keepdims=True) * vh).reshape(B, T, RWKV_DIM)
    out = (yn + bonus).astype(xr.dtype) * g
    return out, S, xr[:, -1:]


def layer(x, mem_k, mem_v, k_buf, v_buf, rwkv_s0, shift0, lp, table):
    B, T, _ = x.shape
    h = x + 0.5 * swiglu(rmsnorm(x, lp['ffn1_norm']), lp['ffn1_wi'], lp['ffn1_wo'])
    u = rmsnorm(h, lp['mix_norm'])
    q_s, k_s, v_s, x_rw, q_m, gates = jnp.split(u @ lp['w_in'], IN_SPLITS, axis=-1)
    q_s = q_s.reshape(B, T, SWA_KV_HEADS, SWA_GROUP, SWA_HEAD_DIM)
    k_s = k_s.reshape(B, T, SWA_KV_HEADS, SWA_HEAD_DIM)
    v_s = v_s.reshape(B, T, SWA_KV_HEADS, SWA_HEAD_DIM)
    sinks = lp['swa_sinks'].reshape(SWA_KV_HEADS, SWA_GROUP)
    if k_buf is None:
        o_swa = swa_prompt(q_s, k_s, v_s, sinks, table)
        w_buf = min(WINDOW, T)
        new_k, new_v = k_s[:, T - w_buf:], v_s[:, T - w_buf:]
    else:
        o_swa, new_k, new_v = swa_sample(q_s, k_s, v_s, k_buf, v_buf, sinks, table)
    o_rw, new_s, new_shift = rwkv7(x_rw, shift0, rwkv_s0, lp)
    o_mem = mem_attend(q_m, mem_k, mem_v)
    gt = jax.nn.sigmoid(gates.reshape(B, T, N_BRANCH, D_MODEL))
    merged = (gt[:, :, 0] * (o_swa @ lp['w_o_swa'])
              + gt[:, :, 1] * (o_rw @ lp['w_o_rwkv'])
              + gt[:, :, 2] * (o_mem @ lp['w_o_mem']))
    h = h + merged @ lp['w_out']
    h = h + 0.5 * swiglu(rmsnorm(h, lp['ffn2_norm']), lp['ffn2_wi'], lp['ffn2_wo'])
    return h, new_k, new_v, new_s, new_shift


def setup_inputs(seed: int = 0) -> dict:
    key = jax.random.key(seed)
    keys = iter(jax.random.split(key, 48))
    f32 = jnp.float32
    L = DEPTH
    w_buf = min(WINDOW, PAST_LEN)

    def nrm(shape, scale):
        return jax.random.normal(next(keys), shape, f32) * scale

    def gain(shape):
        return 1.0 + nrm(shape, 0.01)

    def unif(shape, lo, hi):
        return jax.random.uniform(next(keys), shape, f32, minval=lo, maxval=hi)

    return {
        'x_prompt': nrm((BATCH, SEQ, D_MODEL), 1.0),
        'mem_prompt': nrm((BATCH, N_MEM, D_MODEL), 1.0),
        'x_sample': nrm((DEC_BATCH, DEC_SEQ, D_MODEL), 1.0),
        'cache_swa_k': nrm((L, DEC_BATCH, w_buf, SWA_KV_HEADS, SWA_HEAD_DIM), 1.0),
        'cache_swa_v': nrm((L, DEC_BATCH, w_buf, SWA_KV_HEADS, SWA_HEAD_DIM), 1.0),
        'state_rwkv': nrm((L, DEC_BATCH, RWKV_HEADS, RWKV_HEAD_DIM, RWKV_HEAD_DIM), 0.3),
        'state_rwkv_shift': nrm((L, DEC_BATCH, 1, RWKV_IN), 1.0),
        'cache_mem_k': nrm((L, DEC_BATCH, N_MEM, MEM_HEADS, MEM_HEAD_DIM), 1.0),
        'cache_mem_v': nrm((L, DEC_BATCH, N_MEM, MEM_HEADS, MEM_HEAD_DIM), 1.0),
        'ffn1_norm': gain((L, D_MODEL)),
        'ffn1_wi': nrm((L, D_MODEL, 2 * D_FF), D_MODEL ** -0.5),
        'ffn1_wo': nrm((L, D_FF, D_MODEL), D_FF ** -0.5),
        'mix_norm': gain((L, D_MODEL)),
        'w_in': nrm((L, D_MODEL, IN_DIM), D_MODEL ** -0.5),
        'swa_sinks': nrm((L, SWA_HEADS), 0.5),
        'rel_bias_table': nrm((N_BUCKETS, SWA_HEADS), 0.1),
        'rwkv_mu': unif((L, RWKV_IN), 0.0, 1.0),
        'rwkv_w0': unif((L, RWKV_DIM), -6.0, 1.0),
        'rwkv_w_w2': nrm((L, LORA_W, RWKV_DIM), 0.1),
        'rwkv_a0': nrm((L, RWKV_DIM), 0.1),
        'rwkv_a_w2': nrm((L, LORA_A, RWKV_DIM), 0.1),
        'rwkv_g_w2': nrm((L, LORA_G, RWKV_DIM), LORA_G ** -0.5),
        'rwkv_k_k': 0.85 + nrm((L, RWKV_DIM), 0.01),
        'rwkv_k_a': gain((L, RWKV_DIM)),
        'rwkv_r_k': nrm((L, RWKV_DIM), 0.1),
        'rwkv_ln_w': gain((L, RWKV_DIM)),
        'rwkv_ln_b': nrm((L, RWKV_DIM), 0.01),
        'mem_norm': gain((L, D_MODEL)),
        'w_mem_kv': nrm((L, D_MODEL, 2 * MEM_DIM), D_MODEL ** -0.5),
        'w_o_swa': nrm((L, SWA_Q_DIM, D_MODEL), SWA_Q_DIM ** -0.5),
        'w_o_rwkv': nrm((L, RWKV_DIM, D_MODEL), RWKV_DIM ** -0.5),
        'w_o_mem': nrm((L, MEM_DIM, D_MODEL), MEM_DIM ** -0.5),
        'w_out': nrm((L, D_MODEL, D_MODEL), D_MODEL ** -0.5),
        'ffn2_norm': gain((L, D_MODEL)),
        'ffn2_wi': nrm((L, D_MODEL, 2 * D_FF), D_MODEL ** -0.5),
        'ffn2_wo': nrm((L, D_FF, D_MODEL), D_FF ** -0.5),
        'final_norm': gain((D_MODEL,)),
    }


def reference(x_prompt, mem_prompt, x_sample, cache_swa_k, cache_swa_v, state_rwkv, state_rwkv_shift,
              cache_mem_k, cache_mem_v, ffn1_norm, ffn1_wi, ffn1_wo, mix_norm, w_in, swa_sinks,
              rel_bias_table, rwkv_mu, rwkv_w0, rwkv_w_w2, rwkv_a0, rwkv_a_w2, rwkv_g_w2, rwkv_k_k,
              rwkv_k_a, rwkv_r_k, rwkv_ln_w, rwkv_ln_b, mem_norm, w_mem_kv, w_o_swa, w_o_rwkv, w_o_mem,
              w_out, ffn2_norm, ffn2_wi, ffn2_wo, final_norm):
    B = x_prompt.shape[0]
    hp, hs = x_prompt, x_sample
    p_k, p_v, p_s, p_sh, p_mk, p_mv = [], [], [], [], [], []
    s_k, s_v, s_s, s_sh = [], [], [], []
    for l in range(DEPTH):
        lp = {
            'ffn1_norm': ffn1_norm[l], 'ffn1_wi': ffn1_wi[l], 'ffn1_wo': ffn1_wo[l],
            'mix_norm': mix_norm[l], 'w_in': w_in[l], 'swa_sinks': swa_sinks[l],
            'rwkv_mu': rwkv_mu[l], 'rwkv_w0': rwkv_w0[l], 'rwkv_w_w2': rwkv_w_w2[l],
            'rwkv_a0': rwkv_a0[l], 'rwkv_a_w2': rwkv_a_w2[l], 'rwkv_g_w2': rwkv_g_w2[l],
            'rwkv_k_k': rwkv_k_k[l], 'rwkv_k_a': rwkv_k_a[l], 'rwkv_r_k': rwkv_r_k[l],
            'rwkv_ln_w': rwkv_ln_w[l], 'rwkv_ln_b': rwkv_ln_b[l],
            'w_o_swa': w_o_swa[l], 'w_o_rwkv': w_o_rwkv[l], 'w_o_mem': w_o_mem[l], 'w_out': w_out[l],
            'ffn2_norm': ffn2_norm[l], 'ffn2_wi': ffn2_wi[l], 'ffn2_wo': ffn2_wo[l],
        }
        mk, mv = jnp.split(rmsnorm(mem_prompt, mem_norm[l]) @ w_mem_kv[l], 2, axis=-1)
        mk = mk.reshape(B, N_MEM, MEM_HEADS, MEM_HEAD_DIM)
        mv = mv.reshape(B, N_MEM, MEM_HEADS, MEM_HEAD_DIM)
        s0 = jnp.zeros((B, RWKV_HEADS, RWKV_HEAD_DIM, RWKV_HEAD_DIM), jnp.float32)
        sh0 = jnp.zeros((B, 1, RWKV_IN), x_prompt.dtype)
        hp, kp, vp, sp, shp = layer(hp, mk, mv, None, None, s0, sh0, lp, rel_bias_table)
        hs, ks, vs, ss, shs = layer(hs, cache_mem_k[l], cache_mem_v[l], cache_swa_k[l], cache_swa_v[l],
                                    state_rwkv[l], state_rwkv_shift[l], lp, rel_bias_table)
        p_k.append(kp); p_v.append(vp); p_s.append(sp); p_sh.append(shp); p_mk.append(mk); p_mv.append(mv)
        s_k.append(ks); s_v.append(vs); s_s.append(ss); s_sh.append(shs)
    y_prompt = rmsnorm(hp, final_norm)
    y_sample = rmsnorm(hs, final_norm)
    return (y_prompt, y_sample,
            jnp.stack(p_k), jnp.stack(p_v), jnp.stack(p_s), jnp.stack(p_sh), jnp.stack(p_mk), jnp.stack(p_mv),
            jnp.stack(s_k), jnp.stack(s_v), jnp.stack(s_s), jnp.stack(s_sh))
```

```python
import functools
import math

import jax
import jax.numpy as jnp
import numpy as np
from jax import lax
from jax.experimental import pallas as pl
from jax.experimental.pallas import tpu as pltpu

F32 = jnp.float32
BF16 = jnp.bfloat16

D_MODEL = 2048
D_FF = 5632
SWA_HEADS = 16
SWA_KV_HEADS = 4
SWA_GROUP = SWA_HEADS // SWA_KV_HEADS
SWA_HEAD_DIM = 64
SWA_Q_DIM = SWA_HEADS * SWA_HEAD_DIM
SWA_KV_DIM = SWA_KV_HEADS * SWA_HEAD_DIM
WINDOW = 128
N_BUCKETS = 32
MAX_DISTANCE = 128
RWKV_HEADS = 8
RWKV_HEAD_DIM = 64
RWKV_DIM = RWKV_HEADS * RWKV_HEAD_DIM
LORA_W = 64
LORA_A = 64
LORA_G = 128
RWKV_IN = 3 * RWKV_DIM + LORA_W + LORA_A + LORA_G
N_MEM = 256
MEM_HEADS = 4
MEM_HEAD_DIM = 128
MEM_DIM = MEM_HEADS * MEM_HEAD_DIM
N_BRANCH = 3
PROJ_DIM = SWA_Q_DIM + 2 * SWA_KV_DIM + RWKV_IN + MEM_DIM
NORM_EPS = 1e-6
GN_EPS = 64e-5
NEG_INF = -1e30

LANES = 128
SUBLANES = 8
VMEM_LIMIT = 56 * 1024 * 1024


def _cparams(*sem):
    return pltpu.CompilerParams(dimension_semantics=sem, vmem_limit_bytes=VMEM_LIMIT)


def _rms(x, g):
    return x * lax.rsqrt(jnp.mean(x * x, axis=-1, keepdims=True) + NORM_EPS) * g


def _dot(a, b):
    return jnp.dot(a, b, preferred_element_type=F32)


def _dot_nt(a, b):
    return lax.dot_general(a, b, (((1,), (1,)), ((), ())), preferred_element_type=F32)


def _dot_hi(a, b):
    return jnp.dot(a, b, preferred_element_type=F32, precision=lax.Precision.HIGHEST)


def _ffn_kernel(x_ref, g_ref, wg_ref, wu_ref, wo_ref, gf_ref, o_ref, xn_ref, acc_ref, *, final_norm):
    j = pl.program_id(1)

    @pl.when(j == 0)
    def _():
        xn_ref[...] = _rms(x_ref[...], g_ref[...]).astype(BF16)
        acc_ref[...] = jnp.zeros_like(acc_ref)

    xn = xn_ref[...]
    gate = _dot(xn, wg_ref[...])
    up = _dot(xn, wu_ref[...])
    act = (gate * jax.nn.sigmoid(gate)) * up
    acc_ref[...] += _dot(act.astype(BF16), wo_ref[...])

    @pl.when(j == pl.num_programs(1) - 1)
    def _():
        h = x_ref[...] + 0.5 * acc_ref[...]
        if final_norm:
            h = _rms(h, gf_ref[...])
        o_ref[...] = h


def _ffn(x, g, wi, wo, gf, *, final_norm, tm=512, tf=512):
    m, d = x.shape
    dff = wo.shape[0]
    nf = dff // tf
    return pl.pallas_call(
        functools.partial(_ffn_kernel, final_norm=final_norm),
        out_shape=jax.ShapeDtypeStruct((m, d), F32),
        grid=(m // tm, nf),
        in_specs=[
            pl.BlockSpec((tm, d), lambda i, j: (i, 0)),
            pl.BlockSpec((1, d), lambda i, j: (0, 0)),
            pl.BlockSpec((d, tf), lambda i, j: (0, j)),
            pl.BlockSpec((d, tf), lambda i, j: (0, j + nf)),
            pl.BlockSpec((tf, d), lambda i, j: (j, 0)),
            pl.BlockSpec((1, d), lambda i, j: (0, 0)),
        ],
        out_specs=pl.BlockSpec((tm, d), lambda i, j: (i, 0)),
        scratch_shapes=[pltpu.VMEM((tm, d), BF16), pltpu.VMEM((tm, d), F32)],
        compiler_params=_cparams("parallel", "arbitrary"),
        name="ffn_final" if final_norm else "ffn",
    )(x, g, wi, wi, wo, gf)


def _inproj_kernel(h_ref, g_ref, w_ref, q_ref, kv_ref, xr_ref, qm_ref):
    u = _rms(h_ref[...], g_ref[...]).astype(BF16)
    c0, c1, c2 = SWA_Q_DIM, SWA_Q_DIM + 2 * SWA_KV_DIM, SWA_Q_DIM + 2 * SWA_KV_DIM + RWKV_IN
    q_ref[...] = _dot(u, w_ref[:, 0:c0]).astype(BF16)
    kv_ref[...] = _dot(u, w_ref[:, c0:c1])
    xr_ref[...] = _dot(u, w_ref[:, c1:c2])
    qm_ref[...] = _dot(u, w_ref[:, c2:PROJ_DIM]).astype(BF16)


def _inproj(h, g, w, *, tm=256):
    m, d = h.shape
    row = lambda i: (i, 0)
    return pl.pallas_call(
        _inproj_kernel,
        out_shape=(
            jax.ShapeDtypeStruct((m, SWA_Q_DIM), BF16),
            jax.ShapeDtypeStruct((m, 2 * SWA_KV_DIM), F32),
            jax.ShapeDtypeStruct((m, RWKV_IN), F32),
            jax.ShapeDtypeStruct((m, MEM_DIM), BF16),
        ),
        grid=(m // tm,),
        in_specs=[
            pl.BlockSpec((tm, d), row),
            pl.BlockSpec((1, d), lambda i: (0, 0)),
            pl.BlockSpec((d, PROJ_DIM), lambda i: (0, 0), pipeline_mode=pl.Buffered(1)),
        ],
        out_specs=(
            pl.BlockSpec((tm, SWA_Q_DIM), row),
            pl.BlockSpec((tm, 2 * SWA_KV_DIM), row),
            pl.BlockSpec((tm, RWKV_IN), row),
            pl.BlockSpec((tm, MEM_DIM), row),
        ),
        compiler_params=_cparams("parallel"),
        name="inproj",
    )(h, g, w)


def _norm_matmul_kernel(x_ref, g_ref, w_ref, o_ref):
    o_ref[...] = _dot(_rms(x_ref[...], g_ref[...]).astype(BF16), w_ref[...])


def _norm_matmul(x, g, w, *, tn=512):
    m, d = x.shape
    n = w.shape[1]
    return pl.pallas_call(
        _norm_matmul_kernel,
        out_shape=jax.ShapeDtypeStruct((m, n), F32),
        grid=(n // tn,),
        in_specs=[
            pl.BlockSpec((m, d), lambda j: (0, 0)),
            pl.BlockSpec((1, d), lambda j: (0, 0)),
            pl.BlockSpec((d, tn), lambda j: (0, j)),
        ],
        out_specs=pl.BlockSpec((m, tn), lambda j: (0, j)),
        compiler_params=_cparams("parallel"),
        name="norm_matmul",
    )(x, g, w)


def _sink_softmax(logits, sink):
    m = jnp.maximum(jnp.max(logits, axis=-1, keepdims=True), sink)
    p = jnp.exp(logits - m)
    denom = jnp.sum(p, axis=-1, keepdims=True) + jnp.exp(sink - m)
    return p * (1.0 / denom)


def _swa_prompt_kernel(sink_ref, q_ref, kvc_ref, kvp_ref, bias_ref, o_ref):
    i = pl.program_id(0)
    w = WINDOW
    k = jnp.concatenate([kvp_ref[:, 0:SWA_KV_DIM], kvc_ref[:, 0:SWA_KV_DIM]], axis=0).astype(BF16)
    v = jnp.concatenate([kvp_ref[:, SWA_KV_DIM:], kvc_ref[:, SWA_KV_DIM:]], axis=0).astype(BF16)
    row = lax.broadcasted_iota(jnp.int32, (w, 2 * w), 0)
    col = lax.broadcasted_iota(jnp.int32, (w, 2 * w), 1)
    dist = row + w - col
    valid = (dist >= 0) & (dist < w) & ((col >= w) | (i > 0))
    lane_head = lax.broadcasted_iota(jnp.int32, (w, SWA_KV_DIM), 1) // SWA_HEAD_DIM
    scale = SWA_HEAD_DIM ** -0.5
    for g in range(SWA_GROUP):
        qg = q_ref[:, g * SWA_KV_DIM:(g + 1) * SWA_KV_DIM].astype(F32)
        og = jnp.zeros((w, SWA_KV_DIM), F32)
        for kvh in range(SWA_KV_HEADS):
            head = kvh * SWA_GROUP + g
            sel = lane_head == kvh
            qm = jnp.where(sel, qg, 0.0).astype(BF16)
            logits = _dot_nt(qm, k)
            logits = jnp.where(valid, logits * scale + bias_ref[head], NEG_INF)
            p = _sink_softmax(logits, sink_ref[head])
            og = jnp.where(sel, _dot(p.astype(BF16), v), og)
        o_ref[:, g * SWA_KV_DIM:(g + 1) * SWA_KV_DIM] = og.astype(BF16)


def _swa_prompt(q, kv, bias, sinks):
    t = q.shape[0]
    w = WINDOW
    return pl.pallas_call(
        _swa_prompt_kernel,
        out_shape=jax.ShapeDtypeStruct((t, SWA_Q_DIM), BF16),
        grid=(t // w,),
        in_specs=[
            pl.BlockSpec(memory_space=pltpu.SMEM),
            pl.BlockSpec((w, SWA_Q_DIM), lambda i: (i, 0)),
            pl.BlockSpec((w, 2 * SWA_KV_DIM), lambda i: (i, 0)),
            pl.BlockSpec((w, 2 * SWA_KV_DIM), lambda i: (jnp.maximum(i - 1, 0), 0)),
            pl.BlockSpec((SWA_HEADS, w, 2 * w), lambda i: (0, 0, 0)),
        ],
        out_specs=pl.BlockSpec((w, SWA_Q_DIM), lambda i: (i, 0)),
        compiler_params=_cparams("parallel"),
        name="swa_prompt",
    )(sinks, q, kv, kv, bias)


def _swa_sample_kernel(q_ref, kvn_ref, kb_ref, vb_ref, bias_b_ref, bias_n_ref, sink_ref, o_ref, *, bb, t):
    gt = SWA_GROUP * t
    rows = SWA_KV_HEADS * gt
    w = kb_ref.shape[1]
    scale = SWA_HEAD_DIM ** -0.5
    lane_head = lax.broadcasted_iota(jnp.int32, (gt, SWA_KV_DIM), 1) // SWA_HEAD_DIM
    tok = lax.broadcasted_iota(jnp.int32, (rows, w), 0) % t
    keyj = lax.broadcasted_iota(jnp.int32, (rows, w), 1)
    valid_b = (tok + w - keyj) < WINDOW
    tok_n = lax.broadcasted_iota(jnp.int32, (rows, 1), 0) % t
    sink = sink_ref[...]
    for b in range(bb):
        qgt = q_ref[b].astype(F32)
        qall = jnp.concatenate(
            [jnp.where(lane_head == kvh, qgt, 0.0) for kvh in range(SWA_KV_HEADS)], axis=0)
        kb = kb_ref[b]
        vb = vb_ref[b]
        kvn = kvn_ref[b]
        lb = _dot_nt(qall.astype(BF16), kb.astype(BF16))
        lb = jnp.where(valid_b, lb * scale + bias_b_ref[...], NEG_INF)
        ln = []
        for j in range(t):
            lj = jnp.sum(qall * kvn[j:j + 1, 0:SWA_KV_DIM], axis=-1, keepdims=True)
            ln.append(jnp.where(tok_n >= j, lj * scale + bias_n_ref[:, j:j + 1], NEG_INF))
        m = jnp.maximum(jnp.max(lb, axis=-1, keepdims=True), sink)
        for lj in ln:
            m = jnp.maximum(m, lj)
        pb = jnp.exp(lb - m)
        pn = [jnp.exp(lj - m) for lj in ln]
        denom = jnp.sum(pb, axis=-1, keepdims=True) + jnp.exp(sink - m)
        for pj in pn:
            denom = denom + pj
        inv = 1.0 / denom
        oall = _dot((pb * inv).astype(BF16), vb.astype(BF16))
        for j in range(t):
            oall = oall + (pn[j] * inv) * kvn[j:j + 1, SWA_KV_DIM:]
        og = jnp.zeros((gt, SWA_KV_DIM), F32)
        for kvh in range(SWA_KV_HEADS):
            og = jnp.where(lane_head == kvh, oall[kvh * gt:(kvh + 1) * gt], og)
        o_ref[b] = og.astype(BF16)


def _swa_sample(q, kvn, kbuf, vbuf, bias_b, bias_n, sink_rows, *, bb=8):
    b, gt, _ = q.shape
    t = kvn.shape[1]
    w = kbuf.shape[1]
    rows = SWA_KV_HEADS * gt
    blk = lambda i: (i, 0, 0)
    const = lambda i: (0, 0)
    return pl.pallas_call(
        functools.partial(_swa_sample_kernel, bb=bb, t=t),
        out_shape=jax.ShapeDtypeStruct((b, gt, SWA_KV_DIM), BF16),
        grid=(b // bb,),
        in_specs=[
            pl.BlockSpec((bb, gt, SWA_KV_DIM), blk),
            pl.BlockSpec((bb, t, 2 * SWA_KV_DIM), blk),
            pl.BlockSpec((bb, w, SWA_KV_DIM), blk),
            pl.BlockSpec((bb, w, SWA_KV_DIM), blk),
            pl.BlockSpec((rows, w), const),
            pl.BlockSpec((rows, t), const),
            pl.BlockSpec((rows, 1), const),
        ],
        out_specs=pl.BlockSpec((bb, gt, SWA_KV_DIM), blk),
        compiler_params=_cparams("parallel"),
        name="swa_sample",
    )(q, kvn, kbuf, vbuf, bias_b, bias_n, sink_rows)


def _softmax_rows(x):
    m = jnp.max(x, axis=-1, keepdims=True)
    p = jnp.exp(x - m)
    return p * (1.0 / jnp.sum(p, axis=-1, keepdims=True))


def _mem_heads(q, mk, mv):
    scale = MEM_HEAD_DIM ** -0.5
    outs = []
    for h in range(MEM_HEADS):
        sl = slice(h * MEM_HEAD_DIM, (h + 1) * MEM_HEAD_DIM)
        p = _softmax_rows(_dot_nt(q[:, sl], mk[:, sl]) * scale)
        outs.append(_dot(p.astype(BF16), mv[:, sl]))
    return jnp.concatenate(outs, axis=-1)


def _mem_prompt_kernel(q_ref, mk_ref, mv_ref, o_ref):
    o_ref[...] = _mem_heads(q_ref[...], mk_ref[...].astype(BF16), mv_ref[...].astype(BF16)).astype(BF16)


def _mem_prompt(q, mkv, *, tm=512):
    m = q.shape[0]
    return pl.pallas_call(
        _mem_prompt_kernel,
        out_shape=jax.ShapeDtypeStruct((m, MEM_DIM), BF16),
        grid=(m // tm,),
        in_specs=[
            pl.BlockSpec((tm, MEM_DIM), lambda i: (i, 0)),
            pl.BlockSpec((N_MEM, MEM_DIM), lambda i: (0, 0)),
            pl.BlockSpec((N_MEM, MEM_DIM), lambda i: (0, 1)),
        ],
        out_specs=pl.BlockSpec((tm, MEM_DIM), lambda i: (i, 0)),
        compiler_params=_cparams("parallel"),
        name="mem_prompt",
    )(q, mkv, mkv)


def _mem_sample_kernel(q_ref, mk_ref, mv_ref, o_ref, *, bb):
    for b in range(bb):
        o_ref[b] = _mem_heads(q_ref[b], mk_ref[b].astype(BF16), mv_ref[b].astype(BF16)).astype(BF16)


def _mem_sample(q, mk, mv, *, bb=8):
    b, t, _ = q.shape
    blk = lambda i: (i, 0, 0)
    return pl.pallas_call(
        functools.partial(_mem_sample_kernel, bb=bb),
        out_shape=jax.ShapeDtypeStruct((b, t, MEM_DIM), BF16),
        grid=(b // bb,),
        in_specs=[
            pl.BlockSpec((bb, t, MEM_DIM), blk),
            pl.BlockSpec((bb, N_MEM, MEM_DIM), blk),
            pl.BlockSpec((bb, N_MEM, MEM_DIM), blk),
        ],
        out_specs=pl.BlockSpec((bb, t, MEM_DIM), blk),
        compiler_params=_cparams("parallel"),
        name="mem_sample",
    )(q, mk, mv)


def _head_sum(x, ones_ref):
    hi = x.astype(BF16)
    lo = (x - hi.astype(F32)).astype(BF16)
    return _dot(hi, ones_ref[...]) + _dot(lo, ones_ref[...])


def _rwkv_prep_kernel(x_ref, pv_ref, mu_ref, w0_ref, ww2_ref, a0_ref, aw2_ref, gw2_ref, kk_ref, ka_ref,
                      rk_ref, ones_ref, r_o, w_o, k_o, v_o, kk_o, kka_o, g_o, bonus_o):
    x = x_ref[...]
    xs = x + mu_ref[...] * (pv_ref[...] - x)
    d = RWKV_DIM
    r = xs[:, 0:d]
    k = xs[:, d:2 * d]
    v = xs[:, 2 * d:3 * d]
    lw = xs[:, 3 * d:3 * d + LORA_W]
    la = xs[:, 3 * d + LORA_W:3 * d + LORA_W + LORA_A]
    lg = xs[:, 3 * d + LORA_W + LORA_A:]
    wpre = w0_ref[...] + _dot_hi(jnp.tanh(lw), ww2_ref[...])
    w = -jax.nn.softplus(-wpre) - 0.5
    decay = jnp.exp(-jnp.exp(w))
    a = jax.nn.sigmoid(a0_ref[...] + _dot_hi(la, aw2_ref[...]))
    g = _dot_hi(jax.nn.sigmoid(lg), gw2_ref[...])
    kk = k * kk_ref[...]
    kk = kk / jnp.maximum(jnp.sqrt(_head_sum(kk * kk, ones_ref)), 1e-12)
    kh = k * (1.0 + (a - 1.0) * ka_ref[...])
    r_o[...] = r
    w_o[...] = decay
    k_o[...] = kh
    v_o[...] = v
    kk_o[...] = kk
    kka_o[...] = kk * a
    g_o[...] = g
    bonus_o[...] = _head_sum(r * kh * rk_ref[...], ones_ref) * v


def _rwkv_prep(x, prev, p, ones, *, tm=256):
    m = x.shape[0]
    row = lambda i: (i, 0)
    const = lambda i: (0, 0)
    vec = lambda n: pl.BlockSpec((1, n), const)
    out = jax.ShapeDtypeStruct((m, RWKV_DIM), F32)
    return pl.pallas_call(
        _rwkv_prep_kernel,
        out_shape=(out,) * 8,
        grid=(m // tm,),
        in_specs=[
            pl.BlockSpec((tm, RWKV_IN), row),
            pl.BlockSpec((tm, RWKV_IN), row),
            vec(RWKV_IN), vec(RWKV_DIM),
            pl.BlockSpec((LORA_W, RWKV_DIM), const),
            vec(RWKV_DIM),
            pl.BlockSpec((LORA_A, RWKV_DIM), const),
            pl.BlockSpec((LORA_G, RWKV_DIM), const),
            vec(RWKV_DIM), vec(RWKV_DIM), vec(RWKV_DIM),
            pl.BlockSpec((RWKV_DIM, RWKV_DIM), const),
        ],
        out_specs=(pl.BlockSpec((tm, RWKV_DIM), row),) * 8,
        compiler_params=_cparams("parallel"),
        name="rwkv_prep",
    )(x, prev, p["mu"], p["w0"], p["w_w2"], p["a0"], p["a_w2"], p["g_w2"], p["k_k"], p["k_a"], p["r_k"], ones)


def _half_sums(x, low):
    lo = jnp.sum(jnp.where(low, x, 0.0), axis=-1, keepdims=True)
    hi = jnp.sum(jnp.where(low, 0.0, x), axis=-1, keepdims=True)
    return jnp.where(low, lo, hi)


def _rwkv_scan_kernel(r_ref, w_ref, k_ref, v_ref, kk_ref, kka_ref, s0_ref, y_ref, so_ref, s_scr, *, bb, tt):
    jt = pl.program_id(1)
    group = min(tt, SUBLANES)
    n = RWKV_HEAD_DIM
    pairs = RWKV_HEADS // 2
    low = lax.broadcasted_iota(jnp.int32, (n, LANES), 1) < n
    diag = (lax.broadcasted_iota(jnp.int32, (n, LANES), 1) % n) == lax.broadcasted_iota(jnp.int32, (n, LANES), 0)

    @pl.when(jt == 0)
    def _():
        for b in range(bb):
            for h in range(RWKV_HEADS):
                s_scr[b, h // 2, :, (h % 2) * n:(h % 2 + 1) * n] = s0_ref[b, h]

    def batch_body(b, carry):
        def group_body(t0, states):
            toks = pl.ds(pl.multiple_of(t0 * group, group), group)
            new_states = []
            for p in range(pairs):
                sl = pl.ds(p * LANES, LANES)
                r, w, k, v = r_ref[b, toks, sl], w_ref[b, toks, sl], k_ref[b, toks, sl], v_ref[b, toks, sl]
                kk, kka = kk_ref[b, toks, sl], kka_ref[b, toks, sl]
                s = states[p]
                ys = []
                for j in range(group):
                    one = slice(j, j + 1)
                    sa = _half_sums(s * (-kk[one]), low)
                    vcol = _half_sums(jnp.where(diag, v[one], 0.0), low)
                    s = s * w[one] + sa * kka[one] + vcol * k[one]
                    ycol = _half_sums(s * r[one], low)
                    ys.append(jnp.sum(jnp.where(diag, ycol, 0.0), axis=0, keepdims=True))
                y_ref[b, toks, sl] = jnp.concatenate(ys, axis=0)
                new_states.append(s)
            return tuple(new_states)

        states = tuple(s_scr[b, p] for p in range(pairs))
        states = lax.fori_loop(0, tt // group, group_body, states)
        for p in range(pairs):
            s_scr[b, p] = states[p]
        return carry

    lax.fori_loop(0, bb, batch_body, 0)

    @pl.when(jt == pl.num_programs(1) - 1)
    def _():
        for b in range(bb):
            for h in range(RWKV_HEADS):
                so_ref[b, h] = s_scr[b, h // 2, :, (h % 2) * n:(h % 2 + 1) * n]


def _rwkv_scan(r, w, k, v, kk, kka, s0, *, bb, tt):
    b, t, d = r.shape
    n = RWKV_HEAD_DIM
    tok = pl.BlockSpec((bb, tt, d), lambda i, j: (i, j, 0))
    st = pl.BlockSpec((bb, RWKV_HEADS, n, n), lambda i, j: (i, 0, 0, 0))
    return pl.pallas_call(
        functools.partial(_rwkv_scan_kernel, bb=bb, tt=tt),
        out_shape=(jax.ShapeDtypeStruct((b, t, d), F32), jax.ShapeDtypeStruct((b, RWKV_HEADS, n, n), F32)),
        grid=(b // bb, t // tt),
        in_specs=[tok] * 6 + [st],
        out_specs=(tok, st),
        scratch_shapes=[pltpu.VMEM((bb, RWKV_HEADS // 2, n, LANES), F32)],
        compiler_params=_cparams("parallel", "arbitrary"),
        name="rwkv_scan",
    )(r, w, k, v, kk, kka, s0)


def _rwkv_post_kernel(y_ref, bonus_ref, g_ref, lnw_ref, lnb_ref, ones_ref, o_ref):
    y = y_ref[...]
    inv_n = 1.0 / RWKV_HEAD_DIM
    mu = _head_sum(y, ones_ref) * inv_n
    dlt = y - mu
    var = _head_sum(dlt * dlt, ones_ref) * inv_n
    yn = dlt * lax.rsqrt(var + GN_EPS) * lnw_ref[...] + lnb_ref[...]
    o_ref[...] = ((yn + bonus_ref[...]) * g_ref[...]).astype(BF16)


def _rwkv_post(y, bonus, g, lnw, lnb, ones, *, tm=256):
    m = y.shape[0]
    row = lambda i: (i, 0)
    const = lambda i: (0, 0)
    tile = pl.BlockSpec((tm, RWKV_DIM), row)
    return pl.pallas_call(
        _rwkv_post_kernel,
        out_shape=jax.ShapeDtypeStruct((m, RWKV_DIM), BF16),
        grid=(m // tm,),
        in_specs=[tile, tile, tile, pl.BlockSpec((1, RWKV_DIM), const), pl.BlockSpec((1, RWKV_DIM), const),
                  pl.BlockSpec((RWKV_DIM, RWKV_DIM), const)],
        out_specs=tile,
        compiler_params=_cparams("parallel"),
        name="rwkv_post",
    )(y, bonus, g, lnw, lnb, ones)


def _merge_kernel(h_ref, g_ref, os_ref, or_ref, om_ref, wg0_ref, wg1_ref, wg2_ref, wos_ref, wor_ref, wom_ref,
                  wout_ref, o_ref, u_ref, acc_ref):
    j = pl.program_id(1)

    @pl.when(j == 0)
    def _():
        u_ref[...] = _rms(h_ref[...], g_ref[...]).astype(BF16)
        acc_ref[...] = jnp.zeros_like(acc_ref)

    u = u_ref[...]
    merged = jax.nn.sigmoid(_dot(u, wg0_ref[...])) * _dot(os_ref[...], wos_ref[...])
    merged += jax.nn.sigmoid(_dot(u, wg1_ref[...])) * _dot(or_ref[...], wor_ref[...])
    merged += jax.nn.sigmoid(_dot(u, wg2_ref[...])) * _dot(om_ref[...], wom_ref[...])
    acc_ref[...] += _dot(merged.astype(BF16), wout_ref[...])

    @pl.when(j == pl.num_programs(1) - 1)
    def _():
        o_ref[...] = h_ref[...] + acc_ref[...]


def _merge(h, g, o_swa, o_rw, o_mem, w_gate, wo_swa, wo_rw, wo_mem, w_out, *, tm=512, tn=256):
    m, d = h.shape
    nt = d // tn
    row = lambda i, j: (i, 0)
    col = lambda i, j: (0, j)
    return pl.pallas_call(
        _merge_kernel,
        out_shape=jax.ShapeDtypeStruct((m, d), F32),
        grid=(m // tm, nt),
        in_specs=[
            pl.BlockSpec((tm, d), row),
            pl.BlockSpec((1, d), lambda i, j: (0, 0)),
            pl.BlockSpec((tm, SWA_Q_DIM), row),
            pl.BlockSpec((tm, RWKV_DIM), row),
            pl.BlockSpec((tm, MEM_DIM), row),
            pl.BlockSpec((d, tn), lambda i, j: (0, j)),
            pl.BlockSpec((d, tn), lambda i, j: (0, j + nt)),
            pl.BlockSpec((d, tn), lambda i, j: (0, j + 2 * nt)),
            pl.BlockSpec((SWA_Q_DIM, tn), col),
            pl.BlockSpec((RWKV_DIM, tn), col),
            pl.BlockSpec((MEM_DIM, tn), col),
            pl.BlockSpec((tn, d), lambda i, j: (j, 0)),
        ],
        out_specs=pl.BlockSpec((tm, d), row),
        scratch_shapes=[pltpu.VMEM((tm, d), BF16), pltpu.VMEM((tm, d), F32)],
        compiler_params=_cparams("parallel", "arbitrary"),
        name="merge",
    )(h, g, o_swa, o_rw, o_mem, w_gate, w_gate, w_gate, wo_swa, wo_rw, wo_mem, w_out)


def _t5_bucket(dist):
    max_exact = N_BUCKETS // 2
    d = np.maximum(dist, 0)
    log_ratio = (np.log(np.maximum(d, 1).astype(np.float32) / np.float32(max_exact))
                 / np.float32(math.log(MAX_DISTANCE / max_exact)))
    large = np.minimum(max_exact + (log_ratio * (N_BUCKETS - max_exact)).astype(np.int32), N_BUCKETS - 1)
    return np.where(d < max_exact, d, large).astype(np.int32)


def _q_perm():
    idx = np.arange(SWA_Q_DIM).reshape(SWA_KV_HEADS, SWA_GROUP, SWA_HEAD_DIM)
    return idx.transpose(1, 0, 2).reshape(-1)


def _rwkv_branch(xr, shift0, s0, p, ones, *, bb, tt):
    b, t, _ = xr.shape
    prev = jnp.concatenate([shift0, xr[:, :-1]], axis=1)
    flat = lambda z: z.reshape(b * t, z.shape[-1])
    r, w, k, v, kk, kka, g, bonus = _rwkv_prep(flat(xr), flat(prev), p, ones)
    seq = lambda z: z.reshape(b, t, RWKV_DIM)
    y, s_new = _rwkv_scan(seq(r), seq(w), seq(k), seq(v), seq(kk), seq(kka), s0, bb=bb, tt=tt)
    o = _rwkv_post(flat(y), bonus, g, p["ln_w"], p["ln_b"], ones)
    return o, s_new


def kernel(x_prompt, mem_prompt, x_sample, cache_swa_k, cache_swa_v, state_rwkv, state_rwkv_shift, cache_mem_k, cache_mem_v, ffn1_norm, ffn1_wi, ffn1_wo, mix_norm, w_in, swa_sinks, rel_bias_table, rwkv_mu, rwkv_w0, rwkv_w_w2, rwkv_a0, rwkv_a_w2, rwkv_g_w2, rwkv_k_k, rwkv_k_a, rwkv_r_k, rwkv_ln_w, rwkv_ln_b, mem_norm, w_mem_kv, w_o_swa, w_o_rwkv, w_o_mem, w_out, ffn2_norm, ffn2_wi, ffn2_wo, final_norm):
    assert ffn1_wi.shape[0] == 1, "single-layer trunk"
    bp, tp, d = x_prompt.shape
    bs, ts, _ = x_sample.shape
    assert bp == 1
    row = lambda z: z.reshape(1, -1).astype(F32)

    qperm = _q_perm()
    w_in0 = w_in[0]
    w_proj = jnp.concatenate([w_in0[:, :SWA_Q_DIM][:, qperm], w_in0[:, SWA_Q_DIM:PROJ_DIM]], axis=1).astype(BF16)
    w_gate = w_in0[:, PROJ_DIM:].astype(BF16)
    wo_swa = w_o_swa[0][qperm, :].astype(BF16)
    wo_rw = w_o_rwkv[0].astype(BF16)
    wo_mem = w_o_mem[0].astype(BF16)
    w_out_b = w_out[0].astype(BF16)
    wi1, wo1 = ffn1_wi[0].astype(BF16), ffn1_wo[0].astype(BF16)
    wi2, wo2 = ffn2_wi[0].astype(BF16), ffn2_wo[0].astype(BF16)
    g1, gm, g2, gf = row(ffn1_norm[0]), row(mix_norm[0]), row(ffn2_norm[0]), row(final_norm)
    rp = {
        "mu": row(rwkv_mu[0]), "w0": row(rwkv_w0[0]), "w_w2": rwkv_w_w2[0], "a0": row(rwkv_a0[0]),
        "a_w2": rwkv_a_w2[0], "g_w2": rwkv_g_w2[0], "k_k": row(rwkv_k_k[0]), "k_a": row(rwkv_k_a[0]),
        "r_k": row(rwkv_r_k[0]), "ln_w": row(rwkv_ln_w[0]), "ln_b": row(rwkv_ln_b[0]),
    }
    seg = np.arange(RWKV_DIM) // RWKV_HEAD_DIM
    ones = jnp.asarray(seg[:, None] == seg[None, :], dtype=BF16)
    sinks = swa_sinks[0].astype(F32)
    table = rel_bias_table.astype(F32)

    xp = x_prompt.reshape(tp, d)
    xs = x_sample.reshape(bs * ts, d)
    hp = _ffn(xp, g1, wi1, wo1, gf, final_norm=False)
    hs = _ffn(xs, g1, wi1, wo1, gf, final_norm=False)
    qp, kvp, xrp, qmp = _inproj(hp, gm, w_proj)
    qs, kvs, xrs, qms = _inproj(hs, gm, w_proj)

    w = WINDOW
    dist_p = np.arange(w)[:, None] + w - np.arange(2 * w)[None, :]
    bias_p = jnp.moveaxis(table[_t5_bucket(dist_p)], -1, 0)
    o_swa_p = _swa_prompt(qp, kvp, bias_p, sinks)

    wbuf = cache_swa_k.shape[2]
    dist_s = np.arange(ts)[:, None] + wbuf - np.arange(wbuf + ts)[None, :]
    bias_s = jnp.moveaxis(table[_t5_bucket(dist_s)], -1, 0)
    bias_s = bias_s.reshape(SWA_HEADS * ts, wbuf + ts)
    sink_rows = jnp.repeat(sinks, ts).reshape(SWA_HEADS * ts, 1)
    qs_gt = qs.reshape(bs, ts, SWA_GROUP, SWA_KV_DIM).transpose(0, 2, 1, 3).reshape(bs, SWA_GROUP * ts, SWA_KV_DIM)
    kbuf = cache_swa_k[0].reshape(bs, wbuf, SWA_KV_DIM)
    vbuf = cache_swa_v[0].reshape(bs, wbuf, SWA_KV_DIM)
    o_swa_s = _swa_sample(qs_gt, kvs.reshape(bs, ts, 2 * SWA_KV_DIM), kbuf, vbuf,
                          bias_s[:, :wbuf], bias_s[:, wbuf:], sink_rows)
    o_swa_s = o_swa_s.reshape(bs, SWA_GROUP, ts, SWA_KV_DIM).transpose(0, 2, 1, 3).reshape(bs * ts, SWA_Q_DIM)

    zero_shift = jnp.zeros((bp, 1, RWKV_IN), F32)
    zero_state = jnp.zeros((bp, RWKV_HEADS, RWKV_HEAD_DIM, RWKV_HEAD_DIM), F32)
    o_rw_p, state_p = _rwkv_branch(xrp.reshape(bp, tp, RWKV_IN), zero_shift, zero_state, rp, ones,
                                   bb=1, tt=256)
    o_rw_s, state_s = _rwkv_branch(xrs.reshape(bs, ts, RWKV_IN), state_rwkv_shift[0], state_rwkv[0], rp, ones,
                                   bb=8, tt=ts)

    mkv = _norm_matmul(mem_prompt.reshape(N_MEM, d), row(mem_norm[0]), w_mem_kv[0].astype(BF16))
    o_mem_p = _mem_prompt(qmp, mkv)
    o_mem_s = _mem_sample(qms.reshape(bs, ts, MEM_DIM), cache_mem_k[0].reshape(bs, N_MEM, MEM_DIM),
                          cache_mem_v[0].reshape(bs, N_MEM, MEM_DIM)).reshape(bs * ts, MEM_DIM)

    hp = _merge(hp, gm, o_swa_p, o_rw_p, o_mem_p, w_gate, wo_swa, wo_rw, wo_mem, w_out_b)
    hs = _merge(hs, gm, o_swa_s, o_rw_s, o_mem_s, w_gate, wo_swa, wo_rw, wo_mem, w_out_b)
    y_prompt = _ffn(hp, g2, wi2, wo2, gf, final_norm=True).reshape(bp, tp, d)
    y_sample = _ffn(hs, g2, wi2, wo2, gf, final_norm=True).reshape(bs, ts, d)

    wp = min(w, tp)
    p_k = kvp[tp - wp:, :SWA_KV_DIM].reshape(1, bp, wp, SWA_KV_HEADS, SWA_HEAD_DIM)
    p_v = kvp[tp - wp:, SWA_KV_DIM:].reshape(1, bp, wp, SWA_KV_HEADS, SWA_HEAD_DIM)
    p_mk = mkv[:, :MEM_DIM].reshape(1, bp, N_MEM, MEM_HEADS, MEM_HEAD_DIM)
    p_mv = mkv[:, MEM_DIM:].reshape(1, bp, N_MEM, MEM_HEADS, MEM_HEAD_DIM)
    kvs3 = kvs.reshape(bs, ts, 2 * SWA_KV_DIM)
    s_k = jnp.concatenate([kbuf[:, ts:], kvs3[:, :, :SWA_KV_DIM]], axis=1).reshape(1, bs, wbuf, SWA_KV_HEADS, SWA_HEAD_DIM)
    s_v = jnp.concatenate([vbuf[:, ts:], kvs3[:, :, SWA_KV_DIM:]], axis=1).reshape(1, bs, wbuf, SWA_KV_HEADS, SWA_HEAD_DIM)
    return (y_prompt, y_sample,
            p_k, p_v, state_p[None], xrp[tp - 1:].reshape(1, bp, 1, RWKV_IN), p_mk, p_mv,
            s_k, s_v, state_s[None], xrs.reshape(bs, ts, RWKV_IN)[:, ts - 1:][None])
```

```python
import functools
import math

import jax
import jax.numpy as jnp
import numpy as np
from jax import lax
from jax.experimental import pallas as pl
from jax.experimental.pallas import tpu as pltpu

F32 = jnp.float32
BF16 = jnp.bfloat16

D_MODEL = 2048
D_FF = 5632
SWA_HEADS = 16
SWA_KV_HEADS = 4
SWA_GROUP = SWA_HEADS // SWA_KV_HEADS
SWA_HEAD_DIM = 64
SWA_Q_DIM = SWA_HEADS * SWA_HEAD_DIM
SWA_KV_DIM = SWA_KV_HEADS * SWA_HEAD_DIM
WINDOW = 128
N_BUCKETS = 32
MAX_DISTANCE = 128
RWKV_HEADS = 8
RWKV_HEAD_DIM = 64
RWKV_DIM = RWKV_HEADS * RWKV_HEAD_DIM
LORA_W = 64
LORA_A = 64
LORA_G = 128
RWKV_IN = 3 * RWKV_DIM + LORA_W + LORA_A + LORA_G
N_MEM = 256
MEM_HEADS = 4
MEM_HEAD_DIM = 128
MEM_DIM = MEM_HEADS * MEM_HEAD_DIM
N_BRANCH = 3
PROJ_DIM = SWA_Q_DIM + 2 * SWA_KV_DIM + RWKV_IN + MEM_DIM
NORM_EPS = 1e-6
GN_EPS = 64e-5
NEG_INF = -1e30

LANES = 128
SUBLANES = 8
VMEM_LIMIT = 56 * 1024 * 1024


def _cparams(*sem):
    return pltpu.CompilerParams(dimension_semantics=sem, vmem_limit_bytes=VMEM_LIMIT)


def _rms(x, g):
    return x * lax.rsqrt(jnp.mean(x * x, axis=-1, keepdims=True) + NORM_EPS) * g


def _dot(a, b):
    return jnp.dot(a, b, preferred_element_type=F32)


def _dot_nt(a, b):
    return lax.dot_general(a, b, (((1,), (1,)), ((), ())), preferred_element_type=F32)


def _dot_hi(a, b):
    return jnp.dot(a, b, preferred_element_type=F32, precision=lax.Precision.HIGHEST)


def _ffn_kernel(x_ref, g_ref, wg_ref, wu_ref, wo_ref, gf_ref, o_ref, xn_ref, acc_ref, *, final_norm):
    j = pl.program_id(1)

    @pl.when(j == 0)
    def _():
        xn_ref[...] = _rms(x_ref[...], g_ref[...]).astype(BF16)
        acc_ref[...] = jnp.zeros_like(acc_ref)

    xn = xn_ref[...]
    gate = _dot(xn, wg_ref[...])
    up = _dot(xn, wu_ref[...])
    act = (gate * jax.nn.sigmoid(gate)) * up
    acc_ref[...] += _dot(act.astype(BF16), wo_ref[...])

    @pl.when(j == pl.num_programs(1) - 1)
    def _():
        h = x_ref[...] + 0.5 * acc_ref[...]
        if final_norm:
            h = _rms(h, gf_ref[...])
        o_ref[...] = h


def _ffn(x, g, wi, wo, gf, *, final_norm, tm=512, tf=512):
    m, d = x.shape
    dff = wo.shape[0]
    nf = dff // tf
    return pl.pallas_call(
        functools.partial(_ffn_kernel, final_norm=final_norm),
        out_shape=jax.ShapeDtypeStruct((m, d), F32),
        grid=(m // tm, nf),
        in_specs=[
            pl.BlockSpec((tm, d), lambda i, j: (i, 0)),
            pl.BlockSpec((1, d), lambda i, j: (0, 0)),
            pl.BlockSpec((d, tf), lambda i, j: (0, j)),
            pl.BlockSpec((d, tf), lambda i, j: (0, j + nf)),
            pl.BlockSpec((tf, d), lambda i, j: (j, 0)),
            pl.BlockSpec((1, d), lambda i, j: (0, 0)),
        ],
        out_specs=pl.BlockSpec((tm, d), lambda i, j: (i, 0)),
        scratch_shapes=[pltpu.VMEM((tm, d), BF16), pltpu.VMEM((tm, d), F32)],
        compiler_params=_cparams("parallel", "arbitrary"),
        name="ffn_final" if final_norm else "ffn",
    )(x, g, wi, wi, wo, gf)


def _inproj_kernel(h_ref, g_ref, w_ref, q_ref, kv_ref, xr_ref, qm_ref):
    u = _rms(h_ref[...], g_ref[...]).astype(BF16)
    c0, c1, c2 = SWA_Q_DIM, SWA_Q_DIM + 2 * SWA_KV_DIM, SWA_Q_DIM + 2 * SWA_KV_DIM + RWKV_IN
    q_ref[...] = _dot(u, w_ref[:, 0:c0]).astype(BF16)
    kv_ref[...] = _dot(u, w_ref[:, c0:c1])
    xr_ref[...] = _dot(u, w_ref[:, c1:c2])
    qm_ref[...] = _dot(u, w_ref[:, c2:PROJ_DIM]).astype(BF16)


def _inproj(h, g, w, *, tm=256):
    m, d = h.shape
    row = lambda i: (i, 0)
    return pl.pallas_call(
        _inproj_kernel,
        out_shape=(
            jax.ShapeDtypeStruct((m, SWA_Q_DIM), BF16),
            jax.ShapeDtypeStruct((m, 2 * SWA_KV_DIM), F32),
            jax.ShapeDtypeStruct((m, RWKV_IN), F32),
            jax.ShapeDtypeStruct((m, MEM_DIM), BF16),
        ),
        grid=(m // tm,),
        in_specs=[
            pl.BlockSpec((tm, d), row),
            pl.BlockSpec((1, d), lambda i: (0, 0)),
            pl.BlockSpec((d, PROJ_DIM), lambda i: (0, 0), pipeline_mode=pl.Buffered(1)),
        ],
        out_specs=(
            pl.BlockSpec((tm, SWA_Q_DIM), row),
            pl.BlockSpec((tm, 2 * SWA_KV_DIM), row),
            pl.BlockSpec((tm, RWKV_IN), row),
            pl.BlockSpec((tm, MEM_DIM), row),
        ),
        compiler_params=_cparams("parallel"),
        name="inproj",
    )(h, g, w)


def _norm_matmul_kernel(x_ref, g_ref, w_ref, o_ref):
    o_ref[...] = _dot(_rms(x_ref[...], g_ref[...]).astype(BF16), w_ref[...])


def _norm_matmul(x, g, w, *, tn=512):
    m, d = x.shape
    n = w.shape[1]
    return pl.pallas_call(
        _norm_matmul_kernel,
        out_shape=jax.ShapeDtypeStruct((m, n), F32),
        grid=(n // tn,),
        in_specs=[
            pl.BlockSpec((m, d), lambda j: (0, 0)),
            pl.BlockSpec((1, d), lambda j: (0, 0)),
            pl.BlockSpec((d, tn), lambda j: (0, j)),
        ],
        out_specs=pl.BlockSpec((m, tn), lambda j: (0, j)),
        compiler_params=_cparams("parallel"),
        name="norm_matmul",
    )(x, g, w)


def _sink_softmax(logits, sink):
    m = jnp.maximum(jnp.max(logits, axis=-1, keepdims=True), sink)
    p = jnp.exp(logits - m)
    denom = jnp.sum(p, axis=-1, keepdims=True) + jnp.exp(sink - m)
    return p * (1.0 / denom)


def _swa_prompt_kernel(sink_ref, q_ref, kvc_ref, kvp_ref, bias_ref, o_ref):
    i = pl.program_id(0)
    w = WINDOW
    k = jnp.concatenate([kvp_ref[:, 0:SWA_KV_DIM], kvc_ref[:, 0:SWA_KV_DIM]], axis=0).astype(BF16)
    v = jnp.concatenate([kvp_ref[:, SWA_KV_DIM:], kvc_ref[:, SWA_KV_DIM:]], axis=0).astype(BF16)
    row = lax.broadcasted_iota(jnp.int32, (w, 2 * w), 0)
    col = lax.broadcasted_iota(jnp.int32, (w, 2 * w), 1)
    dist = row + w - col
    valid = (dist >= 0) & (dist < w) & ((col >= w) | (i > 0))
    lane_head = lax.broadcasted_iota(jnp.int32, (w, SWA_KV_DIM), 1) // SWA_HEAD_DIM
    scale = SWA_HEAD_DIM ** -0.5
    for g in range(SWA_GROUP):
        qg = q_ref[:, g * SWA_KV_DIM:(g + 1) * SWA_KV_DIM].astype(F32)
        og = jnp.zeros((w, SWA_KV_DIM), F32)
        for kvh in range(SWA_KV_HEADS):
            head = kvh * SWA_GROUP + g
            sel = lane_head == kvh
            qm = jnp.where(sel, qg, 0.0).astype(BF16)
            logits = _dot_nt(qm, k)
            logits = jnp.where(valid, logits * scale + bias_ref[head], NEG_INF)
            p = _sink_softmax(logits, sink_ref[head])
            og = jnp.where(sel, _dot(p.astype(BF16), v), og)
        o_ref[:, g * SWA_KV_DIM:(g + 1) * SWA_KV_DIM] = og.astype(BF16)


def _swa_prompt(q, kv, bias, sinks):
    t = q.shape[0]
    w = WINDOW
    return pl.pallas_call(
        _swa_prompt_kernel,
        out_shape=jax.ShapeDtypeStruct((t, SWA_Q_DIM), BF16),
        grid=(t // w,),
        in_specs=[
            pl.BlockSpec(memory_space=pltpu.SMEM),
            pl.BlockSpec((w, SWA_Q_DIM), lambda i: (i, 0)),
            pl.BlockSpec((w, 2 * SWA_KV_DIM), lambda i: (i, 0)),
            pl.BlockSpec((w, 2 * SWA_KV_DIM), lambda i: (jnp.maximum(i - 1, 0), 0)),
            pl.BlockSpec((SWA_HEADS, w, 2 * w), lambda i: (0, 0, 0)),
        ],
        out_specs=pl.BlockSpec((w, SWA_Q_DIM), lambda i: (i, 0)),
        compiler_params=_cparams("parallel"),
        name="swa_prompt",
    )(sinks, q, kv, kv, bias)


def _swa_sample_kernel(q_ref, kvn_ref, kb_ref, vb_ref, bias_b_ref, bias_n_ref, sink_ref, o_ref, *, bb, t):
    gt = SWA_GROUP * t
    rows = SWA_KV_HEADS * gt
    w = kb_ref.shape[1]
    scale = SWA_HEAD_DIM ** -0.5
    lane_head = lax.broadcasted_iota(jnp.int32, (gt, SWA_KV_DIM), 1) // SWA_HEAD_DIM
    tok = lax.broadcasted_iota(jnp.int32, (rows, w), 0) % t
    keyj = lax.broadcasted_iota(jnp.int32, (rows, w), 1)
    valid_b = (tok + w - keyj) < WINDOW
    tok_n = lax.broadcasted_iota(jnp.int32, (rows, 1), 0) % t
    sink = sink_ref[...]
    for b in range(bb):
        qgt = q_ref[b].astype(F32)
        qall = jnp.concatenate(
            [jnp.where(lane_head == kvh, qgt, 0.0) for kvh in range(SWA_KV_HEADS)], axis=0)
        kb = kb_ref[b]
        vb = vb_ref[b]
        kvn = kvn_ref[b]
        lb = _dot_nt(qall.astype(BF16), kb.astype(BF16))
        lb = jnp.where(valid_b, lb * scale + bias_b_ref[...], NEG_INF)
        ln = []
        for j in range(t):
            lj = jnp.sum(qall * kvn[j:j + 1, 0:SWA_KV_DIM], axis=-1, keepdims=True)
            ln.append(jnp.where(tok_n >= j, lj * scale + bias_n_ref[:, j:j + 1], NEG_INF))
        m = jnp.maximum(jnp.max(lb, axis=-1, keepdims=True), sink)
        for lj in ln:
            m = jnp.maximum(m, lj)
        pb = jnp.exp(lb - m)
        pn = [jnp.exp(lj - m) for lj in ln]
        denom = jnp.sum(pb, axis=-1, keepdims=True) + jnp.exp(sink - m)
        for pj in pn:
            denom = denom + pj
        inv = 1.0 / denom
        oall = _dot((pb * inv).astype(BF16), vb.astype(BF16))
        for j in range(t):
            oall = oall + (pn[j] * inv) * kvn[j:j + 1, SWA_KV_DIM:]
        og = jnp.zeros((gt, SWA_KV_DIM), F32)
        for kvh in range(SWA_KV_HEADS):
            og = jnp.where(lane_head == kvh, oall[kvh * gt:(kvh + 1) * gt], og)
        o_ref[b] = og.astype(BF16)


def _swa_sample(q, kvn, kbuf, vbuf, bias_b, bias_n, sink_rows, *, bb=8):
    b, gt, _ = q.shape
    t = kvn.shape[1]
    w = kbuf.shape[1]
    rows = SWA_KV_HEADS * gt
    blk = lambda i: (i, 0, 0)
    const = lambda i: (0, 0)
    return pl.pallas_call(
        functools.partial(_swa_sample_kernel, bb=bb, t=t),
        out_shape=jax.ShapeDtypeStruct((b, gt, SWA_KV_DIM), BF16),
        grid=(b // bb,),
        in_specs=[
            pl.BlockSpec((bb, gt, SWA_KV_DIM), blk),
            pl.BlockSpec((bb, t, 2 * SWA_KV_DIM), blk),
            pl.BlockSpec((bb, w, SWA_KV_DIM), blk),
            pl.BlockSpec((bb, w, SWA_KV_DIM), blk),
            pl.BlockSpec((rows, w), const),
            pl.BlockSpec((rows, t), const),
            pl.BlockSpec((rows, 1), const),
        ],
        out_specs=pl.BlockSpec((bb, gt, SWA_KV_DIM), blk),
        compiler_params=_cparams("parallel"),
        name="swa_sample",
    )(q, kvn, kbuf, vbuf, bias_b, bias_n, sink_rows)


def _softmax_rows(x):
    m = jnp.max(x, axis=-1, keepdims=True)
    p = jnp.exp(x - m)
    return p * (1.0 / jnp.sum(p, axis=-1, keepdims=True))


def _mem_heads(q, mk, mv):
    scale = MEM_HEAD_DIM ** -0.5
    outs = []
    for h in range(MEM_HEADS):
        sl = slice(h * MEM_HEAD_DIM, (h + 1) * MEM_HEAD_DIM)
        p = _softmax_rows(_dot_nt(q[:, sl], mk[:, sl]) * scale)
        outs.append(_dot(p.astype(BF16), mv[:, sl]))
    return jnp.concatenate(outs, axis=-1)


def _mem_prompt_kernel(q_ref, mk_ref, mv_ref, o_ref):
    o_ref[...] = _mem_heads(q_ref[...], mk_ref[...].astype(BF16), mv_ref[...].astype(BF16)).astype(BF16)


def _mem_prompt(q, mkv, *, tm=512):
    m = q.shape[0]
    return pl.pallas_call(
        _mem_prompt_kernel,
        out_shape=jax.ShapeDtypeStruct((m, MEM_DIM), BF16),
        grid=(m // tm,),
        in_specs=[
            pl.BlockSpec((tm, MEM_DIM), lambda i: (i, 0)),
            pl.BlockSpec((N_MEM, MEM_DIM), lambda i: (0, 0)),
            pl.BlockSpec((N_MEM, MEM_DIM), lambda i: (0, 1)),
        ],
        out_specs=pl.BlockSpec((tm, MEM_DIM), lambda i: (i, 0)),
        compiler_params=_cparams("parallel"),
        name="mem_prompt",
    )(q, mkv, mkv)


def _mem_sample_kernel(q_ref, mk_ref, mv_ref, o_ref, *, bb):
    for b in range(bb):
        o_ref[b] = _mem_heads(q_ref[b], mk_ref[b].astype(BF16), mv_ref[b].astype(BF16)).astype(BF16)


def _mem_sample(q, mk, mv, *, bb=8):
    b, t, _ = q.shape
    blk = lambda i: (i, 0, 0)
    return pl.pallas_call(
        functools.partial(_mem_sample_kernel, bb=bb),
        out_shape=jax.ShapeDtypeStruct((b, t, MEM_DIM), BF16),
        grid=(b // bb,),
        in_specs=[
            pl.BlockSpec((bb, t, MEM_DIM), blk),
            pl.BlockSpec((bb, N_MEM, MEM_DIM), blk),
            pl.BlockSpec((bb, N_MEM, MEM_DIM), blk),
        ],
        out_specs=pl.BlockSpec((bb, t, MEM_DIM), blk),
        compiler_params=_cparams("parallel"),
        name="mem_sample",
    )(q, mk, mv)


def _head_sum(x, ones_ref):
    hi = x.astype(BF16)
    lo = (x - hi.astype(F32)).astype(BF16)
    return _dot(hi, ones_ref[...]) + _dot(lo, ones_ref[...])


def _rwkv_prep_kernel(x_ref, pv_ref, mu_ref, w0_ref, ww2_ref, a0_ref, aw2_ref, gw2_ref, kk_ref, ka_ref,
                      rk_ref, ones_ref, r_o, w_o, k_o, v_o, kk_o, kka_o, g_o, bonus_o):
    x = x_ref[...]
    xs = x + mu_ref[...] * (pv_ref[...] - x)
    d = RWKV_DIM
    r = xs[:, 0:d]
    k = xs[:, d:2 * d]
    v = xs[:, 2 * d:3 * d]
    lw = xs[:, 3 * d:3 * d + LORA_W]
    la = xs[:, 3 * d + LORA_W:3 * d + LORA_W + LORA_A]
    lg = xs[:, 3 * d + LORA_W + LORA_A:]
    wpre = w0_ref[...] + _dot_hi(jnp.tanh(lw), ww2_ref[...])
    w = -jax.nn.softplus(-wpre) - 0.5
    log_decay = -jnp.exp(w)
    a = jax.nn.sigmoid(a0_ref[...] + _dot_hi(la, aw2_ref[...]))
    g = _dot_hi(jax.nn.sigmoid(lg), gw2_ref[...])
    kk = k * kk_ref[...]
    kk = kk / jnp.maximum(jnp.sqrt(_head_sum(kk * kk, ones_ref)), 1e-12)
    kh = k * (1.0 + (a - 1.0) * ka_ref[...])
    r_o[...] = r
    w_o[...] = log_decay
    k_o[...] = kh
    v_o[...] = v
    kk_o[...] = kk
    kka_o[...] = kk * a
    g_o[...] = g
    bonus_o[...] = _head_sum(r * kh * rk_ref[...], ones_ref) * v


def _rwkv_prep(x, prev, p, ones, *, tm=256):
    m = x.shape[0]
    row = lambda i: (i, 0)
    const = lambda i: (0, 0)
    vec = lambda n: pl.BlockSpec((1, n), const)
    out = jax.ShapeDtypeStruct((m, RWKV_DIM), F32)
    return pl.pallas_call(
        _rwkv_prep_kernel,
        out_shape=(out,) * 8,
        grid=(m // tm,),
        in_specs=[
            pl.BlockSpec((tm, RWKV_IN), row),
            pl.BlockSpec((tm, RWKV_IN), row),
            vec(RWKV_IN), vec(RWKV_DIM),
            pl.BlockSpec((LORA_W, RWKV_DIM), const),
            vec(RWKV_DIM),
            pl.BlockSpec((LORA_A, RWKV_DIM), const),
            pl.BlockSpec((LORA_G, RWKV_DIM), const),
            vec(RWKV_DIM), vec(RWKV_DIM), vec(RWKV_DIM),
            pl.BlockSpec((RWKV_DIM, RWKV_DIM), const),
        ],
        out_specs=(pl.BlockSpec((tm, RWKV_DIM), row),) * 8,
        compiler_params=_cparams("parallel"),
        name="rwkv_prep",
    )(x, prev, p["mu"], p["w0"], p["w_w2"], p["a0"], p["a_w2"], p["g_w2"], p["k_k"], p["k_a"], p["r_k"], ones)


def _half_sums(x, low):
    lo = jnp.sum(jnp.where(low, x, 0.0), axis=-1, keepdims=True)
    hi = jnp.sum(jnp.where(low, 0.0, x), axis=-1, keepdims=True)
    return jnp.where(low, lo, hi)


def _rwkv_scan_kernel(r_ref, w_ref, k_ref, v_ref, kk_ref, kka_ref, s0_ref, y_ref, so_ref, s_scr, *, bb, tt):
    jt = pl.program_id(1)
    group = min(tt, SUBLANES)
    n = RWKV_HEAD_DIM
    pairs = RWKV_HEADS // 2
    low = lax.broadcasted_iota(jnp.int32, (n, LANES), 1) < n
    diag = (lax.broadcasted_iota(jnp.int32, (n, LANES), 1) % n) == lax.broadcasted_iota(jnp.int32, (n, LANES), 0)

    @pl.when(jt == 0)
    def _():
        for b in range(bb):
            for h in range(RWKV_HEADS):
                s_scr[b, h // 2, :, (h % 2) * n:(h % 2 + 1) * n] = s0_ref[b, h]

    def batch_body(b, carry):
        def group_body(t0, states):
            toks = pl.ds(pl.multiple_of(t0 * group, group), group)
            new_states = []
            for p in range(pairs):
                sl = pl.ds(p * LANES, LANES)
                r, w, k, v = r_ref[b, toks, sl], jnp.exp(w_ref[b, toks, sl]), k_ref[b, toks, sl], v_ref[b, toks, sl]
                kk, kka = kk_ref[b, toks, sl], kka_ref[b, toks, sl]
                s = states[p]
                ys = []
                for j in range(group):
                    one = slice(j, j + 1)
                    sa = _half_sums(s * (-kk[one]), low)
                    vcol = _half_sums(jnp.where(diag, v[one], 0.0), low)
                    s = s * w[one] + sa * kka[one] + vcol * k[one]
                    ycol = _half_sums(s * r[one], low)
                    ys.append(jnp.sum(jnp.where(diag, ycol, 0.0), axis=0, keepdims=True))
                y_ref[b, toks, sl] = jnp.concatenate(ys, axis=0)
                new_states.append(s)
            return tuple(new_states)

        states = tuple(s_scr[b, p] for p in range(pairs))
        states = lax.fori_loop(0, tt // group, group_body, states)
        for p in range(pairs):
            s_scr[b, p] = states[p]
        return carry

    lax.fori_loop(0, bb, batch_body, 0)

    @pl.when(jt == pl.num_programs(1) - 1)
    def _():
        for b in range(bb):
            for h in range(RWKV_HEADS):
                so_ref[b, h] = s_scr[b, h // 2, :, (h % 2) * n:(h % 2 + 1) * n]


def _rwkv_scan(r, w, k, v, kk, kka, s0, *, bb, tt):
    b, t, d = r.shape
    n = RWKV_HEAD_DIM
    tok = pl.BlockSpec((bb, tt, d), lambda i, j: (i, j, 0))
    st = pl.BlockSpec((bb, RWKV_HEADS, n, n), lambda i, j: (i, 0, 0, 0))
    return pl.pallas_call(
        functools.partial(_rwkv_scan_kernel, bb=bb, tt=tt),
        out_shape=(jax.ShapeDtypeStruct((b, t, d), F32), jax.ShapeDtypeStruct((b, RWKV_HEADS, n, n), F32)),
        grid=(b // bb, t // tt),
        in_specs=[tok] * 6 + [st],
        out_specs=(tok, st),
        scratch_shapes=[pltpu.VMEM((bb, RWKV_HEADS // 2, n, LANES), F32)],
        compiler_params=_cparams("parallel", "arbitrary"),
        name="rwkv_scan",
    )(r, w, k, v, kk, kka, s0)


CHUNK = 64
GROUP_HEADS = 4
GROUP_W = GROUP_HEADS * RWKV_HEAD_DIM
N_GROUPS = RWKV_HEADS // GROUP_HEADS
(MASK_SAME, MASK_STRICT, MASK_INCL, MASK_LEVEL0) = (0, 1, 2, 3)
N_LEVELS = int(math.log2(CHUNK))


def _chunk_masks():
    i = np.arange(GROUP_W)
    same = (i[:, None] // CHUNK) == (i[None, :] // CHUNK)
    masks = [same, same & (i[None, :] < i[:, None]), same & (i[None, :] <= i[:, None])]
    for lvl in range(N_LEVELS):
        m = 1 << lvl
        masks.append(((i[:, None] // (2 * m)) == (i[None, :] // (2 * m))) & ((i[:, None] // m) != (i[None, :] // m))
                     & (i[None, :] < i[:, None]))
    return np.stack(masks).astype(np.float32)


def _rwkv_chunk_kernel(r_ref, lw_ref, k_ref, v_ref, kk_ref, kka_ref, st0_ref, tri_ref, eye_ref, mask_ref,
                       y_ref, sto_ref, st_scr):
    @pl.when(pl.program_id(0) == 0)
    def _():
        st_scr[...] = st0_ref[...]

    eye = eye_ref[...]
    eye_b = eye.astype(BF16)
    tri = tri_ref[...]
    tile_rows = lambda x: jnp.concatenate([x] * GROUP_HEADS, axis=0)
    block_diag = lambda x: (tile_rows(x) * mask_ref[MASK_SAME]).astype(BF16)

    for g in range(N_GROUPS):
        sl = slice(g * GROUP_W, (g + 1) * GROUP_W)
        lw = lw_ref[:, sl]
        h1 = lw.astype(BF16)
        r1 = lw - h1.astype(F32)
        h2 = r1.astype(BF16)
        h3 = (r1 - h2.astype(F32)).astype(BF16)
        cum = _dot(tri, h1) + _dot(tri, h2) + _dot(tri, h3)
        cum_last = cum[CHUNK - 1:CHUNK, :]
        p_incl = jnp.exp(cum)
        p_prev = jnp.exp(cum - lw)
        p_inv = jnp.exp(-cum)
        p_tail = jnp.exp(cum_last - cum)
        kka = kka_ref[:, sl]
        k = k_ref[:, sl]
        a_bd = block_diag(-kk_ref[:, sl] * p_prev)
        r_f = tile_rows(r_ref[:, sl] * p_incl) * mask_ref[MASK_SAME]
        r_bd = r_f.astype(BF16)
        v_bd = block_diag(v_ref[:, sl])
        b_rep = tile_rows((kka * p_inv).astype(BF16))
        k_rep = tile_rows((k * p_inv).astype(BF16))
        bh_rep = tile_rows((kka * p_tail).astype(BF16))
        kh_rep = tile_rows((k * p_tail).astype(BF16))

        l_ab_f = _dot_nt(a_bd, b_rep) * mask_ref[MASK_STRICT]
        l_ab = l_ab_f.astype(BF16)
        l_ak = (_dot_nt(a_bd, k_rep) * mask_ref[MASK_STRICT]).astype(BF16)
        m_rb = (_dot_nt(r_bd, b_rep) * mask_ref[MASK_INCL]).astype(BF16)
        m_rk = (_dot_nt(r_bd, k_rep) * mask_ref[MASK_INCL]).astype(BF16)

        d = eye + l_ab_f * mask_ref[MASK_LEVEL0]
        for lvl in range(1, N_LEVELS):
            d_b = d.astype(BF16)
            x = (_dot(l_ab, d_b) * mask_ref[MASK_LEVEL0 + lvl]).astype(BF16)
            d = d + _dot(d_b, x)
        t_b = d.astype(BF16)

        wm = _dot(jnp.concatenate([l_ak, m_rk], axis=0), v_bd)
        w_b = wm[:GROUP_W].astype(BF16)
        mrk_v = wm[GROUP_W:]
        twa = _dot(t_b, jnp.concatenate([w_b, a_bd], axis=1)).astype(BF16)
        ry = _dot(m_rb, twa)
        y0 = ry[:, :GROUP_W] + mrk_v
        r_eff = (r_f + ry[:, GROUP_W:]).astype(BF16)
        bh_t = (_dot_nt(eye_b, bh_rep) * mask_ref[MASK_SAME]).astype(BF16)
        kh_t = (_dot_nt(eye_b, kh_rep) * mask_ref[MASK_SAME]).astype(BF16)
        mn = _dot(bh_t, twa)
        n_x = mn[:, :GROUP_W] + _dot(kh_t, v_bd)
        m_x = (eye * jnp.exp(cum_last) + mn[:, GROUP_W:]).astype(BF16)

        st = st_scr[g]
        ys = _dot(jnp.concatenate([m_x, r_eff], axis=0), st.astype(BF16))
        st_scr[g] = ys[:GROUP_W] + n_x
        y_bd = ys[GROUP_W:] + y0
        y = y_bd[0:CHUNK]
        for h in range(1, GROUP_HEADS):
            y = y + y_bd[h * CHUNK:(h + 1) * CHUNK]
        y_ref[:, sl] = y

    @pl.when(pl.program_id(0) == pl.num_programs(0) - 1)
    def _():
        sto_ref[...] = st_scr[...]


def _rwkv_chunked(r, lw, k, v, kk, kka, s0):
    t, d = r.shape
    n = RWKV_HEAD_DIM
    assert CHUNK == n and t % CHUNK == 0
    st0 = jnp.einsum("ghvk,hj->ghkjv", s0.reshape(N_GROUPS, GROUP_HEADS, n, n), jnp.eye(GROUP_HEADS, dtype=F32))
    st0 = st0.reshape(N_GROUPS, GROUP_W, GROUP_W)
    tri = jnp.asarray(np.tril(np.ones((CHUNK, CHUNK), np.float32)), BF16)
    eye = jnp.eye(GROUP_W, dtype=F32)
    masks = jnp.asarray(_chunk_masks())
    tok = pl.BlockSpec((CHUNK, d), lambda c: (c, 0))
    st_spec = pl.BlockSpec((N_GROUPS, GROUP_W, GROUP_W), lambda c: (0, 0, 0))
    y, st = pl.pallas_call(
        _rwkv_chunk_kernel,
        out_shape=(jax.ShapeDtypeStruct((t, d), F32), jax.ShapeDtypeStruct((N_GROUPS, GROUP_W, GROUP_W), F32)),
        grid=(t // CHUNK,),
        in_specs=[tok] * 6 + [st_spec, pl.BlockSpec((CHUNK, CHUNK), lambda c: (0, 0)),
                              pl.BlockSpec((GROUP_W, GROUP_W), lambda c: (0, 0)),
                              pl.BlockSpec(masks.shape, lambda c: (0, 0, 0))],
        out_specs=(tok, st_spec),
        scratch_shapes=[pltpu.VMEM((N_GROUPS, GROUP_W, GROUP_W), F32)],
        compiler_params=_cparams("arbitrary"),
        name="rwkv_chunk",
    )(r, lw, k, v, kk, kka, st0, tri, eye, masks)
    st5 = st.reshape(N_GROUPS, GROUP_HEADS, n, GROUP_HEADS, n)
    s_new = jnp.einsum("ghkjv,hj->ghvk", st5, jnp.eye(GROUP_HEADS, dtype=F32)).reshape(RWKV_HEADS, n, n)
    return y, s_new


def _rwkv_post_kernel(y_ref, bonus_ref, g_ref, lnw_ref, lnb_ref, ones_ref, o_ref):
    y = y_ref[...]
    inv_n = 1.0 / RWKV_HEAD_DIM
    mu = _head_sum(y, ones_ref) * inv_n
    dlt = y - mu
    var = _head_sum(dlt * dlt, ones_ref) * inv_n
    yn = dlt * lax.rsqrt(var + GN_EPS) * lnw_ref[...] + lnb_ref[...]
    o_ref[...] = ((yn + bonus_ref[...]) * g_ref[...]).astype(BF16)


def _rwkv_post(y, bonus, g, lnw, lnb, ones, *, tm=256):
    m = y.shape[0]
    row = lambda i: (i, 0)
    const = lambda i: (0, 0)
    tile = pl.BlockSpec((tm, RWKV_DIM), row)
    return pl.pallas_call(
        _rwkv_post_kernel,
        out_shape=jax.ShapeDtypeStruct((m, RWKV_DIM), BF16),
        grid=(m // tm,),
        in_specs=[tile, tile, tile, pl.BlockSpec((1, RWKV_DIM), const), pl.BlockSpec((1, RWKV_DIM), const),
                  pl.BlockSpec((RWKV_DIM, RWKV_DIM), const)],
        out_specs=tile,
        compiler_params=_cparams("parallel"),
        name="rwkv_post",
    )(y, bonus, g, lnw, lnb, ones)


def _merge_kernel(h_ref, g_ref, os_ref, or_ref, om_ref, wg0_ref, wg1_ref, wg2_ref, wos_ref, wor_ref, wom_ref,
                  wout_ref, o_ref, u_ref, acc_ref):
    j = pl.program_id(1)

    @pl.when(j == 0)
    def _():
        u_ref[...] = _rms(h_ref[...], g_ref[...]).astype(BF16)
        acc_ref[...] = jnp.zeros_like(acc_ref)

    u = u_ref[...]
    merged = jax.nn.sigmoid(_dot(u, wg0_ref[...])) * _dot(os_ref[...], wos_ref[...])
    merged += jax.nn.sigmoid(_dot(u, wg1_ref[...])) * _dot(or_ref[...], wor_ref[...])
    merged += jax.nn.sigmoid(_dot(u, wg2_ref[...])) * _dot(om_ref[...], wom_ref[...])
    acc_ref[...] += _dot(merged.astype(BF16), wout_ref[...])

    @pl.when(j == pl.num_programs(1) - 1)
    def _():
        o_ref[...] = h_ref[...] + acc_ref[...]


def _merge(h, g, o_swa, o_rw, o_mem, w_gate, wo_swa, wo_rw, wo_mem, w_out, *, tm=512, tn=256):
    m, d = h.shape
    nt = d // tn
    row = lambda i, j: (i, 0)
    col = lambda i, j: (0, j)
    return pl.pallas_call(
        _merge_kernel,
        out_shape=jax.ShapeDtypeStruct((m, d), F32),
        grid=(m // tm, nt),
        in_specs=[
            pl.BlockSpec((tm, d), row),
            pl.BlockSpec((1, d), lambda i, j: (0, 0)),
            pl.BlockSpec((tm, SWA_Q_DIM), row),
            pl.BlockSpec((tm, RWKV_DIM), row),
            pl.BlockSpec((tm, MEM_DIM), row),
            pl.BlockSpec((d, tn), lambda i, j: (0, j)),
            pl.BlockSpec((d, tn), lambda i, j: (0, j + nt)),
            pl.BlockSpec((d, tn), lambda i, j: (0, j + 2 * nt)),
            pl.BlockSpec((SWA_Q_DIM, tn), col),
            pl.BlockSpec((RWKV_DIM, tn), col),
            pl.BlockSpec((MEM_DIM, tn), col),
            pl.BlockSpec((tn, d), lambda i, j: (j, 0)),
        ],
        out_specs=pl.BlockSpec((tm, d), row),
        scratch_shapes=[pltpu.VMEM((tm, d), BF16), pltpu.VMEM((tm, d), F32)],
        compiler_params=_cparams("parallel", "arbitrary"),
        name="merge",
    )(h, g, o_swa, o_rw, o_mem, w_gate, w_gate, w_gate, wo_swa, wo_rw, wo_mem, w_out)


def _t5_bucket(dist):
    max_exact = N_BUCKETS // 2
    d = np.maximum(dist, 0)
    log_ratio = (np.log(np.maximum(d, 1).astype(np.float32) / np.float32(max_exact))
                 / np.float32(math.log(MAX_DISTANCE / max_exact)))
    large = np.minimum(max_exact + (log_ratio * (N_BUCKETS - max_exact)).astype(np.int32), N_BUCKETS - 1)
    return np.where(d < max_exact, d, large).astype(np.int32)


def _q_perm():
    idx = np.arange(SWA_Q_DIM).reshape(SWA_KV_HEADS, SWA_GROUP, SWA_HEAD_DIM)
    return idx.transpose(1, 0, 2).reshape(-1)


def _rwkv_branch(xr, shift0, s0, p, ones, *, bb, tt):
    b, t, _ = xr.shape
    prev = jnp.concatenate([shift0, xr[:, :-1]], axis=1)
    flat = lambda z: z.reshape(b * t, z.shape[-1])
    r, w, k, v, kk, kka, g, bonus = _rwkv_prep(flat(xr), flat(prev), p, ones)
    if b == 1:
        y, s_new = _rwkv_chunked(r, w, k, v, kk, kka, s0[0])
        s_new = s_new[None]
    else:
        seq = lambda z: z.reshape(b, t, RWKV_DIM)
        y, s_new = _rwkv_scan(seq(r), seq(w), seq(k), seq(v), seq(kk), seq(kka), s0, bb=bb, tt=tt)
    o = _rwkv_post(flat(y), bonus, g, p["ln_w"], p["ln_b"], ones)
    return o, s_new


def kernel(x_prompt, mem_prompt, x_sample, cache_swa_k, cache_swa_v, state_rwkv, state_rwkv_shift, cache_mem_k, cache_mem_v, ffn1_norm, ffn1_wi, ffn1_wo, mix_norm, w_in, swa_sinks, rel_bias_table, rwkv_mu, rwkv_w0, rwkv_w_w2, rwkv_a0, rwkv_a_w2, rwkv_g_w2, rwkv_k_k, rwkv_k_a, rwkv_r_k, rwkv_ln_w, rwkv_ln_b, mem_norm, w_mem_kv, w_o_swa, w_o_rwkv, w_o_mem, w_out, ffn2_norm, ffn2_wi, ffn2_wo, final_norm):
    assert ffn1_wi.shape[0] == 1, "single-layer trunk"
    bp, tp, d = x_prompt.shape
    bs, ts, _ = x_sample.shape
    assert bp == 1
    row = lambda z: z.reshape(1, -1).astype(F32)

    qperm = _q_perm()
    w_in0 = w_in[0]
    w_proj = jnp.concatenate([w_in0[:, :SWA_Q_DIM][:, qperm], w_in0[:, SWA_Q_DIM:PROJ_DIM]], axis=1).astype(BF16)
    w_gate = w_in0[:, PROJ_DIM:].astype(BF16)
    wo_swa = w_o_swa[0][qperm, :].astype(BF16)
    wo_rw = w_o_rwkv[0].astype(BF16)
    wo_mem = w_o_mem[0].astype(BF16)
    w_out_b = w_out[0].astype(BF16)
    wi1, wo1 = ffn1_wi[0].astype(BF16), ffn1_wo[0].astype(BF16)
    wi2, wo2 = ffn2_wi[0].astype(BF16), ffn2_wo[0].astype(BF16)
    g1, gm, g2, gf = row(ffn1_norm[0]), row(mix_norm[0]), row(ffn2_norm[0]), row(final_norm)
    rp = {
        "mu": row(rwkv_mu[0]), "w0": row(rwkv_w0[0]), "w_w2": rwkv_w_w2[0], "a0": row(rwkv_a0[0]),
        "a_w2": rwkv_a_w2[0], "g_w2": rwkv_g_w2[0], "k_k": row(rwkv_k_k[0]), "k_a": row(rwkv_k_a[0]),
        "r_k": row(rwkv_r_k[0]), "ln_w": row(rwkv_ln_w[0]), "ln_b": row(rwkv_ln_b[0]),
    }
    seg = np.arange(RWKV_DIM) // RWKV_HEAD_DIM
    ones = jnp.asarray(seg[:, None] == seg[None, :], dtype=BF16)
    sinks = swa_sinks[0].astype(F32)
    table = rel_bias_table.astype(F32)

    xp = x_prompt.reshape(tp, d)
    xs = x_sample.reshape(bs * ts, d)
    hp = _ffn(xp, g1, wi1, wo1, gf, final_norm=False)
    hs = _ffn(xs, g1, wi1, wo1, gf, final_norm=False)
    qp, kvp, xrp, qmp = _inproj(hp, gm, w_proj)
    qs, kvs, xrs, qms = _inproj(hs, gm, w_proj)

    w = WINDOW
    dist_p = np.arange(w)[:, None] + w - np.arange(2 * w)[None, :]
    bias_p = jnp.moveaxis(table[_t5_bucket(dist_p)], -1, 0)
    o_swa_p = _swa_prompt(qp, kvp, bias_p, sinks)

    wbuf = cache_swa_k.shape[2]
    dist_s = np.arange(ts)[:, None] + wbuf - np.arange(wbuf + ts)[None, :]
    bias_s = jnp.moveaxis(table[_t5_bucket(dist_s)], -1, 0)
    bias_s = bias_s.reshape(SWA_HEADS * ts, wbuf + ts)
    sink_rows = jnp.repeat(sinks, ts).reshape(SWA_HEADS * ts, 1)
    qs_gt = qs.reshape(bs, ts, SWA_GROUP, SWA_KV_DIM).transpose(0, 2, 1, 3).reshape(bs, SWA_GROUP * ts, SWA_KV_DIM)
    kbuf = cache_swa_k[0].reshape(bs, wbuf, SWA_KV_DIM)
    vbuf = cache_swa_v[0].reshape(bs, wbuf, SWA_KV_DIM)
    o_swa_s = _swa_sample(qs_gt, kvs.reshape(bs, ts, 2 * SWA_KV_DIM), kbuf, vbuf,
                          bias_s[:, :wbuf], bias_s[:, wbuf:], sink_rows)
    o_swa_s = o_swa_s.reshape(bs, SWA_GROUP, ts, SWA_KV_DIM).transpose(0, 2, 1, 3).reshape(bs * ts, SWA_Q_DIM)

    zero_shift = jnp.zeros((bp, 1, RWKV_IN), F32)
    zero_state = jnp.zeros((bp, RWKV_HEADS, RWKV_HEAD_DIM, RWKV_HEAD_DIM), F32)
    o_rw_p, state_p = _rwkv_branch(xrp.reshape(bp, tp, RWKV_IN), zero_shift, zero_state, rp, ones,
                                   bb=1, tt=256)
    o_rw_s, state_s = _rwkv_branch(xrs.reshape(bs, ts, RWKV_IN), state_rwkv_shift[0], state_rwkv[0], rp, ones,
                                   bb=8, tt=ts)

    mkv = _norm_matmul(mem_prompt.reshape(N_MEM, d), row(mem_norm[0]), w_mem_kv[0].astype(BF16))
    o_mem_p = _mem_prompt(qmp, mkv)
    o_mem_s = _mem_sample(qms.reshape(bs, ts, MEM_DIM), cache_mem_k[0].reshape(bs, N_MEM, MEM_DIM),
                          cache_mem_v[0].reshape(bs, N_MEM, MEM_DIM)).reshape(bs * ts, MEM_DIM)

    hp = _merge(hp, gm, o_swa_p, o_rw_p, o_mem_p, w_gate, wo_swa, wo_rw, wo_mem, w_out_b)
    hs = _merge(hs, gm, o_swa_s, o_rw_s, o_mem_s, w_gate, wo_swa, wo_rw, wo_mem, w_out_b)
    y_prompt = _ffn(hp, g2, wi2, wo2, gf, final_norm=True).reshape(bp, tp, d)
    y_sample = _ffn(hs, g2, wi2, wo2, gf, final_norm=True).reshape(bs, ts, d)

    wp = min(w, tp)
    p_k = kvp[tp - wp:, :SWA_KV_DIM].reshape(1, bp, wp, SWA_KV_HEADS, SWA_HEAD_DIM)
    p_v = kvp[tp - wp:, SWA_KV_DIM:].reshape(1, bp, wp, SWA_KV_HEADS, SWA_HEAD_DIM)
    p_mk = mkv[:, :MEM_DIM].reshape(1, bp, N_MEM, MEM_HEADS, MEM_HEAD_DIM)
    p_mv = mkv[:, MEM_DIM:].reshape(1, bp, N_MEM, MEM_HEADS, MEM_HEAD_DIM)
    kvs3 = kvs.reshape(bs, ts, 2 * SWA_KV_DIM)
    s_k = jnp.concatenate([kbuf[:, ts:], kvs3[:, :, :SWA_KV_DIM]], axis=1).reshape(1, bs, wbuf, SWA_KV_HEADS, SWA_HEAD_DIM)
    s_v = jnp.concatenate([vbuf[:, ts:], kvs3[:, :, SWA_KV_DIM:]], axis=1).reshape(1, bs, wbuf, SWA_KV_HEADS, SWA_HEAD_DIM)
    return (y_prompt, y_sample,
            p_k, p_v, state_p[None], xrp[tp - 1:].reshape(1, bp, 1, RWKV_IN), p_mk, p_mv,
            s_k, s_v, state_s[None], xrs.reshape(bs, ts, RWKV_IN)[:, ts - 1:][None])
```

```python
import functools
import math

import jax
import jax.numpy as jnp
import numpy as np
from jax import lax
from jax.experimental import pallas as pl
from jax.experimental.pallas import tpu as pltpu

F32 = jnp.float32
BF16 = jnp.bfloat16

D_MODEL = 2048
D_FF = 5632
SWA_HEADS = 16
SWA_KV_HEADS = 4
SWA_GROUP = SWA_HEADS // SWA_KV_HEADS
SWA_HEAD_DIM = 64
SWA_Q_DIM = SWA_HEADS * SWA_HEAD_DIM
SWA_KV_DIM = SWA_KV_HEADS * SWA_HEAD_DIM
WINDOW = 128
N_BUCKETS = 32
MAX_DISTANCE = 128
RWKV_HEADS = 8
RWKV_HEAD_DIM = 64
RWKV_DIM = RWKV_HEADS * RWKV_HEAD_DIM
LORA_W = 64
LORA_A = 64
LORA_G = 128
RWKV_IN = 3 * RWKV_DIM + LORA_W + LORA_A + LORA_G
N_MEM = 256
MEM_HEADS = 4
MEM_HEAD_DIM = 128
MEM_DIM = MEM_HEADS * MEM_HEAD_DIM
N_BRANCH = 3
PROJ_DIM = SWA_Q_DIM + 2 * SWA_KV_DIM + RWKV_IN + MEM_DIM
NORM_EPS = 1e-6
GN_EPS = 64e-5
NEG_INF = -1e30

LANES = 128
SUBLANES = 8
VMEM_LIMIT = 56 * 1024 * 1024


def _cparams(*sem):
    return pltpu.CompilerParams(dimension_semantics=sem, vmem_limit_bytes=VMEM_LIMIT)


def _rms(x, g):
    return x * lax.rsqrt(jnp.mean(x * x, axis=-1, keepdims=True) + NORM_EPS) * g


def _dot(a, b):
    return jnp.dot(a, b, preferred_element_type=F32)


def _dot_nt(a, b):
    return lax.dot_general(a, b, (((1,), (1,)), ((), ())), preferred_element_type=F32)


def _dot_hi(a, b):
    return jnp.dot(a, b, preferred_element_type=F32, precision=lax.Precision.HIGHEST)


def _ffn_kernel(x_ref, g_ref, wg_ref, wu_ref, wo_ref, gf_ref, o_ref, xn_ref, acc_ref, *, final_norm):
    j = pl.program_id(1)

    @pl.when(j == 0)
    def _():
        xn_ref[...] = _rms(x_ref[...], g_ref[...]).astype(BF16)
        acc_ref[...] = jnp.zeros_like(acc_ref)

    xn = xn_ref[...]
    gate = _dot(xn, wg_ref[...])
    up = _dot(xn, wu_ref[...])
    act = (gate * jax.nn.sigmoid(gate)) * up
    acc_ref[...] += _dot(act.astype(BF16), wo_ref[...])

    @pl.when(j == pl.num_programs(1) - 1)
    def _():
        h = x_ref[...] + 0.5 * acc_ref[...]
        if final_norm:
            h = _rms(h, gf_ref[...])
        o_ref[...] = h


def _ffn(x, g, wi, wo, gf, *, final_norm, tm=512, tf=512):
    m, d = x.shape
    dff = wo.shape[0]
    nf = dff // tf
    return pl.pallas_call(
        functools.partial(_ffn_kernel, final_norm=final_norm),
        out_shape=jax.ShapeDtypeStruct((m, d), F32),
        grid=(m // tm, nf),
        in_specs=[
            pl.BlockSpec((tm, d), lambda i, j: (i, 0)),
            pl.BlockSpec((1, d), lambda i, j: (0, 0)),
            pl.BlockSpec((d, tf), lambda i, j: (0, j)),
            pl.BlockSpec((d, tf), lambda i, j: (0, j + nf)),
            pl.BlockSpec((tf, d), lambda i, j: (j, 0)),
            pl.BlockSpec((1, d), lambda i, j: (0, 0)),
        ],
        out_specs=pl.BlockSpec((tm, d), lambda i, j: (i, 0)),
        scratch_shapes=[pltpu.VMEM((tm, d), BF16), pltpu.VMEM((tm, d), F32)],
        compiler_params=_cparams("parallel", "arbitrary"),
        name="ffn_final" if final_norm else "ffn",
    )(x, g, wi, wi, wo, gf)


def _inproj_kernel(h_ref, g_ref, w_ref, q_ref, kv_ref, xr_ref, qm_ref):
    u = _rms(h_ref[...], g_ref[...]).astype(BF16)
    c0, c1, c2 = SWA_Q_DIM, SWA_Q_DIM + 2 * SWA_KV_DIM, SWA_Q_DIM + 2 * SWA_KV_DIM + RWKV_IN
    q_ref[...] = _dot(u, w_ref[:, 0:c0]).astype(BF16)
    kv_ref[...] = _dot(u, w_ref[:, c0:c1])
    xr_ref[...] = _dot(u, w_ref[:, c1:c2])
    qm_ref[...] = _dot(u, w_ref[:, c2:PROJ_DIM]).astype(BF16)


def _inproj(h, g, w, *, tm=256):
    m, d = h.shape
    row = lambda i: (i, 0)
    return pl.pallas_call(
        _inproj_kernel,
        out_shape=(
            jax.ShapeDtypeStruct((m, SWA_Q_DIM), BF16),
            jax.ShapeDtypeStruct((m, 2 * SWA_KV_DIM), F32),
            jax.ShapeDtypeStruct((m, RWKV_IN), F32),
            jax.ShapeDtypeStruct((m, MEM_DIM), BF16),
        ),
        grid=(m // tm,),
        in_specs=[
            pl.BlockSpec((tm, d), row),
            pl.BlockSpec((1, d), lambda i: (0, 0)),
            pl.BlockSpec((d, PROJ_DIM), lambda i: (0, 0), pipeline_mode=pl.Buffered(1)),
        ],
        out_specs=(
            pl.BlockSpec((tm, SWA_Q_DIM), row),
            pl.BlockSpec((tm, 2 * SWA_KV_DIM), row),
            pl.BlockSpec((tm, RWKV_IN), row),
            pl.BlockSpec((tm, MEM_DIM), row),
        ),
        compiler_params=_cparams("parallel"),
        name="inproj",
    )(h, g, w)


def _norm_matmul_kernel(x_ref, g_ref, w_ref, o_ref):
    o_ref[...] = _dot(_rms(x_ref[...], g_ref[...]).astype(BF16), w_ref[...])


def _norm_matmul(x, g, w, *, tn=512):
    m, d = x.shape
    n = w.shape[1]
    return pl.pallas_call(
        _norm_matmul_kernel,
        out_shape=jax.ShapeDtypeStruct((m, n), F32),
        grid=(n // tn,),
        in_specs=[
            pl.BlockSpec((m, d), lambda j: (0, 0)),
            pl.BlockSpec((1, d), lambda j: (0, 0)),
            pl.BlockSpec((d, tn), lambda j: (0, j)),
        ],
        out_specs=pl.BlockSpec((m, tn), lambda j: (0, j)),
        compiler_params=_cparams("parallel"),
        name="norm_matmul",
    )(x, g, w)


def _sink_softmax(logits, sink):
    m = jnp.maximum(jnp.max(logits, axis=-1, keepdims=True), sink)
    p = jnp.exp(logits - m)
    denom = jnp.sum(p, axis=-1, keepdims=True) + jnp.exp(sink - m)
    return p * (1.0 / denom)


def _swa_prompt_kernel(q_ref, kvc_ref, kvp_ref, bias_ref, sink_ref, o_ref):
    i = pl.program_id(0)
    w = WINDOW
    rows = SWA_KV_HEADS * w
    k = jnp.concatenate([kvp_ref[:, 0:SWA_KV_DIM], kvc_ref[:, 0:SWA_KV_DIM]], axis=0).astype(BF16)
    v = jnp.concatenate([kvp_ref[:, SWA_KV_DIM:], kvc_ref[:, SWA_KV_DIM:]], axis=0).astype(BF16)
    qpos = lax.broadcasted_iota(jnp.int32, (rows, 2 * w), 0) % w
    col = lax.broadcasted_iota(jnp.int32, (rows, 2 * w), 1)
    dist = qpos + w - col
    valid = (dist >= 0) & (dist < w) & ((col >= w) | (i > 0))
    lane_head = lax.broadcasted_iota(jnp.int32, (w, SWA_KV_DIM), 1) // SWA_HEAD_DIM
    scale = SWA_HEAD_DIM ** -0.5
    groups = range(SWA_GROUP)
    logits = []
    for g in groups:
        qg = q_ref[:, g * SWA_KV_DIM:(g + 1) * SWA_KV_DIM].astype(F32)
        qs = jnp.concatenate([jnp.where(lane_head == kvh, qg, 0.0) for kvh in range(SWA_KV_HEADS)], axis=0)
        lg = _dot_nt(qs.astype(BF16), k)
        logits.append(jnp.where(valid, lg * scale + bias_ref[g], NEG_INF))
    m = [jnp.maximum(jnp.max(logits[g], axis=-1, keepdims=True), sink_ref[g]) for g in groups]
    p = [jnp.exp(logits[g] - m[g]) for g in groups]
    inv = [1.0 / (jnp.sum(p[g], axis=-1, keepdims=True) + jnp.exp(sink_ref[g] - m[g])) for g in groups]
    ov = [_dot((p[g] * inv[g]).astype(BF16), v) for g in groups]
    for g in groups:
        og = jnp.zeros((w, SWA_KV_DIM), F32)
        for kvh in range(SWA_KV_HEADS):
            og = jnp.where(lane_head == kvh, ov[g][kvh * w:(kvh + 1) * w], og)
        o_ref[:, g * SWA_KV_DIM:(g + 1) * SWA_KV_DIM] = og.astype(BF16)


def _swa_prompt(q, kv, bias, sink_rows):
    t = q.shape[0]
    w = WINDOW
    rows = SWA_KV_HEADS * w
    return pl.pallas_call(
        _swa_prompt_kernel,
        out_shape=jax.ShapeDtypeStruct((t, SWA_Q_DIM), BF16),
        grid=(t // w,),
        in_specs=[
            pl.BlockSpec((w, SWA_Q_DIM), lambda i: (i, 0)),
            pl.BlockSpec((w, 2 * SWA_KV_DIM), lambda i: (i, 0)),
            pl.BlockSpec((w, 2 * SWA_KV_DIM), lambda i: (jnp.maximum(i - 1, 0), 0)),
            pl.BlockSpec((SWA_GROUP, rows, 2 * w), lambda i: (0, 0, 0)),
            pl.BlockSpec((SWA_GROUP, rows, 1), lambda i: (0, 0, 0)),
        ],
        out_specs=pl.BlockSpec((w, SWA_Q_DIM), lambda i: (i, 0)),
        compiler_params=_cparams("parallel"),
        name="swa_prompt",
    )(q, kv, kv, bias, sink_rows)


def _swa_sample_kernel(q_ref, kvn_ref, kb_ref, vb_ref, bias_b_ref, bias_n_ref, sink_ref, o_ref, *, bb, t):
    gt = SWA_GROUP * t
    rows = SWA_KV_HEADS * gt
    w = kb_ref.shape[1]
    scale = SWA_HEAD_DIM ** -0.5
    lane_head = lax.broadcasted_iota(jnp.int32, (gt, SWA_KV_DIM), 1) // SWA_HEAD_DIM
    tok = lax.broadcasted_iota(jnp.int32, (rows, w), 0) % t
    keyj = lax.broadcasted_iota(jnp.int32, (rows, w), 1)
    valid_b = (tok + w - keyj) < WINDOW
    tok_n = lax.broadcasted_iota(jnp.int32, (rows, 1), 0) % t
    sink = sink_ref[...]
    for b in range(bb):
        qgt = q_ref[b].astype(F32)
        qall = jnp.concatenate(
            [jnp.where(lane_head == kvh, qgt, 0.0) for kvh in range(SWA_KV_HEADS)], axis=0)
        kb = kb_ref[b]
        vb = vb_ref[b]
        kvn = kvn_ref[b]
        lb = _dot_nt(qall.astype(BF16), kb.astype(BF16))
        lb = jnp.where(valid_b, lb * scale + bias_b_ref[...], NEG_INF)
        ln = []
        for j in range(t):
            lj = jnp.sum(qall * kvn[j:j + 1, 0:SWA_KV_DIM], axis=-1, keepdims=True)
            ln.append(jnp.where(tok_n >= j, lj * scale + bias_n_ref[:, j:j + 1], NEG_INF))
        m = jnp.maximum(jnp.max(lb, axis=-1, keepdims=True), sink)
        for lj in ln:
            m = jnp.maximum(m, lj)
        pb = jnp.exp(lb - m)
        pn = [jnp.exp(lj - m) for lj in ln]
        denom = jnp.sum(pb, axis=-1, keepdims=True) + jnp.exp(sink - m)
        for pj in pn:
            denom = denom + pj
        inv = 1.0 / denom
        oall = _dot((pb * inv).astype(BF16), vb.astype(BF16))
        for j in range(t):
            oall = oall + (pn[j] * inv) * kvn[j:j + 1, SWA_KV_DIM:]
        og = jnp.zeros((gt, SWA_KV_DIM), F32)
        for kvh in range(SWA_KV_HEADS):
            og = jnp.where(lane_head == kvh, oall[kvh * gt:(kvh + 1) * gt], og)
        o_ref[b] = og.astype(BF16)


def _swa_sample(q, kvn, kbuf, vbuf, bias_b, bias_n, sink_rows, *, bb=8):
    b, gt, _ = q.shape
    t = kvn.shape[1]
    w = kbuf.shape[1]
    rows = SWA_KV_HEADS * gt
    blk = lambda i: (i, 0, 0)
    const = lambda i: (0, 0)
    return pl.pallas_call(
        functools.partial(_swa_sample_kernel, bb=bb, t=t),
        out_shape=jax.ShapeDtypeStruct((b, gt, SWA_KV_DIM), BF16),
        grid=(b // bb,),
        in_specs=[
            pl.BlockSpec((bb, gt, SWA_KV_DIM), blk),
            pl.BlockSpec((bb, t, 2 * SWA_KV_DIM), blk),
            pl.BlockSpec((bb, w, SWA_KV_DIM), blk),
            pl.BlockSpec((bb, w, SWA_KV_DIM), blk),
            pl.BlockSpec((rows, w), const),
            pl.BlockSpec((rows, t), const),
            pl.BlockSpec((rows, 1), const),
        ],
        out_specs=pl.BlockSpec((bb, gt, SWA_KV_DIM), blk),
        compiler_params=_cparams("parallel"),
        name="swa_sample",
    )(q, kvn, kbuf, vbuf, bias_b, bias_n, sink_rows)


def _softmax_rows(x):
    m = jnp.max(x, axis=-1, keepdims=True)
    p = jnp.exp(x - m)
    return p * (1.0 / jnp.sum(p, axis=-1, keepdims=True))


def _mem_heads(q, mk, mv):
    scale = MEM_HEAD_DIM ** -0.5
    outs = []
    for h in range(MEM_HEADS):
        sl = slice(h * MEM_HEAD_DIM, (h + 1) * MEM_HEAD_DIM)
        p = _softmax_rows(_dot_nt(q[:, sl], mk[:, sl]) * scale)
        outs.append(_dot(p.astype(BF16), mv[:, sl]))
    return jnp.concatenate(outs, axis=-1)


def _mem_prompt_kernel(q_ref, mk_ref, mv_ref, o_ref):
    o_ref[...] = _mem_heads(q_ref[...], mk_ref[...].astype(BF16), mv_ref[...].astype(BF16)).astype(BF16)


def _mem_prompt(q, mkv, *, tm=512):
    m = q.shape[0]
    return pl.pallas_call(
        _mem_prompt_kernel,
        out_shape=jax.ShapeDtypeStruct((m, MEM_DIM), BF16),
        grid=(m // tm,),
        in_specs=[
            pl.BlockSpec((tm, MEM_DIM), lambda i: (i, 0)),
            pl.BlockSpec((N_MEM, MEM_DIM), lambda i: (0, 0)),
            pl.BlockSpec((N_MEM, MEM_DIM), lambda i: (0, 1)),
        ],
        out_specs=pl.BlockSpec((tm, MEM_DIM), lambda i: (i, 0)),
        compiler_params=_cparams("parallel"),
        name="mem_prompt",
    )(q, mkv, mkv)


def _mem_sample_kernel(q_ref, mk_ref, mv_ref, o_ref, *, bb):
    scale = MEM_HEAD_DIM ** -0.5
    for b in range(bb):
        q = q_ref[b]
        outs = []
        for h in range(MEM_HEADS):
            rows = pl.ds(h, N_MEM, stride=MEM_HEADS)
            p = _softmax_rows(_dot_nt(q[:, h * MEM_HEAD_DIM:(h + 1) * MEM_HEAD_DIM], mk_ref[b, rows, :].astype(BF16)) * scale)
            outs.append(_dot(p.astype(BF16), mv_ref[b, rows, :].astype(BF16)))
        o_ref[b] = jnp.concatenate(outs, axis=-1).astype(BF16)


def _mem_sample(q, mk, mv, *, bb=8):
    b, t, _ = q.shape
    blk = lambda i: (i, 0, 0)
    return pl.pallas_call(
        functools.partial(_mem_sample_kernel, bb=bb),
        out_shape=jax.ShapeDtypeStruct((b, t, MEM_DIM), BF16),
        grid=(b // bb,),
        in_specs=[
            pl.BlockSpec((bb, t, MEM_DIM), blk),
            pl.BlockSpec((bb, N_MEM * MEM_HEADS, MEM_HEAD_DIM), blk),
            pl.BlockSpec((bb, N_MEM * MEM_HEADS, MEM_HEAD_DIM), blk),
        ],
        out_specs=pl.BlockSpec((bb, t, MEM_DIM), blk),
        compiler_params=_cparams("parallel"),
        name="mem_sample",
    )(q, mk, mv)


def _head_sum(x, ones_ref):
    hi = x.astype(BF16)
    lo = (x - hi.astype(F32)).astype(BF16)
    return _dot(hi, ones_ref[...]) + _dot(lo, ones_ref[...])


def _rwkv_prep_kernel(x_ref, pre_ref, start_ref, mu_ref, w0_ref, ww2_ref, a0_ref, aw2_ref, gw2_ref, kk_ref, ka_ref,
                      rk_ref, ones_ref, r_o, w_o, k_o, v_o, kk_o, kka_o, g_o, bonus_o, *, seq, tm):
    x = x_ref[...]
    row = lax.broadcasted_iota(jnp.int32, x.shape, 0)
    shifted = pltpu.roll(x, 1, axis=0)
    if seq >= tm:
        is_start = (pl.program_id(0) * tm) % seq == 0
        first_prev = jnp.where(is_start, start_ref[0], pre_ref[SUBLANES - 1:SUBLANES, :])
        prev = jnp.where(row == 0, first_prev, shifted)
    else:
        prev = jnp.where(row % seq == 0, start_ref[...], shifted)
    xs = x + mu_ref[...] * (prev - x)
    d = RWKV_DIM
    r = xs[:, 0:d]
    k = xs[:, d:2 * d]
    v = xs[:, 2 * d:3 * d]
    lw = xs[:, 3 * d:3 * d + LORA_W]
    la = xs[:, 3 * d + LORA_W:3 * d + LORA_W + LORA_A]
    lg = xs[:, 3 * d + LORA_W + LORA_A:]
    wpre = w0_ref[...] + _dot_hi(jnp.tanh(lw), ww2_ref[...])
    w = -jax.nn.softplus(-wpre) - 0.5
    log_decay = -jnp.exp(w)
    a = jax.nn.sigmoid(a0_ref[...] + _dot_hi(la, aw2_ref[...]))
    g = _dot_hi(jax.nn.sigmoid(lg), gw2_ref[...])
    kk = k * kk_ref[...]
    kk = kk / jnp.maximum(jnp.sqrt(_head_sum(kk * kk, ones_ref)), 1e-12)
    kh = k * (1.0 + (a - 1.0) * ka_ref[...])
    r_o[...] = r
    w_o[...] = log_decay
    k_o[...] = kh
    v_o[...] = v
    kk_o[...] = kk
    kka_o[...] = kk * a
    g_o[...] = g
    bonus_o[...] = _head_sum(r * kh * rk_ref[...], ones_ref) * v


def _rwkv_prep(x, shift0, p, ones, *, seq, tm=256):
    m = x.shape[0]
    row = lambda i: (i, 0)
    const = lambda i: (0, 0)
    vec = lambda n: pl.BlockSpec((1, n), const)
    out = jax.ShapeDtypeStruct((m, RWKV_DIM), F32)
    if seq >= tm:
        assert seq % tm == 0
        start = shift0
        start_spec = pl.BlockSpec((1, 1, RWKV_IN), lambda i: ((i * tm) // seq, 0, 0))
    else:
        assert tm % seq == 0
        start = jnp.repeat(shift0[:, 0], seq, axis=0)
        start_spec = pl.BlockSpec((tm, RWKV_IN), row)
    pre_blocks = tm // SUBLANES
    return pl.pallas_call(
        functools.partial(_rwkv_prep_kernel, seq=seq, tm=tm),
        out_shape=(out,) * 8,
        grid=(m // tm,),
        in_specs=[
            pl.BlockSpec((tm, RWKV_IN), row),
            pl.BlockSpec((SUBLANES, RWKV_IN), lambda i: (jnp.maximum(i * pre_blocks - 1, 0), 0)),
            start_spec,
            vec(RWKV_IN), vec(RWKV_DIM),
            pl.BlockSpec((LORA_W, RWKV_DIM), const),
            vec(RWKV_DIM),
            pl.BlockSpec((LORA_A, RWKV_DIM), const),
            pl.BlockSpec((LORA_G, RWKV_DIM), const),
            vec(RWKV_DIM), vec(RWKV_DIM), vec(RWKV_DIM),
            pl.BlockSpec((RWKV_DIM, RWKV_DIM), const),
        ],
        out_specs=(pl.BlockSpec((tm, RWKV_DIM), row),) * 8,
        compiler_params=_cparams("parallel"),
        name="rwkv_prep",
    )(x, x, start, p["mu"], p["w0"], p["w_w2"], p["a0"], p["a_w2"], p["g_w2"], p["k_k"], p["k_a"], p["r_k"], ones)


def _half_sums(x, low):
    lo = jnp.sum(jnp.where(low, x, 0.0), axis=-1, keepdims=True)
    hi = jnp.sum(jnp.where(low, 0.0, x), axis=-1, keepdims=True)
    return jnp.where(low, lo, hi)


def _rwkv_scan_kernel(r_ref, w_ref, k_ref, v_ref, kk_ref, kka_ref, s0_ref, y_ref, so_ref, s_scr, *, bb, tt):
    jt = pl.program_id(1)
    group = min(tt, SUBLANES)
    n = RWKV_HEAD_DIM
    pairs = RWKV_HEADS // 2
    low = lax.broadcasted_iota(jnp.int32, (n, LANES), 1) < n
    diag = (lax.broadcasted_iota(jnp.int32, (n, LANES), 1) % n) == lax.broadcasted_iota(jnp.int32, (n, LANES), 0)

    @pl.when(jt == 0)
    def _():
        for b in range(bb):
            for h in range(RWKV_HEADS):
                s_scr[b, h // 2, :, (h % 2) * n:(h % 2 + 1) * n] = s0_ref[b, h]

    def batch_body(b, carry):
        def group_body(t0, states):
            toks = pl.ds(pl.multiple_of(t0 * group, group), group)
            new_states = []
            for p in range(pairs):
                sl = pl.ds(p * LANES, LANES)
                r, w, k, v = r_ref[b, toks, sl], jnp.exp(w_ref[b, toks, sl]), k_ref[b, toks, sl], v_ref[b, toks, sl]
                kk, kka = kk_ref[b, toks, sl], kka_ref[b, toks, sl]
                s = states[p]
                ys = []
                for j in range(group):
                    one = slice(j, j + 1)
                    sa = _half_sums(s * (-kk[one]), low)
                    vcol = _half_sums(jnp.where(diag, v[one], 0.0), low)
                    s = s * w[one] + sa * kka[one] + vcol * k[one]
                    ycol = _half_sums(s * r[one], low)
                    ys.append(jnp.sum(jnp.where(diag, ycol, 0.0), axis=0, keepdims=True))
                y_ref[b, toks, sl] = jnp.concatenate(ys, axis=0)
                new_states.append(s)
            return tuple(new_states)

        states = tuple(s_scr[b, p] for p in range(pairs))
        states = lax.fori_loop(0, tt // group, group_body, states)
        for p in range(pairs):
            s_scr[b, p] = states[p]
        return carry

    lax.fori_loop(0, bb, batch_body, 0)

    @pl.when(jt == pl.num_programs(1) - 1)
    def _():
        for b in range(bb):
            for h in range(RWKV_HEADS):
                so_ref[b, h] = s_scr[b, h // 2, :, (h % 2) * n:(h % 2 + 1) * n]


def _rwkv_scan(r, w, k, v, kk, kka, s0, *, bb, tt):
    b, t, d = r.shape
    n = RWKV_HEAD_DIM
    tok = pl.BlockSpec((bb, tt, d), lambda i, j: (i, j, 0))
    st = pl.BlockSpec((bb, RWKV_HEADS, n, n), lambda i, j: (i, 0, 0, 0))
    return pl.pallas_call(
        functools.partial(_rwkv_scan_kernel, bb=bb, tt=tt),
        out_shape=(jax.ShapeDtypeStruct((b, t, d), F32), jax.ShapeDtypeStruct((b, RWKV_HEADS, n, n), F32)),
        grid=(b // bb, t // tt),
        in_specs=[tok] * 6 + [st],
        out_specs=(tok, st),
        scratch_shapes=[pltpu.VMEM((bb, RWKV_HEADS // 2, n, LANES), F32)],
        compiler_params=_cparams("parallel", "arbitrary"),
        name="rwkv_scan",
    )(r, w, k, v, kk, kka, s0)


CHUNK = 64
GROUP_HEADS = 4
GROUP_W = GROUP_HEADS * RWKV_HEAD_DIM
N_GROUPS = RWKV_HEADS // GROUP_HEADS
(MASK_SAME, MASK_STRICT, MASK_INCL, MASK_LEVEL0) = (0, 1, 2, 3)
N_LEVELS = int(math.log2(CHUNK))


def _chunk_masks():
    i = np.arange(GROUP_W)
    same = (i[:, None] // CHUNK) == (i[None, :] // CHUNK)
    masks = [same, same & (i[None, :] < i[:, None]), same & (i[None, :] <= i[:, None])]
    for lvl in range(N_LEVELS):
        m = 1 << lvl
        masks.append(((i[:, None] // (2 * m)) == (i[None, :] // (2 * m))) & ((i[:, None] // m) != (i[None, :] // m))
                     & (i[None, :] < i[:, None]))
    return np.stack(masks).astype(np.float32)


def _rwkv_chunk_kernel(r_ref, lw_ref, k_ref, v_ref, kk_ref, kka_ref, st0_ref, tri_ref, eye_ref, mask_ref,
                       y_ref, sto_ref, st_scr):
    @pl.when(pl.program_id(0) == 0)
    def _():
        st_scr[...] = st0_ref[...]

    eye = eye_ref[...]
    eye_b = eye.astype(BF16)
    tri = tri_ref[...]
    tile_rows = lambda x: jnp.concatenate([x] * GROUP_HEADS, axis=0)
    block_diag = lambda x: (tile_rows(x) * mask_ref[MASK_SAME]).astype(BF16)

    n_chunks = y_ref.shape[0] // CHUNK
    chains = [(slice(c * CHUNK, (c + 1) * CHUNK), slice(g * GROUP_W, (g + 1) * GROUP_W))
              for c in range(n_chunks) for g in range(N_GROUPS)]
    each = lambda f, *cols: [f(*args) for args in zip(*cols)]
    same, strict, incl = mask_ref[MASK_SAME], mask_ref[MASK_STRICT], mask_ref[MASK_INCL]

    def cum_decay(lw):
        h1 = lw.astype(BF16)
        r1 = lw - h1.astype(F32)
        h2 = r1.astype(BF16)
        h3 = (r1 - h2.astype(F32)).astype(BF16)
        return _dot(tri, h1) + _dot(tri, h2) + _dot(tri, h3)

    lw = [lw_ref[rows, sl] for rows, sl in chains]
    kka = [kka_ref[rows, sl] for rows, sl in chains]
    k = [k_ref[rows, sl] for rows, sl in chains]
    cum = each(cum_decay, lw)
    cum_last = each(lambda c: c[CHUNK - 1:CHUNK, :], cum)
    p_inv = each(lambda c: jnp.exp(-c), cum)
    p_tail = each(lambda c, cl: jnp.exp(cl - c), cum, cum_last)
    a_bd = [block_diag(-kk_ref[rows, sl] * jnp.exp(c - l)) for (rows, sl), c, l in zip(chains, cum, lw)]
    r_f = [tile_rows(r_ref[rows, sl] * jnp.exp(c)) * same for (rows, sl), c in zip(chains, cum)]
    r_bd = each(lambda x: x.astype(BF16), r_f)
    v_bd = [block_diag(v_ref[rows, sl]) for rows, sl in chains]
    b_rep = each(lambda x, p: tile_rows((x * p).astype(BF16)), kka, p_inv)
    k_rep = each(lambda x, p: tile_rows((x * p).astype(BF16)), k, p_inv)
    bh_rep = each(lambda x, p: tile_rows((x * p).astype(BF16)), kka, p_tail)
    kh_rep = each(lambda x, p: tile_rows((x * p).astype(BF16)), k, p_tail)

    l_ab_f = each(lambda a, b: _dot_nt(a, b) * strict, a_bd, b_rep)
    l_ab = each(lambda x: x.astype(BF16), l_ab_f)
    l_ak = each(lambda a, b: (_dot_nt(a, b) * strict).astype(BF16), a_bd, k_rep)
    m_rb = each(lambda a, b: (_dot_nt(a, b) * incl).astype(BF16), r_bd, b_rep)
    m_rk = each(lambda a, b: (_dot_nt(a, b) * incl).astype(BF16), r_bd, k_rep)
    bh_t = each(lambda x: (_dot_nt(eye_b, x) * same).astype(BF16), bh_rep)
    kh_t = each(lambda x: (_dot_nt(eye_b, x) * same).astype(BF16), kh_rep)

    d = each(lambda l: eye + l * mask_ref[MASK_LEVEL0], l_ab_f)
    for lvl in range(1, N_LEVELS):
        d_b = each(lambda x: x.astype(BF16), d)
        x = each(lambda l, db: (_dot(l, db) * mask_ref[MASK_LEVEL0 + lvl]).astype(BF16), l_ab, d_b)
        d = each(lambda dd, db, xx: dd + _dot(db, xx), d, d_b, x)
    t_b = each(lambda x: x.astype(BF16), d)

    wm = each(lambda a, b, vv: _dot(jnp.concatenate([a, b], axis=0), vv), l_ak, m_rk, v_bd)
    twa = each(lambda t, w, a: _dot(t, jnp.concatenate([w[:GROUP_W].astype(BF16), a], axis=1)).astype(BF16),
               t_b, wm, a_bd)
    ry = each(_dot, m_rb, twa)
    mn = each(_dot, bh_t, twa)
    khv = each(_dot, kh_t, v_bd)
    y0 = each(lambda a, w: a[:, :GROUP_W] + w[GROUP_W:], ry, wm)
    n_x = each(lambda a, b: a[:, :GROUP_W] + b, mn, khv)
    mr = each(lambda a, cl, rf, b: jnp.concatenate(
        [(eye * jnp.exp(cl) + a[:, GROUP_W:]).astype(BF16), (rf + b[:, GROUP_W:]).astype(BF16)], axis=0),
        mn, cum_last, r_f, ry)

    st = [st_scr[g] for g in range(N_GROUPS)]
    for i, (rows, sl) in enumerate(chains):
        g = i % N_GROUPS
        ys = _dot(mr[i], st[g].astype(BF16))
        st[g] = ys[:GROUP_W] + n_x[i]
        y_bd = ys[GROUP_W:] + y0[i]
        y = y_bd[0:CHUNK]
        for h in range(1, GROUP_HEADS):
            y = y + y_bd[h * CHUNK:(h + 1) * CHUNK]
        y_ref[rows, sl] = y
    for g in range(N_GROUPS):
        st_scr[g] = st[g]

    @pl.when(pl.program_id(0) == pl.num_programs(0) - 1)
    def _():
        sto_ref[...] = st_scr[...]


def _rwkv_chunked(r, lw, k, v, kk, kka, s0, *, chunks_per_step=4):
    t, d = r.shape
    n = RWKV_HEAD_DIM
    tt = CHUNK * chunks_per_step
    assert CHUNK == n and t % tt == 0
    st0 = jnp.einsum("ghvk,hj->ghkjv", s0.reshape(N_GROUPS, GROUP_HEADS, n, n), jnp.eye(GROUP_HEADS, dtype=F32))
    st0 = st0.reshape(N_GROUPS, GROUP_W, GROUP_W)
    tri = jnp.asarray(np.tril(np.ones((CHUNK, CHUNK), np.float32)), BF16)
    eye = jnp.eye(GROUP_W, dtype=F32)
    masks = jnp.asarray(_chunk_masks())
    tok = pl.BlockSpec((tt, d), lambda c: (c, 0))
    st_spec = pl.BlockSpec((N_GROUPS, GROUP_W, GROUP_W), lambda c: (0, 0, 0))
    y, st = pl.pallas_call(
        _rwkv_chunk_kernel,
        out_shape=(jax.ShapeDtypeStruct((t, d), F32), jax.ShapeDtypeStruct((N_GROUPS, GROUP_W, GROUP_W), F32)),
        grid=(t // tt,),
        in_specs=[tok] * 6 + [st_spec, pl.BlockSpec((CHUNK, CHUNK), lambda c: (0, 0)),
                              pl.BlockSpec((GROUP_W, GROUP_W), lambda c: (0, 0)),
                              pl.BlockSpec(masks.shape, lambda c: (0, 0, 0))],
        out_specs=(tok, st_spec),
        scratch_shapes=[pltpu.VMEM((N_GROUPS, GROUP_W, GROUP_W), F32)],
        compiler_params=_cparams("arbitrary"),
        name="rwkv_chunk",
    )(r, lw, k, v, kk, kka, st0, tri, eye, masks)
    st5 = st.reshape(N_GROUPS, GROUP_HEADS, n, GROUP_HEADS, n)
    s_new = jnp.einsum("ghkjv,hj->ghvk", st5, jnp.eye(GROUP_HEADS, dtype=F32)).reshape(RWKV_HEADS, n, n)
    return y, s_new


def _rwkv_post_kernel(y_ref, bonus_ref, g_ref, lnw_ref, lnb_ref, ones_ref, o_ref):
    y = y_ref[...]
    inv_n = 1.0 / RWKV_HEAD_DIM
    mu = _head_sum(y, ones_ref) * inv_n
    dlt = y - mu
    var = _head_sum(dlt * dlt, ones_ref) * inv_n
    yn = dlt * lax.rsqrt(var + GN_EPS) * lnw_ref[...] + lnb_ref[...]
    o_ref[...] = ((yn + bonus_ref[...]) * g_ref[...]).astype(BF16)


def _rwkv_post(y, bonus, g, lnw, lnb, ones, *, tm=256):
    m = y.shape[0]
    row = lambda i: (i, 0)
    const = lambda i: (0, 0)
    tile = pl.BlockSpec((tm, RWKV_DIM), row)
    return pl.pallas_call(
        _rwkv_post_kernel,
        out_shape=jax.ShapeDtypeStruct((m, RWKV_DIM), BF16),
        grid=(m // tm,),
        in_specs=[tile, tile, tile, pl.BlockSpec((1, RWKV_DIM), const), pl.BlockSpec((1, RWKV_DIM), const),
                  pl.BlockSpec((RWKV_DIM, RWKV_DIM), const)],
        out_specs=tile,
        compiler_params=_cparams("parallel"),
        name="rwkv_post",
    )(y, bonus, g, lnw, lnb, ones)


def _merge_kernel(h_ref, g_ref, os_ref, or_ref, om_ref, wg0_ref, wg1_ref, wg2_ref, wos_ref, wor_ref, wom_ref,
                  wout_ref, o_ref, u_ref, acc_ref):
    j = pl.program_id(1)

    @pl.when(j == 0)
    def _():
        u_ref[...] = _rms(h_ref[...], g_ref[...]).astype(BF16)
        acc_ref[...] = jnp.zeros_like(acc_ref)

    u = u_ref[...]
    merged = jax.nn.sigmoid(_dot(u, wg0_ref[...])) * _dot(os_ref[...], wos_ref[...])
    merged += jax.nn.sigmoid(_dot(u, wg1_ref[...])) * _dot(or_ref[...], wor_ref[...])
    merged += jax.nn.sigmoid(_dot(u, wg2_ref[...])) * _dot(om_ref[...], wom_ref[...])
    acc_ref[...] += _dot(merged.astype(BF16), wout_ref[...])

    @pl.when(j == pl.num_programs(1) - 1)
    def _():
        o_ref[...] = h_ref[...] + acc_ref[...]


def _merge(h, g, o_swa, o_rw, o_mem, w_in, wo_swa, wo_rw, wo_mem, w_out, *, tm=512, tn=256):
    m, d = h.shape
    nt = d // tn
    assert PROJ_DIM % tn == 0
    g0 = PROJ_DIM // tn
    row = lambda i, j: (i, 0)
    col = lambda i, j: (0, j)
    return pl.pallas_call(
        _merge_kernel,
        out_shape=jax.ShapeDtypeStruct((m, d), F32),
        grid=(m // tm, nt),
        in_specs=[
            pl.BlockSpec((tm, d), row),
            pl.BlockSpec((1, d), lambda i, j: (0, 0)),
            pl.BlockSpec((tm, SWA_Q_DIM), row),
            pl.BlockSpec((tm, RWKV_DIM), row),
            pl.BlockSpec((tm, MEM_DIM), row),
            pl.BlockSpec((d, tn), lambda i, j: (0, g0 + j)),
            pl.BlockSpec((d, tn), lambda i, j: (0, g0 + j + nt)),
            pl.BlockSpec((d, tn), lambda i, j: (0, g0 + j + 2 * nt)),
            pl.BlockSpec((SWA_Q_DIM, tn), col),
            pl.BlockSpec((RWKV_DIM, tn), col),
            pl.BlockSpec((MEM_DIM, tn), col),
            pl.BlockSpec((tn, d), lambda i, j: (j, 0)),
        ],
        out_specs=pl.BlockSpec((tm, d), row),
        scratch_shapes=[pltpu.VMEM((tm, d), BF16), pltpu.VMEM((tm, d), F32)],
        compiler_params=_cparams("parallel", "arbitrary"),
        name="merge",
    )(h, g, o_swa, o_rw, o_mem, w_in, w_in, w_in, wo_swa, wo_rw, wo_mem, w_out)


def _t5_bucket(dist):
    max_exact = N_BUCKETS // 2
    d = np.maximum(dist, 0)
    log_ratio = (np.log(np.maximum(d, 1).astype(np.float32) / np.float32(max_exact))
                 / np.float32(math.log(MAX_DISTANCE / max_exact)))
    large = np.minimum(max_exact + (log_ratio * (N_BUCKETS - max_exact)).astype(np.int32), N_BUCKETS - 1)
    return np.where(d < max_exact, d, large).astype(np.int32)


def _rel_bias(table, dist):
    onehot = np.eye(N_BUCKETS, dtype=np.float32)[_t5_bucket(dist).reshape(-1)]
    bias = jnp.einsum("nb,bh->hn", jnp.asarray(onehot), table, precision=lax.Precision.HIGHEST)
    return bias.reshape(SWA_HEADS, *dist.shape)


def _q_perm():
    idx = np.arange(SWA_Q_DIM).reshape(SWA_KV_HEADS, SWA_GROUP, SWA_HEAD_DIM)
    return idx.transpose(1, 0, 2).reshape(-1)


def _rwkv_branch(xr, shift0, s0, p, ones, *, bb, tt):
    b, t, _ = xr.shape
    flat = lambda z: z.reshape(b * t, z.shape[-1])
    r, w, k, v, kk, kka, g, bonus = _rwkv_prep(flat(xr), shift0, p, ones, seq=t)
    if b == 1:
        y, s_new = _rwkv_chunked(r, w, k, v, kk, kka, s0[0])
        s_new = s_new[None]
    else:
        seq = lambda z: z.reshape(b, t, RWKV_DIM)
        y, s_new = _rwkv_scan(seq(r), seq(w), seq(k), seq(v), seq(kk), seq(kka), s0, bb=bb, tt=tt)
    o = _rwkv_post(flat(y), bonus, g, p["ln_w"], p["ln_b"], ones)
    return o, s_new


def kernel(x_prompt, mem_prompt, x_sample, cache_swa_k, cache_swa_v, state_rwkv, state_rwkv_shift, cache_mem_k, cache_mem_v, ffn1_norm, ffn1_wi, ffn1_wo, mix_norm, w_in, swa_sinks, rel_bias_table, rwkv_mu, rwkv_w0, rwkv_w_w2, rwkv_a0, rwkv_a_w2, rwkv_g_w2, rwkv_k_k, rwkv_k_a, rwkv_r_k, rwkv_ln_w, rwkv_ln_b, mem_norm, w_mem_kv, w_o_swa, w_o_rwkv, w_o_mem, w_out, ffn2_norm, ffn2_wi, ffn2_wo, final_norm):
    assert ffn1_wi.shape[0] == 1, "single-layer trunk"
    bp, tp, d = x_prompt.shape
    bs, ts, _ = x_sample.shape
    assert bp == 1
    row = lambda z: z.reshape(1, -1).astype(F32)

    qperm = _q_perm()
    w_in_b = w_in[0].astype(BF16)
    w_proj = jnp.concatenate([w_in_b[:, :SWA_Q_DIM][:, qperm], w_in_b[:, SWA_Q_DIM:PROJ_DIM]], axis=1)
    wo_swa = w_o_swa[0][qperm, :].astype(BF16)
    wo_rw = w_o_rwkv[0].astype(BF16)
    wo_mem = w_o_mem[0].astype(BF16)
    w_out_b = w_out[0].astype(BF16)
    wi1, wo1 = ffn1_wi[0].astype(BF16), ffn1_wo[0].astype(BF16)
    wi2, wo2 = ffn2_wi[0].astype(BF16), ffn2_wo[0].astype(BF16)
    g1, gm, g2, gf = row(ffn1_norm[0]), row(mix_norm[0]), row(ffn2_norm[0]), row(final_norm)
    rp = {
        "mu": row(rwkv_mu[0]), "w0": row(rwkv_w0[0]), "w_w2": rwkv_w_w2[0], "a0": row(rwkv_a0[0]),
        "a_w2": rwkv_a_w2[0], "g_w2": rwkv_g_w2[0], "k_k": row(rwkv_k_k[0]), "k_a": row(rwkv_k_a[0]),
        "r_k": row(rwkv_r_k[0]), "ln_w": row(rwkv_ln_w[0]), "ln_b": row(rwkv_ln_b[0]),
    }
    seg = np.arange(RWKV_DIM) // RWKV_HEAD_DIM
    ones = jnp.asarray(seg[:, None] == seg[None, :], dtype=BF16)
    sinks = swa_sinks[0].astype(F32)
    table = rel_bias_table.astype(F32)

    xp = x_prompt.reshape(tp, d)
    xs = x_sample.reshape(bs * ts, d)
    hp = _ffn(xp, g1, wi1, wo1, gf, final_norm=False)
    hs = _ffn(xs, g1, wi1, wo1, gf, final_norm=False)
    qp, kvp, xrp, qmp = _inproj(hp, gm, w_proj)
    qs, kvs, xrs, qms = _inproj(hs, gm, w_proj)

    w = WINDOW
    dist_p = np.arange(w)[:, None] + w - np.arange(2 * w)[None, :]
    bias_p = _rel_bias(table, dist_p).reshape(SWA_KV_HEADS, SWA_GROUP, w, 2 * w).transpose(1, 0, 2, 3)
    bias_p = bias_p.reshape(SWA_GROUP, SWA_KV_HEADS * w, 2 * w)
    sink_p = jnp.repeat(sinks.reshape(SWA_KV_HEADS, SWA_GROUP).T, w, axis=1).reshape(SWA_GROUP, SWA_KV_HEADS * w, 1)
    o_swa_p = _swa_prompt(qp, kvp, bias_p, sink_p)

    wbuf = cache_swa_k.shape[2]
    dist_s = np.arange(ts)[:, None] + wbuf - np.arange(wbuf + ts)[None, :]
    bias_s = _rel_bias(table, dist_s).reshape(SWA_HEADS * ts, wbuf + ts)
    sink_rows = jnp.repeat(sinks, ts).reshape(SWA_HEADS * ts, 1)
    qs_gt = qs.reshape(bs, ts, SWA_GROUP, SWA_KV_DIM).transpose(0, 2, 1, 3).reshape(bs, SWA_GROUP * ts, SWA_KV_DIM)
    kbuf = cache_swa_k[0].reshape(bs, wbuf, SWA_KV_DIM)
    vbuf = cache_swa_v[0].reshape(bs, wbuf, SWA_KV_DIM)
    o_swa_s = _swa_sample(qs_gt, kvs.reshape(bs, ts, 2 * SWA_KV_DIM), kbuf, vbuf,
                          bias_s[:, :wbuf], bias_s[:, wbuf:], sink_rows)
    o_swa_s = o_swa_s.reshape(bs, SWA_GROUP, ts, SWA_KV_DIM).transpose(0, 2, 1, 3).reshape(bs * ts, SWA_Q_DIM)

    zero_shift = jnp.zeros((bp, 1, RWKV_IN), F32)
    zero_state = jnp.zeros((bp, RWKV_HEADS, RWKV_HEAD_DIM, RWKV_HEAD_DIM), F32)
    o_rw_p, state_p = _rwkv_branch(xrp.reshape(bp, tp, RWKV_IN), zero_shift, zero_state, rp, ones,
                                   bb=1, tt=256)
    o_rw_s, state_s = _rwkv_branch(xrs.reshape(bs, ts, RWKV_IN), state_rwkv_shift[0], state_rwkv[0], rp, ones,
                                   bb=8, tt=ts)

    mkv = _norm_matmul(mem_prompt.reshape(N_MEM, d), row(mem_norm[0]), w_mem_kv[0].astype(BF16))
    o_mem_p = _mem_prompt(qmp, mkv)
    o_mem_s = _mem_sample(qms.reshape(bs, ts, MEM_DIM), cache_mem_k[0].reshape(bs, N_MEM * MEM_HEADS, MEM_HEAD_DIM),
                          cache_mem_v[0].reshape(bs, N_MEM * MEM_HEADS, MEM_HEAD_DIM)).reshape(bs * ts, MEM_DIM)

    hp = _merge(hp, gm, o_swa_p, o_rw_p, o_mem_p, w_in_b, wo_swa, wo_rw, wo_mem, w_out_b)
    hs = _merge(hs, gm, o_swa_s, o_rw_s, o_mem_s, w_in_b, wo_swa, wo_rw, wo_mem, w_out_b)
    y_prompt = _ffn(hp, g2, wi2, wo2, gf, final_norm=True).reshape(bp, tp, d)
    y_sample = _ffn(hs, g2, wi2, wo2, gf, final_norm=True).reshape(bs, ts, d)

    wp = min(w, tp)
    p_k = kvp[tp - wp:, :SWA_KV_DIM].reshape(1, bp, wp, SWA_KV_HEADS, SWA_HEAD_DIM)
    p_v = kvp[tp - wp:, SWA_KV_DIM:].reshape(1, bp, wp, SWA_KV_HEADS, SWA_HEAD_DIM)
    p_mk = mkv[:, :MEM_DIM].reshape(1, bp, N_MEM, MEM_HEADS, MEM_HEAD_DIM)
    p_mv = mkv[:, MEM_DIM:].reshape(1, bp, N_MEM, MEM_HEADS, MEM_HEAD_DIM)
    kvs3 = kvs.reshape(bs, ts, 2 * SWA_KV_DIM)
    s_k = jnp.concatenate([kbuf[:, ts:], kvs3[:, :, :SWA_KV_DIM]], axis=1).reshape(1, bs, wbuf, SWA_KV_HEADS, SWA_HEAD_DIM)
    s_v = jnp.concatenate([vbuf[:, ts:], kvs3[:, :, SWA_KV_DIM:]], axis=1).reshape(1, bs, wbuf, SWA_KV_HEADS, SWA_HEAD_DIM)
    return (y_prompt, y_sample,
            p_k, p_v, state_p[None], xrp[tp - 1:].reshape(1, bp, 1, RWKV_IN), p_mk, p_mv,
            s_k, s_v, state_s[None], xrs.reshape(bs, ts, RWKV_IN)[:, ts - 1:][None])
```

```python
import functools
import math

import jax
import jax.numpy as jnp
import numpy as np
from jax import lax
from jax.experimental import pallas as pl
from jax.experimental.pallas import tpu as pltpu

F32 = jnp.float32
BF16 = jnp.bfloat16

D_MODEL = 2048
D_FF = 5632
SWA_HEADS = 16
SWA_KV_HEADS = 4
SWA_GROUP = SWA_HEADS // SWA_KV_HEADS
SWA_HEAD_DIM = 64
SWA_Q_DIM = SWA_HEADS * SWA_HEAD_DIM
SWA_KV_DIM = SWA_KV_HEADS * SWA_HEAD_DIM
WINDOW = 128
N_BUCKETS = 32
MAX_DISTANCE = 128
RWKV_HEADS = 8
RWKV_HEAD_DIM = 64
RWKV_DIM = RWKV_HEADS * RWKV_HEAD_DIM
LORA_W = 64
LORA_A = 64
LORA_G = 128
RWKV_IN = 3 * RWKV_DIM + LORA_W + LORA_A + LORA_G
N_MEM = 256
MEM_HEADS = 4
MEM_HEAD_DIM = 128
MEM_DIM = MEM_HEADS * MEM_HEAD_DIM
N_BRANCH = 3
PROJ_DIM = SWA_Q_DIM + 2 * SWA_KV_DIM + RWKV_IN + MEM_DIM
NORM_EPS = 1e-6
GN_EPS = 64e-5
NEG_INF = -1e30

LANES = 128
SUBLANES = 8
VMEM_LIMIT = 56 * 1024 * 1024


def _cparams(*sem):
    return pltpu.CompilerParams(dimension_semantics=sem, vmem_limit_bytes=VMEM_LIMIT)


def _rms(x, g):
    return x * lax.rsqrt(jnp.mean(x * x, axis=-1, keepdims=True) + NORM_EPS) * g


def _dot(a, b):
    return jnp.dot(a, b, preferred_element_type=F32)


def _dot_nt(a, b):
    return lax.dot_general(a, b, (((1,), (1,)), ((), ())), preferred_element_type=F32)


def _dot_hi(a, b):
    return jnp.dot(a, b, preferred_element_type=F32, precision=lax.Precision.HIGHEST)


def _ffn_kernel(x_ref, g_ref, wg_ref, wu_ref, wo_ref, gf_ref, o_ref, xn_ref, acc_ref, *, final_norm):
    j = pl.program_id(1)

    @pl.when(j == 0)
    def _():
        xn_ref[...] = _rms(x_ref[...], g_ref[...]).astype(BF16)
        acc_ref[...] = jnp.zeros_like(acc_ref)

    xn = xn_ref[...]
    gate = _dot(xn, wg_ref[...])
    up = _dot(xn, wu_ref[...])
    act = (gate * jax.nn.sigmoid(gate)) * up
    acc_ref[...] += _dot(act.astype(BF16), wo_ref[...])

    @pl.when(j == pl.num_programs(1) - 1)
    def _():
        h = x_ref[...] + 0.5 * acc_ref[...]
        if final_norm:
            h = _rms(h, gf_ref[...])
        o_ref[...] = h


def _ffn(x, g, wi, wo, gf, *, final_norm, tm=512, tf=512):
    m, d = x.shape
    dff = wo.shape[0]
    nf = dff // tf
    return pl.pallas_call(
        functools.partial(_ffn_kernel, final_norm=final_norm),
        out_shape=jax.ShapeDtypeStruct((m, d), F32),
        grid=(m // tm, nf),
        in_specs=[
            pl.BlockSpec((tm, d), lambda i, j: (i, 0)),
            pl.BlockSpec((1, d), lambda i, j: (0, 0)),
            pl.BlockSpec((d, tf), lambda i, j: (0, j)),
            pl.BlockSpec((d, tf), lambda i, j: (0, j + nf)),
            pl.BlockSpec((tf, d), lambda i, j: (j, 0)),
            pl.BlockSpec((1, d), lambda i, j: (0, 0)),
        ],
        out_specs=pl.BlockSpec((tm, d), lambda i, j: (i, 0)),
        scratch_shapes=[pltpu.VMEM((tm, d), BF16), pltpu.VMEM((tm, d), F32)],
        compiler_params=_cparams("parallel", "arbitrary"),
        name="ffn_final" if final_norm else "ffn",
    )(x, g, wi, wi, wo, gf)


def _inproj_kernel(h_ref, g_ref, w_ref, q_ref, kv_ref, xr_ref, qm_ref):
    u = _rms(h_ref[...], g_ref[...]).astype(BF16)
    c0, c1, c2 = SWA_Q_DIM, SWA_Q_DIM + 2 * SWA_KV_DIM, SWA_Q_DIM + 2 * SWA_KV_DIM + RWKV_IN
    q_ref[...] = _dot(u, w_ref[:, 0:c0]).astype(BF16)
    kv_ref[...] = _dot(u, w_ref[:, c0:c1])
    xr_ref[...] = _dot(u, w_ref[:, c1:c2])
    qm_ref[...] = _dot(u, w_ref[:, c2:PROJ_DIM]).astype(BF16)


def _inproj(h, g, w, *, tm=256):
    m, d = h.shape
    row = lambda i: (i, 0)
    return pl.pallas_call(
        _inproj_kernel,
        out_shape=(
            jax.ShapeDtypeStruct((m, SWA_Q_DIM), BF16),
            jax.ShapeDtypeStruct((m, 2 * SWA_KV_DIM), F32),
            jax.ShapeDtypeStruct((m, RWKV_IN), F32),
            jax.ShapeDtypeStruct((m, MEM_DIM), BF16),
        ),
        grid=(m // tm,),
        in_specs=[
            pl.BlockSpec((tm, d), row),
            pl.BlockSpec((1, d), lambda i: (0, 0)),
            pl.BlockSpec((d, PROJ_DIM), lambda i: (0, 0), pipeline_mode=pl.Buffered(1)),
        ],
        out_specs=(
            pl.BlockSpec((tm, SWA_Q_DIM), row),
            pl.BlockSpec((tm, 2 * SWA_KV_DIM), row),
            pl.BlockSpec((tm, RWKV_IN), row),
            pl.BlockSpec((tm, MEM_DIM), row),
        ),
        compiler_params=_cparams("parallel"),
        name="inproj",
    )(h, g, w)


def _norm_matmul_kernel(x_ref, g_ref, w_ref, o_ref):
    o_ref[...] = _dot(_rms(x_ref[...], g_ref[...]).astype(BF16), w_ref[...])


def _norm_matmul(x, g, w, *, tn=512):
    m, d = x.shape
    n = w.shape[1]
    return pl.pallas_call(
        _norm_matmul_kernel,
        out_shape=jax.ShapeDtypeStruct((m, n), F32),
        grid=(n // tn,),
        in_specs=[
            pl.BlockSpec((m, d), lambda j: (0, 0)),
            pl.BlockSpec((1, d), lambda j: (0, 0)),
            pl.BlockSpec((d, tn), lambda j: (0, j)),
        ],
        out_specs=pl.BlockSpec((m, tn), lambda j: (0, j)),
        compiler_params=_cparams("parallel"),
        name="norm_matmul",
    )(x, g, w)


def _sink_softmax(logits, sink):
    m = jnp.maximum(jnp.max(logits, axis=-1, keepdims=True), sink)
    p = jnp.exp(logits - m)
    denom = jnp.sum(p, axis=-1, keepdims=True) + jnp.exp(sink - m)
    return p * (1.0 / denom)


def _swa_prompt_kernel(q_ref, kvc_ref, kvp_ref, bias_ref, sink_ref, o_ref):
    i = pl.program_id(0)
    w = WINDOW
    rows = SWA_KV_HEADS * w
    k = jnp.concatenate([kvp_ref[:, 0:SWA_KV_DIM], kvc_ref[:, 0:SWA_KV_DIM]], axis=0).astype(BF16)
    v = jnp.concatenate([kvp_ref[:, SWA_KV_DIM:], kvc_ref[:, SWA_KV_DIM:]], axis=0).astype(BF16)
    qpos = lax.broadcasted_iota(jnp.int32, (rows, 2 * w), 0) % w
    col = lax.broadcasted_iota(jnp.int32, (rows, 2 * w), 1)
    dist = qpos + w - col
    valid = (dist >= 0) & (dist < w) & ((col >= w) | (i > 0))
    lane_head = lax.broadcasted_iota(jnp.int32, (w, SWA_KV_DIM), 1) // SWA_HEAD_DIM
    scale = SWA_HEAD_DIM ** -0.5
    groups = range(SWA_GROUP)
    logits = []
    for g in groups:
        qg = q_ref[:, g * SWA_KV_DIM:(g + 1) * SWA_KV_DIM].astype(F32)
        qs = jnp.concatenate([jnp.where(lane_head == kvh, qg, 0.0) for kvh in range(SWA_KV_HEADS)], axis=0)
        lg = _dot_nt(qs.astype(BF16), k)
        logits.append(jnp.where(valid, lg * scale + bias_ref[g], NEG_INF))
    m = [jnp.maximum(jnp.max(logits[g], axis=-1, keepdims=True), sink_ref[g]) for g in groups]
    p = [jnp.exp(logits[g] - m[g]) for g in groups]
    inv = [1.0 / (jnp.sum(p[g], axis=-1, keepdims=True) + jnp.exp(sink_ref[g] - m[g])) for g in groups]
    ov = [_dot((p[g] * inv[g]).astype(BF16), v) for g in groups]
    for g in groups:
        og = jnp.zeros((w, SWA_KV_DIM), F32)
        for kvh in range(SWA_KV_HEADS):
            og = jnp.where(lane_head == kvh, ov[g][kvh * w:(kvh + 1) * w], og)
        o_ref[:, g * SWA_KV_DIM:(g + 1) * SWA_KV_DIM] = og.astype(BF16)


def _swa_prompt(q, kv, bias, sink_rows):
    t = q.shape[0]
    w = WINDOW
    rows = SWA_KV_HEADS * w
    return pl.pallas_call(
        _swa_prompt_kernel,
        out_shape=jax.ShapeDtypeStruct((t, SWA_Q_DIM), BF16),
        grid=(t // w,),
        in_specs=[
            pl.BlockSpec((w, SWA_Q_DIM), lambda i: (i, 0)),
            pl.BlockSpec((w, 2 * SWA_KV_DIM), lambda i: (i, 0)),
            pl.BlockSpec((w, 2 * SWA_KV_DIM), lambda i: (jnp.maximum(i - 1, 0), 0)),
            pl.BlockSpec((SWA_GROUP, rows, 2 * w), lambda i: (0, 0, 0)),
            pl.BlockSpec((SWA_GROUP, rows, 1), lambda i: (0, 0, 0)),
        ],
        out_specs=pl.BlockSpec((w, SWA_Q_DIM), lambda i: (i, 0)),
        compiler_params=_cparams("parallel"),
        name="swa_prompt",
    )(q, kv, kv, bias, sink_rows)


def _swa_sample_kernel(q_ref, kvn_ref, kb_ref, vb_ref, bias_b_ref, bias_n_ref, sink_ref, o_ref, *, bb, t):
    gt = SWA_GROUP * t
    rows = SWA_KV_HEADS * gt
    w = kb_ref.shape[1]
    scale = SWA_HEAD_DIM ** -0.5
    lane_head = lax.broadcasted_iota(jnp.int32, (gt, SWA_KV_DIM), 1) // SWA_HEAD_DIM
    tok = lax.broadcasted_iota(jnp.int32, (rows, w), 0) % t
    keyj = lax.broadcasted_iota(jnp.int32, (rows, w), 1)
    valid_b = (tok + w - keyj) < WINDOW
    tok_n = lax.broadcasted_iota(jnp.int32, (rows, 1), 0) % t
    sink = sink_ref[...]
    bs = range(bb)
    toks = range(t)
    qall = [jnp.concatenate([jnp.where(lane_head == kvh, q_ref[b].astype(F32), 0.0) for kvh in range(SWA_KV_HEADS)],
                            axis=0) for b in bs]
    kvn = [kvn_ref[b] for b in bs]
    lb = [_dot_nt(qall[b].astype(BF16), kb_ref[b].astype(BF16)) for b in bs]
    lb = [jnp.where(valid_b, lb[b] * scale + bias_b_ref[...], NEG_INF) for b in bs]
    ln = [[jnp.sum(qall[b] * kvn[b][j:j + 1, 0:SWA_KV_DIM], axis=-1, keepdims=True) for j in toks] for b in bs]
    ln = [[jnp.where(tok_n >= j, ln[b][j] * scale + bias_n_ref[:, j:j + 1], NEG_INF) for j in toks] for b in bs]
    m = [jnp.maximum(jnp.max(lb[b], axis=-1, keepdims=True), sink) for b in bs]
    m = [functools.reduce(jnp.maximum, ln[b], m[b]) for b in bs]
    pb = [jnp.exp(lb[b] - m[b]) for b in bs]
    pn = [[jnp.exp(ln[b][j] - m[b]) for j in toks] for b in bs]
    denom = [jnp.sum(pb[b], axis=-1, keepdims=True) + jnp.exp(sink - m[b]) for b in bs]
    inv = [1.0 / functools.reduce(jnp.add, pn[b], denom[b]) for b in bs]
    oall = [_dot((pb[b] * inv[b]).astype(BF16), vb_ref[b].astype(BF16)) for b in bs]
    for b in bs:
        ob = oall[b]
        for j in toks:
            ob = ob + (pn[b][j] * inv[b]) * kvn[b][j:j + 1, SWA_KV_DIM:]
        og = jnp.zeros((gt, SWA_KV_DIM), F32)
        for kvh in range(SWA_KV_HEADS):
            og = jnp.where(lane_head == kvh, ob[kvh * gt:(kvh + 1) * gt], og)
        o_ref[b] = og.astype(BF16)


def _swa_sample(q, kvn, kbuf, vbuf, bias_b, bias_n, sink_rows, *, bb=8):
    b, gt, _ = q.shape
    t = kvn.shape[1]
    w = kbuf.shape[1]
    rows = SWA_KV_HEADS * gt
    blk = lambda i: (i, 0, 0)
    const = lambda i: (0, 0)
    return pl.pallas_call(
        functools.partial(_swa_sample_kernel, bb=bb, t=t),
        out_shape=jax.ShapeDtypeStruct((b, gt, SWA_KV_DIM), BF16),
        grid=(b // bb,),
        in_specs=[
            pl.BlockSpec((bb, gt, SWA_KV_DIM), blk),
            pl.BlockSpec((bb, t, 2 * SWA_KV_DIM), blk),
            pl.BlockSpec((bb, w, SWA_KV_DIM), blk),
            pl.BlockSpec((bb, w, SWA_KV_DIM), blk),
            pl.BlockSpec((rows, w), const),
            pl.BlockSpec((rows, t), const),
            pl.BlockSpec((rows, 1), const),
        ],
        out_specs=pl.BlockSpec((bb, gt, SWA_KV_DIM), blk),
        compiler_params=_cparams("parallel"),
        name="swa_sample",
    )(q, kvn, kbuf, vbuf, bias_b, bias_n, sink_rows)


def _softmax_rows(x):
    m = jnp.max(x, axis=-1, keepdims=True)
    p = jnp.exp(x - m)
    return p * (1.0 / jnp.sum(p, axis=-1, keepdims=True))


def _mem_heads(q, mk, mv):
    scale = MEM_HEAD_DIM ** -0.5
    outs = []
    for h in range(MEM_HEADS):
        sl = slice(h * MEM_HEAD_DIM, (h + 1) * MEM_HEAD_DIM)
        p = _softmax_rows(_dot_nt(q[:, sl], mk[:, sl]) * scale)
        outs.append(_dot(p.astype(BF16), mv[:, sl]))
    return jnp.concatenate(outs, axis=-1)


def _mem_prompt_kernel(q_ref, mk_ref, mv_ref, o_ref):
    o_ref[...] = _mem_heads(q_ref[...], mk_ref[...].astype(BF16), mv_ref[...].astype(BF16)).astype(BF16)


def _mem_prompt(q, mkv, *, tm=512):
    m = q.shape[0]
    return pl.pallas_call(
        _mem_prompt_kernel,
        out_shape=jax.ShapeDtypeStruct((m, MEM_DIM), BF16),
        grid=(m // tm,),
        in_specs=[
            pl.BlockSpec((tm, MEM_DIM), lambda i: (i, 0)),
            pl.BlockSpec((N_MEM, MEM_DIM), lambda i: (0, 0)),
            pl.BlockSpec((N_MEM, MEM_DIM), lambda i: (0, 1)),
        ],
        out_specs=pl.BlockSpec((tm, MEM_DIM), lambda i: (i, 0)),
        compiler_params=_cparams("parallel"),
        name="mem_prompt",
    )(q, mkv, mkv)


def _mem_sample_kernel(q_ref, mk_ref, mv_ref, o_ref, *, bb):
    scale = MEM_HEAD_DIM ** -0.5
    pairs = [(b, h) for b in range(bb) for h in range(MEM_HEADS)]
    rows = lambda h: pl.ds(h, N_MEM, stride=MEM_HEADS)
    cols = lambda h: slice(h * MEM_HEAD_DIM, (h + 1) * MEM_HEAD_DIM)
    q = [q_ref[b] for b in range(bb)]
    x = [_dot_nt(q[b][:, cols(h)], mk_ref[b, rows(h), :].astype(BF16)) * scale for b, h in pairs]
    m = [jnp.max(xi, axis=-1, keepdims=True) for xi in x]
    p = [jnp.exp(xi - mi) for xi, mi in zip(x, m)]
    inv = [1.0 / jnp.sum(pi, axis=-1, keepdims=True) for pi in p]
    o = [_dot((pi * ii).astype(BF16), mv_ref[b, rows(h), :].astype(BF16)) for (b, h), pi, ii in zip(pairs, p, inv)]
    for b in range(bb):
        o_ref[b] = jnp.concatenate(o[b * MEM_HEADS:(b + 1) * MEM_HEADS], axis=-1).astype(BF16)


def _mem_sample(q, mk, mv, *, bb=8):
    b, t, _ = q.shape
    blk = lambda i: (i, 0, 0)
    return pl.pallas_call(
        functools.partial(_mem_sample_kernel, bb=bb),
        out_shape=jax.ShapeDtypeStruct((b, t, MEM_DIM), BF16),
        grid=(b // bb,),
        in_specs=[
            pl.BlockSpec((bb, t, MEM_DIM), blk),
            pl.BlockSpec((bb, N_MEM * MEM_HEADS, MEM_HEAD_DIM), blk),
            pl.BlockSpec((bb, N_MEM * MEM_HEADS, MEM_HEAD_DIM), blk),
        ],
        out_specs=pl.BlockSpec((bb, t, MEM_DIM), blk),
        compiler_params=_cparams("parallel"),
        name="mem_sample",
    )(q, mk, mv)


def _head_sum(x, ones_ref):
    hi = x.astype(BF16)
    lo = (x - hi.astype(F32)).astype(BF16)
    return _dot(hi, ones_ref[...]) + _dot(lo, ones_ref[...])


def _rwkv_prep_kernel(x_ref, pre_ref, start_ref, mu_ref, w0_ref, ww2_ref, a0_ref, aw2_ref, gw2_ref, kk_ref, ka_ref,
                      rk_ref, ones_ref, r_o, w_o, k_o, v_o, kk_o, kka_o, g_o, bonus_o, *, seq, tm):
    x = x_ref[...]
    row = lax.broadcasted_iota(jnp.int32, x.shape, 0)
    shifted = pltpu.roll(x, 1, axis=0)
    if seq >= tm:
        is_start = (pl.program_id(0) * tm) % seq == 0
        first_prev = jnp.where(is_start, start_ref[0], pre_ref[SUBLANES - 1:SUBLANES, :])
        prev = jnp.where(row == 0, first_prev, shifted)
    else:
        prev = jnp.where(row % seq == 0, start_ref[...], shifted)
    xs = x + mu_ref[...] * (prev - x)
    d = RWKV_DIM
    r = xs[:, 0:d]
    k = xs[:, d:2 * d]
    v = xs[:, 2 * d:3 * d]
    lw = xs[:, 3 * d:3 * d + LORA_W]
    la = xs[:, 3 * d + LORA_W:3 * d + LORA_W + LORA_A]
    lg = xs[:, 3 * d + LORA_W + LORA_A:]
    wpre = w0_ref[...] + _dot_hi(jnp.tanh(lw), ww2_ref[...])
    w = -jax.nn.softplus(-wpre) - 0.5
    log_decay = -jnp.exp(w)
    a = jax.nn.sigmoid(a0_ref[...] + _dot_hi(la, aw2_ref[...]))
    g = _dot_hi(jax.nn.sigmoid(lg), gw2_ref[...])
    kk = k * kk_ref[...]
    kk = kk / jnp.maximum(jnp.sqrt(_head_sum(kk * kk, ones_ref)), 1e-12)
    kh = k * (1.0 + (a - 1.0) * ka_ref[...])
    r_o[...] = r
    w_o[...] = log_decay
    k_o[...] = kh
    v_o[...] = v
    kk_o[...] = kk
    kka_o[...] = kk * a
    g_o[...] = g
    bonus_o[...] = _head_sum(r * kh * rk_ref[...], ones_ref) * v


def _rwkv_prep(x, shift0, p, ones, *, seq, tm=256):
    m = x.shape[0]
    row = lambda i: (i, 0)
    const = lambda i: (0, 0)
    vec = lambda n: pl.BlockSpec((1, n), const)
    out = jax.ShapeDtypeStruct((m, RWKV_DIM), F32)
    if seq >= tm:
        assert seq % tm == 0
        start = shift0
        start_spec = pl.BlockSpec((1, 1, RWKV_IN), lambda i: ((i * tm) // seq, 0, 0))
    else:
        assert tm % seq == 0
        start = jnp.repeat(shift0[:, 0], seq, axis=0)
        start_spec = pl.BlockSpec((tm, RWKV_IN), row)
    pre_blocks = tm // SUBLANES
    return pl.pallas_call(
        functools.partial(_rwkv_prep_kernel, seq=seq, tm=tm),
        out_shape=(out,) * 8,
        grid=(m // tm,),
        in_specs=[
            pl.BlockSpec((tm, RWKV_IN), row),
            pl.BlockSpec((SUBLANES, RWKV_IN), lambda i: (jnp.maximum(i * pre_blocks - 1, 0), 0)),
            start_spec,
            vec(RWKV_IN), vec(RWKV_DIM),
            pl.BlockSpec((LORA_W, RWKV_DIM), const),
            vec(RWKV_DIM),
            pl.BlockSpec((LORA_A, RWKV_DIM), const),
            pl.BlockSpec((LORA_G, RWKV_DIM), const),
            vec(RWKV_DIM), vec(RWKV_DIM), vec(RWKV_DIM),
            pl.BlockSpec((RWKV_DIM, RWKV_DIM), const),
        ],
        out_specs=(pl.BlockSpec((tm, RWKV_DIM), row),) * 8,
        compiler_params=_cparams("parallel"),
        name="rwkv_prep",
    )(x, x, start, p["mu"], p["w0"], p["w_w2"], p["a0"], p["a_w2"], p["g_w2"], p["k_k"], p["k_a"], p["r_k"], ones)


def _half_sums(x, low):
    lo = jnp.sum(jnp.where(low, x, 0.0), axis=-1, keepdims=True)
    hi = jnp.sum(jnp.where(low, 0.0, x), axis=-1, keepdims=True)
    return jnp.where(low, lo, hi)


def _rwkv_scan_kernel(r_ref, w_ref, k_ref, v_ref, kk_ref, kka_ref, s0_ref, y_ref, so_ref, s_scr, *, bb, tt):
    jt = pl.program_id(1)
    group = min(tt, SUBLANES)
    n = RWKV_HEAD_DIM
    pairs = RWKV_HEADS // 2
    low = lax.broadcasted_iota(jnp.int32, (n, LANES), 1) < n
    diag = (lax.broadcasted_iota(jnp.int32, (n, LANES), 1) % n) == lax.broadcasted_iota(jnp.int32, (n, LANES), 0)

    @pl.when(jt == 0)
    def _():
        for b in range(bb):
            for h in range(RWKV_HEADS):
                s_scr[b, h // 2, :, (h % 2) * n:(h % 2 + 1) * n] = s0_ref[b, h]

    nb = 2 if bb % 2 == 0 else 1
    chains = [(bi, p) for bi in range(nb) for p in range(pairs)]

    def batch_body(b0, carry):
        def group_body(t0, states):
            toks = pl.ds(pl.multiple_of(t0 * group, group), group)
            at = lambda ref: [ref[b0 * nb + bi, toks, pl.ds(p * LANES, LANES)] for bi, p in chains]
            r, k, v, kk, kka = at(r_ref), at(k_ref), at(v_ref), at(kk_ref), at(kka_ref)
            w = [jnp.exp(x) for x in at(w_ref)]
            s = list(states)
            ys = [[] for _ in chains]
            for j in range(group):
                one = slice(j, j + 1)
                sa = [_half_sums(s[c] * (-kk[c][one]), low) for c in range(len(chains))]
                vcol = [_half_sums(jnp.where(diag, v[c][one], 0.0), low) for c in range(len(chains))]
                s = [s[c] * w[c][one] + sa[c] * kka[c][one] + vcol[c] * k[c][one] for c in range(len(chains))]
                ycol = [_half_sums(s[c] * r[c][one], low) for c in range(len(chains))]
                for c in range(len(chains)):
                    ys[c].append(jnp.sum(jnp.where(diag, ycol[c], 0.0), axis=0, keepdims=True))
            for c, (bi, p) in enumerate(chains):
                y_ref[b0 * nb + bi, toks, pl.ds(p * LANES, LANES)] = jnp.concatenate(ys[c], axis=0)
            return tuple(s)

        states = tuple(s_scr[b0 * nb + bi, p] for bi, p in chains)
        states = lax.fori_loop(0, tt // group, group_body, states)
        for c, (bi, p) in enumerate(chains):
            s_scr[b0 * nb + bi, p] = states[c]
        return carry

    lax.fori_loop(0, bb // nb, batch_body, 0)

    @pl.when(jt == pl.num_programs(1) - 1)
    def _():
        for b in range(bb):
            for h in range(RWKV_HEADS):
                so_ref[b, h] = s_scr[b, h // 2, :, (h % 2) * n:(h % 2 + 1) * n]


def _rwkv_scan(r, w, k, v, kk, kka, s0, *, bb, tt):
    b, t, d = r.shape
    n = RWKV_HEAD_DIM
    tok = pl.BlockSpec((bb, tt, d), lambda i, j: (i, j, 0))
    st = pl.BlockSpec((bb, RWKV_HEADS, n, n), lambda i, j: (i, 0, 0, 0))
    return pl.pallas_call(
        functools.partial(_rwkv_scan_kernel, bb=bb, tt=tt),
        out_shape=(jax.ShapeDtypeStruct((b, t, d), F32), jax.ShapeDtypeStruct((b, RWKV_HEADS, n, n), F32)),
        grid=(b // bb, t // tt),
        in_specs=[tok] * 6 + [st],
        out_specs=(tok, st),
        scratch_shapes=[pltpu.VMEM((bb, RWKV_HEADS // 2, n, LANES), F32)],
        compiler_params=_cparams("parallel", "arbitrary"),
        name="rwkv_scan",
    )(r, w, k, v, kk, kka, s0)


CHUNK = 64
GROUP_HEADS = 4
GROUP_W = GROUP_HEADS * RWKV_HEAD_DIM
N_GROUPS = RWKV_HEADS // GROUP_HEADS
(MASK_SAME, MASK_STRICT, MASK_INCL, MASK_LEVEL0) = (0, 1, 2, 3)
N_LEVELS = int(math.log2(CHUNK))


def _chunk_masks():
    i = np.arange(GROUP_W)
    same = (i[:, None] // CHUNK) == (i[None, :] // CHUNK)
    masks = [same, same & (i[None, :] < i[:, None]), same & (i[None, :] <= i[:, None])]
    for lvl in range(N_LEVELS):
        m = 1 << lvl
        masks.append(((i[:, None] // (2 * m)) == (i[None, :] // (2 * m))) & ((i[:, None] // m) != (i[None, :] // m))
                     & (i[None, :] < i[:, None]))
    return np.stack(masks).astype(np.float32)


def _rwkv_chunk_kernel(r_ref, lw_ref, k_ref, v_ref, kk_ref, kka_ref, st0_ref, tri_ref, eye_ref, mask_ref,
                       y_ref, sto_ref, st_scr):
    @pl.when(pl.program_id(0) == 0)
    def _():
        st_scr[...] = st0_ref[...]

    eye = eye_ref[...]
    eye_b = eye.astype(BF16)
    tri = tri_ref[...]
    tile_rows = lambda x: jnp.concatenate([x] * GROUP_HEADS, axis=0)
    block_diag = lambda x: (tile_rows(x) * mask_ref[MASK_SAME]).astype(BF16)

    n_chunks = y_ref.shape[0] // CHUNK
    chains = [(slice(c * CHUNK, (c + 1) * CHUNK), slice(g * GROUP_W, (g + 1) * GROUP_W))
              for c in range(n_chunks) for g in range(N_GROUPS)]
    each = lambda f, *cols: [f(*args) for args in zip(*cols)]
    same, strict, incl = mask_ref[MASK_SAME], mask_ref[MASK_STRICT], mask_ref[MASK_INCL]

    def cum_decay(lw):
        h1 = lw.astype(BF16)
        r1 = lw - h1.astype(F32)
        h2 = r1.astype(BF16)
        h3 = (r1 - h2.astype(F32)).astype(BF16)
        return _dot(tri, h1) + _dot(tri, h2) + _dot(tri, h3)

    lw = [lw_ref[rows, sl] for rows, sl in chains]
    kka = [kka_ref[rows, sl] for rows, sl in chains]
    k = [k_ref[rows, sl] for rows, sl in chains]
    cum = each(cum_decay, lw)
    cum_last = each(lambda c: c[CHUNK - 1:CHUNK, :], cum)
    p_inv = each(lambda c: jnp.exp(-c), cum)
    p_tail = each(lambda c, cl: jnp.exp(cl - c), cum, cum_last)
    a_bd = [block_diag(-kk_ref[rows, sl] * jnp.exp(c - l)) for (rows, sl), c, l in zip(chains, cum, lw)]
    r_f = [tile_rows(r_ref[rows, sl] * jnp.exp(c)) * same for (rows, sl), c in zip(chains, cum)]
    r_bd = each(lambda x: x.astype(BF16), r_f)
    v_bd = [block_diag(v_ref[rows, sl]) for rows, sl in chains]
    b_rep = each(lambda x, p: tile_rows((x * p).astype(BF16)), kka, p_inv)
    k_rep = each(lambda x, p: tile_rows((x * p).astype(BF16)), k, p_inv)
    bh_rep = each(lambda x, p: tile_rows((x * p).astype(BF16)), kka, p_tail)
    kh_rep = each(lambda x, p: tile_rows((x * p).astype(BF16)), k, p_tail)

    l_ab_f = each(lambda a, b: _dot_nt(a, b) * strict, a_bd, b_rep)
    l_ab = each(lambda x: x.astype(BF16), l_ab_f)
    l_ak = each(lambda a, b: (_dot_nt(a, b) * strict).astype(BF16), a_bd, k_rep)
    m_rb = each(lambda a, b: (_dot_nt(a, b) * incl).astype(BF16), r_bd, b_rep)
    m_rk = each(lambda a, b: (_dot_nt(a, b) * incl).astype(BF16), r_bd, k_rep)
    bh_t = each(lambda x: (_dot_nt(eye_b, x) * same).astype(BF16), bh_rep)
    kh_t = each(lambda x: (_dot_nt(eye_b, x) * same).astype(BF16), kh_rep)

    d = each(lambda l: eye + l * mask_ref[MASK_LEVEL0], l_ab_f)
    for lvl in range(1, N_LEVELS):
        d_b = each(lambda x: x.astype(BF16), d)
        x = each(lambda l, db: (_dot(l, db) * mask_ref[MASK_LEVEL0 + lvl]).astype(BF16), l_ab, d_b)
        d = each(lambda dd, db, xx: dd + _dot(db, xx), d, d_b, x)
    t_b = each(lambda x: x.astype(BF16), d)

    wm = each(lambda a, b, vv: _dot(jnp.concatenate([a, b], axis=0), vv), l_ak, m_rk, v_bd)
    twa = each(lambda t, w, a: _dot(t, jnp.concatenate([w[:GROUP_W].astype(BF16), a], axis=1)).astype(BF16),
               t_b, wm, a_bd)
    ry = each(_dot, m_rb, twa)
    mn = each(_dot, bh_t, twa)
    khv = each(_dot, kh_t, v_bd)
    y0 = each(lambda a, w: a[:, :GROUP_W] + w[GROUP_W:], ry, wm)
    n_x = each(lambda a, b: a[:, :GROUP_W] + b, mn, khv)
    mr = each(lambda a, cl, rf, b: jnp.concatenate(
        [(eye * jnp.exp(cl) + a[:, GROUP_W:]).astype(BF16), (rf + b[:, GROUP_W:]).astype(BF16)], axis=0),
        mn, cum_last, r_f, ry)

    st = [st_scr[g] for g in range(N_GROUPS)]
    for i, (rows, sl) in enumerate(chains):
        g = i % N_GROUPS
        ys = _dot(mr[i], st[g].astype(BF16))
        st[g] = ys[:GROUP_W] + n_x[i]
        y_bd = ys[GROUP_W:] + y0[i]
        y = y_bd[0:CHUNK]
        for h in range(1, GROUP_HEADS):
            y = y + y_bd[h * CHUNK:(h + 1) * CHUNK]
        y_ref[rows, sl] = y
    for g in range(N_GROUPS):
        st_scr[g] = st[g]

    @pl.when(pl.program_id(0) == pl.num_programs(0) - 1)
    def _():
        sto_ref[...] = st_scr[...]


def _rwkv_chunked(r, lw, k, v, kk, kka, s0, *, chunks_per_step=4):
    t, d = r.shape
    n = RWKV_HEAD_DIM
    tt = CHUNK * chunks_per_step
    assert CHUNK == n and t % tt == 0
    st0 = jnp.einsum("ghvk,hj->ghkjv", s0.reshape(N_GROUPS, GROUP_HEADS, n, n), jnp.eye(GROUP_HEADS, dtype=F32))
    st0 = st0.reshape(N_GROUPS, GROUP_W, GROUP_W)
    tri = jnp.asarray(np.tril(np.ones((CHUNK, CHUNK), np.float32)), BF16)
    eye = jnp.eye(GROUP_W, dtype=F32)
    masks = jnp.asarray(_chunk_masks())
    tok = pl.BlockSpec((tt, d), lambda c: (c, 0))
    st_spec = pl.BlockSpec((N_GROUPS, GROUP_W, GROUP_W), lambda c: (0, 0, 0))
    y, st = pl.pallas_call(
        _rwkv_chunk_kernel,
        out_shape=(jax.ShapeDtypeStruct((t, d), F32), jax.ShapeDtypeStruct((N_GROUPS, GROUP_W, GROUP_W), F32)),
        grid=(t // tt,),
        in_specs=[tok] * 6 + [st_spec, pl.BlockSpec((CHUNK, CHUNK), lambda c: (0, 0)),
                              pl.BlockSpec((GROUP_W, GROUP_W), lambda c: (0, 0)),
                              pl.BlockSpec(masks.shape, lambda c: (0, 0, 0))],
        out_specs=(tok, st_spec),
        scratch_shapes=[pltpu.VMEM((N_GROUPS, GROUP_W, GROUP_W), F32)],
        compiler_params=_cparams("arbitrary"),
        name="rwkv_chunk",
    )(r, lw, k, v, kk, kka, st0, tri, eye, masks)
    st5 = st.reshape(N_GROUPS, GROUP_HEADS, n, GROUP_HEADS, n)
    s_new = jnp.einsum("ghkjv,hj->ghvk", st5, jnp.eye(GROUP_HEADS, dtype=F32)).reshape(RWKV_HEADS, n, n)
    return y, s_new


def _rwkv_post_kernel(y_ref, bonus_ref, g_ref, lnw_ref, lnb_ref, ones_ref, o_ref):
    y = y_ref[...]
    inv_n = 1.0 / RWKV_HEAD_DIM
    mu = _head_sum(y, ones_ref) * inv_n
    dlt = y - mu
    var = _head_sum(dlt * dlt, ones_ref) * inv_n
    yn = dlt * lax.rsqrt(var + GN_EPS) * lnw_ref[...] + lnb_ref[...]
    o_ref[...] = ((yn + bonus_ref[...]) * g_ref[...]).astype(BF16)


def _rwkv_post(y, bonus, g, lnw, lnb, ones, *, tm=256):
    m = y.shape[0]
    row = lambda i: (i, 0)
    const = lambda i: (0, 0)
    tile = pl.BlockSpec((tm, RWKV_DIM), row)
    return pl.pallas_call(
        _rwkv_post_kernel,
        out_shape=jax.ShapeDtypeStruct((m, RWKV_DIM), BF16),
        grid=(m // tm,),
        in_specs=[tile, tile, tile, pl.BlockSpec((1, RWKV_DIM), const), pl.BlockSpec((1, RWKV_DIM), const),
                  pl.BlockSpec((RWKV_DIM, RWKV_DIM), const)],
        out_specs=tile,
        compiler_params=_cparams("parallel"),
        name="rwkv_post",
    )(y, bonus, g, lnw, lnb, ones)


def _merge_kernel(h_ref, g_ref, os_ref, or_ref, om_ref, wg0_ref, wg1_ref, wg2_ref, wos_ref, wor_ref, wom_ref,
                  wout_ref, o_ref, u_ref, acc_ref):
    j = pl.program_id(1)

    @pl.when(j == 0)
    def _():
        u_ref[...] = _rms(h_ref[...], g_ref[...]).astype(BF16)
        acc_ref[...] = jnp.zeros_like(acc_ref)

    u = u_ref[...]
    merged = jax.nn.sigmoid(_dot(u, wg0_ref[...])) * _dot(os_ref[...], wos_ref[...])
    merged += jax.nn.sigmoid(_dot(u, wg1_ref[...])) * _dot(or_ref[...], wor_ref[...])
    merged += jax.nn.sigmoid(_dot(u, wg2_ref[...])) * _dot(om_ref[...], wom_ref[...])
    acc_ref[...] += _dot(merged.astype(BF16), wout_ref[...])

    @pl.when(j == pl.num_programs(1) - 1)
    def _():
        o_ref[...] = h_ref[...] + acc_ref[...]


def _merge(h, g, o_swa, o_rw, o_mem, w_in, wo_swa, wo_rw, wo_mem, w_out, *, tm=512, tn=512):
    m, d = h.shape
    nt = d // tn
    g0 = 0
    row = lambda i, j: (i, 0)
    col = lambda i, j: (0, j)
    return pl.pallas_call(
        _merge_kernel,
        out_shape=jax.ShapeDtypeStruct((m, d), F32),
        grid=(m // tm, nt),
        in_specs=[
            pl.BlockSpec((tm, d), row),
            pl.BlockSpec((1, d), lambda i, j: (0, 0)),
            pl.BlockSpec((tm, SWA_Q_DIM), row),
            pl.BlockSpec((tm, RWKV_DIM), row),
            pl.BlockSpec((tm, MEM_DIM), row),
            pl.BlockSpec((d, tn), lambda i, j: (0, g0 + j)),
            pl.BlockSpec((d, tn), lambda i, j: (0, g0 + j + nt)),
            pl.BlockSpec((d, tn), lambda i, j: (0, g0 + j + 2 * nt)),
            pl.BlockSpec((SWA_Q_DIM, tn), col),
            pl.BlockSpec((RWKV_DIM, tn), col),
            pl.BlockSpec((MEM_DIM, tn), col),
            pl.BlockSpec((tn, d), lambda i, j: (j, 0)),
        ],
        out_specs=pl.BlockSpec((tm, d), row),
        scratch_shapes=[pltpu.VMEM((tm, d), BF16), pltpu.VMEM((tm, d), F32)],
        compiler_params=_cparams("parallel", "arbitrary"),
        name="merge",
    )(h, g, o_swa, o_rw, o_mem, w_in, w_in, w_in, wo_swa, wo_rw, wo_mem, w_out)


def _t5_bucket(dist):
    max_exact = N_BUCKETS // 2
    d = np.maximum(dist, 0)
    log_ratio = (np.log(np.maximum(d, 1).astype(np.float32) / np.float32(max_exact))
                 / np.float32(math.log(MAX_DISTANCE / max_exact)))
    large = np.minimum(max_exact + (log_ratio * (N_BUCKETS - max_exact)).astype(np.int32), N_BUCKETS - 1)
    return np.where(d < max_exact, d, large).astype(np.int32)


def _rel_bias(table, dist):
    onehot = np.eye(N_BUCKETS, dtype=np.float32)[_t5_bucket(dist).reshape(-1)]
    bias = jnp.einsum("nb,bh->hn", jnp.asarray(onehot), table, precision=lax.Precision.HIGHEST)
    return bias.reshape(SWA_HEADS, *dist.shape)


def _q_perm():
    idx = np.arange(SWA_Q_DIM).reshape(SWA_KV_HEADS, SWA_GROUP, SWA_HEAD_DIM)
    return idx.transpose(1, 0, 2).reshape(-1)


def _rwkv_branch(xr, shift0, s0, p, ones, *, bb, tt):
    b, t, _ = xr.shape
    flat = lambda z: z.reshape(b * t, z.shape[-1])
    r, w, k, v, kk, kka, g, bonus = _rwkv_prep(flat(xr), shift0, p, ones, seq=t)
    if b == 1:
        y, s_new = _rwkv_chunked(r, w, k, v, kk, kka, s0[0])
        s_new = s_new[None]
    else:
        seq = lambda z: z.reshape(b, t, RWKV_DIM)
        y, s_new = _rwkv_scan(seq(r), seq(w), seq(k), seq(v), seq(kk), seq(kka), s0, bb=bb, tt=tt)
    o = _rwkv_post(flat(y), bonus, g, p["ln_w"], p["ln_b"], ones)
    return o, s_new


def kernel(x_prompt, mem_prompt, x_sample, cache_swa_k, cache_swa_v, state_rwkv, state_rwkv_shift, cache_mem_k, cache_mem_v, ffn1_norm, ffn1_wi, ffn1_wo, mix_norm, w_in, swa_sinks, rel_bias_table, rwkv_mu, rwkv_w0, rwkv_w_w2, rwkv_a0, rwkv_a_w2, rwkv_g_w2, rwkv_k_k, rwkv_k_a, rwkv_r_k, rwkv_ln_w, rwkv_ln_b, mem_norm, w_mem_kv, w_o_swa, w_o_rwkv, w_o_mem, w_out, ffn2_norm, ffn2_wi, ffn2_wo, final_norm):
    assert ffn1_wi.shape[0] == 1, "single-layer trunk"
    bp, tp, d = x_prompt.shape
    bs, ts, _ = x_sample.shape
    assert bp == 1
    row = lambda z: z.reshape(1, -1).astype(F32)

    qperm = _q_perm()
    w_in_b = jnp.concatenate([w_in[0][:, PROJ_DIM:], w_in[0][:, :PROJ_DIM]], axis=1).astype(BF16)
    n_gate = N_BRANCH * d
    w_proj = jnp.concatenate([w_in_b[:, n_gate:n_gate + SWA_Q_DIM][:, qperm], w_in_b[:, n_gate + SWA_Q_DIM:]], axis=1)
    wo_swa = w_o_swa[0][qperm, :].astype(BF16)
    wo_rw = w_o_rwkv[0].astype(BF16)
    wo_mem = w_o_mem[0].astype(BF16)
    w_out_b = w_out[0].astype(BF16)
    wi1, wo1 = ffn1_wi[0].astype(BF16), ffn1_wo[0].astype(BF16)
    wi2, wo2 = ffn2_wi[0].astype(BF16), ffn2_wo[0].astype(BF16)
    g1, gm, g2, gf = row(ffn1_norm[0]), row(mix_norm[0]), row(ffn2_norm[0]), row(final_norm)
    rp = {
        "mu": row(rwkv_mu[0]), "w0": row(rwkv_w0[0]), "w_w2": rwkv_w_w2[0], "a0": row(rwkv_a0[0]),
        "a_w2": rwkv_a_w2[0], "g_w2": rwkv_g_w2[0], "k_k": row(rwkv_k_k[0]), "k_a": row(rwkv_k_a[0]),
        "r_k": row(rwkv_r_k[0]), "ln_w": row(rwkv_ln_w[0]), "ln_b": row(rwkv_ln_b[0]),
    }
    seg = np.arange(RWKV_DIM) // RWKV_HEAD_DIM
    ones = jnp.asarray(seg[:, None] == seg[None, :], dtype=BF16)
    sinks = swa_sinks[0].astype(F32)
    table = rel_bias_table.astype(F32)

    xp = x_prompt.reshape(tp, d)
    xs = x_sample.reshape(bs * ts, d)
    hp = _ffn(xp, g1, wi1, wo1, gf, final_norm=False)
    hs = _ffn(xs, g1, wi1, wo1, gf, final_norm=False)
    qp, kvp, xrp, qmp = _inproj(hp, gm, w_proj)
    qs, kvs, xrs, qms = _inproj(hs, gm, w_proj)

    w = WINDOW
    dist_p = np.arange(w)[:, None] + w - np.arange(2 * w)[None, :]
    bias_p = _rel_bias(table, dist_p).reshape(SWA_KV_HEADS, SWA_GROUP, w, 2 * w).transpose(1, 0, 2, 3)
    bias_p = bias_p.reshape(SWA_GROUP, SWA_KV_HEADS * w, 2 * w)
    sink_p = jnp.repeat(sinks.reshape(SWA_KV_HEADS, SWA_GROUP).T, w, axis=1).reshape(SWA_GROUP, SWA_KV_HEADS * w, 1)
    o_swa_p = _swa_prompt(qp, kvp, bias_p, sink_p)

    wbuf = cache_swa_k.shape[2]
    dist_s = np.arange(ts)[:, None] + wbuf - np.arange(wbuf + ts)[None, :]
    bias_s = _rel_bias(table, dist_s).reshape(SWA_HEADS * ts, wbuf + ts)
    sink_rows = jnp.repeat(sinks, ts).reshape(SWA_HEADS * ts, 1)
    qs_gt = qs.reshape(bs, ts, SWA_GROUP, SWA_KV_DIM).transpose(0, 2, 1, 3).reshape(bs, SWA_GROUP * ts, SWA_KV_DIM)
    kbuf = cache_swa_k[0].reshape(bs, wbuf, SWA_KV_DIM)
    vbuf = cache_swa_v[0].reshape(bs, wbuf, SWA_KV_DIM)
    o_swa_s = _swa_sample(qs_gt, kvs.reshape(bs, ts, 2 * SWA_KV_DIM), kbuf, vbuf,
                          bias_s[:, :wbuf], bias_s[:, wbuf:], sink_rows)
    o_swa_s = o_swa_s.reshape(bs, SWA_GROUP, ts, SWA_KV_DIM).transpose(0, 2, 1, 3).reshape(bs * ts, SWA_Q_DIM)

    zero_shift = jnp.zeros((bp, 1, RWKV_IN), F32)
    zero_state = jnp.zeros((bp, RWKV_HEADS, RWKV_HEAD_DIM, RWKV_HEAD_DIM), F32)
    o_rw_p, state_p = _rwkv_branch(xrp.reshape(bp, tp, RWKV_IN), zero_shift, zero_state, rp, ones,
                                   bb=1, tt=256)
    o_rw_s, state_s = _rwkv_branch(xrs.reshape(bs, ts, RWKV_IN), state_rwkv_shift[0], state_rwkv[0], rp, ones,
                                   bb=8, tt=ts)

    mkv = _norm_matmul(mem_prompt.reshape(N_MEM, d), row(mem_norm[0]), w_mem_kv[0].astype(BF16))
    o_mem_p = _mem_prompt(qmp, mkv)
    o_mem_s = _mem_sample(qms.reshape(bs, ts, MEM_DIM), cache_mem_k[0].reshape(bs, N_MEM * MEM_HEADS, MEM_HEAD_DIM),
                          cache_mem_v[0].reshape(bs, N_MEM * MEM_HEADS, MEM_HEAD_DIM)).reshape(bs * ts, MEM_DIM)

    hp = _merge(hp, gm, o_swa_p, o_rw_p, o_mem_p, w_in_b, wo_swa, wo_rw, wo_mem, w_out_b)
    hs = _merge(hs, gm, o_swa_s, o_rw_s, o_mem_s, w_in_b, wo_swa, wo_rw, wo_mem, w_out_b)
    y_prompt = _ffn(hp, g2, wi2, wo2, gf, final_norm=True).reshape(bp, tp, d)
    y_sample = _ffn(hs, g2, wi2, wo2, gf, final_norm=True).reshape(bs, ts, d)

    wp = min(w, tp)
    p_k = kvp[tp - wp:, :SWA_KV_DIM].reshape(1, bp, wp, SWA_KV_HEADS, SWA_HEAD_DIM)
    p_v = kvp[tp - wp:, SWA_KV_DIM:].reshape(1, bp, wp, SWA_KV_HEADS, SWA_HEAD_DIM)
    p_mk = mkv[:, :MEM_DIM].reshape(1, bp, N_MEM, MEM_HEADS, MEM_HEAD_DIM)
    p_mv = mkv[:, MEM_DIM:].reshape(1, bp, N_MEM, MEM_HEADS, MEM_HEAD_DIM)
    kvs3 = kvs.reshape(bs, ts, 2 * SWA_KV_DIM)
    s_k = jnp.concatenate([kbuf[:, ts:], kvs3[:, :, :SWA_KV_DIM]], axis=1).reshape(1, bs, wbuf, SWA_KV_HEADS, SWA_HEAD_DIM)
    s_v = jnp.concatenate([vbuf[:, ts:], kvs3[:, :, SWA_KV_DIM:]], axis=1).reshape(1, bs, wbuf, SWA_KV_HEADS, SWA_HEAD_DIM)
    return (y_prompt, y_sample,
            p_k, p_v, state_p[None], xrp[tp - 1:].reshape(1, bp, 1, RWKV_IN), p_mk, p_mv,
            s_k, s_v, state_s[None], xrs.reshape(bs, ts, RWKV_IN)[:, ts - 1:][None])
```

```python
import functools
import math

import jax
import jax.numpy as jnp
import numpy as np
from jax import lax
from jax.experimental import pallas as pl
from jax.experimental.pallas import tpu as pltpu

F32 = jnp.float32
BF16 = jnp.bfloat16

D_MODEL = 2048
D_FF = 5632
SWA_HEADS = 16
SWA_KV_HEADS = 4
SWA_GROUP = SWA_HEADS // SWA_KV_HEADS
SWA_HEAD_DIM = 64
SWA_Q_DIM = SWA_HEADS * SWA_HEAD_DIM
SWA_KV_DIM = SWA_KV_HEADS * SWA_HEAD_DIM
WINDOW = 128
N_BUCKETS = 32
MAX_DISTANCE = 128
RWKV_HEADS = 8
RWKV_HEAD_DIM = 64
RWKV_DIM = RWKV_HEADS * RWKV_HEAD_DIM
LORA_W = 64
LORA_A = 64
LORA_G = 128
RWKV_IN = 3 * RWKV_DIM + LORA_W + LORA_A + LORA_G
N_MEM = 256
MEM_HEADS = 4
MEM_HEAD_DIM = 128
MEM_DIM = MEM_HEADS * MEM_HEAD_DIM
N_BRANCH = 3
PROJ_DIM = SWA_Q_DIM + 2 * SWA_KV_DIM + RWKV_IN + MEM_DIM
NORM_EPS = 1e-6
GN_EPS = 64e-5
NEG_INF = -1e30

LANES = 128
SUBLANES = 8
VMEM_LIMIT = 56 * 1024 * 1024


def _cparams(*sem):
    return pltpu.CompilerParams(dimension_semantics=sem, vmem_limit_bytes=VMEM_LIMIT)


def _rms(x, g):
    return x * lax.rsqrt(jnp.mean(x * x, axis=-1, keepdims=True) + NORM_EPS) * g


def _dot(a, b):
    return jnp.dot(a, b, preferred_element_type=F32)


def _dot_nt(a, b):
    return lax.dot_general(a, b, (((1,), (1,)), ((), ())), preferred_element_type=F32)


def _dot_hi(a, b):
    return jnp.dot(a, b, preferred_element_type=F32, precision=lax.Precision.HIGHEST)


def _ffn_kernel(x_ref, g_ref, wg_ref, wu_ref, wo_ref, gf_ref, o_ref, xn_ref, acc_ref, *, final_norm):
    j = pl.program_id(1)

    @pl.when(j == 0)
    def _():
        xn_ref[...] = _rms(x_ref[...], g_ref[...]).astype(BF16)
        acc_ref[...] = jnp.zeros_like(acc_ref)

    xn = xn_ref[...]
    gate = _dot(xn, wg_ref[...])
    up = _dot(xn, wu_ref[...])
    act = (gate * jax.nn.sigmoid(gate)) * up
    acc_ref[...] += _dot(act.astype(BF16), wo_ref[...])

    @pl.when(j == pl.num_programs(1) - 1)
    def _():
        h = x_ref[...] + 0.5 * acc_ref[...]
        if final_norm:
            h = _rms(h, gf_ref[...])
        o_ref[...] = h


def _ffn(x, g, wi, wo, gf, *, final_norm, tm=512, tf=512):
    m, d = x.shape
    dff = wo.shape[0]
    nf = dff // tf
    return pl.pallas_call(
        functools.partial(_ffn_kernel, final_norm=final_norm),
        out_shape=jax.ShapeDtypeStruct((m, d), F32),
        grid=(m // tm, nf),
        in_specs=[
            pl.BlockSpec((tm, d), lambda i, j: (i, 0)),
            pl.BlockSpec((1, d), lambda i, j: (0, 0)),
            pl.BlockSpec((d, tf), lambda i, j: (0, j)),
            pl.BlockSpec((d, tf), lambda i, j: (0, j + nf)),
            pl.BlockSpec((tf, d), lambda i, j: (j, 0)),
            pl.BlockSpec((1, d), lambda i, j: (0, 0)),
        ],
        out_specs=pl.BlockSpec((tm, d), lambda i, j: (i, 0)),
        scratch_shapes=[pltpu.VMEM((tm, d), BF16), pltpu.VMEM((tm, d), F32)],
        compiler_params=_cparams("parallel", "arbitrary"),
        name="ffn_final" if final_norm else "ffn",
    )(x, g, wi, wi, wo, gf)


def _inproj_kernel(h_ref, g_ref, wq_ref, w_ref, q_ref, kv_ref, xr_ref, qm_ref):
    u = _rms(h_ref[...], g_ref[...]).astype(BF16)
    c0, c1, c2 = SWA_Q_DIM, SWA_Q_DIM + 2 * SWA_KV_DIM, SWA_Q_DIM + 2 * SWA_KV_DIM + RWKV_IN
    q_ref[...] = _dot(u, wq_ref[...]).astype(BF16)
    kv_ref[...] = _dot(u, w_ref[:, c0:c1])
    xr_ref[...] = _dot(u, w_ref[:, c1:c2])
    qm_ref[...] = _dot(u, w_ref[:, c2:PROJ_DIM]).astype(BF16)


def _inproj(h, g, wq, w, *, tm=256):
    m, d = h.shape
    row = lambda i: (i, 0)
    return pl.pallas_call(
        _inproj_kernel,
        out_shape=(
            jax.ShapeDtypeStruct((m, SWA_Q_DIM), BF16),
            jax.ShapeDtypeStruct((m, 2 * SWA_KV_DIM), F32),
            jax.ShapeDtypeStruct((m, RWKV_IN), F32),
            jax.ShapeDtypeStruct((m, MEM_DIM), BF16),
        ),
        grid=(m // tm,),
        in_specs=[
            pl.BlockSpec((tm, d), row),
            pl.BlockSpec((1, d), lambda i: (0, 0)),
            pl.BlockSpec((d, SWA_Q_DIM), lambda i: (0, 0), pipeline_mode=pl.Buffered(1)),
            pl.BlockSpec((d, PROJ_DIM), lambda i: (0, 0), pipeline_mode=pl.Buffered(1)),
        ],
        out_specs=(
            pl.BlockSpec((tm, SWA_Q_DIM), row),
            pl.BlockSpec((tm, 2 * SWA_KV_DIM), row),
            pl.BlockSpec((tm, RWKV_IN), row),
            pl.BlockSpec((tm, MEM_DIM), row),
        ),
        compiler_params=_cparams("parallel"),
        name="inproj",
    )(h, g, wq, w)


def _norm_matmul_kernel(x_ref, g_ref, w_ref, o_ref):
    o_ref[...] = _dot(_rms(x_ref[...], g_ref[...]).astype(BF16), w_ref[...])


def _norm_matmul(x, g, w, *, tn=512):
    m, d = x.shape
    n = w.shape[1]
    return pl.pallas_call(
        _norm_matmul_kernel,
        out_shape=jax.ShapeDtypeStruct((m, n), F32),
        grid=(n // tn,),
        in_specs=[
            pl.BlockSpec((m, d), lambda j: (0, 0)),
            pl.BlockSpec((1, d), lambda j: (0, 0)),
            pl.BlockSpec((d, tn), lambda j: (0, j)),
        ],
        out_specs=pl.BlockSpec((m, tn), lambda j: (0, j)),
        compiler_params=_cparams("parallel"),
        name="norm_matmul",
    )(x, g, w)


def _sink_softmax(logits, sink):
    m = jnp.maximum(jnp.max(logits, axis=-1, keepdims=True), sink)
    p = jnp.exp(logits - m)
    denom = jnp.sum(p, axis=-1, keepdims=True) + jnp.exp(sink - m)
    return p * (1.0 / denom)


def _swa_prompt_kernel(q_ref, kvc_ref, kvp_ref, bias_ref, sink_ref, o_ref):
    i = pl.program_id(0)
    w = WINDOW
    rows = SWA_KV_HEADS * w
    k = jnp.concatenate([kvp_ref[:, 0:SWA_KV_DIM], kvc_ref[:, 0:SWA_KV_DIM]], axis=0).astype(BF16)
    v = jnp.concatenate([kvp_ref[:, SWA_KV_DIM:], kvc_ref[:, SWA_KV_DIM:]], axis=0).astype(BF16)
    qpos = lax.broadcasted_iota(jnp.int32, (rows, 2 * w), 0) % w
    col = lax.broadcasted_iota(jnp.int32, (rows, 2 * w), 1)
    dist = qpos + w - col
    valid = (dist >= 0) & (dist < w) & ((col >= w) | (i > 0))
    lane_head = lax.broadcasted_iota(jnp.int32, (w, SWA_KV_DIM), 1) // SWA_HEAD_DIM
    scale = SWA_HEAD_DIM ** -0.5
    groups = range(SWA_GROUP)
    logits = []
    for g in groups:
        qg = q_ref[:, g * SWA_KV_DIM:(g + 1) * SWA_KV_DIM].astype(F32)
        qs = jnp.concatenate([jnp.where(lane_head == kvh, qg, 0.0) for kvh in range(SWA_KV_HEADS)], axis=0)
        lg = _dot_nt(qs.astype(BF16), k)
        logits.append(jnp.where(valid, lg * scale + bias_ref[g], NEG_INF))
    m = [jnp.maximum(jnp.max(logits[g], axis=-1, keepdims=True), sink_ref[g]) for g in groups]
    p = [jnp.exp(logits[g] - m[g]) for g in groups]
    inv = [1.0 / (jnp.sum(p[g], axis=-1, keepdims=True) + jnp.exp(sink_ref[g] - m[g])) for g in groups]
    ov = [_dot((p[g] * inv[g]).astype(BF16), v) for g in groups]
    for g in groups:
        og = jnp.zeros((w, SWA_KV_DIM), F32)
        for kvh in range(SWA_KV_HEADS):
            og = jnp.where(lane_head == kvh, ov[g][kvh * w:(kvh + 1) * w], og)
        o_ref[:, g * SWA_KV_DIM:(g + 1) * SWA_KV_DIM] = og.astype(BF16)


def _swa_prompt(q, kv, bias, sink_rows):
    t = q.shape[0]
    w = WINDOW
    rows = SWA_KV_HEADS * w
    return pl.pallas_call(
        _swa_prompt_kernel,
        out_shape=jax.ShapeDtypeStruct((t, SWA_Q_DIM), BF16),
        grid=(t // w,),
        in_specs=[
            pl.BlockSpec((w, SWA_Q_DIM), lambda i: (i, 0)),
            pl.BlockSpec((w, 2 * SWA_KV_DIM), lambda i: (i, 0)),
            pl.BlockSpec((w, 2 * SWA_KV_DIM), lambda i: (jnp.maximum(i - 1, 0), 0)),
            pl.BlockSpec((SWA_GROUP, rows, 2 * w), lambda i: (0, 0, 0)),
            pl.BlockSpec((SWA_GROUP, rows, 1), lambda i: (0, 0, 0)),
        ],
        out_specs=pl.BlockSpec((w, SWA_Q_DIM), lambda i: (i, 0)),
        compiler_params=_cparams("parallel"),
        name="swa_prompt",
    )(q, kv, kv, bias, sink_rows)


def _swa_sample_kernel(q_ref, kvn_ref, kb_ref, vb_ref, bias_b_ref, bias_n_ref, sink_ref, o_ref, ko_ref, vo_ref,
                       *, bb, t):
    gt = SWA_GROUP * t
    rows = SWA_KV_HEADS * gt
    w = kb_ref.shape[2]
    scale = SWA_HEAD_DIM ** -0.5
    lane_head = lax.broadcasted_iota(jnp.int32, (gt, SWA_KV_DIM), 1) // SWA_HEAD_DIM
    tok = lax.broadcasted_iota(jnp.int32, (rows, w), 0) % t
    keyj = lax.broadcasted_iota(jnp.int32, (rows, w), 1)
    valid_b = (tok + w - keyj) < WINDOW
    tok_n = lax.broadcasted_iota(jnp.int32, (rows, 1), 0) % t
    sink = sink_ref[...]
    bs = range(bb)
    toks = range(t)
    qall = [jnp.concatenate([jnp.where(lane_head == kvh, q_ref[b].astype(F32), 0.0) for kvh in range(SWA_KV_HEADS)],
                            axis=0) for b in bs]
    kvn = [kvn_ref[b] for b in bs]
    lb = [_dot(qall[b].astype(BF16), kb_ref[b].astype(BF16)) for b in bs]
    lb = [jnp.where(valid_b, lb[b] * scale + bias_b_ref[...], NEG_INF) for b in bs]
    ln = [[jnp.sum(qall[b] * kvn[b][j:j + 1, 0:SWA_KV_DIM], axis=-1, keepdims=True) for j in toks] for b in bs]
    ln = [[jnp.where(tok_n >= j, ln[b][j] * scale + bias_n_ref[:, j:j + 1], NEG_INF) for j in toks] for b in bs]
    m = [jnp.maximum(jnp.max(lb[b], axis=-1, keepdims=True), sink) for b in bs]
    m = [functools.reduce(jnp.maximum, ln[b], m[b]) for b in bs]
    pb = [jnp.exp(lb[b] - m[b]) for b in bs]
    pn = [[jnp.exp(ln[b][j] - m[b]) for j in toks] for b in bs]
    denom = [jnp.sum(pb[b], axis=-1, keepdims=True) + jnp.exp(sink - m[b]) for b in bs]
    inv = [1.0 / functools.reduce(jnp.add, pn[b], denom[b]) for b in bs]
    oall = [_dot_nt((pb[b] * inv[b]).astype(BF16), vb_ref[b].astype(BF16)) for b in bs]
    for b in bs:
        ob = oall[b]
        for j in toks:
            ob = ob + (pn[b][j] * inv[b]) * kvn[b][j:j + 1, SWA_KV_DIM:]
        og = jnp.zeros((gt, SWA_KV_DIM), F32)
        for kvh in range(SWA_KV_HEADS):
            og = jnp.where(lane_head == kvh, ob[kvh * gt:(kvh + 1) * gt], og)
        o_ref[b] = og.astype(BF16)
    pos = lax.broadcasted_iota(jnp.int32, (SWA_KV_DIM, w), 1)
    pad = jnp.zeros((w - SUBLANES, 2 * SWA_KV_DIM), F32)
    row8 = lax.broadcasted_iota(jnp.int32, (SUBLANES, 2 * SWA_KV_DIM), 0)
    for b in bs:
        last8 = jnp.zeros((SUBLANES, 2 * SWA_KV_DIM), F32)
        for j in toks:
            last8 = jnp.where(row8 == SUBLANES - t + j, kvn[b][j:j + 1], last8)
        tail_t = jnp.concatenate([pad, last8], axis=0).T
        ko_ref[b] = jnp.where(pos >= w - t, tail_t[:SWA_KV_DIM], pltpu.roll(kb_ref[b], w - t, axis=1))
        vo_ref[b] = jnp.where(pos >= w - t, tail_t[SWA_KV_DIM:], pltpu.roll(vb_ref[b], w - t, axis=1))


def _swa_sample(q, kvn, kbuf, vbuf, bias_b, bias_n, sink_rows, *, bb=8):
    b, gt, _ = q.shape
    t = kvn.shape[1]
    w = kbuf.shape[2]
    rows = SWA_KV_HEADS * gt
    blk = lambda i: (i, 0, 0)
    const = lambda i: (0, 0)
    cache = pl.BlockSpec((bb, SWA_KV_DIM, w), blk)
    return pl.pallas_call(
        functools.partial(_swa_sample_kernel, bb=bb, t=t),
        out_shape=(jax.ShapeDtypeStruct((b, gt, SWA_KV_DIM), BF16),
                   jax.ShapeDtypeStruct(kbuf.shape, F32), jax.ShapeDtypeStruct(vbuf.shape, F32)),
        grid=(b // bb,),
        in_specs=[
            pl.BlockSpec((bb, gt, SWA_KV_DIM), blk),
            pl.BlockSpec((bb, t, 2 * SWA_KV_DIM), blk),
            cache, cache,
            pl.BlockSpec((rows, w), const),
            pl.BlockSpec((rows, t), const),
            pl.BlockSpec((rows, 1), const),
        ],
        out_specs=(pl.BlockSpec((bb, gt, SWA_KV_DIM), blk), cache, cache),
        compiler_params=_cparams("parallel"),
        name="swa_sample",
    )(q, kvn, kbuf, vbuf, bias_b, bias_n, sink_rows)


def _softmax_rows(x):
    m = jnp.max(x, axis=-1, keepdims=True)
    p = jnp.exp(x - m)
    return p * (1.0 / jnp.sum(p, axis=-1, keepdims=True))


def _mem_heads(q, mk, mv):
    scale = MEM_HEAD_DIM ** -0.5
    outs = []
    for h in range(MEM_HEADS):
        sl = slice(h * MEM_HEAD_DIM, (h + 1) * MEM_HEAD_DIM)
        p = _softmax_rows(_dot_nt(q[:, sl], mk[:, sl]) * scale)
        outs.append(_dot(p.astype(BF16), mv[:, sl]))
    return jnp.concatenate(outs, axis=-1)


def _mem_prompt_kernel(q_ref, mk_ref, mv_ref, o_ref):
    o_ref[...] = _mem_heads(q_ref[...], mk_ref[...].astype(BF16), mv_ref[...].astype(BF16)).astype(BF16)


def _mem_prompt(q, mkv, *, tm=512):
    m = q.shape[0]
    return pl.pallas_call(
        _mem_prompt_kernel,
        out_shape=jax.ShapeDtypeStruct((m, MEM_DIM), BF16),
        grid=(m // tm,),
        in_specs=[
            pl.BlockSpec((tm, MEM_DIM), lambda i: (i, 0)),
            pl.BlockSpec((N_MEM, MEM_DIM), lambda i: (0, 0)),
            pl.BlockSpec((N_MEM, MEM_DIM), lambda i: (0, 1)),
        ],
        out_specs=pl.BlockSpec((tm, MEM_DIM), lambda i: (i, 0)),
        compiler_params=_cparams("parallel"),
        name="mem_prompt",
    )(q, mkv, mkv)


def _mem_sample_kernel(q_ref, mk_ref, mv_ref, o_ref, *, bb):
    scale = MEM_HEAD_DIM ** -0.5
    pairs = [(b, h) for b in range(bb) for h in range(MEM_HEADS)]
    rows = lambda h: pl.ds(h, N_MEM, stride=MEM_HEADS)
    cols = lambda h: slice(h * MEM_HEAD_DIM, (h + 1) * MEM_HEAD_DIM)
    q = [q_ref[b] for b in range(bb)]
    x = [_dot_nt(q[b][:, cols(h)], mk_ref[b, rows(h), :].astype(BF16)) * scale for b, h in pairs]
    m = [jnp.max(xi, axis=-1, keepdims=True) for xi in x]
    p = [jnp.exp(xi - mi) for xi, mi in zip(x, m)]
    inv = [1.0 / jnp.sum(pi, axis=-1, keepdims=True) for pi in p]
    o = [_dot((pi * ii).astype(BF16), mv_ref[b, rows(h), :].astype(BF16)) for (b, h), pi, ii in zip(pairs, p, inv)]
    for b in range(bb):
        o_ref[b] = jnp.concatenate(o[b * MEM_HEADS:(b + 1) * MEM_HEADS], axis=-1).astype(BF16)


def _mem_sample(q, mk, mv, *, bb=8):
    b, t, _ = q.shape
    blk = lambda i: (i, 0, 0)
    return pl.pallas_call(
        functools.partial(_mem_sample_kernel, bb=bb),
        out_shape=jax.ShapeDtypeStruct((b, t, MEM_DIM), BF16),
        grid=(b // bb,),
        in_specs=[
            pl.BlockSpec((bb, t, MEM_DIM), blk),
            pl.BlockSpec((bb, N_MEM * MEM_HEADS, MEM_HEAD_DIM), blk),
            pl.BlockSpec((bb, N_MEM * MEM_HEADS, MEM_HEAD_DIM), blk),
        ],
        out_specs=pl.BlockSpec((bb, t, MEM_DIM), blk),
        compiler_params=_cparams("parallel"),
        name="mem_sample",
    )(q, mk, mv)


def _head_sum(x, ones_ref):
    hi = x.astype(BF16)
    lo = (x - hi.astype(F32)).astype(BF16)
    return _dot(hi, ones_ref[...]) + _dot(lo, ones_ref[...])


def _rwkv_prep_kernel(x_ref, pre_ref, start_ref, mu_ref, w0_ref, ww2_ref, a0_ref, aw2_ref, gw2_ref, kk_ref, ka_ref,
                      rk_ref, ones_ref, r_o, w_o, k_o, v_o, kk_o, kka_o, g_o, bonus_o, *, seq, tm):
    x = x_ref[...]
    row = lax.broadcasted_iota(jnp.int32, x.shape, 0)
    shifted = pltpu.roll(x, 1, axis=0)
    if seq >= tm:
        is_start = (pl.program_id(0) * tm) % seq == 0
        first_prev = jnp.where(is_start, start_ref[0], pre_ref[SUBLANES - 1:SUBLANES, :])
        prev = jnp.where(row == 0, first_prev, shifted)
    else:
        prev = jnp.where(row % seq == 0, start_ref[...], shifted)
    xs = x + mu_ref[...] * (prev - x)
    d = RWKV_DIM
    r = xs[:, 0:d]
    k = xs[:, d:2 * d]
    v = xs[:, 2 * d:3 * d]
    lw = xs[:, 3 * d:3 * d + LORA_W]
    la = xs[:, 3 * d + LORA_W:3 * d + LORA_W + LORA_A]
    lg = xs[:, 3 * d + LORA_W + LORA_A:]
    wpre = w0_ref[...] + _dot_hi(jnp.tanh(lw), ww2_ref[...])
    w = -jax.nn.softplus(-wpre) - 0.5
    log_decay = -jnp.exp(w)
    a = jax.nn.sigmoid(a0_ref[...] + _dot_hi(la, aw2_ref[...]))
    g = _dot_hi(jax.nn.sigmoid(lg), gw2_ref[...])
    kk = k * kk_ref[...]
    kk = kk / jnp.maximum(jnp.sqrt(_head_sum(kk * kk, ones_ref)), 1e-12)
    kh = k * (1.0 + (a - 1.0) * ka_ref[...])
    r_o[...] = r
    w_o[...] = log_decay
    k_o[...] = kh
    v_o[...] = v
    kk_o[...] = kk
    kka_o[...] = kk * a
    g_o[...] = g
    bonus_o[...] = _head_sum(r * kh * rk_ref[...], ones_ref) * v


def _rwkv_prep(x, shift0, p, ones, *, seq, tm=256):
    m = x.shape[0]
    row = lambda i: (i, 0)
    const = lambda i: (0, 0)
    vec = lambda n: pl.BlockSpec((1, n), const)
    out = jax.ShapeDtypeStruct((m, RWKV_DIM), F32)
    if seq >= tm:
        assert seq % tm == 0
        start = shift0
        start_spec = pl.BlockSpec((1, 1, RWKV_IN), lambda i: ((i * tm) // seq, 0, 0))
    else:
        assert tm % seq == 0
        start = jnp.repeat(shift0[:, 0], seq, axis=0)
        start_spec = pl.BlockSpec((tm, RWKV_IN), row)
    pre_blocks = tm // SUBLANES
    return pl.pallas_call(
        functools.partial(_rwkv_prep_kernel, seq=seq, tm=tm),
        out_shape=(out,) * 8,
        grid=(m // tm,),
        in_specs=[
            pl.BlockSpec((tm, RWKV_IN), row),
            pl.BlockSpec((SUBLANES, RWKV_IN), lambda i: (jnp.maximum(i * pre_blocks - 1, 0), 0)),
            start_spec,
            vec(RWKV_IN), vec(RWKV_DIM),
            pl.BlockSpec((LORA_W, RWKV_DIM), const),
            vec(RWKV_DIM),
            pl.BlockSpec((LORA_A, RWKV_DIM), const),
            pl.BlockSpec((LORA_G, RWKV_DIM), const),
            vec(RWKV_DIM), vec(RWKV_DIM), vec(RWKV_DIM),
            pl.BlockSpec((RWKV_DIM, RWKV_DIM), const),
        ],
        out_specs=(pl.BlockSpec((tm, RWKV_DIM), row),) * 8,
        compiler_params=_cparams("parallel"),
        name="rwkv_prep",
    )(x, x, start, p["mu"], p["w0"], p["w_w2"], p["a0"], p["a_w2"], p["g_w2"], p["k_k"], p["k_a"], p["r_k"], ones)


def _half_sums(x, low):
    lo = jnp.sum(jnp.where(low, x, 0.0), axis=-1, keepdims=True)
    hi = jnp.sum(jnp.where(low, 0.0, x), axis=-1, keepdims=True)
    return jnp.where(low, lo, hi)


def _rwkv_scan_kernel(r_ref, w_ref, k_ref, v_ref, kk_ref, kka_ref, s0_ref, y_ref, so_ref, s_scr, *, bb, tt):
    jt = pl.program_id(1)
    group = min(tt, SUBLANES)
    n = RWKV_HEAD_DIM
    pairs = RWKV_HEADS // 2
    low = lax.broadcasted_iota(jnp.int32, (n, LANES), 1) < n
    diag = (lax.broadcasted_iota(jnp.int32, (n, LANES), 1) % n) == lax.broadcasted_iota(jnp.int32, (n, LANES), 0)

    @pl.when(jt == 0)
    def _():
        for b in range(bb):
            for h in range(RWKV_HEADS):
                s_scr[b, h // 2, :, (h % 2) * n:(h % 2 + 1) * n] = s0_ref[b, h]

    nb = 2 if bb % 2 == 0 else 1
    chains = [(bi, p) for bi in range(nb) for p in range(pairs)]

    def batch_body(b0, carry):
        def group_body(t0, states):
            toks = pl.ds(pl.multiple_of(t0 * group, group), group)
            at = lambda ref: [ref[b0 * nb + bi, toks, pl.ds(p * LANES, LANES)] for bi, p in chains]
            r, k, v, kk, kka = at(r_ref), at(k_ref), at(v_ref), at(kk_ref), at(kka_ref)
            w = [jnp.exp(x) for x in at(w_ref)]
            s = list(states)
            ys = [[] for _ in chains]
            for j in range(group):
                one = slice(j, j + 1)
                sa = [_half_sums(s[c] * (-kk[c][one]), low) for c in range(len(chains))]
                vcol = [_half_sums(jnp.where(diag, v[c][one], 0.0), low) for c in range(len(chains))]
                s = [s[c] * w[c][one] + sa[c] * kka[c][one] + vcol[c] * k[c][one] for c in range(len(chains))]
                ycol = [_half_sums(s[c] * r[c][one], low) for c in range(len(chains))]
                for c in range(len(chains)):
                    ys[c].append(jnp.sum(jnp.where(diag, ycol[c], 0.0), axis=0, keepdims=True))
            for c, (bi, p) in enumerate(chains):
                y_ref[b0 * nb + bi, toks, pl.ds(p * LANES, LANES)] = jnp.concatenate(ys[c], axis=0)
            return tuple(s)

        states = tuple(s_scr[b0 * nb + bi, p] for bi, p in chains)
        states = lax.fori_loop(0, tt // group, group_body, states)
        for c, (bi, p) in enumerate(chains):
            s_scr[b0 * nb + bi, p] = states[c]
        return carry

    lax.fori_loop(0, bb // nb, batch_body, 0)

    @pl.when(jt == pl.num_programs(1) - 1)
    def _():
        for b in range(bb):
            for h in range(RWKV_HEADS):
                so_ref[b, h] = s_scr[b, h // 2, :, (h % 2) * n:(h % 2 + 1) * n]


def _rwkv_scan(r, w, k, v, kk, kka, s0, *, bb, tt):
    b, t, d = r.shape
    n = RWKV_HEAD_DIM
    tok = pl.BlockSpec((bb, tt, d), lambda i, j: (i, j, 0))
    st = pl.BlockSpec((bb, RWKV_HEADS, n, n), lambda i, j: (i, 0, 0, 0))
    return pl.pallas_call(
        functools.partial(_rwkv_scan_kernel, bb=bb, tt=tt),
        out_shape=(jax.ShapeDtypeStruct((b, t, d), F32), jax.ShapeDtypeStruct((b, RWKV_HEADS, n, n), F32)),
        grid=(b // bb, t // tt),
        in_specs=[tok] * 6 + [st],
        out_specs=(tok, st),
        scratch_shapes=[pltpu.VMEM((bb, RWKV_HEADS // 2, n, LANES), F32)],
        compiler_params=_cparams("parallel", "arbitrary"),
        name="rwkv_scan",
    )(r, w, k, v, kk, kka, s0)


def _rwkv_lanes_kernel(r_ref, lw_ref, k_ref, v_ref, kk_ref, kka_ref, s_ref, y_ref, so_ref, v_scr, y_scr, *, t):
    n = RWKV_HEAD_DIM
    nb = s_ref.shape[-1]
    heads = range(2)

    def token_major(ref, j):
        return ref[pl.ds(j, nb, stride=t), :].T

    for j in range(t):
        v_scr[...] = token_major(v_ref, j)
        r_t, k_t, kk_t, kka_t = (token_major(ref, j) for ref in (r_ref, k_ref, kk_ref, kka_ref))
        w_t = jnp.exp(token_major(lw_ref, j))
        src = s_ref if j == 0 else so_ref

        def value_group(g, carry):
            rows = pl.multiple_of(g * SUBLANES, SUBLANES)
            ys = [[] for _ in heads]
            vg = [v_scr[pl.ds(h * n + rows, SUBLANES), :] for h in heads]
            for i in range(SUBLANES):
                for h in heads:
                    f = slice(h * n, (h + 1) * n)
                    s = src[h, rows + i]
                    sa = jnp.sum(s * (-kk_t[f]), axis=0, keepdims=True)
                    s = s * w_t[f] + sa * kka_t[f] + vg[h][i:i + 1] * k_t[f]
                    so_ref[h, rows + i] = s
                    ys[h].append(jnp.sum(s * r_t[f], axis=0, keepdims=True))
            for h in heads:
                y_scr[pl.ds(h * n + rows, SUBLANES), :] = jnp.concatenate(ys[h], axis=0)
            return carry

        lax.fori_loop(0, n // SUBLANES, value_group, 0)
        y_ref[j] = y_scr[...].T


def _rwkv_lanes(r, lw, k, v, kk, kka, s0, *, t):
    m, d = r.shape
    nb = m // t
    n = RWKV_HEAD_DIM
    assert nb == LANES, "one batch per lane"
    tok = pl.BlockSpec((m, LANES), lambda p: (0, p))
    st = pl.BlockSpec((2, n, n, nb), lambda p: (p, 0, 0, 0))
    return pl.pallas_call(
        functools.partial(_rwkv_lanes_kernel, t=t),
        out_shape=(jax.ShapeDtypeStruct((t, nb, d), F32), jax.ShapeDtypeStruct((RWKV_HEADS, n, n, nb), F32)),
        grid=(RWKV_HEADS // 2,),
        in_specs=[tok] * 6 + [st],
        out_specs=(pl.BlockSpec((t, nb, LANES), lambda p: (0, 0, p)), st),
        scratch_shapes=[pltpu.VMEM((LANES, nb), F32), pltpu.VMEM((LANES, nb), F32)],
        compiler_params=_cparams("parallel"),
        name="rwkv_lanes",
    )(r, lw, k, v, kk, kka, s0)


CHUNK = 64
GROUP_HEADS = 4
GROUP_W = GROUP_HEADS * RWKV_HEAD_DIM
N_GROUPS = RWKV_HEADS // GROUP_HEADS
(MASK_SAME, MASK_STRICT, MASK_INCL, MASK_LEVEL0) = (0, 1, 2, 3)
N_LEVELS = int(math.log2(CHUNK))


def _chunk_masks():
    i = np.arange(GROUP_W)
    same = (i[:, None] // CHUNK) == (i[None, :] // CHUNK)
    masks = [same, same & (i[None, :] < i[:, None]), same & (i[None, :] <= i[:, None])]
    for lvl in range(N_LEVELS):
        m = 1 << lvl
        masks.append(((i[:, None] // (2 * m)) == (i[None, :] // (2 * m))) & ((i[:, None] // m) != (i[None, :] // m))
                     & (i[None, :] < i[:, None]))
    return np.stack(masks).astype(np.float32)


def _rwkv_chunk_kernel(r_ref, lw_ref, k_ref, v_ref, kk_ref, kka_ref, st0_ref, tri_ref, eye_ref, mask_ref,
                       y_ref, sto_ref, st_scr):
    @pl.when(pl.program_id(0) == 0)
    def _():
        st_scr[...] = st0_ref[...]

    eye = eye_ref[...]
    eye_b = eye.astype(BF16)
    tri = tri_ref[...]
    tile_rows = lambda x: jnp.concatenate([x] * GROUP_HEADS, axis=0)
    block_diag = lambda x: (tile_rows(x) * mask_ref[MASK_SAME]).astype(BF16)

    n_chunks = y_ref.shape[0] // CHUNK
    chains = [(slice(c * CHUNK, (c + 1) * CHUNK), slice(g * GROUP_W, (g + 1) * GROUP_W))
              for c in range(n_chunks) for g in range(N_GROUPS)]
    each = lambda f, *cols: [f(*args) for args in zip(*cols)]
    same, strict, incl = mask_ref[MASK_SAME], mask_ref[MASK_STRICT], mask_ref[MASK_INCL]

    def cum_decay(lw):
        h1 = lw.astype(BF16)
        r1 = lw - h1.astype(F32)
        h2 = r1.astype(BF16)
        h3 = (r1 - h2.astype(F32)).astype(BF16)
        return _dot(tri, h1) + _dot(tri, h2) + _dot(tri, h3)

    lw = [lw_ref[rows, sl] for rows, sl in chains]
    kka = [kka_ref[rows, sl] for rows, sl in chains]
    k = [k_ref[rows, sl] for rows, sl in chains]
    cum = each(cum_decay, lw)
    cum_last = each(lambda c: c[CHUNK - 1:CHUNK, :], cum)
    p_inv = each(lambda c: jnp.exp(-c), cum)
    p_tail = each(lambda c, cl: jnp.exp(cl - c), cum, cum_last)
    a_bd = [block_diag(-kk_ref[rows, sl] * jnp.exp(c - l)) for (rows, sl), c, l in zip(chains, cum, lw)]
    r_f = [tile_rows(r_ref[rows, sl] * jnp.exp(c)) * same for (rows, sl), c in zip(chains, cum)]
    r_bd = each(lambda x: x.astype(BF16), r_f)
    v_bd = [block_diag(v_ref[rows, sl]) for rows, sl in chains]
    b_rep = each(lambda x, p: tile_rows((x * p).astype(BF16)), kka, p_inv)
    k_rep = each(lambda x, p: tile_rows((x * p).astype(BF16)), k, p_inv)
    bh_rep = each(lambda x, p: tile_rows((x * p).astype(BF16)), kka, p_tail)
    kh_rep = each(lambda x, p: tile_rows((x * p).astype(BF16)), k, p_tail)

    l_ab_f = each(lambda a, b: _dot_nt(a, b) * strict, a_bd, b_rep)
    l_ab = each(lambda x: x.astype(BF16), l_ab_f)
    l_ak = each(lambda a, b: (_dot_nt(a, b) * strict).astype(BF16), a_bd, k_rep)
    m_rb = each(lambda a, b: (_dot_nt(a, b) * incl).astype(BF16), r_bd, b_rep)
    m_rk = each(lambda a, b: (_dot_nt(a, b) * incl).astype(BF16), r_bd, k_rep)
    bh_t = each(lambda x: (_dot_nt(eye_b, x) * same).astype(BF16), bh_rep)
    kh_t = each(lambda x: (_dot_nt(eye_b, x) * same).astype(BF16), kh_rep)

    d = each(lambda l: eye + l * mask_ref[MASK_LEVEL0], l_ab_f)
    for lvl in range(1, N_LEVELS):
        d_b = each(lambda x: x.astype(BF16), d)
        x = each(lambda l, db: (_dot(l, db) * mask_ref[MASK_LEVEL0 + lvl]).astype(BF16), l_ab, d_b)
        d = each(lambda dd, db, xx: dd + _dot(db, xx), d, d_b, x)
    t_b = each(lambda x: x.astype(BF16), d)

    wm = each(lambda a, b, vv: _dot(jnp.concatenate([a, b], axis=0), vv), l_ak, m_rk, v_bd)
    twa = each(lambda t, w, a: _dot(t, jnp.concatenate([w[:GROUP_W].astype(BF16), a], axis=1)).astype(BF16),
               t_b, wm, a_bd)
    ry = each(_dot, m_rb, twa)
    mn = each(_dot, bh_t, twa)
    khv = each(_dot, kh_t, v_bd)
    y0 = each(lambda a, w: a[:, :GROUP_W] + w[GROUP_W:], ry, wm)
    n_x = each(lambda a, b: a[:, :GROUP_W] + b, mn, khv)
    mr = each(lambda a, cl, rf, b: jnp.concatenate(
        [(eye * jnp.exp(cl) + a[:, GROUP_W:]).astype(BF16), (rf + b[:, GROUP_W:]).astype(BF16)], axis=0),
        mn, cum_last, r_f, ry)

    st = [st_scr[g] for g in range(N_GROUPS)]
    for i, (rows, sl) in enumerate(chains):
        g = i % N_GROUPS
        ys = _dot(mr[i], st[g].astype(BF16))
        st[g] = ys[:GROUP_W] + n_x[i]
        y_bd = ys[GROUP_W:] + y0[i]
        y = y_bd[0:CHUNK]
        for h in range(1, GROUP_HEADS):
            y = y + y_bd[h * CHUNK:(h + 1) * CHUNK]
        y_ref[rows, sl] = y
    for g in range(N_GROUPS):
        st_scr[g] = st[g]

    @pl.when(pl.program_id(0) == pl.num_programs(0) - 1)
    def _():
        sto_ref[...] = st_scr[...]


def _rwkv_chunked(r, lw, k, v, kk, kka, s0, *, chunks_per_step=4):
    t, d = r.shape
    n = RWKV_HEAD_DIM
    tt = CHUNK * chunks_per_step
    assert CHUNK == n and t % tt == 0
    st0 = jnp.einsum("ghvk,hj->ghkjv", s0.reshape(N_GROUPS, GROUP_HEADS, n, n), jnp.eye(GROUP_HEADS, dtype=F32))
    st0 = st0.reshape(N_GROUPS, GROUP_W, GROUP_W)
    tri = jnp.asarray(np.tril(np.ones((CHUNK, CHUNK), np.float32)), BF16)
    eye = jnp.eye(GROUP_W, dtype=F32)
    masks = jnp.asarray(_chunk_masks())
    tok = pl.BlockSpec((tt, d), lambda c: (c, 0))
    st_spec = pl.BlockSpec((N_GROUPS, GROUP_W, GROUP_W), lambda c: (0, 0, 0))
    y, st = pl.pallas_call(
        _rwkv_chunk_kernel,
        out_shape=(jax.ShapeDtypeStruct((t, d), F32), jax.ShapeDtypeStruct((N_GROUPS, GROUP_W, GROUP_W), F32)),
        grid=(t // tt,),
        in_specs=[tok] * 6 + [st_spec, pl.BlockSpec((CHUNK, CHUNK), lambda c: (0, 0)),
                              pl.BlockSpec((GROUP_W, GROUP_W), lambda c: (0, 0)),
                              pl.BlockSpec(masks.shape, lambda c: (0, 0, 0))],
        out_specs=(tok, st_spec),
        scratch_shapes=[pltpu.VMEM((N_GROUPS, GROUP_W, GROUP_W), F32)],
        compiler_params=_cparams("arbitrary"),
        name="rwkv_chunk",
    )(r, lw, k, v, kk, kka, st0, tri, eye, masks)
    st5 = st.reshape(N_GROUPS, GROUP_HEADS, n, GROUP_HEADS, n)
    s_new = jnp.einsum("ghkjv,hj->ghvk", st5, jnp.eye(GROUP_HEADS, dtype=F32)).reshape(RWKV_HEADS, n, n)
    return y, s_new


def _rwkv_post_kernel(y_ref, bonus_ref, g_ref, lnw_ref, lnb_ref, ones_ref, o_ref):
    y = y_ref[...]
    inv_n = 1.0 / RWKV_HEAD_DIM
    mu = _head_sum(y, ones_ref) * inv_n
    dlt = y - mu
    var = _head_sum(dlt * dlt, ones_ref) * inv_n
    yn = dlt * lax.rsqrt(var + GN_EPS) * lnw_ref[...] + lnb_ref[...]
    o_ref[...] = ((yn + bonus_ref[...]) * g_ref[...]).astype(BF16)


def _rwkv_post(y, bonus, g, lnw, lnb, ones, *, tm=256):
    m = y.shape[0]
    row = lambda i: (i, 0)
    const = lambda i: (0, 0)
    tile = pl.BlockSpec((tm, RWKV_DIM), row)
    return pl.pallas_call(
        _rwkv_post_kernel,
        out_shape=jax.ShapeDtypeStruct((m, RWKV_DIM), BF16),
        grid=(m // tm,),
        in_specs=[tile, tile, tile, pl.BlockSpec((1, RWKV_DIM), const), pl.BlockSpec((1, RWKV_DIM), const),
                  pl.BlockSpec((RWKV_DIM, RWKV_DIM), const)],
        out_specs=tile,
        compiler_params=_cparams("parallel"),
        name="rwkv_post",
    )(y, bonus, g, lnw, lnb, ones)


GATE_BLOCK = math.gcd(PROJ_DIM, D_MODEL)


def _merge_kernel(h_ref, g_ref, os_ref, or_ref, om_ref, wos_ref, wor_ref, wom_ref, wout_ref, *rest, parts):
    gate_refs, (o_ref, u_ref, acc_ref) = rest[:N_BRANCH * parts], rest[N_BRANCH * parts:]
    j = pl.program_id(1)

    @pl.when(j == 0)
    def _():
        u_ref[...] = _rms(h_ref[...], g_ref[...]).astype(BF16)
        acc_ref[...] = jnp.zeros_like(acc_ref)

    u = u_ref[...]
    merged = None
    for b, (x_ref, w_ref) in enumerate(((os_ref, wos_ref), (or_ref, wor_ref), (om_ref, wom_ref))):
        gate = jnp.concatenate([_dot(u, gate_refs[b * parts + c][...]) for c in range(parts)], axis=1)
        term = jax.nn.sigmoid(gate) * _dot(x_ref[...], w_ref[...])
        merged = term if merged is None else merged + term
    acc_ref[...] += _dot(merged.astype(BF16), wout_ref[...])

    @pl.when(j == pl.num_programs(1) - 1)
    def _():
        o_ref[...] = h_ref[...] + acc_ref[...]


def _merge(h, g, o_swa, o_rw, o_mem, w_in, wo_swa, wo_rw, wo_mem, w_out, *, tm=512, tn=512):
    m, d = h.shape
    nt = d // tn
    parts = tn // GATE_BLOCK
    g0 = PROJ_DIM // GATE_BLOCK
    row = lambda i, j: (i, 0)
    col = lambda i, j: (0, j)
    gate_specs = [pl.BlockSpec((d, GATE_BLOCK), functools.partial(
        lambda i, j, off: (0, off + j * parts), off=g0 + b * (d // GATE_BLOCK) + c))
        for b in range(N_BRANCH) for c in range(parts)]
    return pl.pallas_call(
        functools.partial(_merge_kernel, parts=parts),
        out_shape=jax.ShapeDtypeStruct((m, d), F32),
        grid=(m // tm, nt),
        in_specs=[
            pl.BlockSpec((tm, d), row),
            pl.BlockSpec((1, d), lambda i, j: (0, 0)),
            pl.BlockSpec((tm, SWA_Q_DIM), row),
            pl.BlockSpec((tm, RWKV_DIM), row),
            pl.BlockSpec((tm, MEM_DIM), row),
            pl.BlockSpec((SWA_Q_DIM, tn), col),
            pl.BlockSpec((RWKV_DIM, tn), col),
            pl.BlockSpec((MEM_DIM, tn), col),
            pl.BlockSpec((tn, d), lambda i, j: (j, 0)),
        ] + gate_specs,
        out_specs=pl.BlockSpec((tm, d), row),
        scratch_shapes=[pltpu.VMEM((tm, d), BF16), pltpu.VMEM((tm, d), F32)],
        compiler_params=_cparams("parallel", "arbitrary"),
        name="merge",
    )(h, g, o_swa, o_rw, o_mem, wo_swa, wo_rw, wo_mem, w_out, *([w_in] * (N_BRANCH * parts)))


def _t5_bucket(dist):
    max_exact = N_BUCKETS // 2
    d = np.maximum(dist, 0)
    log_ratio = (np.log(np.maximum(d, 1).astype(np.float32) / np.float32(max_exact))
                 / np.float32(math.log(MAX_DISTANCE / max_exact)))
    large = np.minimum(max_exact + (log_ratio * (N_BUCKETS - max_exact)).astype(np.int32), N_BUCKETS - 1)
    return np.where(d < max_exact, d, large).astype(np.int32)


def _rel_bias(table, dist):
    onehot = np.eye(N_BUCKETS, dtype=np.float32)[_t5_bucket(dist).reshape(-1)]
    bias = jnp.einsum("nb,bh->hn", jnp.asarray(onehot), table, precision=lax.Precision.HIGHEST)
    return bias.reshape(SWA_HEADS, *dist.shape)


def _rwkv_branch(xr, shift0, s0, p, ones, *, bb, tt):
    b, t, _ = xr.shape
    flat = lambda z: z.reshape(b * t, z.shape[-1])
    r, w, k, v, kk, kka, g, bonus = _rwkv_prep(flat(xr), shift0, p, ones, seq=t)
    if b == 1:
        y, s_new = _rwkv_chunked(r, w, k, v, kk, kka, s0[0])
        s_new = s_new[None]
    elif b == LANES:
        y, s_new = _rwkv_lanes(r, w, k, v, kk, kka, jnp.transpose(s0, (1, 2, 3, 0)), t=t)
        y = jnp.transpose(y, (1, 0, 2))
        s_new = jnp.transpose(s_new, (3, 0, 1, 2))
    else:
        seq = lambda z: z.reshape(b, t, RWKV_DIM)
        y, s_new = _rwkv_scan(seq(r), seq(w), seq(k), seq(v), seq(kk), seq(kka), s0, bb=bb, tt=tt)
    o = _rwkv_post(flat(y), bonus, g, p["ln_w"], p["ln_b"], ones)
    return o, s_new


def kernel(x_prompt, mem_prompt, x_sample, cache_swa_k, cache_swa_v, state_rwkv, state_rwkv_shift, cache_mem_k, cache_mem_v, ffn1_norm, ffn1_wi, ffn1_wo, mix_norm, w_in, swa_sinks, rel_bias_table, rwkv_mu, rwkv_w0, rwkv_w_w2, rwkv_a0, rwkv_a_w2, rwkv_g_w2, rwkv_k_k, rwkv_k_a, rwkv_r_k, rwkv_ln_w, rwkv_ln_b, mem_norm, w_mem_kv, w_o_swa, w_o_rwkv, w_o_mem, w_out, ffn2_norm, ffn2_wi, ffn2_wo, final_norm):
    assert ffn1_wi.shape[0] == 1, "single-layer trunk"
    bp, tp, d = x_prompt.shape
    bs, ts, _ = x_sample.shape
    assert bp == 1
    row = lambda z: z.reshape(1, -1).astype(F32)

    w_in_b = w_in[0].astype(BF16)
    w_q = w_in_b[:, :SWA_Q_DIM].reshape(d, SWA_KV_HEADS, SWA_GROUP, SWA_HEAD_DIM).transpose(0, 2, 1, 3).reshape(d, SWA_Q_DIM)
    wo_swa = w_o_swa[0].astype(BF16).reshape(SWA_KV_HEADS, SWA_GROUP, SWA_HEAD_DIM, d).transpose(1, 0, 2, 3).reshape(SWA_Q_DIM, d)
    wo_rw = w_o_rwkv[0].astype(BF16)
    wo_mem = w_o_mem[0].astype(BF16)
    w_out_b = w_out[0].astype(BF16)
    wi1, wo1 = ffn1_wi[0].astype(BF16), ffn1_wo[0].astype(BF16)
    wi2, wo2 = ffn2_wi[0].astype(BF16), ffn2_wo[0].astype(BF16)
    g1, gm, g2, gf = row(ffn1_norm[0]), row(mix_norm[0]), row(ffn2_norm[0]), row(final_norm)
    rp = {
        "mu": row(rwkv_mu[0]), "w0": row(rwkv_w0[0]), "w_w2": rwkv_w_w2[0], "a0": row(rwkv_a0[0]),
        "a_w2": rwkv_a_w2[0], "g_w2": rwkv_g_w2[0], "k_k": row(rwkv_k_k[0]), "k_a": row(rwkv_k_a[0]),
        "r_k": row(rwkv_r_k[0]), "ln_w": row(rwkv_ln_w[0]), "ln_b": row(rwkv_ln_b[0]),
    }
    seg = np.arange(RWKV_DIM) // RWKV_HEAD_DIM
    ones = jnp.asarray(seg[:, None] == seg[None, :], dtype=BF16)
    sinks = swa_sinks[0].astype(F32)
    table = rel_bias_table.astype(F32)

    xp = x_prompt.reshape(tp, d)
    xs = x_sample.reshape(bs * ts, d)
    hp = _ffn(xp, g1, wi1, wo1, gf, final_norm=False)
    hs = _ffn(xs, g1, wi1, wo1, gf, final_norm=False)
    qp, kvp, xrp, qmp = _inproj(hp, gm, w_q, w_in_b)
    qs, kvs, xrs, qms = _inproj(hs, gm, w_q, w_in_b)

    w = WINDOW
    dist_p = np.arange(w)[:, None] + w - np.arange(2 * w)[None, :]
    bias_p = _rel_bias(table, dist_p).reshape(SWA_KV_HEADS, SWA_GROUP, w, 2 * w).transpose(1, 0, 2, 3)
    bias_p = bias_p.reshape(SWA_GROUP, SWA_KV_HEADS * w, 2 * w)
    sink_p = jnp.repeat(sinks.reshape(SWA_KV_HEADS, SWA_GROUP).T, w, axis=1).reshape(SWA_GROUP, SWA_KV_HEADS * w, 1)
    o_swa_p = _swa_prompt(qp, kvp, bias_p, sink_p)

    wbuf = cache_swa_k.shape[2]
    dist_s = np.arange(ts)[:, None] + wbuf - np.arange(wbuf + ts)[None, :]
    bias_s = _rel_bias(table, dist_s).reshape(SWA_HEADS * ts, wbuf + ts)
    sink_rows = jnp.repeat(sinks, ts).reshape(SWA_HEADS * ts, 1)
    qs_gt = qs.reshape(bs, ts, SWA_GROUP, SWA_KV_DIM).transpose(0, 2, 1, 3).reshape(bs, SWA_GROUP * ts, SWA_KV_DIM)
    kbuf = cache_swa_k[0].reshape(bs, wbuf, SWA_KV_DIM).transpose(0, 2, 1)
    vbuf = cache_swa_v[0].reshape(bs, wbuf, SWA_KV_DIM).transpose(0, 2, 1)
    o_swa_s, knew_t, vnew_t = _swa_sample(qs_gt, kvs.reshape(bs, ts, 2 * SWA_KV_DIM), kbuf, vbuf,
                                          bias_s[:, :wbuf], bias_s[:, wbuf:], sink_rows)
    o_swa_s = o_swa_s.reshape(bs, SWA_GROUP, ts, SWA_KV_DIM).transpose(0, 2, 1, 3).reshape(bs * ts, SWA_Q_DIM)

    zero_shift = jnp.zeros((bp, 1, RWKV_IN), F32)
    zero_state = jnp.zeros((bp, RWKV_HEADS, RWKV_HEAD_DIM, RWKV_HEAD_DIM), F32)
    o_rw_p, state_p = _rwkv_branch(xrp.reshape(bp, tp, RWKV_IN), zero_shift, zero_state, rp, ones,
                                   bb=1, tt=256)
    o_rw_s, state_s = _rwkv_branch(xrs.reshape(bs, ts, RWKV_IN), state_rwkv_shift[0], state_rwkv[0], rp, ones,
                                   bb=8, tt=ts)

    mkv = _norm_matmul(mem_prompt.reshape(N_MEM, d), row(mem_norm[0]), w_mem_kv[0].astype(BF16))
    o_mem_p = _mem_prompt(qmp, mkv)
    o_mem_s = _mem_sample(qms.reshape(bs, ts, MEM_DIM), cache_mem_k[0].reshape(bs, N_MEM * MEM_HEADS, MEM_HEAD_DIM),
                          cache_mem_v[0].reshape(bs, N_MEM * MEM_HEADS, MEM_HEAD_DIM)).reshape(bs * ts, MEM_DIM)

    hp = _merge(hp, gm, o_swa_p, o_rw_p, o_mem_p, w_in_b, wo_swa, wo_rw, wo_mem, w_out_b)
    hs = _merge(hs, gm, o_swa_s, o_rw_s, o_mem_s, w_in_b, wo_swa, wo_rw, wo_mem, w_out_b)
    y_prompt = _ffn(hp, g2, wi2, wo2, gf, final_norm=True).reshape(bp, tp, d)
    y_sample = _ffn(hs, g2, wi2, wo2, gf, final_norm=True).reshape(bs, ts, d)

    wp = min(w, tp)
    p_k = kvp[tp - wp:, :SWA_KV_DIM].reshape(1, bp, wp, SWA_KV_HEADS, SWA_HEAD_DIM)
    p_v = kvp[tp - wp:, SWA_KV_DIM:].reshape(1, bp, wp, SWA_KV_HEADS, SWA_HEAD_DIM)
    p_mk = mkv[:, :MEM_DIM].reshape(1, bp, N_MEM, MEM_HEADS, MEM_HEAD_DIM)
    p_mv = mkv[:, MEM_DIM:].reshape(1, bp, N_MEM, MEM_HEADS, MEM_HEAD_DIM)
    s_k = knew_t.transpose(0, 2, 1).reshape(1, bs, wbuf, SWA_KV_HEADS, SWA_HEAD_DIM)
    s_v = vnew_t.transpose(0, 2, 1).reshape(1, bs, wbuf, SWA_KV_HEADS, SWA_HEAD_DIM)
    return (y_prompt, y_sample,
            p_k, p_v, state_p[None], xrp[tp - 1:].reshape(1, bp, 1, RWKV_IN), p_mk, p_mv,
            s_k, s_v, state_s[None], xrs.reshape(bs, ts, RWKV_IN)[:, ts - 1:][None])
```

```python
import functools
import math

import jax
import jax.numpy as jnp
import numpy as np
from jax import lax
from jax.experimental import pallas as pl
from jax.experimental.pallas import tpu as pltpu

F32 = jnp.float32
BF16 = jnp.bfloat16

D_MODEL = 2048
D_FF = 5632
SWA_HEADS = 16
SWA_KV_HEADS = 4
SWA_GROUP = SWA_HEADS // SWA_KV_HEADS
SWA_HEAD_DIM = 64
SWA_Q_DIM = SWA_HEADS * SWA_HEAD_DIM
SWA_KV_DIM = SWA_KV_HEADS * SWA_HEAD_DIM
WINDOW = 128
N_BUCKETS = 32
MAX_DISTANCE = 128
RWKV_HEADS = 8
RWKV_HEAD_DIM = 64
RWKV_DIM = RWKV_HEADS * RWKV_HEAD_DIM
LORA_W = 64
LORA_A = 64
LORA_G = 128
RWKV_IN = 3 * RWKV_DIM + LORA_W + LORA_A + LORA_G
N_MEM = 256
MEM_HEADS = 4
MEM_HEAD_DIM = 128
MEM_DIM = MEM_HEADS * MEM_HEAD_DIM
N_BRANCH = 3
PROJ_DIM = SWA_Q_DIM + 2 * SWA_KV_DIM + RWKV_IN + MEM_DIM
NORM_EPS = 1e-6
GN_EPS = 64e-5
NEG_INF = -1e30

LANES = 128
SUBLANES = 8
VMEM_LIMIT = 56 * 1024 * 1024


def _cparams(*sem):
    return pltpu.CompilerParams(dimension_semantics=sem, vmem_limit_bytes=VMEM_LIMIT)


def _rms(x, g):
    return x * lax.rsqrt(jnp.mean(x * x, axis=-1, keepdims=True) + NORM_EPS) * g


def _dot(a, b):
    return jnp.dot(a, b, preferred_element_type=F32)


def _dot_nt(a, b):
    return lax.dot_general(a, b, (((1,), (1,)), ((), ())), preferred_element_type=F32)


def _dot_hi(a, b):
    return jnp.dot(a, b, preferred_element_type=F32, precision=lax.Precision.HIGHEST)


def _ffn_kernel(x_ref, g_ref, wg_ref, wu_ref, wo_ref, gf_ref, o_ref, xn_ref, *, final_norm):
    j = pl.program_id(1)

    @pl.when(j == 0)
    def _():
        xn_ref[...] = _rms(x_ref[...], g_ref[...]).astype(BF16)
        o_ref[...] = jnp.zeros_like(o_ref)

    xn = xn_ref[...]
    gate = _dot(xn, wg_ref[...].astype(BF16))
    up = _dot(xn, wu_ref[...].astype(BF16))
    act = (gate * jax.nn.sigmoid(gate)) * up
    o_ref[...] += _dot(act.astype(BF16), wo_ref[...].astype(BF16))

    @pl.when(j == pl.num_programs(1) - 1)
    def _():
        h = x_ref[...] + 0.5 * o_ref[...]
        if final_norm:
            h = _rms(h, gf_ref[...])
        o_ref[...] = h


def _ffn(x, g, wi, wo, gf, *, final_norm, tm=1024, tf=256):
    m, d = x.shape
    tm = min(tm, m)
    dff = wo.shape[0]
    nf = dff // tf
    return pl.pallas_call(
        functools.partial(_ffn_kernel, final_norm=final_norm),
        out_shape=jax.ShapeDtypeStruct((m, d), F32),
        grid=(m // tm, nf),
        in_specs=[
            pl.BlockSpec((tm, d), lambda i, j: (i, 0), pipeline_mode=pl.Buffered(1)),
            pl.BlockSpec((1, d), lambda i, j: (0, 0)),
            pl.BlockSpec((d, tf), lambda i, j: (0, j)),
            pl.BlockSpec((d, tf), lambda i, j: (0, j + nf)),
            pl.BlockSpec((tf, d), lambda i, j: (j, 0)),
            pl.BlockSpec((1, d), lambda i, j: (0, 0)),
        ],
        out_specs=pl.BlockSpec((tm, d), lambda i, j: (i, 0)),
        scratch_shapes=[pltpu.VMEM((tm, d), BF16)],
        compiler_params=_cparams("parallel", "arbitrary"),
        name="ffn_final" if final_norm else "ffn",
    )(x, g, wi, wi, wo, gf)


def _inproj_kernel(h_ref, g_ref, wq_ref, w_ref, q_ref, kv_ref, xr_ref, qm_ref):
    u = _rms(h_ref[...], g_ref[...]).astype(BF16)
    c0, c1, c2 = SWA_Q_DIM, SWA_Q_DIM + 2 * SWA_KV_DIM, SWA_Q_DIM + 2 * SWA_KV_DIM + RWKV_IN
    q_ref[...] = _dot(u, wq_ref[...]).astype(BF16)
    kv_ref[...] = _dot(u, w_ref[:, c0:c1])
    xr_ref[...] = _dot(u, w_ref[:, c1:c2])
    qm_ref[...] = _dot(u, w_ref[:, c2:PROJ_DIM]).astype(BF16)


def _inproj(h, g, wq, w, *, tm=256):
    m, d = h.shape
    row = lambda i: (i, 0)
    return pl.pallas_call(
        _inproj_kernel,
        out_shape=(
            jax.ShapeDtypeStruct((m, SWA_Q_DIM), BF16),
            jax.ShapeDtypeStruct((m, 2 * SWA_KV_DIM), F32),
            jax.ShapeDtypeStruct((m, RWKV_IN), F32),
            jax.ShapeDtypeStruct((m, MEM_DIM), BF16),
        ),
        grid=(m // tm,),
        in_specs=[
            pl.BlockSpec((tm, d), row),
            pl.BlockSpec((1, d), lambda i: (0, 0)),
            pl.BlockSpec((d, SWA_Q_DIM), lambda i: (0, 0), pipeline_mode=pl.Buffered(1)),
            pl.BlockSpec((d, PROJ_DIM), lambda i: (0, 0), pipeline_mode=pl.Buffered(1)),
        ],
        out_specs=(
            pl.BlockSpec((tm, SWA_Q_DIM), row),
            pl.BlockSpec((tm, 2 * SWA_KV_DIM), row),
            pl.BlockSpec((tm, RWKV_IN), row),
            pl.BlockSpec((tm, MEM_DIM), row),
        ),
        compiler_params=_cparams("parallel"),
        name="inproj",
    )(h, g, wq, w)


def _norm_matmul_kernel(x_ref, g_ref, w_ref, o_ref):
    o_ref[...] = _dot(_rms(x_ref[...], g_ref[...]).astype(BF16), w_ref[...])


def _norm_matmul(x, g, w, *, tn=512):
    m, d = x.shape
    n = w.shape[1]
    return pl.pallas_call(
        _norm_matmul_kernel,
        out_shape=jax.ShapeDtypeStruct((m, n), F32),
        grid=(n // tn,),
        in_specs=[
            pl.BlockSpec((m, d), lambda j: (0, 0)),
            pl.BlockSpec((1, d), lambda j: (0, 0)),
            pl.BlockSpec((d, tn), lambda j: (0, j)),
        ],
        out_specs=pl.BlockSpec((m, tn), lambda j: (0, j)),
        compiler_params=_cparams("parallel"),
        name="norm_matmul",
    )(x, g, w)


def _sink_softmax(logits, sink):
    m = jnp.maximum(jnp.max(logits, axis=-1, keepdims=True), sink)
    p = jnp.exp(logits - m)
    denom = jnp.sum(p, axis=-1, keepdims=True) + jnp.exp(sink - m)
    return p * (1.0 / denom)


def _swa_prompt_kernel(q_ref, kvc_ref, kvp_ref, bias_ref, sink_ref, o_ref):
    i = pl.program_id(0)
    w = WINDOW
    rows = SWA_KV_HEADS * w
    k = jnp.concatenate([kvp_ref[:, 0:SWA_KV_DIM], kvc_ref[:, 0:SWA_KV_DIM]], axis=0).astype(BF16)
    v = jnp.concatenate([kvp_ref[:, SWA_KV_DIM:], kvc_ref[:, SWA_KV_DIM:]], axis=0).astype(BF16)
    qpos = lax.broadcasted_iota(jnp.int32, (rows, 2 * w), 0) % w
    col = lax.broadcasted_iota(jnp.int32, (rows, 2 * w), 1)
    dist = qpos + w - col
    valid = (dist >= 0) & (dist < w) & ((col >= w) | (i > 0))
    lane_head = lax.broadcasted_iota(jnp.int32, (w, SWA_KV_DIM), 1) // SWA_HEAD_DIM
    scale = SWA_HEAD_DIM ** -0.5
    groups = range(SWA_GROUP)
    logits = []
    for g in groups:
        qg = q_ref[:, g * SWA_KV_DIM:(g + 1) * SWA_KV_DIM].astype(F32)
        qs = jnp.concatenate([jnp.where(lane_head == kvh, qg, 0.0) for kvh in range(SWA_KV_HEADS)], axis=0)
        lg = _dot_nt(qs.astype(BF16), k)
        logits.append(jnp.where(valid, lg * scale + bias_ref[g], NEG_INF))
    m = [jnp.maximum(jnp.max(logits[g], axis=-1, keepdims=True), sink_ref[g]) for g in groups]
    p = [jnp.exp(logits[g] - m[g]) for g in groups]
    inv = [1.0 / (jnp.sum(p[g], axis=-1, keepdims=True) + jnp.exp(sink_ref[g] - m[g])) for g in groups]
    ov = [_dot((p[g] * inv[g]).astype(BF16), v) for g in groups]
    for g in groups:
        og = jnp.zeros((w, SWA_KV_DIM), F32)
        for kvh in range(SWA_KV_HEADS):
            og = jnp.where(lane_head == kvh, ov[g][kvh * w:(kvh + 1) * w], og)
        o_ref[:, g * SWA_KV_DIM:(g + 1) * SWA_KV_DIM] = og.astype(BF16)


def _swa_prompt(q, kv, bias, sink_rows):
    t = q.shape[0]
    w = WINDOW
    rows = SWA_KV_HEADS * w
    return pl.pallas_call(
        _swa_prompt_kernel,
        out_shape=jax.ShapeDtypeStruct((t, SWA_Q_DIM), BF16),
        grid=(t // w,),
        in_specs=[
            pl.BlockSpec((w, SWA_Q_DIM), lambda i: (i, 0)),
            pl.BlockSpec((w, 2 * SWA_KV_DIM), lambda i: (i, 0)),
            pl.BlockSpec((w, 2 * SWA_KV_DIM), lambda i: (jnp.maximum(i - 1, 0), 0)),
            pl.BlockSpec((SWA_GROUP, rows, 2 * w), lambda i: (0, 0, 0)),
            pl.BlockSpec((SWA_GROUP, rows, 1), lambda i: (0, 0, 0)),
        ],
        out_specs=pl.BlockSpec((w, SWA_Q_DIM), lambda i: (i, 0)),
        compiler_params=_cparams("parallel"),
        name="swa_prompt",
    )(q, kv, kv, bias, sink_rows)


def _swa_sample_kernel(q_ref, kvn_ref, kb_ref, vb_ref, bias_b_ref, bias_n_ref, sink_ref, o_ref, ko_ref, vo_ref,
                       *, bb, t):
    gt = SWA_GROUP * t
    rows = SWA_KV_HEADS * gt
    w = kb_ref.shape[2]
    scale = SWA_HEAD_DIM ** -0.5
    lane_head = lax.broadcasted_iota(jnp.int32, (gt, SWA_KV_DIM), 1) // SWA_HEAD_DIM
    tok = lax.broadcasted_iota(jnp.int32, (rows, w), 0) % t
    keyj = lax.broadcasted_iota(jnp.int32, (rows, w), 1)
    valid_b = (tok + w - keyj) < WINDOW
    tok_n = lax.broadcasted_iota(jnp.int32, (rows, 1), 0) % t
    sink = sink_ref[...]
    bs = range(bb)
    toks = range(t)
    qall = [jnp.concatenate([jnp.where(lane_head == kvh, q_ref[b].astype(F32), 0.0) for kvh in range(SWA_KV_HEADS)],
                            axis=0) for b in bs]
    kvn = [kvn_ref[b] for b in bs]
    lb = [_dot(qall[b].astype(BF16), kb_ref[b].astype(BF16)) for b in bs]
    lb = [jnp.where(valid_b, lb[b] * scale + bias_b_ref[...], NEG_INF) for b in bs]
    ln = [[jnp.sum(qall[b] * kvn[b][j:j + 1, 0:SWA_KV_DIM], axis=-1, keepdims=True) for j in toks] for b in bs]
    ln = [[jnp.where(tok_n >= j, ln[b][j] * scale + bias_n_ref[:, j:j + 1], NEG_INF) for j in toks] for b in bs]
    m = [jnp.maximum(jnp.max(lb[b], axis=-1, keepdims=True), sink) for b in bs]
    m = [functools.reduce(jnp.maximum, ln[b], m[b]) for b in bs]
    pb = [jnp.exp(lb[b] - m[b]) for b in bs]
    pn = [[jnp.exp(ln[b][j] - m[b]) for j in toks] for b in bs]
    denom = [jnp.sum(pb[b], axis=-1, keepdims=True) + jnp.exp(sink - m[b]) for b in bs]
    inv = [1.0 / functools.reduce(jnp.add, pn[b], denom[b]) for b in bs]
    oall = [_dot_nt((pb[b] * inv[b]).astype(BF16), vb_ref[b].astype(BF16)) for b in bs]
    for b in bs:
        ob = oall[b]
        for j in toks:
            ob = ob + (pn[b][j] * inv[b]) * kvn[b][j:j + 1, SWA_KV_DIM:]
        og = jnp.zeros((gt, SWA_KV_DIM), F32)
        for kvh in range(SWA_KV_HEADS):
            og = jnp.where(lane_head == kvh, ob[kvh * gt:(kvh + 1) * gt], og)
        o_ref[b] = og.astype(BF16)
    pos = lax.broadcasted_iota(jnp.int32, (SWA_KV_DIM, w), 1)
    pad = jnp.zeros((w - SUBLANES, 2 * SWA_KV_DIM), F32)
    row8 = lax.broadcasted_iota(jnp.int32, (SUBLANES, 2 * SWA_KV_DIM), 0)
    for b in bs:
        last8 = jnp.zeros((SUBLANES, 2 * SWA_KV_DIM), F32)
        for j in toks:
            last8 = jnp.where(row8 == SUBLANES - t + j, kvn[b][j:j + 1], last8)
        tail_t = jnp.concatenate([pad, last8], axis=0).T
        ko_ref[b] = jnp.where(pos >= w - t, tail_t[:SWA_KV_DIM], pltpu.roll(kb_ref[b], w - t, axis=1))
        vo_ref[b] = jnp.where(pos >= w - t, tail_t[SWA_KV_DIM:], pltpu.roll(vb_ref[b], w - t, axis=1))


def _swa_sample(q, kvn, kbuf, vbuf, bias_b, bias_n, sink_rows, *, bb=8):
    b, gt, _ = q.shape
    t = kvn.shape[1]
    w = kbuf.shape[2]
    rows = SWA_KV_HEADS * gt
    blk = lambda i: (i, 0, 0)
    const = lambda i: (0, 0)
    cache = pl.BlockSpec((bb, SWA_KV_DIM, w), blk)
    return pl.pallas_call(
        functools.partial(_swa_sample_kernel, bb=bb, t=t),
        out_shape=(jax.ShapeDtypeStruct((b, gt, SWA_KV_DIM), BF16),
                   jax.ShapeDtypeStruct(kbuf.shape, F32), jax.ShapeDtypeStruct(vbuf.shape, F32)),
        grid=(b // bb,),
        in_specs=[
            pl.BlockSpec((bb, gt, SWA_KV_DIM), blk),
            pl.BlockSpec((bb, t, 2 * SWA_KV_DIM), blk),
            cache, cache,
            pl.BlockSpec((rows, w), const),
            pl.BlockSpec((rows, t), const),
            pl.BlockSpec((rows, 1), const),
        ],
        out_specs=(pl.BlockSpec((bb, gt, SWA_KV_DIM), blk), cache, cache),
        compiler_params=_cparams("parallel"),
        name="swa_sample",
    )(q, kvn, kbuf, vbuf, bias_b, bias_n, sink_rows)


def _softmax_rows(x):
    m = jnp.max(x, axis=-1, keepdims=True)
    p = jnp.exp(x - m)
    return p * (1.0 / jnp.sum(p, axis=-1, keepdims=True))


def _mem_heads(q, mk, mv):
    scale = MEM_HEAD_DIM ** -0.5
    outs = []
    for h in range(MEM_HEADS):
        sl = slice(h * MEM_HEAD_DIM, (h + 1) * MEM_HEAD_DIM)
        p = _softmax_rows(_dot_nt(q[:, sl], mk[:, sl]) * scale)
        outs.append(_dot(p.astype(BF16), mv[:, sl]))
    return jnp.concatenate(outs, axis=-1)


def _mem_prompt_kernel(q_ref, mk_ref, mv_ref, o_ref):
    o_ref[...] = _mem_heads(q_ref[...], mk_ref[...].astype(BF16), mv_ref[...].astype(BF16)).astype(BF16)


def _mem_prompt(q, mkv, *, tm=512):
    m = q.shape[0]
    return pl.pallas_call(
        _mem_prompt_kernel,
        out_shape=jax.ShapeDtypeStruct((m, MEM_DIM), BF16),
        grid=(m // tm,),
        in_specs=[
            pl.BlockSpec((tm, MEM_DIM), lambda i: (i, 0)),
            pl.BlockSpec((N_MEM, MEM_DIM), lambda i: (0, 0)),
            pl.BlockSpec((N_MEM, MEM_DIM), lambda i: (0, 1)),
        ],
        out_specs=pl.BlockSpec((tm, MEM_DIM), lambda i: (i, 0)),
        compiler_params=_cparams("parallel"),
        name="mem_prompt",
    )(q, mkv, mkv)


def _mem_sample_kernel(q_ref, mk_ref, mv_ref, o_ref, *, bb):
    scale = MEM_HEAD_DIM ** -0.5
    pairs = [(b, h) for b in range(bb) for h in range(MEM_HEADS)]
    rows = lambda h: pl.ds(h, N_MEM, stride=MEM_HEADS)
    cols = lambda h: slice(h * MEM_HEAD_DIM, (h + 1) * MEM_HEAD_DIM)
    q = [q_ref[b] for b in range(bb)]
    x = [_dot_nt(q[b][:, cols(h)], mk_ref[b, rows(h), :].astype(BF16)) * scale for b, h in pairs]
    m = [jnp.max(xi, axis=-1, keepdims=True) for xi in x]
    p = [jnp.exp(xi - mi) for xi, mi in zip(x, m)]
    inv = [1.0 / jnp.sum(pi, axis=-1, keepdims=True) for pi in p]
    o = [_dot((pi * ii).astype(BF16), mv_ref[b, rows(h), :].astype(BF16)) for (b, h), pi, ii in zip(pairs, p, inv)]
    for b in range(bb):
        o_ref[b] = jnp.concatenate(o[b * MEM_HEADS:(b + 1) * MEM_HEADS], axis=-1).astype(BF16)


def _mem_sample(q, mk, mv, *, bb=8):
    b, t, _ = q.shape
    blk = lambda i: (i, 0, 0)
    return pl.pallas_call(
        functools.partial(_mem_sample_kernel, bb=bb),
        out_shape=jax.ShapeDtypeStruct((b, t, MEM_DIM), BF16),
        grid=(b // bb,),
        in_specs=[
            pl.BlockSpec((bb, t, MEM_DIM), blk),
            pl.BlockSpec((bb, N_MEM * MEM_HEADS, MEM_HEAD_DIM), blk),
            pl.BlockSpec((bb, N_MEM * MEM_HEADS, MEM_HEAD_DIM), blk),
        ],
        out_specs=pl.BlockSpec((bb, t, MEM_DIM), blk),
        compiler_params=_cparams("parallel"),
        name="mem_sample",
    )(q, mk, mv)


def _head_sum(x, ones_ref):
    hi = x.astype(BF16)
    lo = (x - hi.astype(F32)).astype(BF16)
    return _dot(hi, ones_ref[...]) + _dot(lo, ones_ref[...])


def _rwkv_prep_kernel(x_ref, pre_ref, start_ref, mu_ref, w0_ref, ww2_ref, a0_ref, aw2_ref, gw2_ref, kk_ref, ka_ref,
                      rk_ref, ones_ref, r_o, w_o, k_o, v_o, kk_o, kka_o, g_o, bonus_o, *, seq, tm):
    x = x_ref[...]
    row = lax.broadcasted_iota(jnp.int32, x.shape, 0)
    shifted = pltpu.roll(x, 1, axis=0)
    if seq >= tm:
        is_start = (pl.program_id(0) * tm) % seq == 0
        first_prev = jnp.where(is_start, start_ref[0], pre_ref[SUBLANES - 1:SUBLANES, :])
        prev = jnp.where(row == 0, first_prev, shifted)
    else:
        prev = jnp.where(row % seq == 0, start_ref[...], shifted)
    xs = x + mu_ref[...] * (prev - x)
    d = RWKV_DIM
    r = xs[:, 0:d]
    k = xs[:, d:2 * d]
    v = xs[:, 2 * d:3 * d]
    lw = xs[:, 3 * d:3 * d + LORA_W]
    la = xs[:, 3 * d + LORA_W:3 * d + LORA_W + LORA_A]
    lg = xs[:, 3 * d + LORA_W + LORA_A:]
    wpre = w0_ref[...] + _dot_hi(jnp.tanh(lw), ww2_ref[...])
    w = -jax.nn.softplus(-wpre) - 0.5
    log_decay = -jnp.exp(w)
    a = jax.nn.sigmoid(a0_ref[...] + _dot_hi(la, aw2_ref[...]))
    g = _dot_hi(jax.nn.sigmoid(lg), gw2_ref[...])
    kk = k * kk_ref[...]
    kk = kk / jnp.maximum(jnp.sqrt(_head_sum(kk * kk, ones_ref)), 1e-12)
    kh = k * (1.0 + (a - 1.0) * ka_ref[...])
    r_o[...] = r
    w_o[...] = log_decay
    k_o[...] = kh
    v_o[...] = v
    kk_o[...] = kk
    kka_o[...] = kk * a
    g_o[...] = g
    bonus_o[...] = _head_sum(r * kh * rk_ref[...], ones_ref) * v


def _rwkv_prep(x, shift0, p, ones, *, seq, tm=256):
    m = x.shape[0]
    row = lambda i: (i, 0)
    const = lambda i: (0, 0)
    vec = lambda n: pl.BlockSpec((1, n), const)
    out = jax.ShapeDtypeStruct((m, RWKV_DIM), F32)
    if seq >= tm:
        assert seq % tm == 0
        start = shift0
        start_spec = pl.BlockSpec((1, 1, RWKV_IN), lambda i: ((i * tm) // seq, 0, 0))
    else:
        assert tm % seq == 0
        start = jnp.repeat(shift0[:, 0], seq, axis=0)
        start_spec = pl.BlockSpec((tm, RWKV_IN), row)
    pre_blocks = tm // SUBLANES
    return pl.pallas_call(
        functools.partial(_rwkv_prep_kernel, seq=seq, tm=tm),
        out_shape=(out,) * 8,
        grid=(m // tm,),
        in_specs=[
            pl.BlockSpec((tm, RWKV_IN), row),
            pl.BlockSpec((SUBLANES, RWKV_IN), lambda i: (jnp.maximum(i * pre_blocks - 1, 0), 0)),
            start_spec,
            vec(RWKV_IN), vec(RWKV_DIM),
            pl.BlockSpec((LORA_W, RWKV_DIM), const),
            vec(RWKV_DIM),
            pl.BlockSpec((LORA_A, RWKV_DIM), const),
            pl.BlockSpec((LORA_G, RWKV_DIM), const),
            vec(RWKV_DIM), vec(RWKV_DIM), vec(RWKV_DIM),
            pl.BlockSpec((RWKV_DIM, RWKV_DIM), const),
        ],
        out_specs=(pl.BlockSpec((tm, RWKV_DIM), row),) * 8,
        compiler_params=_cparams("parallel"),
        name="rwkv_prep",
    )(x, x, start, p["mu"], p["w0"], p["w_w2"], p["a0"], p["a_w2"], p["g_w2"], p["k_k"], p["k_a"], p["r_k"], ones)


def _half_sums(x, low):
    lo = jnp.sum(jnp.where(low, x, 0.0), axis=-1, keepdims=True)
    hi = jnp.sum(jnp.where(low, 0.0, x), axis=-1, keepdims=True)
    return jnp.where(low, lo, hi)


def _rwkv_scan_kernel(r_ref, w_ref, k_ref, v_ref, kk_ref, kka_ref, s0_ref, y_ref, so_ref, s_scr, *, bb, tt):
    jt = pl.program_id(1)
    group = min(tt, SUBLANES)
    n = RWKV_HEAD_DIM
    pairs = RWKV_HEADS // 2
    low = lax.broadcasted_iota(jnp.int32, (n, LANES), 1) < n
    diag = (lax.broadcasted_iota(jnp.int32, (n, LANES), 1) % n) == lax.broadcasted_iota(jnp.int32, (n, LANES), 0)

    @pl.when(jt == 0)
    def _():
        for b in range(bb):
            for h in range(RWKV_HEADS):
                s_scr[b, h // 2, :, (h % 2) * n:(h % 2 + 1) * n] = s0_ref[b, h]

    nb = 2 if bb % 2 == 0 else 1
    chains = [(bi, p) for bi in range(nb) for p in range(pairs)]

    def batch_body(b0, carry):
        def group_body(t0, states):
            toks = pl.ds(pl.multiple_of(t0 * group, group), group)
            at = lambda ref: [ref[b0 * nb + bi, toks, pl.ds(p * LANES, LANES)] for bi, p in chains]
            r, k, v, kk, kka = at(r_ref), at(k_ref), at(v_ref), at(kk_ref), at(kka_ref)
            w = [jnp.exp(x) for x in at(w_ref)]
            s = list(states)
            ys = [[] for _ in chains]
            for j in range(group):
                one = slice(j, j + 1)
                sa = [_half_sums(s[c] * (-kk[c][one]), low) for c in range(len(chains))]
                vcol = [_half_sums(jnp.where(diag, v[c][one], 0.0), low) for c in range(len(chains))]
                s = [s[c] * w[c][one] + sa[c] * kka[c][one] + vcol[c] * k[c][one] for c in range(len(chains))]
                ycol = [_half_sums(s[c] * r[c][one], low) for c in range(len(chains))]
                for c in range(len(chains)):
                    ys[c].append(jnp.sum(jnp.where(diag, ycol[c], 0.0), axis=0, keepdims=True))
            for c, (bi, p) in enumerate(chains):
                y_ref[b0 * nb + bi, toks, pl.ds(p * LANES, LANES)] = jnp.concatenate(ys[c], axis=0)
            return tuple(s)

        states = tuple(s_scr[b0 * nb + bi, p] for bi, p in chains)
        states = lax.fori_loop(0, tt // group, group_body, states)
        for c, (bi, p) in enumerate(chains):
            s_scr[b0 * nb + bi, p] = states[c]
        return carry

    lax.fori_loop(0, bb // nb, batch_body, 0)

    @pl.when(jt == pl.num_programs(1) - 1)
    def _():
        for b in range(bb):
            for h in range(RWKV_HEADS):
                so_ref[b, h] = s_scr[b, h // 2, :, (h % 2) * n:(h % 2 + 1) * n]


def _rwkv_scan(r, w, k, v, kk, kka, s0, *, bb, tt):
    b, t, d = r.shape
    n = RWKV_HEAD_DIM
    tok = pl.BlockSpec((bb, tt, d), lambda i, j: (i, j, 0))
    st = pl.BlockSpec((bb, RWKV_HEADS, n, n), lambda i, j: (i, 0, 0, 0))
    return pl.pallas_call(
        functools.partial(_rwkv_scan_kernel, bb=bb, tt=tt),
        out_shape=(jax.ShapeDtypeStruct((b, t, d), F32), jax.ShapeDtypeStruct((b, RWKV_HEADS, n, n), F32)),
        grid=(b // bb, t // tt),
        in_specs=[tok] * 6 + [st],
        out_specs=(tok, st),
        scratch_shapes=[pltpu.VMEM((bb, RWKV_HEADS // 2, n, LANES), F32)],
        compiler_params=_cparams("parallel", "arbitrary"),
        name="rwkv_scan",
    )(r, w, k, v, kk, kka, s0)


def _rwkv_lanes_kernel(r_ref, lw_ref, k_ref, v_ref, kk_ref, kka_ref, s_ref, y_ref, so_ref, v_scr, y_scr, *, t):
    n = RWKV_HEAD_DIM
    nb = s_ref.shape[-1]
    heads = range(2)

    def token_major(ref, j):
        return ref[pl.ds(j, nb, stride=t), :].T

    for j in range(t):
        v_scr[...] = token_major(v_ref, j)
        r_t, k_t, kk_t, kka_t = (token_major(ref, j) for ref in (r_ref, k_ref, kk_ref, kka_ref))
        w_t = jnp.exp(token_major(lw_ref, j))
        src = s_ref if j == 0 else so_ref

        def value_group(g, carry):
            rows = pl.multiple_of(g * SUBLANES, SUBLANES)
            ys = [[] for _ in heads]
            vg = [v_scr[pl.ds(h * n + rows, SUBLANES), :] for h in heads]
            for i in range(SUBLANES):
                for h in heads:
                    f = slice(h * n, (h + 1) * n)
                    s = src[h, rows + i]
                    sa = jnp.sum(s * (-kk_t[f]), axis=0, keepdims=True)
                    s = s * w_t[f] + sa * kka_t[f] + vg[h][i:i + 1] * k_t[f]
                    so_ref[h, rows + i] = s
                    ys[h].append(jnp.sum(s * r_t[f], axis=0, keepdims=True))
            for h in heads:
                y_scr[pl.ds(h * n + rows, SUBLANES), :] = jnp.concatenate(ys[h], axis=0)
            return carry

        lax.fori_loop(0, n // SUBLANES, value_group, 0)
        y_ref[j] = y_scr[...].T


def _rwkv_lanes(r, lw, k, v, kk, kka, s0, *, t):
    m, d = r.shape
    nb = m // t
    n = RWKV_HEAD_DIM
    assert nb == LANES, "one batch per lane"
    tok = pl.BlockSpec((m, LANES), lambda p: (0, p))
    st = pl.BlockSpec((2, n, n, nb), lambda p: (p, 0, 0, 0))
    return pl.pallas_call(
        functools.partial(_rwkv_lanes_kernel, t=t),
        out_shape=(jax.ShapeDtypeStruct((t, nb, d), F32), jax.ShapeDtypeStruct((RWKV_HEADS, n, n, nb), F32)),
        grid=(RWKV_HEADS // 2,),
        in_specs=[tok] * 6 + [st],
        out_specs=(pl.BlockSpec((t, nb, LANES), lambda p: (0, 0, p)), st),
        scratch_shapes=[pltpu.VMEM((LANES, nb), F32), pltpu.VMEM((LANES, nb), F32)],
        compiler_params=_cparams("parallel"),
        name="rwkv_lanes",
    )(r, lw, k, v, kk, kka, s0)


CHUNK = 64
GROUP_HEADS = 4
GROUP_W = GROUP_HEADS * RWKV_HEAD_DIM
N_GROUPS = RWKV_HEADS // GROUP_HEADS
(MASK_SAME, MASK_STRICT, MASK_INCL, MASK_LEVEL0) = (0, 1, 2, 3)
N_LEVELS = int(math.log2(CHUNK))


def _chunk_masks():
    i = np.arange(GROUP_W)
    same = (i[:, None] // CHUNK) == (i[None, :] // CHUNK)
    masks = [same, same & (i[None, :] < i[:, None]), same & (i[None, :] <= i[:, None])]
    for lvl in range(N_LEVELS):
        m = 1 << lvl
        masks.append(((i[:, None] // (2 * m)) == (i[None, :] // (2 * m))) & ((i[:, None] // m) != (i[None, :] // m))
                     & (i[None, :] < i[:, None]))
    return np.stack(masks).astype(np.float32)


def _rwkv_chunk_kernel(r_ref, lw_ref, k_ref, v_ref, kk_ref, kka_ref, st0_ref, tri_ref, eye_ref, mask_ref,
                       y_ref, sto_ref, st_scr):
    @pl.when(pl.program_id(0) == 0)
    def _():
        st_scr[...] = st0_ref[...]

    eye = eye_ref[...]
    eye_b = eye.astype(BF16)
    tri = tri_ref[...]
    tile_rows = lambda x: jnp.concatenate([x] * GROUP_HEADS, axis=0)
    block_diag = lambda x: (tile_rows(x) * mask_ref[MASK_SAME]).astype(BF16)

    n_chunks = y_ref.shape[0] // CHUNK
    chains = [(slice(c * CHUNK, (c + 1) * CHUNK), slice(g * GROUP_W, (g + 1) * GROUP_W))
              for c in range(n_chunks) for g in range(N_GROUPS)]
    each = lambda f, *cols: [f(*args) for args in zip(*cols)]
    same, strict, incl = mask_ref[MASK_SAME], mask_ref[MASK_STRICT], mask_ref[MASK_INCL]

    def cum_decay(lw):
        h1 = lw.astype(BF16)
        r1 = lw - h1.astype(F32)
        h2 = r1.astype(BF16)
        h3 = (r1 - h2.astype(F32)).astype(BF16)
        return _dot(tri, h1) + _dot(tri, h2) + _dot(tri, h3)

    lw = [lw_ref[rows, sl] for rows, sl in chains]
    kka = [kka_ref[rows, sl] for rows, sl in chains]
    k = [k_ref[rows, sl] for rows, sl in chains]
    cum = each(cum_decay, lw)
    cum_last = each(lambda c: c[CHUNK - 1:CHUNK, :], cum)
    p_inv = each(lambda c: jnp.exp(-c), cum)
    p_tail = each(lambda c, cl: jnp.exp(cl - c), cum, cum_last)
    a_bd = [block_diag(-kk_ref[rows, sl] * jnp.exp(c - l)) for (rows, sl), c, l in zip(chains, cum, lw)]
    r_f = [tile_rows(r_ref[rows, sl] * jnp.exp(c)) * same for (rows, sl), c in zip(chains, cum)]
    r_bd = each(lambda x: x.astype(BF16), r_f)
    v_bd = [block_diag(v_ref[rows, sl]) for rows, sl in chains]
    b_rep = each(lambda x, p: tile_rows((x * p).astype(BF16)), kka, p_inv)
    k_rep = each(lambda x, p: tile_rows((x * p).astype(BF16)), k, p_inv)
    bh_rep = each(lambda x, p: tile_rows((x * p).astype(BF16)), kka, p_tail)
    kh_rep = each(lambda x, p: tile_rows((x * p).astype(BF16)), k, p_tail)

    l_ab_f = each(lambda a, b: _dot_nt(a, b) * strict, a_bd, b_rep)
    l_ab = each(lambda x: x.astype(BF16), l_ab_f)
    l_ak = each(lambda a, b: (_dot_nt(a, b) * strict).astype(BF16), a_bd, k_rep)
    m_rb = each(lambda a, b: (_dot_nt(a, b) * incl).astype(BF16), r_bd, b_rep)
    m_rk = each(lambda a, b: (_dot_nt(a, b) * incl).astype(BF16), r_bd, k_rep)
    bh_t = each(lambda x: (_dot_nt(eye_b, x) * same).astype(BF16), bh_rep)
    kh_t = each(lambda x: (_dot_nt(eye_b, x) * same).astype(BF16), kh_rep)

    d = each(lambda l: eye + l * mask_ref[MASK_LEVEL0], l_ab_f)
    for lvl in range(1, N_LEVELS):
        d_b = each(lambda x: x.astype(BF16), d)
        x = each(lambda l, db: (_dot(l, db) * mask_ref[MASK_LEVEL0 + lvl]).astype(BF16), l_ab, d_b)
        d = each(lambda dd, db, xx: dd + _dot(db, xx), d, d_b, x)
    t_b = each(lambda x: x.astype(BF16), d)

    wm = each(lambda a, b, vv: _dot(jnp.concatenate([a, b], axis=0), vv), l_ak, m_rk, v_bd)
    twa = each(lambda t, w, a: _dot(t, jnp.concatenate([w[:GROUP_W].astype(BF16), a], axis=1)).astype(BF16),
               t_b, wm, a_bd)
    ry = each(_dot, m_rb, twa)
    mn = each(_dot, bh_t, twa)
    khv = each(_dot, kh_t, v_bd)
    y0 = each(lambda a, w: a[:, :GROUP_W] + w[GROUP_W:], ry, wm)
    n_x = each(lambda a, b: a[:, :GROUP_W] + b, mn, khv)
    mr = each(lambda a, cl, rf, b: jnp.concatenate(
        [(eye * jnp.exp(cl) + a[:, GROUP_W:]).astype(BF16), (rf + b[:, GROUP_W:]).astype(BF16)], axis=0),
        mn, cum_last, r_f, ry)

    st = [st_scr[g] for g in range(N_GROUPS)]
    for i, (rows, sl) in enumerate(chains):
        g = i % N_GROUPS
        ys = _dot(mr[i], st[g].astype(BF16))
        st[g] = ys[:GROUP_W] + n_x[i]
        y_bd = ys[GROUP_W:] + y0[i]
        y = y_bd[0:CHUNK]
        for h in range(1, GROUP_HEADS):
            y = y + y_bd[h * CHUNK:(h + 1) * CHUNK]
        y_ref[rows, sl] = y
    for g in range(N_GROUPS):
        st_scr[g] = st[g]

    @pl.when(pl.program_id(0) == pl.num_programs(0) - 1)
    def _():
        sto_ref[...] = st_scr[...]


def _rwkv_chunked(r, lw, k, v, kk, kka, s0, *, chunks_per_step=4):
    t, d = r.shape
    n = RWKV_HEAD_DIM
    tt = CHUNK * chunks_per_step
    assert CHUNK == n and t % tt == 0
    st0 = jnp.einsum("ghvk,hj->ghkjv", s0.reshape(N_GROUPS, GROUP_HEADS, n, n), jnp.eye(GROUP_HEADS, dtype=F32))
    st0 = st0.reshape(N_GROUPS, GROUP_W, GROUP_W)
    tri = jnp.asarray(np.tril(np.ones((CHUNK, CHUNK), np.float32)), BF16)
    eye = jnp.eye(GROUP_W, dtype=F32)
    masks = jnp.asarray(_chunk_masks())
    tok = pl.BlockSpec((tt, d), lambda c: (c, 0))
    st_spec = pl.BlockSpec((N_GROUPS, GROUP_W, GROUP_W), lambda c: (0, 0, 0))
    y, st = pl.pallas_call(
        _rwkv_chunk_kernel,
        out_shape=(jax.ShapeDtypeStruct((t, d), F32), jax.ShapeDtypeStruct((N_GROUPS, GROUP_W, GROUP_W), F32)),
        grid=(t // tt,),
        in_specs=[tok] * 6 + [st_spec, pl.BlockSpec((CHUNK, CHUNK), lambda c: (0, 0)),
                              pl.BlockSpec((GROUP_W, GROUP_W), lambda c: (0, 0)),
                              pl.BlockSpec(masks.shape, lambda c: (0, 0, 0))],
        out_specs=(tok, st_spec),
        scratch_shapes=[pltpu.VMEM((N_GROUPS, GROUP_W, GROUP_W), F32)],
        compiler_params=_cparams("arbitrary"),
        name="rwkv_chunk",
    )(r, lw, k, v, kk, kka, st0, tri, eye, masks)
    st5 = st.reshape(N_GROUPS, GROUP_HEADS, n, GROUP_HEADS, n)
    s_new = jnp.einsum("ghkjv,hj->ghvk", st5, jnp.eye(GROUP_HEADS, dtype=F32)).reshape(RWKV_HEADS, n, n)
    return y, s_new


def _rwkv_post_kernel(y_ref, bonus_ref, g_ref, lnw_ref, lnb_ref, ones_ref, o_ref):
    y = y_ref[...]
    inv_n = 1.0 / RWKV_HEAD_DIM
    mu = _head_sum(y, ones_ref) * inv_n
    dlt = y - mu
    var = _head_sum(dlt * dlt, ones_ref) * inv_n
    yn = dlt * lax.rsqrt(var + GN_EPS) * lnw_ref[...] + lnb_ref[...]
    o_ref[...] = ((yn + bonus_ref[...]) * g_ref[...]).astype(BF16)


def _rwkv_post(y, bonus, g, lnw, lnb, ones, *, tm=256):
    m = y.shape[0]
    row = lambda i: (i, 0)
    const = lambda i: (0, 0)
    tile = pl.BlockSpec((tm, RWKV_DIM), row)
    return pl.pallas_call(
        _rwkv_post_kernel,
        out_shape=jax.ShapeDtypeStruct((m, RWKV_DIM), BF16),
        grid=(m // tm,),
        in_specs=[tile, tile, tile, pl.BlockSpec((1, RWKV_DIM), const), pl.BlockSpec((1, RWKV_DIM), const),
                  pl.BlockSpec((RWKV_DIM, RWKV_DIM), const)],
        out_specs=tile,
        compiler_params=_cparams("parallel"),
        name="rwkv_post",
    )(y, bonus, g, lnw, lnb, ones)


GATE_BLOCK = math.gcd(PROJ_DIM, D_MODEL)


def _merge_kernel(h_ref, g_ref, os_ref, or_ref, om_ref, wos_ref, wor_ref, wom_ref, wout_ref, *rest, parts):
    gate_refs, (o_ref, u_ref, acc_ref) = rest[:N_BRANCH * parts], rest[N_BRANCH * parts:]
    j = pl.program_id(1)

    @pl.when(j == 0)
    def _():
        u_ref[...] = _rms(h_ref[...], g_ref[...]).astype(BF16)
        acc_ref[...] = jnp.zeros_like(acc_ref)

    u = u_ref[...]
    merged = None
    for b, (x_ref, w_ref) in enumerate(((os_ref, wos_ref), (or_ref, wor_ref), (om_ref, wom_ref))):
        gate = jnp.concatenate([_dot(u, gate_refs[b * parts + c][...]) for c in range(parts)], axis=1)
        term = jax.nn.sigmoid(gate) * _dot(x_ref[...], w_ref[...])
        merged = term if merged is None else merged + term
    acc_ref[...] += _dot(merged.astype(BF16), wout_ref[...])

    @pl.when(j == pl.num_programs(1) - 1)
    def _():
        o_ref[...] = h_ref[...] + acc_ref[...]


def _merge(h, g, o_swa, o_rw, o_mem, w_in, wo_swa, wo_rw, wo_mem, w_out, *, tm=512, tn=512):
    m, d = h.shape
    nt = d // tn
    parts = tn // GATE_BLOCK
    g0 = PROJ_DIM // GATE_BLOCK
    row = lambda i, j: (i, 0)
    col = lambda i, j: (0, j)
    gate_specs = [pl.BlockSpec((d, GATE_BLOCK), functools.partial(
        lambda i, j, off: (0, off + j * parts), off=g0 + b * (d // GATE_BLOCK) + c))
        for b in range(N_BRANCH) for c in range(parts)]
    return pl.pallas_call(
        functools.partial(_merge_kernel, parts=parts),
        out_shape=jax.ShapeDtypeStruct((m, d), F32),
        grid=(m // tm, nt),
        in_specs=[
            pl.BlockSpec((tm, d), row),
            pl.BlockSpec((1, d), lambda i, j: (0, 0)),
            pl.BlockSpec((tm, SWA_Q_DIM), row),
            pl.BlockSpec((tm, RWKV_DIM), row),
            pl.BlockSpec((tm, MEM_DIM), row),
            pl.BlockSpec((SWA_Q_DIM, tn), col),
            pl.BlockSpec((RWKV_DIM, tn), col),
            pl.BlockSpec((MEM_DIM, tn), col),
            pl.BlockSpec((tn, d), lambda i, j: (j, 0)),
        ] + gate_specs,
        out_specs=pl.BlockSpec((tm, d), row),
        scratch_shapes=[pltpu.VMEM((tm, d), BF16), pltpu.VMEM((tm, d), F32)],
        compiler_params=_cparams("parallel", "arbitrary"),
        name="merge",
    )(h, g, o_swa, o_rw, o_mem, wo_swa, wo_rw, wo_mem, w_out, *([w_in] * (N_BRANCH * parts)))


def _t5_bucket(dist):
    max_exact = N_BUCKETS // 2
    d = np.maximum(dist, 0)
    log_ratio = (np.log(np.maximum(d, 1).astype(np.float32) / np.float32(max_exact))
                 / np.float32(math.log(MAX_DISTANCE / max_exact)))
    large = np.minimum(max_exact + (log_ratio * (N_BUCKETS - max_exact)).astype(np.int32), N_BUCKETS - 1)
    return np.where(d < max_exact, d, large).astype(np.int32)


def _rel_bias(table, dist):
    onehot = np.eye(N_BUCKETS, dtype=np.float32)[_t5_bucket(dist).reshape(-1)]
    bias = jnp.einsum("nb,bh->hn", jnp.asarray(onehot), table, precision=lax.Precision.HIGHEST)
    return bias.reshape(SWA_HEADS, *dist.shape)


def _rwkv_branch(xr, shift0, s0, p, ones, *, bb, tt):
    b, t, _ = xr.shape
    flat = lambda z: z.reshape(b * t, z.shape[-1])
    r, w, k, v, kk, kka, g, bonus = _rwkv_prep(flat(xr), shift0, p, ones, seq=t)
    if b == 1:
        y, s_new = _rwkv_chunked(r, w, k, v, kk, kka, s0[0])
        s_new = s_new[None]
    elif b == LANES:
        y, s_new = _rwkv_lanes(r, w, k, v, kk, kka, jnp.transpose(s0, (1, 2, 3, 0)), t=t)
        y = jnp.transpose(y, (1, 0, 2))
        s_new = jnp.transpose(s_new, (3, 0, 1, 2))
    else:
        seq = lambda z: z.reshape(b, t, RWKV_DIM)
        y, s_new = _rwkv_scan(seq(r), seq(w), seq(k), seq(v), seq(kk), seq(kka), s0, bb=bb, tt=tt)
    o = _rwkv_post(flat(y), bonus, g, p["ln_w"], p["ln_b"], ones)
    return o, s_new


def kernel(x_prompt, mem_prompt, x_sample, cache_swa_k, cache_swa_v, state_rwkv, state_rwkv_shift, cache_mem_k, cache_mem_v, ffn1_norm, ffn1_wi, ffn1_wo, mix_norm, w_in, swa_sinks, rel_bias_table, rwkv_mu, rwkv_w0, rwkv_w_w2, rwkv_a0, rwkv_a_w2, rwkv_g_w2, rwkv_k_k, rwkv_k_a, rwkv_r_k, rwkv_ln_w, rwkv_ln_b, mem_norm, w_mem_kv, w_o_swa, w_o_rwkv, w_o_mem, w_out, ffn2_norm, ffn2_wi, ffn2_wo, final_norm):
    assert ffn1_wi.shape[0] == 1, "single-layer trunk"
    bp, tp, d = x_prompt.shape
    bs, ts, _ = x_sample.shape
    assert bp == 1
    row = lambda z: z.reshape(1, -1).astype(F32)

    w_in_b = w_in[0].astype(BF16)
    w_q = w_in_b[:, :SWA_Q_DIM].reshape(d, SWA_KV_HEADS, SWA_GROUP, SWA_HEAD_DIM).transpose(0, 2, 1, 3).reshape(d, SWA_Q_DIM)
    wo_swa = w_o_swa[0].astype(BF16).reshape(SWA_KV_HEADS, SWA_GROUP, SWA_HEAD_DIM, d).transpose(1, 0, 2, 3).reshape(SWA_Q_DIM, d)
    wo_rw = w_o_rwkv[0].astype(BF16)
    wo_mem = w_o_mem[0].astype(BF16)
    w_out_b = w_out[0].astype(BF16)
    wi1, wo1 = ffn1_wi[0], ffn1_wo[0]
    wi2, wo2 = ffn2_wi[0], ffn2_wo[0]
    g1, gm, g2, gf = row(ffn1_norm[0]), row(mix_norm[0]), row(ffn2_norm[0]), row(final_norm)
    rp = {
        "mu": row(rwkv_mu[0]), "w0": row(rwkv_w0[0]), "w_w2": rwkv_w_w2[0], "a0": row(rwkv_a0[0]),
        "a_w2": rwkv_a_w2[0], "g_w2": rwkv_g_w2[0], "k_k": row(rwkv_k_k[0]), "k_a": row(rwkv_k_a[0]),
        "r_k": row(rwkv_r_k[0]), "ln_w": row(rwkv_ln_w[0]), "ln_b": row(rwkv_ln_b[0]),
    }
    seg = np.arange(RWKV_DIM) // RWKV_HEAD_DIM
    ones = jnp.asarray(seg[:, None] == seg[None, :], dtype=BF16)
    sinks = swa_sinks[0].astype(F32)
    table = rel_bias_table.astype(F32)

    xp = x_prompt.reshape(tp, d)
    xs = x_sample.reshape(bs * ts, d)
    hp = _ffn(xp, g1, wi1, wo1, gf, final_norm=False)
    hs = _ffn(xs, g1, wi1, wo1, gf, final_norm=False)
    qp, kvp, xrp, qmp = _inproj(hp, gm, w_q, w_in_b)
    qs, kvs, xrs, qms = _inproj(hs, gm, w_q, w_in_b)

    w = WINDOW
    dist_p = np.arange(w)[:, None] + w - np.arange(2 * w)[None, :]
    bias_p = _rel_bias(table, dist_p).reshape(SWA_KV_HEADS, SWA_GROUP, w, 2 * w).transpose(1, 0, 2, 3)
    bias_p = bias_p.reshape(SWA_GROUP, SWA_KV_HEADS * w, 2 * w)
    sink_p = jnp.repeat(sinks.reshape(SWA_KV_HEADS, SWA_GROUP).T, w, axis=1).reshape(SWA_GROUP, SWA_KV_HEADS * w, 1)
    o_swa_p = _swa_prompt(qp, kvp, bias_p, sink_p)

    wbuf = cache_swa_k.shape[2]
    dist_s = np.arange(ts)[:, None] + wbuf - np.arange(wbuf + ts)[None, :]
    bias_s = _rel_bias(table, dist_s).reshape(SWA_HEADS * ts, wbuf + ts)
    sink_rows = jnp.repeat(sinks, ts).reshape(SWA_HEADS * ts, 1)
    qs_gt = qs.reshape(bs, ts, SWA_GROUP, SWA_KV_DIM).transpose(0, 2, 1, 3).reshape(bs, SWA_GROUP * ts, SWA_KV_DIM)
    kbuf = cache_swa_k[0].reshape(bs, wbuf, SWA_KV_DIM).transpose(0, 2, 1)
    vbuf = cache_swa_v[0].reshape(bs, wbuf, SWA_KV_DIM).transpose(0, 2, 1)
    o_swa_s, knew_t, vnew_t = _swa_sample(qs_gt, kvs.reshape(bs, ts, 2 * SWA_KV_DIM), kbuf, vbuf,
                                          bias_s[:, :wbuf], bias_s[:, wbuf:], sink_rows)
    o_swa_s = o_swa_s.reshape(bs, SWA_GROUP, ts, SWA_KV_DIM).transpose(0, 2, 1, 3).reshape(bs * ts, SWA_Q_DIM)

    zero_shift = jnp.zeros((bp, 1, RWKV_IN), F32)
    zero_state = jnp.zeros((bp, RWKV_HEADS, RWKV_HEAD_DIM, RWKV_HEAD_DIM), F32)
    o_rw_p, state_p = _rwkv_branch(xrp.reshape(bp, tp, RWKV_IN), zero_shift, zero_state, rp, ones,
                                   bb=1, tt=256)
    o_rw_s, state_s = _rwkv_branch(xrs.reshape(bs, ts, RWKV_IN), state_rwkv_shift[0], state_rwkv[0], rp, ones,
                                   bb=8, tt=ts)

    mkv = _norm_matmul(mem_prompt.reshape(N_MEM, d), row(mem_norm[0]), w_mem_kv[0].astype(BF16))
    o_mem_p = _mem_prompt(qmp, mkv)
    o_mem_s = _mem_sample(qms.reshape(bs, ts, MEM_DIM), cache_mem_k[0].reshape(bs, N_MEM * MEM_HEADS, MEM_HEAD_DIM),
                          cache_mem_v[0].reshape(bs, N_MEM * MEM_HEADS, MEM_HEAD_DIM)).reshape(bs * ts, MEM_DIM)

    hp = _merge(hp, gm, o_swa_p, o_rw_p, o_mem_p, w_in_b, wo_swa, wo_rw, wo_mem, w_out_b)
    hs = _merge(hs, gm, o_swa_s, o_rw_s, o_mem_s, w_in_b, wo_swa, wo_rw, wo_mem, w_out_b)
    y_prompt = _ffn(hp, g2, wi2, wo2, gf, final_norm=True).reshape(bp, tp, d)
    y_sample = _ffn(hs, g2, wi2, wo2, gf, final_norm=True).reshape(bs, ts, d)

    wp = min(w, tp)
    p_k = kvp[tp - wp:, :SWA_KV_DIM].reshape(1, bp, wp, SWA_KV_HEADS, SWA_HEAD_DIM)
    p_v = kvp[tp - wp:, SWA_KV_DIM:].reshape(1, bp, wp, SWA_KV_HEADS, SWA_HEAD_DIM)
    p_mk = mkv[:, :MEM_DIM].reshape(1, bp, N_MEM, MEM_HEADS, MEM_HEAD_DIM)
    p_mv = mkv[:, MEM_DIM:].reshape(1, bp, N_MEM, MEM_HEADS, MEM_HEAD_DIM)
    s_k = knew_t.transpose(0, 2, 1).reshape(1, bs, wbuf, SWA_KV_HEADS, SWA_HEAD_DIM)
    s_v = vnew_t.transpose(0, 2, 1).reshape(1, bs, wbuf, SWA_KV_HEADS, SWA_HEAD_DIM)
    return (y_prompt, y_sample,
            p_k, p_v, state_p[None], xrp[tp - 1:].reshape(1, bp, 1, RWKV_IN), p_mk, p_mv,
            s_k, s_v, state_s[None], xrs.reshape(bs, ts, RWKV_IN)[:, ts - 1:][None])
```

```python
import functools
import math

import jax
import jax.numpy as jnp
import numpy as np
from jax import lax
from jax.experimental import pallas as pl
from jax.experimental.pallas import tpu as pltpu

F32 = jnp.float32
BF16 = jnp.bfloat16

D_MODEL = 2048
D_FF = 5632
SWA_HEADS = 16
SWA_KV_HEADS = 4
SWA_GROUP = SWA_HEADS // SWA_KV_HEADS
SWA_HEAD_DIM = 64
SWA_Q_DIM = SWA_HEADS * SWA_HEAD_DIM
SWA_KV_DIM = SWA_KV_HEADS * SWA_HEAD_DIM
WINDOW = 128
N_BUCKETS = 32
MAX_DISTANCE = 128
RWKV_HEADS = 8
RWKV_HEAD_DIM = 64
RWKV_DIM = RWKV_HEADS * RWKV_HEAD_DIM
LORA_W = 64
LORA_A = 64
LORA_G = 128
RWKV_IN = 3 * RWKV_DIM + LORA_W + LORA_A + LORA_G
N_MEM = 256
MEM_HEADS = 4
MEM_HEAD_DIM = 128
MEM_DIM = MEM_HEADS * MEM_HEAD_DIM
N_BRANCH = 3
PROJ_DIM = SWA_Q_DIM + 2 * SWA_KV_DIM + RWKV_IN + MEM_DIM
NORM_EPS = 1e-6
GN_EPS = 64e-5
NEG_INF = -1e30

LANES = 128
SUBLANES = 8
VMEM_LIMIT = 56 * 1024 * 1024


def _cparams(*sem):
    return pltpu.CompilerParams(dimension_semantics=sem, vmem_limit_bytes=VMEM_LIMIT)


def _rms(x, g):
    return x * lax.rsqrt(jnp.mean(x * x, axis=-1, keepdims=True) + NORM_EPS) * g


def _dot(a, b):
    return jnp.dot(a, b, preferred_element_type=F32)


def _dot_nt(a, b):
    return lax.dot_general(a, b, (((1,), (1,)), ((), ())), preferred_element_type=F32)


def _dot_hi(a, b):
    return jnp.dot(a, b, preferred_element_type=F32, precision=lax.Precision.HIGHEST)


def _ffn_kernel(x_ref, g_ref, wg_ref, wu_ref, wo_ref, gf_ref, o_ref, xn_ref, *, final_norm):
    j = pl.program_id(1)

    @pl.when(j == 0)
    def _():
        xn_ref[...] = _rms(x_ref[...], g_ref[...]).astype(BF16)
        o_ref[...] = jnp.zeros_like(o_ref)

    xn = xn_ref[...]
    gate = _dot(xn, wg_ref[...])
    up = _dot(xn, wu_ref[...])
    act = (gate * jax.nn.sigmoid(gate)) * up
    o_ref[...] += _dot(act.astype(BF16), wo_ref[...])

    @pl.when(j == pl.num_programs(1) - 1)
    def _():
        h = x_ref[...] + 0.5 * o_ref[...]
        if final_norm:
            h = _rms(h, gf_ref[...])
        o_ref[...] = h


def _ffn(x, g, wi, wo, gf, *, final_norm, tm=1024, tf=512):
    m, d = x.shape
    tm = min(tm, m)
    dff = wo.shape[0]
    nf = dff // tf
    return pl.pallas_call(
        functools.partial(_ffn_kernel, final_norm=final_norm),
        out_shape=jax.ShapeDtypeStruct((m, d), F32),
        grid=(m // tm, nf),
        in_specs=[
            pl.BlockSpec((tm, d), lambda i, j: (i, 0), pipeline_mode=pl.Buffered(1)),
            pl.BlockSpec((1, d), lambda i, j: (0, 0)),
            pl.BlockSpec((d, tf), lambda i, j: (0, j)),
            pl.BlockSpec((d, tf), lambda i, j: (0, j + nf)),
            pl.BlockSpec((tf, d), lambda i, j: (j, 0)),
            pl.BlockSpec((1, d), lambda i, j: (0, 0)),
        ],
        out_specs=pl.BlockSpec((tm, d), lambda i, j: (i, 0)),
        scratch_shapes=[pltpu.VMEM((tm, d), BF16)],
        compiler_params=_cparams("parallel", "arbitrary"),
        name="ffn_final" if final_norm else "ffn",
    )(x, g, wi, wi, wo, gf)


def _inproj_kernel(h_ref, g_ref, wq_ref, w_ref, q_ref, kv_ref, xr_ref, qm_ref):
    u = _rms(h_ref[...], g_ref[...]).astype(BF16)
    c0, c1, c2 = SWA_Q_DIM, SWA_Q_DIM + 2 * SWA_KV_DIM, SWA_Q_DIM + 2 * SWA_KV_DIM + RWKV_IN
    q_ref[...] = _dot(u, wq_ref[...]).astype(BF16)
    kv_ref[...] = _dot(u, w_ref[:, c0:c1])
    xr_ref[...] = _dot(u, w_ref[:, c1:c2])
    qm_ref[...] = _dot(u, w_ref[:, c2:PROJ_DIM]).astype(BF16)


def _inproj(h, g, wq, w, *, tm=256):
    m, d = h.shape
    row = lambda i: (i, 0)
    return pl.pallas_call(
        _inproj_kernel,
        out_shape=(
            jax.ShapeDtypeStruct((m, SWA_Q_DIM), BF16),
            jax.ShapeDtypeStruct((m, 2 * SWA_KV_DIM), F32),
            jax.ShapeDtypeStruct((m, RWKV_IN), F32),
            jax.ShapeDtypeStruct((m, MEM_DIM), BF16),
        ),
        grid=(m // tm,),
        in_specs=[
            pl.BlockSpec((tm, d), row),
            pl.BlockSpec((1, d), lambda i: (0, 0)),
            pl.BlockSpec((d, SWA_Q_DIM), lambda i: (0, 0), pipeline_mode=pl.Buffered(1)),
            pl.BlockSpec((d, PROJ_DIM), lambda i: (0, 0), pipeline_mode=pl.Buffered(1)),
        ],
        out_specs=(
            pl.BlockSpec((tm, SWA_Q_DIM), row),
            pl.BlockSpec((tm, 2 * SWA_KV_DIM), row),
            pl.BlockSpec((tm, RWKV_IN), row),
            pl.BlockSpec((tm, MEM_DIM), row),
        ),
        compiler_params=_cparams("parallel"),
        name="inproj",
    )(h, g, wq, w)


def _norm_matmul_kernel(x_ref, g_ref, w_ref, o_ref):
    o_ref[...] = _dot(_rms(x_ref[...], g_ref[...]).astype(BF16), w_ref[...])


def _norm_matmul(x, g, w, *, tn=512):
    m, d = x.shape
    n = w.shape[1]
    return pl.pallas_call(
        _norm_matmul_kernel,
        out_shape=jax.ShapeDtypeStruct((m, n), F32),
        grid=(n // tn,),
        in_specs=[
            pl.BlockSpec((m, d), lambda j: (0, 0)),
            pl.BlockSpec((1, d), lambda j: (0, 0)),
            pl.BlockSpec((d, tn), lambda j: (0, j)),
        ],
        out_specs=pl.BlockSpec((m, tn), lambda j: (0, j)),
        compiler_params=_cparams("parallel"),
        name="norm_matmul",
    )(x, g, w)


def _sink_softmax(logits, sink):
    m = jnp.maximum(jnp.max(logits, axis=-1, keepdims=True), sink)
    p = jnp.exp(logits - m)
    denom = jnp.sum(p, axis=-1, keepdims=True) + jnp.exp(sink - m)
    return p * (1.0 / denom)


def _swa_prompt_kernel(q_ref, kvc_ref, kvp_ref, bias_ref, sink_ref, o_ref):
    i = pl.program_id(0)
    w = WINDOW
    rows = SWA_KV_HEADS * w
    k = jnp.concatenate([kvp_ref[:, 0:SWA_KV_DIM], kvc_ref[:, 0:SWA_KV_DIM]], axis=0).astype(BF16)
    v = jnp.concatenate([kvp_ref[:, SWA_KV_DIM:], kvc_ref[:, SWA_KV_DIM:]], axis=0).astype(BF16)
    qpos = lax.broadcasted_iota(jnp.int32, (rows, 2 * w), 0) % w
    col = lax.broadcasted_iota(jnp.int32, (rows, 2 * w), 1)
    dist = qpos + w - col
    valid = (dist >= 0) & (dist < w) & ((col >= w) | (i > 0))
    lane_head = lax.broadcasted_iota(jnp.int32, (w, SWA_KV_DIM), 1) // SWA_HEAD_DIM
    scale = SWA_HEAD_DIM ** -0.5
    groups = range(SWA_GROUP)
    logits = []
    for g in groups:
        qg = q_ref[:, g * SWA_KV_DIM:(g + 1) * SWA_KV_DIM].astype(F32) * scale
        qs = jnp.concatenate([jnp.where(lane_head == kvh, qg, 0.0) for kvh in range(SWA_KV_HEADS)], axis=0)
        lg = _dot_nt(qs.astype(BF16), k)
        logits.append(jnp.where(valid, lg + bias_ref[g], NEG_INF))
    m = [jnp.maximum(jnp.max(logits[g], axis=-1, keepdims=True), sink_ref[g]) for g in groups]
    p = [jnp.exp(logits[g] - m[g]) for g in groups]
    inv = [1.0 / (jnp.sum(p[g], axis=-1, keepdims=True) + jnp.exp(sink_ref[g] - m[g])) for g in groups]
    ov = [_dot((p[g] * inv[g]).astype(BF16), v) for g in groups]
    for g in groups:
        og = jnp.zeros((w, SWA_KV_DIM), F32)
        for kvh in range(SWA_KV_HEADS):
            og = jnp.where(lane_head == kvh, ov[g][kvh * w:(kvh + 1) * w], og)
        o_ref[:, g * SWA_KV_DIM:(g + 1) * SWA_KV_DIM] = og.astype(BF16)


def _swa_prompt(q, kv, bias, sink_rows):
    t = q.shape[0]
    w = WINDOW
    rows = SWA_KV_HEADS * w
    return pl.pallas_call(
        _swa_prompt_kernel,
        out_shape=jax.ShapeDtypeStruct((t, SWA_Q_DIM), BF16),
        grid=(t // w,),
        in_specs=[
            pl.BlockSpec((w, SWA_Q_DIM), lambda i: (i, 0)),
            pl.BlockSpec((w, 2 * SWA_KV_DIM), lambda i: (i, 0)),
            pl.BlockSpec((w, 2 * SWA_KV_DIM), lambda i: (jnp.maximum(i - 1, 0), 0)),
            pl.BlockSpec((SWA_GROUP, rows, 2 * w), lambda i: (0, 0, 0)),
            pl.BlockSpec((SWA_GROUP, rows, 1), lambda i: (0, 0, 0)),
        ],
        out_specs=pl.BlockSpec((w, SWA_Q_DIM), lambda i: (i, 0)),
        compiler_params=_cparams("parallel"),
        name="swa_prompt",
    )(q, kv, kv, bias, sink_rows)


def _swa_sample_kernel(q_ref, kvn_ref, kb_ref, vb_ref, bias_b_ref, bias_n_ref, sink_ref, o_ref, ko_ref, vo_ref,
                       *, bb, t):
    gt = SWA_GROUP * t
    rows = SWA_KV_HEADS * gt
    w = kb_ref.shape[2]
    scale = SWA_HEAD_DIM ** -0.5
    lane_head = lax.broadcasted_iota(jnp.int32, (gt, SWA_KV_DIM), 1) // SWA_HEAD_DIM
    tok = lax.broadcasted_iota(jnp.int32, (rows, w), 0) % t
    keyj = lax.broadcasted_iota(jnp.int32, (rows, w), 1)
    valid_b = (tok + w - keyj) < WINDOW
    tok_n = lax.broadcasted_iota(jnp.int32, (rows, 1), 0) % t
    sink = sink_ref[...]
    bs = range(bb)
    toks = range(t)
    qall = [jnp.concatenate([jnp.where(lane_head == kvh, q_ref[b].astype(F32), 0.0) for kvh in range(SWA_KV_HEADS)],
                            axis=0) for b in bs]
    kvn = [kvn_ref[b] for b in bs]
    lb = [_dot(qall[b].astype(BF16), kb_ref[b].astype(BF16)) for b in bs]
    lb = [jnp.where(valid_b, lb[b] * scale + bias_b_ref[...], NEG_INF) for b in bs]
    ln = [[jnp.sum(qall[b] * kvn[b][j:j + 1, 0:SWA_KV_DIM], axis=-1, keepdims=True) for j in toks] for b in bs]
    ln = [[jnp.where(tok_n >= j, ln[b][j] * scale + bias_n_ref[:, j:j + 1], NEG_INF) for j in toks] for b in bs]
    m = [jnp.maximum(jnp.max(lb[b], axis=-1, keepdims=True), sink) for b in bs]
    m = [functools.reduce(jnp.maximum, ln[b], m[b]) for b in bs]
    pb = [jnp.exp(lb[b] - m[b]) for b in bs]
    pn = [[jnp.exp(ln[b][j] - m[b]) for j in toks] for b in bs]
    denom = [jnp.sum(pb[b], axis=-1, keepdims=True) + jnp.exp(sink - m[b]) for b in bs]
    inv = [1.0 / functools.reduce(jnp.add, pn[b], denom[b]) for b in bs]
    oall = [_dot_nt((pb[b] * inv[b]).astype(BF16), vb_ref[b].astype(BF16)) for b in bs]
    for b in bs:
        ob = oall[b]
        for j in toks:
            ob = ob + (pn[b][j] * inv[b]) * kvn[b][j:j + 1, SWA_KV_DIM:]
        og = jnp.zeros((gt, SWA_KV_DIM), F32)
        for kvh in range(SWA_KV_HEADS):
            og = jnp.where(lane_head == kvh, ob[kvh * gt:(kvh + 1) * gt], og)
        o_ref[b] = og.astype(BF16)
    pos = lax.broadcasted_iota(jnp.int32, (SWA_KV_DIM, w), 1)
    pad = jnp.zeros((w - SUBLANES, 2 * SWA_KV_DIM), F32)
    row8 = lax.broadcasted_iota(jnp.int32, (SUBLANES, 2 * SWA_KV_DIM), 0)
    for b in bs:
        last8 = jnp.zeros((SUBLANES, 2 * SWA_KV_DIM), F32)
        for j in toks:
            last8 = jnp.where(row8 == SUBLANES - t + j, kvn[b][j:j + 1], last8)
        tail_t = jnp.concatenate([pad, last8], axis=0).T
        ko_ref[b] = jnp.where(pos >= w - t, tail_t[:SWA_KV_DIM], pltpu.roll(kb_ref[b], w - t, axis=1))
        vo_ref[b] = jnp.where(pos >= w - t, tail_t[SWA_KV_DIM:], pltpu.roll(vb_ref[b], w - t, axis=1))


def _swa_sample(q, kvn, kbuf, vbuf, bias_b, bias_n, sink_rows, *, bb=8):
    b, gt, _ = q.shape
    t = kvn.shape[1]
    w = kbuf.shape[2]
    rows = SWA_KV_HEADS * gt
    blk = lambda i: (i, 0, 0)
    const = lambda i: (0, 0)
    cache = pl.BlockSpec((bb, SWA_KV_DIM, w), blk)
    return pl.pallas_call(
        functools.partial(_swa_sample_kernel, bb=bb, t=t),
        out_shape=(jax.ShapeDtypeStruct((b, gt, SWA_KV_DIM), BF16),
                   jax.ShapeDtypeStruct(kbuf.shape, F32), jax.ShapeDtypeStruct(vbuf.shape, F32)),
        grid=(b // bb,),
        in_specs=[
            pl.BlockSpec((bb, gt, SWA_KV_DIM), blk),
            pl.BlockSpec((bb, t, 2 * SWA_KV_DIM), blk),
            cache, cache,
            pl.BlockSpec((rows, w), const),
            pl.BlockSpec((rows, t), const),
            pl.BlockSpec((rows, 1), const),
        ],
        out_specs=(pl.BlockSpec((bb, gt, SWA_KV_DIM), blk), cache, cache),
        compiler_params=_cparams("parallel"),
        name="swa_sample",
    )(q, kvn, kbuf, vbuf, bias_b, bias_n, sink_rows)


def _softmax_rows(x):
    m = jnp.max(x, axis=-1, keepdims=True)
    p = jnp.exp(x - m)
    return p * (1.0 / jnp.sum(p, axis=-1, keepdims=True))


def _mem_heads(q, mk, mv):
    scale = MEM_HEAD_DIM ** -0.5
    outs = []
    for h in range(MEM_HEADS):
        sl = slice(h * MEM_HEAD_DIM, (h + 1) * MEM_HEAD_DIM)
        p = _softmax_rows(_dot_nt(q[:, sl], mk[:, sl]) * scale)
        outs.append(_dot(p.astype(BF16), mv[:, sl]))
    return jnp.concatenate(outs, axis=-1)


def _mem_prompt_kernel(q_ref, mk_ref, mv_ref, o_ref):
    o_ref[...] = _mem_heads(q_ref[...], mk_ref[...].astype(BF16), mv_ref[...].astype(BF16)).astype(BF16)


def _mem_prompt(q, mkv, *, tm=512):
    m = q.shape[0]
    return pl.pallas_call(
        _mem_prompt_kernel,
        out_shape=jax.ShapeDtypeStruct((m, MEM_DIM), BF16),
        grid=(m // tm,),
        in_specs=[
            pl.BlockSpec((tm, MEM_DIM), lambda i: (i, 0)),
            pl.BlockSpec((N_MEM, MEM_DIM), lambda i: (0, 0)),
            pl.BlockSpec((N_MEM, MEM_DIM), lambda i: (0, 1)),
        ],
        out_specs=pl.BlockSpec((tm, MEM_DIM), lambda i: (i, 0)),
        compiler_params=_cparams("parallel"),
        name="mem_prompt",
    )(q, mkv, mkv)


def _mem_sample_kernel(q_ref, mk_ref, mv_ref, o_ref, *, bb):
    scale = MEM_HEAD_DIM ** -0.5
    pairs = [(b, h) for b in range(bb) for h in range(MEM_HEADS)]
    rows = lambda h: pl.ds(h, N_MEM, stride=MEM_HEADS)
    cols = lambda h: slice(h * MEM_HEAD_DIM, (h + 1) * MEM_HEAD_DIM)
    q = [q_ref[b] for b in range(bb)]
    x = [_dot_nt(q[b][:, cols(h)], mk_ref[b, rows(h), :].astype(BF16)) * scale for b, h in pairs]
    m = [jnp.max(xi, axis=-1, keepdims=True) for xi in x]
    p = [jnp.exp(xi - mi) for xi, mi in zip(x, m)]
    inv = [1.0 / jnp.sum(pi, axis=-1, keepdims=True) for pi in p]
    o = [_dot((pi * ii).astype(BF16), mv_ref[b, rows(h), :].astype(BF16)) for (b, h), pi, ii in zip(pairs, p, inv)]
    for b in range(bb):
        o_ref[b] = jnp.concatenate(o[b * MEM_HEADS:(b + 1) * MEM_HEADS], axis=-1).astype(BF16)


def _mem_sample(q, mk, mv, *, bb=8):
    b, t, _ = q.shape
    blk = lambda i: (i, 0, 0)
    return pl.pallas_call(
        functools.partial(_mem_sample_kernel, bb=bb),
        out_shape=jax.ShapeDtypeStruct((b, t, MEM_DIM), BF16),
        grid=(b // bb,),
        in_specs=[
            pl.BlockSpec((bb, t, MEM_DIM), blk),
            pl.BlockSpec((bb, N_MEM * MEM_HEADS, MEM_HEAD_DIM), blk),
            pl.BlockSpec((bb, N_MEM * MEM_HEADS, MEM_HEAD_DIM), blk),
        ],
        out_specs=pl.BlockSpec((bb, t, MEM_DIM), blk),
        compiler_params=_cparams("parallel"),
        name="mem_sample",
    )(q, mk, mv)


def _head_sum(x, ones_ref):
    hi = x.astype(BF16)
    lo = (x - hi.astype(F32)).astype(BF16)
    return _dot(hi, ones_ref[...]) + _dot(lo, ones_ref[...])


def _rwkv_prep_kernel(x_ref, pre_ref, start_ref, mu_ref, w0_ref, ww2_ref, a0_ref, aw2_ref, gw2_ref, kk_ref, ka_ref,
                      rk_ref, ones_ref, r_o, w_o, k_o, v_o, kk_o, kka_o, g_o, bonus_o, *, seq, tm):
    x = x_ref[...]
    row = lax.broadcasted_iota(jnp.int32, x.shape, 0)
    shifted = pltpu.roll(x, 1, axis=0)
    if seq >= tm:
        is_start = (pl.program_id(0) * tm) % seq == 0
        first_prev = jnp.where(is_start, start_ref[0], pre_ref[SUBLANES - 1:SUBLANES, :])
        prev = jnp.where(row == 0, first_prev, shifted)
    else:
        prev = jnp.where(row % seq == 0, start_ref[...], shifted)
    xs = x + mu_ref[...] * (prev - x)
    d = RWKV_DIM
    r = xs[:, 0:d]
    k = xs[:, d:2 * d]
    v = xs[:, 2 * d:3 * d]
    lw = xs[:, 3 * d:3 * d + LORA_W]
    la = xs[:, 3 * d + LORA_W:3 * d + LORA_W + LORA_A]
    lg = xs[:, 3 * d + LORA_W + LORA_A:]
    wpre = w0_ref[...] + _dot_hi(jnp.tanh(lw), ww2_ref[...])
    w = -jax.nn.softplus(-wpre) - 0.5
    log_decay = -jnp.exp(w)
    a = jax.nn.sigmoid(a0_ref[...] + _dot_hi(la, aw2_ref[...]))
    g = _dot_hi(jax.nn.sigmoid(lg), gw2_ref[...])
    kk = k * kk_ref[...]
    kk = kk / jnp.maximum(jnp.sqrt(_head_sum(kk * kk, ones_ref)), 1e-12)
    kh = k * (1.0 + (a - 1.0) * ka_ref[...])
    r_o[...] = r
    w_o[...] = log_decay
    k_o[...] = kh
    v_o[...] = v
    kk_o[...] = kk
    kka_o[...] = kk * a
    g_o[...] = g
    bonus_o[...] = _head_sum(r * kh * rk_ref[...], ones_ref) * v


def _rwkv_prep(x, shift0, p, ones, *, seq, tm=256):
    m = x.shape[0]
    row = lambda i: (i, 0)
    const = lambda i: (0, 0)
    vec = lambda n: pl.BlockSpec((1, n), const)
    out = jax.ShapeDtypeStruct((m, RWKV_DIM), F32)
    if seq >= tm:
        assert seq % tm == 0
        start = shift0
        start_spec = pl.BlockSpec((1, 1, RWKV_IN), lambda i: ((i * tm) // seq, 0, 0))
    else:
        assert tm % seq == 0
        start = jnp.repeat(shift0[:, 0], seq, axis=0)
        start_spec = pl.BlockSpec((tm, RWKV_IN), row)
    pre_blocks = tm // SUBLANES
    return pl.pallas_call(
        functools.partial(_rwkv_prep_kernel, seq=seq, tm=tm),
        out_shape=(out,) * 8,
        grid=(m // tm,),
        in_specs=[
            pl.BlockSpec((tm, RWKV_IN), row),
            pl.BlockSpec((SUBLANES, RWKV_IN), lambda i: (jnp.maximum(i * pre_blocks - 1, 0), 0)),
            start_spec,
            vec(RWKV_IN), vec(RWKV_DIM),
            pl.BlockSpec((LORA_W, RWKV_DIM), const),
            vec(RWKV_DIM),
            pl.BlockSpec((LORA_A, RWKV_DIM), const),
            pl.BlockSpec((LORA_G, RWKV_DIM), const),
            vec(RWKV_DIM), vec(RWKV_DIM), vec(RWKV_DIM),
            pl.BlockSpec((RWKV_DIM, RWKV_DIM), const),
        ],
        out_specs=(pl.BlockSpec((tm, RWKV_DIM), row),) * 8,
        compiler_params=_cparams("parallel"),
        name="rwkv_prep",
    )(x, x, start, p["mu"], p["w0"], p["w_w2"], p["a0"], p["a_w2"], p["g_w2"], p["k_k"], p["k_a"], p["r_k"], ones)


def _half_sums(x, low):
    lo = jnp.sum(jnp.where(low, x, 0.0), axis=-1, keepdims=True)
    hi = jnp.sum(jnp.where(low, 0.0, x), axis=-1, keepdims=True)
    return jnp.where(low, lo, hi)


def _rwkv_scan_kernel(r_ref, w_ref, k_ref, v_ref, kk_ref, kka_ref, s0_ref, y_ref, so_ref, s_scr, *, bb, tt):
    jt = pl.program_id(1)
    group = min(tt, SUBLANES)
    n = RWKV_HEAD_DIM
    pairs = RWKV_HEADS // 2
    low = lax.broadcasted_iota(jnp.int32, (n, LANES), 1) < n
    diag = (lax.broadcasted_iota(jnp.int32, (n, LANES), 1) % n) == lax.broadcasted_iota(jnp.int32, (n, LANES), 0)

    @pl.when(jt == 0)
    def _():
        for b in range(bb):
            for h in range(RWKV_HEADS):
                s_scr[b, h // 2, :, (h % 2) * n:(h % 2 + 1) * n] = s0_ref[b, h]

    nb = 2 if bb % 2 == 0 else 1
    chains = [(bi, p) for bi in range(nb) for p in range(pairs)]

    def batch_body(b0, carry):
        def group_body(t0, states):
            toks = pl.ds(pl.multiple_of(t0 * group, group), group)
            at = lambda ref: [ref[b0 * nb + bi, toks, pl.ds(p * LANES, LANES)] for bi, p in chains]
            r, k, v, kk, kka = at(r_ref), at(k_ref), at(v_ref), at(kk_ref), at(kka_ref)
            w = [jnp.exp(x) for x in at(w_ref)]
            s = list(states)
            ys = [[] for _ in chains]
            for j in range(group):
                one = slice(j, j + 1)
                sa = [_half_sums(s[c] * (-kk[c][one]), low) for c in range(len(chains))]
                vcol = [_half_sums(jnp.where(diag, v[c][one], 0.0), low) for c in range(len(chains))]
                s = [s[c] * w[c][one] + sa[c] * kka[c][one] + vcol[c] * k[c][one] for c in range(len(chains))]
                ycol = [_half_sums(s[c] * r[c][one], low) for c in range(len(chains))]
                for c in range(len(chains)):
                    ys[c].append(jnp.sum(jnp.where(diag, ycol[c], 0.0), axis=0, keepdims=True))
            for c, (bi, p) in enumerate(chains):
                y_ref[b0 * nb + bi, toks, pl.ds(p * LANES, LANES)] = jnp.concatenate(ys[c], axis=0)
            return tuple(s)

        states = tuple(s_scr[b0 * nb + bi, p] for bi, p in chains)
        states = lax.fori_loop(0, tt // group, group_body, states)
        for c, (bi, p) in enumerate(chains):
            s_scr[b0 * nb + bi, p] = states[c]
        return carry

    lax.fori_loop(0, bb // nb, batch_body, 0)

    @pl.when(jt == pl.num_programs(1) - 1)
    def _():
        for b in range(bb):
            for h in range(RWKV_HEADS):
                so_ref[b, h] = s_scr[b, h // 2, :, (h % 2) * n:(h % 2 + 1) * n]


def _rwkv_scan(r, w, k, v, kk, kka, s0, *, bb, tt):
    b, t, d = r.shape
    n = RWKV_HEAD_DIM
    tok = pl.BlockSpec((bb, tt, d), lambda i, j: (i, j, 0))
    st = pl.BlockSpec((bb, RWKV_HEADS, n, n), lambda i, j: (i, 0, 0, 0))
    return pl.pallas_call(
        functools.partial(_rwkv_scan_kernel, bb=bb, tt=tt),
        out_shape=(jax.ShapeDtypeStruct((b, t, d), F32), jax.ShapeDtypeStruct((b, RWKV_HEADS, n, n), F32)),
        grid=(b // bb, t // tt),
        in_specs=[tok] * 6 + [st],
        out_specs=(tok, st),
        scratch_shapes=[pltpu.VMEM((bb, RWKV_HEADS // 2, n, LANES), F32)],
        compiler_params=_cparams("parallel", "arbitrary"),
        name="rwkv_scan",
    )(r, w, k, v, kk, kka, s0)


def _rwkv_lanes_kernel(r_ref, lw_ref, k_ref, v_ref, kk_ref, kka_ref, s_ref, y_ref, so_ref, v_scr, y_scr, *, t):
    n = RWKV_HEAD_DIM
    nb = s_ref.shape[-1]
    heads = range(2)

    def token_major(ref, j):
        return ref[pl.ds(j, nb, stride=t), :].T

    for j in range(t):
        v_scr[...] = token_major(v_ref, j)
        r_t, k_t, kk_t, kka_t = (token_major(ref, j) for ref in (r_ref, k_ref, kk_ref, kka_ref))
        w_t = jnp.exp(token_major(lw_ref, j))
        src = s_ref if j == 0 else so_ref

        def value_group(g, carry):
            rows = pl.multiple_of(g * SUBLANES, SUBLANES)
            ys = [[] for _ in heads]
            vg = [v_scr[pl.ds(h * n + rows, SUBLANES), :] for h in heads]
            for i in range(SUBLANES):
                for h in heads:
                    f = slice(h * n, (h + 1) * n)
                    s = src[h, rows + i]
                    sa = jnp.sum(s * (-kk_t[f]), axis=0, keepdims=True)
                    s = s * w_t[f] + sa * kka_t[f] + vg[h][i:i + 1] * k_t[f]
                    so_ref[h, rows + i] = s
                    ys[h].append(jnp.sum(s * r_t[f], axis=0, keepdims=True))
            for h in heads:
                y_scr[pl.ds(h * n + rows, SUBLANES), :] = jnp.concatenate(ys[h], axis=0)
            return carry

        lax.fori_loop(0, n // SUBLANES, value_group, 0)
        y_ref[j] = y_scr[...].T


def _rwkv_lanes(r, lw, k, v, kk, kka, s0, *, t):
    m, d = r.shape
    nb = m // t
    n = RWKV_HEAD_DIM
    assert nb == LANES, "one batch per lane"
    tok = pl.BlockSpec((m, LANES), lambda p: (0, p))
    st = pl.BlockSpec((2, n, n, nb), lambda p: (p, 0, 0, 0))
    return pl.pallas_call(
        functools.partial(_rwkv_lanes_kernel, t=t),
        out_shape=(jax.ShapeDtypeStruct((t, nb, d), F32), jax.ShapeDtypeStruct((RWKV_HEADS, n, n, nb), F32)),
        grid=(RWKV_HEADS // 2,),
        in_specs=[tok] * 6 + [st],
        out_specs=(pl.BlockSpec((t, nb, LANES), lambda p: (0, 0, p)), st),
        scratch_shapes=[pltpu.VMEM((LANES, nb), F32), pltpu.VMEM((LANES, nb), F32)],
        compiler_params=_cparams("parallel"),
        name="rwkv_lanes",
    )(r, lw, k, v, kk, kka, s0)


CHUNK = 64
GROUP_HEADS = 4
GROUP_W = GROUP_HEADS * RWKV_HEAD_DIM
N_GROUPS = RWKV_HEADS // GROUP_HEADS
(MASK_SAME, MASK_STRICT, MASK_INCL, MASK_LEVEL0) = (0, 1, 2, 3)
N_LEVELS = int(math.log2(CHUNK))


def _chunk_masks():
    i = np.arange(GROUP_W)
    same = (i[:, None] // CHUNK) == (i[None, :] // CHUNK)
    masks = [same, same & (i[None, :] < i[:, None]), same & (i[None, :] <= i[:, None])]
    for lvl in range(N_LEVELS):
        m = 1 << lvl
        masks.append(((i[:, None] // (2 * m)) == (i[None, :] // (2 * m))) & ((i[:, None] // m) != (i[None, :] // m))
                     & (i[None, :] < i[:, None]))
    return np.stack(masks).astype(np.float32)


def _rwkv_chunk_kernel(r_ref, lw_ref, k_ref, v_ref, kk_ref, kka_ref, st0_ref, tri_ref, eye_ref, mask_ref,
                       y_ref, sto_ref, st_scr):
    @pl.when(pl.program_id(0) == 0)
    def _():
        st_scr[...] = st0_ref[...]

    eye = eye_ref[...]
    tri = tri_ref[...]
    tile_rows = lambda x: jnp.concatenate([x] * GROUP_HEADS, axis=0)
    block_diag = lambda x: (tile_rows(x) * mask_ref[MASK_SAME]).astype(BF16)

    n_chunks = y_ref.shape[0] // CHUNK
    chains = [(slice(c * CHUNK, (c + 1) * CHUNK), slice(g * GROUP_W, (g + 1) * GROUP_W))
              for c in range(n_chunks) for g in range(N_GROUPS)]
    each = lambda f, *cols: [f(*args) for args in zip(*cols)]
    same, strict, incl = mask_ref[MASK_SAME], mask_ref[MASK_STRICT], mask_ref[MASK_INCL]

    def cum_decay(lw):
        h1 = lw.astype(BF16)
        r1 = lw - h1.astype(F32)
        h2 = r1.astype(BF16)
        h3 = (r1 - h2.astype(F32)).astype(BF16)
        return _dot(tri, h1) + _dot(tri, h2) + _dot(tri, h3)

    lw = [lw_ref[rows, sl] for rows, sl in chains]
    kka = [kka_ref[rows, sl] for rows, sl in chains]
    k = [k_ref[rows, sl] for rows, sl in chains]
    cum = each(cum_decay, lw)
    cum_last = each(lambda c: c[CHUNK - 1:CHUNK, :], cum)
    p_inv = each(lambda c: jnp.exp(-c), cum)
    p_tail = each(lambda c, cl: jnp.exp(cl - c), cum, cum_last)
    a_bd = [block_diag(-kk_ref[rows, sl] * jnp.exp(c - l)) for (rows, sl), c, l in zip(chains, cum, lw)]
    r_f = [tile_rows(r_ref[rows, sl] * jnp.exp(c)) * same for (rows, sl), c in zip(chains, cum)]
    r_bd = each(lambda x: x.astype(BF16), r_f)
    v_bd = [block_diag(v_ref[rows, sl]) for rows, sl in chains]
    b_rep = each(lambda x, p: tile_rows((x * p).astype(BF16)), kka, p_inv)
    k_rep = each(lambda x, p: tile_rows((x * p).astype(BF16)), k, p_inv)
    bh_rep = each(lambda x, p: tile_rows(x * p), kka, p_tail)
    kh_rep = each(lambda x, p: tile_rows(x * p), k, p_tail)

    l_ab_f = each(lambda a, b: _dot_nt(a, b) * strict, a_bd, b_rep)
    l_ab = each(lambda x: x.astype(BF16), l_ab_f)
    l_ak = each(lambda a, b: (_dot_nt(a, b) * strict).astype(BF16), a_bd, k_rep)
    m_rb = each(lambda a, b: (_dot_nt(a, b) * incl).astype(BF16), r_bd, b_rep)
    m_rk = each(lambda a, b: (_dot_nt(a, b) * incl).astype(BF16), r_bd, k_rep)
    bh_t = each(lambda x: (x.T * same).astype(BF16), bh_rep)
    kh_t = each(lambda x: (x.T * same).astype(BF16), kh_rep)

    d = each(lambda l: eye + l * mask_ref[MASK_LEVEL0], l_ab_f)
    for lvl in range(1, N_LEVELS):
        d_b = each(lambda x: x.astype(BF16), d)
        x = each(lambda l, db: (_dot(l, db) * mask_ref[MASK_LEVEL0 + lvl]).astype(BF16), l_ab, d_b)
        d = each(lambda dd, db, xx: dd + _dot(db, xx), d, d_b, x)
    t_b = each(lambda x: x.astype(BF16), d)

    wm = each(lambda a, b, vv: _dot(jnp.concatenate([a, b], axis=0), vv), l_ak, m_rk, v_bd)
    twa = each(lambda t, w, a: _dot(t, jnp.concatenate([w[:GROUP_W].astype(BF16), a], axis=1)).astype(BF16),
               t_b, wm, a_bd)
    ry = each(_dot, m_rb, twa)
    mn = each(_dot, bh_t, twa)
    khv = each(_dot, kh_t, v_bd)
    y0 = each(lambda a, w: a[:, :GROUP_W] + w[GROUP_W:], ry, wm)
    n_x = each(lambda a, b: a[:, :GROUP_W] + b, mn, khv)
    mr = each(lambda a, cl, rf, b: jnp.concatenate(
        [(eye * jnp.exp(cl) + a[:, GROUP_W:]).astype(BF16), (rf + b[:, GROUP_W:]).astype(BF16)], axis=0),
        mn, cum_last, r_f, ry)

    st = [st_scr[g] for g in range(N_GROUPS)]
    for i, (rows, sl) in enumerate(chains):
        g = i % N_GROUPS
        ys = _dot(mr[i], st[g].astype(BF16))
        st[g] = ys[:GROUP_W] + n_x[i]
        y_bd = ys[GROUP_W:] + y0[i]
        y = y_bd[0:CHUNK]
        for h in range(1, GROUP_HEADS):
            y = y + y_bd[h * CHUNK:(h + 1) * CHUNK]
        y_ref[rows, sl] = y
    for g in range(N_GROUPS):
        st_scr[g] = st[g]

    @pl.when(pl.program_id(0) == pl.num_programs(0) - 1)
    def _():
        sto_ref[...] = st_scr[...]


def _rwkv_chunked(r, lw, k, v, kk, kka, s0, *, chunks_per_step=4):
    t, d = r.shape
    n = RWKV_HEAD_DIM
    tt = CHUNK * chunks_per_step
    assert CHUNK == n and t % tt == 0
    st0 = jnp.einsum("ghvk,hj->ghkjv", s0.reshape(N_GROUPS, GROUP_HEADS, n, n), jnp.eye(GROUP_HEADS, dtype=F32))
    st0 = st0.reshape(N_GROUPS, GROUP_W, GROUP_W)
    tri = jnp.asarray(np.tril(np.ones((CHUNK, CHUNK), np.float32)), BF16)
    eye = jnp.eye(GROUP_W, dtype=F32)
    masks = jnp.asarray(_chunk_masks())
    tok = pl.BlockSpec((tt, d), lambda c: (c, 0))
    st_spec = pl.BlockSpec((N_GROUPS, GROUP_W, GROUP_W), lambda c: (0, 0, 0))
    y, st = pl.pallas_call(
        _rwkv_chunk_kernel,
        out_shape=(jax.ShapeDtypeStruct((t, d), F32), jax.ShapeDtypeStruct((N_GROUPS, GROUP_W, GROUP_W), F32)),
        grid=(t // tt,),
        in_specs=[tok] * 6 + [st_spec, pl.BlockSpec((CHUNK, CHUNK), lambda c: (0, 0)),
                              pl.BlockSpec((GROUP_W, GROUP_W), lambda c: (0, 0)),
                              pl.BlockSpec(masks.shape, lambda c: (0, 0, 0))],
        out_specs=(tok, st_spec),
        scratch_shapes=[pltpu.VMEM((N_GROUPS, GROUP_W, GROUP_W), F32)],
        compiler_params=_cparams("arbitrary"),
        name="rwkv_chunk",
    )(r, lw, k, v, kk, kka, st0, tri, eye, masks)
    st5 = st.reshape(N_GROUPS, GROUP_HEADS, n, GROUP_HEADS, n)
    s_new = jnp.einsum("ghkjv,hj->ghvk", st5, jnp.eye(GROUP_HEADS, dtype=F32)).reshape(RWKV_HEADS, n, n)
    return y, s_new


def _rwkv_post_kernel(y_ref, bonus_ref, g_ref, lnw_ref, lnb_ref, ones_ref, o_ref):
    y = y_ref[...]
    inv_n = 1.0 / RWKV_HEAD_DIM
    mu = _head_sum(y, ones_ref) * inv_n
    dlt = y - mu
    var = _head_sum(dlt * dlt, ones_ref) * inv_n
    yn = dlt * lax.rsqrt(var + GN_EPS) * lnw_ref[...] + lnb_ref[...]
    o_ref[...] = ((yn + bonus_ref[...]) * g_ref[...]).astype(BF16)


def _rwkv_post(y, bonus, g, lnw, lnb, ones, *, tm=256):
    m = y.shape[0]
    row = lambda i: (i, 0)
    const = lambda i: (0, 0)
    tile = pl.BlockSpec((tm, RWKV_DIM), row)
    return pl.pallas_call(
        _rwkv_post_kernel,
        out_shape=jax.ShapeDtypeStruct((m, RWKV_DIM), BF16),
        grid=(m // tm,),
        in_specs=[tile, tile, tile, pl.BlockSpec((1, RWKV_DIM), const), pl.BlockSpec((1, RWKV_DIM), const),
                  pl.BlockSpec((RWKV_DIM, RWKV_DIM), const)],
        out_specs=tile,
        compiler_params=_cparams("parallel"),
        name="rwkv_post",
    )(y, bonus, g, lnw, lnb, ones)


GATE_BLOCK = math.gcd(PROJ_DIM, D_MODEL)


def _merge_kernel(h_ref, g_ref, os_ref, or_ref, om_ref, wos_ref, wor_ref, wom_ref, wout_ref, *rest, parts):
    gate_refs, (o_ref, u_ref, acc_ref) = rest[:N_BRANCH * parts], rest[N_BRANCH * parts:]
    j = pl.program_id(1)

    @pl.when(j == 0)
    def _():
        u_ref[...] = _rms(h_ref[...], g_ref[...]).astype(BF16)
        acc_ref[...] = jnp.zeros_like(acc_ref)

    u = u_ref[...]
    merged = None
    for b, (x_ref, w_ref) in enumerate(((os_ref, wos_ref), (or_ref, wor_ref), (om_ref, wom_ref))):
        gate = jnp.concatenate([_dot(u, gate_refs[b * parts + c][...]) for c in range(parts)], axis=1)
        term = jax.nn.sigmoid(gate) * _dot(x_ref[...], w_ref[...])
        merged = term if merged is None else merged + term
    acc_ref[...] += _dot(merged.astype(BF16), wout_ref[...])

    @pl.when(j == pl.num_programs(1) - 1)
    def _():
        o_ref[...] = h_ref[...] + acc_ref[...]


def _merge(h, g, o_swa, o_rw, o_mem, w_in, wo_swa, wo_rw, wo_mem, w_out, *, tm=512, tn=512):
    m, d = h.shape
    nt = d // tn
    parts = tn // GATE_BLOCK
    g0 = PROJ_DIM // GATE_BLOCK
    row = lambda i, j: (i, 0)
    col = lambda i, j: (0, j)
    gate_specs = [pl.BlockSpec((d, GATE_BLOCK), functools.partial(
        lambda i, j, off: (0, off + j * parts), off=g0 + b * (d // GATE_BLOCK) + c))
        for b in range(N_BRANCH) for c in range(parts)]
    return pl.pallas_call(
        functools.partial(_merge_kernel, parts=parts),
        out_shape=jax.ShapeDtypeStruct((m, d), F32),
        grid=(m // tm, nt),
        in_specs=[
            pl.BlockSpec((tm, d), row),
            pl.BlockSpec((1, d), lambda i, j: (0, 0)),
            pl.BlockSpec((tm, SWA_Q_DIM), row),
            pl.BlockSpec((tm, RWKV_DIM), row),
            pl.BlockSpec((tm, MEM_DIM), row),
            pl.BlockSpec((SWA_Q_DIM, tn), col),
            pl.BlockSpec((RWKV_DIM, tn), col),
            pl.BlockSpec((MEM_DIM, tn), col),
            pl.BlockSpec((tn, d), lambda i, j: (j, 0)),
        ] + gate_specs,
        out_specs=pl.BlockSpec((tm, d), row),
        scratch_shapes=[pltpu.VMEM((tm, d), BF16), pltpu.VMEM((tm, d), F32)],
        compiler_params=_cparams("parallel", "arbitrary"),
        name="merge",
    )(h, g, o_swa, o_rw, o_mem, wo_swa, wo_rw, wo_mem, w_out, *([w_in] * (N_BRANCH * parts)))


def _t5_bucket(dist):
    max_exact = N_BUCKETS // 2
    d = np.maximum(dist, 0)
    log_ratio = (np.log(np.maximum(d, 1).astype(np.float32) / np.float32(max_exact))
                 / np.float32(math.log(MAX_DISTANCE / max_exact)))
    large = np.minimum(max_exact + (log_ratio * (N_BUCKETS - max_exact)).astype(np.int32), N_BUCKETS - 1)
    return np.where(d < max_exact, d, large).astype(np.int32)


def _rel_bias(table, dist):
    onehot = np.eye(N_BUCKETS, dtype=np.float32)[_t5_bucket(dist).reshape(-1)]
    bias = jnp.einsum("nb,bh->hn", jnp.asarray(onehot), table, precision=lax.Precision.HIGHEST)
    return bias.reshape(SWA_HEADS, *dist.shape)


def _rwkv_branch(xr, shift0, s0, p, ones, *, bb, tt):
    b, t, _ = xr.shape
    flat = lambda z: z.reshape(b * t, z.shape[-1])
    r, w, k, v, kk, kka, g, bonus = _rwkv_prep(flat(xr), shift0, p, ones, seq=t)
    if b == 1:
        y, s_new = _rwkv_chunked(r, w, k, v, kk, kka, s0[0])
        s_new = s_new[None]
    elif b == LANES:
        y, s_new = _rwkv_lanes(r, w, k, v, kk, kka, jnp.transpose(s0, (1, 2, 3, 0)), t=t)
        y = jnp.transpose(y, (1, 0, 2))
        s_new = jnp.transpose(s_new, (3, 0, 1, 2))
    else:
        seq = lambda z: z.reshape(b, t, RWKV_DIM)
        y, s_new = _rwkv_scan(seq(r), seq(w), seq(k), seq(v), seq(kk), seq(kka), s0, bb=bb, tt=tt)
    o = _rwkv_post(flat(y), bonus, g, p["ln_w"], p["ln_b"], ones)
    return o, s_new


def kernel(x_prompt, mem_prompt, x_sample, cache_swa_k, cache_swa_v, state_rwkv, state_rwkv_shift, cache_mem_k, cache_mem_v, ffn1_norm, ffn1_wi, ffn1_wo, mix_norm, w_in, swa_sinks, rel_bias_table, rwkv_mu, rwkv_w0, rwkv_w_w2, rwkv_a0, rwkv_a_w2, rwkv_g_w2, rwkv_k_k, rwkv_k_a, rwkv_r_k, rwkv_ln_w, rwkv_ln_b, mem_norm, w_mem_kv, w_o_swa, w_o_rwkv, w_o_mem, w_out, ffn2_norm, ffn2_wi, ffn2_wo, final_norm):
    assert ffn1_wi.shape[0] == 1, "single-layer trunk"
    bp, tp, d = x_prompt.shape
    bs, ts, _ = x_sample.shape
    assert bp == 1
    row = lambda z: z.reshape(1, -1).astype(F32)

    w_in_b = w_in[0].astype(BF16)
    w_q = w_in_b[:, :SWA_Q_DIM].reshape(d, SWA_KV_HEADS, SWA_GROUP, SWA_HEAD_DIM).transpose(0, 2, 1, 3).reshape(d, SWA_Q_DIM)
    wo_swa = w_o_swa[0].astype(BF16).reshape(SWA_KV_HEADS, SWA_GROUP, SWA_HEAD_DIM, d).transpose(1, 0, 2, 3).reshape(SWA_Q_DIM, d)
    wo_rw = w_o_rwkv[0].astype(BF16)
    wo_mem = w_o_mem[0].astype(BF16)
    w_out_b = w_out[0].astype(BF16)
    wi1, wo1 = ffn1_wi[0].astype(BF16), ffn1_wo[0].astype(BF16)
    wi2, wo2 = ffn2_wi[0].astype(BF16), ffn2_wo[0].astype(BF16)
    g1, gm, g2, gf = row(ffn1_norm[0]), row(mix_norm[0]), row(ffn2_norm[0]), row(final_norm)
    rp = {
        "mu": row(rwkv_mu[0]), "w0": row(rwkv_w0[0]), "w_w2": rwkv_w_w2[0], "a0": row(rwkv_a0[0]),
        "a_w2": rwkv_a_w2[0], "g_w2": rwkv_g_w2[0], "k_k": row(rwkv_k_k[0]), "k_a": row(rwkv_k_a[0]),
        "r_k": row(rwkv_r_k[0]), "ln_w": row(rwkv_ln_w[0]), "ln_b": row(rwkv_ln_b[0]),
    }
    seg = np.arange(RWKV_DIM) // RWKV_HEAD_DIM
    ones = jnp.asarray(seg[:, None] == seg[None, :], dtype=BF16)
    sinks = swa_sinks[0].astype(F32)
    table = rel_bias_table.astype(F32)

    xp = x_prompt.reshape(tp, d)
    xs = x_sample.reshape(bs * ts, d)
    hp = _ffn(xp, g1, wi1, wo1, gf, final_norm=False)
    hs = _ffn(xs, g1, wi1, wo1, gf, final_norm=False)
    qp, kvp, xrp, qmp = _inproj(hp, gm, w_q, w_in_b)
    qs, kvs, xrs, qms = _inproj(hs, gm, w_q, w_in_b)

    w = WINDOW
    dist_p = np.arange(w)[:, None] + w - np.arange(2 * w)[None, :]
    bias_p = _rel_bias(table, dist_p).reshape(SWA_KV_HEADS, SWA_GROUP, w, 2 * w).transpose(1, 0, 2, 3)
    bias_p = bias_p.reshape(SWA_GROUP, SWA_KV_HEADS * w, 2 * w)
    sink_p = jnp.repeat(sinks.reshape(SWA_KV_HEADS, SWA_GROUP).T, w, axis=1).reshape(SWA_GROUP, SWA_KV_HEADS * w, 1)
    o_swa_p = _swa_prompt(qp, kvp, bias_p, sink_p)

    wbuf = cache_swa_k.shape[2]
    dist_s = np.arange(ts)[:, None] + wbuf - np.arange(wbuf + ts)[None, :]
    bias_s = _rel_bias(table, dist_s).reshape(SWA_HEADS * ts, wbuf + ts)
    sink_rows = jnp.repeat(sinks, ts).reshape(SWA_HEADS * ts, 1)
    qs_gt = qs.reshape(bs, ts, SWA_GROUP, SWA_KV_DIM).transpose(0, 2, 1, 3).reshape(bs, SWA_GROUP * ts, SWA_KV_DIM)
    kbuf = cache_swa_k[0].reshape(bs, wbuf, SWA_KV_DIM).transpose(0, 2, 1)
    vbuf = cache_swa_v[0].reshape(bs, wbuf, SWA_KV_DIM).transpose(0, 2, 1)
    o_swa_s, knew_t, vnew_t = _swa_sample(qs_gt, kvs.reshape(bs, ts, 2 * SWA_KV_DIM), kbuf, vbuf,
                                          bias_s[:, :wbuf], bias_s[:, wbuf:], sink_rows)
    o_swa_s = o_swa_s.reshape(bs, SWA_GROUP, ts, SWA_KV_DIM).transpose(0, 2, 1, 3).reshape(bs * ts, SWA_Q_DIM)

    zero_shift = jnp.zeros((bp, 1, RWKV_IN), F32)
    zero_state = jnp.zeros((bp, RWKV_HEADS, RWKV_HEAD_DIM, RWKV_HEAD_DIM), F32)
    o_rw_p, state_p = _rwkv_branch(xrp.reshape(bp, tp, RWKV_IN), zero_shift, zero_state, rp, ones,
                                   bb=1, tt=256)
    o_rw_s, state_s = _rwkv_branch(xrs.reshape(bs, ts, RWKV_IN), state_rwkv_shift[0], state_rwkv[0], rp, ones,
                                   bb=8, tt=ts)

    mkv = _norm_matmul(mem_prompt.reshape(N_MEM, d), row(mem_norm[0]), w_mem_kv[0].astype(BF16))
    o_mem_p = _mem_prompt(qmp, mkv)
    o_mem_s = _mem_sample(qms.reshape(bs, ts, MEM_DIM), cache_mem_k[0].reshape(bs, N_MEM * MEM_HEADS, MEM_HEAD_DIM),
                          cache_mem_v[0].reshape(bs, N_MEM * MEM_HEADS, MEM_HEAD_DIM)).reshape(bs * ts, MEM_DIM)

    hp = _merge(hp, gm, o_swa_p, o_rw_p, o_mem_p, w_in_b, wo_swa, wo_rw, wo_mem, w_out_b)
    hs = _merge(hs, gm, o_swa_s, o_rw_s, o_mem_s, w_in_b, wo_swa, wo_rw, wo_mem, w_out_b)
    y_prompt = _ffn(hp, g2, wi2, wo2, gf, final_norm=True).reshape(bp, tp, d)
    y_sample = _ffn(hs, g2, wi2, wo2, gf, final_norm=True).reshape(bs, ts, d)

    wp = min(w, tp)
    p_k = kvp[tp - wp:, :SWA_KV_DIM].reshape(1, bp, wp, SWA_KV_HEADS, SWA_HEAD_DIM)
    p_v = kvp[tp - wp:, SWA_KV_DIM:].reshape(1, bp, wp, SWA_KV_HEADS, SWA_HEAD_DIM)
    p_mk = mkv[:, :MEM_DIM].reshape(1, bp, N_MEM, MEM_HEADS, MEM_HEAD_DIM)
    p_mv = mkv[:, MEM_DIM:].reshape(1, bp, N_MEM, MEM_HEADS, MEM_HEAD_DIM)
    s_k = knew_t.transpose(0, 2, 1).reshape(1, bs, wbuf, SWA_KV_HEADS, SWA_HEAD_DIM)
    s_v = vnew_t.transpose(0, 2, 1).reshape(1, bs, wbuf, SWA_KV_HEADS, SWA_HEAD_DIM)
    return (y_prompt, y_sample,
            p_k, p_v, state_p[None], xrp[tp - 1:].reshape(1, bp, 1, RWKV_IN), p_mk, p_mv,
            s_k, s_v, state_s[None], xrs.reshape(bs, ts, RWKV_IN)[:, ts - 1:][None])
```

```python
import functools
import math

import jax
import jax.numpy as jnp
import numpy as np
from jax import lax
from jax.experimental import pallas as pl
from jax.experimental.pallas import tpu as pltpu

F32 = jnp.float32
BF16 = jnp.bfloat16

D_MODEL = 2048
D_FF = 5632
SWA_HEADS = 16
SWA_KV_HEADS = 4
SWA_GROUP = SWA_HEADS // SWA_KV_HEADS
SWA_HEAD_DIM = 64
SWA_Q_DIM = SWA_HEADS * SWA_HEAD_DIM
SWA_KV_DIM = SWA_KV_HEADS * SWA_HEAD_DIM
WINDOW = 128
N_BUCKETS = 32
MAX_DISTANCE = 128
RWKV_HEADS = 8
RWKV_HEAD_DIM = 64
RWKV_DIM = RWKV_HEADS * RWKV_HEAD_DIM
LORA_W = 64
LORA_A = 64
LORA_G = 128
RWKV_IN = 3 * RWKV_DIM + LORA_W + LORA_A + LORA_G
N_MEM = 256
MEM_HEADS = 4
MEM_HEAD_DIM = 128
MEM_DIM = MEM_HEADS * MEM_HEAD_DIM
N_BRANCH = 3
PROJ_DIM = SWA_Q_DIM + 2 * SWA_KV_DIM + RWKV_IN + MEM_DIM
NORM_EPS = 1e-6
GN_EPS = 64e-5
NEG_INF = -1e30

LANES = 128
SUBLANES = 8
VMEM_LIMIT = 56 * 1024 * 1024


def _cparams(*sem):
    return pltpu.CompilerParams(dimension_semantics=sem, vmem_limit_bytes=VMEM_LIMIT)


def _rms(x, g):
    return x * lax.rsqrt(jnp.mean(x * x, axis=-1, keepdims=True) + NORM_EPS) * g


def _dot(a, b):
    return jnp.dot(a, b, preferred_element_type=F32)


def _dot_nt(a, b):
    return lax.dot_general(a, b, (((1,), (1,)), ((), ())), preferred_element_type=F32)


def _dot_hi(a, b):
    return jnp.dot(a, b, preferred_element_type=F32, precision=lax.Precision.HIGHEST)


def _ffn_kernel(x_ref, g_ref, wg_ref, wu_ref, wo_ref, gf_ref, *rest, final_norm, n_riders):
    rider_in, (o_ref, *rider_out), (xn_ref, acc_ref) = rest[:n_riders], rest[n_riders:2 * n_riders + 1], rest[-2:]
    for src, dst in zip(rider_in, rider_out):
        dst[...] = src[...].astype(BF16)
    j = pl.program_id(1)

    @pl.when(j == 0)
    def _():
        xn_ref[...] = _rms(x_ref[...], g_ref[...]).astype(BF16)
        acc_ref[...] = jnp.zeros_like(acc_ref)

    xn = xn_ref[...]
    gate = _dot(xn, wg_ref[...])
    up = _dot(xn, wu_ref[...])
    act = (gate * jax.nn.sigmoid(gate)) * up
    acc_ref[...] += _dot(act.astype(BF16), wo_ref[...])

    @pl.when(j == pl.num_programs(1) - 1)
    def _():
        h = x_ref[...] + 0.5 * acc_ref[...]
        if final_norm:
            h = _rms(h, gf_ref[...])
        o_ref[...] = h


def _rider_tiling(shape, steps):
    rows, cols = shape
    best, best_score = (1, 1), (0, 0)
    for nr in range(1, rows // 16 + 1):
        if rows % nr or (rows // nr) % 16:
            continue
        for nc in range(1, cols // LANES + 1):
            if cols % nc or (cols // nc) % LANES or nr * nc > steps:
                continue
            score = (min(cols // nc, 1024), nr * nc)
            if score > best_score:
                best, best_score = (nr, nc), score
    return best


def _ffn(x, g, wi, wo, gf, *, final_norm, riders=(), tm=512, tf=512):
    m, d = x.shape
    dff = wo.shape[0]
    nf = dff // tf
    steps = (m // tm) * nf
    rider_specs = []
    for arr in riders:
        nr, nc = _rider_tiling(arr.shape, steps)
        br, bc = arr.shape[0] // nr, arr.shape[1] // nc
        rider_specs.append(pl.BlockSpec((br, bc), functools.partial(
            lambda i, j, nc, last: (jnp.minimum(i * nf + j, last) // nc, jnp.minimum(i * nf + j, last) % nc),
            nc=nc, last=nr * nc - 1)))
    out = pl.pallas_call(
        functools.partial(_ffn_kernel, final_norm=final_norm, n_riders=len(riders)),
        out_shape=[jax.ShapeDtypeStruct((m, d), F32)] + [jax.ShapeDtypeStruct(arr.shape, BF16) for arr in riders],
        grid=(m // tm, nf),
        in_specs=[
            pl.BlockSpec((tm, d), lambda i, j: (i, 0)),
            pl.BlockSpec((1, d), lambda i, j: (0, 0)),
            pl.BlockSpec((d, tf), lambda i, j: (0, j)),
            pl.BlockSpec((d, tf), lambda i, j: (0, j + nf)),
            pl.BlockSpec((tf, d), lambda i, j: (j, 0)),
            pl.BlockSpec((1, d), lambda i, j: (0, 0)),
        ] + rider_specs,
        out_specs=[pl.BlockSpec((tm, d), lambda i, j: (i, 0))] + rider_specs,
        scratch_shapes=[pltpu.VMEM((tm, d), BF16), pltpu.VMEM((tm, d), F32)],
        compiler_params=_cparams("arbitrary", "arbitrary"),
        name="ffn_final" if final_norm else "ffn",
    )(x, g, wi, wi, wo, gf, *riders)
    return out if riders else out[0]


def _inproj_kernel(h_ref, g_ref, wq_ref, w_ref, q_ref, kv_ref, xr_ref, qm_ref):
    u = _rms(h_ref[...], g_ref[...]).astype(BF16)
    c0, c1, c2 = SWA_Q_DIM, SWA_Q_DIM + 2 * SWA_KV_DIM, SWA_Q_DIM + 2 * SWA_KV_DIM + RWKV_IN
    q_ref[...] = _dot(u, wq_ref[...]).astype(BF16)
    kv_ref[...] = _dot(u, w_ref[:, c0:c1])
    xr_ref[...] = _dot(u, w_ref[:, c1:c2])
    qm_ref[...] = _dot(u, w_ref[:, c2:PROJ_DIM]).astype(BF16)


def _inproj(h, g, wq, w, *, tm=256):
    m, d = h.shape
    row = lambda i: (i, 0)
    return pl.pallas_call(
        _inproj_kernel,
        out_shape=(
            jax.ShapeDtypeStruct((m, SWA_Q_DIM), BF16),
            jax.ShapeDtypeStruct((m, 2 * SWA_KV_DIM), F32),
            jax.ShapeDtypeStruct((m, RWKV_IN), F32),
            jax.ShapeDtypeStruct((m, MEM_DIM), BF16),
        ),
        grid=(m // tm,),
        in_specs=[
            pl.BlockSpec((tm, d), row),
            pl.BlockSpec((1, d), lambda i: (0, 0)),
            pl.BlockSpec((d, SWA_Q_DIM), lambda i: (0, 0), pipeline_mode=pl.Buffered(1)),
            pl.BlockSpec((d, PROJ_DIM), lambda i: (0, 0), pipeline_mode=pl.Buffered(1)),
        ],
        out_specs=(
            pl.BlockSpec((tm, SWA_Q_DIM), row),
            pl.BlockSpec((tm, 2 * SWA_KV_DIM), row),
            pl.BlockSpec((tm, RWKV_IN), row),
            pl.BlockSpec((tm, MEM_DIM), row),
        ),
        compiler_params=_cparams("parallel"),
        name="inproj",
    )(h, g, wq, w)


def _norm_matmul_kernel(x_ref, g_ref, w_ref, o_ref):
    o_ref[...] = _dot(_rms(x_ref[...], g_ref[...]).astype(BF16), w_ref[...])


def _norm_matmul(x, g, w, *, tn=512):
    m, d = x.shape
    n = w.shape[1]
    return pl.pallas_call(
        _norm_matmul_kernel,
        out_shape=jax.ShapeDtypeStruct((m, n), F32),
        grid=(n // tn,),
        in_specs=[
            pl.BlockSpec((m, d), lambda j: (0, 0)),
            pl.BlockSpec((1, d), lambda j: (0, 0)),
            pl.BlockSpec((d, tn), lambda j: (0, j)),
        ],
        out_specs=pl.BlockSpec((m, tn), lambda j: (0, j)),
        compiler_params=_cparams("parallel"),
        name="norm_matmul",
    )(x, g, w)


def _sink_softmax(logits, sink):
    m = jnp.maximum(jnp.max(logits, axis=-1, keepdims=True), sink)
    p = jnp.exp(logits - m)
    denom = jnp.sum(p, axis=-1, keepdims=True) + jnp.exp(sink - m)
    return p * (1.0 / denom)


def _swa_prompt_kernel(q_ref, kvc_ref, kvp_ref, bias_ref, sink_ref, o_ref):
    i = pl.program_id(0)
    w = WINDOW
    rows = SWA_KV_HEADS * w
    k = jnp.concatenate([kvp_ref[:, 0:SWA_KV_DIM], kvc_ref[:, 0:SWA_KV_DIM]], axis=0).astype(BF16)
    v = jnp.concatenate([kvp_ref[:, SWA_KV_DIM:], kvc_ref[:, SWA_KV_DIM:]], axis=0).astype(BF16)
    qpos = lax.broadcasted_iota(jnp.int32, (rows, 2 * w), 0) % w
    col = lax.broadcasted_iota(jnp.int32, (rows, 2 * w), 1)
    dist = qpos + w - col
    valid = (dist >= 0) & (dist < w) & ((col >= w) | (i > 0))
    lane_head = lax.broadcasted_iota(jnp.int32, (w, SWA_KV_DIM), 1) // SWA_HEAD_DIM
    scale = SWA_HEAD_DIM ** -0.5
    groups = range(SWA_GROUP)
    logits = []
    for g in groups:
        qg = q_ref[:, g * SWA_KV_DIM:(g + 1) * SWA_KV_DIM].astype(F32) * scale
        qs = jnp.concatenate([jnp.where(lane_head == kvh, qg, 0.0) for kvh in range(SWA_KV_HEADS)], axis=0)
        lg = _dot_nt(qs.astype(BF16), k)
        logits.append(jnp.where(valid, lg + bias_ref[g], NEG_INF))
    m = [jnp.maximum(jnp.max(logits[g], axis=-1, keepdims=True), sink_ref[g]) for g in groups]
    p = [jnp.exp(logits[g] - m[g]) for g in groups]
    inv = [1.0 / (jnp.sum(p[g], axis=-1, keepdims=True) + jnp.exp(sink_ref[g] - m[g])) for g in groups]
    ov = [_dot((p[g] * inv[g]).astype(BF16), v) for g in groups]
    for g in groups:
        og = jnp.zeros((w, SWA_KV_DIM), F32)
        for kvh in range(SWA_KV_HEADS):
            og = jnp.where(lane_head == kvh, ov[g][kvh * w:(kvh + 1) * w], og)
        o_ref[:, g * SWA_KV_DIM:(g + 1) * SWA_KV_DIM] = og.astype(BF16)


def _swa_prompt(q, kv, bias, sink_rows):
    t = q.shape[0]
    w = WINDOW
    rows = SWA_KV_HEADS * w
    return pl.pallas_call(
        _swa_prompt_kernel,
        out_shape=jax.ShapeDtypeStruct((t, SWA_Q_DIM), BF16),
        grid=(t // w,),
        in_specs=[
            pl.BlockSpec((w, SWA_Q_DIM), lambda i: (i, 0)),
            pl.BlockSpec((w, 2 * SWA_KV_DIM), lambda i: (i, 0)),
            pl.BlockSpec((w, 2 * SWA_KV_DIM), lambda i: (jnp.maximum(i - 1, 0), 0)),
            pl.BlockSpec((SWA_GROUP, rows, 2 * w), lambda i: (0, 0, 0)),
            pl.BlockSpec((SWA_GROUP, rows, 1), lambda i: (0, 0, 0)),
        ],
        out_specs=pl.BlockSpec((w, SWA_Q_DIM), lambda i: (i, 0)),
        compiler_params=_cparams("parallel"),
        name="swa_prompt",
    )(q, kv, kv, bias, sink_rows)


def _swa_sample_kernel(q_ref, kvn_ref, kb_ref, vb_ref, bias_b_ref, bias_n_ref, sink_ref, o_ref, ko_ref, vo_ref,
                       *, bb, t):
    gt = SWA_GROUP * t
    rows = SWA_KV_HEADS * gt
    w = kb_ref.shape[2]
    scale = SWA_HEAD_DIM ** -0.5
    lane_head = lax.broadcasted_iota(jnp.int32, (gt, SWA_KV_DIM), 1) // SWA_HEAD_DIM
    tok = lax.broadcasted_iota(jnp.int32, (rows, w), 0) % t
    keyj = lax.broadcasted_iota(jnp.int32, (rows, w), 1)
    valid_b = (tok + w - keyj) < WINDOW
    tok_n = lax.broadcasted_iota(jnp.int32, (rows, 1), 0) % t
    sink = sink_ref[...]
    bs = range(bb)
    toks = range(t)
    qall = [jnp.concatenate([jnp.where(lane_head == kvh, q_ref[b].astype(F32), 0.0) for kvh in range(SWA_KV_HEADS)],
                            axis=0) for b in bs]
    kvn = [kvn_ref[b] for b in bs]
    lb = [_dot(qall[b].astype(BF16), kb_ref[b].astype(BF16)) for b in bs]
    lb = [jnp.where(valid_b, lb[b] * scale + bias_b_ref[...], NEG_INF) for b in bs]
    ln = [[jnp.sum(qall[b] * kvn[b][j:j + 1, 0:SWA_KV_DIM], axis=-1, keepdims=True) for j in toks] for b in bs]
    ln = [[jnp.where(tok_n >= j, ln[b][j] * scale + bias_n_ref[:, j:j + 1], NEG_INF) for j in toks] for b in bs]
    m = [jnp.maximum(jnp.max(lb[b], axis=-1, keepdims=True), sink) for b in bs]
    m = [functools.reduce(jnp.maximum, ln[b], m[b]) for b in bs]
    pb = [jnp.exp(lb[b] - m[b]) for b in bs]
    pn = [[jnp.exp(ln[b][j] - m[b]) for j in toks] for b in bs]
    denom = [jnp.sum(pb[b], axis=-1, keepdims=True) + jnp.exp(sink - m[b]) for b in bs]
    inv = [1.0 / functools.reduce(jnp.add, pn[b], denom[b]) for b in bs]
    oall = [_dot_nt((pb[b] * inv[b]).astype(BF16), vb_ref[b].astype(BF16)) for b in bs]
    for b in bs:
        ob = oall[b]
        for j in toks:
            ob = ob + (pn[b][j] * inv[b]) * kvn[b][j:j + 1, SWA_KV_DIM:]
        og = jnp.zeros((gt, SWA_KV_DIM), F32)
        for kvh in range(SWA_KV_HEADS):
            og = jnp.where(lane_head == kvh, ob[kvh * gt:(kvh + 1) * gt], og)
        o_ref[b] = og.astype(BF16)
    pos = lax.broadcasted_iota(jnp.int32, (SWA_KV_DIM, w), 1)
    pad = jnp.zeros((w - SUBLANES, 2 * SWA_KV_DIM), F32)
    row8 = lax.broadcasted_iota(jnp.int32, (SUBLANES, 2 * SWA_KV_DIM), 0)
    for b in bs:
        last8 = jnp.zeros((SUBLANES, 2 * SWA_KV_DIM), F32)
        for j in toks:
            last8 = jnp.where(row8 == SUBLANES - t + j, kvn[b][j:j + 1], last8)
        tail_t = jnp.concatenate([pad, last8], axis=0).T
        ko_ref[b] = jnp.where(pos >= w - t, tail_t[:SWA_KV_DIM], pltpu.roll(kb_ref[b], w - t, axis=1))
        vo_ref[b] = jnp.where(pos >= w - t, tail_t[SWA_KV_DIM:], pltpu.roll(vb_ref[b], w - t, axis=1))


def _swa_sample(q, kvn, kbuf, vbuf, bias_b, bias_n, sink_rows, *, bb=8):
    b, gt, _ = q.shape
    t = kvn.shape[1]
    w = kbuf.shape[2]
    rows = SWA_KV_HEADS * gt
    blk = lambda i: (i, 0, 0)
    const = lambda i: (0, 0)
    cache = pl.BlockSpec((bb, SWA_KV_DIM, w), blk)
    return pl.pallas_call(
        functools.partial(_swa_sample_kernel, bb=bb, t=t),
        out_shape=(jax.ShapeDtypeStruct((b, gt, SWA_KV_DIM), BF16),
                   jax.ShapeDtypeStruct(kbuf.shape, F32), jax.ShapeDtypeStruct(vbuf.shape, F32)),
        grid=(b // bb,),
        in_specs=[
            pl.BlockSpec((bb, gt, SWA_KV_DIM), blk),
            pl.BlockSpec((bb, t, 2 * SWA_KV_DIM), blk),
            cache, cache,
            pl.BlockSpec((rows, w), const),
            pl.BlockSpec((rows, t), const),
            pl.BlockSpec((rows, 1), const),
        ],
        out_specs=(pl.BlockSpec((bb, gt, SWA_KV_DIM), blk), cache, cache),
        compiler_params=_cparams("parallel"),
        name="swa_sample",
    )(q, kvn, kbuf, vbuf, bias_b, bias_n, sink_rows)


def _softmax_rows(x):
    m = jnp.max(x, axis=-1, keepdims=True)
    p = jnp.exp(x - m)
    return p * (1.0 / jnp.sum(p, axis=-1, keepdims=True))


def _mem_heads(q, mk, mv):
    scale = MEM_HEAD_DIM ** -0.5
    outs = []
    for h in range(MEM_HEADS):
        sl = slice(h * MEM_HEAD_DIM, (h + 1) * MEM_HEAD_DIM)
        p = _softmax_rows(_dot_nt(q[:, sl], mk[:, sl]) * scale)
        outs.append(_dot(p.astype(BF16), mv[:, sl]))
    return jnp.concatenate(outs, axis=-1)


def _mem_prompt_kernel(q_ref, mk_ref, mv_ref, o_ref):
    o_ref[...] = _mem_heads(q_ref[...], mk_ref[...].astype(BF16), mv_ref[...].astype(BF16)).astype(BF16)


def _mem_prompt(q, mkv, *, tm=512):
    m = q.shape[0]
    return pl.pallas_call(
        _mem_prompt_kernel,
        out_shape=jax.ShapeDtypeStruct((m, MEM_DIM), BF16),
        grid=(m // tm,),
        in_specs=[
            pl.BlockSpec((tm, MEM_DIM), lambda i: (i, 0)),
            pl.BlockSpec((N_MEM, MEM_DIM), lambda i: (0, 0)),
            pl.BlockSpec((N_MEM, MEM_DIM), lambda i: (0, 1)),
        ],
        out_specs=pl.BlockSpec((tm, MEM_DIM), lambda i: (i, 0)),
        compiler_params=_cparams("parallel"),
        name="mem_prompt",
    )(q, mkv, mkv)


def _mem_sample_kernel(q_ref, mk_ref, mv_ref, o_ref, *, bb):
    scale = MEM_HEAD_DIM ** -0.5
    pairs = [(b, h) for b in range(bb) for h in range(MEM_HEADS)]
    rows = lambda h: pl.ds(h, N_MEM, stride=MEM_HEADS)
    cols = lambda h: slice(h * MEM_HEAD_DIM, (h + 1) * MEM_HEAD_DIM)
    q = [q_ref[b] for b in range(bb)]
    x = [_dot_nt(q[b][:, cols(h)], mk_ref[b, rows(h), :].astype(BF16)) * scale for b, h in pairs]
    m = [jnp.max(xi, axis=-1, keepdims=True) for xi in x]
    p = [jnp.exp(xi - mi) for xi, mi in zip(x, m)]
    inv = [1.0 / jnp.sum(pi, axis=-1, keepdims=True) for pi in p]
    o = [_dot((pi * ii).astype(BF16), mv_ref[b, rows(h), :].astype(BF16)) for (b, h), pi, ii in zip(pairs, p, inv)]
    for b in range(bb):
        o_ref[b] = jnp.concatenate(o[b * MEM_HEADS:(b + 1) * MEM_HEADS], axis=-1).astype(BF16)


def _mem_sample(q, mk, mv, *, bb=8):
    b, t, _ = q.shape
    blk = lambda i: (i, 0, 0)
    return pl.pallas_call(
        functools.partial(_mem_sample_kernel, bb=bb),
        out_shape=jax.ShapeDtypeStruct((b, t, MEM_DIM), BF16),
        grid=(b // bb,),
        in_specs=[
            pl.BlockSpec((bb, t, MEM_DIM), blk),
            pl.BlockSpec((bb, N_MEM * MEM_HEADS, MEM_HEAD_DIM), blk),
            pl.BlockSpec((bb, N_MEM * MEM_HEADS, MEM_HEAD_DIM), blk),
        ],
        out_specs=pl.BlockSpec((bb, t, MEM_DIM), blk),
        compiler_params=_cparams("parallel"),
        name="mem_sample",
    )(q, mk, mv)


def _head_sum(x, ones_ref):
    hi = x.astype(BF16)
    lo = (x - hi.astype(F32)).astype(BF16)
    ones = ones_ref[...]
    w = ones.shape[0]
    return jnp.concatenate([_dot(hi[:, c:c + w], ones) + _dot(lo[:, c:c + w], ones) for c in range(0, x.shape[1], w)],
                           axis=1)


def _token_shift(x_ref, pre_ref, start_ref, mu_ref, *, seq, tm, tile):
    x = x_ref[...]
    row = lax.broadcasted_iota(jnp.int32, x.shape, 0)
    shifted = pltpu.roll(x, 1, axis=0)
    if seq >= tm:
        is_start = (tile * tm) % seq == 0
        first_prev = jnp.where(is_start, start_ref[0], pre_ref[SUBLANES - 1:SUBLANES, :])
        prev = jnp.where(row == 0, first_prev, shifted)
    else:
        prev = jnp.where(row % seq == 0, start_ref[...], shifted)
    return x + mu_ref[...] * (prev - x)


def _rwkv_features(xs, w0_ref, ww2_ref, a0_ref, aw2_ref, gw2_ref, kk_ref, ka_ref, rk_ref, ones_ref):
    d = RWKV_DIM
    r = xs[:, 0:d]
    k = xs[:, d:2 * d]
    v = xs[:, 2 * d:3 * d]
    lw = xs[:, 3 * d:3 * d + LORA_W]
    la = xs[:, 3 * d + LORA_W:3 * d + LORA_W + LORA_A]
    lg = xs[:, 3 * d + LORA_W + LORA_A:]
    wpre = w0_ref[...] + _dot_hi(jnp.tanh(lw), ww2_ref[...])
    w = -jax.nn.softplus(-wpre) - 0.5
    log_decay = -jnp.exp(w)
    a = jax.nn.sigmoid(a0_ref[...] + _dot_hi(la, aw2_ref[...]))
    g = _dot_hi(jax.nn.sigmoid(lg), gw2_ref[...])
    kk = k * kk_ref[...]
    kk = kk / jnp.maximum(jnp.sqrt(_head_sum(kk * kk, ones_ref)), 1e-12)
    kh = k * (1.0 + (a - 1.0) * ka_ref[...])
    bonus = _head_sum(r * kh * rk_ref[...], ones_ref) * v
    return r, log_decay, kh, v, kk, kk * a, g, bonus


def _rwkv_prep_kernel(x_ref, pre_ref, start_ref, mu_ref, w0_ref, ww2_ref, a0_ref, aw2_ref, gw2_ref, kk_ref, ka_ref,
                      rk_ref, ones_ref, *outs, seq, tm):
    xs = _token_shift(x_ref, pre_ref, start_ref, mu_ref, seq=seq, tm=tm, tile=pl.program_id(0))
    feats = _rwkv_features(xs, w0_ref, ww2_ref, a0_ref, aw2_ref, gw2_ref, kk_ref, ka_ref, rk_ref, ones_ref)
    for o_ref, val in zip(outs, feats):
        o_ref[...] = val


def _rwkv_prep(x, shift0, p, ones, *, seq, tm=256):
    m = x.shape[0]
    row = lambda i: (i, 0)
    const = lambda i: (0, 0)
    vec = lambda n: pl.BlockSpec((1, n), const)
    out = jax.ShapeDtypeStruct((m, RWKV_DIM), F32)
    if seq >= tm:
        assert seq % tm == 0
        start = shift0
        start_spec = pl.BlockSpec((1, 1, RWKV_IN), lambda i: ((i * tm) // seq, 0, 0))
    else:
        assert tm % seq == 0
        start = jnp.repeat(shift0[:, 0], seq, axis=0)
        start_spec = pl.BlockSpec((tm, RWKV_IN), row)
    pre_blocks = tm // SUBLANES
    return pl.pallas_call(
        functools.partial(_rwkv_prep_kernel, seq=seq, tm=tm),
        out_shape=(out,) * 8,
        grid=(m // tm,),
        in_specs=[
            pl.BlockSpec((tm, RWKV_IN), row),
            pl.BlockSpec((SUBLANES, RWKV_IN), lambda i: (jnp.maximum(i * pre_blocks - 1, 0), 0)),
            start_spec,
            vec(RWKV_IN), vec(RWKV_DIM),
            pl.BlockSpec((LORA_W, RWKV_DIM), const),
            vec(RWKV_DIM),
            pl.BlockSpec((LORA_A, RWKV_DIM), const),
            pl.BlockSpec((LORA_G, RWKV_DIM), const),
            vec(RWKV_DIM), vec(RWKV_DIM), vec(RWKV_DIM),
            pl.BlockSpec(ones.shape, const),
        ],
        out_specs=(pl.BlockSpec((tm, RWKV_DIM), row),) * 8,
        compiler_params=_cparams("parallel"),
        name="rwkv_prep",
    )(x, x, start, p["mu"], p["w0"], p["w_w2"], p["a0"], p["a_w2"], p["g_w2"], p["k_k"], p["k_a"], p["r_k"], ones)


def _rwkv_lanes_kernel(r_ref, lw_ref, k_ref, v_ref, kk_ref, kka_ref, s_ref, y_ref, so_ref, v_scr, y_scr, *, t):
    n = RWKV_HEAD_DIM
    nb = s_ref.shape[-1]
    heads = range(2)

    def token_major(ref, j):
        return ref[pl.ds(j, nb, stride=t), :].T

    for j in range(t):
        v_scr[...] = token_major(v_ref, j)
        r_t, k_t, kk_t, kka_t = (token_major(ref, j) for ref in (r_ref, k_ref, kk_ref, kka_ref))
        w_t = jnp.exp(token_major(lw_ref, j))
        src = s_ref if j == 0 else so_ref

        def value_group(g, carry):
            rows = pl.multiple_of(g * SUBLANES, SUBLANES)
            ys = [[] for _ in heads]
            vg = [v_scr[pl.ds(h * n + rows, SUBLANES), :] for h in heads]
            for i in range(SUBLANES):
                for h in heads:
                    f = slice(h * n, (h + 1) * n)
                    s = src[h, rows + i]
                    sa = jnp.sum(s * (-kk_t[f]), axis=0, keepdims=True)
                    s = s * w_t[f] + sa * kka_t[f] + vg[h][i:i + 1] * k_t[f]
                    so_ref[h, rows + i] = s
                    ys[h].append(jnp.sum(s * r_t[f], axis=0, keepdims=True))
            for h in heads:
                y_scr[pl.ds(h * n + rows, SUBLANES), :] = jnp.concatenate(ys[h], axis=0)
            return carry

        lax.fori_loop(0, n // SUBLANES, value_group, 0)
        y_ref[j] = y_scr[...].T


def _rwkv_lanes(r, lw, k, v, kk, kka, s0, *, t):
    m, d = r.shape
    nb = m // t
    n = RWKV_HEAD_DIM
    assert nb == LANES, "one batch per lane"
    tok = pl.BlockSpec((m, LANES), lambda p: (0, p))
    st = pl.BlockSpec((2, n, n, nb), lambda p: (p, 0, 0, 0))
    return pl.pallas_call(
        functools.partial(_rwkv_lanes_kernel, t=t),
        out_shape=(jax.ShapeDtypeStruct((t, nb, d), F32), jax.ShapeDtypeStruct((RWKV_HEADS, n, n, nb), F32)),
        grid=(RWKV_HEADS // 2,),
        in_specs=[tok] * 6 + [st],
        out_specs=(pl.BlockSpec((t, nb, LANES), lambda p: (0, 0, p)), st),
        scratch_shapes=[pltpu.VMEM((LANES, nb), F32), pltpu.VMEM((LANES, nb), F32)],
        compiler_params=_cparams("parallel"),
        name="rwkv_lanes",
    )(r, lw, k, v, kk, kka, s0)


CHUNK = 64
GROUP_HEADS = 4
GROUP_W = GROUP_HEADS * RWKV_HEAD_DIM
N_GROUPS = RWKV_HEADS // GROUP_HEADS
(MASK_SAME, MASK_STRICT, MASK_INCL, MASK_LEVEL0) = (0, 1, 2, 3)
N_LEVELS = int(math.log2(CHUNK))


def _chunk_masks():
    i = np.arange(GROUP_W)
    same = (i[:, None] // CHUNK) == (i[None, :] // CHUNK)
    masks = [same, same & (i[None, :] < i[:, None]), same & (i[None, :] <= i[:, None])]
    for lvl in range(N_LEVELS):
        m = 1 << lvl
        masks.append(((i[:, None] // (2 * m)) == (i[None, :] // (2 * m))) & ((i[:, None] // m) != (i[None, :] // m))
                     & (i[None, :] < i[:, None]))
    return np.stack(masks).astype(np.float32)


def _rwkv_prompt_kernel(x_ref, pre_ref, start_ref, mu_ref, w0_ref, ww2_ref, a0_ref, aw2_ref, gw2_ref, kk_p_ref,
                        ka_ref, rk_ref, ones_ref, lnw_ref, lnb_ref, st0_ref, tri_ref, eye_ref, mask_ref,
                        o_ref, sto_ref, st_scr, xs_ref, y_ref, r_set, lw_set, k_set, v_set, kk_set, kka_set, g_set,
                        bonus_set, *, seq, tm, n_tiles):
    step = pl.program_id(0)
    cur = (step + 1) % 2
    nxt = step % 2
    sets = (r_set, lw_set, k_set, v_set, kk_set, kka_set, g_set, bonus_set)
    r_ref, lw_ref, k_ref, v_ref, kk_ref, kka_ref = (s.at[cur] for s in sets[:6])

    @pl.when(step == 0)
    def _():
        st_scr[...] = st0_ref[...]
        for s in sets:
            s[1] = jnp.zeros(s.shape[1:], F32)

    xs_ref[...] = _token_shift(x_ref, pre_ref, start_ref, mu_ref, seq=seq, tm=tm, tile=jnp.minimum(step, n_tiles - 1))
    n_chunks = tm // CHUNK

    piece = 2 * CHUNK
    def features(c):
        rows = slice(c * piece, (c + 1) * piece)
        feats = _rwkv_features(xs_ref[rows, :], w0_ref, ww2_ref, a0_ref, aw2_ref, gw2_ref, kk_p_ref, ka_ref, rk_ref,
                               ones_ref)
        for s, val in zip(sets, feats):
            s[nxt, rows, :] = val

    eye = eye_ref[...]
    tri = tri_ref[...]
    tile_rows = lambda x: jnp.concatenate([x] * GROUP_HEADS, axis=0)
    block_diag = lambda x: (tile_rows(x) * mask_ref[MASK_SAME]).astype(BF16)

    chains = [(slice(c * CHUNK, (c + 1) * CHUNK), slice(g * GROUP_W, (g + 1) * GROUP_W))
              for c in range(n_chunks) for g in range(N_GROUPS)]
    each = lambda f, *cols: [f(*args) for args in zip(*cols)]
    same, strict, incl = mask_ref[MASK_SAME], mask_ref[MASK_STRICT], mask_ref[MASK_INCL]

    def cum_decay(lw):
        h1 = lw.astype(BF16)
        r1 = lw - h1.astype(F32)
        h2 = r1.astype(BF16)
        h3 = (r1 - h2.astype(F32)).astype(BF16)
        return _dot(tri, h1) + _dot(tri, h2) + _dot(tri, h3)

    lw = [lw_ref[rows, sl] for rows, sl in chains]
    kka = [kka_ref[rows, sl] for rows, sl in chains]
    k = [k_ref[rows, sl] for rows, sl in chains]
    cum = each(cum_decay, lw)
    cum_last = each(lambda c: c[CHUNK - 1:CHUNK, :], cum)
    p_inv = each(lambda c: jnp.exp(-c), cum)
    p_tail = each(lambda c, cl: jnp.exp(cl - c), cum, cum_last)
    a_bd = [block_diag(-kk_ref[rows, sl] * jnp.exp(c - l)) for (rows, sl), c, l in zip(chains, cum, lw)]
    r_f = [tile_rows(r_ref[rows, sl] * jnp.exp(c)) * same for (rows, sl), c in zip(chains, cum)]
    r_bd = each(lambda x: x.astype(BF16), r_f)
    v_bd = [block_diag(v_ref[rows, sl]) for rows, sl in chains]
    b_rep = each(lambda x, p: tile_rows((x * p).astype(BF16)), kka, p_inv)
    k_rep = each(lambda x, p: tile_rows((x * p).astype(BF16)), k, p_inv)
    bh_rep = each(lambda x, p: tile_rows(x * p), kka, p_tail)
    kh_rep = each(lambda x, p: tile_rows(x * p), k, p_tail)

    l_ab_f = each(lambda a, b: _dot_nt(a, b) * strict, a_bd, b_rep)
    l_ab = each(lambda x: x.astype(BF16), l_ab_f)
    l_ak = each(lambda a, b: (_dot_nt(a, b) * strict).astype(BF16), a_bd, k_rep)
    m_rb = each(lambda a, b: (_dot_nt(a, b) * incl).astype(BF16), r_bd, b_rep)
    m_rk = each(lambda a, b: (_dot_nt(a, b) * incl).astype(BF16), r_bd, k_rep)
    bh_t = each(lambda x: (x.T * same).astype(BF16), bh_rep)
    kh_t = each(lambda x: (x.T * same).astype(BF16), kh_rep)

    pending = list(range(tm // piece))

    def next_features():
        if pending:
            features(pending.pop(0))

    d = each(lambda l: eye + l * mask_ref[MASK_LEVEL0], l_ab_f)
    for lvl in range(1, N_LEVELS):
        d_b = each(lambda x: x.astype(BF16), d)
        x = each(lambda l, db: (_dot(l, db) * mask_ref[MASK_LEVEL0 + lvl]).astype(BF16), l_ab, d_b)
        next_features()
        d = each(lambda dd, db, xx: dd + _dot(db, xx), d, d_b, x)
    t_b = each(lambda x: x.astype(BF16), d)
    while pending:
        next_features()

    wm = each(lambda a, b, vv: _dot(jnp.concatenate([a, b], axis=0), vv), l_ak, m_rk, v_bd)
    twa = each(lambda t, w, a: _dot(t, jnp.concatenate([w[:GROUP_W].astype(BF16), a], axis=1)).astype(BF16),
               t_b, wm, a_bd)
    ry = each(_dot, m_rb, twa)
    mn = each(_dot, bh_t, twa)
    khv = each(_dot, kh_t, v_bd)
    y0 = each(lambda a, w: a[:, :GROUP_W] + w[GROUP_W:], ry, wm)
    n_x = each(lambda a, b: a[:, :GROUP_W] + b, mn, khv)
    mr = each(lambda a, cl, rf, b: jnp.concatenate(
        [(eye * jnp.exp(cl) + a[:, GROUP_W:]).astype(BF16), (rf + b[:, GROUP_W:]).astype(BF16)], axis=0),
        mn, cum_last, r_f, ry)

    st = [st_scr[g] for g in range(N_GROUPS)]
    for i, (rows, sl) in enumerate(chains):
        g = i % N_GROUPS
        ys = _dot(mr[i], st[g].astype(BF16))
        st[g] = ys[:GROUP_W] + n_x[i]
        y_bd = ys[GROUP_W:] + y0[i]
        y = y_bd[0:CHUNK]
        for h in range(1, GROUP_HEADS):
            y = y + y_bd[h * CHUNK:(h + 1) * CHUNK]
        y_ref[rows, sl] = y
    for g in range(N_GROUPS):
        st_scr[g] = st[g]
    o_ref[...] = _rwkv_output(y_ref[...], bonus_set[cur], g_set[cur], lnw_ref, lnb_ref, ones_ref)

    @pl.when(step == pl.num_programs(0) - 1)
    def _():
        sto_ref[...] = st_scr[...]


def _rwkv_prompt(x, shift0, s0, p, ones, *, chunks_per_step=4):
    t = x.shape[0]
    d = RWKV_DIM
    n = RWKV_HEAD_DIM
    tt = CHUNK * chunks_per_step
    assert CHUNK == n and t % tt == 0
    st0 = jnp.einsum("ghvk,hj->ghkjv", s0.reshape(N_GROUPS, GROUP_HEADS, n, n), jnp.eye(GROUP_HEADS, dtype=F32))
    st0 = st0.reshape(N_GROUPS, GROUP_W, GROUP_W)
    tri = jnp.asarray(np.tril(np.ones((CHUNK, CHUNK), np.float32)), BF16)
    eye = jnp.eye(GROUP_W, dtype=F32)
    masks = jnp.asarray(_chunk_masks())
    const = lambda c: (0, 0)
    vec = lambda width: pl.BlockSpec((1, width), const)
    st_spec = pl.BlockSpec((N_GROUPS, GROUP_W, GROUP_W), lambda c: (0, 0, 0))
    feature_set = pltpu.VMEM((2, tt, d), F32)
    pre_blocks = tt // SUBLANES
    n_tiles = t // tt
    fill = lambda c: jnp.minimum(c, n_tiles - 1)
    o, st = pl.pallas_call(
        functools.partial(_rwkv_prompt_kernel, seq=t, tm=tt, n_tiles=n_tiles),
        out_shape=(jax.ShapeDtypeStruct((t, d), BF16), jax.ShapeDtypeStruct((N_GROUPS, GROUP_W, GROUP_W), F32)),
        grid=(n_tiles + 1,),
        in_specs=[
            pl.BlockSpec((tt, RWKV_IN), lambda c: (fill(c), 0)),
            pl.BlockSpec((SUBLANES, RWKV_IN), lambda c: (jnp.maximum(fill(c) * pre_blocks - 1, 0), 0)),
            pl.BlockSpec((1, 1, RWKV_IN), lambda c: (0, 0, 0)),
            vec(RWKV_IN), vec(d),
            pl.BlockSpec((LORA_W, d), const),
            vec(d),
            pl.BlockSpec((LORA_A, d), const),
            pl.BlockSpec((LORA_G, d), const),
            vec(d), vec(d), vec(d),
            pl.BlockSpec(ones.shape, const),
            vec(d), vec(d),
            st_spec,
            pl.BlockSpec((CHUNK, CHUNK), const),
            pl.BlockSpec((GROUP_W, GROUP_W), const),
            pl.BlockSpec(masks.shape, lambda c: (0, 0, 0)),
        ],
        out_specs=(pl.BlockSpec((tt, d), lambda c: (jnp.maximum(c - 1, 0), 0)), st_spec),
        scratch_shapes=[pltpu.VMEM((N_GROUPS, GROUP_W, GROUP_W), F32), pltpu.VMEM((tt, RWKV_IN), F32),
                        pltpu.VMEM((tt, d), F32)] + [feature_set] * 8,
        compiler_params=_cparams("arbitrary"),
        name="rwkv_prompt",
    )(x, x, shift0, p["mu"], p["w0"], p["w_w2"], p["a0"], p["a_w2"], p["g_w2"], p["k_k"], p["k_a"], p["r_k"], ones,
      p["ln_w"], p["ln_b"], st0, tri, eye, masks)
    st5 = st.reshape(N_GROUPS, GROUP_HEADS, n, GROUP_HEADS, n)
    s_new = jnp.einsum("ghkjv,hj->ghvk", st5, jnp.eye(GROUP_HEADS, dtype=F32)).reshape(RWKV_HEADS, n, n)
    return o, s_new


def _rwkv_output(y, bonus, gate, lnw_ref, lnb_ref, ones_ref):
    inv_n = 1.0 / RWKV_HEAD_DIM
    mu = _head_sum(y, ones_ref) * inv_n
    dlt = y - mu
    var = _head_sum(dlt * dlt, ones_ref) * inv_n
    yn = dlt * lax.rsqrt(var + GN_EPS) * lnw_ref[...] + lnb_ref[...]
    return ((yn + bonus) * gate).astype(BF16)


def _rwkv_post_kernel(y_ref, bonus_ref, g_ref, lnw_ref, lnb_ref, ones_ref, o_ref):
    o_ref[...] = _rwkv_output(y_ref[...], bonus_ref[...], g_ref[...], lnw_ref, lnb_ref, ones_ref)


def _rwkv_post(y, bonus, g, lnw, lnb, ones, *, tm=256):
    m = y.shape[0]
    row = lambda i: (i, 0)
    const = lambda i: (0, 0)
    tile = pl.BlockSpec((tm, RWKV_DIM), row)
    return pl.pallas_call(
        _rwkv_post_kernel,
        out_shape=jax.ShapeDtypeStruct((m, RWKV_DIM), BF16),
        grid=(m // tm,),
        in_specs=[tile, tile, tile, pl.BlockSpec((1, RWKV_DIM), const), pl.BlockSpec((1, RWKV_DIM), const),
                  pl.BlockSpec(ones.shape, const)],
        out_specs=tile,
        compiler_params=_cparams("parallel"),
        name="rwkv_post",
    )(y, bonus, g, lnw, lnb, ones)


GATE_BLOCK = math.gcd(PROJ_DIM, D_MODEL)


def _merge_kernel(h_ref, g_ref, os_ref, or_ref, om_ref, wos_ref, wor_ref, wom_ref, wout_ref, *rest, parts):
    gate_refs, (o_ref, u_ref, acc_ref) = rest[:N_BRANCH * parts], rest[N_BRANCH * parts:]
    j = pl.program_id(1)

    @pl.when(j == 0)
    def _():
        u_ref[...] = _rms(h_ref[...], g_ref[...]).astype(BF16)
        acc_ref[...] = jnp.zeros_like(acc_ref)

    u = u_ref[...]
    merged = None
    for b, (x_ref, w_ref) in enumerate(((os_ref, wos_ref), (or_ref, wor_ref), (om_ref, wom_ref))):
        gate = jnp.concatenate([_dot(u, gate_refs[b * parts + c][...]) for c in range(parts)], axis=1)
        term = jax.nn.sigmoid(gate) * _dot(x_ref[...], w_ref[...])
        merged = term if merged is None else merged + term
    acc_ref[...] += _dot(merged.astype(BF16), wout_ref[...])

    @pl.when(j == pl.num_programs(1) - 1)
    def _():
        o_ref[...] = h_ref[...] + acc_ref[...]


def _merge(h, g, o_swa, o_rw, o_mem, w_in, wo_swa, wo_rw, wo_mem, w_out, *, tm=512, tn=512):
    m, d = h.shape
    nt = d // tn
    parts = tn // GATE_BLOCK
    g0 = PROJ_DIM // GATE_BLOCK
    row = lambda i, j: (i, 0)
    col = lambda i, j: (0, j)
    gate_specs = [pl.BlockSpec((d, GATE_BLOCK), functools.partial(
        lambda i, j, off: (0, off + j * parts), off=g0 + b * (d // GATE_BLOCK) + c))
        for b in range(N_BRANCH) for c in range(parts)]
    return pl.pallas_call(
        functools.partial(_merge_kernel, parts=parts),
        out_shape=jax.ShapeDtypeStruct((m, d), F32),
        grid=(m // tm, nt),
        in_specs=[
            pl.BlockSpec((tm, d), row),
            pl.BlockSpec((1, d), lambda i, j: (0, 0)),
            pl.BlockSpec((tm, SWA_Q_DIM), row),
            pl.BlockSpec((tm, RWKV_DIM), row),
            pl.BlockSpec((tm, MEM_DIM), row),
            pl.BlockSpec((SWA_Q_DIM, tn), col),
            pl.BlockSpec((RWKV_DIM, tn), col),
            pl.BlockSpec((MEM_DIM, tn), col),
            pl.BlockSpec((tn, d), lambda i, j: (j, 0)),
        ] + gate_specs,
        out_specs=pl.BlockSpec((tm, d), row),
        scratch_shapes=[pltpu.VMEM((tm, d), BF16), pltpu.VMEM((tm, d), F32)],
        compiler_params=_cparams("parallel", "arbitrary"),
        name="merge",
    )(h, g, o_swa, o_rw, o_mem, wo_swa, wo_rw, wo_mem, w_out, *([w_in] * (N_BRANCH * parts)))


def _t5_bucket(dist):
    max_exact = N_BUCKETS // 2
    d = np.maximum(dist, 0)
    log_ratio = (np.log(np.maximum(d, 1).astype(np.float32) / np.float32(max_exact))
                 / np.float32(math.log(MAX_DISTANCE / max_exact)))
    large = np.minimum(max_exact + (log_ratio * (N_BUCKETS - max_exact)).astype(np.int32), N_BUCKETS - 1)
    return np.where(d < max_exact, d, large).astype(np.int32)


def _rel_bias(table, dist):
    onehot = np.eye(N_BUCKETS, dtype=np.float32)[_t5_bucket(dist).reshape(-1)]
    bias = jnp.einsum("nb,bh->hn", jnp.asarray(onehot), table, precision=lax.Precision.HIGHEST)
    return bias.reshape(SWA_HEADS, *dist.shape)


def _rwkv_branch(xr, shift0, s0, p, ones):
    b, t, _ = xr.shape
    flat = lambda z: z.reshape(b * t, z.shape[-1])
    if b == 1:
        o, s_new = _rwkv_prompt(flat(xr), shift0, s0[0], p, ones)
        return o, s_new[None]
    assert b == LANES, "short sequences are batched one per lane"
    r, w, k, v, kk, kka, g, bonus = _rwkv_prep(flat(xr), shift0, p, ones, seq=t)
    y, s_new = _rwkv_lanes(r, w, k, v, kk, kka, jnp.transpose(s0, (1, 2, 3, 0)), t=t)
    o = _rwkv_post(flat(jnp.transpose(y, (1, 0, 2))), bonus, g, p["ln_w"], p["ln_b"], ones)
    return o, jnp.transpose(s_new, (3, 0, 1, 2))


def kernel(x_prompt, mem_prompt, x_sample, cache_swa_k, cache_swa_v, state_rwkv, state_rwkv_shift, cache_mem_k, cache_mem_v, ffn1_norm, ffn1_wi, ffn1_wo, mix_norm, w_in, swa_sinks, rel_bias_table, rwkv_mu, rwkv_w0, rwkv_w_w2, rwkv_a0, rwkv_a_w2, rwkv_g_w2, rwkv_k_k, rwkv_k_a, rwkv_r_k, rwkv_ln_w, rwkv_ln_b, mem_norm, w_mem_kv, w_o_swa, w_o_rwkv, w_o_mem, w_out, ffn2_norm, ffn2_wi, ffn2_wo, final_norm):
    assert ffn1_wi.shape[0] == 1, "single-layer trunk"
    bp, tp, d = x_prompt.shape
    bs, ts, _ = x_sample.shape
    assert bp == 1
    row = lambda z: z.reshape(1, -1).astype(F32)

    wo_swa = w_o_swa[0].astype(BF16).reshape(SWA_KV_HEADS, SWA_GROUP, SWA_HEAD_DIM, d).transpose(1, 0, 2, 3).reshape(SWA_Q_DIM, d)
    wo_rw = w_o_rwkv[0].astype(BF16)
    wo_mem = w_o_mem[0].astype(BF16)
    w_out_b = w_out[0].astype(BF16)
    wi1, wo1 = ffn1_wi[0].astype(BF16), ffn1_wo[0].astype(BF16)
    g1, gm, g2, gf = row(ffn1_norm[0]), row(mix_norm[0]), row(ffn2_norm[0]), row(final_norm)
    rp = {
        "mu": row(rwkv_mu[0]), "w0": row(rwkv_w0[0]), "w_w2": rwkv_w_w2[0], "a0": row(rwkv_a0[0]),
        "a_w2": rwkv_a_w2[0], "g_w2": rwkv_g_w2[0], "k_k": row(rwkv_k_k[0]), "k_a": row(rwkv_k_a[0]),
        "r_k": row(rwkv_r_k[0]), "ln_w": row(rwkv_ln_w[0]), "ln_b": row(rwkv_ln_b[0]),
    }
    seg = np.arange(GROUP_W) // RWKV_HEAD_DIM
    ones = jnp.asarray(seg[:, None] == seg[None, :], dtype=BF16)
    sinks = swa_sinks[0].astype(F32)
    table = rel_bias_table.astype(F32)

    xp = x_prompt.reshape(tp, d)
    xs = x_sample.reshape(bs * ts, d)
    hp, wi2, wo2, w_in_b = _ffn(xp, g1, wi1, wo1, gf, final_norm=False,
                                riders=(ffn2_wi[0], ffn2_wo[0], w_in[0]))
    hs = _ffn(xs, g1, wi1, wo1, gf, final_norm=False)
    w_q = w_in_b[:, :SWA_Q_DIM].reshape(d, SWA_KV_HEADS, SWA_GROUP, SWA_HEAD_DIM).transpose(0, 2, 1, 3).reshape(d, SWA_Q_DIM)
    qp, kvp, xrp, qmp = _inproj(hp, gm, w_q, w_in_b)
    qs, kvs, xrs, qms = _inproj(hs, gm, w_q, w_in_b)

    w = WINDOW
    dist_p = np.arange(w)[:, None] + w - np.arange(2 * w)[None, :]
    bias_p = _rel_bias(table, dist_p).reshape(SWA_KV_HEADS, SWA_GROUP, w, 2 * w).transpose(1, 0, 2, 3)
    bias_p = bias_p.reshape(SWA_GROUP, SWA_KV_HEADS * w, 2 * w)
    sink_p = jnp.repeat(sinks.reshape(SWA_KV_HEADS, SWA_GROUP).T, w, axis=1).reshape(SWA_GROUP, SWA_KV_HEADS * w, 1)
    o_swa_p = _swa_prompt(qp, kvp, bias_p, sink_p)

    wbuf = cache_swa_k.shape[2]
    dist_s = np.arange(ts)[:, None] + wbuf - np.arange(wbuf + ts)[None, :]
    bias_s = _rel_bias(table, dist_s).reshape(SWA_HEADS * ts, wbuf + ts)
    sink_rows = jnp.repeat(sinks, ts).reshape(SWA_HEADS * ts, 1)
    qs_gt = qs.reshape(bs, ts, SWA_GROUP, SWA_KV_DIM).transpose(0, 2, 1, 3).reshape(bs, SWA_GROUP * ts, SWA_KV_DIM)
    kbuf = cache_swa_k[0].reshape(bs, wbuf, SWA_KV_DIM).transpose(0, 2, 1)
    vbuf = cache_swa_v[0].reshape(bs, wbuf, SWA_KV_DIM).transpose(0, 2, 1)
    o_swa_s, knew_t, vnew_t = _swa_sample(qs_gt, kvs.reshape(bs, ts, 2 * SWA_KV_DIM), kbuf, vbuf,
                                          bias_s[:, :wbuf], bias_s[:, wbuf:], sink_rows)
    o_swa_s = o_swa_s.reshape(bs, SWA_GROUP, ts, SWA_KV_DIM).transpose(0, 2, 1, 3).reshape(bs * ts, SWA_Q_DIM)

    zero_shift = jnp.zeros((bp, 1, RWKV_IN), F32)
    zero_state = jnp.zeros((bp, RWKV_HEADS, RWKV_HEAD_DIM, RWKV_HEAD_DIM), F32)
    o_rw_p, state_p = _rwkv_branch(xrp.reshape(bp, tp, RWKV_IN), zero_shift, zero_state, rp, ones)
    o_rw_s, state_s = _rwkv_branch(xrs.reshape(bs, ts, RWKV_IN), state_rwkv_shift[0], state_rwkv[0], rp, ones)

    mkv = _norm_matmul(mem_prompt.reshape(N_MEM, d), row(mem_norm[0]), w_mem_kv[0].astype(BF16))
    o_mem_p = _mem_prompt(qmp, mkv)
    o_mem_s = _mem_sample(qms.reshape(bs, ts, MEM_DIM), cache_mem_k[0].reshape(bs, N_MEM * MEM_HEADS, MEM_HEAD_DIM),
                          cache_mem_v[0].reshape(bs, N_MEM * MEM_HEADS, MEM_HEAD_DIM)).reshape(bs * ts, MEM_DIM)

    hp = _merge(hp, gm, o_swa_p, o_rw_p, o_mem_p, w_in_b, wo_swa, wo_rw, wo_mem, w_out_b)
    hs = _merge(hs, gm, o_swa_s, o_rw_s, o_mem_s, w_in_b, wo_swa, wo_rw, wo_mem, w_out_b)
    y_prompt = _ffn(hp, g2, wi2, wo2, gf, final_norm=True).reshape(bp, tp, d)
    y_sample = _ffn(hs, g2, wi2, wo2, gf, final_norm=True).reshape(bs, ts, d)

    wp = min(w, tp)
    p_k = kvp[tp - wp:, :SWA_KV_DIM].reshape(1, bp, wp, SWA_KV_HEADS, SWA_HEAD_DIM)
    p_v = kvp[tp - wp:, SWA_KV_DIM:].reshape(1, bp, wp, SWA_KV_HEADS, SWA_HEAD_DIM)
    p_mk = mkv[:, :MEM_DIM].reshape(1, bp, N_MEM, MEM_HEADS, MEM_HEAD_DIM)
    p_mv = mkv[:, MEM_DIM:].reshape(1, bp, N_MEM, MEM_HEADS, MEM_HEAD_DIM)
    s_k = knew_t.transpose(0, 2, 1).reshape(1, bs, wbuf, SWA_KV_HEADS, SWA_HEAD_DIM)
    s_v = vnew_t.transpose(0, 2, 1).reshape(1, bs, wbuf, SWA_KV_HEADS, SWA_HEAD_DIM)
    return (y_prompt, y_sample,
            p_k, p_v, state_p[None], xrp[tp - 1:].reshape(1, bp, 1, RWKV_IN), p_mk, p_mv,
            s_k, s_v, state_s[None], xrs.reshape(bs, ts, RWKV_IN)[:, ts - 1:][None])
```

```python
import functools
import math

import jax
import jax.numpy as jnp
import numpy as np
from jax import lax
from jax.experimental import pallas as pl
from jax.experimental.pallas import tpu as pltpu

F32 = jnp.float32
BF16 = jnp.bfloat16

D_MODEL = 2048
D_FF = 5632
SWA_HEADS = 16
SWA_KV_HEADS = 4
SWA_GROUP = SWA_HEADS // SWA_KV_HEADS
SWA_HEAD_DIM = 64
SWA_Q_DIM = SWA_HEADS * SWA_HEAD_DIM
SWA_KV_DIM = SWA_KV_HEADS * SWA_HEAD_DIM
WINDOW = 128
N_BUCKETS = 32
MAX_DISTANCE = 128
RWKV_HEADS = 8
RWKV_HEAD_DIM = 64
RWKV_DIM = RWKV_HEADS * RWKV_HEAD_DIM
LORA_W = 64
LORA_A = 64
LORA_G = 128
RWKV_IN = 3 * RWKV_DIM + LORA_W + LORA_A + LORA_G
N_MEM = 256
MEM_HEADS = 4
MEM_HEAD_DIM = 128
MEM_DIM = MEM_HEADS * MEM_HEAD_DIM
N_BRANCH = 3
PROJ_DIM = SWA_Q_DIM + 2 * SWA_KV_DIM + RWKV_IN + MEM_DIM
NORM_EPS = 1e-6
GN_EPS = 64e-5
NEG_INF = -1e30

LANES = 128
SUBLANES = 8
VMEM_LIMIT = 56 * 1024 * 1024


def _cparams(*sem):
    return pltpu.CompilerParams(dimension_semantics=sem, vmem_limit_bytes=VMEM_LIMIT)


def _rms(x, g):
    return x * lax.rsqrt(jnp.mean(x * x, axis=-1, keepdims=True) + NORM_EPS) * g


def _dot(a, b):
    return jnp.dot(a, b, preferred_element_type=F32)


def _dot_nt(a, b):
    return lax.dot_general(a, b, (((1,), (1,)), ((), ())), preferred_element_type=F32)


def _dot_hi(a, b):
    return jnp.dot(a, b, preferred_element_type=F32, precision=lax.Precision.HIGHEST)


def _ffn_kernel(x_ref, g_ref, wg_ref, wu_ref, wo_ref, gf_ref, *rest, final_norm, n_riders, emit_weights):
    rider_in, (o_ref, *extra_out), (xn_ref, acc_ref) = rest[:n_riders], rest[n_riders:-2], rest[-2:]
    for src, dst in zip(rider_in, extra_out[:n_riders]):
        dst[...] = src[...].astype(BF16)
    j = pl.program_id(1)

    @pl.when(j == 0)
    def _():
        xn_ref[...] = _rms(x_ref[...], g_ref[...]).astype(BF16)
        acc_ref[...] = jnp.zeros_like(acc_ref)

    wg, wu, wo = wg_ref[...], wu_ref[...], wo_ref[...]
    if emit_weights:
        wg, wu, wo = wg.astype(BF16), wu.astype(BF16), wo.astype(BF16)
        for dst, val in zip(extra_out[n_riders:], (wg, wu, wo)):
            dst[...] = val
    xn = xn_ref[...]
    gate = _dot(xn, wg)
    up = _dot(xn, wu)
    act = (gate * jax.nn.sigmoid(gate)) * up
    acc_ref[...] += _dot(act.astype(BF16), wo)

    @pl.when(j == pl.num_programs(1) - 1)
    def _():
        h = x_ref[...] + 0.5 * acc_ref[...]
        if final_norm:
            h = _rms(h, gf_ref[...])
        o_ref[...] = h


def _rider_tiling(shape, steps):
    rows, cols = shape
    best, best_score = (1, 1), (0, 0)
    for nr in range(1, rows // 16 + 1):
        if rows % nr or (rows // nr) % 16:
            continue
        for nc in range(1, cols // LANES + 1):
            if cols % nc or (cols // nc) % LANES or nr * nc > steps:
                continue
            score = (min(cols // nc, 1024), nr * nc)
            if score > best_score:
                best, best_score = (nr, nc), score
    return best


def _ffn(x, g, wi, wo, gf, *, final_norm, riders=(), emit_weights=False, tm=512, tf=512):
    m, d = x.shape
    dff = wo.shape[0]
    nf = dff // tf
    steps = (m // tm) * nf
    wg, wu = wi if isinstance(wi, tuple) else (wi, wi)
    up_off = 0 if isinstance(wi, tuple) else nf
    assert not emit_weights or m == tm
    weight_specs = [pl.BlockSpec((d, tf), lambda i, j: (0, j)), pl.BlockSpec((d, tf), lambda i, j: (0, j)),
                    pl.BlockSpec((tf, d), lambda i, j: (j, 0))]
    weight_out = [jax.ShapeDtypeStruct((d, dff), BF16)] * 2 + [jax.ShapeDtypeStruct((dff, d), BF16)]
    rider_specs = []
    for arr in riders:
        nr, nc = _rider_tiling(arr.shape, steps)
        br, bc = arr.shape[0] // nr, arr.shape[1] // nc
        rider_specs.append(pl.BlockSpec((br, bc), functools.partial(
            lambda i, j, nc, last: (jnp.minimum(i * nf + j, last) // nc, jnp.minimum(i * nf + j, last) % nc),
            nc=nc, last=nr * nc - 1)))
    out = pl.pallas_call(
        functools.partial(_ffn_kernel, final_norm=final_norm, n_riders=len(riders), emit_weights=emit_weights),
        out_shape=([jax.ShapeDtypeStruct((m, d), F32)] + [jax.ShapeDtypeStruct(arr.shape, BF16) for arr in riders]
                   + (weight_out if emit_weights else [])),
        grid=(m // tm, nf),
        in_specs=[
            pl.BlockSpec((tm, d), lambda i, j: (i, 0)),
            pl.BlockSpec((1, d), lambda i, j: (0, 0)),
            pl.BlockSpec((d, tf), lambda i, j: (0, j)),
            pl.BlockSpec((d, tf), lambda i, j: (0, j + up_off)),
            pl.BlockSpec((tf, d), lambda i, j: (j, 0)),
            pl.BlockSpec((1, d), lambda i, j: (0, 0)),
        ] + rider_specs,
        out_specs=([pl.BlockSpec((tm, d), lambda i, j: (i, 0))] + rider_specs
                   + (weight_specs if emit_weights else [])),
        scratch_shapes=[pltpu.VMEM((tm, d), BF16), pltpu.VMEM((tm, d), F32)],
        compiler_params=_cparams("arbitrary", "arbitrary"),
        name="ffn_final" if final_norm else "ffn",
    )(x, g, wg, wu, wo, gf, *riders)
    return out if (riders or emit_weights) else out[0]


def _inproj_kernel(h_ref, g_ref, wq_ref, w_ref, q_ref, kv_ref, xr_ref, qm_ref):
    u = _rms(h_ref[...], g_ref[...]).astype(BF16)
    c0, c1, c2 = SWA_Q_DIM, SWA_Q_DIM + 2 * SWA_KV_DIM, SWA_Q_DIM + 2 * SWA_KV_DIM + RWKV_IN
    q_ref[...] = _dot(u, wq_ref[...]).astype(BF16)
    kv_ref[...] = _dot(u, w_ref[:, c0:c1])
    xr_ref[...] = _dot(u, w_ref[:, c1:c2])
    qm_ref[...] = _dot(u, w_ref[:, c2:PROJ_DIM]).astype(BF16)


def _inproj(h, g, wq, w, *, tm=256):
    m, d = h.shape
    row = lambda i: (i, 0)
    return pl.pallas_call(
        _inproj_kernel,
        out_shape=(
            jax.ShapeDtypeStruct((m, SWA_Q_DIM), BF16),
            jax.ShapeDtypeStruct((m, 2 * SWA_KV_DIM), F32),
            jax.ShapeDtypeStruct((m, RWKV_IN), F32),
            jax.ShapeDtypeStruct((m, MEM_DIM), BF16),
        ),
        grid=(m // tm,),
        in_specs=[
            pl.BlockSpec((tm, d), row),
            pl.BlockSpec((1, d), lambda i: (0, 0)),
            pl.BlockSpec((d, SWA_Q_DIM), lambda i: (0, 0), pipeline_mode=pl.Buffered(1)),
            pl.BlockSpec((d, PROJ_DIM), lambda i: (0, 0), pipeline_mode=pl.Buffered(1)),
        ],
        out_specs=(
            pl.BlockSpec((tm, SWA_Q_DIM), row),
            pl.BlockSpec((tm, 2 * SWA_KV_DIM), row),
            pl.BlockSpec((tm, RWKV_IN), row),
            pl.BlockSpec((tm, MEM_DIM), row),
        ),
        compiler_params=_cparams("parallel"),
        name="inproj",
    )(h, g, wq, w)


def _norm_matmul_kernel(x_ref, g_ref, w_ref, o_ref):
    o_ref[...] = _dot(_rms(x_ref[...], g_ref[...]).astype(BF16), w_ref[...])


def _norm_matmul(x, g, w, *, tn=512):
    m, d = x.shape
    n = w.shape[1]
    return pl.pallas_call(
        _norm_matmul_kernel,
        out_shape=jax.ShapeDtypeStruct((m, n), F32),
        grid=(n // tn,),
        in_specs=[
            pl.BlockSpec((m, d), lambda j: (0, 0)),
            pl.BlockSpec((1, d), lambda j: (0, 0)),
            pl.BlockSpec((d, tn), lambda j: (0, j)),
        ],
        out_specs=pl.BlockSpec((m, tn), lambda j: (0, j)),
        compiler_params=_cparams("parallel"),
        name="norm_matmul",
    )(x, g, w)


def _sink_softmax(logits, sink):
    m = jnp.maximum(jnp.max(logits, axis=-1, keepdims=True), sink)
    p = jnp.exp(logits - m)
    denom = jnp.sum(p, axis=-1, keepdims=True) + jnp.exp(sink - m)
    return p * (1.0 / denom)


def _swa_prompt_kernel(q_ref, kvc_ref, kvp_ref, bias_ref, sink_ref, o_ref, *, nq):
    i = pl.program_id(0)
    w = WINDOW
    rows = SWA_KV_HEADS * w
    kv_blocks = [kvp_ref[...]] + [kvc_ref[s * w:(s + 1) * w, :] for s in range(nq)]
    k_blocks = [x[:, 0:SWA_KV_DIM].astype(BF16) for x in kv_blocks]
    v_blocks = [x[:, SWA_KV_DIM:].astype(BF16) for x in kv_blocks]
    qpos = lax.broadcasted_iota(jnp.int32, (rows, 2 * w), 0) % w
    col = lax.broadcasted_iota(jnp.int32, (rows, 2 * w), 1)
    dist = qpos + w - col
    in_window = (dist >= 0) & (dist < w)
    first = in_window & ((col >= w) | (i > 0))
    lane_head = lax.broadcasted_iota(jnp.int32, (w, SWA_KV_DIM), 1) // SWA_HEAD_DIM
    scale = SWA_HEAD_DIM ** -0.5
    pairs = [(s, g) for s in range(nq) for g in range(SWA_GROUP)]
    k = [jnp.concatenate(k_blocks[s:s + 2], axis=0) for s in range(nq)]
    v = [jnp.concatenate(v_blocks[s:s + 2], axis=0) for s in range(nq)]
    logits = []
    for s, g in pairs:
        qg = q_ref[s * w:(s + 1) * w, g * SWA_KV_DIM:(g + 1) * SWA_KV_DIM].astype(F32) * scale
        qs = jnp.concatenate([jnp.where(lane_head == kvh, qg, 0.0) for kvh in range(SWA_KV_HEADS)], axis=0)
        lg = _dot_nt(qs.astype(BF16), k[s])
        logits.append(jnp.where(first if s == 0 else in_window, lg + bias_ref[g], NEG_INF))
    sink = [sink_ref[g] for _, g in pairs]
    m = [jnp.maximum(jnp.max(x, axis=-1, keepdims=True), sk) for x, sk in zip(logits, sink)]
    p = [jnp.exp(x - mi) for x, mi in zip(logits, m)]
    inv = [1.0 / (jnp.sum(pi, axis=-1, keepdims=True) + jnp.exp(sk - mi)) for pi, sk, mi in zip(p, sink, m)]
    ov = [_dot((pi * ii).astype(BF16), v[s]) for (s, _), pi, ii in zip(pairs, p, inv)]
    for (s, g), o in zip(pairs, ov):
        og = jnp.zeros((w, SWA_KV_DIM), F32)
        for kvh in range(SWA_KV_HEADS):
            og = jnp.where(lane_head == kvh, o[kvh * w:(kvh + 1) * w], og)
        o_ref[s * w:(s + 1) * w, g * SWA_KV_DIM:(g + 1) * SWA_KV_DIM] = og.astype(BF16)


def _swa_prompt(q, kv, bias, sink_rows, *, nq=2):
    t = q.shape[0]
    w = WINDOW
    rows = SWA_KV_HEADS * w
    return pl.pallas_call(
        functools.partial(_swa_prompt_kernel, nq=nq),
        out_shape=jax.ShapeDtypeStruct((t, SWA_Q_DIM), BF16),
        grid=(t // (nq * w),),
        in_specs=[
            pl.BlockSpec((nq * w, SWA_Q_DIM), lambda i: (i, 0)),
            pl.BlockSpec((nq * w, 2 * SWA_KV_DIM), lambda i: (i, 0)),
            pl.BlockSpec((w, 2 * SWA_KV_DIM), lambda i: (jnp.maximum(i * nq - 1, 0), 0)),
            pl.BlockSpec((SWA_GROUP, rows, 2 * w), lambda i: (0, 0, 0)),
            pl.BlockSpec((SWA_GROUP, rows, 1), lambda i: (0, 0, 0)),
        ],
        out_specs=pl.BlockSpec((nq * w, SWA_Q_DIM), lambda i: (i, 0)),
        compiler_params=_cparams("parallel"),
        name="swa_prompt",
    )(q, kv, kv, bias, sink_rows)


def _swa_sample_kernel(q_ref, kvn_ref, kb_ref, vb_ref, bias_b_ref, bias_n_ref, sink_ref, o_ref, ko_ref, vo_ref,
                       *, bb, t):
    gt = SWA_GROUP * t
    rows = SWA_KV_HEADS * gt
    w = kb_ref.shape[2]
    scale = SWA_HEAD_DIM ** -0.5
    lane_head = lax.broadcasted_iota(jnp.int32, (gt, SWA_KV_DIM), 1) // SWA_HEAD_DIM
    tok = lax.broadcasted_iota(jnp.int32, (rows, w), 0) % t
    keyj = lax.broadcasted_iota(jnp.int32, (rows, w), 1)
    valid_b = (tok + w - keyj) < WINDOW
    tok_n = lax.broadcasted_iota(jnp.int32, (rows, 1), 0) % t
    sink = sink_ref[...]
    bs = range(bb)
    toks = range(t)
    qall = [jnp.concatenate([jnp.where(lane_head == kvh, q_ref[b].astype(F32), 0.0) for kvh in range(SWA_KV_HEADS)],
                            axis=0) for b in bs]
    kvn = [kvn_ref[b] for b in bs]
    lb = [_dot(qall[b].astype(BF16), kb_ref[b].astype(BF16)) for b in bs]
    lb = [jnp.where(valid_b, lb[b] * scale + bias_b_ref[...], NEG_INF) for b in bs]
    ln = [[jnp.sum(qall[b] * kvn[b][j:j + 1, 0:SWA_KV_DIM], axis=-1, keepdims=True) for j in toks] for b in bs]
    ln = [[jnp.where(tok_n >= j, ln[b][j] * scale + bias_n_ref[:, j:j + 1], NEG_INF) for j in toks] for b in bs]
    m = [jnp.maximum(jnp.max(lb[b], axis=-1, keepdims=True), sink) for b in bs]
    m = [functools.reduce(jnp.maximum, ln[b], m[b]) for b in bs]
    pb = [jnp.exp(lb[b] - m[b]) for b in bs]
    pn = [[jnp.exp(ln[b][j] - m[b]) for j in toks] for b in bs]
    denom = [jnp.sum(pb[b], axis=-1, keepdims=True) + jnp.exp(sink - m[b]) for b in bs]
    inv = [1.0 / functools.reduce(jnp.add, pn[b], denom[b]) for b in bs]
    oall = [_dot_nt((pb[b] * inv[b]).astype(BF16), vb_ref[b].astype(BF16)) for b in bs]
    for b in bs:
        ob = oall[b]
        for j in toks:
            ob = ob + (pn[b][j] * inv[b]) * kvn[b][j:j + 1, SWA_KV_DIM:]
        og = jnp.zeros((gt, SWA_KV_DIM), F32)
        for kvh in range(SWA_KV_HEADS):
            og = jnp.where(lane_head == kvh, ob[kvh * gt:(kvh + 1) * gt], og)
        o_ref[b] = og.astype(BF16)
    pos = lax.broadcasted_iota(jnp.int32, (SWA_KV_DIM, w), 1)
    pad = jnp.zeros((w - SUBLANES, 2 * SWA_KV_DIM), F32)
    row8 = lax.broadcasted_iota(jnp.int32, (SUBLANES, 2 * SWA_KV_DIM), 0)
    for b in bs:
        last8 = jnp.zeros((SUBLANES, 2 * SWA_KV_DIM), F32)
        for j in toks:
            last8 = jnp.where(row8 == SUBLANES - t + j, kvn[b][j:j + 1], last8)
        tail_t = jnp.concatenate([pad, last8], axis=0).T
        ko_ref[b] = jnp.where(pos >= w - t, tail_t[:SWA_KV_DIM], pltpu.roll(kb_ref[b], w - t, axis=1))
        vo_ref[b] = jnp.where(pos >= w - t, tail_t[SWA_KV_DIM:], pltpu.roll(vb_ref[b], w - t, axis=1))


def _swa_sample(q, kvn, kbuf, vbuf, bias_b, bias_n, sink_rows, *, bb=8):
    b, gt, _ = q.shape
    t = kvn.shape[1]
    w = kbuf.shape[2]
    rows = SWA_KV_HEADS * gt
    blk = lambda i: (i, 0, 0)
    const = lambda i: (0, 0)
    cache = pl.BlockSpec((bb, SWA_KV_DIM, w), blk)
    return pl.pallas_call(
        functools.partial(_swa_sample_kernel, bb=bb, t=t),
        out_shape=(jax.ShapeDtypeStruct((b, gt, SWA_KV_DIM), BF16),
                   jax.ShapeDtypeStruct(kbuf.shape, F32), jax.ShapeDtypeStruct(vbuf.shape, F32)),
        grid=(b // bb,),
        in_specs=[
            pl.BlockSpec((bb, gt, SWA_KV_DIM), blk),
            pl.BlockSpec((bb, t, 2 * SWA_KV_DIM), blk),
            cache, cache,
            pl.BlockSpec((rows, w), const),
            pl.BlockSpec((rows, t), const),
            pl.BlockSpec((rows, 1), const),
        ],
        out_specs=(pl.BlockSpec((bb, gt, SWA_KV_DIM), blk), cache, cache),
        compiler_params=_cparams("parallel"),
        name="swa_sample",
    )(q, kvn, kbuf, vbuf, bias_b, bias_n, sink_rows)


def _softmax_rows(x):
    m = jnp.max(x, axis=-1, keepdims=True)
    p = jnp.exp(x - m)
    return p * (1.0 / jnp.sum(p, axis=-1, keepdims=True))


def _mem_heads(q, mk, mv):
    scale = MEM_HEAD_DIM ** -0.5
    outs = []
    for h in range(MEM_HEADS):
        sl = slice(h * MEM_HEAD_DIM, (h + 1) * MEM_HEAD_DIM)
        p = _softmax_rows(_dot_nt(q[:, sl], mk[:, sl]) * scale)
        outs.append(_dot(p.astype(BF16), mv[:, sl]))
    return jnp.concatenate(outs, axis=-1)


def _mem_prompt_kernel(q_ref, mk_ref, mv_ref, o_ref):
    o_ref[...] = _mem_heads(q_ref[...], mk_ref[...].astype(BF16), mv_ref[...].astype(BF16)).astype(BF16)


def _mem_prompt(q, mkv, *, tm=512):
    m = q.shape[0]
    return pl.pallas_call(
        _mem_prompt_kernel,
        out_shape=jax.ShapeDtypeStruct((m, MEM_DIM), BF16),
        grid=(m // tm,),
        in_specs=[
            pl.BlockSpec((tm, MEM_DIM), lambda i: (i, 0)),
            pl.BlockSpec((N_MEM, MEM_DIM), lambda i: (0, 0)),
            pl.BlockSpec((N_MEM, MEM_DIM), lambda i: (0, 1)),
        ],
        out_specs=pl.BlockSpec((tm, MEM_DIM), lambda i: (i, 0)),
        compiler_params=_cparams("parallel"),
        name="mem_prompt",
    )(q, mkv, mkv)


def _mem_sample_kernel(q_ref, mk_ref, mv_ref, o_ref, *, bb):
    scale = MEM_HEAD_DIM ** -0.5
    pairs = [(b, h) for b in range(bb) for h in range(MEM_HEADS)]
    rows = lambda h: pl.ds(h, N_MEM, stride=MEM_HEADS)
    cols = lambda h: slice(h * MEM_HEAD_DIM, (h + 1) * MEM_HEAD_DIM)
    q = [q_ref[b] for b in range(bb)]
    x = [_dot_nt(q[b][:, cols(h)], mk_ref[b, rows(h), :].astype(BF16)) * scale for b, h in pairs]
    m = [jnp.max(xi, axis=-1, keepdims=True) for xi in x]
    p = [jnp.exp(xi - mi) for xi, mi in zip(x, m)]
    inv = [1.0 / jnp.sum(pi, axis=-1, keepdims=True) for pi in p]
    o = [_dot((pi * ii).astype(BF16), mv_ref[b, rows(h), :].astype(BF16)) for (b, h), pi, ii in zip(pairs, p, inv)]
    for b in range(bb):
        o_ref[b] = jnp.concatenate(o[b * MEM_HEADS:(b + 1) * MEM_HEADS], axis=-1).astype(BF16)


def _mem_sample(q, mk, mv, *, bb=8):
    b, t, _ = q.shape
    blk = lambda i: (i, 0, 0)
    return pl.pallas_call(
        functools.partial(_mem_sample_kernel, bb=bb),
        out_shape=jax.ShapeDtypeStruct((b, t, MEM_DIM), BF16),
        grid=(b // bb,),
        in_specs=[
            pl.BlockSpec((bb, t, MEM_DIM), blk),
            pl.BlockSpec((bb, N_MEM * MEM_HEADS, MEM_HEAD_DIM), blk),
            pl.BlockSpec((bb, N_MEM * MEM_HEADS, MEM_HEAD_DIM), blk),
        ],
        out_specs=pl.BlockSpec((bb, t, MEM_DIM), blk),
        compiler_params=_cparams("parallel"),
        name="mem_sample",
    )(q, mk, mv)


def _head_sum(x, ones_ref):
    hi = x.astype(BF16)
    lo = (x - hi.astype(F32)).astype(BF16)
    ones = ones_ref[...]
    w = ones.shape[0]
    return jnp.concatenate([_dot(hi[:, c:c + w], ones) + _dot(lo[:, c:c + w], ones) for c in range(0, x.shape[1], w)],
                           axis=1)


def _token_shift(x_ref, pre_ref, start_ref, mu_ref, *, seq, tm, tile):
    x = x_ref[...]
    row = lax.broadcasted_iota(jnp.int32, x.shape, 0)
    shifted = pltpu.roll(x, 1, axis=0)
    if seq >= tm:
        is_start = (tile * tm) % seq == 0
        first_prev = jnp.where(is_start, start_ref[0], pre_ref[SUBLANES - 1:SUBLANES, :])
        prev = jnp.where(row == 0, first_prev, shifted)
    else:
        prev = jnp.where(row % seq == 0, start_ref[...], shifted)
    return x + mu_ref[...] * (prev - x)


def _rwkv_features(xs, w0_ref, ww2_ref, a0_ref, aw2_ref, gw2_ref, kk_ref, ka_ref, rk_ref, ones_ref):
    d = RWKV_DIM
    r = xs[:, 0:d]
    k = xs[:, d:2 * d]
    v = xs[:, 2 * d:3 * d]
    lw = xs[:, 3 * d:3 * d + LORA_W]
    la = xs[:, 3 * d + LORA_W:3 * d + LORA_W + LORA_A]
    lg = xs[:, 3 * d + LORA_W + LORA_A:]
    wpre = w0_ref[...] + _dot_hi(jnp.tanh(lw), ww2_ref[...])
    w = -jax.nn.softplus(-wpre) - 0.5
    log_decay = -jnp.exp(w)
    a = jax.nn.sigmoid(a0_ref[...] + _dot_hi(la, aw2_ref[...]))
    g = _dot_hi(jax.nn.sigmoid(lg), gw2_ref[...])
    kk = k * kk_ref[...]
    kk = kk / jnp.maximum(jnp.sqrt(_head_sum(kk * kk, ones_ref)), 1e-12)
    kh = k * (1.0 + (a - 1.0) * ka_ref[...])
    bonus = _head_sum(r * kh * rk_ref[...], ones_ref) * v
    return r, log_decay, kh, v, kk, kk * a, g, bonus


def _rwkv_prep_kernel(x_ref, pre_ref, start_ref, mu_ref, w0_ref, ww2_ref, a0_ref, aw2_ref, gw2_ref, kk_ref, ka_ref,
                      rk_ref, ones_ref, *outs, seq, tm):
    xs = _token_shift(x_ref, pre_ref, start_ref, mu_ref, seq=seq, tm=tm, tile=pl.program_id(0))
    feats = _rwkv_features(xs, w0_ref, ww2_ref, a0_ref, aw2_ref, gw2_ref, kk_ref, ka_ref, rk_ref, ones_ref)
    for o_ref, val in zip(outs, feats):
        o_ref[...] = val


def _rwkv_prep(x, shift0, p, ones, *, seq, tm=256):
    m = x.shape[0]
    row = lambda i: (i, 0)
    const = lambda i: (0, 0)
    vec = lambda n: pl.BlockSpec((1, n), const)
    out = jax.ShapeDtypeStruct((m, RWKV_DIM), F32)
    if seq >= tm:
        assert seq % tm == 0
        start = shift0
        start_spec = pl.BlockSpec((1, 1, RWKV_IN), lambda i: ((i * tm) // seq, 0, 0))
    else:
        assert tm % seq == 0
        start = jnp.repeat(shift0[:, 0], seq, axis=0)
        start_spec = pl.BlockSpec((tm, RWKV_IN), row)
    pre_blocks = tm // SUBLANES
    return pl.pallas_call(
        functools.partial(_rwkv_prep_kernel, seq=seq, tm=tm),
        out_shape=(out,) * 8,
        grid=(m // tm,),
        in_specs=[
            pl.BlockSpec((tm, RWKV_IN), row),
            pl.BlockSpec((SUBLANES, RWKV_IN), lambda i: (jnp.maximum(i * pre_blocks - 1, 0), 0)),
            start_spec,
            vec(RWKV_IN), vec(RWKV_DIM),
            pl.BlockSpec((LORA_W, RWKV_DIM), const),
            vec(RWKV_DIM),
            pl.BlockSpec((LORA_A, RWKV_DIM), const),
            pl.BlockSpec((LORA_G, RWKV_DIM), const),
            vec(RWKV_DIM), vec(RWKV_DIM), vec(RWKV_DIM),
            pl.BlockSpec(ones.shape, const),
        ],
        out_specs=(pl.BlockSpec((tm, RWKV_DIM), row),) * 8,
        compiler_params=_cparams("parallel"),
        name="rwkv_prep",
    )(x, x, start, p["mu"], p["w0"], p["w_w2"], p["a0"], p["a_w2"], p["g_w2"], p["k_k"], p["k_a"], p["r_k"], ones)


def _rwkv_lanes_kernel(r_ref, lw_ref, k_ref, v_ref, kk_ref, kka_ref, s_ref, y_ref, so_ref, v_scr, y_scr, *, t):
    n = RWKV_HEAD_DIM
    nb = s_ref.shape[-1]
    heads = range(2)

    def token_major(ref, j):
        return ref[pl.ds(j, nb, stride=t), :].T

    for j in range(t):
        v_scr[...] = token_major(v_ref, j)
        r_t, k_t, kk_t, kka_t = (token_major(ref, j) for ref in (r_ref, k_ref, kk_ref, kka_ref))
        w_t = jnp.exp(token_major(lw_ref, j))
        src = s_ref if j == 0 else so_ref

        def value_group(g, carry):
            rows = pl.multiple_of(g * SUBLANES, SUBLANES)
            ys = [[] for _ in heads]
            vg = [v_scr[pl.ds(h * n + rows, SUBLANES), :] for h in heads]
            for i in range(SUBLANES):
                for h in heads:
                    f = slice(h * n, (h + 1) * n)
                    s = src[h, rows + i]
                    sa = jnp.sum(s * (-kk_t[f]), axis=0, keepdims=True)
                    s = s * w_t[f] + sa * kka_t[f] + vg[h][i:i + 1] * k_t[f]
                    so_ref[h, rows + i] = s
                    ys[h].append(jnp.sum(s * r_t[f], axis=0, keepdims=True))
            for h in heads:
                y_scr[pl.ds(h * n + rows, SUBLANES), :] = jnp.concatenate(ys[h], axis=0)
            return carry

        lax.fori_loop(0, n // SUBLANES, value_group, 0)
        y_ref[j] = y_scr[...].T


def _rwkv_lanes(r, lw, k, v, kk, kka, s0, *, t):
    m, d = r.shape
    nb = m // t
    n = RWKV_HEAD_DIM
    assert nb == LANES, "one batch per lane"
    tok = pl.BlockSpec((m, LANES), lambda p: (0, p))
    st = pl.BlockSpec((2, n, n, nb), lambda p: (p, 0, 0, 0))
    return pl.pallas_call(
        functools.partial(_rwkv_lanes_kernel, t=t),
        out_shape=(jax.ShapeDtypeStruct((t, nb, d), F32), jax.ShapeDtypeStruct((RWKV_HEADS, n, n, nb), F32)),
        grid=(RWKV_HEADS // 2,),
        in_specs=[tok] * 6 + [st],
        out_specs=(pl.BlockSpec((t, nb, LANES), lambda p: (0, 0, p)), st),
        scratch_shapes=[pltpu.VMEM((LANES, nb), F32), pltpu.VMEM((LANES, nb), F32)],
        compiler_params=_cparams("parallel"),
        name="rwkv_lanes",
    )(r, lw, k, v, kk, kka, s0)


CHUNK = 64
GROUP_HEADS = 4
GROUP_W = GROUP_HEADS * RWKV_HEAD_DIM
N_GROUPS = RWKV_HEADS // GROUP_HEADS
(MASK_SAME, MASK_STRICT, MASK_INCL, MASK_LEVEL0) = (0, 1, 2, 3)
N_LEVELS = int(math.log2(CHUNK))


def _chunk_masks():
    i = np.arange(GROUP_W)
    same = (i[:, None] // CHUNK) == (i[None, :] // CHUNK)
    masks = [same, same & (i[None, :] < i[:, None]), same & (i[None, :] <= i[:, None])]
    for lvl in range(N_LEVELS):
        m = 1 << lvl
        masks.append(((i[:, None] // (2 * m)) == (i[None, :] // (2 * m))) & ((i[:, None] // m) != (i[None, :] // m))
                     & (i[None, :] < i[:, None]))
    return np.stack(masks).astype(np.float32)


def _rwkv_prompt_kernel(x_ref, pre_ref, start_ref, mu_ref, w0_ref, ww2_ref, a0_ref, aw2_ref, gw2_ref, kk_p_ref,
                        ka_ref, rk_ref, ones_ref, lnw_ref, lnb_ref, st0_ref, tri_ref, eye_ref, mask_ref,
                        o_ref, sto_ref, st_scr, xs_ref, y_ref, r_set, lw_set, k_set, v_set, kk_set, kka_set, g_set,
                        bonus_set, *, seq, tm, n_tiles):
    step = pl.program_id(0)
    cur = (step + 1) % 2
    nxt = step % 2
    sets = (r_set, lw_set, k_set, v_set, kk_set, kka_set, g_set, bonus_set)
    r_ref, lw_ref, k_ref, v_ref, kk_ref, kka_ref = (s.at[cur] for s in sets[:6])

    @pl.when(step == 0)
    def _():
        st_scr[...] = st0_ref[...]
        for s in sets:
            s[1] = jnp.zeros(s.shape[1:], F32)

    xs_ref[...] = _token_shift(x_ref, pre_ref, start_ref, mu_ref, seq=seq, tm=tm, tile=jnp.minimum(step, n_tiles - 1))
    n_chunks = tm // CHUNK

    piece = 2 * CHUNK
    def features(c):
        rows = slice(c * piece, (c + 1) * piece)
        feats = _rwkv_features(xs_ref[rows, :], w0_ref, ww2_ref, a0_ref, aw2_ref, gw2_ref, kk_p_ref, ka_ref, rk_ref,
                               ones_ref)
        for s, val in zip(sets, feats):
            s[nxt, rows, :] = val

    eye = eye_ref[...]
    tri = tri_ref[...]
    tile_rows = lambda x: jnp.concatenate([x] * GROUP_HEADS, axis=0)
    block_diag = lambda x: (tile_rows(x) * mask_ref[MASK_SAME]).astype(BF16)

    chains = [(slice(c * CHUNK, (c + 1) * CHUNK), slice(g * GROUP_W, (g + 1) * GROUP_W))
              for c in range(n_chunks) for g in range(N_GROUPS)]
    each = lambda f, *cols: [f(*args) for args in zip(*cols)]
    same, strict, incl = mask_ref[MASK_SAME], mask_ref[MASK_STRICT], mask_ref[MASK_INCL]

    def cum_decay(lw):
        h1 = lw.astype(BF16)
        r1 = lw - h1.astype(F32)
        h2 = r1.astype(BF16)
        h3 = (r1 - h2.astype(F32)).astype(BF16)
        return _dot(tri, h1) + _dot(tri, h2) + _dot(tri, h3)

    lw = [lw_ref[rows, sl] for rows, sl in chains]
    kka = [kka_ref[rows, sl] for rows, sl in chains]
    k = [k_ref[rows, sl] for rows, sl in chains]
    cum = each(cum_decay, lw)
    cum_last = each(lambda c: c[CHUNK - 1:CHUNK, :], cum)
    p_inv = each(lambda c: jnp.exp(-c), cum)
    p_tail = each(lambda c, cl: jnp.exp(cl - c), cum, cum_last)
    a_bd = [block_diag(-kk_ref[rows, sl] * jnp.exp(c - l)) for (rows, sl), c, l in zip(chains, cum, lw)]
    r_f = [tile_rows(r_ref[rows, sl] * jnp.exp(c)) * same for (rows, sl), c in zip(chains, cum)]
    r_bd = each(lambda x: x.astype(BF16), r_f)
    v_bd = [block_diag(v_ref[rows, sl]) for rows, sl in chains]
    b_rep = each(lambda x, p: tile_rows((x * p).astype(BF16)), kka, p_inv)
    k_rep = each(lambda x, p: tile_rows((x * p).astype(BF16)), k, p_inv)
    bh_rep = each(lambda x, p: tile_rows(x * p), kka, p_tail)
    kh_rep = each(lambda x, p: tile_rows(x * p), k, p_tail)

    l_ab_f = each(lambda a, b: _dot_nt(a, b) * strict, a_bd, b_rep)
    l_ab = each(lambda x: x.astype(BF16), l_ab_f)
    l_ak = each(lambda a, b: (_dot_nt(a, b) * strict).astype(BF16), a_bd, k_rep)
    m_rb = each(lambda a, b: (_dot_nt(a, b) * incl).astype(BF16), r_bd, b_rep)
    m_rk = each(lambda a, b: (_dot_nt(a, b) * incl).astype(BF16), r_bd, k_rep)
    bh_t = each(lambda x: (x.T * same).astype(BF16), bh_rep)
    kh_t = each(lambda x: (x.T * same).astype(BF16), kh_rep)

    pending = list(range(tm // piece))

    def next_features():
        if pending:
            features(pending.pop(0))

    d = each(lambda l: eye + l * mask_ref[MASK_LEVEL0], l_ab_f)
    for lvl in range(1, N_LEVELS):
        d_b = each(lambda x: x.astype(BF16), d)
        x = each(lambda l, db: (_dot(l, db) * mask_ref[MASK_LEVEL0 + lvl]).astype(BF16), l_ab, d_b)
        next_features()
        d = each(lambda dd, db, xx: dd + _dot(db, xx), d, d_b, x)
    t_b = each(lambda x: x.astype(BF16), d)
    while pending:
        next_features()

    wm = each(lambda a, b, vv: _dot(jnp.concatenate([a, b], axis=0), vv), l_ak, m_rk, v_bd)
    twa = each(lambda t, w, a: _dot(t, jnp.concatenate([w[:GROUP_W].astype(BF16), a], axis=1)).astype(BF16),
               t_b, wm, a_bd)
    ry = each(_dot, m_rb, twa)
    mn = each(_dot, bh_t, twa)
    khv = each(_dot, kh_t, v_bd)
    y0 = each(lambda a, w: a[:, :GROUP_W] + w[GROUP_W:], ry, wm)
    n_x = each(lambda a, b: a[:, :GROUP_W] + b, mn, khv)
    mr = each(lambda a, cl, rf, b: jnp.concatenate(
        [(eye * jnp.exp(cl) + a[:, GROUP_W:]).astype(BF16), (rf + b[:, GROUP_W:]).astype(BF16)], axis=0),
        mn, cum_last, r_f, ry)

    st = [st_scr[g] for g in range(N_GROUPS)]
    for i, (rows, sl) in enumerate(chains):
        g = i % N_GROUPS
        ys = _dot(mr[i], st[g].astype(BF16))
        st[g] = ys[:GROUP_W] + n_x[i]
        y_bd = ys[GROUP_W:] + y0[i]
        y = y_bd[0:CHUNK]
        for h in range(1, GROUP_HEADS):
            y = y + y_bd[h * CHUNK:(h + 1) * CHUNK]
        y_ref[rows, sl] = y
    for g in range(N_GROUPS):
        st_scr[g] = st[g]
    o_ref[...] = _rwkv_output(y_ref[...], bonus_set[cur], g_set[cur], lnw_ref, lnb_ref, ones_ref)

    @pl.when(step == pl.num_programs(0) - 1)
    def _():
        sto_ref[...] = st_scr[...]


def _rwkv_prompt(x, shift0, s0, p, ones, *, chunks_per_step=4):
    t = x.shape[0]
    d = RWKV_DIM
    n = RWKV_HEAD_DIM
    tt = CHUNK * chunks_per_step
    assert CHUNK == n and t % tt == 0
    st0 = jnp.einsum("ghvk,hj->ghkjv", s0.reshape(N_GROUPS, GROUP_HEADS, n, n), jnp.eye(GROUP_HEADS, dtype=F32))
    st0 = st0.reshape(N_GROUPS, GROUP_W, GROUP_W)
    tri = jnp.asarray(np.tril(np.ones((CHUNK, CHUNK), np.float32)), BF16)
    eye = jnp.eye(GROUP_W, dtype=F32)
    masks = jnp.asarray(_chunk_masks())
    const = lambda c: (0, 0)
    vec = lambda width: pl.BlockSpec((1, width), const)
    st_spec = pl.BlockSpec((N_GROUPS, GROUP_W, GROUP_W), lambda c: (0, 0, 0))
    feature_set = pltpu.VMEM((2, tt, d), F32)
    pre_blocks = tt // SUBLANES
    n_tiles = t // tt
    fill = lambda c: jnp.minimum(c, n_tiles - 1)
    o, st = pl.pallas_call(
        functools.partial(_rwkv_prompt_kernel, seq=t, tm=tt, n_tiles=n_tiles),
        out_shape=(jax.ShapeDtypeStruct((t, d), BF16), jax.ShapeDtypeStruct((N_GROUPS, GROUP_W, GROUP_W), F32)),
        grid=(n_tiles + 1,),
        in_specs=[
            pl.BlockSpec((tt, RWKV_IN), lambda c: (fill(c), 0)),
            pl.BlockSpec((SUBLANES, RWKV_IN), lambda c: (jnp.maximum(fill(c) * pre_blocks - 1, 0), 0)),
            pl.BlockSpec((1, 1, RWKV_IN), lambda c: (0, 0, 0)),
            vec(RWKV_IN), vec(d),
            pl.BlockSpec((LORA_W, d), const),
            vec(d),
            pl.BlockSpec((LORA_A, d), const),
            pl.BlockSpec((LORA_G, d), const),
            vec(d), vec(d), vec(d),
            pl.BlockSpec(ones.shape, const),
            vec(d), vec(d),
            st_spec,
            pl.BlockSpec((CHUNK, CHUNK), const),
            pl.BlockSpec((GROUP_W, GROUP_W), const),
            pl.BlockSpec(masks.shape, lambda c: (0, 0, 0)),
        ],
        out_specs=(pl.BlockSpec((tt, d), lambda c: (jnp.maximum(c - 1, 0), 0)), st_spec),
        scratch_shapes=[pltpu.VMEM((N_GROUPS, GROUP_W, GROUP_W), F32), pltpu.VMEM((tt, RWKV_IN), F32),
                        pltpu.VMEM((tt, d), F32)] + [feature_set] * 8,
        compiler_params=_cparams("arbitrary"),
        name="rwkv_prompt",
    )(x, x, shift0, p["mu"], p["w0"], p["w_w2"], p["a0"], p["a_w2"], p["g_w2"], p["k_k"], p["k_a"], p["r_k"], ones,
      p["ln_w"], p["ln_b"], st0, tri, eye, masks)
    st5 = st.reshape(N_GROUPS, GROUP_HEADS, n, GROUP_HEADS, n)
    s_new = jnp.einsum("ghkjv,hj->ghvk", st5, jnp.eye(GROUP_HEADS, dtype=F32)).reshape(RWKV_HEADS, n, n)
    return o, s_new


def _rwkv_output(y, bonus, gate, lnw_ref, lnb_ref, ones_ref):
    inv_n = 1.0 / RWKV_HEAD_DIM
    mu = _head_sum(y, ones_ref) * inv_n
    dlt = y - mu
    var = _head_sum(dlt * dlt, ones_ref) * inv_n
    yn = dlt * lax.rsqrt(var + GN_EPS) * lnw_ref[...] + lnb_ref[...]
    return ((yn + bonus) * gate).astype(BF16)


def _rwkv_post_kernel(y_ref, bonus_ref, g_ref, lnw_ref, lnb_ref, ones_ref, o_ref):
    o_ref[...] = _rwkv_output(y_ref[...], bonus_ref[...], g_ref[...], lnw_ref, lnb_ref, ones_ref)


def _rwkv_post(y, bonus, g, lnw, lnb, ones, *, tm=256):
    m = y.shape[0]
    row = lambda i: (i, 0)
    const = lambda i: (0, 0)
    tile = pl.BlockSpec((tm, RWKV_DIM), row)
    return pl.pallas_call(
        _rwkv_post_kernel,
        out_shape=jax.ShapeDtypeStruct((m, RWKV_DIM), BF16),
        grid=(m // tm,),
        in_specs=[tile, tile, tile, pl.BlockSpec((1, RWKV_DIM), const), pl.BlockSpec((1, RWKV_DIM), const),
                  pl.BlockSpec(ones.shape, const)],
        out_specs=tile,
        compiler_params=_cparams("parallel"),
        name="rwkv_post",
    )(y, bonus, g, lnw, lnb, ones)


GATE_BLOCK = math.gcd(PROJ_DIM, D_MODEL)


def _merge_kernel(h_ref, g_ref, os_ref, or_ref, om_ref, wos_ref, wor_ref, wom_ref, wout_ref, *rest, parts):
    gate_refs, (o_ref, u_ref, acc_ref) = rest[:N_BRANCH * parts], rest[N_BRANCH * parts:]
    j = pl.program_id(1)

    @pl.when(j == 0)
    def _():
        u_ref[...] = _rms(h_ref[...], g_ref[...]).astype(BF16)
        acc_ref[...] = jnp.zeros_like(acc_ref)

    u = u_ref[...]
    merged = None
    for b, (x_ref, w_ref) in enumerate(((os_ref, wos_ref), (or_ref, wor_ref), (om_ref, wom_ref))):
        gate = jnp.concatenate([_dot(u, gate_refs[b * parts + c][...]) for c in range(parts)], axis=1)
        term = jax.nn.sigmoid(gate) * _dot(x_ref[...], w_ref[...])
        merged = term if merged is None else merged + term
    acc_ref[...] += _dot(merged.astype(BF16), wout_ref[...])

    @pl.when(j == pl.num_programs(1) - 1)
    def _():
        o_ref[...] = h_ref[...] + acc_ref[...]


def _merge(h, g, o_swa, o_rw, o_mem, w_in, wo_swa, wo_rw, wo_mem, w_out, *, tm=512, tn=512):
    m, d = h.shape
    nt = d // tn
    parts = tn // GATE_BLOCK
    g0 = PROJ_DIM // GATE_BLOCK
    row = lambda i, j: (i, 0)
    col = lambda i, j: (0, j)
    gate_specs = [pl.BlockSpec((d, GATE_BLOCK), functools.partial(
        lambda i, j, off: (0, off + j * parts), off=g0 + b * (d // GATE_BLOCK) + c))
        for b in range(N_BRANCH) for c in range(parts)]
    return pl.pallas_call(
        functools.partial(_merge_kernel, parts=parts),
        out_shape=jax.ShapeDtypeStruct((m, d), F32),
        grid=(m // tm, nt),
        in_specs=[
            pl.BlockSpec((tm, d), row),
            pl.BlockSpec((1, d), lambda i, j: (0, 0)),
            pl.BlockSpec((tm, SWA_Q_DIM), row),
            pl.BlockSpec((tm, RWKV_DIM), row),
            pl.BlockSpec((tm, MEM_DIM), row),
            pl.BlockSpec((SWA_Q_DIM, tn), col),
            pl.BlockSpec((RWKV_DIM, tn), col),
            pl.BlockSpec((MEM_DIM, tn), col),
            pl.BlockSpec((tn, d), lambda i, j: (j, 0)),
        ] + gate_specs,
        out_specs=pl.BlockSpec((tm, d), row),
        scratch_shapes=[pltpu.VMEM((tm, d), BF16), pltpu.VMEM((tm, d), F32)],
        compiler_params=_cparams("parallel", "arbitrary"),
        name="merge",
    )(h, g, o_swa, o_rw, o_mem, wo_swa, wo_rw, wo_mem, w_out, *([w_in] * (N_BRANCH * parts)))


def _t5_bucket(dist):
    max_exact = N_BUCKETS // 2
    d = np.maximum(dist, 0)
    log_ratio = (np.log(np.maximum(d, 1).astype(np.float32) / np.float32(max_exact))
                 / np.float32(math.log(MAX_DISTANCE / max_exact)))
    large = np.minimum(max_exact + (log_ratio * (N_BUCKETS - max_exact)).astype(np.int32), N_BUCKETS - 1)
    return np.where(d < max_exact, d, large).astype(np.int32)


def _rel_bias(table, dist):
    onehot = np.eye(N_BUCKETS, dtype=np.float32)[_t5_bucket(dist).reshape(-1)]
    bias = jnp.einsum("nb,bh->hn", jnp.asarray(onehot), table, precision=lax.Precision.HIGHEST)
    return bias.reshape(SWA_HEADS, *dist.shape)


def _rwkv_branch(xr, shift0, s0, p, ones):
    b, t, _ = xr.shape
    flat = lambda z: z.reshape(b * t, z.shape[-1])
    if b == 1:
        o, s_new = _rwkv_prompt(flat(xr), shift0, s0[0], p, ones)
        return o, s_new[None]
    assert b == LANES, "short sequences are batched one per lane"
    r, w, k, v, kk, kka, g, bonus = _rwkv_prep(flat(xr), shift0, p, ones, seq=t)
    y, s_new = _rwkv_lanes(r, w, k, v, kk, kka, jnp.transpose(s0, (1, 2, 3, 0)), t=t)
    o = _rwkv_post(flat(jnp.transpose(y, (1, 0, 2))), bonus, g, p["ln_w"], p["ln_b"], ones)
    return o, jnp.transpose(s_new, (3, 0, 1, 2))


def kernel(x_prompt, mem_prompt, x_sample, cache_swa_k, cache_swa_v, state_rwkv, state_rwkv_shift, cache_mem_k, cache_mem_v, ffn1_norm, ffn1_wi, ffn1_wo, mix_norm, w_in, swa_sinks, rel_bias_table, rwkv_mu, rwkv_w0, rwkv_w_w2, rwkv_a0, rwkv_a_w2, rwkv_g_w2, rwkv_k_k, rwkv_k_a, rwkv_r_k, rwkv_ln_w, rwkv_ln_b, mem_norm, w_mem_kv, w_o_swa, w_o_rwkv, w_o_mem, w_out, ffn2_norm, ffn2_wi, ffn2_wo, final_norm):
    assert ffn1_wi.shape[0] == 1, "single-layer trunk"
    bp, tp, d = x_prompt.shape
    bs, ts, _ = x_sample.shape
    assert bp == 1
    row = lambda z: z.reshape(1, -1).astype(F32)

    g1, gm, g2, gf = row(ffn1_norm[0]), row(mix_norm[0]), row(ffn2_norm[0]), row(final_norm)
    rp = {
        "mu": row(rwkv_mu[0]), "w0": row(rwkv_w0[0]), "w_w2": rwkv_w_w2[0], "a0": row(rwkv_a0[0]),
        "a_w2": rwkv_a_w2[0], "g_w2": rwkv_g_w2[0], "k_k": row(rwkv_k_k[0]), "k_a": row(rwkv_k_a[0]),
        "r_k": row(rwkv_r_k[0]), "ln_w": row(rwkv_ln_w[0]), "ln_b": row(rwkv_ln_b[0]),
    }
    seg = np.arange(GROUP_W) // RWKV_HEAD_DIM
    ones = jnp.asarray(seg[:, None] == seg[None, :], dtype=BF16)
    sinks = swa_sinks[0].astype(F32)
    table = rel_bias_table.astype(F32)

    xp = x_prompt.reshape(tp, d)
    xs = x_sample.reshape(bs * ts, d)
    hs, wg1, wu1, wo1 = _ffn(xs, g1, ffn1_wi[0], ffn1_wo[0], gf, final_norm=False, emit_weights=True, tf=256)
    hp, wi2, wo2, w_in_b, wo_swa, wo_rw, wo_mem, w_out_b = _ffn(
        xp, g1, (wg1, wu1), wo1, gf, final_norm=False,
        riders=(ffn2_wi[0], ffn2_wo[0], w_in[0], w_o_swa[0], w_o_rwkv[0], w_o_mem[0], w_out[0]))
    wo_swa = wo_swa.reshape(SWA_KV_HEADS, SWA_GROUP, SWA_HEAD_DIM, d).transpose(1, 0, 2, 3).reshape(SWA_Q_DIM, d)
    w_q = w_in_b[:, :SWA_Q_DIM].reshape(d, SWA_KV_HEADS, SWA_GROUP, SWA_HEAD_DIM).transpose(0, 2, 1, 3).reshape(d, SWA_Q_DIM)
    qp, kvp, xrp, qmp = _inproj(hp, gm, w_q, w_in_b)
    qs, kvs, xrs, qms = _inproj(hs, gm, w_q, w_in_b)

    w = WINDOW
    dist_p = np.arange(w)[:, None] + w - np.arange(2 * w)[None, :]
    bias_p = _rel_bias(table, dist_p).reshape(SWA_KV_HEADS, SWA_GROUP, w, 2 * w).transpose(1, 0, 2, 3)
    bias_p = bias_p.reshape(SWA_GROUP, SWA_KV_HEADS * w, 2 * w)
    sink_p = jnp.repeat(sinks.reshape(SWA_KV_HEADS, SWA_GROUP).T, w, axis=1).reshape(SWA_GROUP, SWA_KV_HEADS * w, 1)
    o_swa_p = _swa_prompt(qp, kvp, bias_p, sink_p)

    wbuf = cache_swa_k.shape[2]
    dist_s = np.arange(ts)[:, None] + wbuf - np.arange(wbuf + ts)[None, :]
    bias_s = _rel_bias(table, dist_s).reshape(SWA_HEADS * ts, wbuf + ts)
    sink_rows = jnp.repeat(sinks, ts).reshape(SWA_HEADS * ts, 1)
    qs_gt = qs.reshape(bs, ts, SWA_GROUP, SWA_KV_DIM).transpose(0, 2, 1, 3).reshape(bs, SWA_GROUP * ts, SWA_KV_DIM)
    kbuf = cache_swa_k[0].reshape(bs, wbuf, SWA_KV_DIM).transpose(0, 2, 1)
    vbuf = cache_swa_v[0].reshape(bs, wbuf, SWA_KV_DIM).transpose(0, 2, 1)
    o_swa_s, knew_t, vnew_t = _swa_sample(qs_gt, kvs.reshape(bs, ts, 2 * SWA_KV_DIM), kbuf, vbuf,
                                          bias_s[:, :wbuf], bias_s[:, wbuf:], sink_rows)
    o_swa_s = o_swa_s.reshape(bs, SWA_GROUP, ts, SWA_KV_DIM).transpose(0, 2, 1, 3).reshape(bs * ts, SWA_Q_DIM)

    zero_shift = jnp.zeros((bp, 1, RWKV_IN), F32)
    zero_state = jnp.zeros((bp, RWKV_HEADS, RWKV_HEAD_DIM, RWKV_HEAD_DIM), F32)
    o_rw_p, state_p = _rwkv_branch(xrp.reshape(bp, tp, RWKV_IN), zero_shift, zero_state, rp, ones)
    o_rw_s, state_s = _rwkv_branch(xrs.reshape(bs, ts, RWKV_IN), state_rwkv_shift[0], state_rwkv[0], rp, ones)

    mkv = _norm_matmul(mem_prompt.reshape(N_MEM, d), row(mem_norm[0]), w_mem_kv[0].astype(BF16))
    o_mem_p = _mem_prompt(qmp, mkv)
    o_mem_s = _mem_sample(qms.reshape(bs, ts, MEM_DIM), cache_mem_k[0].reshape(bs, N_MEM * MEM_HEADS, MEM_HEAD_DIM),
                          cache_mem_v[0].reshape(bs, N_MEM * MEM_HEADS, MEM_HEAD_DIM)).reshape(bs * ts, MEM_DIM)

    hp = _merge(hp, gm, o_swa_p, o_rw_p, o_mem_p, w_in_b, wo_swa, wo_rw, wo_mem, w_out_b)
    hs = _merge(hs, gm, o_swa_s, o_rw_s, o_mem_s, w_in_b, wo_swa, wo_rw, wo_mem, w_out_b)
    y_prompt = _ffn(hp, g2, wi2, wo2, gf, final_norm=True).reshape(bp, tp, d)
    y_sample = _ffn(hs, g2, wi2, wo2, gf, final_norm=True).reshape(bs, ts, d)

    wp = min(w, tp)
    p_k = kvp[tp - wp:, :SWA_KV_DIM].reshape(1, bp, wp, SWA_KV_HEADS, SWA_HEAD_DIM)
    p_v = kvp[tp - wp:, SWA_KV_DIM:].reshape(1, bp, wp, SWA_KV_HEADS, SWA_HEAD_DIM)
    p_mk = mkv[:, :MEM_DIM].reshape(1, bp, N_MEM, MEM_HEADS, MEM_HEAD_DIM)
    p_mv = mkv[:, MEM_DIM:].reshape(1, bp, N_MEM, MEM_HEADS, MEM_HEAD_DIM)
    s_k = knew_t.transpose(0, 2, 1).reshape(1, bs, wbuf, SWA_KV_HEADS, SWA_HEAD_DIM)
    s_v = vnew_t.transpose(0, 2, 1).reshape(1, bs, wbuf, SWA_KV_HEADS, SWA_HEAD_DIM)
    return (y_prompt, y_sample,
            p_k, p_v, state_p[None], xrp[tp - 1:].reshape(1, bp, 1, RWKV_IN), p_mk, p_mv,
            s_k, s_v, state_s[None], xrs.reshape(bs, ts, RWKV_IN)[:, ts - 1:][None])
```

```python
import functools
import math

import jax
import jax.numpy as jnp
import numpy as np
from jax import lax
from jax.experimental import pallas as pl
from jax.experimental.pallas import tpu as pltpu

F32 = jnp.float32
BF16 = jnp.bfloat16

D_MODEL = 2048
D_FF = 5632
SWA_HEADS = 16
SWA_KV_HEADS = 4
SWA_GROUP = SWA_HEADS // SWA_KV_HEADS
SWA_HEAD_DIM = 64
SWA_Q_DIM = SWA_HEADS * SWA_HEAD_DIM
SWA_KV_DIM = SWA_KV_HEADS * SWA_HEAD_DIM
WINDOW = 128
N_BUCKETS = 32
MAX_DISTANCE = 128
RWKV_HEADS = 8
RWKV_HEAD_DIM = 64
RWKV_DIM = RWKV_HEADS * RWKV_HEAD_DIM
LORA_W = 64
LORA_A = 64
LORA_G = 128
RWKV_IN = 3 * RWKV_DIM + LORA_W + LORA_A + LORA_G
N_MEM = 256
MEM_HEADS = 4
MEM_HEAD_DIM = 128
MEM_DIM = MEM_HEADS * MEM_HEAD_DIM
N_BRANCH = 3
PROJ_DIM = SWA_Q_DIM + 2 * SWA_KV_DIM + RWKV_IN + MEM_DIM
NORM_EPS = 1e-6
GN_EPS = 64e-5
NEG_INF = -1e30

LANES = 128
SUBLANES = 8
VMEM_LIMIT = 56 * 1024 * 1024


def _cparams(*sem):
    return pltpu.CompilerParams(dimension_semantics=sem, vmem_limit_bytes=VMEM_LIMIT)


def _rms(x, g):
    return x * lax.rsqrt(jnp.mean(x * x, axis=-1, keepdims=True) + NORM_EPS) * g


def _dot(a, b):
    return jnp.dot(a, b, preferred_element_type=F32)


def _dot_nt(a, b):
    return lax.dot_general(a, b, (((1,), (1,)), ((), ())), preferred_element_type=F32)


def _dot_hi(a, b):
    return jnp.dot(a, b, preferred_element_type=F32, precision=lax.Precision.HIGHEST)


def _ffn_kernel(x_ref, g_ref, wg_ref, wu_ref, wo_ref, gf_ref, *rest, final_norm, n_riders, emit_weights):
    rider_in, (o_ref, *extra_out), (xn_ref, acc_ref) = rest[:n_riders], rest[n_riders:-2], rest[-2:]
    for src, dst in zip(rider_in, extra_out[:n_riders]):
        dst[...] = src[...].astype(BF16)
    j = pl.program_id(1)

    @pl.when(j == 0)
    def _():
        xn_ref[...] = _rms(x_ref[...], g_ref[...]).astype(BF16)
        acc_ref[...] = jnp.zeros_like(acc_ref)

    wg, wu, wo = wg_ref[...], wu_ref[...], wo_ref[...]
    if emit_weights:
        wg, wu, wo = wg.astype(BF16), wu.astype(BF16), wo.astype(BF16)
        for dst, val in zip(extra_out[n_riders:], (wg, wu, wo)):
            dst[...] = val
    xn = xn_ref[...]
    gate = _dot(xn, wg)
    up = _dot(xn, wu)
    act = (gate * jax.nn.sigmoid(gate)) * up
    acc_ref[...] += _dot(act.astype(BF16), wo)

    @pl.when(j == pl.num_programs(1) - 1)
    def _():
        h = x_ref[...] + 0.5 * acc_ref[...]
        if final_norm:
            h = _rms(h, gf_ref[...])
        o_ref[...] = h


def _rider_tiling(shape, steps):
    rows, cols = shape
    best, best_score = (1, 1), (0, 0)
    for nr in range(1, rows // 16 + 1):
        if rows % nr or (rows // nr) % 16:
            continue
        for nc in range(1, cols // LANES + 1):
            if cols % nc or (cols // nc) % LANES or nr * nc > steps:
                continue
            score = (min(cols // nc, 1024), nr * nc)
            if score > best_score:
                best, best_score = (nr, nc), score
    return best


def _ffn(x, g, wi, wo, gf, *, final_norm, riders=(), emit_weights=False, tm=512, tf=512):
    m, d = x.shape
    dff = wo.shape[0]
    nf = dff // tf
    steps = (m // tm) * nf
    wg, wu = wi if isinstance(wi, tuple) else (wi, wi)
    up_off = 0 if isinstance(wi, tuple) else nf
    assert not emit_weights or m == tm
    weight_specs = [pl.BlockSpec((d, tf), lambda i, j: (0, j)), pl.BlockSpec((d, tf), lambda i, j: (0, j)),
                    pl.BlockSpec((tf, d), lambda i, j: (j, 0))]
    weight_out = [jax.ShapeDtypeStruct((d, dff), BF16)] * 2 + [jax.ShapeDtypeStruct((dff, d), BF16)]
    rider_specs = []
    for arr in riders:
        nr, nc = _rider_tiling(arr.shape, steps)
        br, bc = arr.shape[0] // nr, arr.shape[1] // nc
        rider_specs.append(pl.BlockSpec((br, bc), functools.partial(
            lambda i, j, nc, last: (jnp.minimum(i * nf + j, last) // nc, jnp.minimum(i * nf + j, last) % nc),
            nc=nc, last=nr * nc - 1)))
    out = pl.pallas_call(
        functools.partial(_ffn_kernel, final_norm=final_norm, n_riders=len(riders), emit_weights=emit_weights),
        out_shape=([jax.ShapeDtypeStruct((m, d), F32)] + [jax.ShapeDtypeStruct(arr.shape, BF16) for arr in riders]
                   + (weight_out if emit_weights else [])),
        grid=(m // tm, nf),
        in_specs=[
            pl.BlockSpec((tm, d), lambda i, j: (i, 0)),
            pl.BlockSpec((1, d), lambda i, j: (0, 0)),
            pl.BlockSpec((d, tf), lambda i, j: (0, j)),
            pl.BlockSpec((d, tf), lambda i, j: (0, j + up_off)),
            pl.BlockSpec((tf, d), lambda i, j: (j, 0)),
            pl.BlockSpec((1, d), lambda i, j: (0, 0)),
        ] + rider_specs,
        out_specs=([pl.BlockSpec((tm, d), lambda i, j: (i, 0))] + rider_specs
                   + (weight_specs if emit_weights else [])),
        scratch_shapes=[pltpu.VMEM((tm, d), BF16), pltpu.VMEM((tm, d), F32)],
        compiler_params=_cparams("arbitrary", "arbitrary"),
        name="ffn_final" if final_norm else "ffn",
    )(x, g, wg, wu, wo, gf, *riders)
    return out if (riders or emit_weights) else out[0]


def _inproj_kernel(h_ref, g_ref, wq_ref, w_ref, q_ref, kv_ref, xr_ref, qm_ref):
    u = _rms(h_ref[...], g_ref[...]).astype(BF16)
    c0, c1, c2 = SWA_Q_DIM, SWA_Q_DIM + 2 * SWA_KV_DIM, SWA_Q_DIM + 2 * SWA_KV_DIM + RWKV_IN
    q_ref[...] = _dot(u, wq_ref[...]).astype(BF16)
    kv_ref[...] = _dot(u, w_ref[:, c0:c1])
    xr_ref[...] = _dot(u, w_ref[:, c1:c2])
    qm_ref[...] = _dot(u, w_ref[:, c2:PROJ_DIM]).astype(BF16)


def _inproj(h, g, wq, w, *, tm=256):
    m, d = h.shape
    row = lambda i: (i, 0)
    return pl.pallas_call(
        _inproj_kernel,
        out_shape=(
            jax.ShapeDtypeStruct((m, SWA_Q_DIM), BF16),
            jax.ShapeDtypeStruct((m, 2 * SWA_KV_DIM), F32),
            jax.ShapeDtypeStruct((m, RWKV_IN), F32),
            jax.ShapeDtypeStruct((m, MEM_DIM), BF16),
        ),
        grid=(m // tm,),
        in_specs=[
            pl.BlockSpec((tm, d), row),
            pl.BlockSpec((1, d), lambda i: (0, 0)),
            pl.BlockSpec((d, SWA_Q_DIM), lambda i: (0, 0), pipeline_mode=pl.Buffered(1)),
            pl.BlockSpec((d, PROJ_DIM), lambda i: (0, 0), pipeline_mode=pl.Buffered(1)),
        ],
        out_specs=(
            pl.BlockSpec((tm, SWA_Q_DIM), row),
            pl.BlockSpec((tm, 2 * SWA_KV_DIM), row),
            pl.BlockSpec((tm, RWKV_IN), row),
            pl.BlockSpec((tm, MEM_DIM), row),
        ),
        compiler_params=_cparams("parallel"),
        name="inproj",
    )(h, g, wq, w)


def _norm_matmul_kernel(x_ref, g_ref, w_ref, o_ref):
    o_ref[...] = _dot(_rms(x_ref[...], g_ref[...]).astype(BF16), w_ref[...])


def _norm_matmul(x, g, w, *, tn=512):
    m, d = x.shape
    n = w.shape[1]
    return pl.pallas_call(
        _norm_matmul_kernel,
        out_shape=jax.ShapeDtypeStruct((m, n), F32),
        grid=(n // tn,),
        in_specs=[
            pl.BlockSpec((m, d), lambda j: (0, 0)),
            pl.BlockSpec((1, d), lambda j: (0, 0)),
            pl.BlockSpec((d, tn), lambda j: (0, j)),
        ],
        out_specs=pl.BlockSpec((m, tn), lambda j: (0, j)),
        compiler_params=_cparams("parallel"),
        name="norm_matmul",
    )(x, g, w)


def _swa_prompt_kernel(q_ref, kvc_ref, kvp_ref, bias_ref, sink_ref, o_ref, *, nq):
    i = pl.program_id(0)
    w = WINDOW
    rows = SWA_KV_HEADS * w
    kv_blocks = [kvp_ref[...]] + [kvc_ref[s * w:(s + 1) * w, :] for s in range(nq)]
    k_blocks = [x[:, 0:SWA_KV_DIM].astype(BF16) for x in kv_blocks]
    v_blocks = [x[:, SWA_KV_DIM:].astype(BF16) for x in kv_blocks]
    qpos = lax.broadcasted_iota(jnp.int32, (rows, 2 * w), 0) % w
    col = lax.broadcasted_iota(jnp.int32, (rows, 2 * w), 1)
    dist = qpos + w - col
    in_window = (dist >= 0) & (dist < w)
    first = in_window & ((col >= w) | (i > 0))
    lane_head = lax.broadcasted_iota(jnp.int32, (w, SWA_KV_DIM), 1) // SWA_HEAD_DIM
    scale = SWA_HEAD_DIM ** -0.5
    pairs = [(s, g) for s in range(nq) for g in range(SWA_GROUP)]
    k = [jnp.concatenate(k_blocks[s:s + 2], axis=0) for s in range(nq)]
    v = [jnp.concatenate(v_blocks[s:s + 2], axis=0) for s in range(nq)]
    logits = []
    for s, g in pairs:
        qg = q_ref[s * w:(s + 1) * w, g * SWA_KV_DIM:(g + 1) * SWA_KV_DIM].astype(F32) * scale
        qs = jnp.concatenate([jnp.where(lane_head == kvh, qg, 0.0) for kvh in range(SWA_KV_HEADS)], axis=0)
        lg = _dot_nt(qs.astype(BF16), k[s])
        logits.append(jnp.where(first if s == 0 else in_window, lg + bias_ref[g], NEG_INF))
    sink = [sink_ref[g] for _, g in pairs]
    m = [jnp.maximum(jnp.max(x, axis=-1, keepdims=True), sk) for x, sk in zip(logits, sink)]
    p = [jnp.exp(x - mi) for x, mi in zip(logits, m)]
    inv = [1.0 / (jnp.sum(pi, axis=-1, keepdims=True) + jnp.exp(sk - mi)) for pi, sk, mi in zip(p, sink, m)]
    ov = [_dot((pi * ii).astype(BF16), v[s]) for (s, _), pi, ii in zip(pairs, p, inv)]
    for (s, g), o in zip(pairs, ov):
        og = jnp.zeros((w, SWA_KV_DIM), F32)
        for kvh in range(SWA_KV_HEADS):
            og = jnp.where(lane_head == kvh, o[kvh * w:(kvh + 1) * w], og)
        o_ref[s * w:(s + 1) * w, g * SWA_KV_DIM:(g + 1) * SWA_KV_DIM] = og.astype(BF16)


def _swa_prompt(q, kv, bias, sink_rows, *, nq=8):
    t = q.shape[0]
    w = WINDOW
    rows = SWA_KV_HEADS * w
    return pl.pallas_call(
        functools.partial(_swa_prompt_kernel, nq=nq),
        out_shape=jax.ShapeDtypeStruct((t, SWA_Q_DIM), BF16),
        grid=(t // (nq * w),),
        in_specs=[
            pl.BlockSpec((nq * w, SWA_Q_DIM), lambda i: (i, 0)),
            pl.BlockSpec((nq * w, 2 * SWA_KV_DIM), lambda i: (i, 0)),
            pl.BlockSpec((w, 2 * SWA_KV_DIM), lambda i: (jnp.maximum(i * nq - 1, 0), 0)),
            pl.BlockSpec((SWA_GROUP, rows, 2 * w), lambda i: (0, 0, 0)),
            pl.BlockSpec((SWA_GROUP, rows, 1), lambda i: (0, 0, 0)),
        ],
        out_specs=pl.BlockSpec((nq * w, SWA_Q_DIM), lambda i: (i, 0)),
        compiler_params=_cparams("parallel"),
        name="swa_prompt",
    )(q, kv, kv, bias, sink_rows)


def _swa_sample_kernel(q_ref, kvn_ref, kb_ref, vb_ref, bias_b_ref, bias_n_ref, sink_ref, o_ref, ko_ref, vo_ref,
                       *, bb, t):
    gt = SWA_GROUP * t
    rows = SWA_KV_HEADS * gt
    w = kb_ref.shape[2]
    scale = SWA_HEAD_DIM ** -0.5
    lane_head = lax.broadcasted_iota(jnp.int32, (gt, SWA_KV_DIM), 1) // SWA_HEAD_DIM
    tok = lax.broadcasted_iota(jnp.int32, (rows, w), 0) % t
    keyj = lax.broadcasted_iota(jnp.int32, (rows, w), 1)
    valid_b = (tok + w - keyj) < WINDOW
    tok_n = lax.broadcasted_iota(jnp.int32, (rows, 1), 0) % t
    sink = sink_ref[...]
    bs = range(bb)
    toks = range(t)
    qall = [jnp.concatenate([jnp.where(lane_head == kvh, q_ref[b].astype(F32), 0.0) for kvh in range(SWA_KV_HEADS)],
                            axis=0) for b in bs]
    kvn = [kvn_ref[b] for b in bs]
    lb = [_dot(qall[b].astype(BF16), kb_ref[b].astype(BF16)) for b in bs]
    lb = [jnp.where(valid_b, lb[b] * scale + bias_b_ref[...], NEG_INF) for b in bs]
    ln = [[jnp.sum(qall[b] * kvn[b][j:j + 1, 0:SWA_KV_DIM], axis=-1, keepdims=True) for j in toks] for b in bs]
    ln = [[jnp.where(tok_n >= j, ln[b][j] * scale + bias_n_ref[:, j:j + 1], NEG_INF) for j in toks] for b in bs]
    m = [jnp.maximum(jnp.max(lb[b], axis=-1, keepdims=True), sink) for b in bs]
    m = [functools.reduce(jnp.maximum, ln[b], m[b]) for b in bs]
    pb = [jnp.exp(lb[b] - m[b]) for b in bs]
    pn = [[jnp.exp(ln[b][j] - m[b]) for j in toks] for b in bs]
    denom = [jnp.sum(pb[b], axis=-1, keepdims=True) + jnp.exp(sink - m[b]) for b in bs]
    inv = [1.0 / functools.reduce(jnp.add, pn[b], denom[b]) for b in bs]
    oall = [_dot_nt((pb[b] * inv[b]).astype(BF16), vb_ref[b].astype(BF16)) for b in bs]
    for b in bs:
        ob = oall[b]
        for j in toks:
            ob = ob + (pn[b][j] * inv[b]) * kvn[b][j:j + 1, SWA_KV_DIM:]
        og = jnp.zeros((gt, SWA_KV_DIM), F32)
        for kvh in range(SWA_KV_HEADS):
            og = jnp.where(lane_head == kvh, ob[kvh * gt:(kvh + 1) * gt], og)
        o_ref[b] = og.astype(BF16)
    pos = lax.broadcasted_iota(jnp.int32, (SWA_KV_DIM, w), 1)
    pad = jnp.zeros((w - SUBLANES, 2 * SWA_KV_DIM), F32)
    row8 = lax.broadcasted_iota(jnp.int32, (SUBLANES, 2 * SWA_KV_DIM), 0)
    for b in bs:
        last8 = jnp.zeros((SUBLANES, 2 * SWA_KV_DIM), F32)
        for j in toks:
            last8 = jnp.where(row8 == SUBLANES - t + j, kvn[b][j:j + 1], last8)
        tail_t = jnp.concatenate([pad, last8], axis=0).T
        ko_ref[b] = jnp.where(pos >= w - t, tail_t[:SWA_KV_DIM], pltpu.roll(kb_ref[b], w - t, axis=1))
        vo_ref[b] = jnp.where(pos >= w - t, tail_t[SWA_KV_DIM:], pltpu.roll(vb_ref[b], w - t, axis=1))


def _swa_sample(q, kvn, kbuf, vbuf, bias_b, bias_n, sink_rows, *, bb=8):
    b, gt, _ = q.shape
    t = kvn.shape[1]
    w = kbuf.shape[2]
    rows = SWA_KV_HEADS * gt
    blk = lambda i: (i, 0, 0)
    const = lambda i: (0, 0)
    cache = pl.BlockSpec((bb, SWA_KV_DIM, w), blk)
    return pl.pallas_call(
        functools.partial(_swa_sample_kernel, bb=bb, t=t),
        out_shape=(jax.ShapeDtypeStruct((b, gt, SWA_KV_DIM), BF16),
                   jax.ShapeDtypeStruct(kbuf.shape, F32), jax.ShapeDtypeStruct(vbuf.shape, F32)),
        grid=(b // bb,),
        in_specs=[
            pl.BlockSpec((bb, gt, SWA_KV_DIM), blk),
            pl.BlockSpec((bb, t, 2 * SWA_KV_DIM), blk),
            cache, cache,
            pl.BlockSpec((rows, w), const),
            pl.BlockSpec((rows, t), const),
            pl.BlockSpec((rows, 1), const),
        ],
        out_specs=(pl.BlockSpec((bb, gt, SWA_KV_DIM), blk), cache, cache),
        compiler_params=_cparams("parallel"),
        name="swa_sample",
    )(q, kvn, kbuf, vbuf, bias_b, bias_n, sink_rows)


def _mem_heads(q, mk, mv):
    scale = MEM_HEAD_DIM ** -0.5
    cols = [slice(h * MEM_HEAD_DIM, (h + 1) * MEM_HEAD_DIM) for h in range(MEM_HEADS)]
    x = [_dot_nt(q[:, sl], mk[:, sl]) * scale for sl in cols]
    m = [jnp.max(xi, axis=-1, keepdims=True) for xi in x]
    p = [jnp.exp(xi - mi) for xi, mi in zip(x, m)]
    inv = [1.0 / jnp.sum(pi, axis=-1, keepdims=True) for pi in p]
    return jnp.concatenate([_dot((pi * ii).astype(BF16), mv[:, sl]) for pi, ii, sl in zip(p, inv, cols)], axis=-1)


def _mem_prompt_kernel(q_ref, mk_ref, mv_ref, o_ref):
    o_ref[...] = _mem_heads(q_ref[...], mk_ref[...].astype(BF16), mv_ref[...].astype(BF16)).astype(BF16)


def _mem_prompt(q, mkv, *, tm=512):
    m = q.shape[0]
    return pl.pallas_call(
        _mem_prompt_kernel,
        out_shape=jax.ShapeDtypeStruct((m, MEM_DIM), BF16),
        grid=(m // tm,),
        in_specs=[
            pl.BlockSpec((tm, MEM_DIM), lambda i: (i, 0)),
            pl.BlockSpec((N_MEM, MEM_DIM), lambda i: (0, 0)),
            pl.BlockSpec((N_MEM, MEM_DIM), lambda i: (0, 1)),
        ],
        out_specs=pl.BlockSpec((tm, MEM_DIM), lambda i: (i, 0)),
        compiler_params=_cparams("parallel"),
        name="mem_prompt",
    )(q, mkv, mkv)


def _mem_sample_kernel(q_ref, mk_ref, mv_ref, o_ref, *, bb):
    scale = MEM_HEAD_DIM ** -0.5
    pairs = [(b, h) for b in range(bb) for h in range(MEM_HEADS)]
    rows = lambda h: pl.ds(h, N_MEM, stride=MEM_HEADS)
    cols = lambda h: slice(h * MEM_HEAD_DIM, (h + 1) * MEM_HEAD_DIM)
    q = [q_ref[b] for b in range(bb)]
    x = [_dot_nt(q[b][:, cols(h)], mk_ref[b, rows(h), :].astype(BF16)) * scale for b, h in pairs]
    m = [jnp.max(xi, axis=-1, keepdims=True) for xi in x]
    p = [jnp.exp(xi - mi) for xi, mi in zip(x, m)]
    inv = [1.0 / jnp.sum(pi, axis=-1, keepdims=True) for pi in p]
    o = [_dot((pi * ii).astype(BF16), mv_ref[b, rows(h), :].astype(BF16)) for (b, h), pi, ii in zip(pairs, p, inv)]
    for b in range(bb):
        o_ref[b] = jnp.concatenate(o[b * MEM_HEADS:(b + 1) * MEM_HEADS], axis=-1).astype(BF16)


def _mem_sample(q, mk, mv, *, bb=8):
    b, t, _ = q.shape
    blk = lambda i: (i, 0, 0)
    return pl.pallas_call(
        functools.partial(_mem_sample_kernel, bb=bb),
        out_shape=jax.ShapeDtypeStruct((b, t, MEM_DIM), BF16),
        grid=(b // bb,),
        in_specs=[
            pl.BlockSpec((bb, t, MEM_DIM), blk),
            pl.BlockSpec((bb, N_MEM * MEM_HEADS, MEM_HEAD_DIM), blk),
            pl.BlockSpec((bb, N_MEM * MEM_HEADS, MEM_HEAD_DIM), blk),
        ],
        out_specs=pl.BlockSpec((bb, t, MEM_DIM), blk),
        compiler_params=_cparams("parallel"),
        name="mem_sample",
    )(q, mk, mv)


def _head_sum(x, ones_ref):
    hi = x.astype(BF16)
    lo = (x - hi.astype(F32)).astype(BF16)
    ones = ones_ref[...]
    w = ones.shape[0]
    return jnp.concatenate([_dot(hi[:, c:c + w], ones) + _dot(lo[:, c:c + w], ones) for c in range(0, x.shape[1], w)],
                           axis=1)


def _token_shift(x_ref, pre_ref, start_ref, mu_ref, *, seq, tm, tile):
    x = x_ref[...]
    row = lax.broadcasted_iota(jnp.int32, x.shape, 0)
    shifted = pltpu.roll(x, 1, axis=0)
    if seq >= tm:
        is_start = (tile * tm) % seq == 0
        first_prev = jnp.where(is_start, start_ref[0], pre_ref[SUBLANES - 1:SUBLANES, :])
        prev = jnp.where(row == 0, first_prev, shifted)
    else:
        prev = jnp.where(row % seq == 0, start_ref[...], shifted)
    return x + mu_ref[...] * (prev - x)


def _rwkv_features(xs, w0_ref, ww2_ref, a0_ref, aw2_ref, gw2_ref, kk_ref, ka_ref, rk_ref, ones_ref):
    d = RWKV_DIM
    r = xs[:, 0:d]
    k = xs[:, d:2 * d]
    v = xs[:, 2 * d:3 * d]
    lw = xs[:, 3 * d:3 * d + LORA_W]
    la = xs[:, 3 * d + LORA_W:3 * d + LORA_W + LORA_A]
    lg = xs[:, 3 * d + LORA_W + LORA_A:]
    wpre = w0_ref[...] + _dot_hi(jnp.tanh(lw), ww2_ref[...])
    w = -jax.nn.softplus(-wpre) - 0.5
    log_decay = -jnp.exp(w)
    a = jax.nn.sigmoid(a0_ref[...] + _dot_hi(la, aw2_ref[...]))
    g = _dot_hi(jax.nn.sigmoid(lg), gw2_ref[...])
    kk = k * kk_ref[...]
    kk = kk / jnp.maximum(jnp.sqrt(_head_sum(kk * kk, ones_ref)), 1e-12)
    kh = k * (1.0 + (a - 1.0) * ka_ref[...])
    bonus = _head_sum(r * kh * rk_ref[...], ones_ref) * v
    return r, log_decay, kh, v, kk, kk * a, g, bonus


def _rwkv_prep_kernel(x_ref, pre_ref, start_ref, mu_ref, w0_ref, ww2_ref, a0_ref, aw2_ref, gw2_ref, kk_ref, ka_ref,
                      rk_ref, ones_ref, *outs, seq, tm):
    xs = _token_shift(x_ref, pre_ref, start_ref, mu_ref, seq=seq, tm=tm, tile=pl.program_id(0))
    feats = _rwkv_features(xs, w0_ref, ww2_ref, a0_ref, aw2_ref, gw2_ref, kk_ref, ka_ref, rk_ref, ones_ref)
    for o_ref, val in zip(outs, feats):
        o_ref[...] = val


def _rwkv_prep(x, shift0, p, ones, *, seq, tm=256):
    m = x.shape[0]
    row = lambda i: (i, 0)
    const = lambda i: (0, 0)
    vec = lambda n: pl.BlockSpec((1, n), const)
    out = jax.ShapeDtypeStruct((m, RWKV_DIM), F32)
    if seq >= tm:
        assert seq % tm == 0
        start = shift0
        start_spec = pl.BlockSpec((1, 1, RWKV_IN), lambda i: ((i * tm) // seq, 0, 0))
    else:
        assert tm % seq == 0
        start = jnp.repeat(shift0[:, 0], seq, axis=0)
        start_spec = pl.BlockSpec((tm, RWKV_IN), row)
    pre_blocks = tm // SUBLANES
    return pl.pallas_call(
        functools.partial(_rwkv_prep_kernel, seq=seq, tm=tm),
        out_shape=(out,) * 8,
        grid=(m // tm,),
        in_specs=[
            pl.BlockSpec((tm, RWKV_IN), row),
            pl.BlockSpec((SUBLANES, RWKV_IN), lambda i: (jnp.maximum(i * pre_blocks - 1, 0), 0)),
            start_spec,
            vec(RWKV_IN), vec(RWKV_DIM),
            pl.BlockSpec((LORA_W, RWKV_DIM), const),
            vec(RWKV_DIM),
            pl.BlockSpec((LORA_A, RWKV_DIM), const),
            pl.BlockSpec((LORA_G, RWKV_DIM), const),
            vec(RWKV_DIM), vec(RWKV_DIM), vec(RWKV_DIM),
            pl.BlockSpec(ones.shape, const),
        ],
        out_specs=(pl.BlockSpec((tm, RWKV_DIM), row),) * 8,
        compiler_params=_cparams("parallel"),
        name="rwkv_prep",
    )(x, x, start, p["mu"], p["w0"], p["w_w2"], p["a0"], p["a_w2"], p["g_w2"], p["k_k"], p["k_a"], p["r_k"], ones)


def _rwkv_lanes_kernel(r_ref, lw_ref, k_ref, v_ref, kk_ref, kka_ref, s_ref, y_ref, so_ref, v_scr, y_scr, *, t):
    n = RWKV_HEAD_DIM
    nb = s_ref.shape[-1]
    heads = range(2)

    def token_major(ref, j):
        return ref[pl.ds(j, nb, stride=t), :].T

    for j in range(t):
        v_scr[...] = token_major(v_ref, j)
        r_t, k_t, kk_t, kka_t = (token_major(ref, j) for ref in (r_ref, k_ref, kk_ref, kka_ref))
        w_t = jnp.exp(token_major(lw_ref, j))
        src = s_ref if j == 0 else so_ref

        def value_group(g, carry):
            rows = pl.multiple_of(g * SUBLANES, SUBLANES)
            ys = [[] for _ in heads]
            vg = [v_scr[pl.ds(h * n + rows, SUBLANES), :] for h in heads]
            for i in range(SUBLANES):
                for h in heads:
                    f = slice(h * n, (h + 1) * n)
                    s = src[h, rows + i]
                    sa = jnp.sum(s * (-kk_t[f]), axis=0, keepdims=True)
                    s = s * w_t[f] + sa * kka_t[f] + vg[h][i:i + 1] * k_t[f]
                    so_ref[h, rows + i] = s
                    ys[h].append(jnp.sum(s * r_t[f], axis=0, keepdims=True))
            for h in heads:
                y_scr[pl.ds(h * n + rows, SUBLANES), :] = jnp.concatenate(ys[h], axis=0)
            return carry

        lax.fori_loop(0, n // SUBLANES, value_group, 0)
        y_ref[j] = y_scr[...].T


def _rwkv_lanes(r, lw, k, v, kk, kka, s0, *, t):
    m, d = r.shape
    nb = m // t
    n = RWKV_HEAD_DIM
    assert nb == LANES, "one batch per lane"
    tok = pl.BlockSpec((m, LANES), lambda p: (0, p))
    st = pl.BlockSpec((2, n, n, nb), lambda p: (p, 0, 0, 0))
    return pl.pallas_call(
        functools.partial(_rwkv_lanes_kernel, t=t),
        out_shape=(jax.ShapeDtypeStruct((t, nb, d), F32), jax.ShapeDtypeStruct((RWKV_HEADS, n, n, nb), F32)),
        grid=(RWKV_HEADS // 2,),
        in_specs=[tok] * 6 + [st],
        out_specs=(pl.BlockSpec((t, nb, LANES), lambda p: (0, 0, p)), st),
        scratch_shapes=[pltpu.VMEM((LANES, nb), F32), pltpu.VMEM((LANES, nb), F32)],
        compiler_params=_cparams("parallel"),
        name="rwkv_lanes",
    )(r, lw, k, v, kk, kka, s0)


CHUNK = 64
GROUP_HEADS = 4
GROUP_W = GROUP_HEADS * RWKV_HEAD_DIM
N_GROUPS = RWKV_HEADS // GROUP_HEADS
(MASK_SAME, MASK_STRICT, MASK_INCL, MASK_LEVEL0) = (0, 1, 2, 3)
N_LEVELS = int(math.log2(CHUNK))


def _chunk_masks():
    i = np.arange(GROUP_W)
    same = (i[:, None] // CHUNK) == (i[None, :] // CHUNK)
    masks = [same, same & (i[None, :] < i[:, None]), same & (i[None, :] <= i[:, None])]
    for lvl in range(N_LEVELS):
        m = 1 << lvl
        masks.append(((i[:, None] // (2 * m)) == (i[None, :] // (2 * m))) & ((i[:, None] // m) != (i[None, :] // m))
                     & (i[None, :] < i[:, None]))
    return np.stack(masks).astype(np.float32)


def _rwkv_prompt_kernel(x_ref, pre_ref, start_ref, mu_ref, w0_ref, ww2_ref, a0_ref, aw2_ref, gw2_ref, kk_p_ref,
                        ka_ref, rk_ref, ones_ref, lnw_ref, lnb_ref, st0_ref, tri_ref, eye_ref, mask_ref,
                        o_ref, sto_ref, st_scr, xs_ref, y_ref, r_set, lw_set, k_set, v_set, kk_set, kka_set, g_set,
                        bonus_set, *, seq, tm, n_tiles):
    step = pl.program_id(0)
    cur = (step + 1) % 2
    nxt = step % 2
    sets = (r_set, lw_set, k_set, v_set, kk_set, kka_set, g_set, bonus_set)
    r_ref, lw_ref, k_ref, v_ref, kk_ref, kka_ref = (s.at[cur] for s in sets[:6])

    @pl.when(step == 0)
    def _():
        st_scr[...] = st0_ref[...]
        for s in sets:
            s[1] = jnp.zeros(s.shape[1:], F32)

    xs_ref[...] = _token_shift(x_ref, pre_ref, start_ref, mu_ref, seq=seq, tm=tm, tile=jnp.minimum(step, n_tiles - 1))
    n_chunks = tm // CHUNK

    piece = 2 * CHUNK
    def features(c):
        rows = slice(c * piece, (c + 1) * piece)
        feats = _rwkv_features(xs_ref[rows, :], w0_ref, ww2_ref, a0_ref, aw2_ref, gw2_ref, kk_p_ref, ka_ref, rk_ref,
                               ones_ref)
        for s, val in zip(sets, feats):
            s[nxt, rows, :] = val

    eye = eye_ref[...]
    tri = tri_ref[...]
    tile_rows = lambda x: jnp.concatenate([x] * GROUP_HEADS, axis=0)
    block_diag = lambda x: (tile_rows(x) * mask_ref[MASK_SAME]).astype(BF16)

    chains = [(slice(c * CHUNK, (c + 1) * CHUNK), slice(g * GROUP_W, (g + 1) * GROUP_W))
              for c in range(n_chunks) for g in range(N_GROUPS)]
    each = lambda f, *cols: [f(*args) for args in zip(*cols)]
    same, strict, incl = mask_ref[MASK_SAME], mask_ref[MASK_STRICT], mask_ref[MASK_INCL]

    def cum_decay(lw):
        h1 = lw.astype(BF16)
        r1 = lw - h1.astype(F32)
        h2 = r1.astype(BF16)
        h3 = (r1 - h2.astype(F32)).astype(BF16)
        return _dot(tri, h1) + _dot(tri, h2) + _dot(tri, h3)

    lw = [lw_ref[rows, sl] for rows, sl in chains]
    kka = [kka_ref[rows, sl] for rows, sl in chains]
    k = [k_ref[rows, sl] for rows, sl in chains]
    cum = each(cum_decay, lw)
    cum_last = each(lambda c: c[CHUNK - 1:CHUNK, :], cum)
    p_inv = each(lambda c: jnp.exp(-c), cum)
    p_tail = each(lambda c, cl: jnp.exp(cl - c), cum, cum_last)
    a_bd = [block_diag(-kk_ref[rows, sl] * jnp.exp(c - l)) for (rows, sl), c, l in zip(chains, cum, lw)]
    r_f = [tile_rows(r_ref[rows, sl] * jnp.exp(c)) * same for (rows, sl), c in zip(chains, cum)]
    r_bd = each(lambda x: x.astype(BF16), r_f)
    v_bd = [block_diag(v_ref[rows, sl]) for rows, sl in chains]
    b_rep = each(lambda x, p: tile_rows((x * p).astype(BF16)), kka, p_inv)
    k_rep = each(lambda x, p: tile_rows((x * p).astype(BF16)), k, p_inv)
    bh_rep = each(lambda x, p: tile_rows(x * p), kka, p_tail)
    kh_rep = each(lambda x, p: tile_rows(x * p), k, p_tail)

    l_ab_f = each(lambda a, b: _dot_nt(a, b) * strict, a_bd, b_rep)
    l_ab = each(lambda x: x.astype(BF16), l_ab_f)
    l_ak = each(lambda a, b: (_dot_nt(a, b) * strict).astype(BF16), a_bd, k_rep)
    m_rb = each(lambda a, b: (_dot_nt(a, b) * incl).astype(BF16), r_bd, b_rep)
    m_rk = each(lambda a, b: (_dot_nt(a, b) * incl).astype(BF16), r_bd, k_rep)
    bh_t = each(lambda x: (x.T * same).astype(BF16), bh_rep)
    kh_t = each(lambda x: (x.T * same).astype(BF16), kh_rep)

    pending = list(range(tm // piece))

    def next_features():
        if pending:
            features(pending.pop(0))

    d = each(lambda l: eye + l * mask_ref[MASK_LEVEL0], l_ab_f)
    for lvl in range(1, N_LEVELS):
        d_b = each(lambda x: x.astype(BF16), d)
        x = each(lambda l, db: (_dot(l, db) * mask_ref[MASK_LEVEL0 + lvl]).astype(BF16), l_ab, d_b)
        next_features()
        d = each(lambda dd, db, xx: dd + _dot(db, xx), d, d_b, x)
    t_b = each(lambda x: x.astype(BF16), d)
    while pending:
        next_features()

    wm = each(lambda a, b, vv: _dot(jnp.concatenate([a, b], axis=0), vv), l_ak, m_rk, v_bd)
    twa = each(lambda t, w, a: _dot(t, jnp.concatenate([w[:GROUP_W].astype(BF16), a], axis=1)).astype(BF16),
               t_b, wm, a_bd)
    ry = each(_dot, m_rb, twa)
    mn = each(_dot, bh_t, twa)
    khv = each(_dot, kh_t, v_bd)
    y0 = each(lambda a, w: a[:, :GROUP_W] + w[GROUP_W:], ry, wm)
    n_x = each(lambda a, b: a[:, :GROUP_W] + b, mn, khv)
    mr = each(lambda a, cl, rf, b: jnp.concatenate(
        [(eye * jnp.exp(cl) + a[:, GROUP_W:]).astype(BF16), (rf + b[:, GROUP_W:]).astype(BF16)], axis=0),
        mn, cum_last, r_f, ry)

    st = [st_scr[g] for g in range(N_GROUPS)]
    for i, (rows, sl) in enumerate(chains):
        g = i % N_GROUPS
        ys = _dot(mr[i], st[g].astype(BF16))
        st[g] = ys[:GROUP_W] + n_x[i]
        y_bd = ys[GROUP_W:] + y0[i]
        y = y_bd[0:CHUNK]
        for h in range(1, GROUP_HEADS):
            y = y + y_bd[h * CHUNK:(h + 1) * CHUNK]
        y_ref[rows, sl] = y
    for g in range(N_GROUPS):
        st_scr[g] = st[g]
    o_ref[...] = _rwkv_output(y_ref[...], bonus_set[cur], g_set[cur], lnw_ref, lnb_ref, ones_ref)

    @pl.when(step == pl.num_programs(0) - 1)
    def _():
        sto_ref[...] = st_scr[...]


def _rwkv_prompt(x, shift0, s0, p, ones, *, chunks_per_step=4):
    t = x.shape[0]
    d = RWKV_DIM
    n = RWKV_HEAD_DIM
    tt = CHUNK * chunks_per_step
    assert CHUNK == n and t % tt == 0
    st0 = jnp.einsum("ghvk,hj->ghkjv", s0.reshape(N_GROUPS, GROUP_HEADS, n, n), jnp.eye(GROUP_HEADS, dtype=F32))
    st0 = st0.reshape(N_GROUPS, GROUP_W, GROUP_W)
    tri = jnp.asarray(np.tril(np.ones((CHUNK, CHUNK), np.float32)), BF16)
    eye = jnp.eye(GROUP_W, dtype=F32)
    masks = jnp.asarray(_chunk_masks())
    const = lambda c: (0, 0)
    vec = lambda width: pl.BlockSpec((1, width), const)
    st_spec = pl.BlockSpec((N_GROUPS, GROUP_W, GROUP_W), lambda c: (0, 0, 0))
    feature_set = pltpu.VMEM((2, tt, d), F32)
    pre_blocks = tt // SUBLANES
    n_tiles = t // tt
    fill = lambda c: jnp.minimum(c, n_tiles - 1)
    o, st = pl.pallas_call(
        functools.partial(_rwkv_prompt_kernel, seq=t, tm=tt, n_tiles=n_tiles),
        out_shape=(jax.ShapeDtypeStruct((t, d), BF16), jax.ShapeDtypeStruct((N_GROUPS, GROUP_W, GROUP_W), F32)),
        grid=(n_tiles + 1,),
        in_specs=[
            pl.BlockSpec((tt, RWKV_IN), lambda c: (fill(c), 0)),
            pl.BlockSpec((SUBLANES, RWKV_IN), lambda c: (jnp.maximum(fill(c) * pre_blocks - 1, 0), 0)),
            pl.BlockSpec((1, 1, RWKV_IN), lambda c: (0, 0, 0)),
            vec(RWKV_IN), vec(d),
            pl.BlockSpec((LORA_W, d), const),
            vec(d),
            pl.BlockSpec((LORA_A, d), const),
            pl.BlockSpec((LORA_G, d), const),
            vec(d), vec(d), vec(d),
            pl.BlockSpec(ones.shape, const),
            vec(d), vec(d),
            st_spec,
            pl.BlockSpec((CHUNK, CHUNK), const),
            pl.BlockSpec((GROUP_W, GROUP_W), const),
            pl.BlockSpec(masks.shape, lambda c: (0, 0, 0)),
        ],
        out_specs=(pl.BlockSpec((tt, d), lambda c: (jnp.maximum(c - 1, 0), 0)), st_spec),
        scratch_shapes=[pltpu.VMEM((N_GROUPS, GROUP_W, GROUP_W), F32), pltpu.VMEM((tt, RWKV_IN), F32),
                        pltpu.VMEM((tt, d), F32)] + [feature_set] * 8,
        compiler_params=_cparams("arbitrary"),
        name="rwkv_prompt",
    )(x, x, shift0, p["mu"], p["w0"], p["w_w2"], p["a0"], p["a_w2"], p["g_w2"], p["k_k"], p["k_a"], p["r_k"], ones,
      p["ln_w"], p["ln_b"], st0, tri, eye, masks)
    st5 = st.reshape(N_GROUPS, GROUP_HEADS, n, GROUP_HEADS, n)
    s_new = jnp.einsum("ghkjv,hj->ghvk", st5, jnp.eye(GROUP_HEADS, dtype=F32)).reshape(RWKV_HEADS, n, n)
    return o, s_new


def _rwkv_output(y, bonus, gate, lnw_ref, lnb_ref, ones_ref):
    inv_n = 1.0 / RWKV_HEAD_DIM
    mu = _head_sum(y, ones_ref) * inv_n
    dlt = y - mu
    var = _head_sum(dlt * dlt, ones_ref) * inv_n
    yn = dlt * lax.rsqrt(var + GN_EPS) * lnw_ref[...] + lnb_ref[...]
    return ((yn + bonus) * gate).astype(BF16)


def _rwkv_post_kernel(y_ref, bonus_ref, g_ref, lnw_ref, lnb_ref, ones_ref, o_ref):
    o_ref[...] = _rwkv_output(y_ref[...], bonus_ref[...], g_ref[...], lnw_ref, lnb_ref, ones_ref)


def _rwkv_post(y, bonus, g, lnw, lnb, ones, *, tm=256):
    m = y.shape[0]
    row = lambda i: (i, 0)
    const = lambda i: (0, 0)
    tile = pl.BlockSpec((tm, RWKV_DIM), row)
    return pl.pallas_call(
        _rwkv_post_kernel,
        out_shape=jax.ShapeDtypeStruct((m, RWKV_DIM), BF16),
        grid=(m // tm,),
        in_specs=[tile, tile, tile, pl.BlockSpec((1, RWKV_DIM), const), pl.BlockSpec((1, RWKV_DIM), const),
                  pl.BlockSpec(ones.shape, const)],
        out_specs=tile,
        compiler_params=_cparams("parallel"),
        name="rwkv_post",
    )(y, bonus, g, lnw, lnb, ones)


GATE_BLOCK = math.gcd(PROJ_DIM, D_MODEL)


def _merge_kernel(h_ref, g_ref, os_ref, or_ref, om_ref, wos_ref, wor_ref, wom_ref, wout_ref, *rest, parts):
    gate_refs, (o_ref, u_ref, acc_ref) = rest[:N_BRANCH * parts], rest[N_BRANCH * parts:]
    j = pl.program_id(1)

    @pl.when(j == 0)
    def _():
        u_ref[...] = _rms(h_ref[...], g_ref[...]).astype(BF16)
        acc_ref[...] = jnp.zeros_like(acc_ref)

    u = u_ref[...]
    merged = None
    for b, (x_ref, w_ref) in enumerate(((os_ref, wos_ref), (or_ref, wor_ref), (om_ref, wom_ref))):
        gate = jnp.concatenate([_dot(u, gate_refs[b * parts + c][...]) for c in range(parts)], axis=1)
        term = jax.nn.sigmoid(gate) * _dot(x_ref[...], w_ref[...])
        merged = term if merged is None else merged + term
    acc_ref[...] += _dot(merged.astype(BF16), wout_ref[...])

    @pl.when(j == pl.num_programs(1) - 1)
    def _():
        o_ref[...] = h_ref[...] + acc_ref[...]


def _merge(h, g, o_swa, o_rw, o_mem, w_in, wo_swa, wo_rw, wo_mem, w_out, *, tm=512, tn=512):
    m, d = h.shape
    nt = d // tn
    parts = tn // GATE_BLOCK
    g0 = PROJ_DIM // GATE_BLOCK
    row = lambda i, j: (i, 0)
    col = lambda i, j: (0, j)
    gate_specs = [pl.BlockSpec((d, GATE_BLOCK), functools.partial(
        lambda i, j, off: (0, off + j * parts), off=g0 + b * (d // GATE_BLOCK) + c))
        for b in range(N_BRANCH) for c in range(parts)]
    return pl.pallas_call(
        functools.partial(_merge_kernel, parts=parts),
        out_shape=jax.ShapeDtypeStruct((m, d), F32),
        grid=(m // tm, nt),
        in_specs=[
            pl.BlockSpec((tm, d), row),
            pl.BlockSpec((1, d), lambda i, j: (0, 0)),
            pl.BlockSpec((tm, SWA_Q_DIM), row),
            pl.BlockSpec((tm, RWKV_DIM), row),
            pl.BlockSpec((tm, MEM_DIM), row),
            pl.BlockSpec((SWA_Q_DIM, tn), col),
            pl.BlockSpec((RWKV_DIM, tn), col),
            pl.BlockSpec((MEM_DIM, tn), col),
            pl.BlockSpec((tn, d), lambda i, j: (j, 0)),
        ] + gate_specs,
        out_specs=pl.BlockSpec((tm, d), row),
        scratch_shapes=[pltpu.VMEM((tm, d), BF16), pltpu.VMEM((tm, d), F32)],
        compiler_params=_cparams("parallel", "arbitrary"),
        name="merge",
    )(h, g, o_swa, o_rw, o_mem, wo_swa, wo_rw, wo_mem, w_out, *([w_in] * (N_BRANCH * parts)))


def _t5_bucket(dist):
    max_exact = N_BUCKETS // 2
    d = np.maximum(dist, 0)
    log_ratio = (np.log(np.maximum(d, 1).astype(np.float32) / np.float32(max_exact))
                 / np.float32(math.log(MAX_DISTANCE / max_exact)))
    large = np.minimum(max_exact + (log_ratio * (N_BUCKETS - max_exact)).astype(np.int32), N_BUCKETS - 1)
    return np.where(d < max_exact, d, large).astype(np.int32)


def _rel_bias(table, dist):
    onehot = np.eye(N_BUCKETS, dtype=np.float32)[_t5_bucket(dist).reshape(-1)]
    bias = jnp.einsum("nb,bh->hn", jnp.asarray(onehot), table, precision=lax.Precision.HIGHEST)
    return bias.reshape(SWA_HEADS, *dist.shape)


def _rwkv_branch(xr, shift0, s0, p, ones):
    b, t, _ = xr.shape
    flat = lambda z: z.reshape(b * t, z.shape[-1])
    if b == 1:
        o, s_new = _rwkv_prompt(flat(xr), shift0, s0[0], p, ones)
        return o, s_new[None]
    assert b == LANES, "short sequences are batched one per lane"
    r, w, k, v, kk, kka, g, bonus = _rwkv_prep(flat(xr), shift0, p, ones, seq=t)
    y, s_new = _rwkv_lanes(r, w, k, v, kk, kka, jnp.transpose(s0, (1, 2, 3, 0)), t=t)
    o = _rwkv_post(flat(jnp.transpose(y, (1, 0, 2))), bonus, g, p["ln_w"], p["ln_b"], ones)
    return o, jnp.transpose(s_new, (3, 0, 1, 2))


def kernel(x_prompt, mem_prompt, x_sample, cache_swa_k, cache_swa_v, state_rwkv, state_rwkv_shift, cache_mem_k, cache_mem_v, ffn1_norm, ffn1_wi, ffn1_wo, mix_norm, w_in, swa_sinks, rel_bias_table, rwkv_mu, rwkv_w0, rwkv_w_w2, rwkv_a0, rwkv_a_w2, rwkv_g_w2, rwkv_k_k, rwkv_k_a, rwkv_r_k, rwkv_ln_w, rwkv_ln_b, mem_norm, w_mem_kv, w_o_swa, w_o_rwkv, w_o_mem, w_out, ffn2_norm, ffn2_wi, ffn2_wo, final_norm):
    assert ffn1_wi.shape[0] == 1, "single-layer trunk"
    bp, tp, d = x_prompt.shape
    bs, ts, _ = x_sample.shape
    assert bp == 1
    row = lambda z: z.reshape(1, -1).astype(F32)

    g1, gm, g2, gf = row(ffn1_norm[0]), row(mix_norm[0]), row(ffn2_norm[0]), row(final_norm)
    rp = {
        "mu": row(rwkv_mu[0]), "w0": row(rwkv_w0[0]), "w_w2": rwkv_w_w2[0], "a0": row(rwkv_a0[0]),
        "a_w2": rwkv_a_w2[0], "g_w2": rwkv_g_w2[0], "k_k": row(rwkv_k_k[0]), "k_a": row(rwkv_k_a[0]),
        "r_k": row(rwkv_r_k[0]), "ln_w": row(rwkv_ln_w[0]), "ln_b": row(rwkv_ln_b[0]),
    }
    seg = np.arange(GROUP_W) // RWKV_HEAD_DIM
    ones = jnp.asarray(seg[:, None] == seg[None, :], dtype=BF16)
    sinks = swa_sinks[0].astype(F32)
    table = rel_bias_table.astype(F32)

    xp = x_prompt.reshape(tp, d)
    xs = x_sample.reshape(bs * ts, d)
    hs, wg1, wu1, wo1 = _ffn(xs, g1, ffn1_wi[0], ffn1_wo[0], gf, final_norm=False, emit_weights=True, tf=256)
    hp, wi2, wo2, w_in_b, wo_swa, wo_rw, wo_mem, w_out_b = _ffn(
        xp, g1, (wg1, wu1), wo1, gf, final_norm=False,
        riders=(ffn2_wi[0], ffn2_wo[0], w_in[0], w_o_swa[0], w_o_rwkv[0], w_o_mem[0], w_out[0]))
    wo_swa = wo_swa.reshape(SWA_KV_HEADS, SWA_GROUP, SWA_HEAD_DIM, d).transpose(1, 0, 2, 3).reshape(SWA_Q_DIM, d)
    w_q = w_in_b[:, :SWA_Q_DIM].reshape(d, SWA_KV_HEADS, SWA_GROUP, SWA_HEAD_DIM).transpose(0, 2, 1, 3).reshape(d, SWA_Q_DIM)
    qp, kvp, xrp, qmp = _inproj(hp, gm, w_q, w_in_b)
    qs, kvs, xrs, qms = _inproj(hs, gm, w_q, w_in_b)

    w = WINDOW
    dist_p = np.arange(w)[:, None] + w - np.arange(2 * w)[None, :]
    bias_p = _rel_bias(table, dist_p).reshape(SWA_KV_HEADS, SWA_GROUP, w, 2 * w).transpose(1, 0, 2, 3)
    bias_p = bias_p.reshape(SWA_GROUP, SWA_KV_HEADS * w, 2 * w)
    sink_p = jnp.repeat(sinks.reshape(SWA_KV_HEADS, SWA_GROUP).T, w, axis=1).reshape(SWA_GROUP, SWA_KV_HEADS * w, 1)
    o_swa_p = _swa_prompt(qp, kvp, bias_p, sink_p)

    wbuf = cache_swa_k.shape[2]
    dist_s = np.arange(ts)[:, None] + wbuf - np.arange(wbuf + ts)[None, :]
    bias_s = _rel_bias(table, dist_s).reshape(SWA_HEADS * ts, wbuf + ts)
    sink_rows = jnp.repeat(sinks, ts).reshape(SWA_HEADS * ts, 1)
    qs_gt = qs.reshape(bs, ts, SWA_GROUP, SWA_KV_DIM).transpose(0, 2, 1, 3).reshape(bs, SWA_GROUP * ts, SWA_KV_DIM)
    kbuf = cache_swa_k[0].reshape(bs, wbuf, SWA_KV_DIM).transpose(0, 2, 1)
    vbuf = cache_swa_v[0].reshape(bs, wbuf, SWA_KV_DIM).transpose(0, 2, 1)
    o_swa_s, knew_t, vnew_t = _swa_sample(qs_gt, kvs.reshape(bs, ts, 2 * SWA_KV_DIM), kbuf, vbuf,
                                          bias_s[:, :wbuf], bias_s[:, wbuf:], sink_rows)
    o_swa_s = o_swa_s.reshape(bs, SWA_GROUP, ts, SWA_KV_DIM).transpose(0, 2, 1, 3).reshape(bs * ts, SWA_Q_DIM)

    zero_shift = jnp.zeros((bp, 1, RWKV_IN), F32)
    zero_state = jnp.zeros((bp, RWKV_HEADS, RWKV_HEAD_DIM, RWKV_HEAD_DIM), F32)
    o_rw_p, state_p = _rwkv_branch(xrp.reshape(bp, tp, RWKV_IN), zero_shift, zero_state, rp, ones)
    o_rw_s, state_s = _rwkv_branch(xrs.reshape(bs, ts, RWKV_IN), state_rwkv_shift[0], state_rwkv[0], rp, ones)

    mkv = _norm_matmul(mem_prompt.reshape(N_MEM, d), row(mem_norm[0]), w_mem_kv[0].astype(BF16))
    o_mem_p = _mem_prompt(qmp, mkv)
    o_mem_s = _mem_sample(qms.reshape(bs, ts, MEM_DIM), cache_mem_k[0].reshape(bs, N_MEM * MEM_HEADS, MEM_HEAD_DIM),
                          cache_mem_v[0].reshape(bs, N_MEM * MEM_HEADS, MEM_HEAD_DIM)).reshape(bs * ts, MEM_DIM)

    hp = _merge(hp, gm, o_swa_p, o_rw_p, o_mem_p, w_in_b, wo_swa, wo_rw, wo_mem, w_out_b)
    hs = _merge(hs, gm, o_swa_s, o_rw_s, o_mem_s, w_in_b, wo_swa, wo_rw, wo_mem, w_out_b)
    y_prompt = _ffn(hp, g2, wi2, wo2, gf, final_norm=True).reshape(bp, tp, d)
    y_sample = _ffn(hs, g2, wi2, wo2, gf, final_norm=True).reshape(bs, ts, d)

    wp = min(w, tp)
    p_k = kvp[tp - wp:, :SWA_KV_DIM].reshape(1, bp, wp, SWA_KV_HEADS, SWA_HEAD_DIM)
    p_v = kvp[tp - wp:, SWA_KV_DIM:].reshape(1, bp, wp, SWA_KV_HEADS, SWA_HEAD_DIM)
    p_mk = mkv[:, :MEM_DIM].reshape(1, bp, N_MEM, MEM_HEADS, MEM_HEAD_DIM)
    p_mv = mkv[:, MEM_DIM:].reshape(1, bp, N_MEM, MEM_HEADS, MEM_HEAD_DIM)
    s_k = knew_t.transpose(0, 2, 1).reshape(1, bs, wbuf, SWA_KV_HEADS, SWA_HEAD_DIM)
    s_v = vnew_t.transpose(0, 2, 1).reshape(1, bs, wbuf, SWA_KV_HEADS, SWA_HEAD_DIM)
    return (y_prompt, y_sample,
            p_k, p_v, state_p[None], xrp[tp - 1:].reshape(1, bp, 1, RWKV_IN), p_mk, p_mv,
            s_k, s_v, state_s[None], xrs.reshape(bs, ts, RWKV_IN)[:, ts - 1:][None])
```

```python
import functools
import math

import jax
import jax.numpy as jnp
import numpy as np
from jax import lax
from jax.experimental import pallas as pl
from jax.experimental.pallas import tpu as pltpu

F32 = jnp.float32
BF16 = jnp.bfloat16

D_MODEL = 2048
D_FF = 5632
SWA_HEADS = 16
SWA_KV_HEADS = 4
SWA_GROUP = SWA_HEADS // SWA_KV_HEADS
SWA_HEAD_DIM = 64
SWA_Q_DIM = SWA_HEADS * SWA_HEAD_DIM
SWA_KV_DIM = SWA_KV_HEADS * SWA_HEAD_DIM
WINDOW = 128
N_BUCKETS = 32
MAX_DISTANCE = 128
RWKV_HEADS = 8
RWKV_HEAD_DIM = 64
RWKV_DIM = RWKV_HEADS * RWKV_HEAD_DIM
LORA_W = 64
LORA_A = 64
LORA_G = 128
RWKV_IN = 3 * RWKV_DIM + LORA_W + LORA_A + LORA_G
N_MEM = 256
MEM_HEADS = 4
MEM_HEAD_DIM = 128
MEM_DIM = MEM_HEADS * MEM_HEAD_DIM
N_BRANCH = 3
PROJ_DIM = SWA_Q_DIM + 2 * SWA_KV_DIM + RWKV_IN + MEM_DIM
NORM_EPS = 1e-6
GN_EPS = 64e-5
NEG_INF = -1e30

LANES = 128
SUBLANES = 8
VMEM_LIMIT = 56 * 1024 * 1024


def _cparams(*sem):
    return pltpu.CompilerParams(dimension_semantics=sem, vmem_limit_bytes=VMEM_LIMIT)


def _rms(x, g):
    return x * lax.rsqrt(jnp.mean(x * x, axis=-1, keepdims=True) + NORM_EPS) * g


def _dot(a, b):
    return jnp.dot(a, b, preferred_element_type=F32)


def _dot_nt(a, b):
    return lax.dot_general(a, b, (((1,), (1,)), ((), ())), preferred_element_type=F32)


def _dot_hi(a, b):
    return jnp.dot(a, b, preferred_element_type=F32, precision=lax.Precision.HIGHEST)


def _ffn_kernel(x_ref, g_ref, wg_ref, wu_ref, wo_ref, gf_ref, *rest, final_norm, n_riders, emit_weights):
    rider_in, (o_ref, *extra_out), (xn_ref, acc_ref) = rest[:n_riders], rest[n_riders:-2], rest[-2:]
    j = pl.program_id(1)
    if n_riders:
        @pl.when((pl.program_id(0) * pl.num_programs(1) + j) % RIDER_PERIOD == 0)
        def _():
            for src, dst in zip(rider_in, extra_out[:n_riders]):
                dst[...] = src[...].astype(BF16)

    @pl.when(j == 0)
    def _():
        xn_ref[...] = _rms(x_ref[...], g_ref[...]).astype(BF16)
        acc_ref[...] = jnp.zeros_like(acc_ref)

    wg, wu, wo = wg_ref[...], wu_ref[...], wo_ref[...]
    if emit_weights:
        wg, wu, wo = wg.astype(BF16), wu.astype(BF16), wo.astype(BF16)
        for dst, val in zip(extra_out[n_riders:], (wg, wu, wo)):
            dst[...] = val
    xn = xn_ref[...]
    gate = _dot(xn, wg)
    up = _dot(xn, wu)
    act = (gate * jax.nn.sigmoid(gate)) * up
    acc_ref[...] += _dot(act.astype(BF16), wo)

    @pl.when(j == pl.num_programs(1) - 1)
    def _():
        h = x_ref[...] + 0.5 * acc_ref[...]
        if final_norm:
            h = _rms(h, gf_ref[...])
        o_ref[...] = h


RIDER_PERIOD = 2


def _rider_tiling(shape, steps):
    rows, cols = shape
    best, best_score = (1, 1), (0, 0)
    for nr in range(1, rows // 16 + 1):
        if rows % nr or (rows // nr) % 16:
            continue
        for nc in range(1, cols // LANES + 1):
            if cols % nc or (cols // nc) % LANES or nr * nc > steps:
                continue
            score = (min(cols // nc, 1024), nr * nc)
            if score > best_score:
                best, best_score = (nr, nc), score
    return best


def _ffn(x, g, wi, wo, gf, *, final_norm, riders=(), emit_weights=False, tm=512, tf=512):
    m, d = x.shape
    dff = wo.shape[0]
    nf = dff // tf
    steps = (m // tm) * nf
    wg, wu = wi if isinstance(wi, tuple) else (wi, wi)
    up_off = 0 if isinstance(wi, tuple) else nf
    assert not emit_weights or m == tm
    weight_specs = [pl.BlockSpec((d, tf), lambda i, j: (0, j)), pl.BlockSpec((d, tf), lambda i, j: (0, j)),
                    pl.BlockSpec((tf, d), lambda i, j: (j, 0))]
    weight_out = [jax.ShapeDtypeStruct((d, dff), BF16)] * 2 + [jax.ShapeDtypeStruct((dff, d), BF16)]
    rider_specs = []
    for arr in riders:
        nr, nc = _rider_tiling(arr.shape, max(steps // RIDER_PERIOD, 1))
        br, bc = arr.shape[0] // nr, arr.shape[1] // nc
        rider_specs.append(pl.BlockSpec((br, bc), functools.partial(
            lambda i, j, nc, last: (jnp.minimum((i * nf + j) // RIDER_PERIOD, last) // nc,
                                    jnp.minimum((i * nf + j) // RIDER_PERIOD, last) % nc),
            nc=nc, last=nr * nc - 1)))
    out = pl.pallas_call(
        functools.partial(_ffn_kernel, final_norm=final_norm, n_riders=len(riders), emit_weights=emit_weights),
        out_shape=([jax.ShapeDtypeStruct((m, d), F32)] + [jax.ShapeDtypeStruct(arr.shape, BF16) for arr in riders]
                   + (weight_out if emit_weights else [])),
        grid=(m // tm, nf),
        in_specs=[
            pl.BlockSpec((tm, d), lambda i, j: (i, 0)),
            pl.BlockSpec((1, d), lambda i, j: (0, 0)),
            pl.BlockSpec((d, tf), lambda i, j: (0, j)),
            pl.BlockSpec((d, tf), lambda i, j: (0, j + up_off)),
            pl.BlockSpec((tf, d), lambda i, j: (j, 0)),
            pl.BlockSpec((1, d), lambda i, j: (0, 0)),
        ] + rider_specs,
        out_specs=([pl.BlockSpec((tm, d), lambda i, j: (i, 0))] + rider_specs
                   + (weight_specs if emit_weights else [])),
        scratch_shapes=[pltpu.VMEM((tm, d), BF16), pltpu.VMEM((tm, d), F32)],
        compiler_params=_cparams("arbitrary", "arbitrary"),
        name="ffn_final" if final_norm else "ffn",
    )(x, g, wg, wu, wo, gf, *riders)
    return out if (riders or emit_weights) else out[0]


def _inproj_kernel(h_ref, g_ref, wq_ref, w_ref, q_ref, kv_ref, xr_ref, qm_ref):
    u = _rms(h_ref[...], g_ref[...]).astype(BF16)
    c0, c1, c2 = SWA_Q_DIM, SWA_Q_DIM + 2 * SWA_KV_DIM, SWA_Q_DIM + 2 * SWA_KV_DIM + RWKV_IN
    q_ref[...] = _dot(u, wq_ref[...]).astype(BF16)
    kv_ref[...] = _dot(u, w_ref[:, c0:c1])
    xr_ref[...] = _dot(u, w_ref[:, c1:c2])
    qm_ref[...] = _dot(u, w_ref[:, c2:PROJ_DIM]).astype(BF16)


def _inproj(h, g, wq, w, *, tm=256):
    m, d = h.shape
    row = lambda i: (i, 0)
    return pl.pallas_call(
        _inproj_kernel,
        out_shape=(
            jax.ShapeDtypeStruct((m, SWA_Q_DIM), BF16),
            jax.ShapeDtypeStruct((m, 2 * SWA_KV_DIM), F32),
            jax.ShapeDtypeStruct((m, RWKV_IN), F32),
            jax.ShapeDtypeStruct((m, MEM_DIM), BF16),
        ),
        grid=(m // tm,),
        in_specs=[
            pl.BlockSpec((tm, d), row),
            pl.BlockSpec((1, d), lambda i: (0, 0)),
            pl.BlockSpec((d, SWA_Q_DIM), lambda i: (0, 0), pipeline_mode=pl.Buffered(1)),
            pl.BlockSpec((d, PROJ_DIM), lambda i: (0, 0), pipeline_mode=pl.Buffered(1)),
        ],
        out_specs=(
            pl.BlockSpec((tm, SWA_Q_DIM), row),
            pl.BlockSpec((tm, 2 * SWA_KV_DIM), row),
            pl.BlockSpec((tm, RWKV_IN), row),
            pl.BlockSpec((tm, MEM_DIM), row),
        ),
        compiler_params=_cparams("parallel"),
        name="inproj",
    )(h, g, wq, w)


def _norm_matmul_kernel(x_ref, g_ref, w_ref, o_ref):
    o_ref[...] = _dot(_rms(x_ref[...], g_ref[...]).astype(BF16), w_ref[...])


def _norm_matmul(x, g, w, *, tn=512):
    m, d = x.shape
    n = w.shape[1]
    return pl.pallas_call(
        _norm_matmul_kernel,
        out_shape=jax.ShapeDtypeStruct((m, n), F32),
        grid=(n // tn,),
        in_specs=[
            pl.BlockSpec((m, d), lambda j: (0, 0)),
            pl.BlockSpec((1, d), lambda j: (0, 0)),
            pl.BlockSpec((d, tn), lambda j: (0, j)),
        ],
        out_specs=pl.BlockSpec((m, tn), lambda j: (0, j)),
        compiler_params=_cparams("parallel"),
        name="norm_matmul",
    )(x, g, w)


def _swa_prompt_kernel(q_ref, kvc_ref, kvp_ref, bias_ref, sink_ref, o_ref, *, nq):
    i = pl.program_id(0)
    w = WINDOW
    rows = SWA_KV_HEADS * w
    kv_blocks = [kvp_ref[...]] + [kvc_ref[s * w:(s + 1) * w, :] for s in range(nq)]
    k_blocks = [x[:, 0:SWA_KV_DIM].astype(BF16) for x in kv_blocks]
    v_blocks = [x[:, SWA_KV_DIM:].astype(BF16) for x in kv_blocks]
    qpos = lax.broadcasted_iota(jnp.int32, (rows, 2 * w), 0) % w
    col = lax.broadcasted_iota(jnp.int32, (rows, 2 * w), 1)
    dist = qpos + w - col
    in_window = (dist >= 0) & (dist < w)
    first = in_window & ((col >= w) | (i > 0))
    lane_head = lax.broadcasted_iota(jnp.int32, (w, SWA_KV_DIM), 1) // SWA_HEAD_DIM
    scale = SWA_HEAD_DIM ** -0.5
    pairs = [(s, g) for s in range(nq) for g in range(SWA_GROUP)]
    k = [jnp.concatenate(k_blocks[s:s + 2], axis=0) for s in range(nq)]
    v = [jnp.concatenate(v_blocks[s:s + 2], axis=0) for s in range(nq)]
    logits = []
    for s, g in pairs:
        qg = q_ref[s * w:(s + 1) * w, g * SWA_KV_DIM:(g + 1) * SWA_KV_DIM].astype(F32) * scale
        qs = jnp.concatenate([jnp.where(lane_head == kvh, qg, 0.0) for kvh in range(SWA_KV_HEADS)], axis=0)
        lg = _dot_nt(qs.astype(BF16), k[s])
        logits.append(jnp.where(first if s == 0 else in_window, lg + bias_ref[g], NEG_INF))
    sink = [sink_ref[g] for _, g in pairs]
    m = [jnp.maximum(jnp.max(x, axis=-1, keepdims=True), sk) for x, sk in zip(logits, sink)]
    p = [jnp.exp(x - mi) for x, mi in zip(logits, m)]
    inv = [1.0 / (jnp.sum(pi, axis=-1, keepdims=True) + jnp.exp(sk - mi)) for pi, sk, mi in zip(p, sink, m)]
    ov = [_dot((pi * ii).astype(BF16), v[s]) for (s, _), pi, ii in zip(pairs, p, inv)]
    for (s, g), o in zip(pairs, ov):
        og = jnp.zeros((w, SWA_KV_DIM), F32)
        for kvh in range(SWA_KV_HEADS):
            og = jnp.where(lane_head == kvh, o[kvh * w:(kvh + 1) * w], og)
        o_ref[s * w:(s + 1) * w, g * SWA_KV_DIM:(g + 1) * SWA_KV_DIM] = og.astype(BF16)


def _swa_prompt(q, kv, bias, sink_rows, *, nq=8):
    t = q.shape[0]
    w = WINDOW
    rows = SWA_KV_HEADS * w
    return pl.pallas_call(
        functools.partial(_swa_prompt_kernel, nq=nq),
        out_shape=jax.ShapeDtypeStruct((t, SWA_Q_DIM), BF16),
        grid=(t // (nq * w),),
        in_specs=[
            pl.BlockSpec((nq * w, SWA_Q_DIM), lambda i: (i, 0)),
            pl.BlockSpec((nq * w, 2 * SWA_KV_DIM), lambda i: (i, 0)),
            pl.BlockSpec((w, 2 * SWA_KV_DIM), lambda i: (jnp.maximum(i * nq - 1, 0), 0)),
            pl.BlockSpec((SWA_GROUP, rows, 2 * w), lambda i: (0, 0, 0)),
            pl.BlockSpec((SWA_GROUP, rows, 1), lambda i: (0, 0, 0)),
        ],
        out_specs=pl.BlockSpec((nq * w, SWA_Q_DIM), lambda i: (i, 0)),
        compiler_params=_cparams("parallel"),
        name="swa_prompt",
    )(q, kv, kv, bias, sink_rows)


def _swa_sample_kernel(q_ref, kvn_ref, kb_ref, vb_ref, bias_b_ref, bias_n_ref, sink_ref, o_ref, ko_ref, vo_ref,
                       *, bb, t):
    gt = SWA_GROUP * t
    rows = SWA_KV_HEADS * gt
    w = kb_ref.shape[2]
    scale = SWA_HEAD_DIM ** -0.5
    lane_head = lax.broadcasted_iota(jnp.int32, (gt, SWA_KV_DIM), 1) // SWA_HEAD_DIM
    tok = lax.broadcasted_iota(jnp.int32, (rows, w), 0) % t
    keyj = lax.broadcasted_iota(jnp.int32, (rows, w), 1)
    valid_b = (tok + w - keyj) < WINDOW
    tok_n = lax.broadcasted_iota(jnp.int32, (rows, 1), 0) % t
    sink = sink_ref[...]
    bs = range(bb)
    toks = range(t)
    qall = [jnp.concatenate([jnp.where(lane_head == kvh, q_ref[b].astype(F32), 0.0) for kvh in range(SWA_KV_HEADS)],
                            axis=0) for b in bs]
    kvn = [kvn_ref[b] for b in bs]
    lb = [_dot(qall[b].astype(BF16), kb_ref[b].astype(BF16)) for b in bs]
    lb = [jnp.where(valid_b, lb[b] * scale + bias_b_ref[...], NEG_INF) for b in bs]
    ln = [[jnp.sum(qall[b] * kvn[b][j:j + 1, 0:SWA_KV_DIM], axis=-1, keepdims=True) for j in toks] for b in bs]
    ln = [[jnp.where(tok_n >= j, ln[b][j] * scale + bias_n_ref[:, j:j + 1], NEG_INF) for j in toks] for b in bs]
    m = [jnp.maximum(jnp.max(lb[b], axis=-1, keepdims=True), sink) for b in bs]
    m = [functools.reduce(jnp.maximum, ln[b], m[b]) for b in bs]
    pb = [jnp.exp(lb[b] - m[b]) for b in bs]
    pn = [[jnp.exp(ln[b][j] - m[b]) for j in toks] for b in bs]
    denom = [jnp.sum(pb[b], axis=-1, keepdims=True) + jnp.exp(sink - m[b]) for b in bs]
    inv = [1.0 / functools.reduce(jnp.add, pn[b], denom[b]) for b in bs]
    oall = [_dot_nt((pb[b] * inv[b]).astype(BF16), vb_ref[b].astype(BF16)) for b in bs]
    for b in bs:
        ob = oall[b]
        for j in toks:
            ob = ob + (pn[b][j] * inv[b]) * kvn[b][j:j + 1, SWA_KV_DIM:]
        og = jnp.zeros((gt, SWA_KV_DIM), F32)
        for kvh in range(SWA_KV_HEADS):
            og = jnp.where(lane_head == kvh, ob[kvh * gt:(kvh + 1) * gt], og)
        o_ref[b] = og.astype(BF16)
    pos = lax.broadcasted_iota(jnp.int32, (SWA_KV_DIM, w), 1)
    pad = jnp.zeros((w - SUBLANES, 2 * SWA_KV_DIM), F32)
    row8 = lax.broadcasted_iota(jnp.int32, (SUBLANES, 2 * SWA_KV_DIM), 0)
    for b in bs:
        last8 = jnp.zeros((SUBLANES, 2 * SWA_KV_DIM), F32)
        for j in toks:
            last8 = jnp.where(row8 == SUBLANES - t + j, kvn[b][j:j + 1], last8)
        tail_t = jnp.concatenate([pad, last8], axis=0).T
        ko_ref[b] = jnp.where(pos >= w - t, tail_t[:SWA_KV_DIM], pltpu.roll(kb_ref[b], w - t, axis=1))
        vo_ref[b] = jnp.where(pos >= w - t, tail_t[SWA_KV_DIM:], pltpu.roll(vb_ref[b], w - t, axis=1))


def _swa_sample(q, kvn, kbuf, vbuf, bias_b, bias_n, sink_rows, *, bb=8):
    b, gt, _ = q.shape
    t = kvn.shape[1]
    w = kbuf.shape[2]
    rows = SWA_KV_HEADS * gt
    blk = lambda i: (i, 0, 0)
    const = lambda i: (0, 0)
    cache = pl.BlockSpec((bb, SWA_KV_DIM, w), blk)
    return pl.pallas_call(
        functools.partial(_swa_sample_kernel, bb=bb, t=t),
        out_shape=(jax.ShapeDtypeStruct((b, gt, SWA_KV_DIM), BF16),
                   jax.ShapeDtypeStruct(kbuf.shape, F32), jax.ShapeDtypeStruct(vbuf.shape, F32)),
        grid=(b // bb,),
        in_specs=[
            pl.BlockSpec((bb, gt, SWA_KV_DIM), blk),
            pl.BlockSpec((bb, t, 2 * SWA_KV_DIM), blk),
            cache, cache,
            pl.BlockSpec((rows, w), const),
            pl.BlockSpec((rows, t), const),
            pl.BlockSpec((rows, 1), const),
        ],
        out_specs=(pl.BlockSpec((bb, gt, SWA_KV_DIM), blk), cache, cache),
        compiler_params=_cparams("parallel"),
        name="swa_sample",
    )(q, kvn, kbuf, vbuf, bias_b, bias_n, sink_rows)


def _mem_heads(q, mk, mv):
    scale = MEM_HEAD_DIM ** -0.5
    cols = [slice(h * MEM_HEAD_DIM, (h + 1) * MEM_HEAD_DIM) for h in range(MEM_HEADS)]
    x = [_dot_nt(q[:, sl], mk[:, sl]) * scale for sl in cols]
    m = [jnp.max(xi, axis=-1, keepdims=True) for xi in x]
    p = [jnp.exp(xi - mi) for xi, mi in zip(x, m)]
    inv = [1.0 / jnp.sum(pi, axis=-1, keepdims=True) for pi in p]
    return jnp.concatenate([_dot((pi * ii).astype(BF16), mv[:, sl]) for pi, ii, sl in zip(p, inv, cols)], axis=-1)


def _mem_prompt_kernel(q_ref, mk_ref, mv_ref, o_ref):
    o_ref[...] = _mem_heads(q_ref[...], mk_ref[...].astype(BF16), mv_ref[...].astype(BF16)).astype(BF16)


def _mem_prompt(q, mkv, *, tm=512):
    m = q.shape[0]
    return pl.pallas_call(
        _mem_prompt_kernel,
        out_shape=jax.ShapeDtypeStruct((m, MEM_DIM), BF16),
        grid=(m // tm,),
        in_specs=[
            pl.BlockSpec((tm, MEM_DIM), lambda i: (i, 0)),
            pl.BlockSpec((N_MEM, MEM_DIM), lambda i: (0, 0)),
            pl.BlockSpec((N_MEM, MEM_DIM), lambda i: (0, 1)),
        ],
        out_specs=pl.BlockSpec((tm, MEM_DIM), lambda i: (i, 0)),
        compiler_params=_cparams("parallel"),
        name="mem_prompt",
    )(q, mkv, mkv)


def _mem_sample_kernel(q_ref, mk_ref, mv_ref, o_ref, *, bb):
    scale = MEM_HEAD_DIM ** -0.5
    pairs = [(b, h) for b in range(bb) for h in range(MEM_HEADS)]
    rows = lambda h: pl.ds(h, N_MEM, stride=MEM_HEADS)
    cols = lambda h: slice(h * MEM_HEAD_DIM, (h + 1) * MEM_HEAD_DIM)
    q = [q_ref[b] for b in range(bb)]
    x = [_dot_nt(q[b][:, cols(h)], mk_ref[b, rows(h), :].astype(BF16)) * scale for b, h in pairs]
    m = [jnp.max(xi, axis=-1, keepdims=True) for xi in x]
    p = [jnp.exp(xi - mi) for xi, mi in zip(x, m)]
    inv = [1.0 / jnp.sum(pi, axis=-1, keepdims=True) for pi in p]
    o = [_dot((pi * ii).astype(BF16), mv_ref[b, rows(h), :].astype(BF16)) for (b, h), pi, ii in zip(pairs, p, inv)]
    for b in range(bb):
        o_ref[b] = jnp.concatenate(o[b * MEM_HEADS:(b + 1) * MEM_HEADS], axis=-1).astype(BF16)


def _mem_sample(q, mk, mv, *, bb=8):
    b, t, _ = q.shape
    blk = lambda i: (i, 0, 0)
    return pl.pallas_call(
        functools.partial(_mem_sample_kernel, bb=bb),
        out_shape=jax.ShapeDtypeStruct((b, t, MEM_DIM), BF16),
        grid=(b // bb,),
        in_specs=[
            pl.BlockSpec((bb, t, MEM_DIM), blk),
            pl.BlockSpec((bb, N_MEM * MEM_HEADS, MEM_HEAD_DIM), blk),
            pl.BlockSpec((bb, N_MEM * MEM_HEADS, MEM_HEAD_DIM), blk),
        ],
        out_specs=pl.BlockSpec((bb, t, MEM_DIM), blk),
        compiler_params=_cparams("parallel"),
        name="mem_sample",
    )(q, mk, mv)


def _head_sum(x, ones_ref):
    hi = x.astype(BF16)
    lo = (x - hi.astype(F32)).astype(BF16)
    ones = ones_ref[...]
    w = ones.shape[0]
    return jnp.concatenate([_dot(hi[:, c:c + w], ones) + _dot(lo[:, c:c + w], ones) for c in range(0, x.shape[1], w)],
                           axis=1)


def _token_shift(x_ref, pre_ref, start_ref, mu_ref, *, seq, tm, tile):
    x = x_ref[...]
    row = lax.broadcasted_iota(jnp.int32, x.shape, 0)
    shifted = pltpu.roll(x, 1, axis=0)
    if seq >= tm:
        is_start = (tile * tm) % seq == 0
        first_prev = jnp.where(is_start, start_ref[0], pre_ref[SUBLANES - 1:SUBLANES, :])
        prev = jnp.where(row == 0, first_prev, shifted)
    else:
        prev = jnp.where(row % seq == 0, start_ref[...], shifted)
    return x + mu_ref[...] * (prev - x)


def _rwkv_features(xs, w0_ref, ww2_ref, a0_ref, aw2_ref, gw2_ref, kk_ref, ka_ref, rk_ref, ones_ref):
    d = RWKV_DIM
    r = xs[:, 0:d]
    k = xs[:, d:2 * d]
    v = xs[:, 2 * d:3 * d]
    lw = xs[:, 3 * d:3 * d + LORA_W]
    la = xs[:, 3 * d + LORA_W:3 * d + LORA_W + LORA_A]
    lg = xs[:, 3 * d + LORA_W + LORA_A:]
    wpre = w0_ref[...] + _dot_hi(jnp.tanh(lw), ww2_ref[...])
    w = -jax.nn.softplus(-wpre) - 0.5
    log_decay = -jnp.exp(w)
    a = jax.nn.sigmoid(a0_ref[...] + _dot_hi(la, aw2_ref[...]))
    g = _dot_hi(jax.nn.sigmoid(lg), gw2_ref[...])
    kk = k * kk_ref[...]
    kk = kk / jnp.maximum(jnp.sqrt(_head_sum(kk * kk, ones_ref)), 1e-12)
    kh = k * (1.0 + (a - 1.0) * ka_ref[...])
    bonus = _head_sum(r * kh * rk_ref[...], ones_ref) * v
    return r, log_decay, kh, v, kk, kk * a, g, bonus


def _rwkv_prep_kernel(x_ref, pre_ref, start_ref, mu_ref, w0_ref, ww2_ref, a0_ref, aw2_ref, gw2_ref, kk_ref, ka_ref,
                      rk_ref, ones_ref, *outs, seq, tm):
    xs = _token_shift(x_ref, pre_ref, start_ref, mu_ref, seq=seq, tm=tm, tile=pl.program_id(0))
    feats = _rwkv_features(xs, w0_ref, ww2_ref, a0_ref, aw2_ref, gw2_ref, kk_ref, ka_ref, rk_ref, ones_ref)
    for o_ref, val in zip(outs, feats):
        o_ref[...] = val


def _rwkv_prep(x, shift0, p, ones, *, seq, tm=256):
    m = x.shape[0]
    row = lambda i: (i, 0)
    const = lambda i: (0, 0)
    vec = lambda n: pl.BlockSpec((1, n), const)
    out = jax.ShapeDtypeStruct((m, RWKV_DIM), F32)
    if seq >= tm:
        assert seq % tm == 0
        start = shift0
        start_spec = pl.BlockSpec((1, 1, RWKV_IN), lambda i: ((i * tm) // seq, 0, 0))
    else:
        assert tm % seq == 0
        start = jnp.repeat(shift0[:, 0], seq, axis=0)
        start_spec = pl.BlockSpec((tm, RWKV_IN), row)
    pre_blocks = tm // SUBLANES
    return pl.pallas_call(
        functools.partial(_rwkv_prep_kernel, seq=seq, tm=tm),
        out_shape=(out,) * 8,
        grid=(m // tm,),
        in_specs=[
            pl.BlockSpec((tm, RWKV_IN), row),
            pl.BlockSpec((SUBLANES, RWKV_IN), lambda i: (jnp.maximum(i * pre_blocks - 1, 0), 0)),
            start_spec,
            vec(RWKV_IN), vec(RWKV_DIM),
            pl.BlockSpec((LORA_W, RWKV_DIM), const),
            vec(RWKV_DIM),
            pl.BlockSpec((LORA_A, RWKV_DIM), const),
            pl.BlockSpec((LORA_G, RWKV_DIM), const),
            vec(RWKV_DIM), vec(RWKV_DIM), vec(RWKV_DIM),
            pl.BlockSpec(ones.shape, const),
        ],
        out_specs=(pl.BlockSpec((tm, RWKV_DIM), row),) * 8,
        compiler_params=_cparams("parallel"),
        name="rwkv_prep",
    )(x, x, start, p["mu"], p["w0"], p["w_w2"], p["a0"], p["a_w2"], p["g_w2"], p["k_k"], p["k_a"], p["r_k"], ones)


def _rwkv_lanes_kernel(r_ref, lw_ref, k_ref, v_ref, kk_ref, kka_ref, s_ref, y_ref, so_ref, v_scr, y_scr, *, t):
    n = RWKV_HEAD_DIM
    nb = s_ref.shape[-1]
    heads = range(2)

    def token_major(ref, j):
        return ref[pl.ds(j, nb, stride=t), :].T

    for j in range(t):
        v_scr[...] = token_major(v_ref, j)
        r_t, k_t, kk_t, kka_t = (token_major(ref, j) for ref in (r_ref, k_ref, kk_ref, kka_ref))
        w_t = jnp.exp(token_major(lw_ref, j))
        src = s_ref if j == 0 else so_ref

        def value_group(g, carry):
            rows = pl.multiple_of(g * SUBLANES, SUBLANES)
            ys = [[] for _ in heads]
            vg = [v_scr[pl.ds(h * n + rows, SUBLANES), :] for h in heads]
            for i in range(SUBLANES):
                for h in heads:
                    f = slice(h * n, (h + 1) * n)
                    s = src[h, rows + i]
                    sa = jnp.sum(s * (-kk_t[f]), axis=0, keepdims=True)
                    s = s * w_t[f] + sa * kka_t[f] + vg[h][i:i + 1] * k_t[f]
                    so_ref[h, rows + i] = s
                    ys[h].append(jnp.sum(s * r_t[f], axis=0, keepdims=True))
            for h in heads:
                y_scr[pl.ds(h * n + rows, SUBLANES), :] = jnp.concatenate(ys[h], axis=0)
            return carry

        lax.fori_loop(0, n // SUBLANES, value_group, 0)
        y_ref[j] = y_scr[...].T


def _rwkv_lanes(r, lw, k, v, kk, kka, s0, *, t):
    m, d = r.shape
    nb = m // t
    n = RWKV_HEAD_DIM
    assert nb == LANES, "one batch per lane"
    tok = pl.BlockSpec((m, LANES), lambda p: (0, p))
    st = pl.BlockSpec((2, n, n, nb), lambda p: (p, 0, 0, 0))
    return pl.pallas_call(
        functools.partial(_rwkv_lanes_kernel, t=t),
        out_shape=(jax.ShapeDtypeStruct((t, nb, d), F32), jax.ShapeDtypeStruct((RWKV_HEADS, n, n, nb), F32)),
        grid=(RWKV_HEADS // 2,),
        in_specs=[tok] * 6 + [st],
        out_specs=(pl.BlockSpec((t, nb, LANES), lambda p: (0, 0, p)), st),
        scratch_shapes=[pltpu.VMEM((LANES, nb), F32), pltpu.VMEM((LANES, nb), F32)],
        compiler_params=_cparams("parallel"),
        name="rwkv_lanes",
    )(r, lw, k, v, kk, kka, s0)


CHUNK = 64
GROUP_HEADS = 4
GROUP_W = GROUP_HEADS * RWKV_HEAD_DIM
N_GROUPS = RWKV_HEADS // GROUP_HEADS
(MASK_SAME, MASK_STRICT, MASK_INCL, MASK_LEVEL0) = (0, 1, 2, 3)
N_LEVELS = int(math.log2(CHUNK))


def _chunk_masks():
    i = np.arange(GROUP_W)
    same = (i[:, None] // CHUNK) == (i[None, :] // CHUNK)
    masks = [same, same & (i[None, :] < i[:, None]), same & (i[None, :] <= i[:, None])]
    for lvl in range(N_LEVELS):
        m = 1 << lvl
        masks.append(((i[:, None] // (2 * m)) == (i[None, :] // (2 * m))) & ((i[:, None] // m) != (i[None, :] // m))
                     & (i[None, :] < i[:, None]))
    return np.stack(masks).astype(np.float32)


def _rwkv_prompt_kernel(x_ref, pre_ref, start_ref, mu_ref, w0_ref, ww2_ref, a0_ref, aw2_ref, gw2_ref, kk_p_ref,
                        ka_ref, rk_ref, ones_ref, lnw_ref, lnb_ref, st0_ref, tri_ref, eye_ref, mask_ref,
                        o_ref, sto_ref, st_scr, xs_ref, y_ref, r_set, lw_set, k_set, v_set, kk_set, kka_set, g_set,
                        bonus_set, *, seq, tm, n_tiles):
    step = pl.program_id(0)
    cur = (step + 1) % 2
    nxt = step % 2
    sets = (r_set, lw_set, k_set, v_set, kk_set, kka_set, g_set, bonus_set)
    r_ref, lw_ref, k_ref, v_ref, kk_ref, kka_ref = (s.at[cur] for s in sets[:6])

    @pl.when(step == 0)
    def _():
        st_scr[...] = st0_ref[...]
        for s in sets:
            s[1] = jnp.zeros(s.shape[1:], F32)

    xs_ref[...] = _token_shift(x_ref, pre_ref, start_ref, mu_ref, seq=seq, tm=tm, tile=jnp.minimum(step, n_tiles - 1))
    n_chunks = tm // CHUNK

    piece = 2 * CHUNK
    def features(c):
        rows = slice(c * piece, (c + 1) * piece)
        feats = _rwkv_features(xs_ref[rows, :], w0_ref, ww2_ref, a0_ref, aw2_ref, gw2_ref, kk_p_ref, ka_ref, rk_ref,
                               ones_ref)
        for s, val in zip(sets, feats):
            s[nxt, rows, :] = val

    eye = eye_ref[...]
    tri = tri_ref[...]
    tile_rows = lambda x: jnp.concatenate([x] * GROUP_HEADS, axis=0)
    block_diag = lambda x: (tile_rows(x) * mask_ref[MASK_SAME]).astype(BF16)

    chains = [(slice(c * CHUNK, (c + 1) * CHUNK), slice(g * GROUP_W, (g + 1) * GROUP_W))
              for c in range(n_chunks) for g in range(N_GROUPS)]
    each = lambda f, *cols: [f(*args) for args in zip(*cols)]
    same, strict, incl = mask_ref[MASK_SAME], mask_ref[MASK_STRICT], mask_ref[MASK_INCL]

    def cum_decay(lw):
        h1 = lw.astype(BF16)
        r1 = lw - h1.astype(F32)
        h2 = r1.astype(BF16)
        h3 = (r1 - h2.astype(F32)).astype(BF16)
        return _dot(tri, h1) + _dot(tri, h2) + _dot(tri, h3)

    lw = [lw_ref[rows, sl] for rows, sl in chains]
    kka = [kka_ref[rows, sl] for rows, sl in chains]
    k = [k_ref[rows, sl] for rows, sl in chains]
    cum = each(cum_decay, lw)
    cum_last = each(lambda c: c[CHUNK - 1:CHUNK, :], cum)
    p_inv = each(lambda c: jnp.exp(-c), cum)
    p_tail = each(lambda c, cl: jnp.exp(cl - c), cum, cum_last)
    a_bd = [block_diag(-kk_ref[rows, sl] * jnp.exp(c - l)) for (rows, sl), c, l in zip(chains, cum, lw)]
    r_f = [tile_rows(r_ref[rows, sl] * jnp.exp(c)) * same for (rows, sl), c in zip(chains, cum)]
    r_bd = each(lambda x: x.astype(BF16), r_f)
    v_bd = [block_diag(v_ref[rows, sl]) for rows, sl in chains]
    b_rep = each(lambda x, p: tile_rows((x * p).astype(BF16)), kka, p_inv)
    k_rep = each(lambda x, p: tile_rows((x * p).astype(BF16)), k, p_inv)
    bh_rep = each(lambda x, p: tile_rows(x * p), kka, p_tail)
    kh_rep = each(lambda x, p: tile_rows(x * p), k, p_tail)

    l_ab_f = each(lambda a, b: _dot_nt(a, b) * strict, a_bd, b_rep)
    l_ab = each(lambda x: x.astype(BF16), l_ab_f)
    l_ak = each(lambda a, b: (_dot_nt(a, b) * strict).astype(BF16), a_bd, k_rep)
    m_rb = each(lambda a, b: (_dot_nt(a, b) * incl).astype(BF16), r_bd, b_rep)
    m_rk = each(lambda a, b: (_dot_nt(a, b) * incl).astype(BF16), r_bd, k_rep)
    bh_t = each(lambda x: (x.T * same).astype(BF16), bh_rep)
    kh_t = each(lambda x: (x.T * same).astype(BF16), kh_rep)

    pending = list(range(tm // piece))

    def next_features():
        if pending:
            features(pending.pop(0))

    d = each(lambda l: eye + l * mask_ref[MASK_LEVEL0], l_ab_f)
    for lvl in range(1, N_LEVELS):
        d_b = each(lambda x: x.astype(BF16), d)
        x = each(lambda l, db: (_dot(l, db) * mask_ref[MASK_LEVEL0 + lvl]).astype(BF16), l_ab, d_b)
        next_features()
        d = each(lambda dd, db, xx: dd + _dot(db, xx), d, d_b, x)
    t_b = each(lambda x: x.astype(BF16), d)
    while pending:
        next_features()

    wm = each(lambda a, b, vv: _dot(jnp.concatenate([a, b], axis=0), vv), l_ak, m_rk, v_bd)
    twa = each(lambda t, w, a: _dot(t, jnp.concatenate([w[:GROUP_W].astype(BF16), a], axis=1)).astype(BF16),
               t_b, wm, a_bd)
    ry = each(_dot, m_rb, twa)
    mn = each(_dot, bh_t, twa)
    khv = each(_dot, kh_t, v_bd)
    y0 = each(lambda a, w: a[:, :GROUP_W] + w[GROUP_W:], ry, wm)
    n_x = each(lambda a, b: a[:, :GROUP_W] + b, mn, khv)
    mr = each(lambda a, cl, rf, b: jnp.concatenate(
        [(eye * jnp.exp(cl) + a[:, GROUP_W:]).astype(BF16), (rf + b[:, GROUP_W:]).astype(BF16)], axis=0),
        mn, cum_last, r_f, ry)

    st = [st_scr[g] for g in range(N_GROUPS)]
    for i, (rows, sl) in enumerate(chains):
        g = i % N_GROUPS
        ys = _dot(mr[i], st[g].astype(BF16))
        st[g] = ys[:GROUP_W] + n_x[i]
        y_bd = ys[GROUP_W:] + y0[i]
        y = y_bd[0:CHUNK]
        for h in range(1, GROUP_HEADS):
            y = y + y_bd[h * CHUNK:(h + 1) * CHUNK]
        y_ref[rows, sl] = y
    for g in range(N_GROUPS):
        st_scr[g] = st[g]
    o_ref[...] = _rwkv_output(y_ref[...], bonus_set[cur], g_set[cur], lnw_ref, lnb_ref, ones_ref)

    @pl.when(step == pl.num_programs(0) - 1)
    def _():
        sto_ref[...] = st_scr[...]


def _rwkv_prompt(x, shift0, s0, p, ones, *, chunks_per_step=4):
    t = x.shape[0]
    d = RWKV_DIM
    n = RWKV_HEAD_DIM
    tt = CHUNK * chunks_per_step
    assert CHUNK == n and t % tt == 0
    st0 = jnp.einsum("ghvk,hj->ghkjv", s0.reshape(N_GROUPS, GROUP_HEADS, n, n), jnp.eye(GROUP_HEADS, dtype=F32))
    st0 = st0.reshape(N_GROUPS, GROUP_W, GROUP_W)
    tri = jnp.asarray(np.tril(np.ones((CHUNK, CHUNK), np.float32)), BF16)
    eye = jnp.eye(GROUP_W, dtype=F32)
    masks = jnp.asarray(_chunk_masks())
    const = lambda c: (0, 0)
    vec = lambda width: pl.BlockSpec((1, width), const)
    st_spec = pl.BlockSpec((N_GROUPS, GROUP_W, GROUP_W), lambda c: (0, 0, 0))
    feature_set = pltpu.VMEM((2, tt, d), F32)
    pre_blocks = tt // SUBLANES
    n_tiles = t // tt
    fill = lambda c: jnp.minimum(c, n_tiles - 1)
    o, st = pl.pallas_call(
        functools.partial(_rwkv_prompt_kernel, seq=t, tm=tt, n_tiles=n_tiles),
        out_shape=(jax.ShapeDtypeStruct((t, d), BF16), jax.ShapeDtypeStruct((N_GROUPS, GROUP_W, GROUP_W), F32)),
        grid=(n_tiles + 1,),
        in_specs=[
            pl.BlockSpec((tt, RWKV_IN), lambda c: (fill(c), 0)),
            pl.BlockSpec((SUBLANES, RWKV_IN), lambda c: (jnp.maximum(fill(c) * pre_blocks - 1, 0), 0)),
            pl.BlockSpec((1, 1, RWKV_IN), lambda c: (0, 0, 0)),
            vec(RWKV_IN), vec(d),
            pl.BlockSpec((LORA_W, d), const),
            vec(d),
            pl.BlockSpec((LORA_A, d), const),
            pl.BlockSpec((LORA_G, d), const),
            vec(d), vec(d), vec(d),
            pl.BlockSpec(ones.shape, const),
            vec(d), vec(d),
            st_spec,
            pl.BlockSpec((CHUNK, CHUNK), const),
            pl.BlockSpec((GROUP_W, GROUP_W), const),
            pl.BlockSpec(masks.shape, lambda c: (0, 0, 0)),
        ],
        out_specs=(pl.BlockSpec((tt, d), lambda c: (jnp.maximum(c - 1, 0), 0)), st_spec),
        scratch_shapes=[pltpu.VMEM((N_GROUPS, GROUP_W, GROUP_W), F32), pltpu.VMEM((tt, RWKV_IN), F32),
                        pltpu.VMEM((tt, d), F32)] + [feature_set] * 8,
        compiler_params=_cparams("arbitrary"),
        name="rwkv_prompt",
    )(x, x, shift0, p["mu"], p["w0"], p["w_w2"], p["a0"], p["a_w2"], p["g_w2"], p["k_k"], p["k_a"], p["r_k"], ones,
      p["ln_w"], p["ln_b"], st0, tri, eye, masks)
    st5 = st.reshape(N_GROUPS, GROUP_HEADS, n, GROUP_HEADS, n)
    s_new = jnp.einsum("ghkjv,hj->ghvk", st5, jnp.eye(GROUP_HEADS, dtype=F32)).reshape(RWKV_HEADS, n, n)
    return o, s_new


def _rwkv_output(y, bonus, gate, lnw_ref, lnb_ref, ones_ref):
    inv_n = 1.0 / RWKV_HEAD_DIM
    mu = _head_sum(y, ones_ref) * inv_n
    dlt = y - mu
    var = _head_sum(dlt * dlt, ones_ref) * inv_n
    yn = dlt * lax.rsqrt(var + GN_EPS) * lnw_ref[...] + lnb_ref[...]
    return ((yn + bonus) * gate).astype(BF16)


def _rwkv_post_kernel(y_ref, bonus_ref, g_ref, lnw_ref, lnb_ref, ones_ref, o_ref):
    o_ref[...] = _rwkv_output(y_ref[...], bonus_ref[...], g_ref[...], lnw_ref, lnb_ref, ones_ref)


def _rwkv_post(y, bonus, g, lnw, lnb, ones, *, tm=256):
    m = y.shape[0]
    row = lambda i: (i, 0)
    const = lambda i: (0, 0)
    tile = pl.BlockSpec((tm, RWKV_DIM), row)
    return pl.pallas_call(
        _rwkv_post_kernel,
        out_shape=jax.ShapeDtypeStruct((m, RWKV_DIM), BF16),
        grid=(m // tm,),
        in_specs=[tile, tile, tile, pl.BlockSpec((1, RWKV_DIM), const), pl.BlockSpec((1, RWKV_DIM), const),
                  pl.BlockSpec(ones.shape, const)],
        out_specs=tile,
        compiler_params=_cparams("parallel"),
        name="rwkv_post",
    )(y, bonus, g, lnw, lnb, ones)


GATE_BLOCK = math.gcd(PROJ_DIM, D_MODEL)


def _merge_kernel(h_ref, g_ref, os_ref, or_ref, om_ref, wos_ref, wor_ref, wom_ref, wout_ref, *rest, parts):
    gate_refs, (o_ref, u_ref, acc_ref) = rest[:N_BRANCH * parts], rest[N_BRANCH * parts:]
    j = pl.program_id(1)

    @pl.when(j == 0)
    def _():
        u_ref[...] = _rms(h_ref[...], g_ref[...]).astype(BF16)
        acc_ref[...] = jnp.zeros_like(acc_ref)

    u = u_ref[...]
    merged = None
    for b, (x_ref, w_ref) in enumerate(((os_ref, wos_ref), (or_ref, wor_ref), (om_ref, wom_ref))):
        gate = jnp.concatenate([_dot(u, gate_refs[b * parts + c][...]) for c in range(parts)], axis=1)
        term = jax.nn.sigmoid(gate) * _dot(x_ref[...], w_ref[...])
        merged = term if merged is None else merged + term
    acc_ref[...] += _dot(merged.astype(BF16), wout_ref[...])

    @pl.when(j == pl.num_programs(1) - 1)
    def _():
        o_ref[...] = h_ref[...] + acc_ref[...]


def _merge(h, g, o_swa, o_rw, o_mem, w_in, wo_swa, wo_rw, wo_mem, w_out, *, tm=512, tn=512):
    m, d = h.shape
    nt = d // tn
    parts = tn // GATE_BLOCK
    g0 = PROJ_DIM // GATE_BLOCK
    row = lambda i, j: (i, 0)
    col = lambda i, j: (0, j)
    gate_specs = [pl.BlockSpec((d, GATE_BLOCK), functools.partial(
        lambda i, j, off: (0, off + j * parts), off=g0 + b * (d // GATE_BLOCK) + c))
        for b in range(N_BRANCH) for c in range(parts)]
    return pl.pallas_call(
        functools.partial(_merge_kernel, parts=parts),
        out_shape=jax.ShapeDtypeStruct((m, d), F32),
        grid=(m // tm, nt),
        in_specs=[
            pl.BlockSpec((tm, d), row),
            pl.BlockSpec((1, d), lambda i, j: (0, 0)),
            pl.BlockSpec((tm, SWA_Q_DIM), row),
            pl.BlockSpec((tm, RWKV_DIM), row),
            pl.BlockSpec((tm, MEM_DIM), row),
            pl.BlockSpec((SWA_Q_DIM, tn), col),
            pl.BlockSpec((RWKV_DIM, tn), col),
            pl.BlockSpec((MEM_DIM, tn), col),
            pl.BlockSpec((tn, d), lambda i, j: (j, 0)),
        ] + gate_specs,
        out_specs=pl.BlockSpec((tm, d), row),
        scratch_shapes=[pltpu.VMEM((tm, d), BF16), pltpu.VMEM((tm, d), F32)],
        compiler_params=_cparams("parallel", "arbitrary"),
        name="merge",
    )(h, g, o_swa, o_rw, o_mem, wo_swa, wo_rw, wo_mem, w_out, *([w_in] * (N_BRANCH * parts)))


def _t5_bucket(dist):
    max_exact = N_BUCKETS // 2
    d = np.maximum(dist, 0)
    log_ratio = (np.log(np.maximum(d, 1).astype(np.float32) / np.float32(max_exact))
                 / np.float32(math.log(MAX_DISTANCE / max_exact)))
    large = np.minimum(max_exact + (log_ratio * (N_BUCKETS - max_exact)).astype(np.int32), N_BUCKETS - 1)
    return np.where(d < max_exact, d, large).astype(np.int32)


def _rel_bias(table, dist):
    onehot = np.eye(N_BUCKETS, dtype=np.float32)[_t5_bucket(dist).reshape(-1)]
    bias = jnp.einsum("nb,bh->hn", jnp.asarray(onehot), table, precision=lax.Precision.HIGHEST)
    return bias.reshape(SWA_HEADS, *dist.shape)


def _rwkv_branch(xr, shift0, s0, p, ones):
    b, t, _ = xr.shape
    flat = lambda z: z.reshape(b * t, z.shape[-1])
    if b == 1:
        o, s_new = _rwkv_prompt(flat(xr), shift0, s0[0], p, ones)
        return o, s_new[None]
    assert b == LANES, "short sequences are batched one per lane"
    r, w, k, v, kk, kka, g, bonus = _rwkv_prep(flat(xr), shift0, p, ones, seq=t)
    y, s_new = _rwkv_lanes(r, w, k, v, kk, kka, jnp.transpose(s0, (1, 2, 3, 0)), t=t)
    o = _rwkv_post(flat(jnp.transpose(y, (1, 0, 2))), bonus, g, p["ln_w"], p["ln_b"], ones)
    return o, jnp.transpose(s_new, (3, 0, 1, 2))


def kernel(x_prompt, mem_prompt, x_sample, cache_swa_k, cache_swa_v, state_rwkv, state_rwkv_shift, cache_mem_k, cache_mem_v, ffn1_norm, ffn1_wi, ffn1_wo, mix_norm, w_in, swa_sinks, rel_bias_table, rwkv_mu, rwkv_w0, rwkv_w_w2, rwkv_a0, rwkv_a_w2, rwkv_g_w2, rwkv_k_k, rwkv_k_a, rwkv_r_k, rwkv_ln_w, rwkv_ln_b, mem_norm, w_mem_kv, w_o_swa, w_o_rwkv, w_o_mem, w_out, ffn2_norm, ffn2_wi, ffn2_wo, final_norm):
    assert ffn1_wi.shape[0] == 1, "single-layer trunk"
    bp, tp, d = x_prompt.shape
    bs, ts, _ = x_sample.shape
    assert bp == 1
    row = lambda z: z.reshape(1, -1).astype(F32)

    g1, gm, g2, gf = row(ffn1_norm[0]), row(mix_norm[0]), row(ffn2_norm[0]), row(final_norm)
    rp = {
        "mu": row(rwkv_mu[0]), "w0": row(rwkv_w0[0]), "w_w2": rwkv_w_w2[0], "a0": row(rwkv_a0[0]),
        "a_w2": rwkv_a_w2[0], "g_w2": rwkv_g_w2[0], "k_k": row(rwkv_k_k[0]), "k_a": row(rwkv_k_a[0]),
        "r_k": row(rwkv_r_k[0]), "ln_w": row(rwkv_ln_w[0]), "ln_b": row(rwkv_ln_b[0]),
    }
    seg = np.arange(GROUP_W) // RWKV_HEAD_DIM
    ones = jnp.asarray(seg[:, None] == seg[None, :], dtype=BF16)
    sinks = swa_sinks[0].astype(F32)
    table = rel_bias_table.astype(F32)

    xp = x_prompt.reshape(tp, d)
    xs = x_sample.reshape(bs * ts, d)
    hs, wg1, wu1, wo1 = _ffn(xs, g1, ffn1_wi[0], ffn1_wo[0], gf, final_norm=False, emit_weights=True, tf=256)
    hp, wi2, wo2, w_in_b, wo_swa, wo_rw, wo_mem, w_out_b = _ffn(
        xp, g1, (wg1, wu1), wo1, gf, final_norm=False,
        riders=(ffn2_wi[0], ffn2_wo[0], w_in[0], w_o_swa[0], w_o_rwkv[0], w_o_mem[0], w_out[0]))
    wo_swa = wo_swa.reshape(SWA_KV_HEADS, SWA_GROUP, SWA_HEAD_DIM, d).transpose(1, 0, 2, 3).reshape(SWA_Q_DIM, d)
    w_q = w_in_b[:, :SWA_Q_DIM].reshape(d, SWA_KV_HEADS, SWA_GROUP, SWA_HEAD_DIM).transpose(0, 2, 1, 3).reshape(d, SWA_Q_DIM)
    qp, kvp, xrp, qmp = _inproj(hp, gm, w_q, w_in_b)
    qs, kvs, xrs, qms = _inproj(hs, gm, w_q, w_in_b)

    w = WINDOW
    dist_p = np.arange(w)[:, None] + w - np.arange(2 * w)[None, :]
    bias_p = _rel_bias(table, dist_p).reshape(SWA_KV_HEADS, SWA_GROUP, w, 2 * w).transpose(1, 0, 2, 3)
    bias_p = bias_p.reshape(SWA_GROUP, SWA_KV_HEADS * w, 2 * w)
    sink_p = jnp.repeat(sinks.reshape(SWA_KV_HEADS, SWA_GROUP).T, w, axis=1).reshape(SWA_GROUP, SWA_KV_HEADS * w, 1)
    o_swa_p = _swa_prompt(qp, kvp, bias_p, sink_p)

    wbuf = cache_swa_k.shape[2]
    dist_s = np.arange(ts)[:, None] + wbuf - np.arange(wbuf + ts)[None, :]
    bias_s = _rel_bias(table, dist_s).reshape(SWA_HEADS * ts, wbuf + ts)
    sink_rows = jnp.repeat(sinks, ts).reshape(SWA_HEADS * ts, 1)
    qs_gt = qs.reshape(bs, ts, SWA_GROUP, SWA_KV_DIM).transpose(0, 2, 1, 3).reshape(bs, SWA_GROUP * ts, SWA_KV_DIM)
    kbuf = cache_swa_k[0].reshape(bs, wbuf, SWA_KV_DIM).transpose(0, 2, 1)
    vbuf = cache_swa_v[0].reshape(bs, wbuf, SWA_KV_DIM).transpose(0, 2, 1)
    o_swa_s, knew_t, vnew_t = _swa_sample(qs_gt, kvs.reshape(bs, ts, 2 * SWA_KV_DIM), kbuf, vbuf,
                                          bias_s[:, :wbuf], bias_s[:, wbuf:], sink_rows)
    o_swa_s = o_swa_s.reshape(bs, SWA_GROUP, ts, SWA_KV_DIM).transpose(0, 2, 1, 3).reshape(bs * ts, SWA_Q_DIM)

    zero_shift = jnp.zeros((bp, 1, RWKV_IN), F32)
    zero_state = jnp.zeros((bp, RWKV_HEADS, RWKV_HEAD_DIM, RWKV_HEAD_DIM), F32)
    o_rw_p, state_p = _rwkv_branch(xrp.reshape(bp, tp, RWKV_IN), zero_shift, zero_state, rp, ones)
    o_rw_s, state_s = _rwkv_branch(xrs.reshape(bs, ts, RWKV_IN), state_rwkv_shift[0], state_rwkv[0], rp, ones)

    mkv = _norm_matmul(mem_prompt.reshape(N_MEM, d), row(mem_norm[0]), w_mem_kv[0].astype(BF16))
    o_mem_p = _mem_prompt(qmp, mkv)
    o_mem_s = _mem_sample(qms.reshape(bs, ts, MEM_DIM), cache_mem_k[0].reshape(bs, N_MEM * MEM_HEADS, MEM_HEAD_DIM),
                          cache_mem_v[0].reshape(bs, N_MEM * MEM_HEADS, MEM_HEAD_DIM)).reshape(bs * ts, MEM_DIM)

    hp = _merge(hp, gm, o_swa_p, o_rw_p, o_mem_p, w_in_b, wo_swa, wo_rw, wo_mem, w_out_b)
    hs = _merge(hs, gm, o_swa_s, o_rw_s, o_mem_s, w_in_b, wo_swa, wo_rw, wo_mem, w_out_b)
    y_prompt = _ffn(hp, g2, wi2, wo2, gf, final_norm=True).reshape(bp, tp, d)
    y_sample = _ffn(hs, g2, wi2, wo2, gf, final_norm=True).reshape(bs, ts, d)

    wp = min(w, tp)
    p_k = kvp[tp - wp:, :SWA_KV_DIM].reshape(1, bp, wp, SWA_KV_HEADS, SWA_HEAD_DIM)
    p_v = kvp[tp - wp:, SWA_KV_DIM:].reshape(1, bp, wp, SWA_KV_HEADS, SWA_HEAD_DIM)
    p_mk = mkv[:, :MEM_DIM].reshape(1, bp, N_MEM, MEM_HEADS, MEM_HEAD_DIM)
    p_mv = mkv[:, MEM_DIM:].reshape(1, bp, N_MEM, MEM_HEADS, MEM_HEAD_DIM)
    s_k = knew_t.transpose(0, 2, 1).reshape(1, bs, wbuf, SWA_KV_HEADS, SWA_HEAD_DIM)
    s_v = vnew_t.transpose(0, 2, 1).reshape(1, bs, wbuf, SWA_KV_HEADS, SWA_HEAD_DIM)
    return (y_prompt, y_sample,
            p_k, p_v, state_p[None], xrp[tp - 1:].reshape(1, bp, 1, RWKV_IN), p_mk, p_mv,
            s_k, s_v, state_s[None], xrs.reshape(bs, ts, RWKV_IN)[:, ts - 1:][None])
```

```python
import functools
import math

import jax
import jax.numpy as jnp
import numpy as np
from jax import lax
from jax.experimental import pallas as pl
from jax.experimental.pallas import tpu as pltpu

F32 = jnp.float32
BF16 = jnp.bfloat16

D_MODEL = 2048
SWA_HEADS = 16
SWA_KV_HEADS = 4
SWA_GROUP = SWA_HEADS // SWA_KV_HEADS
SWA_HEAD_DIM = 64
SWA_Q_DIM = SWA_HEADS * SWA_HEAD_DIM
SWA_KV_DIM = SWA_KV_HEADS * SWA_HEAD_DIM
WINDOW = 128
N_BUCKETS = 32
MAX_DISTANCE = 128
RWKV_HEADS = 8
RWKV_HEAD_DIM = 64
RWKV_DIM = RWKV_HEADS * RWKV_HEAD_DIM
LORA_W = 64
LORA_A = 64
LORA_G = 128
RWKV_IN = 3 * RWKV_DIM + LORA_W + LORA_A + LORA_G
N_MEM = 256
MEM_HEADS = 4
MEM_HEAD_DIM = 128
MEM_DIM = MEM_HEADS * MEM_HEAD_DIM
N_BRANCH = 3
PROJ_DIM = SWA_Q_DIM + 2 * SWA_KV_DIM + RWKV_IN + MEM_DIM
NORM_EPS = 1e-6
GN_EPS = 64e-5
NEG_INF = -1e30

LANES = 128
SUBLANES = 8
VMEM_LIMIT = 56 * 1024 * 1024


def _cparams(*sem):
    return pltpu.CompilerParams(dimension_semantics=sem, vmem_limit_bytes=VMEM_LIMIT)


def _rms(x, g):
    return x * lax.rsqrt(jnp.mean(x * x, axis=-1, keepdims=True) + NORM_EPS) * g


def _dot(a, b):
    return jnp.dot(a, b, preferred_element_type=F32)


def _dot_nt(a, b):
    return lax.dot_general(a, b, (((1,), (1,)), ((), ())), preferred_element_type=F32)


def _dot_hi(a, b):
    return jnp.dot(a, b, preferred_element_type=F32, precision=lax.Precision.HIGHEST)


def _ffn_kernel(x_ref, g_ref, wg_ref, wu_ref, wo_ref, gf_ref, *rest, final_norm, n_riders, emit_weights):
    rider_in, (o_ref, *extra_out), (xn_ref, acc_ref) = rest[:n_riders], rest[n_riders:-2], rest[-2:]
    for src, dst in zip(rider_in, extra_out[:n_riders]):
        dst[...] = src[...].astype(BF16)
    j = pl.program_id(1)

    @pl.when(j == 0)
    def _():
        xn_ref[...] = _rms(x_ref[...], g_ref[...]).astype(BF16)
        acc_ref[...] = jnp.zeros_like(acc_ref)

    wg, wu, wo = wg_ref[...], wu_ref[...], wo_ref[...]
    if emit_weights:
        wg, wu, wo = wg.astype(BF16), wu.astype(BF16), wo.astype(BF16)
        for dst, val in zip(extra_out[n_riders:], (wg, wu, wo)):
            dst[...] = val
    xn = xn_ref[...]
    gate = _dot(xn, wg)
    up = _dot(xn, wu)
    act = (gate * jax.nn.sigmoid(gate)) * up
    acc_ref[...] += _dot(act.astype(BF16), wo)

    @pl.when(j == pl.num_programs(1) - 1)
    def _():
        h = x_ref[...] + 0.5 * acc_ref[...]
        if final_norm:
            h = _rms(h, gf_ref[...])
        o_ref[...] = h


def _rider_tiling(shape, steps):
    rows, cols = shape
    best, best_score = (1, 1), (0, 0)
    for nr in range(1, rows // 16 + 1):
        if rows % nr or (rows // nr) % 16:
            continue
        for nc in range(1, cols // LANES + 1):
            if cols % nc or (cols // nc) % LANES or nr * nc > steps:
                continue
            score = (min(cols // nc, 1024), nr * nc)
            if score > best_score:
                best, best_score = (nr, nc), score
    return best


def _ffn(x, g, wi, wo, gf, *, final_norm, riders=(), emit_weights=False, tm=512, tf=512):
    m, d = x.shape
    dff = wo.shape[0]
    nf = dff // tf
    steps = (m // tm) * nf
    wg, wu = wi if isinstance(wi, tuple) else (wi, wi)
    up_off = 0 if isinstance(wi, tuple) else nf
    assert not emit_weights or m == tm
    weight_specs = [pl.BlockSpec((d, tf), lambda i, j: (0, j)), pl.BlockSpec((d, tf), lambda i, j: (0, j)),
                    pl.BlockSpec((tf, d), lambda i, j: (j, 0))]
    weight_out = [jax.ShapeDtypeStruct((d, dff), BF16)] * 2 + [jax.ShapeDtypeStruct((dff, d), BF16)]
    rider_specs = []
    for arr in riders:
        nr, nc = _rider_tiling(arr.shape, steps)
        br, bc = arr.shape[0] // nr, arr.shape[1] // nc
        rider_specs.append(pl.BlockSpec((br, bc), functools.partial(
            lambda i, j, nc, last: (jnp.minimum(i * nf + j, last) // nc, jnp.minimum(i * nf + j, last) % nc),
            nc=nc, last=nr * nc - 1)))
    out = pl.pallas_call(
        functools.partial(_ffn_kernel, final_norm=final_norm, n_riders=len(riders), emit_weights=emit_weights),
        out_shape=([jax.ShapeDtypeStruct((m, d), F32)] + [jax.ShapeDtypeStruct(arr.shape, BF16) for arr in riders]
                   + (weight_out if emit_weights else [])),
        grid=(m // tm, nf),
        in_specs=[
            pl.BlockSpec((tm, d), lambda i, j: (i, 0)),
            pl.BlockSpec((1, d), lambda i, j: (0, 0)),
            pl.BlockSpec((d, tf), lambda i, j: (0, j)),
            pl.BlockSpec((d, tf), lambda i, j: (0, j + up_off)),
            pl.BlockSpec((tf, d), lambda i, j: (j, 0)),
            pl.BlockSpec((1, d), lambda i, j: (0, 0)),
        ] + rider_specs,
        out_specs=([pl.BlockSpec((tm, d), lambda i, j: (i, 0))] + rider_specs
                   + (weight_specs if emit_weights else [])),
        scratch_shapes=[pltpu.VMEM((tm, d), BF16), pltpu.VMEM((tm, d), F32)],
        compiler_params=_cparams("arbitrary", "arbitrary"),
        name="ffn_final" if final_norm else "ffn",
    )(x, g, wg, wu, wo, gf, *riders)
    return out if (riders or emit_weights) else out[0]


def _inproj_kernel(h_ref, g_ref, wq_ref, w_ref, q_ref, kv_ref, xr_ref, qm_ref):
    u = _rms(h_ref[...], g_ref[...]).astype(BF16)
    c0, c1, c2 = SWA_Q_DIM, SWA_Q_DIM + 2 * SWA_KV_DIM, SWA_Q_DIM + 2 * SWA_KV_DIM + RWKV_IN
    q_ref[...] = _dot(u, wq_ref[...]).astype(BF16)
    kv_ref[...] = _dot(u, w_ref[:, c0:c1])
    xr_ref[...] = _dot(u, w_ref[:, c1:c2])
    qm_ref[...] = _dot(u, w_ref[:, c2:PROJ_DIM]).astype(BF16)


def _inproj(h, g, wq, w, *, tm=256):
    m, d = h.shape
    row = lambda i: (i, 0)
    return pl.pallas_call(
        _inproj_kernel,
        out_shape=(
            jax.ShapeDtypeStruct((m, SWA_Q_DIM), BF16),
            jax.ShapeDtypeStruct((m, 2 * SWA_KV_DIM), F32),
            jax.ShapeDtypeStruct((m, RWKV_IN), F32),
            jax.ShapeDtypeStruct((m, MEM_DIM), BF16),
        ),
        grid=(m // tm,),
        in_specs=[
            pl.BlockSpec((tm, d), row),
            pl.BlockSpec((1, d), lambda i: (0, 0)),
            pl.BlockSpec((d, SWA_Q_DIM), lambda i: (0, 0), pipeline_mode=pl.Buffered(1)),
            pl.BlockSpec((d, PROJ_DIM), lambda i: (0, 0), pipeline_mode=pl.Buffered(1)),
        ],
        out_specs=(
            pl.BlockSpec((tm, SWA_Q_DIM), row),
            pl.BlockSpec((tm, 2 * SWA_KV_DIM), row),
            pl.BlockSpec((tm, RWKV_IN), row),
            pl.BlockSpec((tm, MEM_DIM), row),
        ),
        compiler_params=_cparams("parallel"),
        name="inproj",
    )(h, g, wq, w)


def _norm_matmul_kernel(x_ref, g_ref, w_ref, o_ref):
    o_ref[...] = _dot(_rms(x_ref[...], g_ref[...]).astype(BF16), w_ref[...])


def _norm_matmul(x, g, w, *, tn=512):
    m, d = x.shape
    n = w.shape[1]
    return pl.pallas_call(
        _norm_matmul_kernel,
        out_shape=jax.ShapeDtypeStruct((m, n), F32),
        grid=(n // tn,),
        in_specs=[
            pl.BlockSpec((m, d), lambda j: (0, 0)),
            pl.BlockSpec((1, d), lambda j: (0, 0)),
            pl.BlockSpec((d, tn), lambda j: (0, j)),
        ],
        out_specs=pl.BlockSpec((m, tn), lambda j: (0, j)),
        compiler_params=_cparams("parallel"),
        name="norm_matmul",
    )(x, g, w)


def _swa_prompt_kernel(q_ref, kvc_ref, kvp_ref, bias_ref, sink_ref, o_ref, *, nq):
    i = pl.program_id(0)
    w = WINDOW
    rows = SWA_KV_HEADS * w
    kv_blocks = [kvp_ref[...]] + [kvc_ref[s * w:(s + 1) * w, :] for s in range(nq)]
    k_blocks = [x[:, 0:SWA_KV_DIM].astype(BF16) for x in kv_blocks]
    v_blocks = [x[:, SWA_KV_DIM:].astype(BF16) for x in kv_blocks]
    qpos = lax.broadcasted_iota(jnp.int32, (rows, 2 * w), 0) % w
    col = lax.broadcasted_iota(jnp.int32, (rows, 2 * w), 1)
    dist = qpos + w - col
    in_window = (dist >= 0) & (dist < w)
    first = in_window & ((col >= w) | (i > 0))
    lane_head = lax.broadcasted_iota(jnp.int32, (w, SWA_KV_DIM), 1) // SWA_HEAD_DIM
    scale = SWA_HEAD_DIM ** -0.5
    pairs = [(s, g) for s in range(nq) for g in range(SWA_GROUP)]
    k = [jnp.concatenate(k_blocks[s:s + 2], axis=0) for s in range(nq)]
    v = [jnp.concatenate(v_blocks[s:s + 2], axis=0) for s in range(nq)]
    logits = []
    for s, g in pairs:
        qg = q_ref[s * w:(s + 1) * w, g * SWA_KV_DIM:(g + 1) * SWA_KV_DIM].astype(F32) * scale
        qs = jnp.concatenate([jnp.where(lane_head == kvh, qg, 0.0) for kvh in range(SWA_KV_HEADS)], axis=0)
        lg = _dot_nt(qs.astype(BF16), k[s])
        logits.append(jnp.where(first if s == 0 else in_window, lg + bias_ref[g], NEG_INF))
    sink = [sink_ref[g] for _, g in pairs]
    m = [jnp.maximum(jnp.max(x, axis=-1, keepdims=True), sk) for x, sk in zip(logits, sink)]
    p = [jnp.exp(x - mi) for x, mi in zip(logits, m)]
    inv = [1.0 / (jnp.sum(pi, axis=-1, keepdims=True) + jnp.exp(sk - mi)) for pi, sk, mi in zip(p, sink, m)]
    ov = [_dot((pi * ii).astype(BF16), v[s]) for (s, _), pi, ii in zip(pairs, p, inv)]
    for (s, g), o in zip(pairs, ov):
        og = jnp.zeros((w, SWA_KV_DIM), F32)
        for kvh in range(SWA_KV_HEADS):
            og = jnp.where(lane_head == kvh, o[kvh * w:(kvh + 1) * w], og)
        o_ref[s * w:(s + 1) * w, g * SWA_KV_DIM:(g + 1) * SWA_KV_DIM] = og.astype(BF16)


def _swa_prompt(q, kv, bias, sink_rows, *, nq=8):
    t = q.shape[0]
    w = WINDOW
    rows = SWA_KV_HEADS * w
    return pl.pallas_call(
        functools.partial(_swa_prompt_kernel, nq=nq),
        out_shape=jax.ShapeDtypeStruct((t, SWA_Q_DIM), BF16),
        grid=(t // (nq * w),),
        in_specs=[
            pl.BlockSpec((nq * w, SWA_Q_DIM), lambda i: (i, 0)),
            pl.BlockSpec((nq * w, 2 * SWA_KV_DIM), lambda i: (i, 0)),
            pl.BlockSpec((w, 2 * SWA_KV_DIM), lambda i: (jnp.maximum(i * nq - 1, 0), 0)),
            pl.BlockSpec((SWA_GROUP, rows, 2 * w), lambda i: (0, 0, 0)),
            pl.BlockSpec((SWA_GROUP, rows, 1), lambda i: (0, 0, 0)),
        ],
        out_specs=pl.BlockSpec((nq * w, SWA_Q_DIM), lambda i: (i, 0)),
        compiler_params=_cparams("parallel"),
        name="swa_prompt",
    )(q, kv, kv, bias, sink_rows)


def _swa_sample_kernel(q_ref, kvn_ref, kb_ref, vb_ref, bias_b_ref, bias_n_ref, sink_ref, o_ref, ko_ref, vo_ref,
                       *, bb, t):
    gt = SWA_GROUP * t
    rows = SWA_KV_HEADS * gt
    w = kb_ref.shape[2]
    scale = SWA_HEAD_DIM ** -0.5
    lane_head = lax.broadcasted_iota(jnp.int32, (gt, SWA_KV_DIM), 1) // SWA_HEAD_DIM
    tok = lax.broadcasted_iota(jnp.int32, (rows, w), 0) % t
    keyj = lax.broadcasted_iota(jnp.int32, (rows, w), 1)
    valid_b = (tok + w - keyj) < WINDOW
    tok_n = lax.broadcasted_iota(jnp.int32, (rows, 1), 0) % t
    sink = sink_ref[...]
    bs = range(bb)
    toks = range(t)
    qall = [jnp.concatenate([jnp.where(lane_head == kvh, q_ref[b].astype(F32), 0.0) for kvh in range(SWA_KV_HEADS)],
                            axis=0) for b in bs]
    kvn = [kvn_ref[b] for b in bs]
    lb = [_dot(qall[b].astype(BF16), kb_ref[b].astype(BF16)) for b in bs]
    lb = [jnp.where(valid_b, lb[b] * scale + bias_b_ref[...], NEG_INF) for b in bs]
    ln = [[jnp.sum(qall[b] * kvn[b][j:j + 1, 0:SWA_KV_DIM], axis=-1, keepdims=True) for j in toks] for b in bs]
    ln = [[jnp.where(tok_n >= j, ln[b][j] * scale + bias_n_ref[:, j:j + 1], NEG_INF) for j in toks] for b in bs]
    m = [jnp.maximum(jnp.max(lb[b], axis=-1, keepdims=True), sink) for b in bs]
    m = [functools.reduce(jnp.maximum, ln[b], m[b]) for b in bs]
    pb = [jnp.exp(lb[b] - m[b]) for b in bs]
    pn = [[jnp.exp(ln[b][j] - m[b]) for j in toks] for b in bs]
    denom = [jnp.sum(pb[b], axis=-1, keepdims=True) + jnp.exp(sink - m[b]) for b in bs]
    inv = [1.0 / functools.reduce(jnp.add, pn[b], denom[b]) for b in bs]
    oall = [_dot_nt((pb[b] * inv[b]).astype(BF16), vb_ref[b].astype(BF16)) for b in bs]
    for b in bs:
        ob = oall[b]
        for j in toks:
            ob = ob + (pn[b][j] * inv[b]) * kvn[b][j:j + 1, SWA_KV_DIM:]
        og = jnp.zeros((gt, SWA_KV_DIM), F32)
        for kvh in range(SWA_KV_HEADS):
            og = jnp.where(lane_head == kvh, ob[kvh * gt:(kvh + 1) * gt], og)
        o_ref[b] = og.astype(BF16)
    pos = lax.broadcasted_iota(jnp.int32, (SWA_KV_DIM, w), 1)
    pad = jnp.zeros((w - SUBLANES, 2 * SWA_KV_DIM), F32)
    row8 = lax.broadcasted_iota(jnp.int32, (SUBLANES, 2 * SWA_KV_DIM), 0)
    for b in bs:
        last8 = jnp.zeros((SUBLANES, 2 * SWA_KV_DIM), F32)
        for j in toks:
            last8 = jnp.where(row8 == SUBLANES - t + j, kvn[b][j:j + 1], last8)
        tail_t = jnp.concatenate([pad, last8], axis=0).T
        ko_ref[b] = jnp.where(pos >= w - t, tail_t[:SWA_KV_DIM], pltpu.roll(kb_ref[b], w - t, axis=1))
        vo_ref[b] = jnp.where(pos >= w - t, tail_t[SWA_KV_DIM:], pltpu.roll(vb_ref[b], w - t, axis=1))


def _swa_sample(q, kvn, kbuf, vbuf, bias_b, bias_n, sink_rows, *, bb=8):
    b, gt, _ = q.shape
    t = kvn.shape[1]
    w = kbuf.shape[2]
    rows = SWA_KV_HEADS * gt
    blk = lambda i: (i, 0, 0)
    const = lambda i: (0, 0)
    cache = pl.BlockSpec((bb, SWA_KV_DIM, w), blk)
    return pl.pallas_call(
        functools.partial(_swa_sample_kernel, bb=bb, t=t),
        out_shape=(jax.ShapeDtypeStruct((b, gt, SWA_KV_DIM), BF16),
                   jax.ShapeDtypeStruct(kbuf.shape, F32), jax.ShapeDtypeStruct(vbuf.shape, F32)),
        grid=(b // bb,),
        in_specs=[
            pl.BlockSpec((bb, gt, SWA_KV_DIM), blk),
            pl.BlockSpec((bb, t, 2 * SWA_KV_DIM), blk),
            cache, cache,
            pl.BlockSpec((rows, w), const),
            pl.BlockSpec((rows, t), const),
            pl.BlockSpec((rows, 1), const),
        ],
        out_specs=(pl.BlockSpec((bb, gt, SWA_KV_DIM), blk), cache, cache),
        compiler_params=_cparams("parallel"),
        name="swa_sample",
    )(q, kvn, kbuf, vbuf, bias_b, bias_n, sink_rows)


def _mem_heads(q, mk, mv):
    scale = MEM_HEAD_DIM ** -0.5
    cols = [slice(h * MEM_HEAD_DIM, (h + 1) * MEM_HEAD_DIM) for h in range(MEM_HEADS)]
    x = [_dot_nt(q[:, sl], mk[:, sl]) * scale for sl in cols]
    m = [jnp.max(xi, axis=-1, keepdims=True) for xi in x]
    p = [jnp.exp(xi - mi) for xi, mi in zip(x, m)]
    inv = [1.0 / jnp.sum(pi, axis=-1, keepdims=True) for pi in p]
    return jnp.concatenate([_dot((pi * ii).astype(BF16), mv[:, sl]) for pi, ii, sl in zip(p, inv, cols)], axis=-1)


def _mem_prompt_kernel(q_ref, mk_ref, mv_ref, o_ref):
    o_ref[...] = _mem_heads(q_ref[...], mk_ref[...].astype(BF16), mv_ref[...].astype(BF16)).astype(BF16)


def _mem_prompt(q, mkv, *, tm=512):
    m = q.shape[0]
    return pl.pallas_call(
        _mem_prompt_kernel,
        out_shape=jax.ShapeDtypeStruct((m, MEM_DIM), BF16),
        grid=(m // tm,),
        in_specs=[
            pl.BlockSpec((tm, MEM_DIM), lambda i: (i, 0)),
            pl.BlockSpec((N_MEM, MEM_DIM), lambda i: (0, 0)),
            pl.BlockSpec((N_MEM, MEM_DIM), lambda i: (0, 1)),
        ],
        out_specs=pl.BlockSpec((tm, MEM_DIM), lambda i: (i, 0)),
        compiler_params=_cparams("parallel"),
        name="mem_prompt",
    )(q, mkv, mkv)


def _mem_sample_kernel(q_ref, mk_ref, mv_ref, o_ref, *, bb):
    scale = MEM_HEAD_DIM ** -0.5
    pairs = [(b, h) for b in range(bb) for h in range(MEM_HEADS)]
    rows = lambda h: pl.ds(h, N_MEM, stride=MEM_HEADS)
    cols = lambda h: slice(h * MEM_HEAD_DIM, (h + 1) * MEM_HEAD_DIM)
    q = [q_ref[b] for b in range(bb)]
    x = [_dot_nt(q[b][:, cols(h)], mk_ref[b, rows(h), :].astype(BF16)) * scale for b, h in pairs]
    m = [jnp.max(xi, axis=-1, keepdims=True) for xi in x]
    p = [jnp.exp(xi - mi) for xi, mi in zip(x, m)]
    inv = [1.0 / jnp.sum(pi, axis=-1, keepdims=True) for pi in p]
    o = [_dot((pi * ii).astype(BF16), mv_ref[b, rows(h), :].astype(BF16)) for (b, h), pi, ii in zip(pairs, p, inv)]
    for b in range(bb):
        o_ref[b] = jnp.concatenate(o[b * MEM_HEADS:(b + 1) * MEM_HEADS], axis=-1).astype(BF16)


def _mem_sample(q, mk, mv, *, bb=8):
    b, t, _ = q.shape
    blk = lambda i: (i, 0, 0)
    return pl.pallas_call(
        functools.partial(_mem_sample_kernel, bb=bb),
        out_shape=jax.ShapeDtypeStruct((b, t, MEM_DIM), BF16),
        grid=(b // bb,),
        in_specs=[
            pl.BlockSpec((bb, t, MEM_DIM), blk),
            pl.BlockSpec((bb, N_MEM * MEM_HEADS, MEM_HEAD_DIM), blk),
            pl.BlockSpec((bb, N_MEM * MEM_HEADS, MEM_HEAD_DIM), blk),
        ],
        out_specs=pl.BlockSpec((bb, t, MEM_DIM), blk),
        compiler_params=_cparams("parallel"),
        name="mem_sample",
    )(q, mk, mv)


def _head_sum(x, ones_ref):
    hi = x.astype(BF16)
    lo = (x - hi.astype(F32)).astype(BF16)
    ones = ones_ref[...]
    w = ones.shape[0]
    return jnp.concatenate([_dot(hi[:, c:c + w], ones) + _dot(lo[:, c:c + w], ones) for c in range(0, x.shape[1], w)],
                           axis=1)


def _token_shift(x_ref, pre_ref, start_ref, mu_ref, *, seq, tm, tile):
    x = x_ref[...]
    row = lax.broadcasted_iota(jnp.int32, x.shape, 0)
    shifted = pltpu.roll(x, 1, axis=0)
    if seq >= tm:
        is_start = (tile * tm) % seq == 0
        first_prev = jnp.where(is_start, start_ref[0], pre_ref[SUBLANES - 1:SUBLANES, :])
        prev = jnp.where(row == 0, first_prev, shifted)
    else:
        prev = jnp.where(row % seq == 0, start_ref[...], shifted)
    return x + mu_ref[...] * (prev - x)


def _rwkv_features(xs, w0_ref, ww2_ref, a0_ref, aw2_ref, gw2_ref, kk_ref, ka_ref, rk_ref, ones_ref):
    d = RWKV_DIM
    r = xs[:, 0:d]
    k = xs[:, d:2 * d]
    v = xs[:, 2 * d:3 * d]
    lw = xs[:, 3 * d:3 * d + LORA_W]
    la = xs[:, 3 * d + LORA_W:3 * d + LORA_W + LORA_A]
    lg = xs[:, 3 * d + LORA_W + LORA_A:]
    wpre = w0_ref[...] + _dot_hi(jnp.tanh(lw), ww2_ref[...])
    w = -jax.nn.softplus(-wpre) - 0.5
    log_decay = -jnp.exp(w)
    a = jax.nn.sigmoid(a0_ref[...] + _dot_hi(la, aw2_ref[...]))
    g = _dot_hi(jax.nn.sigmoid(lg), gw2_ref[...])
    kk = k * kk_ref[...]
    kk = kk / jnp.maximum(jnp.sqrt(_head_sum(kk * kk, ones_ref)), 1e-12)
    kh = k * (1.0 + (a - 1.0) * ka_ref[...])
    bonus = _head_sum(r * kh * rk_ref[...], ones_ref) * v
    return r, log_decay, kh, v, kk, kk * a, g, bonus


def _rwkv_prep_kernel(x_ref, pre_ref, start_ref, mu_ref, w0_ref, ww2_ref, a0_ref, aw2_ref, gw2_ref, kk_ref, ka_ref,
                      rk_ref, ones_ref, *outs, seq, tm):
    xs = _token_shift(x_ref, pre_ref, start_ref, mu_ref, seq=seq, tm=tm, tile=pl.program_id(0))
    feats = _rwkv_features(xs, w0_ref, ww2_ref, a0_ref, aw2_ref, gw2_ref, kk_ref, ka_ref, rk_ref, ones_ref)
    for o_ref, val in zip(outs, feats):
        o_ref[...] = val


def _rwkv_prep(x, shift0, p, ones, *, seq, tm=256):
    m = x.shape[0]
    row = lambda i: (i, 0)
    const = lambda i: (0, 0)
    vec = lambda n: pl.BlockSpec((1, n), const)
    out = jax.ShapeDtypeStruct((m, RWKV_DIM), F32)
    if seq >= tm:
        assert seq % tm == 0
        start = shift0
        start_spec = pl.BlockSpec((1, 1, RWKV_IN), lambda i: ((i * tm) // seq, 0, 0))
    else:
        assert tm % seq == 0
        start = jnp.repeat(shift0[:, 0], seq, axis=0)
        start_spec = pl.BlockSpec((tm, RWKV_IN), row)
    pre_blocks = tm // SUBLANES
    return pl.pallas_call(
        functools.partial(_rwkv_prep_kernel, seq=seq, tm=tm),
        out_shape=(out,) * 8,
        grid=(m // tm,),
        in_specs=[
            pl.BlockSpec((tm, RWKV_IN), row),
            pl.BlockSpec((SUBLANES, RWKV_IN), lambda i: (jnp.maximum(i * pre_blocks - 1, 0), 0)),
            start_spec,
            vec(RWKV_IN), vec(RWKV_DIM),
            pl.BlockSpec((LORA_W, RWKV_DIM), const),
            vec(RWKV_DIM),
            pl.BlockSpec((LORA_A, RWKV_DIM), const),
            pl.BlockSpec((LORA_G, RWKV_DIM), const),
            vec(RWKV_DIM), vec(RWKV_DIM), vec(RWKV_DIM),
            pl.BlockSpec(ones.shape, const),
        ],
        out_specs=(pl.BlockSpec((tm, RWKV_DIM), row),) * 8,
        compiler_params=_cparams("parallel"),
        name="rwkv_prep",
    )(x, x, start, p["mu"], p["w0"], p["w_w2"], p["a0"], p["a_w2"], p["g_w2"], p["k_k"], p["k_a"], p["r_k"], ones)


def _rwkv_lanes_kernel(r_ref, lw_ref, k_ref, v_ref, kk_ref, kka_ref, s_ref, y_ref, so_ref, v_scr, y_scr, *, t):
    n = RWKV_HEAD_DIM
    nb = s_ref.shape[-1]
    heads = range(2)

    def token_major(ref, j):
        return ref[pl.ds(j, nb, stride=t), :].T

    for j in range(t):
        v_scr[...] = token_major(v_ref, j)
        r_t, k_t, kk_t, kka_t = (token_major(ref, j) for ref in (r_ref, k_ref, kk_ref, kka_ref))
        w_t = jnp.exp(token_major(lw_ref, j))
        src = s_ref if j == 0 else so_ref

        def value_group(g, carry):
            rows = pl.multiple_of(g * SUBLANES, SUBLANES)
            ys = [[] for _ in heads]
            vg = [v_scr[pl.ds(h * n + rows, SUBLANES), :] for h in heads]
            for i in range(SUBLANES):
                for h in heads:
                    f = slice(h * n, (h + 1) * n)
                    s = src[h, rows + i]
                    sa = jnp.sum(s * (-kk_t[f]), axis=0, keepdims=True)
                    s = s * w_t[f] + sa * kka_t[f] + vg[h][i:i + 1] * k_t[f]
                    so_ref[h, rows + i] = s
                    ys[h].append(jnp.sum(s * r_t[f], axis=0, keepdims=True))
            for h in heads:
                y_scr[pl.ds(h * n + rows, SUBLANES), :] = jnp.concatenate(ys[h], axis=0)
            return carry

        lax.fori_loop(0, n // SUBLANES, value_group, 0)
        y_ref[j] = y_scr[...].T


def _rwkv_lanes(r, lw, k, v, kk, kka, s0, *, t):
    m, d = r.shape
    nb = m // t
    n = RWKV_HEAD_DIM
    assert nb == LANES, "one batch per lane"
    tok = pl.BlockSpec((m, LANES), lambda p: (0, p))
    st = pl.BlockSpec((2, n, n, nb), lambda p: (p, 0, 0, 0))
    return pl.pallas_call(
        functools.partial(_rwkv_lanes_kernel, t=t),
        out_shape=(jax.ShapeDtypeStruct((t, nb, d), F32), jax.ShapeDtypeStruct((RWKV_HEADS, n, n, nb), F32)),
        grid=(RWKV_HEADS // 2,),
        in_specs=[tok] * 6 + [st],
        out_specs=(pl.BlockSpec((t, nb, LANES), lambda p: (0, 0, p)), st),
        scratch_shapes=[pltpu.VMEM((LANES, nb), F32), pltpu.VMEM((LANES, nb), F32)],
        compiler_params=_cparams("parallel"),
        name="rwkv_lanes",
    )(r, lw, k, v, kk, kka, s0)


CHUNK = 64
GROUP_HEADS = 4
GROUP_W = GROUP_HEADS * RWKV_HEAD_DIM
N_GROUPS = RWKV_HEADS // GROUP_HEADS
(MASK_SAME, MASK_STRICT, MASK_INCL, MASK_LEVEL0) = (0, 1, 2, 3)
N_LEVELS = int(math.log2(CHUNK))


def _chunk_masks():
    i = np.arange(GROUP_W)
    same = (i[:, None] // CHUNK) == (i[None, :] // CHUNK)
    masks = [same, same & (i[None, :] < i[:, None]), same & (i[None, :] <= i[:, None])]
    for lvl in range(N_LEVELS):
        m = 1 << lvl
        masks.append(((i[:, None] // (2 * m)) == (i[None, :] // (2 * m))) & ((i[:, None] // m) != (i[None, :] // m))
                     & (i[None, :] < i[:, None]))
    return np.stack(masks).astype(np.float32)


def _rwkv_prompt_kernel(x_ref, pre_ref, start_ref, mu_ref, w0_ref, ww2_ref, a0_ref, aw2_ref, gw2_ref, kk_p_ref,
                        ka_ref, rk_ref, ones_ref, lnw_ref, lnb_ref, st0_ref, tri_ref, eye_ref, mask_ref,
                        o_ref, sto_ref, st_scr, xs_ref, y_ref, r_set, lw_set, k_set, v_set, kk_set, kka_set, g_set,
                        bonus_set, *, seq, tm, n_tiles):
    step = pl.program_id(0)
    cur = (step + 1) % 2
    nxt = step % 2
    sets = (r_set, lw_set, k_set, v_set, kk_set, kka_set, g_set, bonus_set)
    r_ref, lw_ref, k_ref, v_ref, kk_ref, kka_ref = (s.at[cur] for s in sets[:6])

    @pl.when(step == 0)
    def _():
        st_scr[...] = st0_ref[...]
        for s in sets:
            s[1] = jnp.zeros(s.shape[1:], F32)

    xs_ref[...] = _token_shift(x_ref, pre_ref, start_ref, mu_ref, seq=seq, tm=tm, tile=jnp.minimum(step, n_tiles - 1))
    n_chunks = tm // CHUNK

    piece = 2 * CHUNK
    def features(c):
        rows = slice(c * piece, (c + 1) * piece)
        feats = _rwkv_features(xs_ref[rows, :], w0_ref, ww2_ref, a0_ref, aw2_ref, gw2_ref, kk_p_ref, ka_ref, rk_ref,
                               ones_ref)
        for s, val in zip(sets, feats):
            s[nxt, rows, :] = val

    eye = eye_ref[...]
    tri = tri_ref[...]
    tile_rows = lambda x: jnp.concatenate([x] * GROUP_HEADS, axis=0)
    block_diag = lambda x: (tile_rows(x) * mask_ref[MASK_SAME]).astype(BF16)

    chains = [(slice(c * CHUNK, (c + 1) * CHUNK), slice(g * GROUP_W, (g + 1) * GROUP_W))
              for c in range(n_chunks) for g in range(N_GROUPS)]
    each = lambda f, *cols: [f(*args) for args in zip(*cols)]
    same, strict, incl = mask_ref[MASK_SAME], mask_ref[MASK_STRICT], mask_ref[MASK_INCL]

    def cum_decay(lw):
        h1 = lw.astype(BF16)
        r1 = lw - h1.astype(F32)
        h2 = r1.astype(BF16)
        h3 = (r1 - h2.astype(F32)).astype(BF16)
        return _dot(tri, h1) + _dot(tri, h2) + _dot(tri, h3)

    lw = [lw_ref[rows, sl] for rows, sl in chains]
    kka = [kka_ref[rows, sl] for rows, sl in chains]
    k = [k_ref[rows, sl] for rows, sl in chains]
    cum = each(cum_decay, lw)
    cum_last = each(lambda c: c[CHUNK - 1:CHUNK, :], cum)
    p_inv = each(lambda c: jnp.exp(-c), cum)
    p_tail = each(lambda c, cl: jnp.exp(cl - c), cum, cum_last)
    a_bd = [block_diag(-kk_ref[rows, sl] * jnp.exp(c - l)) for (rows, sl), c, l in zip(chains, cum, lw)]
    r_f = [tile_rows(r_ref[rows, sl] * jnp.exp(c)) * same for (rows, sl), c in zip(chains, cum)]
    r_bd = each(lambda x: x.astype(BF16), r_f)
    v_bd = [block_diag(v_ref[rows, sl]) for rows, sl in chains]
    b_rep = each(lambda x, p: tile_rows((x * p).astype(BF16)), kka, p_inv)
    k_rep = each(lambda x, p: tile_rows((x * p).astype(BF16)), k, p_inv)
    bh_rep = each(lambda x, p: tile_rows(x * p), kka, p_tail)
    kh_rep = each(lambda x, p: tile_rows(x * p), k, p_tail)

    l_ab_f = each(lambda a, b: _dot_nt(a, b) * strict, a_bd, b_rep)
    l_ab = each(lambda x: x.astype(BF16), l_ab_f)
    l_ak = each(lambda a, b: (_dot_nt(a, b) * strict).astype(BF16), a_bd, k_rep)
    m_rb = each(lambda a, b: (_dot_nt(a, b) * incl).astype(BF16), r_bd, b_rep)
    m_rk = each(lambda a, b: (_dot_nt(a, b) * incl).astype(BF16), r_bd, k_rep)
    bh_t = each(lambda x: (x.T * same).astype(BF16), bh_rep)
    kh_t = each(lambda x: (x.T * same).astype(BF16), kh_rep)

    pending = list(range(tm // piece))

    def next_features():
        if pending:
            features(pending.pop(0))

    d = each(lambda l: eye + l * mask_ref[MASK_LEVEL0], l_ab_f)
    for lvl in range(1, N_LEVELS):
        d_b = each(lambda x: x.astype(BF16), d)
        x = each(lambda l, db: (_dot(l, db) * mask_ref[MASK_LEVEL0 + lvl]).astype(BF16), l_ab, d_b)
        next_features()
        d = each(lambda dd, db, xx: dd + _dot(db, xx), d, d_b, x)
    t_b = each(lambda x: x.astype(BF16), d)
    while pending:
        next_features()

    wm = each(lambda a, b, vv: _dot(jnp.concatenate([a, b], axis=0), vv), l_ak, m_rk, v_bd)
    twa = each(lambda t, w, a: _dot(t, jnp.concatenate([w[:GROUP_W].astype(BF16), a], axis=1)).astype(BF16),
               t_b, wm, a_bd)
    ry = each(_dot, m_rb, twa)
    mn = each(_dot, bh_t, twa)
    khv = each(_dot, kh_t, v_bd)
    y0 = each(lambda a, w: a[:, :GROUP_W] + w[GROUP_W:], ry, wm)
    n_x = each(lambda a, b: a[:, :GROUP_W] + b, mn, khv)
    mr = each(lambda a, cl, rf, b: jnp.concatenate(
        [(eye * jnp.exp(cl) + a[:, GROUP_W:]).astype(BF16), (rf + b[:, GROUP_W:]).astype(BF16)], axis=0),
        mn, cum_last, r_f, ry)

    st = [st_scr[g] for g in range(N_GROUPS)]
    for i, (rows, sl) in enumerate(chains):
        g = i % N_GROUPS
        ys = _dot(mr[i], st[g].astype(BF16))
        st[g] = ys[:GROUP_W] + n_x[i]
        y_bd = ys[GROUP_W:] + y0[i]
        y = y_bd[0:CHUNK]
        for h in range(1, GROUP_HEADS):
            y = y + y_bd[h * CHUNK:(h + 1) * CHUNK]
        y_ref[rows, sl] = y
    for g in range(N_GROUPS):
        st_scr[g] = st[g]
    o_ref[...] = _rwkv_output(y_ref[...], bonus_set[cur], g_set[cur], lnw_ref, lnb_ref, ones_ref)

    @pl.when(step == pl.num_programs(0) - 1)
    def _():
        sto_ref[...] = st_scr[...]


def _rwkv_prompt(x, shift0, s0, p, ones, *, chunks_per_step=4):
    t = x.shape[0]
    d = RWKV_DIM
    n = RWKV_HEAD_DIM
    tt = CHUNK * chunks_per_step
    assert CHUNK == n and t % tt == 0
    st0 = jnp.einsum("ghvk,hj->ghkjv", s0.reshape(N_GROUPS, GROUP_HEADS, n, n), jnp.eye(GROUP_HEADS, dtype=F32))
    st0 = st0.reshape(N_GROUPS, GROUP_W, GROUP_W)
    tri = jnp.asarray(np.tril(np.ones((CHUNK, CHUNK), np.float32)), BF16)
    eye = jnp.eye(GROUP_W, dtype=F32)
    masks = jnp.asarray(_chunk_masks())
    const = lambda c: (0, 0)
    vec = lambda width: pl.BlockSpec((1, width), const)
    st_spec = pl.BlockSpec((N_GROUPS, GROUP_W, GROUP_W), lambda c: (0, 0, 0))
    feature_set = pltpu.VMEM((2, tt, d), F32)
    pre_blocks = tt // SUBLANES
    n_tiles = t // tt
    fill = lambda c: jnp.minimum(c, n_tiles - 1)
    o, st = pl.pallas_call(
        functools.partial(_rwkv_prompt_kernel, seq=t, tm=tt, n_tiles=n_tiles),
        out_shape=(jax.ShapeDtypeStruct((t, d), BF16), jax.ShapeDtypeStruct((N_GROUPS, GROUP_W, GROUP_W), F32)),
        grid=(n_tiles + 1,),
        in_specs=[
            pl.BlockSpec((tt, RWKV_IN), lambda c: (fill(c), 0)),
            pl.BlockSpec((SUBLANES, RWKV_IN), lambda c: (jnp.maximum(fill(c) * pre_blocks - 1, 0), 0)),
            pl.BlockSpec((1, 1, RWKV_IN), lambda c: (0, 0, 0)),
            vec(RWKV_IN), vec(d),
            pl.BlockSpec((LORA_W, d), const),
            vec(d),
            pl.BlockSpec((LORA_A, d), const),
            pl.BlockSpec((LORA_G, d), const),
            vec(d), vec(d), vec(d),
            pl.BlockSpec(ones.shape, const),
            vec(d), vec(d),
            st_spec,
            pl.BlockSpec((CHUNK, CHUNK), const),
            pl.BlockSpec((GROUP_W, GROUP_W), const),
            pl.BlockSpec(masks.shape, lambda c: (0, 0, 0)),
        ],
        out_specs=(pl.BlockSpec((tt, d), lambda c: (jnp.maximum(c - 1, 0), 0)), st_spec),
        scratch_shapes=[pltpu.VMEM((N_GROUPS, GROUP_W, GROUP_W), F32), pltpu.VMEM((tt, RWKV_IN), F32),
                        pltpu.VMEM((tt, d), F32)] + [feature_set] * 8,
        compiler_params=_cparams("arbitrary"),
        name="rwkv_prompt",
    )(x, x, shift0, p["mu"], p["w0"], p["w_w2"], p["a0"], p["a_w2"], p["g_w2"], p["k_k"], p["k_a"], p["r_k"], ones,
      p["ln_w"], p["ln_b"], st0, tri, eye, masks)
    st5 = st.reshape(N_GROUPS, GROUP_HEADS, n, GROUP_HEADS, n)
    s_new = jnp.einsum("ghkjv,hj->ghvk", st5, jnp.eye(GROUP_HEADS, dtype=F32)).reshape(RWKV_HEADS, n, n)
    return o, s_new


def _rwkv_output(y, bonus, gate, lnw_ref, lnb_ref, ones_ref):
    inv_n = 1.0 / RWKV_HEAD_DIM
    mu = _head_sum(y, ones_ref) * inv_n
    dlt = y - mu
    var = _head_sum(dlt * dlt, ones_ref) * inv_n
    yn = dlt * lax.rsqrt(var + GN_EPS) * lnw_ref[...] + lnb_ref[...]
    return ((yn + bonus) * gate).astype(BF16)


def _rwkv_post_kernel(y_ref, bonus_ref, g_ref, lnw_ref, lnb_ref, ones_ref, o_ref):
    o_ref[...] = _rwkv_output(y_ref[...], bonus_ref[...], g_ref[...], lnw_ref, lnb_ref, ones_ref)


def _rwkv_post(y, bonus, g, lnw, lnb, ones, *, tm=256):
    m = y.shape[0]
    row = lambda i: (i, 0)
    const = lambda i: (0, 0)
    tile = pl.BlockSpec((tm, RWKV_DIM), row)
    return pl.pallas_call(
        _rwkv_post_kernel,
        out_shape=jax.ShapeDtypeStruct((m, RWKV_DIM), BF16),
        grid=(m // tm,),
        in_specs=[tile, tile, tile, pl.BlockSpec((1, RWKV_DIM), const), pl.BlockSpec((1, RWKV_DIM), const),
                  pl.BlockSpec(ones.shape, const)],
        out_specs=tile,
        compiler_params=_cparams("parallel"),
        name="rwkv_post",
    )(y, bonus, g, lnw, lnb, ones)


GATE_BLOCK = math.gcd(PROJ_DIM, D_MODEL)


def _merge_kernel(h_ref, g_ref, os_ref, or_ref, om_ref, wos_ref, wor_ref, wom_ref, wout_ref, *rest, parts):
    gate_refs, (o_ref, u_ref, acc_ref) = rest[:N_BRANCH * parts], rest[N_BRANCH * parts:]
    j = pl.program_id(1)

    @pl.when(j == 0)
    def _():
        u_ref[...] = _rms(h_ref[...], g_ref[...]).astype(BF16)
        acc_ref[...] = jnp.zeros_like(acc_ref)

    u = u_ref[...]
    merged = None
    for b, (x_ref, w_ref) in enumerate(((os_ref, wos_ref), (or_ref, wor_ref), (om_ref, wom_ref))):
        gate = jnp.concatenate([_dot(u, gate_refs[b * parts + c][...]) for c in range(parts)], axis=1)
        term = jax.nn.sigmoid(gate) * _dot(x_ref[...], w_ref[...])
        merged = term if merged is None else merged + term
    acc_ref[...] += _dot(merged.astype(BF16), wout_ref[...])

    @pl.when(j == pl.num_programs(1) - 1)
    def _():
        o_ref[...] = h_ref[...] + acc_ref[...]


def _merge(h, g, o_swa, o_rw, o_mem, w_in, wo_swa, wo_rw, wo_mem, w_out, *, tm=512, tn=512):
    m, d = h.shape
    nt = d // tn
    parts = tn // GATE_BLOCK
    g0 = PROJ_DIM // GATE_BLOCK
    row = lambda i, j: (i, 0)
    col = lambda i, j: (0, j)
    gate_specs = [pl.BlockSpec((d, GATE_BLOCK), functools.partial(
        lambda i, j, off: (0, off + j * parts), off=g0 + b * (d // GATE_BLOCK) + c))
        for b in range(N_BRANCH) for c in range(parts)]
    return pl.pallas_call(
        functools.partial(_merge_kernel, parts=parts),
        out_shape=jax.ShapeDtypeStruct((m, d), F32),
        grid=(m // tm, nt),
        in_specs=[
            pl.BlockSpec((tm, d), row),
            pl.BlockSpec((1, d), lambda i, j: (0, 0)),
            pl.BlockSpec((tm, SWA_Q_DIM), row),
            pl.BlockSpec((tm, RWKV_DIM), row),
            pl.BlockSpec((tm, MEM_DIM), row),
            pl.BlockSpec((SWA_Q_DIM, tn), col),
            pl.BlockSpec((RWKV_DIM, tn), col),
            pl.BlockSpec((MEM_DIM, tn), col),
            pl.BlockSpec((tn, d), lambda i, j: (j, 0)),
        ] + gate_specs,
        out_specs=pl.BlockSpec((tm, d), row),
        scratch_shapes=[pltpu.VMEM((tm, d), BF16), pltpu.VMEM((tm, d), F32)],
        compiler_params=_cparams("parallel", "arbitrary"),
        name="merge",
    )(h, g, o_swa, o_rw, o_mem, wo_swa, wo_rw, wo_mem, w_out, *([w_in] * (N_BRANCH * parts)))


def _t5_bucket(dist):
    max_exact = N_BUCKETS // 2
    d = np.maximum(dist, 0)
    log_ratio = (np.log(np.maximum(d, 1).astype(np.float32) / np.float32(max_exact))
                 / np.float32(math.log(MAX_DISTANCE / max_exact)))
    large = np.minimum(max_exact + (log_ratio * (N_BUCKETS - max_exact)).astype(np.int32), N_BUCKETS - 1)
    return np.where(d < max_exact, d, large).astype(np.int32)


def _rel_bias(table, dist):
    onehot = np.eye(N_BUCKETS, dtype=np.float32)[_t5_bucket(dist).reshape(-1)]
    bias = jnp.einsum("nb,bh->hn", jnp.asarray(onehot), table, precision=lax.Precision.HIGHEST)
    return bias.reshape(SWA_HEADS, *dist.shape)


def _rwkv_branch(xr, shift0, s0, p, ones):
    b, t, _ = xr.shape
    flat = lambda z: z.reshape(b * t, z.shape[-1])
    if b == 1:
        o, s_new = _rwkv_prompt(flat(xr), shift0, s0[0], p, ones)
        return o, s_new[None]
    assert b == LANES, "short sequences are batched one per lane"
    r, w, k, v, kk, kka, g, bonus = _rwkv_prep(flat(xr), shift0, p, ones, seq=t)
    y, s_new = _rwkv_lanes(r, w, k, v, kk, kka, jnp.transpose(s0, (1, 2, 3, 0)), t=t)
    o = _rwkv_post(flat(jnp.transpose(y, (1, 0, 2))), bonus, g, p["ln_w"], p["ln_b"], ones)
    return o, jnp.transpose(s_new, (3, 0, 1, 2))


def kernel(x_prompt, mem_prompt, x_sample, cache_swa_k, cache_swa_v, state_rwkv, state_rwkv_shift, cache_mem_k, cache_mem_v, ffn1_norm, ffn1_wi, ffn1_wo, mix_norm, w_in, swa_sinks, rel_bias_table, rwkv_mu, rwkv_w0, rwkv_w_w2, rwkv_a0, rwkv_a_w2, rwkv_g_w2, rwkv_k_k, rwkv_k_a, rwkv_r_k, rwkv_ln_w, rwkv_ln_b, mem_norm, w_mem_kv, w_o_swa, w_o_rwkv, w_o_mem, w_out, ffn2_norm, ffn2_wi, ffn2_wo, final_norm):
    assert ffn1_wi.shape[0] == 1, "single-layer trunk"
    bp, tp, d = x_prompt.shape
    bs, ts, _ = x_sample.shape
    assert bp == 1
    row = lambda z: z.reshape(1, -1).astype(F32)

    g1, gm, g2, gf = row(ffn1_norm[0]), row(mix_norm[0]), row(ffn2_norm[0]), row(final_norm)
    rp = {
        "mu": row(rwkv_mu[0]), "w0": row(rwkv_w0[0]), "w_w2": rwkv_w_w2[0], "a0": row(rwkv_a0[0]),
        "a_w2": rwkv_a_w2[0], "g_w2": rwkv_g_w2[0], "k_k": row(rwkv_k_k[0]), "k_a": row(rwkv_k_a[0]),
        "r_k": row(rwkv_r_k[0]), "ln_w": row(rwkv_ln_w[0]), "ln_b": row(rwkv_ln_b[0]),
    }
    seg = np.arange(GROUP_W) // RWKV_HEAD_DIM
    ones = jnp.asarray(seg[:, None] == seg[None, :], dtype=BF16)
    sinks = swa_sinks[0].astype(F32)
    table = rel_bias_table.astype(F32)

    xp = x_prompt.reshape(tp, d)
    xs = x_sample.reshape(bs * ts, d)
    hs, wg1, wu1, wo1 = _ffn(xs, g1, ffn1_wi[0], ffn1_wo[0], gf, final_norm=False, emit_weights=True, tf=256)
    hp, wi2, wo2, w_in_b, wo_swa, wo_rw, wo_mem, w_out_b = _ffn(
        xp, g1, (wg1, wu1), wo1, gf, final_norm=False,
        riders=(ffn2_wi[0], ffn2_wo[0], w_in[0], w_o_swa[0], w_o_rwkv[0], w_o_mem[0], w_out[0]))
    wo_swa = wo_swa.reshape(SWA_KV_HEADS, SWA_GROUP, SWA_HEAD_DIM, d).transpose(1, 0, 2, 3).reshape(SWA_Q_DIM, d)
    w_q = w_in_b[:, :SWA_Q_DIM].reshape(d, SWA_KV_HEADS, SWA_GROUP, SWA_HEAD_DIM).transpose(0, 2, 1, 3).reshape(d, SWA_Q_DIM)
    qp, kvp, xrp, qmp = _inproj(hp, gm, w_q, w_in_b)
    qs, kvs, xrs, qms = _inproj(hs, gm, w_q, w_in_b)

    w = WINDOW
    dist_p = np.arange(w)[:, None] + w - np.arange(2 * w)[None, :]
    bias_p = _rel_bias(table, dist_p).reshape(SWA_KV_HEADS, SWA_GROUP, w, 2 * w).transpose(1, 0, 2, 3)
    bias_p = bias_p.reshape(SWA_GROUP, SWA_KV_HEADS * w, 2 * w)
    sink_p = jnp.repeat(sinks.reshape(SWA_KV_HEADS, SWA_GROUP).T, w, axis=1).reshape(SWA_GROUP, SWA_KV_HEADS * w, 1)
    o_swa_p = _swa_prompt(qp, kvp, bias_p, sink_p)

    wbuf = cache_swa_k.shape[2]
    dist_s = np.arange(ts)[:, None] + wbuf - np.arange(wbuf + ts)[None, :]
    bias_s = _rel_bias(table, dist_s).reshape(SWA_HEADS * ts, wbuf + ts)
    sink_rows = jnp.repeat(sinks, ts).reshape(SWA_HEADS * ts, 1)
    qs_gt = qs.reshape(bs, ts, SWA_GROUP, SWA_KV_DIM).transpose(0, 2, 1, 3).reshape(bs, SWA_GROUP * ts, SWA_KV_DIM)
    kbuf = cache_swa_k[0].reshape(bs, wbuf, SWA_KV_DIM).transpose(0, 2, 1)
    vbuf = cache_swa_v[0].reshape(bs, wbuf, SWA_KV_DIM).transpose(0, 2, 1)
    o_swa_s, knew_t, vnew_t = _swa_sample(qs_gt, kvs.reshape(bs, ts, 2 * SWA_KV_DIM), kbuf, vbuf,
                                          bias_s[:, :wbuf], bias_s[:, wbuf:], sink_rows)
    o_swa_s = o_swa_s.reshape(bs, SWA_GROUP, ts, SWA_KV_DIM).transpose(0, 2, 1, 3).reshape(bs * ts, SWA_Q_DIM)

    zero_shift = jnp.zeros((bp, 1, RWKV_IN), F32)
    zero_state = jnp.zeros((bp, RWKV_HEADS, RWKV_HEAD_DIM, RWKV_HEAD_DIM), F32)
    o_rw_p, state_p = _rwkv_branch(xrp.reshape(bp, tp, RWKV_IN), zero_shift, zero_state, rp, ones)
    o_rw_s, state_s = _rwkv_branch(xrs.reshape(bs, ts, RWKV_IN), state_rwkv_shift[0], state_rwkv[0], rp, ones)

    mkv = _norm_matmul(mem_prompt.reshape(N_MEM, d), row(mem_norm[0]), w_mem_kv[0].astype(BF16))
    o_mem_p = _mem_prompt(qmp, mkv)
    o_mem_s = _mem_sample(qms.reshape(bs, ts, MEM_DIM), cache_mem_k[0].reshape(bs, N_MEM * MEM_HEADS, MEM_HEAD_DIM),
                          cache_mem_v[0].reshape(bs, N_MEM * MEM_HEADS, MEM_HEAD_DIM)).reshape(bs * ts, MEM_DIM)

    hp = _merge(hp, gm, o_swa_p, o_rw_p, o_mem_p, w_in_b, wo_swa, wo_rw, wo_mem, w_out_b)
    hs = _merge(hs, gm, o_swa_s, o_rw_s, o_mem_s, w_in_b, wo_swa, wo_rw, wo_mem, w_out_b)
    y_prompt = _ffn(hp, g2, wi2, wo2, gf, final_norm=True).reshape(bp, tp, d)
    y_sample = _ffn(hs, g2, wi2, wo2, gf, final_norm=True).reshape(bs, ts, d)

    wp = min(w, tp)
    p_k = kvp[tp - wp:, :SWA_KV_DIM].reshape(1, bp, wp, SWA_KV_HEADS, SWA_HEAD_DIM)
    p_v = kvp[tp - wp:, SWA_KV_DIM:].reshape(1, bp, wp, SWA_KV_HEADS, SWA_HEAD_DIM)
    p_mk = mkv[:, :MEM_DIM].reshape(1, bp, N_MEM, MEM_HEADS, MEM_HEAD_DIM)
    p_mv = mkv[:, MEM_DIM:].reshape(1, bp, N_MEM, MEM_HEADS, MEM_HEAD_DIM)
    s_k = knew_t.transpose(0, 2, 1).reshape(1, bs, wbuf, SWA_KV_HEADS, SWA_HEAD_DIM)
    s_v = vnew_t.transpose(0, 2, 1).reshape(1, bs, wbuf, SWA_KV_HEADS, SWA_HEAD_DIM)
    return (y_prompt, y_sample,
            p_k, p_v, state_p[None], xrp[tp - 1:].reshape(1, bp, 1, RWKV_IN), p_mk, p_mv,
            s_k, s_v, state_s[None], xrs.reshape(bs, ts, RWKV_IN)[:, ts - 1:][None])
```

```python
import functools
import math

import jax
import jax.numpy as jnp
import numpy as np
from jax import lax
from jax.experimental import pallas as pl
from jax.experimental.pallas import tpu as pltpu

F32 = jnp.float32
BF16 = jnp.bfloat16

D_MODEL = 2048
SWA_HEADS = 16
SWA_KV_HEADS = 4
SWA_GROUP = SWA_HEADS // SWA_KV_HEADS
SWA_HEAD_DIM = 64
SWA_Q_DIM = SWA_HEADS * SWA_HEAD_DIM
SWA_KV_DIM = SWA_KV_HEADS * SWA_HEAD_DIM
WINDOW = 128
N_BUCKETS = 32
MAX_DISTANCE = 128
RWKV_HEADS = 8
RWKV_HEAD_DIM = 64
RWKV_DIM = RWKV_HEADS * RWKV_HEAD_DIM
LORA_W = 64
LORA_A = 64
LORA_G = 128
RWKV_IN = 3 * RWKV_DIM + LORA_W + LORA_A + LORA_G
N_MEM = 256
MEM_HEADS = 4
MEM_HEAD_DIM = 128
MEM_DIM = MEM_HEADS * MEM_HEAD_DIM
N_BRANCH = 3
PROJ_DIM = SWA_Q_DIM + 2 * SWA_KV_DIM + RWKV_IN + MEM_DIM
NORM_EPS = 1e-6
GN_EPS = 64e-5
NEG_INF = -1e30

LANES = 128
SUBLANES = 8
VMEM_LIMIT = 56 * 1024 * 1024


def _cparams(*sem):
    return pltpu.CompilerParams(dimension_semantics=sem, vmem_limit_bytes=VMEM_LIMIT)


def _rms(x, g):
    return x * lax.rsqrt(jnp.mean(x * x, axis=-1, keepdims=True) + NORM_EPS) * g


def _dot(a, b):
    return jnp.dot(a, b, preferred_element_type=F32)


def _dot_nt(a, b):
    return lax.dot_general(a, b, (((1,), (1,)), ((), ())), preferred_element_type=F32)


def _dot_hi(a, b):
    return jnp.dot(a, b, preferred_element_type=F32, precision=lax.Precision.HIGHEST)


def _ffn_kernel(x_ref, g_ref, wg_ref, wu_ref, wo_ref, gf_ref, *rest, final_norm, n_riders, emit_weights):
    rider_in, (o_ref, *extra_out), (xn_ref, acc_ref) = rest[:n_riders], rest[n_riders:-2], rest[-2:]
    for src, dst in zip(rider_in, extra_out[:n_riders]):
        dst[...] = src[...].astype(BF16)
    j = pl.program_id(1)

    @pl.when(j == 0)
    def _():
        xn_ref[...] = _rms(x_ref[...], g_ref[...]).astype(BF16)
        acc_ref[...] = jnp.zeros_like(acc_ref)

    wg, wu, wo = wg_ref[...], wu_ref[...], wo_ref[...]
    if emit_weights:
        wg, wu, wo = wg.astype(BF16), wu.astype(BF16), wo.astype(BF16)
        for dst, val in zip(extra_out[n_riders:], (wg, wu, wo)):
            dst[...] = val
    xn = xn_ref[...]
    gate = _dot(xn, wg)
    up = _dot(xn, wu)
    act = (gate * jax.nn.sigmoid(gate)) * up
    acc_ref[...] += _dot(act.astype(BF16), wo)

    @pl.when(j == pl.num_programs(1) - 1)
    def _():
        h = x_ref[...] + 0.5 * acc_ref[...]
        if final_norm:
            h = _rms(h, gf_ref[...])
        o_ref[...] = h


def _rider_tiling(shape, steps):
    rows, cols = shape
    best, best_score = (1, 1), (0, 0)
    for nr in range(1, rows // 16 + 1):
        if rows % nr or (rows // nr) % 16:
            continue
        for nc in range(1, cols // LANES + 1):
            if cols % nc or (cols // nc) % LANES or nr * nc > steps:
                continue
            score = (min(cols // nc, 1024), nr * nc)
            if score > best_score:
                best, best_score = (nr, nc), score
    return best


def _rider_specs(riders, steps, step_of):
    specs = []
    for arr in riders:
        nr, nc = _rider_tiling(arr.shape, steps)
        specs.append(pl.BlockSpec((arr.shape[0] // nr, arr.shape[1] // nc), functools.partial(
            lambda *idx, nc, last: (jnp.minimum(step_of(*idx), last) // nc, jnp.minimum(step_of(*idx), last) % nc),
            nc=nc, last=nr * nc - 1)))
    return specs


def _ffn(x, g, wi, wo, gf, *, final_norm, riders=(), emit_weights=False, tm=512, tf=512):
    m, d = x.shape
    dff = wo.shape[0]
    nf = dff // tf
    steps = (m // tm) * nf
    wg, wu = wi if isinstance(wi, tuple) else (wi, wi)
    up_off = 0 if isinstance(wi, tuple) else nf
    assert not emit_weights or m == tm
    weight_specs = [pl.BlockSpec((d, tf), lambda i, j: (0, j)), pl.BlockSpec((d, tf), lambda i, j: (0, j)),
                    pl.BlockSpec((tf, d), lambda i, j: (j, 0))]
    weight_out = [jax.ShapeDtypeStruct((d, dff), BF16)] * 2 + [jax.ShapeDtypeStruct((dff, d), BF16)]
    rider_specs = _rider_specs(riders, steps, lambda i, j: i * nf + j)
    out = pl.pallas_call(
        functools.partial(_ffn_kernel, final_norm=final_norm, n_riders=len(riders), emit_weights=emit_weights),
        out_shape=([jax.ShapeDtypeStruct((m, d), F32)] + [jax.ShapeDtypeStruct(arr.shape, BF16) for arr in riders]
                   + (weight_out if emit_weights else [])),
        grid=(m // tm, nf),
        in_specs=[
            pl.BlockSpec((tm, d), lambda i, j: (i, 0)),
            pl.BlockSpec((1, d), lambda i, j: (0, 0)),
            pl.BlockSpec((d, tf), lambda i, j: (0, j)),
            pl.BlockSpec((d, tf), lambda i, j: (0, j + up_off)),
            pl.BlockSpec((tf, d), lambda i, j: (j, 0)),
            pl.BlockSpec((1, d), lambda i, j: (0, 0)),
        ] + rider_specs,
        out_specs=([pl.BlockSpec((tm, d), lambda i, j: (i, 0))] + rider_specs
                   + (weight_specs if emit_weights else [])),
        scratch_shapes=[pltpu.VMEM((tm, d), BF16), pltpu.VMEM((tm, d), F32)],
        compiler_params=_cparams("arbitrary", "arbitrary"),
        name="ffn_final" if final_norm else "ffn",
    )(x, g, wg, wu, wo, gf, *riders)
    return out if (riders or emit_weights) else out[0]


def _inproj_kernel(h_ref, g_ref, wq_ref, w_ref, q_ref, kv_ref, xr_ref, qm_ref):
    u = _rms(h_ref[...], g_ref[...]).astype(BF16)
    c0, c1, c2 = SWA_Q_DIM, SWA_Q_DIM + 2 * SWA_KV_DIM, SWA_Q_DIM + 2 * SWA_KV_DIM + RWKV_IN
    q_ref[...] = _dot(u, wq_ref[...]).astype(BF16)
    kv_ref[...] = _dot(u, w_ref[:, c0:c1])
    xr_ref[...] = _dot(u, w_ref[:, c1:c2])
    qm_ref[...] = _dot(u, w_ref[:, c2:PROJ_DIM]).astype(BF16)


def _inproj(h, g, wq, w, *, tm=256):
    m, d = h.shape
    row = lambda i: (i, 0)
    return pl.pallas_call(
        _inproj_kernel,
        out_shape=(
            jax.ShapeDtypeStruct((m, SWA_Q_DIM), BF16),
            jax.ShapeDtypeStruct((m, 2 * SWA_KV_DIM), F32),
            jax.ShapeDtypeStruct((m, RWKV_IN), F32),
            jax.ShapeDtypeStruct((m, MEM_DIM), BF16),
        ),
        grid=(m // tm,),
        in_specs=[
            pl.BlockSpec((tm, d), row),
            pl.BlockSpec((1, d), lambda i: (0, 0)),
            pl.BlockSpec((d, SWA_Q_DIM), lambda i: (0, 0), pipeline_mode=pl.Buffered(1)),
            pl.BlockSpec((d, PROJ_DIM), lambda i: (0, 0), pipeline_mode=pl.Buffered(1)),
        ],
        out_specs=(
            pl.BlockSpec((tm, SWA_Q_DIM), row),
            pl.BlockSpec((tm, 2 * SWA_KV_DIM), row),
            pl.BlockSpec((tm, RWKV_IN), row),
            pl.BlockSpec((tm, MEM_DIM), row),
        ),
        compiler_params=_cparams("parallel"),
        name="inproj",
    )(h, g, wq, w)


def _norm_matmul_kernel(x_ref, g_ref, w_ref, o_ref):
    o_ref[...] = _dot(_rms(x_ref[...], g_ref[...]).astype(BF16), w_ref[...])


def _norm_matmul(x, g, w, *, tn=512):
    m, d = x.shape
    n = w.shape[1]
    return pl.pallas_call(
        _norm_matmul_kernel,
        out_shape=jax.ShapeDtypeStruct((m, n), F32),
        grid=(n // tn,),
        in_specs=[
            pl.BlockSpec((m, d), lambda j: (0, 0)),
            pl.BlockSpec((1, d), lambda j: (0, 0)),
            pl.BlockSpec((d, tn), lambda j: (0, j)),
        ],
        out_specs=pl.BlockSpec((m, tn), lambda j: (0, j)),
        compiler_params=_cparams("parallel"),
        name="norm_matmul",
    )(x, g, w)


def _swa_prompt_kernel(q_ref, kvc_ref, kvp_ref, bias_ref, sink_ref, o_ref, *, nq):
    i = pl.program_id(0)
    w = WINDOW
    rows = SWA_KV_HEADS * w
    kv_blocks = [kvp_ref[...]] + [kvc_ref[s * w:(s + 1) * w, :] for s in range(nq)]
    k_blocks = [x[:, 0:SWA_KV_DIM].astype(BF16) for x in kv_blocks]
    v_blocks = [x[:, SWA_KV_DIM:].astype(BF16) for x in kv_blocks]
    qpos = lax.broadcasted_iota(jnp.int32, (rows, 2 * w), 0) % w
    col = lax.broadcasted_iota(jnp.int32, (rows, 2 * w), 1)
    dist = qpos + w - col
    in_window = (dist >= 0) & (dist < w)
    first = in_window & ((col >= w) | (i > 0))
    lane_head = lax.broadcasted_iota(jnp.int32, (w, SWA_KV_DIM), 1) // SWA_HEAD_DIM
    scale = SWA_HEAD_DIM ** -0.5
    pairs = [(s, g) for s in range(nq) for g in range(SWA_GROUP)]
    k = [jnp.concatenate(k_blocks[s:s + 2], axis=0) for s in range(nq)]
    v = [jnp.concatenate(v_blocks[s:s + 2], axis=0) for s in range(nq)]
    logits = []
    for s, g in pairs:
        qg = q_ref[s * w:(s + 1) * w, g * SWA_KV_DIM:(g + 1) * SWA_KV_DIM].astype(F32) * scale
        qs = jnp.concatenate([jnp.where(lane_head == kvh, qg, 0.0) for kvh in range(SWA_KV_HEADS)], axis=0)
        lg = _dot_nt(qs.astype(BF16), k[s])
        logits.append(jnp.where(first if s == 0 else in_window, lg + bias_ref[g], NEG_INF))
    sink = [sink_ref[g] for _, g in pairs]
    m = [jnp.maximum(jnp.max(x, axis=-1, keepdims=True), sk) for x, sk in zip(logits, sink)]
    p = [jnp.exp(x - mi) for x, mi in zip(logits, m)]
    inv = [1.0 / (jnp.sum(pi, axis=-1, keepdims=True) + jnp.exp(sk - mi)) for pi, sk, mi in zip(p, sink, m)]
    ov = [_dot((pi * ii).astype(BF16), v[s]) for (s, _), pi, ii in zip(pairs, p, inv)]
    for (s, g), o in zip(pairs, ov):
        og = jnp.zeros((w, SWA_KV_DIM), F32)
        for kvh in range(SWA_KV_HEADS):
            og = jnp.where(lane_head == kvh, o[kvh * w:(kvh + 1) * w], og)
        o_ref[s * w:(s + 1) * w, g * SWA_KV_DIM:(g + 1) * SWA_KV_DIM] = og.astype(BF16)


def _swa_prompt(q, kv, bias, sink_rows, *, nq=8):
    t = q.shape[0]
    w = WINDOW
    rows = SWA_KV_HEADS * w
    return pl.pallas_call(
        functools.partial(_swa_prompt_kernel, nq=nq),
        out_shape=jax.ShapeDtypeStruct((t, SWA_Q_DIM), BF16),
        grid=(t // (nq * w),),
        in_specs=[
            pl.BlockSpec((nq * w, SWA_Q_DIM), lambda i: (i, 0)),
            pl.BlockSpec((nq * w, 2 * SWA_KV_DIM), lambda i: (i, 0)),
            pl.BlockSpec((w, 2 * SWA_KV_DIM), lambda i: (jnp.maximum(i * nq - 1, 0), 0)),
            pl.BlockSpec((SWA_GROUP, rows, 2 * w), lambda i: (0, 0, 0)),
            pl.BlockSpec((SWA_GROUP, rows, 1), lambda i: (0, 0, 0)),
        ],
        out_specs=pl.BlockSpec((nq * w, SWA_Q_DIM), lambda i: (i, 0)),
        compiler_params=_cparams("parallel"),
        name="swa_prompt",
    )(q, kv, kv, bias, sink_rows)


def _swa_sample_kernel(q_ref, kvn_ref, kb_ref, vb_ref, bias_b_ref, bias_n_ref, sink_ref, o_ref, ko_ref, vo_ref,
                       *, bb, t):
    gt = SWA_GROUP * t
    rows = SWA_KV_HEADS * gt
    w = kb_ref.shape[2]
    scale = SWA_HEAD_DIM ** -0.5
    lane_head = lax.broadcasted_iota(jnp.int32, (gt, SWA_KV_DIM), 1) // SWA_HEAD_DIM
    tok = lax.broadcasted_iota(jnp.int32, (rows, w), 0) % t
    keyj = lax.broadcasted_iota(jnp.int32, (rows, w), 1)
    valid_b = (tok + w - keyj) < WINDOW
    tok_n = lax.broadcasted_iota(jnp.int32, (rows, 1), 0) % t
    sink = sink_ref[...]
    bs = range(bb)
    toks = range(t)
    qall = [jnp.concatenate([jnp.where(lane_head == kvh, q_ref[b].astype(F32), 0.0) for kvh in range(SWA_KV_HEADS)],
                            axis=0) for b in bs]
    kvn = [kvn_ref[b] for b in bs]
    lb = [_dot(qall[b].astype(BF16), kb_ref[b].astype(BF16)) for b in bs]
    lb = [jnp.where(valid_b, lb[b] * scale + bias_b_ref[...], NEG_INF) for b in bs]
    ln = [[jnp.sum(qall[b] * kvn[b][j:j + 1, 0:SWA_KV_DIM], axis=-1, keepdims=True) for j in toks] for b in bs]
    ln = [[jnp.where(tok_n >= j, ln[b][j] * scale + bias_n_ref[:, j:j + 1], NEG_INF) for j in toks] for b in bs]
    m = [jnp.maximum(jnp.max(lb[b], axis=-1, keepdims=True), sink) for b in bs]
    m = [functools.reduce(jnp.maximum, ln[b], m[b]) for b in bs]
    pb = [jnp.exp(lb[b] - m[b]) for b in bs]
    pn = [[jnp.exp(ln[b][j] - m[b]) for j in toks] for b in bs]
    denom = [jnp.sum(pb[b], axis=-1, keepdims=True) + jnp.exp(sink - m[b]) for b in bs]
    inv = [1.0 / functools.reduce(jnp.add, pn[b], denom[b]) for b in bs]
    oall = [_dot_nt((pb[b] * inv[b]).astype(BF16), vb_ref[b].astype(BF16)) for b in bs]
    for b in bs:
        ob = oall[b]
        for j in toks:
            ob = ob + (pn[b][j] * inv[b]) * kvn[b][j:j + 1, SWA_KV_DIM:]
        og = jnp.zeros((gt, SWA_KV_DIM), F32)
        for kvh in range(SWA_KV_HEADS):
            og = jnp.where(lane_head == kvh, ob[kvh * gt:(kvh + 1) * gt], og)
        o_ref[b] = og.astype(BF16)
    pos = lax.broadcasted_iota(jnp.int32, (SWA_KV_DIM, w), 1)
    pad = jnp.zeros((w - SUBLANES, 2 * SWA_KV_DIM), F32)
    row8 = lax.broadcasted_iota(jnp.int32, (SUBLANES, 2 * SWA_KV_DIM), 0)
    for b in bs:
        last8 = jnp.zeros((SUBLANES, 2 * SWA_KV_DIM), F32)
        for j in toks:
            last8 = jnp.where(row8 == SUBLANES - t + j, kvn[b][j:j + 1], last8)
        tail_t = jnp.concatenate([pad, last8], axis=0).T
        ko_ref[b] = jnp.where(pos >= w - t, tail_t[:SWA_KV_DIM], pltpu.roll(kb_ref[b], w - t, axis=1))
        vo_ref[b] = jnp.where(pos >= w - t, tail_t[SWA_KV_DIM:], pltpu.roll(vb_ref[b], w - t, axis=1))


def _swa_sample(q, kvn, kbuf, vbuf, bias_b, bias_n, sink_rows, *, bb=8):
    b, gt, _ = q.shape
    t = kvn.shape[1]
    w = kbuf.shape[2]
    rows = SWA_KV_HEADS * gt
    blk = lambda i: (i, 0, 0)
    const = lambda i: (0, 0)
    cache = pl.BlockSpec((bb, SWA_KV_DIM, w), blk)
    return pl.pallas_call(
        functools.partial(_swa_sample_kernel, bb=bb, t=t),
        out_shape=(jax.ShapeDtypeStruct((b, gt, SWA_KV_DIM), BF16),
                   jax.ShapeDtypeStruct(kbuf.shape, F32), jax.ShapeDtypeStruct(vbuf.shape, F32)),
        grid=(b // bb,),
        in_specs=[
            pl.BlockSpec((bb, gt, SWA_KV_DIM), blk),
            pl.BlockSpec((bb, t, 2 * SWA_KV_DIM), blk),
            cache, cache,
            pl.BlockSpec((rows, w), const),
            pl.BlockSpec((rows, t), const),
            pl.BlockSpec((rows, 1), const),
        ],
        out_specs=(pl.BlockSpec((bb, gt, SWA_KV_DIM), blk), cache, cache),
        compiler_params=_cparams("parallel"),
        name="swa_sample",
    )(q, kvn, kbuf, vbuf, bias_b, bias_n, sink_rows)


def _mem_heads(q, mk, mv):
    scale = MEM_HEAD_DIM ** -0.5
    cols = [slice(h * MEM_HEAD_DIM, (h + 1) * MEM_HEAD_DIM) for h in range(MEM_HEADS)]
    x = [_dot_nt(q[:, sl], mk[:, sl]) * scale for sl in cols]
    m = [jnp.max(xi, axis=-1, keepdims=True) for xi in x]
    p = [jnp.exp(xi - mi) for xi, mi in zip(x, m)]
    inv = [1.0 / jnp.sum(pi, axis=-1, keepdims=True) for pi in p]
    return jnp.concatenate([_dot((pi * ii).astype(BF16), mv[:, sl]) for pi, ii, sl in zip(p, inv, cols)], axis=-1)


def _mem_prompt_kernel(q_ref, mk_ref, mv_ref, o_ref):
    o_ref[...] = _mem_heads(q_ref[...], mk_ref[...].astype(BF16), mv_ref[...].astype(BF16)).astype(BF16)


def _mem_prompt(q, mkv, *, tm=512):
    m = q.shape[0]
    return pl.pallas_call(
        _mem_prompt_kernel,
        out_shape=jax.ShapeDtypeStruct((m, MEM_DIM), BF16),
        grid=(m // tm,),
        in_specs=[
            pl.BlockSpec((tm, MEM_DIM), lambda i: (i, 0)),
            pl.BlockSpec((N_MEM, MEM_DIM), lambda i: (0, 0)),
            pl.BlockSpec((N_MEM, MEM_DIM), lambda i: (0, 1)),
        ],
        out_specs=pl.BlockSpec((tm, MEM_DIM), lambda i: (i, 0)),
        compiler_params=_cparams("parallel"),
        name="mem_prompt",
    )(q, mkv, mkv)


def _mem_sample_kernel(q_ref, mk_ref, mv_ref, o_ref, *, bb):
    scale = MEM_HEAD_DIM ** -0.5
    pairs = [(b, h) for b in range(bb) for h in range(MEM_HEADS)]
    rows = lambda h: pl.ds(h, N_MEM, stride=MEM_HEADS)
    cols = lambda h: slice(h * MEM_HEAD_DIM, (h + 1) * MEM_HEAD_DIM)
    q = [q_ref[b] for b in range(bb)]
    x = [_dot_nt(q[b][:, cols(h)], mk_ref[b, rows(h), :].astype(BF16)) * scale for b, h in pairs]
    m = [jnp.max(xi, axis=-1, keepdims=True) for xi in x]
    p = [jnp.exp(xi - mi) for xi, mi in zip(x, m)]
    inv = [1.0 / jnp.sum(pi, axis=-1, keepdims=True) for pi in p]
    o = [_dot((pi * ii).astype(BF16), mv_ref[b, rows(h), :].astype(BF16)) for (b, h), pi, ii in zip(pairs, p, inv)]
    for b in range(bb):
        o_ref[b] = jnp.concatenate(o[b * MEM_HEADS:(b + 1) * MEM_HEADS], axis=-1).astype(BF16)


def _mem_sample(q, mk, mv, *, bb=8):
    b, t, _ = q.shape
    blk = lambda i: (i, 0, 0)
    return pl.pallas_call(
        functools.partial(_mem_sample_kernel, bb=bb),
        out_shape=jax.ShapeDtypeStruct((b, t, MEM_DIM), BF16),
        grid=(b // bb,),
        in_specs=[
            pl.BlockSpec((bb, t, MEM_DIM), blk),
            pl.BlockSpec((bb, N_MEM * MEM_HEADS, MEM_HEAD_DIM), blk),
            pl.BlockSpec((bb, N_MEM * MEM_HEADS, MEM_HEAD_DIM), blk),
        ],
        out_specs=pl.BlockSpec((bb, t, MEM_DIM), blk),
        compiler_params=_cparams("parallel"),
        name="mem_sample",
    )(q, mk, mv)


def _head_sum(x, ones_ref):
    hi = x.astype(BF16)
    lo = (x - hi.astype(F32)).astype(BF16)
    ones = ones_ref[...]
    w = ones.shape[0]
    return jnp.concatenate([_dot(hi[:, c:c + w], ones) + _dot(lo[:, c:c + w], ones) for c in range(0, x.shape[1], w)],
                           axis=1)


def _token_shift(x_ref, pre_ref, start_ref, mu_ref, *, seq, tm, tile):
    x = x_ref[...]
    row = lax.broadcasted_iota(jnp.int32, x.shape, 0)
    shifted = pltpu.roll(x, 1, axis=0)
    if seq >= tm:
        is_start = (tile * tm) % seq == 0
        first_prev = jnp.where(is_start, start_ref[0], pre_ref[SUBLANES - 1:SUBLANES, :])
        prev = jnp.where(row == 0, first_prev, shifted)
    else:
        prev = jnp.where(row % seq == 0, start_ref[...], shifted)
    return x + mu_ref[...] * (prev - x)


def _rwkv_features(xs, w0_ref, ww2_ref, a0_ref, aw2_ref, gw2_ref, kk_ref, ka_ref, rk_ref, ones_ref):
    d = RWKV_DIM
    r = xs[:, 0:d]
    k = xs[:, d:2 * d]
    v = xs[:, 2 * d:3 * d]
    lw = xs[:, 3 * d:3 * d + LORA_W]
    la = xs[:, 3 * d + LORA_W:3 * d + LORA_W + LORA_A]
    lg = xs[:, 3 * d + LORA_W + LORA_A:]
    wpre = w0_ref[...] + _dot_hi(jnp.tanh(lw), ww2_ref[...])
    w = -jax.nn.softplus(-wpre) - 0.5
    log_decay = -jnp.exp(w)
    a = jax.nn.sigmoid(a0_ref[...] + _dot_hi(la, aw2_ref[...]))
    g = _dot_hi(jax.nn.sigmoid(lg), gw2_ref[...])
    kk = k * kk_ref[...]
    kk = kk / jnp.maximum(jnp.sqrt(_head_sum(kk * kk, ones_ref)), 1e-12)
    kh = k * (1.0 + (a - 1.0) * ka_ref[...])
    bonus = _head_sum(r * kh * rk_ref[...], ones_ref) * v
    return r, log_decay, kh, v, kk, kk * a, g, bonus


def _rwkv_prep_kernel(x_ref, pre_ref, start_ref, mu_ref, w0_ref, ww2_ref, a0_ref, aw2_ref, gw2_ref, kk_ref, ka_ref,
                      rk_ref, ones_ref, *outs, seq, tm):
    xs = _token_shift(x_ref, pre_ref, start_ref, mu_ref, seq=seq, tm=tm, tile=pl.program_id(0))
    feats = _rwkv_features(xs, w0_ref, ww2_ref, a0_ref, aw2_ref, gw2_ref, kk_ref, ka_ref, rk_ref, ones_ref)
    for o_ref, val in zip(outs, feats):
        o_ref[...] = val


def _rwkv_prep(x, shift0, p, ones, *, seq, tm=256):
    m = x.shape[0]
    row = lambda i: (i, 0)
    const = lambda i: (0, 0)
    vec = lambda n: pl.BlockSpec((1, n), const)
    out = jax.ShapeDtypeStruct((m, RWKV_DIM), F32)
    if seq >= tm:
        assert seq % tm == 0
        start = shift0
        start_spec = pl.BlockSpec((1, 1, RWKV_IN), lambda i: ((i * tm) // seq, 0, 0))
    else:
        assert tm % seq == 0
        start = jnp.repeat(shift0[:, 0], seq, axis=0)
        start_spec = pl.BlockSpec((tm, RWKV_IN), row)
    pre_blocks = tm // SUBLANES
    return pl.pallas_call(
        functools.partial(_rwkv_prep_kernel, seq=seq, tm=tm),
        out_shape=(out,) * 8,
        grid=(m // tm,),
        in_specs=[
            pl.BlockSpec((tm, RWKV_IN), row),
            pl.BlockSpec((SUBLANES, RWKV_IN), lambda i: (jnp.maximum(i * pre_blocks - 1, 0), 0)),
            start_spec,
            vec(RWKV_IN), vec(RWKV_DIM),
            pl.BlockSpec((LORA_W, RWKV_DIM), const),
            vec(RWKV_DIM),
            pl.BlockSpec((LORA_A, RWKV_DIM), const),
            pl.BlockSpec((LORA_G, RWKV_DIM), const),
            vec(RWKV_DIM), vec(RWKV_DIM), vec(RWKV_DIM),
            pl.BlockSpec(ones.shape, const),
        ],
        out_specs=(pl.BlockSpec((tm, RWKV_DIM), row),) * 8,
        compiler_params=_cparams("parallel"),
        name="rwkv_prep",
    )(x, x, start, p["mu"], p["w0"], p["w_w2"], p["a0"], p["a_w2"], p["g_w2"], p["k_k"], p["k_a"], p["r_k"], ones)


def _rwkv_lanes_kernel(r_ref, lw_ref, k_ref, v_ref, kk_ref, kka_ref, s_ref, y_ref, so_ref, v_scr, y_scr, *, t):
    n = RWKV_HEAD_DIM
    nb = s_ref.shape[-1]
    heads = range(2)

    def token_major(ref, j):
        return ref[pl.ds(j, nb, stride=t), :].T

    for j in range(t):
        v_scr[...] = token_major(v_ref, j)
        r_t, k_t, kk_t, kka_t = (token_major(ref, j) for ref in (r_ref, k_ref, kk_ref, kka_ref))
        w_t = jnp.exp(token_major(lw_ref, j))
        src = s_ref if j == 0 else so_ref

        def value_group(g, carry):
            rows = pl.multiple_of(g * SUBLANES, SUBLANES)
            ys = [[] for _ in heads]
            vg = [v_scr[pl.ds(h * n + rows, SUBLANES), :] for h in heads]
            for i in range(SUBLANES):
                for h in heads:
                    f = slice(h * n, (h + 1) * n)
                    s = src[h, rows + i]
                    sa = jnp.sum(s * (-kk_t[f]), axis=0, keepdims=True)
                    s = s * w_t[f] + sa * kka_t[f] + vg[h][i:i + 1] * k_t[f]
                    so_ref[h, rows + i] = s
                    ys[h].append(jnp.sum(s * r_t[f], axis=0, keepdims=True))
            for h in heads:
                y_scr[pl.ds(h * n + rows, SUBLANES), :] = jnp.concatenate(ys[h], axis=0)
            return carry

        lax.fori_loop(0, n // SUBLANES, value_group, 0)
        y_ref[j] = y_scr[...].T


def _rwkv_lanes(r, lw, k, v, kk, kka, s0, *, t):
    m, d = r.shape
    nb = m // t
    n = RWKV_HEAD_DIM
    assert nb == LANES, "one batch per lane"
    tok = pl.BlockSpec((m, LANES), lambda p: (0, p))
    st = pl.BlockSpec((2, n, n, nb), lambda p: (p, 0, 0, 0))
    return pl.pallas_call(
        functools.partial(_rwkv_lanes_kernel, t=t),
        out_shape=(jax.ShapeDtypeStruct((t, nb, d), F32), jax.ShapeDtypeStruct((RWKV_HEADS, n, n, nb), F32)),
        grid=(RWKV_HEADS // 2,),
        in_specs=[tok] * 6 + [st],
        out_specs=(pl.BlockSpec((t, nb, LANES), lambda p: (0, 0, p)), st),
        scratch_shapes=[pltpu.VMEM((LANES, nb), F32), pltpu.VMEM((LANES, nb), F32)],
        compiler_params=_cparams("parallel"),
        name="rwkv_lanes",
    )(r, lw, k, v, kk, kka, s0)


CHUNK = 64
GROUP_HEADS = 4
GROUP_W = GROUP_HEADS * RWKV_HEAD_DIM
N_GROUPS = RWKV_HEADS // GROUP_HEADS
(MASK_SAME, MASK_STRICT, MASK_INCL, MASK_LEVEL0) = (0, 1, 2, 3)
N_LEVELS = int(math.log2(CHUNK))


def _chunk_masks():
    i = np.arange(GROUP_W)
    same = (i[:, None] // CHUNK) == (i[None, :] // CHUNK)
    masks = [same, same & (i[None, :] < i[:, None]), same & (i[None, :] <= i[:, None])]
    for lvl in range(N_LEVELS):
        m = 1 << lvl
        masks.append(((i[:, None] // (2 * m)) == (i[None, :] // (2 * m))) & ((i[:, None] // m) != (i[None, :] // m))
                     & (i[None, :] < i[:, None]))
    return np.stack(masks).astype(np.float32)


def _rwkv_prompt_kernel(x_ref, pre_ref, start_ref, mu_ref, w0_ref, ww2_ref, a0_ref, aw2_ref, gw2_ref, kk_p_ref,
                        ka_ref, rk_ref, ones_ref, lnw_ref, lnb_ref, st0_ref, tri_ref, eye_ref, mask_ref,
                        *rest, seq, tm, n_tiles, n_riders):
    rider_in, (o_ref, sto_ref), rider_out = rest[:n_riders], rest[n_riders:n_riders + 2], rest[n_riders + 2:-11]
    st_scr, xs_ref, y_ref, r_set, lw_set, k_set, v_set, kk_set, kka_set, g_set, bonus_set = rest[-11:]
    for src, dst in zip(rider_in, rider_out):
        dst[...] = src[...].astype(BF16)
    step = pl.program_id(0)
    cur = (step + 1) % 2
    nxt = step % 2
    sets = (r_set, lw_set, k_set, v_set, kk_set, kka_set, g_set, bonus_set)
    r_ref, lw_ref, k_ref, v_ref, kk_ref, kka_ref = (s.at[cur] for s in sets[:6])

    @pl.when(step == 0)
    def _():
        st_scr[...] = st0_ref[...]
        for s in sets:
            s[1] = jnp.zeros(s.shape[1:], F32)

    xs_ref[...] = _token_shift(x_ref, pre_ref, start_ref, mu_ref, seq=seq, tm=tm, tile=jnp.minimum(step, n_tiles - 1))
    n_chunks = tm // CHUNK

    piece = 2 * CHUNK
    def features(c):
        rows = slice(c * piece, (c + 1) * piece)
        feats = _rwkv_features(xs_ref[rows, :], w0_ref, ww2_ref, a0_ref, aw2_ref, gw2_ref, kk_p_ref, ka_ref, rk_ref,
                               ones_ref)
        for s, val in zip(sets, feats):
            s[nxt, rows, :] = val

    eye = eye_ref[...]
    tri = tri_ref[...]
    tile_rows = lambda x: jnp.concatenate([x] * GROUP_HEADS, axis=0)
    block_diag = lambda x: (tile_rows(x) * mask_ref[MASK_SAME]).astype(BF16)

    chains = [(slice(c * CHUNK, (c + 1) * CHUNK), slice(g * GROUP_W, (g + 1) * GROUP_W))
              for c in range(n_chunks) for g in range(N_GROUPS)]
    each = lambda f, *cols: [f(*args) for args in zip(*cols)]
    same, strict, incl = mask_ref[MASK_SAME], mask_ref[MASK_STRICT], mask_ref[MASK_INCL]

    def cum_decay(lw):
        h1 = lw.astype(BF16)
        r1 = lw - h1.astype(F32)
        h2 = r1.astype(BF16)
        h3 = (r1 - h2.astype(F32)).astype(BF16)
        return _dot(tri, h1) + _dot(tri, h2) + _dot(tri, h3)

    lw = [lw_ref[rows, sl] for rows, sl in chains]
    kka = [kka_ref[rows, sl] for rows, sl in chains]
    k = [k_ref[rows, sl] for rows, sl in chains]
    cum = each(cum_decay, lw)
    cum_last = each(lambda c: c[CHUNK - 1:CHUNK, :], cum)
    p_inv = each(lambda c: jnp.exp(-c), cum)
    p_tail = each(lambda c, cl: jnp.exp(cl - c), cum, cum_last)
    a_bd = [block_diag(-kk_ref[rows, sl] * jnp.exp(c - l)) for (rows, sl), c, l in zip(chains, cum, lw)]
    r_f = [tile_rows(r_ref[rows, sl] * jnp.exp(c)) * same for (rows, sl), c in zip(chains, cum)]
    r_bd = each(lambda x: x.astype(BF16), r_f)
    v_bd = [block_diag(v_ref[rows, sl]) for rows, sl in chains]
    b_rep = each(lambda x, p: tile_rows((x * p).astype(BF16)), kka, p_inv)
    k_rep = each(lambda x, p: tile_rows((x * p).astype(BF16)), k, p_inv)
    bh_rep = each(lambda x, p: tile_rows(x * p), kka, p_tail)
    kh_rep = each(lambda x, p: tile_rows(x * p), k, p_tail)

    l_ab_f = each(lambda a, b: _dot_nt(a, b) * strict, a_bd, b_rep)
    l_ab = each(lambda x: x.astype(BF16), l_ab_f)
    l_ak = each(lambda a, b: (_dot_nt(a, b) * strict).astype(BF16), a_bd, k_rep)
    m_rb = each(lambda a, b: (_dot_nt(a, b) * incl).astype(BF16), r_bd, b_rep)
    m_rk = each(lambda a, b: (_dot_nt(a, b) * incl).astype(BF16), r_bd, k_rep)
    bh_t = each(lambda x: (x.T * same).astype(BF16), bh_rep)
    kh_t = each(lambda x: (x.T * same).astype(BF16), kh_rep)

    pending = list(range(tm // piece))

    def next_features():
        if pending:
            features(pending.pop(0))

    d = each(lambda l: eye + l * mask_ref[MASK_LEVEL0], l_ab_f)
    for lvl in range(1, N_LEVELS):
        d_b = each(lambda x: x.astype(BF16), d)
        x = each(lambda l, db: (_dot(l, db) * mask_ref[MASK_LEVEL0 + lvl]).astype(BF16), l_ab, d_b)
        next_features()
        d = each(lambda dd, db, xx: dd + _dot(db, xx), d, d_b, x)
    t_b = each(lambda x: x.astype(BF16), d)
    while pending:
        next_features()

    wm = each(lambda a, b, vv: _dot(jnp.concatenate([a, b], axis=0), vv), l_ak, m_rk, v_bd)
    twa = each(lambda t, w, a: _dot(t, jnp.concatenate([w[:GROUP_W].astype(BF16), a], axis=1)).astype(BF16),
               t_b, wm, a_bd)
    ry = each(_dot, m_rb, twa)
    mn = each(_dot, bh_t, twa)
    khv = each(_dot, kh_t, v_bd)
    y0 = each(lambda a, w: a[:, :GROUP_W] + w[GROUP_W:], ry, wm)
    n_x = each(lambda a, b: a[:, :GROUP_W] + b, mn, khv)
    mr = each(lambda a, cl, rf, b: jnp.concatenate(
        [(eye * jnp.exp(cl) + a[:, GROUP_W:]).astype(BF16), (rf + b[:, GROUP_W:]).astype(BF16)], axis=0),
        mn, cum_last, r_f, ry)

    st = [st_scr[g] for g in range(N_GROUPS)]
    for i, (rows, sl) in enumerate(chains):
        g = i % N_GROUPS
        ys = _dot(mr[i], st[g].astype(BF16))
        st[g] = ys[:GROUP_W] + n_x[i]
        y_bd = ys[GROUP_W:] + y0[i]
        y = y_bd[0:CHUNK]
        for h in range(1, GROUP_HEADS):
            y = y + y_bd[h * CHUNK:(h + 1) * CHUNK]
        y_ref[rows, sl] = y
    for g in range(N_GROUPS):
        st_scr[g] = st[g]
    o_ref[...] = _rwkv_output(y_ref[...], bonus_set[cur], g_set[cur], lnw_ref, lnb_ref, ones_ref)

    @pl.when(step == pl.num_programs(0) - 1)
    def _():
        sto_ref[...] = st_scr[...]


def _rwkv_prompt(x, shift0, s0, p, ones, *, riders=(), chunks_per_step=4):
    t = x.shape[0]
    d = RWKV_DIM
    n = RWKV_HEAD_DIM
    tt = CHUNK * chunks_per_step
    assert CHUNK == n and t % tt == 0
    st0 = jnp.einsum("ghvk,hj->ghkjv", s0.reshape(N_GROUPS, GROUP_HEADS, n, n), jnp.eye(GROUP_HEADS, dtype=F32))
    st0 = st0.reshape(N_GROUPS, GROUP_W, GROUP_W)
    tri = jnp.asarray(np.tril(np.ones((CHUNK, CHUNK), np.float32)), BF16)
    eye = jnp.eye(GROUP_W, dtype=F32)
    masks = jnp.asarray(_chunk_masks())
    const = lambda c: (0, 0)
    vec = lambda width: pl.BlockSpec((1, width), const)
    st_spec = pl.BlockSpec((N_GROUPS, GROUP_W, GROUP_W), lambda c: (0, 0, 0))
    feature_set = pltpu.VMEM((2, tt, d), F32)
    pre_blocks = tt // SUBLANES
    n_tiles = t // tt
    fill = lambda c: jnp.minimum(c, n_tiles - 1)
    rider_specs = _rider_specs(riders, n_tiles + 1, lambda c: c)
    o, st, *rounded = pl.pallas_call(
        functools.partial(_rwkv_prompt_kernel, seq=t, tm=tt, n_tiles=n_tiles, n_riders=len(riders)),
        out_shape=[jax.ShapeDtypeStruct((t, d), BF16), jax.ShapeDtypeStruct((N_GROUPS, GROUP_W, GROUP_W), F32)]
        + [jax.ShapeDtypeStruct(arr.shape, BF16) for arr in riders],
        grid=(n_tiles + 1,),
        in_specs=[
            pl.BlockSpec((tt, RWKV_IN), lambda c: (fill(c), 0)),
            pl.BlockSpec((SUBLANES, RWKV_IN), lambda c: (jnp.maximum(fill(c) * pre_blocks - 1, 0), 0)),
            pl.BlockSpec((1, 1, RWKV_IN), lambda c: (0, 0, 0)),
            vec(RWKV_IN), vec(d),
            pl.BlockSpec((LORA_W, d), const),
            vec(d),
            pl.BlockSpec((LORA_A, d), const),
            pl.BlockSpec((LORA_G, d), const),
            vec(d), vec(d), vec(d),
            pl.BlockSpec(ones.shape, const),
            vec(d), vec(d),
            st_spec,
            pl.BlockSpec((CHUNK, CHUNK), const),
            pl.BlockSpec((GROUP_W, GROUP_W), const),
            pl.BlockSpec(masks.shape, lambda c: (0, 0, 0)),
        ] + rider_specs,
        out_specs=[pl.BlockSpec((tt, d), lambda c: (jnp.maximum(c - 1, 0), 0)), st_spec] + rider_specs,
        scratch_shapes=[pltpu.VMEM((N_GROUPS, GROUP_W, GROUP_W), F32), pltpu.VMEM((tt, RWKV_IN), F32),
                        pltpu.VMEM((tt, d), F32)] + [feature_set] * 8,
        compiler_params=_cparams("arbitrary"),
        name="rwkv_prompt",
    )(x, x, shift0, p["mu"], p["w0"], p["w_w2"], p["a0"], p["a_w2"], p["g_w2"], p["k_k"], p["k_a"], p["r_k"], ones,
      p["ln_w"], p["ln_b"], st0, tri, eye, masks, *riders)
    st5 = st.reshape(N_GROUPS, GROUP_HEADS, n, GROUP_HEADS, n)
    s_new = jnp.einsum("ghkjv,hj->ghvk", st5, jnp.eye(GROUP_HEADS, dtype=F32)).reshape(RWKV_HEADS, n, n)
    return o, s_new, rounded


def _rwkv_output(y, bonus, gate, lnw_ref, lnb_ref, ones_ref):
    inv_n = 1.0 / RWKV_HEAD_DIM
    mu = _head_sum(y, ones_ref) * inv_n
    dlt = y - mu
    var = _head_sum(dlt * dlt, ones_ref) * inv_n
    yn = dlt * lax.rsqrt(var + GN_EPS) * lnw_ref[...] + lnb_ref[...]
    return ((yn + bonus) * gate).astype(BF16)


def _rwkv_post_kernel(y_ref, bonus_ref, g_ref, lnw_ref, lnb_ref, ones_ref, o_ref):
    o_ref[...] = _rwkv_output(y_ref[...], bonus_ref[...], g_ref[...], lnw_ref, lnb_ref, ones_ref)


def _rwkv_post(y, bonus, g, lnw, lnb, ones, *, tm=256):
    m = y.shape[0]
    row = lambda i: (i, 0)
    const = lambda i: (0, 0)
    tile = pl.BlockSpec((tm, RWKV_DIM), row)
    return pl.pallas_call(
        _rwkv_post_kernel,
        out_shape=jax.ShapeDtypeStruct((m, RWKV_DIM), BF16),
        grid=(m // tm,),
        in_specs=[tile, tile, tile, pl.BlockSpec((1, RWKV_DIM), const), pl.BlockSpec((1, RWKV_DIM), const),
                  pl.BlockSpec(ones.shape, const)],
        out_specs=tile,
        compiler_params=_cparams("parallel"),
        name="rwkv_post",
    )(y, bonus, g, lnw, lnb, ones)


GATE_BLOCK = math.gcd(PROJ_DIM, D_MODEL)


def _merge_kernel(h_ref, g_ref, os_ref, or_ref, om_ref, wos_ref, wor_ref, wom_ref, wout_ref, *rest, parts):
    gate_refs, (o_ref, u_ref, acc_ref) = rest[:N_BRANCH * parts], rest[N_BRANCH * parts:]
    j = pl.program_id(1)

    @pl.when(j == 0)
    def _():
        u_ref[...] = _rms(h_ref[...], g_ref[...]).astype(BF16)
        acc_ref[...] = jnp.zeros_like(acc_ref)

    u = u_ref[...]
    merged = None
    for b, (x_ref, w_ref) in enumerate(((os_ref, wos_ref), (or_ref, wor_ref), (om_ref, wom_ref))):
        gate = jnp.concatenate([_dot(u, gate_refs[b * parts + c][...]) for c in range(parts)], axis=1)
        term = jax.nn.sigmoid(gate) * _dot(x_ref[...], w_ref[...])
        merged = term if merged is None else merged + term
    acc_ref[...] += _dot(merged.astype(BF16), wout_ref[...])

    @pl.when(j == pl.num_programs(1) - 1)
    def _():
        o_ref[...] = h_ref[...] + acc_ref[...]


def _merge(h, g, o_swa, o_rw, o_mem, w_in, wo_swa, wo_rw, wo_mem, w_out, *, tm=512, tn=512):
    m, d = h.shape
    nt = d // tn
    parts = tn // GATE_BLOCK
    g0 = PROJ_DIM // GATE_BLOCK
    row = lambda i, j: (i, 0)
    col = lambda i, j: (0, j)
    gate_specs = [pl.BlockSpec((d, GATE_BLOCK), functools.partial(
        lambda i, j, off: (0, off + j * parts), off=g0 + b * (d // GATE_BLOCK) + c))
        for b in range(N_BRANCH) for c in range(parts)]
    return pl.pallas_call(
        functools.partial(_merge_kernel, parts=parts),
        out_shape=jax.ShapeDtypeStruct((m, d), F32),
        grid=(m // tm, nt),
        in_specs=[
            pl.BlockSpec((tm, d), row),
            pl.BlockSpec((1, d), lambda i, j: (0, 0)),
            pl.BlockSpec((tm, SWA_Q_DIM), row),
            pl.BlockSpec((tm, RWKV_DIM), row),
            pl.BlockSpec((tm, MEM_DIM), row),
            pl.BlockSpec((SWA_Q_DIM, tn), col),
            pl.BlockSpec((RWKV_DIM, tn), col),
            pl.BlockSpec((MEM_DIM, tn), col),
            pl.BlockSpec((tn, d), lambda i, j: (j, 0)),
        ] + gate_specs,
        out_specs=pl.BlockSpec((tm, d), row),
        scratch_shapes=[pltpu.VMEM((tm, d), BF16), pltpu.VMEM((tm, d), F32)],
        compiler_params=_cparams("parallel", "arbitrary"),
        name="merge",
    )(h, g, o_swa, o_rw, o_mem, wo_swa, wo_rw, wo_mem, w_out, *([w_in] * (N_BRANCH * parts)))


def _t5_bucket(dist):
    max_exact = N_BUCKETS // 2
    d = np.maximum(dist, 0)
    log_ratio = (np.log(np.maximum(d, 1).astype(np.float32) / np.float32(max_exact))
                 / np.float32(math.log(MAX_DISTANCE / max_exact)))
    large = np.minimum(max_exact + (log_ratio * (N_BUCKETS - max_exact)).astype(np.int32), N_BUCKETS - 1)
    return np.where(d < max_exact, d, large).astype(np.int32)


def _rel_bias(table, dist):
    onehot = np.eye(N_BUCKETS, dtype=np.float32)[_t5_bucket(dist).reshape(-1)]
    bias = jnp.einsum("nb,bh->hn", jnp.asarray(onehot), table, precision=lax.Precision.HIGHEST)
    return bias.reshape(SWA_HEADS, *dist.shape)


def _rwkv_branch(xr, shift0, s0, p, ones, riders=()):
    b, t, _ = xr.shape
    flat = lambda z: z.reshape(b * t, z.shape[-1])
    if b == 1:
        o, s_new, rounded = _rwkv_prompt(flat(xr), shift0, s0[0], p, ones, riders=riders)
        return o, s_new[None], rounded
    assert b == LANES and not riders, "short sequences are batched one per lane"
    r, w, k, v, kk, kka, g, bonus = _rwkv_prep(flat(xr), shift0, p, ones, seq=t)
    y, s_new = _rwkv_lanes(r, w, k, v, kk, kka, jnp.transpose(s0, (1, 2, 3, 0)), t=t)
    o = _rwkv_post(flat(jnp.transpose(y, (1, 0, 2))), bonus, g, p["ln_w"], p["ln_b"], ones)
    return o, jnp.transpose(s_new, (3, 0, 1, 2))


def kernel(x_prompt, mem_prompt, x_sample, cache_swa_k, cache_swa_v, state_rwkv, state_rwkv_shift, cache_mem_k, cache_mem_v, ffn1_norm, ffn1_wi, ffn1_wo, mix_norm, w_in, swa_sinks, rel_bias_table, rwkv_mu, rwkv_w0, rwkv_w_w2, rwkv_a0, rwkv_a_w2, rwkv_g_w2, rwkv_k_k, rwkv_k_a, rwkv_r_k, rwkv_ln_w, rwkv_ln_b, mem_norm, w_mem_kv, w_o_swa, w_o_rwkv, w_o_mem, w_out, ffn2_norm, ffn2_wi, ffn2_wo, final_norm):
    assert ffn1_wi.shape[0] == 1, "single-layer trunk"
    bp, tp, d = x_prompt.shape
    bs, ts, _ = x_sample.shape
    assert bp == 1
    row = lambda z: z.reshape(1, -1).astype(F32)

    g1, gm, g2, gf = row(ffn1_norm[0]), row(mix_norm[0]), row(ffn2_norm[0]), row(final_norm)
    rp = {
        "mu": row(rwkv_mu[0]), "w0": row(rwkv_w0[0]), "w_w2": rwkv_w_w2[0], "a0": row(rwkv_a0[0]),
        "a_w2": rwkv_a_w2[0], "g_w2": rwkv_g_w2[0], "k_k": row(rwkv_k_k[0]), "k_a": row(rwkv_k_a[0]),
        "r_k": row(rwkv_r_k[0]), "ln_w": row(rwkv_ln_w[0]), "ln_b": row(rwkv_ln_b[0]),
    }
    seg = np.arange(GROUP_W) // RWKV_HEAD_DIM
    ones = jnp.asarray(seg[:, None] == seg[None, :], dtype=BF16)
    sinks = swa_sinks[0].astype(F32)
    table = rel_bias_table.astype(F32)

    xp = x_prompt.reshape(tp, d)
    xs = x_sample.reshape(bs * ts, d)
    hs, wg1, wu1, wo1 = _ffn(xs, g1, ffn1_wi[0], ffn1_wo[0], gf, final_norm=False, emit_weights=True, tf=256)
    hp, w_in_b, wo_swa, wo_rw, wo_mem, w_out_b = _ffn(
        xp, g1, (wg1, wu1), wo1, gf, final_norm=False,
        riders=(w_in[0], w_o_swa[0], w_o_rwkv[0], w_o_mem[0], w_out[0]))
    wo_swa = wo_swa.reshape(SWA_KV_HEADS, SWA_GROUP, SWA_HEAD_DIM, d).transpose(1, 0, 2, 3).reshape(SWA_Q_DIM, d)
    w_q = w_in_b[:, :SWA_Q_DIM].reshape(d, SWA_KV_HEADS, SWA_GROUP, SWA_HEAD_DIM).transpose(0, 2, 1, 3).reshape(d, SWA_Q_DIM)
    qp, kvp, xrp, qmp = _inproj(hp, gm, w_q, w_in_b)
    qs, kvs, xrs, qms = _inproj(hs, gm, w_q, w_in_b)

    w = WINDOW
    dist_p = np.arange(w)[:, None] + w - np.arange(2 * w)[None, :]
    bias_p = _rel_bias(table, dist_p).reshape(SWA_KV_HEADS, SWA_GROUP, w, 2 * w).transpose(1, 0, 2, 3)
    bias_p = bias_p.reshape(SWA_GROUP, SWA_KV_HEADS * w, 2 * w)
    sink_p = jnp.repeat(sinks.reshape(SWA_KV_HEADS, SWA_GROUP).T, w, axis=1).reshape(SWA_GROUP, SWA_KV_HEADS * w, 1)
    o_swa_p = _swa_prompt(qp, kvp, bias_p, sink_p)

    wbuf = cache_swa_k.shape[2]
    dist_s = np.arange(ts)[:, None] + wbuf - np.arange(wbuf + ts)[None, :]
    bias_s = _rel_bias(table, dist_s).reshape(SWA_HEADS * ts, wbuf + ts)
    sink_rows = jnp.repeat(sinks, ts).reshape(SWA_HEADS * ts, 1)
    qs_gt = qs.reshape(bs, ts, SWA_GROUP, SWA_KV_DIM).transpose(0, 2, 1, 3).reshape(bs, SWA_GROUP * ts, SWA_KV_DIM)
    kbuf = cache_swa_k[0].reshape(bs, wbuf, SWA_KV_DIM).transpose(0, 2, 1)
    vbuf = cache_swa_v[0].reshape(bs, wbuf, SWA_KV_DIM).transpose(0, 2, 1)
    o_swa_s, knew_t, vnew_t = _swa_sample(qs_gt, kvs.reshape(bs, ts, 2 * SWA_KV_DIM), kbuf, vbuf,
                                          bias_s[:, :wbuf], bias_s[:, wbuf:], sink_rows)
    o_swa_s = o_swa_s.reshape(bs, SWA_GROUP, ts, SWA_KV_DIM).transpose(0, 2, 1, 3).reshape(bs * ts, SWA_Q_DIM)

    zero_shift = jnp.zeros((bp, 1, RWKV_IN), F32)
    zero_state = jnp.zeros((bp, RWKV_HEADS, RWKV_HEAD_DIM, RWKV_HEAD_DIM), F32)
    o_rw_p, state_p, (wi2, wo2) = _rwkv_branch(xrp.reshape(bp, tp, RWKV_IN), zero_shift, zero_state, rp, ones,
                                               riders=(ffn2_wi[0], ffn2_wo[0]))
    o_rw_s, state_s = _rwkv_branch(xrs.reshape(bs, ts, RWKV_IN), state_rwkv_shift[0], state_rwkv[0], rp, ones)

    mkv = _norm_matmul(mem_prompt.reshape(N_MEM, d), row(mem_norm[0]), w_mem_kv[0].astype(BF16))
    o_mem_p = _mem_prompt(qmp, mkv)
    o_mem_s = _mem_sample(qms.reshape(bs, ts, MEM_DIM), cache_mem_k[0].reshape(bs, N_MEM * MEM_HEADS, MEM_HEAD_DIM),
                          cache_mem_v[0].reshape(bs, N_MEM * MEM_HEADS, MEM_HEAD_DIM)).reshape(bs * ts, MEM_DIM)

    hp = _merge(hp, gm, o_swa_p, o_rw_p, o_mem_p, w_in_b, wo_swa, wo_rw, wo_mem, w_out_b)
    hs = _merge(hs, gm, o_swa_s, o_rw_s, o_mem_s, w_in_b, wo_swa, wo_rw, wo_mem, w_out_b)
    y_prompt = _ffn(hp, g2, wi2, wo2, gf, final_norm=True).reshape(bp, tp, d)
    y_sample = _ffn(hs, g2, wi2, wo2, gf, final_norm=True).reshape(bs, ts, d)

    wp = min(w, tp)
    p_k = kvp[tp - wp:, :SWA_KV_DIM].reshape(1, bp, wp, SWA_KV_HEADS, SWA_HEAD_DIM)
    p_v = kvp[tp - wp:, SWA_KV_DIM:].reshape(1, bp, wp, SWA_KV_HEADS, SWA_HEAD_DIM)
    p_mk = mkv[:, :MEM_DIM].reshape(1, bp, N_MEM, MEM_HEADS, MEM_HEAD_DIM)
    p_mv = mkv[:, MEM_DIM:].reshape(1, bp, N_MEM, MEM_HEADS, MEM_HEAD_DIM)
    s_k = knew_t.transpose(0, 2, 1).reshape(1, bs, wbuf, SWA_KV_HEADS, SWA_HEAD_DIM)
    s_v = vnew_t.transpose(0, 2, 1).reshape(1, bs, wbuf, SWA_KV_HEADS, SWA_HEAD_DIM)
    return (y_prompt, y_sample,
            p_k, p_v, state_p[None], xrp[tp - 1:].reshape(1, bp, 1, RWKV_IN), p_mk, p_mv,
            s_k, s_v, state_s[None], xrs.reshape(bs, ts, RWKV_IN)[:, ts - 1:][None])
```

```python
import functools
import math

import jax
import jax.numpy as jnp
import numpy as np
from jax import lax
from jax.experimental import pallas as pl
from jax.experimental.pallas import tpu as pltpu

F32 = jnp.float32
BF16 = jnp.bfloat16

D_MODEL = 2048
SWA_HEADS = 16
SWA_KV_HEADS = 4
SWA_GROUP = SWA_HEADS // SWA_KV_HEADS
SWA_HEAD_DIM = 64
SWA_Q_DIM = SWA_HEADS * SWA_HEAD_DIM
SWA_KV_DIM = SWA_KV_HEADS * SWA_HEAD_DIM
WINDOW = 128
N_BUCKETS = 32
MAX_DISTANCE = 128
RWKV_HEADS = 8
RWKV_HEAD_DIM = 64
RWKV_DIM = RWKV_HEADS * RWKV_HEAD_DIM
LORA_W = 64
LORA_A = 64
LORA_G = 128
RWKV_IN = 3 * RWKV_DIM + LORA_W + LORA_A + LORA_G
N_MEM = 256
MEM_HEADS = 4
MEM_HEAD_DIM = 128
MEM_DIM = MEM_HEADS * MEM_HEAD_DIM
N_BRANCH = 3
PROJ_DIM = SWA_Q_DIM + 2 * SWA_KV_DIM + RWKV_IN + MEM_DIM
NORM_EPS = 1e-6
GN_EPS = 64e-5
NEG_INF = -1e30

LANES = 128
SUBLANES = 8
VMEM_LIMIT = 56 * 1024 * 1024


def _cparams(*sem):
    return pltpu.CompilerParams(dimension_semantics=sem, vmem_limit_bytes=VMEM_LIMIT)


def _rms(x, g):
    return x * lax.rsqrt(jnp.mean(x * x, axis=-1, keepdims=True) + NORM_EPS) * g


def _dot(a, b):
    return jnp.dot(a, b, preferred_element_type=F32)


def _dot_nt(a, b):
    return lax.dot_general(a, b, (((1,), (1,)), ((), ())), preferred_element_type=F32)


def _dot_hi(a, b):
    a_hi = a.astype(BF16)
    b_hi = b.astype(BF16)
    a_lo = (a - a_hi.astype(F32)).astype(BF16)
    b_lo = (b - b_hi.astype(F32)).astype(BF16)
    return _dot(a_hi, b_hi) + _dot(a_hi, b_lo) + _dot(a_lo, b_hi)


def _ffn_kernel(x_ref, g_ref, wg_ref, wu_ref, wo_ref, gf_ref, *rest, final_norm, n_riders, emit_weights):
    rider_in, (o_ref, *extra_out), (xn_ref, acc_ref) = rest[:n_riders], rest[n_riders:-2], rest[-2:]
    for src, dst in zip(rider_in, extra_out[:n_riders]):
        dst[...] = src[...].astype(BF16)
    j = pl.program_id(1)

    @pl.when(j == 0)
    def _():
        xn_ref[...] = _rms(x_ref[...], g_ref[...]).astype(BF16)
        acc_ref[...] = jnp.zeros_like(acc_ref)

    wg, wu, wo = wg_ref[...], wu_ref[...], wo_ref[...]
    if emit_weights:
        wg, wu, wo = wg.astype(BF16), wu.astype(BF16), wo.astype(BF16)
        for dst, val in zip(extra_out[n_riders:], (wg, wu, wo)):
            dst[...] = val
    xn = xn_ref[...]
    gate = _dot(xn, wg)
    up = _dot(xn, wu)
    act = (gate * jax.nn.sigmoid(gate)) * up
    acc_ref[...] += _dot(act.astype(BF16), wo)

    @pl.when(j == pl.num_programs(1) - 1)
    def _():
        h = x_ref[...] + 0.5 * acc_ref[...]
        if final_norm:
            h = _rms(h, gf_ref[...])
        o_ref[...] = h


def _rider_tiling(shape, steps):
    rows, cols = shape
    best, best_score = (1, 1), (0, 0)
    for nr in range(1, rows // 16 + 1):
        if rows % nr or (rows // nr) % 16:
            continue
        for nc in range(1, cols // LANES + 1):
            if cols % nc or (cols // nc) % LANES or nr * nc > steps:
                continue
            score = (min(cols // nc, 1024), nr * nc)
            if score > best_score:
                best, best_score = (nr, nc), score
    return best


def _rider_specs(riders, steps, step_of):
    specs = []
    for arr in riders:
        nr, nc = _rider_tiling(arr.shape, steps)
        specs.append(pl.BlockSpec((arr.shape[0] // nr, arr.shape[1] // nc), functools.partial(
            lambda *idx, nc, last: (jnp.minimum(step_of(*idx), last) // nc, jnp.minimum(step_of(*idx), last) % nc),
            nc=nc, last=nr * nc - 1)))
    return specs


def _ffn(x, g, wi, wo, gf, *, final_norm, riders=(), emit_weights=False, tm=512, tf=512):
    m, d = x.shape
    dff = wo.shape[0]
    nf = dff // tf
    steps = (m // tm) * nf
    wg, wu = wi if isinstance(wi, tuple) else (wi, wi)
    up_off = 0 if isinstance(wi, tuple) else nf
    assert not emit_weights or m == tm
    weight_specs = [pl.BlockSpec((d, tf), lambda i, j: (0, j)), pl.BlockSpec((d, tf), lambda i, j: (0, j)),
                    pl.BlockSpec((tf, d), lambda i, j: (j, 0))]
    weight_out = [jax.ShapeDtypeStruct((d, dff), BF16)] * 2 + [jax.ShapeDtypeStruct((dff, d), BF16)]
    rider_specs = _rider_specs(riders, steps, lambda i, j: i * nf + j)
    out = pl.pallas_call(
        functools.partial(_ffn_kernel, final_norm=final_norm, n_riders=len(riders), emit_weights=emit_weights),
        out_shape=([jax.ShapeDtypeStruct((m, d), F32)] + [jax.ShapeDtypeStruct(arr.shape, BF16) for arr in riders]
                   + (weight_out if emit_weights else [])),
        grid=(m // tm, nf),
        in_specs=[
            pl.BlockSpec((tm, d), lambda i, j: (i, 0)),
            pl.BlockSpec((1, d), lambda i, j: (0, 0)),
            pl.BlockSpec((d, tf), lambda i, j: (0, j)),
            pl.BlockSpec((d, tf), lambda i, j: (0, j + up_off)),
            pl.BlockSpec((tf, d), lambda i, j: (j, 0)),
            pl.BlockSpec((1, d), lambda i, j: (0, 0)),
        ] + rider_specs,
        out_specs=([pl.BlockSpec((tm, d), lambda i, j: (i, 0))] + rider_specs
                   + (weight_specs if emit_weights else [])),
        scratch_shapes=[pltpu.VMEM((tm, d), BF16), pltpu.VMEM((tm, d), F32)],
        compiler_params=_cparams("arbitrary", "arbitrary"),
        name="ffn_final" if final_norm else "ffn",
    )(x, g, wg, wu, wo, gf, *riders)
    return out if (riders or emit_weights) else out[0]


def _inproj_kernel(h_ref, g_ref, wq_ref, w_ref, q_ref, kv_ref, xr_ref, qm_ref):
    u = _rms(h_ref[...], g_ref[...]).astype(BF16)
    c0, c1, c2 = SWA_Q_DIM, SWA_Q_DIM + 2 * SWA_KV_DIM, SWA_Q_DIM + 2 * SWA_KV_DIM + RWKV_IN
    q_ref[...] = _dot(u, wq_ref[...]).astype(BF16)
    kv_ref[...] = _dot(u, w_ref[:, c0:c1])
    xr_ref[...] = _dot(u, w_ref[:, c1:c2])
    qm_ref[...] = _dot(u, w_ref[:, c2:PROJ_DIM]).astype(BF16)


def _inproj(h, g, wq, w, *, tm=256):
    m, d = h.shape
    row = lambda i: (i, 0)
    return pl.pallas_call(
        _inproj_kernel,
        out_shape=(
            jax.ShapeDtypeStruct((m, SWA_Q_DIM), BF16),
            jax.ShapeDtypeStruct((m, 2 * SWA_KV_DIM), F32),
            jax.ShapeDtypeStruct((m, RWKV_IN), F32),
            jax.ShapeDtypeStruct((m, MEM_DIM), BF16),
        ),
        grid=(m // tm,),
        in_specs=[
            pl.BlockSpec((tm, d), row),
            pl.BlockSpec((1, d), lambda i: (0, 0)),
            pl.BlockSpec((d, SWA_Q_DIM), lambda i: (0, 0), pipeline_mode=pl.Buffered(1)),
            pl.BlockSpec((d, PROJ_DIM), lambda i: (0, 0), pipeline_mode=pl.Buffered(1)),
        ],
        out_specs=(
            pl.BlockSpec((tm, SWA_Q_DIM), row),
            pl.BlockSpec((tm, 2 * SWA_KV_DIM), row),
            pl.BlockSpec((tm, RWKV_IN), row),
            pl.BlockSpec((tm, MEM_DIM), row),
        ),
        compiler_params=_cparams("parallel"),
        name="inproj",
    )(h, g, wq, w)


def _norm_matmul_kernel(x_ref, g_ref, w_ref, o_ref):
    o_ref[...] = _dot(_rms(x_ref[...], g_ref[...]).astype(BF16), w_ref[...])


def _norm_matmul(x, g, w, *, tn=512):
    m, d = x.shape
    n = w.shape[1]
    return pl.pallas_call(
        _norm_matmul_kernel,
        out_shape=jax.ShapeDtypeStruct((m, n), F32),
        grid=(n // tn,),
        in_specs=[
            pl.BlockSpec((m, d), lambda j: (0, 0)),
            pl.BlockSpec((1, d), lambda j: (0, 0)),
            pl.BlockSpec((d, tn), lambda j: (0, j)),
        ],
        out_specs=pl.BlockSpec((m, tn), lambda j: (0, j)),
        compiler_params=_cparams("parallel"),
        name="norm_matmul",
    )(x, g, w)


def _swa_prompt_kernel(q_ref, kvc_ref, kvp_ref, bias_ref, sink_ref, o_ref, *, nq):
    i = pl.program_id(0)
    w = WINDOW
    rows = SWA_KV_HEADS * w
    kv_blocks = [kvp_ref[...]] + [kvc_ref[s * w:(s + 1) * w, :] for s in range(nq)]
    k_blocks = [x[:, 0:SWA_KV_DIM].astype(BF16) for x in kv_blocks]
    v_blocks = [x[:, SWA_KV_DIM:].astype(BF16) for x in kv_blocks]
    qpos = lax.broadcasted_iota(jnp.int32, (rows, 2 * w), 0) % w
    col = lax.broadcasted_iota(jnp.int32, (rows, 2 * w), 1)
    dist = qpos + w - col
    in_window = (dist >= 0) & (dist < w)
    first = in_window & ((col >= w) | (i > 0))
    lane_head = lax.broadcasted_iota(jnp.int32, (w, SWA_KV_DIM), 1) // SWA_HEAD_DIM
    scale = SWA_HEAD_DIM ** -0.5
    pairs = [(s, g) for s in range(nq) for g in range(SWA_GROUP)]
    k = [jnp.concatenate(k_blocks[s:s + 2], axis=0) for s in range(nq)]
    v = [jnp.concatenate(v_blocks[s:s + 2], axis=0) for s in range(nq)]
    logits = []
    for s, g in pairs:
        qg = q_ref[s * w:(s + 1) * w, g * SWA_KV_DIM:(g + 1) * SWA_KV_DIM].astype(F32) * scale
        qs = jnp.concatenate([jnp.where(lane_head == kvh, qg, 0.0) for kvh in range(SWA_KV_HEADS)], axis=0)
        lg = _dot_nt(qs.astype(BF16), k[s])
        logits.append(jnp.where(first if s == 0 else in_window, lg + bias_ref[g], NEG_INF))
    sink = [sink_ref[g] for _, g in pairs]
    m = [jnp.maximum(jnp.max(x, axis=-1, keepdims=True), sk) for x, sk in zip(logits, sink)]
    p = [jnp.exp(x - mi) for x, mi in zip(logits, m)]
    inv = [1.0 / (jnp.sum(pi, axis=-1, keepdims=True) + jnp.exp(sk - mi)) for pi, sk, mi in zip(p, sink, m)]
    ov = [_dot((pi * ii).astype(BF16), v[s]) for (s, _), pi, ii in zip(pairs, p, inv)]
    for (s, g), o in zip(pairs, ov):
        og = jnp.zeros((w, SWA_KV_DIM), F32)
        for kvh in range(SWA_KV_HEADS):
            og = jnp.where(lane_head == kvh, o[kvh * w:(kvh + 1) * w], og)
        o_ref[s * w:(s + 1) * w, g * SWA_KV_DIM:(g + 1) * SWA_KV_DIM] = og.astype(BF16)


def _swa_prompt(q, kv, bias, sink_rows, *, nq=8):
    t = q.shape[0]
    w = WINDOW
    rows = SWA_KV_HEADS * w
    return pl.pallas_call(
        functools.partial(_swa_prompt_kernel, nq=nq),
        out_shape=jax.ShapeDtypeStruct((t, SWA_Q_DIM), BF16),
        grid=(t // (nq * w),),
        in_specs=[
            pl.BlockSpec((nq * w, SWA_Q_DIM), lambda i: (i, 0)),
            pl.BlockSpec((nq * w, 2 * SWA_KV_DIM), lambda i: (i, 0)),
            pl.BlockSpec((w, 2 * SWA_KV_DIM), lambda i: (jnp.maximum(i * nq - 1, 0), 0)),
            pl.BlockSpec((SWA_GROUP, rows, 2 * w), lambda i: (0, 0, 0)),
            pl.BlockSpec((SWA_GROUP, rows, 1), lambda i: (0, 0, 0)),
        ],
        out_specs=pl.BlockSpec((nq * w, SWA_Q_DIM), lambda i: (i, 0)),
        compiler_params=_cparams("parallel"),
        name="swa_prompt",
    )(q, kv, kv, bias, sink_rows)


def _swa_sample_kernel(q_ref, kvn_ref, kb_ref, vb_ref, bias_b_ref, bias_n_ref, sink_ref, o_ref, ko_ref, vo_ref,
                       *, bb, t):
    gt = SWA_GROUP * t
    rows = SWA_KV_HEADS * gt
    w = kb_ref.shape[2]
    scale = SWA_HEAD_DIM ** -0.5
    lane_head = lax.broadcasted_iota(jnp.int32, (gt, SWA_KV_DIM), 1) // SWA_HEAD_DIM
    tok = lax.broadcasted_iota(jnp.int32, (rows, w), 0) % t
    keyj = lax.broadcasted_iota(jnp.int32, (rows, w), 1)
    valid_b = (tok + w - keyj) < WINDOW
    tok_n = lax.broadcasted_iota(jnp.int32, (rows, 1), 0) % t
    sink = sink_ref[...]
    bs = range(bb)
    toks = range(t)
    qall = [jnp.concatenate([jnp.where(lane_head == kvh, q_ref[b].astype(F32), 0.0) for kvh in range(SWA_KV_HEADS)],
                            axis=0) for b in bs]
    kvn = [kvn_ref[b] for b in bs]
    lb = [_dot(qall[b].astype(BF16), kb_ref[b].astype(BF16)) for b in bs]
    lb = [jnp.where(valid_b, lb[b] * scale + bias_b_ref[...], NEG_INF) for b in bs]
    ln = [[jnp.sum(qall[b] * kvn[b][j:j + 1, 0:SWA_KV_DIM], axis=-1, keepdims=True) for j in toks] for b in bs]
    ln = [[jnp.where(tok_n >= j, ln[b][j] * scale + bias_n_ref[:, j:j + 1], NEG_INF) for j in toks] for b in bs]
    m = [jnp.maximum(jnp.max(lb[b], axis=-1, keepdims=True), sink) for b in bs]
    m = [functools.reduce(jnp.maximum, ln[b], m[b]) for b in bs]
    pb = [jnp.exp(lb[b] - m[b]) for b in bs]
    pn = [[jnp.exp(ln[b][j] - m[b]) for j in toks] for b in bs]
    denom = [jnp.sum(pb[b], axis=-1, keepdims=True) + jnp.exp(sink - m[b]) for b in bs]
    inv = [1.0 / functools.reduce(jnp.add, pn[b], denom[b]) for b in bs]
    oall = [_dot_nt((pb[b] * inv[b]).astype(BF16), vb_ref[b].astype(BF16)) for b in bs]
    for b in bs:
        ob = oall[b]
        for j in toks:
            ob = ob + (pn[b][j] * inv[b]) * kvn[b][j:j + 1, SWA_KV_DIM:]
        og = jnp.zeros((gt, SWA_KV_DIM), F32)
        for kvh in range(SWA_KV_HEADS):
            og = jnp.where(lane_head == kvh, ob[kvh * gt:(kvh + 1) * gt], og)
        o_ref[b] = og.astype(BF16)
    pos = lax.broadcasted_iota(jnp.int32, (SWA_KV_DIM, w), 1)
    pad = jnp.zeros((w - SUBLANES, 2 * SWA_KV_DIM), F32)
    row8 = lax.broadcasted_iota(jnp.int32, (SUBLANES, 2 * SWA_KV_DIM), 0)
    for b in bs:
        last8 = jnp.zeros((SUBLANES, 2 * SWA_KV_DIM), F32)
        for j in toks:
            last8 = jnp.where(row8 == SUBLANES - t + j, kvn[b][j:j + 1], last8)
        tail_t = jnp.concatenate([pad, last8], axis=0).T
        ko_ref[b] = jnp.where(pos >= w - t, tail_t[:SWA_KV_DIM], pltpu.roll(kb_ref[b], w - t, axis=1))
        vo_ref[b] = jnp.where(pos >= w - t, tail_t[SWA_KV_DIM:], pltpu.roll(vb_ref[b], w - t, axis=1))


def _swa_sample(q, kvn, kbuf, vbuf, bias_b, bias_n, sink_rows, *, bb=8):
    b, gt, _ = q.shape
    t = kvn.shape[1]
    w = kbuf.shape[2]
    rows = SWA_KV_HEADS * gt
    blk = lambda i: (i, 0, 0)
    const = lambda i: (0, 0)
    cache = pl.BlockSpec((bb, SWA_KV_DIM, w), blk)
    return pl.pallas_call(
        functools.partial(_swa_sample_kernel, bb=bb, t=t),
        out_shape=(jax.ShapeDtypeStruct((b, gt, SWA_KV_DIM), BF16),
                   jax.ShapeDtypeStruct(kbuf.shape, F32), jax.ShapeDtypeStruct(vbuf.shape, F32)),
        grid=(b // bb,),
        in_specs=[
            pl.BlockSpec((bb, gt, SWA_KV_DIM), blk),
            pl.BlockSpec((bb, t, 2 * SWA_KV_DIM), blk),
            cache, cache,
            pl.BlockSpec((rows, w), const),
            pl.BlockSpec((rows, t), const),
            pl.BlockSpec((rows, 1), const),
        ],
        out_specs=(pl.BlockSpec((bb, gt, SWA_KV_DIM), blk), cache, cache),
        compiler_params=_cparams("parallel"),
        name="swa_sample",
    )(q, kvn, kbuf, vbuf, bias_b, bias_n, sink_rows)


def _mem_heads(q, mk, mv):
    scale = MEM_HEAD_DIM ** -0.5
    cols = [slice(h * MEM_HEAD_DIM, (h + 1) * MEM_HEAD_DIM) for h in range(MEM_HEADS)]
    x = [_dot_nt(q[:, sl], mk[:, sl]) * scale for sl in cols]
    m = [jnp.max(xi, axis=-1, keepdims=True) for xi in x]
    p = [jnp.exp(xi - mi) for xi, mi in zip(x, m)]
    inv = [1.0 / jnp.sum(pi, axis=-1, keepdims=True) for pi in p]
    return jnp.concatenate([_dot((pi * ii).astype(BF16), mv[:, sl]) for pi, ii, sl in zip(p, inv, cols)], axis=-1)


def _mem_prompt_kernel(q_ref, mk_ref, mv_ref, o_ref):
    o_ref[...] = _mem_heads(q_ref[...], mk_ref[...].astype(BF16), mv_ref[...].astype(BF16)).astype(BF16)


def _mem_prompt(q, mkv, *, tm=512):
    m = q.shape[0]
    return pl.pallas_call(
        _mem_prompt_kernel,
        out_shape=jax.ShapeDtypeStruct((m, MEM_DIM), BF16),
        grid=(m // tm,),
        in_specs=[
            pl.BlockSpec((tm, MEM_DIM), lambda i: (i, 0)),
            pl.BlockSpec((N_MEM, MEM_DIM), lambda i: (0, 0)),
            pl.BlockSpec((N_MEM, MEM_DIM), lambda i: (0, 1)),
        ],
        out_specs=pl.BlockSpec((tm, MEM_DIM), lambda i: (i, 0)),
        compiler_params=_cparams("parallel"),
        name="mem_prompt",
    )(q, mkv, mkv)


def _mem_sample_kernel(q_ref, mk_ref, mv_ref, o_ref, *, bb):
    scale = MEM_HEAD_DIM ** -0.5
    pairs = [(b, h) for b in range(bb) for h in range(MEM_HEADS)]
    rows = lambda h: pl.ds(h, N_MEM, stride=MEM_HEADS)
    cols = lambda h: slice(h * MEM_HEAD_DIM, (h + 1) * MEM_HEAD_DIM)
    q = [q_ref[b] for b in range(bb)]
    x = [_dot_nt(q[b][:, cols(h)], mk_ref[b, rows(h), :].astype(BF16)) * scale for b, h in pairs]
    m = [jnp.max(xi, axis=-1, keepdims=True) for xi in x]
    p = [jnp.exp(xi - mi) for xi, mi in zip(x, m)]
    inv = [1.0 / jnp.sum(pi, axis=-1, keepdims=True) for pi in p]
    o = [_dot((pi * ii).astype(BF16), mv_ref[b, rows(h), :].astype(BF16)) for (b, h), pi, ii in zip(pairs, p, inv)]
    for b in range(bb):
        o_ref[b] = jnp.concatenate(o[b * MEM_HEADS:(b + 1) * MEM_HEADS], axis=-1).astype(BF16)


def _mem_sample(q, mk, mv, *, bb=8):
    b, t, _ = q.shape
    blk = lambda i: (i, 0, 0)
    return pl.pallas_call(
        functools.partial(_mem_sample_kernel, bb=bb),
        out_shape=jax.ShapeDtypeStruct((b, t, MEM_DIM), BF16),
        grid=(b // bb,),
        in_specs=[
            pl.BlockSpec((bb, t, MEM_DIM), blk),
            pl.BlockSpec((bb, N_MEM * MEM_HEADS, MEM_HEAD_DIM), blk),
            pl.BlockSpec((bb, N_MEM * MEM_HEADS, MEM_HEAD_DIM), blk),
        ],
        out_specs=pl.BlockSpec((bb, t, MEM_DIM), blk),
        compiler_params=_cparams("parallel"),
        name="mem_sample",
    )(q, mk, mv)


def _head_sum(x, ones_ref):
    hi = x.astype(BF16)
    lo = (x - hi.astype(F32)).astype(BF16)
    ones = ones_ref[...]
    w = ones.shape[0]
    return jnp.concatenate([_dot(hi[:, c:c + w], ones) + _dot(lo[:, c:c + w], ones) for c in range(0, x.shape[1], w)],
                           axis=1)


def _token_shift(x_ref, pre_ref, start_ref, mu_ref, *, seq, tm, tile):
    x = x_ref[...]
    row = lax.broadcasted_iota(jnp.int32, x.shape, 0)
    shifted = pltpu.roll(x, 1, axis=0)
    if seq >= tm:
        is_start = (tile * tm) % seq == 0
        first_prev = jnp.where(is_start, start_ref[0], pre_ref[SUBLANES - 1:SUBLANES, :])
        prev = jnp.where(row == 0, first_prev, shifted)
    else:
        prev = jnp.where(row % seq == 0, start_ref[...], shifted)
    return x + mu_ref[...] * (prev - x)


def _rwkv_features(xs, w0_ref, ww2_ref, a0_ref, aw2_ref, gw2_ref, kk_ref, ka_ref, rk_ref, ones_ref):
    d = RWKV_DIM
    r = xs[:, 0:d]
    k = xs[:, d:2 * d]
    v = xs[:, 2 * d:3 * d]
    lw = xs[:, 3 * d:3 * d + LORA_W]
    la = xs[:, 3 * d + LORA_W:3 * d + LORA_W + LORA_A]
    lg = xs[:, 3 * d + LORA_W + LORA_A:]
    wpre = w0_ref[...] + _dot_hi(jnp.tanh(lw), ww2_ref[...])
    w = -jax.nn.softplus(-wpre) - 0.5
    log_decay = -jnp.exp(w)
    a = jax.nn.sigmoid(a0_ref[...] + _dot_hi(la, aw2_ref[...]))
    g = _dot_hi(jax.nn.sigmoid(lg), gw2_ref[...])
    kk = k * kk_ref[...]
    kk = kk / jnp.maximum(jnp.sqrt(_head_sum(kk * kk, ones_ref)), 1e-12)
    kh = k * (1.0 + (a - 1.0) * ka_ref[...])
    bonus = _head_sum(r * kh * rk_ref[...], ones_ref) * v
    return r, log_decay, kh, v, kk, kk * a, g, bonus


def _rwkv_prep_kernel(x_ref, pre_ref, start_ref, mu_ref, w0_ref, ww2_ref, a0_ref, aw2_ref, gw2_ref, kk_ref, ka_ref,
                      rk_ref, ones_ref, *outs, seq, tm):
    xs = _token_shift(x_ref, pre_ref, start_ref, mu_ref, seq=seq, tm=tm, tile=pl.program_id(0))
    feats = _rwkv_features(xs, w0_ref, ww2_ref, a0_ref, aw2_ref, gw2_ref, kk_ref, ka_ref, rk_ref, ones_ref)
    for o_ref, val in zip(outs, feats):
        o_ref[...] = val


def _rwkv_prep(x, shift0, p, ones, *, seq, tm=256):
    m = x.shape[0]
    row = lambda i: (i, 0)
    const = lambda i: (0, 0)
    vec = lambda n: pl.BlockSpec((1, n), const)
    out = jax.ShapeDtypeStruct((m, RWKV_DIM), F32)
    if seq >= tm:
        assert seq % tm == 0
        start = shift0
        start_spec = pl.BlockSpec((1, 1, RWKV_IN), lambda i: ((i * tm) // seq, 0, 0))
    else:
        assert tm % seq == 0
        start = jnp.repeat(shift0[:, 0], seq, axis=0)
        start_spec = pl.BlockSpec((tm, RWKV_IN), row)
    pre_blocks = tm // SUBLANES
    return pl.pallas_call(
        functools.partial(_rwkv_prep_kernel, seq=seq, tm=tm),
        out_shape=(out,) * 8,
        grid=(m // tm,),
        in_specs=[
            pl.BlockSpec((tm, RWKV_IN), row),
            pl.BlockSpec((SUBLANES, RWKV_IN), lambda i: (jnp.maximum(i * pre_blocks - 1, 0), 0)),
            start_spec,
            vec(RWKV_IN), vec(RWKV_DIM),
            pl.BlockSpec((LORA_W, RWKV_DIM), const),
            vec(RWKV_DIM),
            pl.BlockSpec((LORA_A, RWKV_DIM), const),
            pl.BlockSpec((LORA_G, RWKV_DIM), const),
            vec(RWKV_DIM), vec(RWKV_DIM), vec(RWKV_DIM),
            pl.BlockSpec(ones.shape, const),
        ],
        out_specs=(pl.BlockSpec((tm, RWKV_DIM), row),) * 8,
        compiler_params=_cparams("parallel"),
        name="rwkv_prep",
    )(x, x, start, p["mu"], p["w0"], p["w_w2"], p["a0"], p["a_w2"], p["g_w2"], p["k_k"], p["k_a"], p["r_k"], ones)


def _rwkv_lanes_kernel(r_ref, lw_ref, k_ref, v_ref, kk_ref, kka_ref, s_ref, y_ref, so_ref, v_scr, y_scr, *, t):
    n = RWKV_HEAD_DIM
    nb = s_ref.shape[-1]
    heads = range(2)

    def token_major(ref, j):
        return ref[pl.ds(j, nb, stride=t), :].T

    for j in range(t):
        v_scr[...] = token_major(v_ref, j)
        r_t, k_t, kk_t, kka_t = (token_major(ref, j) for ref in (r_ref, k_ref, kk_ref, kka_ref))
        w_t = jnp.exp(token_major(lw_ref, j))
        src = s_ref if j == 0 else so_ref

        def value_group(g, carry):
            rows = pl.multiple_of(g * SUBLANES, SUBLANES)
            ys = [[] for _ in heads]
            vg = [v_scr[pl.ds(h * n + rows, SUBLANES), :] for h in heads]
            for i in range(SUBLANES):
                for h in heads:
                    f = slice(h * n, (h + 1) * n)
                    s = src[h, rows + i]
                    sa = jnp.sum(s * (-kk_t[f]), axis=0, keepdims=True)
                    s = s * w_t[f] + sa * kka_t[f] + vg[h][i:i + 1] * k_t[f]
                    so_ref[h, rows + i] = s
                    ys[h].append(jnp.sum(s * r_t[f], axis=0, keepdims=True))
            for h in heads:
                y_scr[pl.ds(h * n + rows, SUBLANES), :] = jnp.concatenate(ys[h], axis=0)
            return carry

        lax.fori_loop(0, n // SUBLANES, value_group, 0)
        y_ref[j] = y_scr[...].T


def _rwkv_lanes(r, lw, k, v, kk, kka, s0, *, t):
    m, d = r.shape
    nb = m // t
    n = RWKV_HEAD_DIM
    assert nb == LANES, "one batch per lane"
    tok = pl.BlockSpec((m, LANES), lambda p: (0, p))
    st = pl.BlockSpec((2, n, n, nb), lambda p: (p, 0, 0, 0))
    return pl.pallas_call(
        functools.partial(_rwkv_lanes_kernel, t=t),
        out_shape=(jax.ShapeDtypeStruct((t, nb, d), F32), jax.ShapeDtypeStruct((RWKV_HEADS, n, n, nb), F32)),
        grid=(RWKV_HEADS // 2,),
        in_specs=[tok] * 6 + [st],
        out_specs=(pl.BlockSpec((t, nb, LANES), lambda p: (0, 0, p)), st),
        scratch_shapes=[pltpu.VMEM((LANES, nb), F32), pltpu.VMEM((LANES, nb), F32)],
        compiler_params=_cparams("parallel"),
        name="rwkv_lanes",
    )(r, lw, k, v, kk, kka, s0)


CHUNK = 64
GROUP_HEADS = 4
GROUP_W = GROUP_HEADS * RWKV_HEAD_DIM
N_GROUPS = RWKV_HEADS // GROUP_HEADS
(MASK_SAME, MASK_STRICT, MASK_INCL, MASK_LEVEL0) = (0, 1, 2, 3)
N_LEVELS = int(math.log2(CHUNK))


def _chunk_masks():
    i = np.arange(GROUP_W)
    same = (i[:, None] // CHUNK) == (i[None, :] // CHUNK)
    masks = [same, same & (i[None, :] < i[:, None]), same & (i[None, :] <= i[:, None])]
    for lvl in range(N_LEVELS):
        m = 1 << lvl
        masks.append(((i[:, None] // (2 * m)) == (i[None, :] // (2 * m))) & ((i[:, None] // m) != (i[None, :] // m))
                     & (i[None, :] < i[:, None]))
    return np.stack(masks).astype(np.float32)


def _rwkv_prompt_kernel(x_ref, pre_ref, start_ref, mu_ref, w0_ref, ww2_ref, a0_ref, aw2_ref, gw2_ref, kk_p_ref,
                        ka_ref, rk_ref, ones_ref, lnw_ref, lnb_ref, st0_ref, tri_ref, eye_ref, mask_ref,
                        *rest, seq, tm, n_tiles, n_riders):
    rider_in, (o_ref, sto_ref), rider_out = rest[:n_riders], rest[n_riders:n_riders + 2], rest[n_riders + 2:-11]
    st_scr, xs_ref, y_ref, r_set, lw_set, k_set, v_set, kk_set, kka_set, g_set, bonus_set = rest[-11:]
    for src, dst in zip(rider_in, rider_out):
        dst[...] = src[...].astype(BF16)
    step = pl.program_id(0)
    cur = (step + 1) % 2
    nxt = step % 2
    sets = (r_set, lw_set, k_set, v_set, kk_set, kka_set, g_set, bonus_set)
    r_ref, lw_ref, k_ref, v_ref, kk_ref, kka_ref = (s.at[cur] for s in sets[:6])

    @pl.when(step == 0)
    def _():
        st_scr[...] = st0_ref[...]
        for s in sets:
            s[1] = jnp.zeros(s.shape[1:], F32)

    xs_ref[...] = _token_shift(x_ref, pre_ref, start_ref, mu_ref, seq=seq, tm=tm, tile=jnp.minimum(step, n_tiles - 1))
    n_chunks = tm // CHUNK

    piece = 2 * CHUNK
    def features(c):
        rows = slice(c * piece, (c + 1) * piece)
        feats = _rwkv_features(xs_ref[rows, :], w0_ref, ww2_ref, a0_ref, aw2_ref, gw2_ref, kk_p_ref, ka_ref, rk_ref,
                               ones_ref)
        for s, val in zip(sets, feats):
            s[nxt, rows, :] = val

    eye = eye_ref[...]
    tri = tri_ref[...]
    tile_rows = lambda x: jnp.concatenate([x] * GROUP_HEADS, axis=0)
    block_diag = lambda x: (tile_rows(x) * mask_ref[MASK_SAME]).astype(BF16)

    chains = [(slice(c * CHUNK, (c + 1) * CHUNK), slice(g * GROUP_W, (g + 1) * GROUP_W))
              for c in range(n_chunks) for g in range(N_GROUPS)]
    each = lambda f, *cols: [f(*args) for args in zip(*cols)]
    same, strict, incl = mask_ref[MASK_SAME], mask_ref[MASK_STRICT], mask_ref[MASK_INCL]

    def cum_decay(lw):
        h1 = lw.astype(BF16)
        r1 = lw - h1.astype(F32)
        h2 = r1.astype(BF16)
        h3 = (r1 - h2.astype(F32)).astype(BF16)
        return _dot(tri, h1) + _dot(tri, h2) + _dot(tri, h3)

    lw = [lw_ref[rows, sl] for rows, sl in chains]
    kka = [kka_ref[rows, sl] for rows, sl in chains]
    k = [k_ref[rows, sl] for rows, sl in chains]
    cum = each(cum_decay, lw)
    cum_last = each(lambda c: c[CHUNK - 1:CHUNK, :], cum)
    p_inv = each(lambda c: jnp.exp(-c), cum)
    p_tail = each(lambda c, cl: jnp.exp(cl - c), cum, cum_last)
    a_bd = [block_diag(-kk_ref[rows, sl] * jnp.exp(c - l)) for (rows, sl), c, l in zip(chains, cum, lw)]
    r_f = [tile_rows(r_ref[rows, sl] * jnp.exp(c)) * same for (rows, sl), c in zip(chains, cum)]
    r_bd = each(lambda x: x.astype(BF16), r_f)
    v_bd = [block_diag(v_ref[rows, sl]) for rows, sl in chains]
    b_rep = each(lambda x, p: tile_rows((x * p).astype(BF16)), kka, p_inv)
    k_rep = each(lambda x, p: tile_rows((x * p).astype(BF16)), k, p_inv)
    bh_rep = each(lambda x, p: tile_rows(x * p), kka, p_tail)
    kh_rep = each(lambda x, p: tile_rows(x * p), k, p_tail)

    l_ab_f = each(lambda a, b: _dot_nt(a, b) * strict, a_bd, b_rep)
    l_ab = each(lambda x: x.astype(BF16), l_ab_f)
    l_ak = each(lambda a, b: (_dot_nt(a, b) * strict).astype(BF16), a_bd, k_rep)
    m_rb = each(lambda a, b: (_dot_nt(a, b) * incl).astype(BF16), r_bd, b_rep)
    m_rk = each(lambda a, b: (_dot_nt(a, b) * incl).astype(BF16), r_bd, k_rep)
    bh_t = each(lambda x: (x.T * same).astype(BF16), bh_rep)
    kh_t = each(lambda x: (x.T * same).astype(BF16), kh_rep)

    pending = list(range(tm // piece))

    def next_features():
        if pending:
            features(pending.pop(0))

    d = each(lambda l: eye + l * mask_ref[MASK_LEVEL0], l_ab_f)
    for lvl in range(1, N_LEVELS):
        d_b = each(lambda x: x.astype(BF16), d)
        x = each(lambda l, db: (_dot(l, db) * mask_ref[MASK_LEVEL0 + lvl]).astype(BF16), l_ab, d_b)
        next_features()
        d = each(lambda dd, db, xx: dd + _dot(db, xx), d, d_b, x)
    t_b = each(lambda x: x.astype(BF16), d)
    while pending:
        next_features()

    wm = each(lambda a, b, vv: _dot(jnp.concatenate([a, b], axis=0), vv), l_ak, m_rk, v_bd)
    twa = each(lambda t, w, a: _dot(t, jnp.concatenate([w[:GROUP_W].astype(BF16), a], axis=1)).astype(BF16),
               t_b, wm, a_bd)
    ry = each(_dot, m_rb, twa)
    mn = each(_dot, bh_t, twa)
    khv = each(_dot, kh_t, v_bd)
    y0 = each(lambda a, w: a[:, :GROUP_W] + w[GROUP_W:], ry, wm)
    n_x = each(lambda a, b: a[:, :GROUP_W] + b, mn, khv)
    mr = each(lambda a, cl, rf, b: jnp.concatenate(
        [(eye * jnp.exp(cl) + a[:, GROUP_W:]).astype(BF16), (rf + b[:, GROUP_W:]).astype(BF16)], axis=0),
        mn, cum_last, r_f, ry)

    st = [st_scr[g] for g in range(N_GROUPS)]
    for i, (rows, sl) in enumerate(chains):
        g = i % N_GROUPS
        ys = _dot(mr[i], st[g].astype(BF16))
        st[g] = ys[:GROUP_W] + n_x[i]
        y_bd = ys[GROUP_W:] + y0[i]
        y = y_bd[0:CHUNK]
        for h in range(1, GROUP_HEADS):
            y = y + y_bd[h * CHUNK:(h + 1) * CHUNK]
        y_ref[rows, sl] = y
    for g in range(N_GROUPS):
        st_scr[g] = st[g]
    o_ref[...] = _rwkv_output(y_ref[...], bonus_set[cur], g_set[cur], lnw_ref, lnb_ref, ones_ref)

    @pl.when(step == pl.num_programs(0) - 1)
    def _():
        sto_ref[...] = st_scr[...]


def _rwkv_prompt(x, shift0, s0, p, ones, *, riders=(), chunks_per_step=4):
    t = x.shape[0]
    d = RWKV_DIM
    n = RWKV_HEAD_DIM
    tt = CHUNK * chunks_per_step
    assert CHUNK == n and t % tt == 0
    st0 = jnp.einsum("ghvk,hj->ghkjv", s0.reshape(N_GROUPS, GROUP_HEADS, n, n), jnp.eye(GROUP_HEADS, dtype=F32))
    st0 = st0.reshape(N_GROUPS, GROUP_W, GROUP_W)
    tri = jnp.asarray(np.tril(np.ones((CHUNK, CHUNK), np.float32)), BF16)
    eye = jnp.eye(GROUP_W, dtype=F32)
    masks = jnp.asarray(_chunk_masks())
    const = lambda c: (0, 0)
    vec = lambda width: pl.BlockSpec((1, width), const)
    st_spec = pl.BlockSpec((N_GROUPS, GROUP_W, GROUP_W), lambda c: (0, 0, 0))
    feature_set = pltpu.VMEM((2, tt, d), F32)
    pre_blocks = tt // SUBLANES
    n_tiles = t // tt
    fill = lambda c: jnp.minimum(c, n_tiles - 1)
    rider_specs = _rider_specs(riders, n_tiles + 1, lambda c: c)
    o, st, *rounded = pl.pallas_call(
        functools.partial(_rwkv_prompt_kernel, seq=t, tm=tt, n_tiles=n_tiles, n_riders=len(riders)),
        out_shape=[jax.ShapeDtypeStruct((t, d), BF16), jax.ShapeDtypeStruct((N_GROUPS, GROUP_W, GROUP_W), F32)]
        + [jax.ShapeDtypeStruct(arr.shape, BF16) for arr in riders],
        grid=(n_tiles + 1,),
        in_specs=[
            pl.BlockSpec((tt, RWKV_IN), lambda c: (fill(c), 0)),
            pl.BlockSpec((SUBLANES, RWKV_IN), lambda c: (jnp.maximum(fill(c) * pre_blocks - 1, 0), 0)),
            pl.BlockSpec((1, 1, RWKV_IN), lambda c: (0, 0, 0)),
            vec(RWKV_IN), vec(d),
            pl.BlockSpec((LORA_W, d), const),
            vec(d),
            pl.BlockSpec((LORA_A, d), const),
            pl.BlockSpec((LORA_G, d), const),
            vec(d), vec(d), vec(d),
            pl.BlockSpec(ones.shape, const),
            vec(d), vec(d),
            st_spec,
            pl.BlockSpec((CHUNK, CHUNK), const),
            pl.BlockSpec((GROUP_W, GROUP_W), const),
            pl.BlockSpec(masks.shape, lambda c: (0, 0, 0)),
        ] + rider_specs,
        out_specs=[pl.BlockSpec((tt, d), lambda c: (jnp.maximum(c - 1, 0), 0)), st_spec] + rider_specs,
        scratch_shapes=[pltpu.VMEM((N_GROUPS, GROUP_W, GROUP_W), F32), pltpu.VMEM((tt, RWKV_IN), F32),
                        pltpu.VMEM((tt, d), F32)] + [feature_set] * 8,
        compiler_params=_cparams("arbitrary"),
        name="rwkv_prompt",
    )(x, x, shift0, p["mu"], p["w0"], p["w_w2"], p["a0"], p["a_w2"], p["g_w2"], p["k_k"], p["k_a"], p["r_k"], ones,
      p["ln_w"], p["ln_b"], st0, tri, eye, masks, *riders)
    st5 = st.reshape(N_GROUPS, GROUP_HEADS, n, GROUP_HEADS, n)
    s_new = jnp.einsum("ghkjv,hj->ghvk", st5, jnp.eye(GROUP_HEADS, dtype=F32)).reshape(RWKV_HEADS, n, n)
    return o, s_new, rounded


def _rwkv_output(y, bonus, gate, lnw_ref, lnb_ref, ones_ref):
    inv_n = 1.0 / RWKV_HEAD_DIM
    mu = _head_sum(y, ones_ref) * inv_n
    dlt = y - mu
    var = _head_sum(dlt * dlt, ones_ref) * inv_n
    yn = dlt * lax.rsqrt(var + GN_EPS) * lnw_ref[...] + lnb_ref[...]
    return ((yn + bonus) * gate).astype(BF16)


def _rwkv_post_kernel(y_ref, bonus_ref, g_ref, lnw_ref, lnb_ref, ones_ref, o_ref):
    o_ref[...] = _rwkv_output(y_ref[...], bonus_ref[...], g_ref[...], lnw_ref, lnb_ref, ones_ref)


def _rwkv_post(y, bonus, g, lnw, lnb, ones, *, tm=256):
    m = y.shape[0]
    row = lambda i: (i, 0)
    const = lambda i: (0, 0)
    tile = pl.BlockSpec((tm, RWKV_DIM), row)
    return pl.pallas_call(
        _rwkv_post_kernel,
        out_shape=jax.ShapeDtypeStruct((m, RWKV_DIM), BF16),
        grid=(m // tm,),
        in_specs=[tile, tile, tile, pl.BlockSpec((1, RWKV_DIM), const), pl.BlockSpec((1, RWKV_DIM), const),
                  pl.BlockSpec(ones.shape, const)],
        out_specs=tile,
        compiler_params=_cparams("parallel"),
        name="rwkv_post",
    )(y, bonus, g, lnw, lnb, ones)


GATE_BLOCK = math.gcd(PROJ_DIM, D_MODEL)


def _merge_kernel(h_ref, g_ref, os_ref, or_ref, om_ref, wos_ref, wor_ref, wom_ref, wout_ref, *rest, parts):
    gate_refs, (o_ref, u_ref, acc_ref) = rest[:N_BRANCH * parts], rest[N_BRANCH * parts:]
    j = pl.program_id(1)

    @pl.when(j == 0)
    def _():
        u_ref[...] = _rms(h_ref[...], g_ref[...]).astype(BF16)
        acc_ref[...] = jnp.zeros_like(acc_ref)

    u = u_ref[...]
    merged = None
    for b, (x_ref, w_ref) in enumerate(((os_ref, wos_ref), (or_ref, wor_ref), (om_ref, wom_ref))):
        gate = jnp.concatenate([_dot(u, gate_refs[b * parts + c][...]) for c in range(parts)], axis=1)
        term = jax.nn.sigmoid(gate) * _dot(x_ref[...], w_ref[...])
        merged = term if merged is None else merged + term
    acc_ref[...] += _dot(merged.astype(BF16), wout_ref[...])

    @pl.when(j == pl.num_programs(1) - 1)
    def _():
        o_ref[...] = h_ref[...] + acc_ref[...]


def _merge(h, g, o_swa, o_rw, o_mem, w_in, wo_swa, wo_rw, wo_mem, w_out, *, tm=512, tn=512):
    m, d = h.shape
    nt = d // tn
    parts = tn // GATE_BLOCK
    g0 = PROJ_DIM // GATE_BLOCK
    row = lambda i, j: (i, 0)
    col = lambda i, j: (0, j)
    gate_specs = [pl.BlockSpec((d, GATE_BLOCK), functools.partial(
        lambda i, j, off: (0, off + j * parts), off=g0 + b * (d // GATE_BLOCK) + c))
        for b in range(N_BRANCH) for c in range(parts)]
    return pl.pallas_call(
        functools.partial(_merge_kernel, parts=parts),
        out_shape=jax.ShapeDtypeStruct((m, d), F32),
        grid=(m // tm, nt),
        in_specs=[
            pl.BlockSpec((tm, d), row),
            pl.BlockSpec((1, d), lambda i, j: (0, 0)),
            pl.BlockSpec((tm, SWA_Q_DIM), row),
            pl.BlockSpec((tm, RWKV_DIM), row),
            pl.BlockSpec((tm, MEM_DIM), row),
            pl.BlockSpec((SWA_Q_DIM, tn), col),
            pl.BlockSpec((RWKV_DIM, tn), col),
            pl.BlockSpec((MEM_DIM, tn), col),
            pl.BlockSpec((tn, d), lambda i, j: (j, 0)),
        ] + gate_specs,
        out_specs=pl.BlockSpec((tm, d), row),
        scratch_shapes=[pltpu.VMEM((tm, d), BF16), pltpu.VMEM((tm, d), F32)],
        compiler_params=_cparams("parallel", "arbitrary"),
        name="merge",
    )(h, g, o_swa, o_rw, o_mem, wo_swa, wo_rw, wo_mem, w_out, *([w_in] * (N_BRANCH * parts)))


def _t5_bucket(dist):
    max_exact = N_BUCKETS // 2
    d = np.maximum(dist, 0)
    log_ratio = (np.log(np.maximum(d, 1).astype(np.float32) / np.float32(max_exact))
                 / np.float32(math.log(MAX_DISTANCE / max_exact)))
    large = np.minimum(max_exact + (log_ratio * (N_BUCKETS - max_exact)).astype(np.int32), N_BUCKETS - 1)
    return np.where(d < max_exact, d, large).astype(np.int32)


def _rel_bias(table, dist):
    onehot = np.eye(N_BUCKETS, dtype=np.float32)[_t5_bucket(dist).reshape(-1)]
    bias = jnp.einsum("nb,bh->hn", jnp.asarray(onehot), table, precision=lax.Precision.HIGHEST)
    return bias.reshape(SWA_HEADS, *dist.shape)


def _rwkv_branch(xr, shift0, s0, p, ones, riders=()):
    b, t, _ = xr.shape
    flat = lambda z: z.reshape(b * t, z.shape[-1])
    if b == 1:
        o, s_new, rounded = _rwkv_prompt(flat(xr), shift0, s0[0], p, ones, riders=riders)
        return o, s_new[None], rounded
    assert b == LANES and not riders, "short sequences are batched one per lane"
    r, w, k, v, kk, kka, g, bonus = _rwkv_prep(flat(xr), shift0, p, ones, seq=t)
    y, s_new = _rwkv_lanes(r, w, k, v, kk, kka, jnp.transpose(s0, (1, 2, 3, 0)), t=t)
    o = _rwkv_post(flat(jnp.transpose(y, (1, 0, 2))), bonus, g, p["ln_w"], p["ln_b"], ones)
    return o, jnp.transpose(s_new, (3, 0, 1, 2))


def kernel(x_prompt, mem_prompt, x_sample, cache_swa_k, cache_swa_v, state_rwkv, state_rwkv_shift, cache_mem_k, cache_mem_v, ffn1_norm, ffn1_wi, ffn1_wo, mix_norm, w_in, swa_sinks, rel_bias_table, rwkv_mu, rwkv_w0, rwkv_w_w2, rwkv_a0, rwkv_a_w2, rwkv_g_w2, rwkv_k_k, rwkv_k_a, rwkv_r_k, rwkv_ln_w, rwkv_ln_b, mem_norm, w_mem_kv, w_o_swa, w_o_rwkv, w_o_mem, w_out, ffn2_norm, ffn2_wi, ffn2_wo, final_norm):
    assert ffn1_wi.shape[0] == 1, "single-layer trunk"
    bp, tp, d = x_prompt.shape
    bs, ts, _ = x_sample.shape
    assert bp == 1
    row = lambda z: z.reshape(1, -1).astype(F32)

    g1, gm, g2, gf = row(ffn1_norm[0]), row(mix_norm[0]), row(ffn2_norm[0]), row(final_norm)
    rp = {
        "mu": row(rwkv_mu[0]), "w0": row(rwkv_w0[0]), "w_w2": rwkv_w_w2[0], "a0": row(rwkv_a0[0]),
        "a_w2": rwkv_a_w2[0], "g_w2": rwkv_g_w2[0], "k_k": row(rwkv_k_k[0]), "k_a": row(rwkv_k_a[0]),
        "r_k": row(rwkv_r_k[0]), "ln_w": row(rwkv_ln_w[0]), "ln_b": row(rwkv_ln_b[0]),
    }
    seg = np.arange(GROUP_W) // RWKV_HEAD_DIM
    ones = jnp.asarray(seg[:, None] == seg[None, :], dtype=BF16)
    sinks = swa_sinks[0].astype(F32)
    table = rel_bias_table.astype(F32)

    xp = x_prompt.reshape(tp, d)
    xs = x_sample.reshape(bs * ts, d)
    hs, wg1, wu1, wo1 = _ffn(xs, g1, ffn1_wi[0], ffn1_wo[0], gf, final_norm=False, emit_weights=True, tf=256)
    hp, w_in_b, wo_swa, wo_rw, wo_mem, w_out_b = _ffn(
        xp, g1, (wg1, wu1), wo1, gf, final_norm=False,
        riders=(w_in[0], w_o_swa[0], w_o_rwkv[0], w_o_mem[0], w_out[0]))
    wo_swa = wo_swa.reshape(SWA_KV_HEADS, SWA_GROUP, SWA_HEAD_DIM, d).transpose(1, 0, 2, 3).reshape(SWA_Q_DIM, d)
    w_q = w_in_b[:, :SWA_Q_DIM].reshape(d, SWA_KV_HEADS, SWA_GROUP, SWA_HEAD_DIM).transpose(0, 2, 1, 3).reshape(d, SWA_Q_DIM)
    qp, kvp, xrp, qmp = _inproj(hp, gm, w_q, w_in_b)
    qs, kvs, xrs, qms = _inproj(hs, gm, w_q, w_in_b)

    w = WINDOW
    dist_p = np.arange(w)[:, None] + w - np.arange(2 * w)[None, :]
    bias_p = _rel_bias(table, dist_p).reshape(SWA_KV_HEADS, SWA_GROUP, w, 2 * w).transpose(1, 0, 2, 3)
    bias_p = bias_p.reshape(SWA_GROUP, SWA_KV_HEADS * w, 2 * w)
    sink_p = jnp.repeat(sinks.reshape(SWA_KV_HEADS, SWA_GROUP).T, w, axis=1).reshape(SWA_GROUP, SWA_KV_HEADS * w, 1)
    o_swa_p = _swa_prompt(qp, kvp, bias_p, sink_p)

    wbuf = cache_swa_k.shape[2]
    dist_s = np.arange(ts)[:, None] + wbuf - np.arange(wbuf + ts)[None, :]
    bias_s = _rel_bias(table, dist_s).reshape(SWA_HEADS * ts, wbuf + ts)
    sink_rows = jnp.repeat(sinks, ts).reshape(SWA_HEADS * ts, 1)
    qs_gt = qs.reshape(bs, ts, SWA_GROUP, SWA_KV_DIM).transpose(0, 2, 1, 3).reshape(bs, SWA_GROUP * ts, SWA_KV_DIM)
    kbuf = cache_swa_k[0].reshape(bs, wbuf, SWA_KV_DIM).transpose(0, 2, 1)
    vbuf = cache_swa_v[0].reshape(bs, wbuf, SWA_KV_DIM).transpose(0, 2, 1)
    o_swa_s, knew_t, vnew_t = _swa_sample(qs_gt, kvs.reshape(bs, ts, 2 * SWA_KV_DIM), kbuf, vbuf,
                                          bias_s[:, :wbuf], bias_s[:, wbuf:], sink_rows)
    o_swa_s = o_swa_s.reshape(bs, SWA_GROUP, ts, SWA_KV_DIM).transpose(0, 2, 1, 3).reshape(bs * ts, SWA_Q_DIM)

    zero_shift = jnp.zeros((bp, 1, RWKV_IN), F32)
    zero_state = jnp.zeros((bp, RWKV_HEADS, RWKV_HEAD_DIM, RWKV_HEAD_DIM), F32)
    o_rw_p, state_p, (wi2, wo2) = _rwkv_branch(xrp.reshape(bp, tp, RWKV_IN), zero_shift, zero_state, rp, ones,
                                               riders=(ffn2_wi[0], ffn2_wo[0]))
    o_rw_s, state_s = _rwkv_branch(xrs.reshape(bs, ts, RWKV_IN), state_rwkv_shift[0], state_rwkv[0], rp, ones)

    mkv = _norm_matmul(mem_prompt.reshape(N_MEM, d), row(mem_norm[0]), w_mem_kv[0].astype(BF16))
    o_mem_p = _mem_prompt(qmp, mkv)
    o_mem_s = _mem_sample(qms.reshape(bs, ts, MEM_DIM), cache_mem_k[0].reshape(bs, N_MEM * MEM_HEADS, MEM_HEAD_DIM),
                          cache_mem_v[0].reshape(bs, N_MEM * MEM_HEADS, MEM_HEAD_DIM)).reshape(bs * ts, MEM_DIM)

    hp = _merge(hp, gm, o_swa_p, o_rw_p, o_mem_p, w_in_b, wo_swa, wo_rw, wo_mem, w_out_b)
    hs = _merge(hs, gm, o_swa_s, o_rw_s, o_mem_s, w_in_b, wo_swa, wo_rw, wo_mem, w_out_b)
    y_prompt = _ffn(hp, g2, wi2, wo2, gf, final_norm=True).reshape(bp, tp, d)
    y_sample = _ffn(hs, g2, wi2, wo2, gf, final_norm=True).reshape(bs, ts, d)

    wp = min(w, tp)
    p_k = kvp[tp - wp:, :SWA_KV_DIM].reshape(1, bp, wp, SWA_KV_HEADS, SWA_HEAD_DIM)
    p_v = kvp[tp - wp:, SWA_KV_DIM:].reshape(1, bp, wp, SWA_KV_HEADS, SWA_HEAD_DIM)
    p_mk = mkv[:, :MEM_DIM].reshape(1, bp, N_MEM, MEM_HEADS, MEM_HEAD_DIM)
    p_mv = mkv[:, MEM_DIM:].reshape(1, bp, N_MEM, MEM_HEADS, MEM_HEAD_DIM)
    s_k = knew_t.transpose(0, 2, 1).reshape(1, bs, wbuf, SWA_KV_HEADS, SWA_HEAD_DIM)
    s_v = vnew_t.transpose(0, 2, 1).reshape(1, bs, wbuf, SWA_KV_HEADS, SWA_HEAD_DIM)
    return (y_prompt, y_sample,
            p_k, p_v, state_p[None], xrp[tp - 1:].reshape(1, bp, 1, RWKV_IN), p_mk, p_mv,
            s_k, s_v, state_s[None], xrs.reshape(bs, ts, RWKV_IN)[:, ts - 1:][None])
```

```python
import functools
import math

import jax
import jax.numpy as jnp
import numpy as np
from jax import lax
from jax.experimental import pallas as pl
from jax.experimental.pallas import tpu as pltpu

F32 = jnp.float32
BF16 = jnp.bfloat16

D_MODEL = 2048
SWA_HEADS = 16
SWA_KV_HEADS = 4
SWA_GROUP = SWA_HEADS // SWA_KV_HEADS
SWA_HEAD_DIM = 64
SWA_Q_DIM = SWA_HEADS * SWA_HEAD_DIM
SWA_KV_DIM = SWA_KV_HEADS * SWA_HEAD_DIM
WINDOW = 128
N_BUCKETS = 32
MAX_DISTANCE = 128
RWKV_HEADS = 8
RWKV_HEAD_DIM = 64
RWKV_DIM = RWKV_HEADS * RWKV_HEAD_DIM
LORA_W = 64
LORA_A = 64
LORA_G = 128
RWKV_IN = 3 * RWKV_DIM + LORA_W + LORA_A + LORA_G
N_MEM = 256
MEM_HEADS = 4
MEM_HEAD_DIM = 128
MEM_DIM = MEM_HEADS * MEM_HEAD_DIM
N_BRANCH = 3
PROJ_DIM = SWA_Q_DIM + 2 * SWA_KV_DIM + RWKV_IN + MEM_DIM
NORM_EPS = 1e-6
GN_EPS = 64e-5
NEG_INF = -1e30

LANES = 128
SUBLANES = 8
VMEM_LIMIT = 56 * 1024 * 1024


def _cparams(*sem):
    return pltpu.CompilerParams(dimension_semantics=sem, vmem_limit_bytes=VMEM_LIMIT)


def _rms(x, g):
    return x * lax.rsqrt(jnp.mean(x * x, axis=-1, keepdims=True) + NORM_EPS) * g


def _dot(a, b):
    return jnp.dot(a, b, preferred_element_type=F32)


def _dot_nt(a, b):
    return lax.dot_general(a, b, (((1,), (1,)), ((), ())), preferred_element_type=F32)


def _dot_hi(a, b):
    a_hi = a.astype(BF16)
    b_hi = b.astype(BF16)
    a_lo = (a - a_hi.astype(F32)).astype(BF16)
    b_lo = (b - b_hi.astype(F32)).astype(BF16)
    return _dot(a_hi, b_hi) + _dot(a_hi, b_lo) + _dot(a_lo, b_hi)


def _ffn_kernel(x_ref, g_ref, wg_ref, wu_ref, wo_ref, gf_ref, *rest, final_norm, n_riders, emit_weights):
    rider_in, (o_ref, *extra_out), (xn_ref, acc_ref) = rest[:n_riders], rest[n_riders:-2], rest[-2:]
    for src, dst in zip(rider_in, extra_out[:n_riders]):
        dst[...] = src[...].astype(BF16)
    j = pl.program_id(1)

    @pl.when(j == 0)
    def _():
        xn_ref[...] = _rms(x_ref[...], g_ref[...]).astype(BF16)
        acc_ref[...] = jnp.zeros_like(acc_ref)

    wg, wu, wo = wg_ref[...], wu_ref[...], wo_ref[...]
    if emit_weights:
        wg, wu, wo = wg.astype(BF16), wu.astype(BF16), wo.astype(BF16)
        for dst, val in zip(extra_out[n_riders:], (wg, wu, wo)):
            dst[...] = val
    xn = xn_ref[...]
    gate = _dot(xn, wg)
    up = _dot(xn, wu)
    act = (gate * jax.nn.sigmoid(gate)) * up
    acc_ref[...] += _dot(act.astype(BF16), wo)

    @pl.when(j == pl.num_programs(1) - 1)
    def _():
        h = x_ref[...] + 0.5 * acc_ref[...]
        if final_norm:
            h = _rms(h, gf_ref[...])
        o_ref[...] = h


def _rider_tiling(shape, steps):
    rows, cols = shape
    best, best_score = (1, 1), (0, 0)
    for nr in range(1, rows // 16 + 1):
        if rows % nr or (rows // nr) % 16:
            continue
        for nc in range(1, cols // LANES + 1):
            if cols % nc or (cols // nc) % LANES or nr * nc > steps:
                continue
            score = (min(cols // nc, 1024), nr * nc)
            if score > best_score:
                best, best_score = (nr, nc), score
    return best


def _rider_specs(riders, steps, step_of):
    specs = []
    for arr in riders:
        nr, nc = _rider_tiling(arr.shape, steps)
        specs.append(pl.BlockSpec((arr.shape[0] // nr, arr.shape[1] // nc), functools.partial(
            lambda *idx, nc, last: (jnp.minimum(step_of(*idx), last) // nc, jnp.minimum(step_of(*idx), last) % nc),
            nc=nc, last=nr * nc - 1)))
    return specs


def _ffn(x, g, wi, wo, gf, *, final_norm, riders=(), emit_weights=False, tm=512, tf=512):
    m, d = x.shape
    dff = wo.shape[0]
    nf = dff // tf
    steps = (m // tm) * nf
    wg, wu = wi if isinstance(wi, tuple) else (wi, wi)
    up_off = 0 if isinstance(wi, tuple) else nf
    assert not emit_weights or m == tm
    weight_specs = [pl.BlockSpec((d, tf), lambda i, j: (0, j)), pl.BlockSpec((d, tf), lambda i, j: (0, j)),
                    pl.BlockSpec((tf, d), lambda i, j: (j, 0))]
    weight_out = [jax.ShapeDtypeStruct((d, dff), BF16)] * 2 + [jax.ShapeDtypeStruct((dff, d), BF16)]
    rider_specs = _rider_specs(riders, steps, lambda i, j: i * nf + j)
    out = pl.pallas_call(
        functools.partial(_ffn_kernel, final_norm=final_norm, n_riders=len(riders), emit_weights=emit_weights),
        out_shape=([jax.ShapeDtypeStruct((m, d), F32)] + [jax.ShapeDtypeStruct(arr.shape, BF16) for arr in riders]
                   + (weight_out if emit_weights else [])),
        grid=(m // tm, nf),
        in_specs=[
            pl.BlockSpec((tm, d), lambda i, j: (i, 0)),
            pl.BlockSpec((1, d), lambda i, j: (0, 0)),
            pl.BlockSpec((d, tf), lambda i, j: (0, j)),
            pl.BlockSpec((d, tf), lambda i, j: (0, j + up_off)),
            pl.BlockSpec((tf, d), lambda i, j: (j, 0)),
            pl.BlockSpec((1, d), lambda i, j: (0, 0)),
        ] + rider_specs,
        out_specs=([pl.BlockSpec((tm, d), lambda i, j: (i, 0))] + rider_specs
                   + (weight_specs if emit_weights else [])),
        scratch_shapes=[pltpu.VMEM((tm, d), BF16), pltpu.VMEM((tm, d), F32)],
        compiler_params=_cparams("arbitrary", "arbitrary"),
        name="ffn_final" if final_norm else "ffn",
    )(x, g, wg, wu, wo, gf, *riders)
    return out if (riders or emit_weights) else out[0]


def _inproj_kernel(h_ref, g_ref, wq_ref, w_ref, q_ref, kv_ref, xr_ref, qm_ref, u_ref):
    u = _rms(h_ref[...], g_ref[...]).astype(BF16)
    u_ref[...] = u
    c0, c1, c2 = SWA_Q_DIM, SWA_Q_DIM + 2 * SWA_KV_DIM, SWA_Q_DIM + 2 * SWA_KV_DIM + RWKV_IN
    q_ref[...] = _dot(u, wq_ref[...]).astype(BF16)
    kv_ref[...] = _dot(u, w_ref[:, c0:c1])
    xr_ref[...] = _dot(u, w_ref[:, c1:c2])
    qm_ref[...] = _dot(u, w_ref[:, c2:PROJ_DIM]).astype(BF16)


def _inproj(h, g, wq, w, *, tm=256):
    m, d = h.shape
    row = lambda i: (i, 0)
    return pl.pallas_call(
        _inproj_kernel,
        out_shape=(
            jax.ShapeDtypeStruct((m, SWA_Q_DIM), BF16),
            jax.ShapeDtypeStruct((m, 2 * SWA_KV_DIM), F32),
            jax.ShapeDtypeStruct((m, RWKV_IN), F32),
            jax.ShapeDtypeStruct((m, MEM_DIM), BF16),
            jax.ShapeDtypeStruct((m, d), BF16),
        ),
        grid=(m // tm,),
        in_specs=[
            pl.BlockSpec((tm, d), row),
            pl.BlockSpec((1, d), lambda i: (0, 0)),
            pl.BlockSpec((d, SWA_Q_DIM), lambda i: (0, 0), pipeline_mode=pl.Buffered(1)),
            pl.BlockSpec((d, PROJ_DIM), lambda i: (0, 0), pipeline_mode=pl.Buffered(1)),
        ],
        out_specs=(
            pl.BlockSpec((tm, SWA_Q_DIM), row),
            pl.BlockSpec((tm, 2 * SWA_KV_DIM), row),
            pl.BlockSpec((tm, RWKV_IN), row),
            pl.BlockSpec((tm, MEM_DIM), row),
            pl.BlockSpec((tm, d), row),
        ),
        compiler_params=_cparams("parallel"),
        name="inproj",
    )(h, g, wq, w)


def _norm_matmul_kernel(x_ref, g_ref, w_ref, o_ref):
    o_ref[...] = _dot(_rms(x_ref[...], g_ref[...]).astype(BF16), w_ref[...])


def _norm_matmul(x, g, w, *, tn=512):
    m, d = x.shape
    n = w.shape[1]
    return pl.pallas_call(
        _norm_matmul_kernel,
        out_shape=jax.ShapeDtypeStruct((m, n), F32),
        grid=(n // tn,),
        in_specs=[
            pl.BlockSpec((m, d), lambda j: (0, 0)),
            pl.BlockSpec((1, d), lambda j: (0, 0)),
            pl.BlockSpec((d, tn), lambda j: (0, j)),
        ],
        out_specs=pl.BlockSpec((m, tn), lambda j: (0, j)),
        compiler_params=_cparams("parallel"),
        name="norm_matmul",
    )(x, g, w)


def _swa_prompt_kernel(q_ref, kvc_ref, kvp_ref, bias_ref, sink_ref, o_ref, *, nq):
    i = pl.program_id(0)
    w = WINDOW
    rows = SWA_KV_HEADS * w
    kv_blocks = [kvp_ref[...]] + [kvc_ref[s * w:(s + 1) * w, :] for s in range(nq)]
    k_blocks = [x[:, 0:SWA_KV_DIM].astype(BF16) for x in kv_blocks]
    v_blocks = [x[:, SWA_KV_DIM:].astype(BF16) for x in kv_blocks]
    qpos = lax.broadcasted_iota(jnp.int32, (rows, 2 * w), 0) % w
    col = lax.broadcasted_iota(jnp.int32, (rows, 2 * w), 1)
    dist = qpos + w - col
    in_window = (dist >= 0) & (dist < w)
    first = in_window & ((col >= w) | (i > 0))
    lane_head = lax.broadcasted_iota(jnp.int32, (w, SWA_KV_DIM), 1) // SWA_HEAD_DIM
    scale = SWA_HEAD_DIM ** -0.5
    pairs = [(s, g) for s in range(nq) for g in range(SWA_GROUP)]
    k = [jnp.concatenate(k_blocks[s:s + 2], axis=0) for s in range(nq)]
    v = [jnp.concatenate(v_blocks[s:s + 2], axis=0) for s in range(nq)]
    logits = []
    for s, g in pairs:
        qg = q_ref[s * w:(s + 1) * w, g * SWA_KV_DIM:(g + 1) * SWA_KV_DIM].astype(F32) * scale
        qs = jnp.concatenate([jnp.where(lane_head == kvh, qg, 0.0) for kvh in range(SWA_KV_HEADS)], axis=0)
        lg = _dot_nt(qs.astype(BF16), k[s])
        logits.append(jnp.where(first if s == 0 else in_window, lg + bias_ref[g], NEG_INF))
    sink = [sink_ref[g] for _, g in pairs]
    m = [jnp.maximum(jnp.max(x, axis=-1, keepdims=True), sk) for x, sk in zip(logits, sink)]
    p = [jnp.exp(x - mi) for x, mi in zip(logits, m)]
    inv = [1.0 / (jnp.sum(pi, axis=-1, keepdims=True) + jnp.exp(sk - mi)) for pi, sk, mi in zip(p, sink, m)]
    ov = [_dot((pi * ii).astype(BF16), v[s]) for (s, _), pi, ii in zip(pairs, p, inv)]
    for (s, g), o in zip(pairs, ov):
        og = jnp.zeros((w, SWA_KV_DIM), F32)
        for kvh in range(SWA_KV_HEADS):
            og = jnp.where(lane_head == kvh, o[kvh * w:(kvh + 1) * w], og)
        o_ref[s * w:(s + 1) * w, g * SWA_KV_DIM:(g + 1) * SWA_KV_DIM] = og.astype(BF16)


def _swa_prompt(q, kv, bias, sink_rows, *, nq=8):
    t = q.shape[0]
    w = WINDOW
    rows = SWA_KV_HEADS * w
    return pl.pallas_call(
        functools.partial(_swa_prompt_kernel, nq=nq),
        out_shape=jax.ShapeDtypeStruct((t, SWA_Q_DIM), BF16),
        grid=(t // (nq * w),),
        in_specs=[
            pl.BlockSpec((nq * w, SWA_Q_DIM), lambda i: (i, 0)),
            pl.BlockSpec((nq * w, 2 * SWA_KV_DIM), lambda i: (i, 0)),
            pl.BlockSpec((w, 2 * SWA_KV_DIM), lambda i: (jnp.maximum(i * nq - 1, 0), 0)),
            pl.BlockSpec((SWA_GROUP, rows, 2 * w), lambda i: (0, 0, 0)),
            pl.BlockSpec((SWA_GROUP, rows, 1), lambda i: (0, 0, 0)),
        ],
        out_specs=pl.BlockSpec((nq * w, SWA_Q_DIM), lambda i: (i, 0)),
        compiler_params=_cparams("parallel"),
        name="swa_prompt",
    )(q, kv, kv, bias, sink_rows)


def _swa_sample_kernel(q_ref, kvn_ref, kb_ref, vb_ref, bias_b_ref, bias_n_ref, sink_ref, o_ref, ko_ref, vo_ref,
                       *, bb, t):
    gt = SWA_GROUP * t
    rows = SWA_KV_HEADS * gt
    w = kb_ref.shape[2]
    scale = SWA_HEAD_DIM ** -0.5
    lane_head = lax.broadcasted_iota(jnp.int32, (gt, SWA_KV_DIM), 1) // SWA_HEAD_DIM
    tok = lax.broadcasted_iota(jnp.int32, (rows, w), 0) % t
    keyj = lax.broadcasted_iota(jnp.int32, (rows, w), 1)
    valid_b = (tok + w - keyj) < WINDOW
    tok_n = lax.broadcasted_iota(jnp.int32, (rows, 1), 0) % t
    sink = sink_ref[...]
    bs = range(bb)
    toks = range(t)
    qall = [jnp.concatenate([jnp.where(lane_head == kvh, q_ref[b].astype(F32), 0.0) for kvh in range(SWA_KV_HEADS)],
                            axis=0) for b in bs]
    kvn = [kvn_ref[b] for b in bs]
    lb = [_dot(qall[b].astype(BF16), kb_ref[b].astype(BF16)) for b in bs]
    lb = [jnp.where(valid_b, lb[b] * scale + bias_b_ref[...], NEG_INF) for b in bs]
    ln = [[jnp.sum(qall[b] * kvn[b][j:j + 1, 0:SWA_KV_DIM], axis=-1, keepdims=True) for j in toks] for b in bs]
    ln = [[jnp.where(tok_n >= j, ln[b][j] * scale + bias_n_ref[:, j:j + 1], NEG_INF) for j in toks] for b in bs]
    m = [jnp.maximum(jnp.max(lb[b], axis=-1, keepdims=True), sink) for b in bs]
    m = [functools.reduce(jnp.maximum, ln[b], m[b]) for b in bs]
    pb = [jnp.exp(lb[b] - m[b]) for b in bs]
    pn = [[jnp.exp(ln[b][j] - m[b]) for j in toks] for b in bs]
    denom = [jnp.sum(pb[b], axis=-1, keepdims=True) + jnp.exp(sink - m[b]) for b in bs]
    inv = [1.0 / functools.reduce(jnp.add, pn[b], denom[b]) for b in bs]
    oall = [_dot_nt((pb[b] * inv[b]).astype(BF16), vb_ref[b].astype(BF16)) for b in bs]
    for b in bs:
        ob = oall[b]
        for j in toks:
            ob = ob + (pn[b][j] * inv[b]) * kvn[b][j:j + 1, SWA_KV_DIM:]
        og = jnp.zeros((gt, SWA_KV_DIM), F32)
        for kvh in range(SWA_KV_HEADS):
            og = jnp.where(lane_head == kvh, ob[kvh * gt:(kvh + 1) * gt], og)
        o_ref[b] = og.astype(BF16)
    pos = lax.broadcasted_iota(jnp.int32, (SWA_KV_DIM, w), 1)
    pad = jnp.zeros((w - SUBLANES, 2 * SWA_KV_DIM), F32)
    row8 = lax.broadcasted_iota(jnp.int32, (SUBLANES, 2 * SWA_KV_DIM), 0)
    for b in bs:
        last8 = jnp.zeros((SUBLANES, 2 * SWA_KV_DIM), F32)
        for j in toks:
            last8 = jnp.where(row8 == SUBLANES - t + j, kvn[b][j:j + 1], last8)
        tail_t = jnp.concatenate([pad, last8], axis=0).T
        ko_ref[b] = jnp.where(pos >= w - t, tail_t[:SWA_KV_DIM], pltpu.roll(kb_ref[b], w - t, axis=1))
        vo_ref[b] = jnp.where(pos >= w - t, tail_t[SWA_KV_DIM:], pltpu.roll(vb_ref[b], w - t, axis=1))


def _swa_sample(q, kvn, kbuf, vbuf, bias_b, bias_n, sink_rows, *, bb=8):
    b, gt, _ = q.shape
    t = kvn.shape[1]
    w = kbuf.shape[2]
    rows = SWA_KV_HEADS * gt
    blk = lambda i: (i, 0, 0)
    const = lambda i: (0, 0)
    cache = pl.BlockSpec((bb, SWA_KV_DIM, w), blk)
    return pl.pallas_call(
        functools.partial(_swa_sample_kernel, bb=bb, t=t),
        out_shape=(jax.ShapeDtypeStruct((b, gt, SWA_KV_DIM), BF16),
                   jax.ShapeDtypeStruct(kbuf.shape, F32), jax.ShapeDtypeStruct(vbuf.shape, F32)),
        grid=(b // bb,),
        in_specs=[
            pl.BlockSpec((bb, gt, SWA_KV_DIM), blk),
            pl.BlockSpec((bb, t, 2 * SWA_KV_DIM), blk),
            cache, cache,
            pl.BlockSpec((rows, w), const),
            pl.BlockSpec((rows, t), const),
            pl.BlockSpec((rows, 1), const),
        ],
        out_specs=(pl.BlockSpec((bb, gt, SWA_KV_DIM), blk), cache, cache),
        compiler_params=_cparams("parallel"),
        name="swa_sample",
    )(q, kvn, kbuf, vbuf, bias_b, bias_n, sink_rows)


def _mem_heads(q, mk, mv):
    scale = MEM_HEAD_DIM ** -0.5
    cols = [slice(h * MEM_HEAD_DIM, (h + 1) * MEM_HEAD_DIM) for h in range(MEM_HEADS)]
    x = [_dot_nt(q[:, sl], mk[:, sl]) * scale for sl in cols]
    m = [jnp.max(xi, axis=-1, keepdims=True) for xi in x]
    p = [jnp.exp(xi - mi) for xi, mi in zip(x, m)]
    inv = [1.0 / jnp.sum(pi, axis=-1, keepdims=True) for pi in p]
    return jnp.concatenate([_dot((pi * ii).astype(BF16), mv[:, sl]) for pi, ii, sl in zip(p, inv, cols)], axis=-1)


def _mem_prompt_kernel(q_ref, mk_ref, mv_ref, o_ref):
    o_ref[...] = _mem_heads(q_ref[...], mk_ref[...].astype(BF16), mv_ref[...].astype(BF16)).astype(BF16)


def _mem_prompt(q, mkv, *, tm=512):
    m = q.shape[0]
    return pl.pallas_call(
        _mem_prompt_kernel,
        out_shape=jax.ShapeDtypeStruct((m, MEM_DIM), BF16),
        grid=(m // tm,),
        in_specs=[
            pl.BlockSpec((tm, MEM_DIM), lambda i: (i, 0)),
            pl.BlockSpec((N_MEM, MEM_DIM), lambda i: (0, 0)),
            pl.BlockSpec((N_MEM, MEM_DIM), lambda i: (0, 1)),
        ],
        out_specs=pl.BlockSpec((tm, MEM_DIM), lambda i: (i, 0)),
        compiler_params=_cparams("parallel"),
        name="mem_prompt",
    )(q, mkv, mkv)


def _mem_sample_kernel(q_ref, mk_ref, mv_ref, o_ref, *, bb):
    scale = MEM_HEAD_DIM ** -0.5
    pairs = [(b, h) for b in range(bb) for h in range(MEM_HEADS)]
    rows = lambda h: pl.ds(h, N_MEM, stride=MEM_HEADS)
    cols = lambda h: slice(h * MEM_HEAD_DIM, (h + 1) * MEM_HEAD_DIM)
    q = [q_ref[b] for b in range(bb)]
    x = [_dot_nt(q[b][:, cols(h)], mk_ref[b, rows(h), :].astype(BF16)) * scale for b, h in pairs]
    m = [jnp.max(xi, axis=-1, keepdims=True) for xi in x]
    p = [jnp.exp(xi - mi) for xi, mi in zip(x, m)]
    inv = [1.0 / jnp.sum(pi, axis=-1, keepdims=True) for pi in p]
    o = [_dot((pi * ii).astype(BF16), mv_ref[b, rows(h), :].astype(BF16)) for (b, h), pi, ii in zip(pairs, p, inv)]
    for b in range(bb):
        o_ref[b] = jnp.concatenate(o[b * MEM_HEADS:(b + 1) * MEM_HEADS], axis=-1).astype(BF16)


def _mem_sample(q, mk, mv, *, bb=8):
    b, t, _ = q.shape
    blk = lambda i: (i, 0, 0)
    return pl.pallas_call(
        functools.partial(_mem_sample_kernel, bb=bb),
        out_shape=jax.ShapeDtypeStruct((b, t, MEM_DIM), BF16),
        grid=(b // bb,),
        in_specs=[
            pl.BlockSpec((bb, t, MEM_DIM), blk),
            pl.BlockSpec((bb, N_MEM * MEM_HEADS, MEM_HEAD_DIM), blk),
            pl.BlockSpec((bb, N_MEM * MEM_HEADS, MEM_HEAD_DIM), blk),
        ],
        out_specs=pl.BlockSpec((bb, t, MEM_DIM), blk),
        compiler_params=_cparams("parallel"),
        name="mem_sample",
    )(q, mk, mv)


def _head_sum(x, ones_ref):
    hi = x.astype(BF16)
    lo = (x - hi.astype(F32)).astype(BF16)
    ones = ones_ref[...]
    w = ones.shape[0]
    return jnp.concatenate([_dot(hi[:, c:c + w], ones) + _dot(lo[:, c:c + w], ones) for c in range(0, x.shape[1], w)],
                           axis=1)


def _token_shift(x_ref, pre_ref, start_ref, mu_ref, *, seq, tm, tile):
    x = x_ref[...]
    row = lax.broadcasted_iota(jnp.int32, x.shape, 0)
    shifted = pltpu.roll(x, 1, axis=0)
    if seq >= tm:
        is_start = (tile * tm) % seq == 0
        first_prev = jnp.where(is_start, start_ref[0], pre_ref[SUBLANES - 1:SUBLANES, :])
        prev = jnp.where(row == 0, first_prev, shifted)
    else:
        prev = jnp.where(row % seq == 0, start_ref[...], shifted)
    return x + mu_ref[...] * (prev - x)


def _rwkv_features(xs, w0_ref, ww2_ref, a0_ref, aw2_ref, gw2_ref, kk_ref, ka_ref, rk_ref, ones_ref):
    d = RWKV_DIM
    r = xs[:, 0:d]
    k = xs[:, d:2 * d]
    v = xs[:, 2 * d:3 * d]
    lw = xs[:, 3 * d:3 * d + LORA_W]
    la = xs[:, 3 * d + LORA_W:3 * d + LORA_W + LORA_A]
    lg = xs[:, 3 * d + LORA_W + LORA_A:]
    wpre = w0_ref[...] + _dot_hi(jnp.tanh(lw), ww2_ref[...])
    w = -jax.nn.softplus(-wpre) - 0.5
    log_decay = -jnp.exp(w)
    a = jax.nn.sigmoid(a0_ref[...] + _dot_hi(la, aw2_ref[...]))
    g = _dot_hi(jax.nn.sigmoid(lg), gw2_ref[...])
    kk = k * kk_ref[...]
    kk = kk / jnp.maximum(jnp.sqrt(_head_sum(kk * kk, ones_ref)), 1e-12)
    kh = k * (1.0 + (a - 1.0) * ka_ref[...])
    bonus = _head_sum(r * kh * rk_ref[...], ones_ref) * v
    return r, log_decay, kh, v, kk, kk * a, g, bonus


def _rwkv_prep_kernel(x_ref, pre_ref, start_ref, mu_ref, w0_ref, ww2_ref, a0_ref, aw2_ref, gw2_ref, kk_ref, ka_ref,
                      rk_ref, ones_ref, *outs, seq, tm):
    xs = _token_shift(x_ref, pre_ref, start_ref, mu_ref, seq=seq, tm=tm, tile=pl.program_id(0))
    feats = _rwkv_features(xs, w0_ref, ww2_ref, a0_ref, aw2_ref, gw2_ref, kk_ref, ka_ref, rk_ref, ones_ref)
    for o_ref, val in zip(outs, feats):
        o_ref[...] = val


def _rwkv_prep(x, shift0, p, ones, *, seq, tm=256):
    m = x.shape[0]
    row = lambda i: (i, 0)
    const = lambda i: (0, 0)
    vec = lambda n: pl.BlockSpec((1, n), const)
    out = jax.ShapeDtypeStruct((m, RWKV_DIM), F32)
    if seq >= tm:
        assert seq % tm == 0
        start = shift0
        start_spec = pl.BlockSpec((1, 1, RWKV_IN), lambda i: ((i * tm) // seq, 0, 0))
    else:
        assert tm % seq == 0
        start = jnp.repeat(shift0[:, 0], seq, axis=0)
        start_spec = pl.BlockSpec((tm, RWKV_IN), row)
    pre_blocks = tm // SUBLANES
    return pl.pallas_call(
        functools.partial(_rwkv_prep_kernel, seq=seq, tm=tm),
        out_shape=(out,) * 8,
        grid=(m // tm,),
        in_specs=[
            pl.BlockSpec((tm, RWKV_IN), row),
            pl.BlockSpec((SUBLANES, RWKV_IN), lambda i: (jnp.maximum(i * pre_blocks - 1, 0), 0)),
            start_spec,
            vec(RWKV_IN), vec(RWKV_DIM),
            pl.BlockSpec((LORA_W, RWKV_DIM), const),
            vec(RWKV_DIM),
            pl.BlockSpec((LORA_A, RWKV_DIM), const),
            pl.BlockSpec((LORA_G, RWKV_DIM), const),
            vec(RWKV_DIM), vec(RWKV_DIM), vec(RWKV_DIM),
            pl.BlockSpec(ones.shape, const),
        ],
        out_specs=(pl.BlockSpec((tm, RWKV_DIM), row),) * 8,
        compiler_params=_cparams("parallel"),
        name="rwkv_prep",
    )(x, x, start, p["mu"], p["w0"], p["w_w2"], p["a0"], p["a_w2"], p["g_w2"], p["k_k"], p["k_a"], p["r_k"], ones)


def _rwkv_lanes_kernel(r_ref, lw_ref, k_ref, v_ref, kk_ref, kka_ref, s_ref, y_ref, so_ref, v_scr, y_scr, *, t):
    n = RWKV_HEAD_DIM
    nb = s_ref.shape[-1]
    heads = range(2)

    def token_major(ref, j):
        return ref[pl.ds(j, nb, stride=t), :].T

    for j in range(t):
        v_scr[...] = token_major(v_ref, j)
        r_t, k_t, kk_t, kka_t = (token_major(ref, j) for ref in (r_ref, k_ref, kk_ref, kka_ref))
        w_t = jnp.exp(token_major(lw_ref, j))
        src = s_ref if j == 0 else so_ref

        def value_group(g, carry):
            rows = pl.multiple_of(g * SUBLANES, SUBLANES)
            ys = [[] for _ in heads]
            vg = [v_scr[pl.ds(h * n + rows, SUBLANES), :] for h in heads]
            for i in range(SUBLANES):
                for h in heads:
                    f = slice(h * n, (h + 1) * n)
                    s = src[h, rows + i]
                    sa = jnp.sum(s * (-kk_t[f]), axis=0, keepdims=True)
                    s = s * w_t[f] + sa * kka_t[f] + vg[h][i:i + 1] * k_t[f]
                    so_ref[h, rows + i] = s
                    ys[h].append(jnp.sum(s * r_t[f], axis=0, keepdims=True))
            for h in heads:
                y_scr[pl.ds(h * n + rows, SUBLANES), :] = jnp.concatenate(ys[h], axis=0)
            return carry

        lax.fori_loop(0, n // SUBLANES, value_group, 0)
        y_ref[j] = y_scr[...].T


def _rwkv_lanes(r, lw, k, v, kk, kka, s0, *, t):
    m, d = r.shape
    nb = m // t
    n = RWKV_HEAD_DIM
    assert nb == LANES, "one batch per lane"
    tok = pl.BlockSpec((m, LANES), lambda p: (0, p))
    st = pl.BlockSpec((2, n, n, nb), lambda p: (p, 0, 0, 0))
    return pl.pallas_call(
        functools.partial(_rwkv_lanes_kernel, t=t),
        out_shape=(jax.ShapeDtypeStruct((t, nb, d), F32), jax.ShapeDtypeStruct((RWKV_HEADS, n, n, nb), F32)),
        grid=(RWKV_HEADS // 2,),
        in_specs=[tok] * 6 + [st],
        out_specs=(pl.BlockSpec((t, nb, LANES), lambda p: (0, 0, p)), st),
        scratch_shapes=[pltpu.VMEM((LANES, nb), F32), pltpu.VMEM((LANES, nb), F32)],
        compiler_params=_cparams("parallel"),
        name="rwkv_lanes",
    )(r, lw, k, v, kk, kka, s0)


CHUNK = 64
GROUP_HEADS = 4
GROUP_W = GROUP_HEADS * RWKV_HEAD_DIM
N_GROUPS = RWKV_HEADS // GROUP_HEADS
(MASK_SAME, MASK_STRICT, MASK_INCL, MASK_LEVEL0) = (0, 1, 2, 3)
N_LEVELS = int(math.log2(CHUNK))


def _chunk_masks():
    i = np.arange(GROUP_W)
    same = (i[:, None] // CHUNK) == (i[None, :] // CHUNK)
    masks = [same, same & (i[None, :] < i[:, None]), same & (i[None, :] <= i[:, None])]
    for lvl in range(N_LEVELS):
        m = 1 << lvl
        masks.append(((i[:, None] // (2 * m)) == (i[None, :] // (2 * m))) & ((i[:, None] // m) != (i[None, :] // m))
                     & (i[None, :] < i[:, None]))
    return np.stack(masks).astype(np.float32)


def _rwkv_prompt_kernel(x_ref, pre_ref, start_ref, mu_ref, w0_ref, ww2_ref, a0_ref, aw2_ref, gw2_ref, kk_p_ref,
                        ka_ref, rk_ref, ones_ref, lnw_ref, lnb_ref, st0_ref, tri_ref, eye_ref, mask_ref,
                        *rest, seq, tm, n_tiles, n_riders):
    rider_in, (o_ref, sto_ref), rider_out = rest[:n_riders], rest[n_riders:n_riders + 2], rest[n_riders + 2:-11]
    st_scr, xs_ref, y_ref, r_set, lw_set, k_set, v_set, kk_set, kka_set, g_set, bonus_set = rest[-11:]
    for src, dst in zip(rider_in, rider_out):
        dst[...] = src[...].astype(BF16)
    step = pl.program_id(0)
    cur = (step + 1) % 2
    nxt = step % 2
    sets = (r_set, lw_set, k_set, v_set, kk_set, kka_set, g_set, bonus_set)
    r_ref, lw_ref, k_ref, v_ref, kk_ref, kka_ref = (s.at[cur] for s in sets[:6])

    @pl.when(step == 0)
    def _():
        st_scr[...] = st0_ref[...]
        for s in sets:
            s[1] = jnp.zeros(s.shape[1:], F32)

    xs_ref[...] = _token_shift(x_ref, pre_ref, start_ref, mu_ref, seq=seq, tm=tm, tile=jnp.minimum(step, n_tiles - 1))
    n_chunks = tm // CHUNK

    piece = 2 * CHUNK
    def features(c):
        rows = slice(c * piece, (c + 1) * piece)
        feats = _rwkv_features(xs_ref[rows, :], w0_ref, ww2_ref, a0_ref, aw2_ref, gw2_ref, kk_p_ref, ka_ref, rk_ref,
                               ones_ref)
        for s, val in zip(sets, feats):
            s[nxt, rows, :] = val

    eye = eye_ref[...]
    tri = tri_ref[...]
    tile_rows = lambda x: jnp.concatenate([x] * GROUP_HEADS, axis=0)
    block_diag = lambda x: (tile_rows(x) * mask_ref[MASK_SAME]).astype(BF16)

    chains = [(slice(c * CHUNK, (c + 1) * CHUNK), slice(g * GROUP_W, (g + 1) * GROUP_W))
              for c in range(n_chunks) for g in range(N_GROUPS)]
    each = lambda f, *cols: [f(*args) for args in zip(*cols)]
    same, strict, incl = mask_ref[MASK_SAME], mask_ref[MASK_STRICT], mask_ref[MASK_INCL]

    def cum_decay(lw):
        h1 = lw.astype(BF16)
        r1 = lw - h1.astype(F32)
        h2 = r1.astype(BF16)
        h3 = (r1 - h2.astype(F32)).astype(BF16)
        return _dot(tri, h1) + _dot(tri, h2) + _dot(tri, h3)

    lw = [lw_ref[rows, sl] for rows, sl in chains]
    kka = [kka_ref[rows, sl] for rows, sl in chains]
    k = [k_ref[rows, sl] for rows, sl in chains]
    cum = each(cum_decay, lw)
    cum_last = each(lambda c: c[CHUNK - 1:CHUNK, :], cum)
    p_inv = each(lambda c: jnp.exp(-c), cum)
    p_tail = each(lambda c, cl: jnp.exp(cl - c), cum, cum_last)
    a_bd = [block_diag(-kk_ref[rows, sl] * jnp.exp(c - l)) for (rows, sl), c, l in zip(chains, cum, lw)]
    r_f = [tile_rows(r_ref[rows, sl] * jnp.exp(c)) * same for (rows, sl), c in zip(chains, cum)]
    r_bd = each(lambda x: x.astype(BF16), r_f)
    v_bd = [block_diag(v_ref[rows, sl]) for rows, sl in chains]
    b_rep = each(lambda x, p: tile_rows((x * p).astype(BF16)), kka, p_inv)
    k_rep = each(lambda x, p: tile_rows((x * p).astype(BF16)), k, p_inv)
    bh_rep = each(lambda x, p: tile_rows(x * p), kka, p_tail)
    kh_rep = each(lambda x, p: tile_rows(x * p), k, p_tail)

    l_ab_f = each(lambda a, b: _dot_nt(a, b) * strict, a_bd, b_rep)
    l_ab = each(lambda x: x.astype(BF16), l_ab_f)
    l_ak = each(lambda a, b: (_dot_nt(a, b) * strict).astype(BF16), a_bd, k_rep)
    m_rb = each(lambda a, b: (_dot_nt(a, b) * incl).astype(BF16), r_bd, b_rep)
    m_rk = each(lambda a, b: (_dot_nt(a, b) * incl).astype(BF16), r_bd, k_rep)
    bh_t = each(lambda x: (x.T * same).astype(BF16), bh_rep)
    kh_t = each(lambda x: (x.T * same).astype(BF16), kh_rep)

    pending = list(range(tm // piece))

    def next_features():
        if pending:
            features(pending.pop(0))

    d = each(lambda l: eye + l * mask_ref[MASK_LEVEL0], l_ab_f)
    for lvl in range(1, N_LEVELS):
        d_b = each(lambda x: x.astype(BF16), d)
        x = each(lambda l, db: (_dot(l, db) * mask_ref[MASK_LEVEL0 + lvl]).astype(BF16), l_ab, d_b)
        next_features()
        d = each(lambda dd, db, xx: dd + _dot(db, xx), d, d_b, x)
    t_b = each(lambda x: x.astype(BF16), d)
    while pending:
        next_features()

    wm = each(lambda a, b, vv: _dot(jnp.concatenate([a, b], axis=0), vv), l_ak, m_rk, v_bd)
    twa = each(lambda t, w, a: _dot(t, jnp.concatenate([w[:GROUP_W].astype(BF16), a], axis=1)).astype(BF16),
               t_b, wm, a_bd)
    ry = each(_dot, m_rb, twa)
    mn = each(_dot, bh_t, twa)
    khv = each(_dot, kh_t, v_bd)
    y0 = each(lambda a, w: a[:, :GROUP_W] + w[GROUP_W:], ry, wm)
    n_x = each(lambda a, b: a[:, :GROUP_W] + b, mn, khv)
    mr = each(lambda a, cl, rf, b: jnp.concatenate(
        [(eye * jnp.exp(cl) + a[:, GROUP_W:]).astype(BF16), (rf + b[:, GROUP_W:]).astype(BF16)], axis=0),
        mn, cum_last, r_f, ry)

    st = [st_scr[g] for g in range(N_GROUPS)]
    for i, (rows, sl) in enumerate(chains):
        g = i % N_GROUPS
        ys = _dot(mr[i], st[g].astype(BF16))
        st[g] = ys[:GROUP_W] + n_x[i]
        y_bd = ys[GROUP_W:] + y0[i]
        y = y_bd[0:CHUNK]
        for h in range(1, GROUP_HEADS):
            y = y + y_bd[h * CHUNK:(h + 1) * CHUNK]
        y_ref[rows, sl] = y
    for g in range(N_GROUPS):
        st_scr[g] = st[g]
    o_ref[...] = _rwkv_output(y_ref[...], bonus_set[cur], g_set[cur], lnw_ref, lnb_ref, ones_ref)

    @pl.when(step == pl.num_programs(0) - 1)
    def _():
        sto_ref[...] = st_scr[...]


def _rwkv_prompt(x, shift0, s0, p, ones, *, riders=(), chunks_per_step=4):
    t = x.shape[0]
    d = RWKV_DIM
    n = RWKV_HEAD_DIM
    tt = CHUNK * chunks_per_step
    assert CHUNK == n and t % tt == 0
    st0 = jnp.einsum("ghvk,hj->ghkjv", s0.reshape(N_GROUPS, GROUP_HEADS, n, n), jnp.eye(GROUP_HEADS, dtype=F32))
    st0 = st0.reshape(N_GROUPS, GROUP_W, GROUP_W)
    tri = jnp.asarray(np.tril(np.ones((CHUNK, CHUNK), np.float32)), BF16)
    eye = jnp.eye(GROUP_W, dtype=F32)
    masks = jnp.asarray(_chunk_masks())
    const = lambda c: (0, 0)
    vec = lambda width: pl.BlockSpec((1, width), const)
    st_spec = pl.BlockSpec((N_GROUPS, GROUP_W, GROUP_W), lambda c: (0, 0, 0))
    feature_set = pltpu.VMEM((2, tt, d), F32)
    pre_blocks = tt // SUBLANES
    n_tiles = t // tt
    fill = lambda c: jnp.minimum(c, n_tiles - 1)
    rider_specs = _rider_specs(riders, n_tiles + 1, lambda c: c)
    o, st, *rounded = pl.pallas_call(
        functools.partial(_rwkv_prompt_kernel, seq=t, tm=tt, n_tiles=n_tiles, n_riders=len(riders)),
        out_shape=[jax.ShapeDtypeStruct((t, d), BF16), jax.ShapeDtypeStruct((N_GROUPS, GROUP_W, GROUP_W), F32)]
        + [jax.ShapeDtypeStruct(arr.shape, BF16) for arr in riders],
        grid=(n_tiles + 1,),
        in_specs=[
            pl.BlockSpec((tt, RWKV_IN), lambda c: (fill(c), 0)),
            pl.BlockSpec((SUBLANES, RWKV_IN), lambda c: (jnp.maximum(fill(c) * pre_blocks - 1, 0), 0)),
            pl.BlockSpec((1, 1, RWKV_IN), lambda c: (0, 0, 0)),
            vec(RWKV_IN), vec(d),
            pl.BlockSpec((LORA_W, d), const),
            vec(d),
            pl.BlockSpec((LORA_A, d), const),
            pl.BlockSpec((LORA_G, d), const),
            vec(d), vec(d), vec(d),
            pl.BlockSpec(ones.shape, const),
            vec(d), vec(d),
            st_spec,
            pl.BlockSpec((CHUNK, CHUNK), const),
            pl.BlockSpec((GROUP_W, GROUP_W), const),
            pl.BlockSpec(masks.shape, lambda c: (0, 0, 0)),
        ] + rider_specs,
        out_specs=[pl.BlockSpec((tt, d), lambda c: (jnp.maximum(c - 1, 0), 0)), st_spec] + rider_specs,
        scratch_shapes=[pltpu.VMEM((N_GROUPS, GROUP_W, GROUP_W), F32), pltpu.VMEM((tt, RWKV_IN), F32),
                        pltpu.VMEM((tt, d), F32)] + [feature_set] * 8,
        compiler_params=_cparams("arbitrary"),
        name="rwkv_prompt",
    )(x, x, shift0, p["mu"], p["w0"], p["w_w2"], p["a0"], p["a_w2"], p["g_w2"], p["k_k"], p["k_a"], p["r_k"], ones,
      p["ln_w"], p["ln_b"], st0, tri, eye, masks, *riders)
    st5 = st.reshape(N_GROUPS, GROUP_HEADS, n, GROUP_HEADS, n)
    s_new = jnp.einsum("ghkjv,hj->ghvk", st5, jnp.eye(GROUP_HEADS, dtype=F32)).reshape(RWKV_HEADS, n, n)
    return o, s_new, rounded


def _rwkv_output(y, bonus, gate, lnw_ref, lnb_ref, ones_ref):
    inv_n = 1.0 / RWKV_HEAD_DIM
    mu = _head_sum(y, ones_ref) * inv_n
    dlt = y - mu
    var = _head_sum(dlt * dlt, ones_ref) * inv_n
    yn = dlt * lax.rsqrt(var + GN_EPS) * lnw_ref[...] + lnb_ref[...]
    return ((yn + bonus) * gate).astype(BF16)


def _rwkv_post_kernel(y_ref, bonus_ref, g_ref, lnw_ref, lnb_ref, ones_ref, o_ref):
    o_ref[...] = _rwkv_output(y_ref[...], bonus_ref[...], g_ref[...], lnw_ref, lnb_ref, ones_ref)


def _rwkv_post(y, bonus, g, lnw, lnb, ones, *, tm=256):
    m = y.shape[0]
    row = lambda i: (i, 0)
    const = lambda i: (0, 0)
    tile = pl.BlockSpec((tm, RWKV_DIM), row)
    return pl.pallas_call(
        _rwkv_post_kernel,
        out_shape=jax.ShapeDtypeStruct((m, RWKV_DIM), BF16),
        grid=(m // tm,),
        in_specs=[tile, tile, tile, pl.BlockSpec((1, RWKV_DIM), const), pl.BlockSpec((1, RWKV_DIM), const),
                  pl.BlockSpec(ones.shape, const)],
        out_specs=tile,
        compiler_params=_cparams("parallel"),
        name="rwkv_post",
    )(y, bonus, g, lnw, lnb, ones)


GATE_BLOCK = math.gcd(PROJ_DIM, D_MODEL)


def _merge_kernel(h_ref, u_ref, os_ref, or_ref, om_ref, wos_ref, wor_ref, wom_ref, wout_ref, *rest, parts):
    gate_refs, (o_ref, acc_ref) = rest[:N_BRANCH * parts], rest[N_BRANCH * parts:]
    j = pl.program_id(1)

    @pl.when(j == 0)
    def _():
        acc_ref[...] = jnp.zeros_like(acc_ref)

    u = u_ref[...]
    merged = None
    for b, (x_ref, w_ref) in enumerate(((os_ref, wos_ref), (or_ref, wor_ref), (om_ref, wom_ref))):
        gate = jnp.concatenate([_dot(u, gate_refs[b * parts + c][...]) for c in range(parts)], axis=1)
        term = jax.nn.sigmoid(gate) * _dot(x_ref[...], w_ref[...])
        merged = term if merged is None else merged + term
    acc_ref[...] += _dot(merged.astype(BF16), wout_ref[...])

    @pl.when(j == pl.num_programs(1) - 1)
    def _():
        o_ref[...] = h_ref[...] + acc_ref[...]


def _merge(h, u, o_swa, o_rw, o_mem, w_in, wo_swa, wo_rw, wo_mem, w_out, *, tm=512, tn=512):
    m, d = h.shape
    nt = d // tn
    parts = tn // GATE_BLOCK
    g0 = PROJ_DIM // GATE_BLOCK
    row = lambda i, j: (i, 0)
    col = lambda i, j: (0, j)
    gate_specs = [pl.BlockSpec((d, GATE_BLOCK), functools.partial(
        lambda i, j, off: (0, off + j * parts), off=g0 + b * (d // GATE_BLOCK) + c))
        for b in range(N_BRANCH) for c in range(parts)]
    return pl.pallas_call(
        functools.partial(_merge_kernel, parts=parts),
        out_shape=jax.ShapeDtypeStruct((m, d), F32),
        grid=(m // tm, nt),
        in_specs=[
            pl.BlockSpec((tm, d), row),
            pl.BlockSpec((tm, d), row),
            pl.BlockSpec((tm, SWA_Q_DIM), row),
            pl.BlockSpec((tm, RWKV_DIM), row),
            pl.BlockSpec((tm, MEM_DIM), row),
            pl.BlockSpec((SWA_Q_DIM, tn), col),
            pl.BlockSpec((RWKV_DIM, tn), col),
            pl.BlockSpec((MEM_DIM, tn), col),
            pl.BlockSpec((tn, d), lambda i, j: (j, 0)),
        ] + gate_specs,
        out_specs=pl.BlockSpec((tm, d), row),
        scratch_shapes=[pltpu.VMEM((tm, d), F32)],
        compiler_params=_cparams("parallel", "arbitrary"),
        name="merge",
    )(h, u, o_swa, o_rw, o_mem, wo_swa, wo_rw, wo_mem, w_out, *([w_in] * (N_BRANCH * parts)))


def _t5_bucket(dist):
    max_exact = N_BUCKETS // 2
    d = np.maximum(dist, 0)
    log_ratio = (np.log(np.maximum(d, 1).astype(np.float32) / np.float32(max_exact))
                 / np.float32(math.log(MAX_DISTANCE / max_exact)))
    large = np.minimum(max_exact + (log_ratio * (N_BUCKETS - max_exact)).astype(np.int32), N_BUCKETS - 1)
    return np.where(d < max_exact, d, large).astype(np.int32)


def _rel_bias(table, dist):
    onehot = np.eye(N_BUCKETS, dtype=np.float32)[_t5_bucket(dist).reshape(-1)]
    bias = jnp.einsum("nb,bh->hn", jnp.asarray(onehot), table, precision=lax.Precision.HIGHEST)
    return bias.reshape(SWA_HEADS, *dist.shape)


def _rwkv_branch(xr, shift0, s0, p, ones, riders=()):
    b, t, _ = xr.shape
    flat = lambda z: z.reshape(b * t, z.shape[-1])
    if b == 1:
        o, s_new, rounded = _rwkv_prompt(flat(xr), shift0, s0[0], p, ones, riders=riders)
        return o, s_new[None], rounded
    assert b == LANES and not riders, "short sequences are batched one per lane"
    r, w, k, v, kk, kka, g, bonus = _rwkv_prep(flat(xr), shift0, p, ones, seq=t)
    y, s_new = _rwkv_lanes(r, w, k, v, kk, kka, jnp.transpose(s0, (1, 2, 3, 0)), t=t)
    o = _rwkv_post(flat(jnp.transpose(y, (1, 0, 2))), bonus, g, p["ln_w"], p["ln_b"], ones)
    return o, jnp.transpose(s_new, (3, 0, 1, 2))


def kernel(x_prompt, mem_prompt, x_sample, cache_swa_k, cache_swa_v, state_rwkv, state_rwkv_shift, cache_mem_k, cache_mem_v, ffn1_norm, ffn1_wi, ffn1_wo, mix_norm, w_in, swa_sinks, rel_bias_table, rwkv_mu, rwkv_w0, rwkv_w_w2, rwkv_a0, rwkv_a_w2, rwkv_g_w2, rwkv_k_k, rwkv_k_a, rwkv_r_k, rwkv_ln_w, rwkv_ln_b, mem_norm, w_mem_kv, w_o_swa, w_o_rwkv, w_o_mem, w_out, ffn2_norm, ffn2_wi, ffn2_wo, final_norm):
    assert ffn1_wi.shape[0] == 1, "single-layer trunk"
    bp, tp, d = x_prompt.shape
    bs, ts, _ = x_sample.shape
    assert bp == 1
    row = lambda z: z.reshape(1, -1).astype(F32)

    g1, gm, g2, gf = row(ffn1_norm[0]), row(mix_norm[0]), row(ffn2_norm[0]), row(final_norm)
    rp = {
        "mu": row(rwkv_mu[0]), "w0": row(rwkv_w0[0]), "w_w2": rwkv_w_w2[0], "a0": row(rwkv_a0[0]),
        "a_w2": rwkv_a_w2[0], "g_w2": rwkv_g_w2[0], "k_k": row(rwkv_k_k[0]), "k_a": row(rwkv_k_a[0]),
        "r_k": row(rwkv_r_k[0]), "ln_w": row(rwkv_ln_w[0]), "ln_b": row(rwkv_ln_b[0]),
    }
    seg = np.arange(GROUP_W) // RWKV_HEAD_DIM
    ones = jnp.asarray(seg[:, None] == seg[None, :], dtype=BF16)
    sinks = swa_sinks[0].astype(F32)
    table = rel_bias_table.astype(F32)

    xp = x_prompt.reshape(tp, d)
    xs = x_sample.reshape(bs * ts, d)
    hs, wg1, wu1, wo1 = _ffn(xs, g1, ffn1_wi[0], ffn1_wo[0], gf, final_norm=False, emit_weights=True, tf=256)
    hp, w_in_b, wo_swa, wo_rw, wo_mem, w_out_b = _ffn(
        xp, g1, (wg1, wu1), wo1, gf, final_norm=False,
        riders=(w_in[0], w_o_swa[0], w_o_rwkv[0], w_o_mem[0], w_out[0]))
    wo_swa = wo_swa.reshape(SWA_KV_HEADS, SWA_GROUP, SWA_HEAD_DIM, d).transpose(1, 0, 2, 3).reshape(SWA_Q_DIM, d)
    w_q = w_in_b[:, :SWA_Q_DIM].reshape(d, SWA_KV_HEADS, SWA_GROUP, SWA_HEAD_DIM).transpose(0, 2, 1, 3).reshape(d, SWA_Q_DIM)
    qp, kvp, xrp, qmp, up = _inproj(hp, gm, w_q, w_in_b)
    qs, kvs, xrs, qms, us = _inproj(hs, gm, w_q, w_in_b)

    w = WINDOW
    dist_p = np.arange(w)[:, None] + w - np.arange(2 * w)[None, :]
    bias_p = _rel_bias(table, dist_p).reshape(SWA_KV_HEADS, SWA_GROUP, w, 2 * w).transpose(1, 0, 2, 3)
    bias_p = bias_p.reshape(SWA_GROUP, SWA_KV_HEADS * w, 2 * w)
    sink_p = jnp.repeat(sinks.reshape(SWA_KV_HEADS, SWA_GROUP).T, w, axis=1).reshape(SWA_GROUP, SWA_KV_HEADS * w, 1)
    o_swa_p = _swa_prompt(qp, kvp, bias_p, sink_p)

    wbuf = cache_swa_k.shape[2]
    dist_s = np.arange(ts)[:, None] + wbuf - np.arange(wbuf + ts)[None, :]
    bias_s = _rel_bias(table, dist_s).reshape(SWA_HEADS * ts, wbuf + ts)
    sink_rows = jnp.repeat(sinks, ts).reshape(SWA_HEADS * ts, 1)
    qs_gt = qs.reshape(bs, ts, SWA_GROUP, SWA_KV_DIM).transpose(0, 2, 1, 3).reshape(bs, SWA_GROUP * ts, SWA_KV_DIM)
    kbuf = cache_swa_k[0].reshape(bs, wbuf, SWA_KV_DIM).transpose(0, 2, 1)
    vbuf = cache_swa_v[0].reshape(bs, wbuf, SWA_KV_DIM).transpose(0, 2, 1)
    o_swa_s, knew_t, vnew_t = _swa_sample(qs_gt, kvs.reshape(bs, ts, 2 * SWA_KV_DIM), kbuf, vbuf,
                                          bias_s[:, :wbuf], bias_s[:, wbuf:], sink_rows)
    o_swa_s = o_swa_s.reshape(bs, SWA_GROUP, ts, SWA_KV_DIM).transpose(0, 2, 1, 3).reshape(bs * ts, SWA_Q_DIM)

    zero_shift = jnp.zeros((bp, 1, RWKV_IN), F32)
    zero_state = jnp.zeros((bp, RWKV_HEADS, RWKV_HEAD_DIM, RWKV_HEAD_DIM), F32)
    o_rw_p, state_p, (wi2, wo2) = _rwkv_branch(xrp.reshape(bp, tp, RWKV_IN), zero_shift, zero_state, rp, ones,
                                               riders=(ffn2_wi[0], ffn2_wo[0]))
    o_rw_s, state_s = _rwkv_branch(xrs.reshape(bs, ts, RWKV_IN), state_rwkv_shift[0], state_rwkv[0], rp, ones)

    mkv = _norm_matmul(mem_prompt.reshape(N_MEM, d), row(mem_norm[0]), w_mem_kv[0].astype(BF16))
    o_mem_p = _mem_prompt(qmp, mkv)
    o_mem_s = _mem_sample(qms.reshape(bs, ts, MEM_DIM), cache_mem_k[0].reshape(bs, N_MEM * MEM_HEADS, MEM_HEAD_DIM),
                          cache_mem_v[0].reshape(bs, N_MEM * MEM_HEADS, MEM_HEAD_DIM)).reshape(bs * ts, MEM_DIM)

    hp = _merge(hp, up, o_swa_p, o_rw_p, o_mem_p, w_in_b, wo_swa, wo_rw, wo_mem, w_out_b)
    hs = _merge(hs, us, o_swa_s, o_rw_s, o_mem_s, w_in_b, wo_swa, wo_rw, wo_mem, w_out_b)
    y_prompt = _ffn(hp, g2, wi2, wo2, gf, final_norm=True).reshape(bp, tp, d)
    y_sample = _ffn(hs, g2, wi2, wo2, gf, final_norm=True).reshape(bs, ts, d)

    wp = min(w, tp)
    p_k = kvp[tp - wp:, :SWA_KV_DIM].reshape(1, bp, wp, SWA_KV_HEADS, SWA_HEAD_DIM)
    p_v = kvp[tp - wp:, SWA_KV_DIM:].reshape(1, bp, wp, SWA_KV_HEADS, SWA_HEAD_DIM)
    p_mk = mkv[:, :MEM_DIM].reshape(1, bp, N_MEM, MEM_HEADS, MEM_HEAD_DIM)
    p_mv = mkv[:, MEM_DIM:].reshape(1, bp, N_MEM, MEM_HEADS, MEM_HEAD_DIM)
    s_k = knew_t.transpose(0, 2, 1).reshape(1, bs, wbuf, SWA_KV_HEADS, SWA_HEAD_DIM)
    s_v = vnew_t.transpose(0, 2, 1).reshape(1, bs, wbuf, SWA_KV_HEADS, SWA_HEAD_DIM)
    return (y_prompt, y_sample,
            p_k, p_v, state_p[None], xrp[tp - 1:].reshape(1, bp, 1, RWKV_IN), p_mk, p_mv,
            s_k, s_v, state_s[None], xrs.reshape(bs, ts, RWKV_IN)[:, ts - 1:][None])
```

```python
import functools
import math

import jax
import jax.numpy as jnp
import numpy as np
from jax import lax
from jax.experimental import pallas as pl
from jax.experimental.pallas import tpu as pltpu

F32 = jnp.float32
BF16 = jnp.bfloat16

D_MODEL = 2048
SWA_HEADS = 16
SWA_KV_HEADS = 4
SWA_GROUP = SWA_HEADS // SWA_KV_HEADS
SWA_HEAD_DIM = 64
SWA_Q_DIM = SWA_HEADS * SWA_HEAD_DIM
SWA_KV_DIM = SWA_KV_HEADS * SWA_HEAD_DIM
WINDOW = 128
N_BUCKETS = 32
MAX_DISTANCE = 128
RWKV_HEADS = 8
RWKV_HEAD_DIM = 64
RWKV_DIM = RWKV_HEADS * RWKV_HEAD_DIM
LORA_W = 64
LORA_A = 64
LORA_G = 128
RWKV_IN = 3 * RWKV_DIM + LORA_W + LORA_A + LORA_G
N_MEM = 256
MEM_HEADS = 4
MEM_HEAD_DIM = 128
MEM_DIM = MEM_HEADS * MEM_HEAD_DIM
N_BRANCH = 3
PROJ_DIM = SWA_Q_DIM + 2 * SWA_KV_DIM + RWKV_IN + MEM_DIM
NORM_EPS = 1e-6
GN_EPS = 64e-5
NEG_INF = -1e30

LANES = 128
SUBLANES = 8
VMEM_LIMIT = 56 * 1024 * 1024


def _cparams(*sem):
    return pltpu.CompilerParams(dimension_semantics=sem, vmem_limit_bytes=VMEM_LIMIT)


def _rms(x, g):
    return x * lax.rsqrt(jnp.mean(x * x, axis=-1, keepdims=True) + NORM_EPS) * g


def _dot(a, b):
    return jnp.dot(a, b, preferred_element_type=F32)


def _dot_nt(a, b):
    return lax.dot_general(a, b, (((1,), (1,)), ((), ())), preferred_element_type=F32)


def _dot_hi(a, b):
    a_hi = a.astype(BF16)
    b_hi = b.astype(BF16)
    a_lo = (a - a_hi.astype(F32)).astype(BF16)
    b_lo = (b - b_hi.astype(F32)).astype(BF16)
    return _dot(a_hi, b_hi) + _dot(a_hi, b_lo) + _dot(a_lo, b_hi)


def _ffn_kernel(x_ref, g_ref, wg_ref, wu_ref, wo_ref, gf_ref, *rest, final_norm, n_riders, emit_weights):
    rider_in, (o_ref, *extra_out), (xn_ref, acc_ref) = rest[:n_riders], rest[n_riders:-2], rest[-2:]
    for src, dst in zip(rider_in, extra_out[:n_riders]):
        dst[...] = src[...].astype(BF16)
    j = pl.program_id(1)

    @pl.when(j == 0)
    def _():
        xn_ref[...] = _rms(x_ref[...], g_ref[...]).astype(BF16)
        acc_ref[...] = jnp.zeros_like(acc_ref)

    wg, wu, wo = wg_ref[...], wu_ref[...], wo_ref[...]
    if emit_weights:
        wg, wu, wo = wg.astype(BF16), wu.astype(BF16), wo.astype(BF16)
        for dst, val in zip(extra_out[n_riders:], (wg, wu, wo)):
            dst[...] = val
    xn = xn_ref[...]
    gate = _dot(xn, wg)
    up = _dot(xn, wu)
    act = (gate * jax.nn.sigmoid(gate)) * up
    acc_ref[...] += _dot(act.astype(BF16), wo)

    @pl.when(j == pl.num_programs(1) - 1)
    def _():
        h = x_ref[...] + 0.5 * acc_ref[...]
        if final_norm:
            h = _rms(h, gf_ref[...])
        o_ref[...] = h


def _rider_tiling(shape, steps):
    rows, cols = shape
    best, best_score = (1, 1), (0, 0)
    for nr in range(1, rows // 16 + 1):
        if rows % nr or (rows // nr) % 16:
            continue
        for nc in range(1, cols // LANES + 1):
            if cols % nc or (cols // nc) % LANES or nr * nc > steps:
                continue
            score = (min(cols // nc, 1024), nr * nc)
            if score > best_score:
                best, best_score = (nr, nc), score
    return best


def _rider_specs(riders, steps, step_of):
    specs = []
    for arr in riders:
        nr, nc = _rider_tiling(arr.shape, steps)
        specs.append(pl.BlockSpec((arr.shape[0] // nr, arr.shape[1] // nc), functools.partial(
            lambda *idx, nc, last: (jnp.minimum(step_of(*idx), last) // nc, jnp.minimum(step_of(*idx), last) % nc),
            nc=nc, last=nr * nc - 1)))
    return specs


def _ffn(x, g, wi, wo, gf, *, final_norm, riders=(), emit_weights=False, tm=512, tf=512):
    m, d = x.shape
    dff = wo.shape[0]
    nf = dff // tf
    steps = (m // tm) * nf
    wg, wu = wi if isinstance(wi, tuple) else (wi, wi)
    up_off = 0 if isinstance(wi, tuple) else nf
    assert not emit_weights or m == tm
    weight_specs = [pl.BlockSpec((d, tf), lambda i, j: (0, j)), pl.BlockSpec((d, tf), lambda i, j: (0, j)),
                    pl.BlockSpec((tf, d), lambda i, j: (j, 0))]
    weight_out = [jax.ShapeDtypeStruct((d, dff), BF16)] * 2 + [jax.ShapeDtypeStruct((dff, d), BF16)]
    rider_specs = _rider_specs(riders, steps, lambda i, j: i * nf + j)
    out = pl.pallas_call(
        functools.partial(_ffn_kernel, final_norm=final_norm, n_riders=len(riders), emit_weights=emit_weights),
        out_shape=([jax.ShapeDtypeStruct((m, d), F32)] + [jax.ShapeDtypeStruct(arr.shape, BF16) for arr in riders]
                   + (weight_out if emit_weights else [])),
        grid=(m // tm, nf),
        in_specs=[
            pl.BlockSpec((tm, d), lambda i, j: (i, 0)),
            pl.BlockSpec((1, d), lambda i, j: (0, 0)),
            pl.BlockSpec((d, tf), lambda i, j: (0, j)),
            pl.BlockSpec((d, tf), lambda i, j: (0, j + up_off)),
            pl.BlockSpec((tf, d), lambda i, j: (j, 0)),
            pl.BlockSpec((1, d), lambda i, j: (0, 0)),
        ] + rider_specs,
        out_specs=([pl.BlockSpec((tm, d), lambda i, j: (i, 0))] + rider_specs
                   + (weight_specs if emit_weights else [])),
        scratch_shapes=[pltpu.VMEM((tm, d), BF16), pltpu.VMEM((tm, d), F32)],
        compiler_params=_cparams("arbitrary", "arbitrary"),
        name="ffn_final" if final_norm else "ffn",
    )(x, g, wg, wu, wo, gf, *riders)
    return out if (riders or emit_weights) else out[0]


def _inproj_kernel(h_ref, g_ref, wq_ref, w_ref, q_ref, kv_ref, xr_ref, qm_ref, u_ref):
    u = _rms(h_ref[...], g_ref[...]).astype(BF16)
    u_ref[...] = u
    c0, c1, c2 = SWA_Q_DIM, SWA_Q_DIM + 2 * SWA_KV_DIM, SWA_Q_DIM + 2 * SWA_KV_DIM + RWKV_IN
    q_ref[...] = _dot(u, wq_ref[...]).astype(BF16)
    kv_ref[...] = _dot(u, w_ref[:, c0:c1])
    xr_ref[...] = _dot(u, w_ref[:, c1:c2])
    qm_ref[...] = _dot(u, w_ref[:, c2:PROJ_DIM]).astype(BF16)


def _inproj(h, g, wq, w, *, tm=512):
    m, d = h.shape
    row = lambda i: (i, 0)
    return pl.pallas_call(
        _inproj_kernel,
        out_shape=(
            jax.ShapeDtypeStruct((m, SWA_Q_DIM), BF16),
            jax.ShapeDtypeStruct((m, 2 * SWA_KV_DIM), F32),
            jax.ShapeDtypeStruct((m, RWKV_IN), F32),
            jax.ShapeDtypeStruct((m, MEM_DIM), BF16),
            jax.ShapeDtypeStruct((m, d), BF16),
        ),
        grid=(m // tm,),
        in_specs=[
            pl.BlockSpec((tm, d), row),
            pl.BlockSpec((1, d), lambda i: (0, 0)),
            pl.BlockSpec((d, SWA_Q_DIM), lambda i: (0, 0), pipeline_mode=pl.Buffered(1)),
            pl.BlockSpec((d, PROJ_DIM), lambda i: (0, 0), pipeline_mode=pl.Buffered(1)),
        ],
        out_specs=(
            pl.BlockSpec((tm, SWA_Q_DIM), row),
            pl.BlockSpec((tm, 2 * SWA_KV_DIM), row),
            pl.BlockSpec((tm, RWKV_IN), row),
            pl.BlockSpec((tm, MEM_DIM), row),
            pl.BlockSpec((tm, d), row),
        ),
        compiler_params=_cparams("parallel"),
        name="inproj",
    )(h, g, wq, w)


def _norm_matmul_kernel(x_ref, g_ref, w_ref, o_ref):
    o_ref[...] = _dot(_rms(x_ref[...], g_ref[...]).astype(BF16), w_ref[...])


def _norm_matmul(x, g, w, *, tn=512):
    m, d = x.shape
    n = w.shape[1]
    return pl.pallas_call(
        _norm_matmul_kernel,
        out_shape=jax.ShapeDtypeStruct((m, n), F32),
        grid=(n // tn,),
        in_specs=[
            pl.BlockSpec((m, d), lambda j: (0, 0)),
            pl.BlockSpec((1, d), lambda j: (0, 0)),
            pl.BlockSpec((d, tn), lambda j: (0, j)),
        ],
        out_specs=pl.BlockSpec((m, tn), lambda j: (0, j)),
        compiler_params=_cparams("parallel"),
        name="norm_matmul",
    )(x, g, w)


def _swa_prompt_kernel(q_ref, kvc_ref, kvp_ref, bias_ref, sink_ref, o_ref, *, nq):
    i = pl.program_id(0)
    w = WINDOW
    rows = SWA_KV_HEADS * w
    kv_blocks = [kvp_ref[...]] + [kvc_ref[s * w:(s + 1) * w, :] for s in range(nq)]
    k_blocks = [x[:, 0:SWA_KV_DIM].astype(BF16) for x in kv_blocks]
    v_blocks = [x[:, SWA_KV_DIM:].astype(BF16) for x in kv_blocks]
    qpos = lax.broadcasted_iota(jnp.int32, (rows, 2 * w), 0) % w
    col = lax.broadcasted_iota(jnp.int32, (rows, 2 * w), 1)
    dist = qpos + w - col
    in_window = (dist >= 0) & (dist < w)
    first = in_window & ((col >= w) | (i > 0))
    lane_head = lax.broadcasted_iota(jnp.int32, (w, SWA_KV_DIM), 1) // SWA_HEAD_DIM
    scale = SWA_HEAD_DIM ** -0.5
    pairs = [(s, g) for s in range(nq) for g in range(SWA_GROUP)]
    k = [jnp.concatenate(k_blocks[s:s + 2], axis=0) for s in range(nq)]
    v = [jnp.concatenate(v_blocks[s:s + 2], axis=0) for s in range(nq)]
    logits = []
    for s, g in pairs:
        qg = q_ref[s * w:(s + 1) * w, g * SWA_KV_DIM:(g + 1) * SWA_KV_DIM].astype(F32) * scale
        qs = jnp.concatenate([jnp.where(lane_head == kvh, qg, 0.0) for kvh in range(SWA_KV_HEADS)], axis=0)
        lg = _dot_nt(qs.astype(BF16), k[s])
        logits.append(jnp.where(first if s == 0 else in_window, lg + bias_ref[g], NEG_INF))
    sink = [sink_ref[g] for _, g in pairs]
    m = [jnp.maximum(jnp.max(x, axis=-1, keepdims=True), sk) for x, sk in zip(logits, sink)]
    p = [jnp.exp(x - mi) for x, mi in zip(logits, m)]
    inv = [1.0 / (jnp.sum(pi, axis=-1, keepdims=True) + jnp.exp(sk - mi)) for pi, sk, mi in zip(p, sink, m)]
    ov = [_dot((pi * ii).astype(BF16), v[s]) for (s, _), pi, ii in zip(pairs, p, inv)]
    for (s, g), o in zip(pairs, ov):
        og = jnp.zeros((w, SWA_KV_DIM), F32)
        for kvh in range(SWA_KV_HEADS):
            og = jnp.where(lane_head == kvh, o[kvh * w:(kvh + 1) * w], og)
        o_ref[s * w:(s + 1) * w, g * SWA_KV_DIM:(g + 1) * SWA_KV_DIM] = og.astype(BF16)


def _swa_prompt(q, kv, bias, sink_rows, *, nq=8):
    t = q.shape[0]
    w = WINDOW
    rows = SWA_KV_HEADS * w
    return pl.pallas_call(
        functools.partial(_swa_prompt_kernel, nq=nq),
        out_shape=jax.ShapeDtypeStruct((t, SWA_Q_DIM), BF16),
        grid=(t // (nq * w),),
        in_specs=[
            pl.BlockSpec((nq * w, SWA_Q_DIM), lambda i: (i, 0)),
            pl.BlockSpec((nq * w, 2 * SWA_KV_DIM), lambda i: (i, 0)),
            pl.BlockSpec((w, 2 * SWA_KV_DIM), lambda i: (jnp.maximum(i * nq - 1, 0), 0)),
            pl.BlockSpec((SWA_GROUP, rows, 2 * w), lambda i: (0, 0, 0)),
            pl.BlockSpec((SWA_GROUP, rows, 1), lambda i: (0, 0, 0)),
        ],
        out_specs=pl.BlockSpec((nq * w, SWA_Q_DIM), lambda i: (i, 0)),
        compiler_params=_cparams("parallel"),
        name="swa_prompt",
    )(q, kv, kv, bias, sink_rows)


def _swa_sample_kernel(q_ref, kvn_ref, kb_ref, vb_ref, bias_b_ref, bias_n_ref, sink_ref, o_ref, ko_ref, vo_ref,
                       *, bb, t):
    gt = SWA_GROUP * t
    rows = SWA_KV_HEADS * gt
    w = kb_ref.shape[2]
    scale = SWA_HEAD_DIM ** -0.5
    lane_head = lax.broadcasted_iota(jnp.int32, (gt, SWA_KV_DIM), 1) // SWA_HEAD_DIM
    tok = lax.broadcasted_iota(jnp.int32, (rows, w), 0) % t
    keyj = lax.broadcasted_iota(jnp.int32, (rows, w), 1)
    valid_b = (tok + w - keyj) < WINDOW
    tok_n = lax.broadcasted_iota(jnp.int32, (rows, 1), 0) % t
    sink = sink_ref[...]
    bs = range(bb)
    toks = range(t)
    qall = [jnp.concatenate([jnp.where(lane_head == kvh, q_ref[b].astype(F32), 0.0) for kvh in range(SWA_KV_HEADS)],
                            axis=0) for b in bs]
    kvn = [kvn_ref[b] for b in bs]
    lb = [_dot(qall[b].astype(BF16), kb_ref[b].astype(BF16)) for b in bs]
    lb = [jnp.where(valid_b, lb[b] * scale + bias_b_ref[...], NEG_INF) for b in bs]
    ln = [[jnp.sum(qall[b] * kvn[b][j:j + 1, 0:SWA_KV_DIM], axis=-1, keepdims=True) for j in toks] for b in bs]
    ln = [[jnp.where(tok_n >= j, ln[b][j] * scale + bias_n_ref[:, j:j + 1], NEG_INF) for j in toks] for b in bs]
    m = [jnp.maximum(jnp.max(lb[b], axis=-1, keepdims=True), sink) for b in bs]
    m = [functools.reduce(jnp.maximum, ln[b], m[b]) for b in bs]
    pb = [jnp.exp(lb[b] - m[b]) for b in bs]
    pn = [[jnp.exp(ln[b][j] - m[b]) for j in toks] for b in bs]
    denom = [jnp.sum(pb[b], axis=-1, keepdims=True) + jnp.exp(sink - m[b]) for b in bs]
    inv = [1.0 / functools.reduce(jnp.add, pn[b], denom[b]) for b in bs]
    oall = [_dot_nt((pb[b] * inv[b]).astype(BF16), vb_ref[b].astype(BF16)) for b in bs]
    for b in bs:
        ob = oall[b]
        for j in toks:
            ob = ob + (pn[b][j] * inv[b]) * kvn[b][j:j + 1, SWA_KV_DIM:]
        og = jnp.zeros((gt, SWA_KV_DIM), F32)
        for kvh in range(SWA_KV_HEADS):
            og = jnp.where(lane_head == kvh, ob[kvh * gt:(kvh + 1) * gt], og)
        o_ref[b] = og.astype(BF16)
    pos = lax.broadcasted_iota(jnp.int32, (SWA_KV_DIM, w), 1)
    pad = jnp.zeros((w - SUBLANES, 2 * SWA_KV_DIM), F32)
    row8 = lax.broadcasted_iota(jnp.int32, (SUBLANES, 2 * SWA_KV_DIM), 0)
    for b in bs:
        last8 = jnp.zeros((SUBLANES, 2 * SWA_KV_DIM), F32)
        for j in toks:
            last8 = jnp.where(row8 == SUBLANES - t + j, kvn[b][j:j + 1], last8)
        tail_t = jnp.concatenate([pad, last8], axis=0).T
        ko_ref[b] = jnp.where(pos >= w - t, tail_t[:SWA_KV_DIM], pltpu.roll(kb_ref[b], w - t, axis=1))
        vo_ref[b] = jnp.where(pos >= w - t, tail_t[SWA_KV_DIM:], pltpu.roll(vb_ref[b], w - t, axis=1))


def _swa_sample(q, kvn, kbuf, vbuf, bias_b, bias_n, sink_rows, *, bb=8):
    b, gt, _ = q.shape
    t = kvn.shape[1]
    w = kbuf.shape[2]
    rows = SWA_KV_HEADS * gt
    blk = lambda i: (i, 0, 0)
    const = lambda i: (0, 0)
    cache = pl.BlockSpec((bb, SWA_KV_DIM, w), blk)
    return pl.pallas_call(
        functools.partial(_swa_sample_kernel, bb=bb, t=t),
        out_shape=(jax.ShapeDtypeStruct((b, gt, SWA_KV_DIM), BF16),
                   jax.ShapeDtypeStruct(kbuf.shape, F32), jax.ShapeDtypeStruct(vbuf.shape, F32)),
        grid=(b // bb,),
        in_specs=[
            pl.BlockSpec((bb, gt, SWA_KV_DIM), blk),
            pl.BlockSpec((bb, t, 2 * SWA_KV_DIM), blk),
            cache, cache,
            pl.BlockSpec((rows, w), const),
            pl.BlockSpec((rows, t), const),
            pl.BlockSpec((rows, 1), const),
        ],
        out_specs=(pl.BlockSpec((bb, gt, SWA_KV_DIM), blk), cache, cache),
        compiler_params=_cparams("parallel"),
        name="swa_sample",
    )(q, kvn, kbuf, vbuf, bias_b, bias_n, sink_rows)


def _mem_heads(q, mk, mv):
    scale = MEM_HEAD_DIM ** -0.5
    cols = [slice(h * MEM_HEAD_DIM, (h + 1) * MEM_HEAD_DIM) for h in range(MEM_HEADS)]
    x = [_dot_nt(q[:, sl], mk[:, sl]) * scale for sl in cols]
    m = [jnp.max(xi, axis=-1, keepdims=True) for xi in x]
    p = [jnp.exp(xi - mi) for xi, mi in zip(x, m)]
    inv = [1.0 / jnp.sum(pi, axis=-1, keepdims=True) for pi in p]
    return jnp.concatenate([_dot((pi * ii).astype(BF16), mv[:, sl]) for pi, ii, sl in zip(p, inv, cols)], axis=-1)


def _mem_prompt_kernel(q_ref, mk_ref, mv_ref, o_ref):
    o_ref[...] = _mem_heads(q_ref[...], mk_ref[...].astype(BF16), mv_ref[...].astype(BF16)).astype(BF16)


def _mem_prompt(q, mkv, *, tm=512):
    m = q.shape[0]
    return pl.pallas_call(
        _mem_prompt_kernel,
        out_shape=jax.ShapeDtypeStruct((m, MEM_DIM), BF16),
        grid=(m // tm,),
        in_specs=[
            pl.BlockSpec((tm, MEM_DIM), lambda i: (i, 0)),
            pl.BlockSpec((N_MEM, MEM_DIM), lambda i: (0, 0)),
            pl.BlockSpec((N_MEM, MEM_DIM), lambda i: (0, 1)),
        ],
        out_specs=pl.BlockSpec((tm, MEM_DIM), lambda i: (i, 0)),
        compiler_params=_cparams("parallel"),
        name="mem_prompt",
    )(q, mkv, mkv)


def _mem_sample_kernel(q_ref, mk_ref, mv_ref, o_ref, *, bb):
    scale = MEM_HEAD_DIM ** -0.5
    pairs = [(b, h) for b in range(bb) for h in range(MEM_HEADS)]
    rows = lambda h: pl.ds(h, N_MEM, stride=MEM_HEADS)
    cols = lambda h: slice(h * MEM_HEAD_DIM, (h + 1) * MEM_HEAD_DIM)
    q = [q_ref[b] for b in range(bb)]
    x = [_dot_nt(q[b][:, cols(h)], mk_ref[b, rows(h), :].astype(BF16)) * scale for b, h in pairs]
    m = [jnp.max(xi, axis=-1, keepdims=True) for xi in x]
    p = [jnp.exp(xi - mi) for xi, mi in zip(x, m)]
    inv = [1.0 / jnp.sum(pi, axis=-1, keepdims=True) for pi in p]
    o = [_dot((pi * ii).astype(BF16), mv_ref[b, rows(h), :].astype(BF16)) for (b, h), pi, ii in zip(pairs, p, inv)]
    for b in range(bb):
        o_ref[b] = jnp.concatenate(o[b * MEM_HEADS:(b + 1) * MEM_HEADS], axis=-1).astype(BF16)


def _mem_sample(q, mk, mv, *, bb=8):
    b, t, _ = q.shape
    blk = lambda i: (i, 0, 0)
    return pl.pallas_call(
        functools.partial(_mem_sample_kernel, bb=bb),
        out_shape=jax.ShapeDtypeStruct((b, t, MEM_DIM), BF16),
        grid=(b // bb,),
        in_specs=[
            pl.BlockSpec((bb, t, MEM_DIM), blk),
            pl.BlockSpec((bb, N_MEM * MEM_HEADS, MEM_HEAD_DIM), blk),
            pl.BlockSpec((bb, N_MEM * MEM_HEADS, MEM_HEAD_DIM), blk),
        ],
        out_specs=pl.BlockSpec((bb, t, MEM_DIM), blk),
        compiler_params=_cparams("parallel"),
        name="mem_sample",
    )(q, mk, mv)


def _head_sum(x, ones_ref):
    hi = x.astype(BF16)
    lo = (x - hi.astype(F32)).astype(BF16)
    ones = ones_ref[...]
    w = ones.shape[0]
    return jnp.concatenate([_dot(hi[:, c:c + w], ones) + _dot(lo[:, c:c + w], ones) for c in range(0, x.shape[1], w)],
                           axis=1)


def _token_shift(x_ref, pre_ref, start_ref, mu_ref, *, seq, tm, tile):
    x = x_ref[...]
    row = lax.broadcasted_iota(jnp.int32, x.shape, 0)
    shifted = pltpu.roll(x, 1, axis=0)
    if seq >= tm:
        is_start = (tile * tm) % seq == 0
        first_prev = jnp.where(is_start, start_ref[0], pre_ref[SUBLANES - 1:SUBLANES, :])
        prev = jnp.where(row == 0, first_prev, shifted)
    else:
        prev = jnp.where(row % seq == 0, start_ref[...], shifted)
    return x + mu_ref[...] * (prev - x)


def _rwkv_features(xs, w0_ref, ww2_ref, a0_ref, aw2_ref, gw2_ref, kk_ref, ka_ref, rk_ref, ones_ref):
    d = RWKV_DIM
    r = xs[:, 0:d]
    k = xs[:, d:2 * d]
    v = xs[:, 2 * d:3 * d]
    lw = xs[:, 3 * d:3 * d + LORA_W]
    la = xs[:, 3 * d + LORA_W:3 * d + LORA_W + LORA_A]
    lg = xs[:, 3 * d + LORA_W + LORA_A:]
    wpre = w0_ref[...] + _dot_hi(jnp.tanh(lw), ww2_ref[...])
    w = -jax.nn.softplus(-wpre) - 0.5
    log_decay = -jnp.exp(w)
    a = jax.nn.sigmoid(a0_ref[...] + _dot_hi(la, aw2_ref[...]))
    g = _dot_hi(jax.nn.sigmoid(lg), gw2_ref[...])
    kk = k * kk_ref[...]
    kk = kk / jnp.maximum(jnp.sqrt(_head_sum(kk * kk, ones_ref)), 1e-12)
    kh = k * (1.0 + (a - 1.0) * ka_ref[...])
    bonus = _head_sum(r * kh * rk_ref[...], ones_ref) * v
    return r, log_decay, kh, v, kk, kk * a, g, bonus


def _rwkv_prep_kernel(x_ref, pre_ref, start_ref, mu_ref, w0_ref, ww2_ref, a0_ref, aw2_ref, gw2_ref, kk_ref, ka_ref,
                      rk_ref, ones_ref, *outs, seq, tm):
    xs = _token_shift(x_ref, pre_ref, start_ref, mu_ref, seq=seq, tm=tm, tile=pl.program_id(0))
    feats = _rwkv_features(xs, w0_ref, ww2_ref, a0_ref, aw2_ref, gw2_ref, kk_ref, ka_ref, rk_ref, ones_ref)
    for o_ref, val in zip(outs, feats):
        o_ref[...] = val


def _rwkv_prep(x, shift0, p, ones, *, seq, tm=256):
    m = x.shape[0]
    row = lambda i: (i, 0)
    const = lambda i: (0, 0)
    vec = lambda n: pl.BlockSpec((1, n), const)
    out = jax.ShapeDtypeStruct((m, RWKV_DIM), F32)
    if seq >= tm:
        assert seq % tm == 0
        start = shift0
        start_spec = pl.BlockSpec((1, 1, RWKV_IN), lambda i: ((i * tm) // seq, 0, 0))
    else:
        assert tm % seq == 0
        start = jnp.repeat(shift0[:, 0], seq, axis=0)
        start_spec = pl.BlockSpec((tm, RWKV_IN), row)
    pre_blocks = tm // SUBLANES
    return pl.pallas_call(
        functools.partial(_rwkv_prep_kernel, seq=seq, tm=tm),
        out_shape=(out,) * 8,
        grid=(m // tm,),
        in_specs=[
            pl.BlockSpec((tm, RWKV_IN), row),
            pl.BlockSpec((SUBLANES, RWKV_IN), lambda i: (jnp.maximum(i * pre_blocks - 1, 0), 0)),
            start_spec,
            vec(RWKV_IN), vec(RWKV_DIM),
            pl.BlockSpec((LORA_W, RWKV_DIM), const),
            vec(RWKV_DIM),
            pl.BlockSpec((LORA_A, RWKV_DIM), const),
            pl.BlockSpec((LORA_G, RWKV_DIM), const),
            vec(RWKV_DIM), vec(RWKV_DIM), vec(RWKV_DIM),
            pl.BlockSpec(ones.shape, const),
        ],
        out_specs=(pl.BlockSpec((tm, RWKV_DIM), row),) * 8,
        compiler_params=_cparams("parallel"),
        name="rwkv_prep",
    )(x, x, start, p["mu"], p["w0"], p["w_w2"], p["a0"], p["a_w2"], p["g_w2"], p["k_k"], p["k_a"], p["r_k"], ones)


def _rwkv_lanes_kernel(r_ref, lw_ref, k_ref, v_ref, kk_ref, kka_ref, s_ref, y_ref, so_ref, v_scr, y_scr, *, t):
    n = RWKV_HEAD_DIM
    nb = s_ref.shape[-1]
    heads = range(2)

    def token_major(ref, j):
        return ref[pl.ds(j, nb, stride=t), :].T

    for j in range(t):
        v_scr[...] = token_major(v_ref, j)
        r_t, k_t, kka_t = (token_major(ref, j) for ref in (r_ref, k_ref, kka_ref))
        neg_kk_t = -token_major(kk_ref, j)
        w_t = jnp.exp(token_major(lw_ref, j))
        src = s_ref if j == 0 else so_ref

        def value_group(g, carry):
            rows = pl.multiple_of(g * SUBLANES, SUBLANES)
            ys = [[] for _ in heads]
            vg = [v_scr[pl.ds(h * n + rows, SUBLANES), :] for h in heads]
            for i in range(SUBLANES):
                for h in heads:
                    f = slice(h * n, (h + 1) * n)
                    s = src[h, rows + i]
                    sa = jnp.sum(s * neg_kk_t[f], axis=0, keepdims=True)
                    s = s * w_t[f] + sa * kka_t[f] + vg[h][i:i + 1] * k_t[f]
                    so_ref[h, rows + i] = s
                    ys[h].append(jnp.sum(s * r_t[f], axis=0, keepdims=True))
            for h in heads:
                y_scr[pl.ds(h * n + rows, SUBLANES), :] = jnp.concatenate(ys[h], axis=0)
            return carry

        lax.fori_loop(0, n // SUBLANES, value_group, 0)
        y_ref[j] = y_scr[...].T


def _rwkv_lanes(r, lw, k, v, kk, kka, s0, *, t):
    m, d = r.shape
    nb = m // t
    n = RWKV_HEAD_DIM
    assert nb == LANES, "one batch per lane"
    tok = pl.BlockSpec((m, LANES), lambda p: (0, p))
    st = pl.BlockSpec((2, n, n, nb), lambda p: (p, 0, 0, 0))
    return pl.pallas_call(
        functools.partial(_rwkv_lanes_kernel, t=t),
        out_shape=(jax.ShapeDtypeStruct((t, nb, d), F32), jax.ShapeDtypeStruct((RWKV_HEADS, n, n, nb), F32)),
        grid=(RWKV_HEADS // 2,),
        in_specs=[tok] * 6 + [st],
        out_specs=(pl.BlockSpec((t, nb, LANES), lambda p: (0, 0, p)), st),
        scratch_shapes=[pltpu.VMEM((LANES, nb), F32), pltpu.VMEM((LANES, nb), F32)],
        compiler_params=_cparams("parallel"),
        name="rwkv_lanes",
    )(r, lw, k, v, kk, kka, s0)


CHUNK = 64
GROUP_HEADS = 4
GROUP_W = GROUP_HEADS * RWKV_HEAD_DIM
N_GROUPS = RWKV_HEADS // GROUP_HEADS
(MASK_SAME, MASK_STRICT, MASK_INCL, MASK_LEVEL0) = (0, 1, 2, 3)
N_LEVELS = int(math.log2(CHUNK))


def _chunk_masks():
    i = np.arange(GROUP_W)
    same = (i[:, None] // CHUNK) == (i[None, :] // CHUNK)
    masks = [same, same & (i[None, :] < i[:, None]), same & (i[None, :] <= i[:, None])]
    for lvl in range(N_LEVELS):
        m = 1 << lvl
        masks.append(((i[:, None] // (2 * m)) == (i[None, :] // (2 * m))) & ((i[:, None] // m) != (i[None, :] // m))
                     & (i[None, :] < i[:, None]))
    return np.stack(masks).astype(np.float32)


def _rwkv_prompt_kernel(x_ref, pre_ref, start_ref, mu_ref, w0_ref, ww2_ref, a0_ref, aw2_ref, gw2_ref, kk_p_ref,
                        ka_ref, rk_ref, ones_ref, lnw_ref, lnb_ref, st0_ref, tri_ref, eye_ref, mask_ref,
                        *rest, seq, tm, n_tiles, n_riders):
    rider_in, (o_ref, sto_ref), rider_out = rest[:n_riders], rest[n_riders:n_riders + 2], rest[n_riders + 2:-11]
    st_scr, xs_ref, y_ref, r_set, lw_set, k_set, v_set, kk_set, kka_set, g_set, bonus_set = rest[-11:]
    for src, dst in zip(rider_in, rider_out):
        dst[...] = src[...].astype(BF16)
    step = pl.program_id(0)
    cur = (step + 1) % 2
    nxt = step % 2
    sets = (r_set, lw_set, k_set, v_set, kk_set, kka_set, g_set, bonus_set)
    r_ref, lw_ref, k_ref, v_ref, kk_ref, kka_ref = (s.at[cur] for s in sets[:6])

    @pl.when(step == 0)
    def _():
        st_scr[...] = st0_ref[...]
        for s in sets:
            s[1] = jnp.zeros(s.shape[1:], F32)

    xs_ref[...] = _token_shift(x_ref, pre_ref, start_ref, mu_ref, seq=seq, tm=tm, tile=jnp.minimum(step, n_tiles - 1))
    n_chunks = tm // CHUNK

    piece = 2 * CHUNK
    def features(c):
        rows = slice(c * piece, (c + 1) * piece)
        feats = _rwkv_features(xs_ref[rows, :], w0_ref, ww2_ref, a0_ref, aw2_ref, gw2_ref, kk_p_ref, ka_ref, rk_ref,
                               ones_ref)
        for s, val in zip(sets, feats):
            s[nxt, rows, :] = val

    eye = eye_ref[...]
    tri = tri_ref[...]
    tile_rows = lambda x: jnp.concatenate([x] * GROUP_HEADS, axis=0)
    block_diag = lambda x: (tile_rows(x) * mask_ref[MASK_SAME]).astype(BF16)

    chains = [(slice(c * CHUNK, (c + 1) * CHUNK), slice(g * GROUP_W, (g + 1) * GROUP_W))
              for c in range(n_chunks) for g in range(N_GROUPS)]
    each = lambda f, *cols: [f(*args) for args in zip(*cols)]
    same, strict, incl = mask_ref[MASK_SAME], mask_ref[MASK_STRICT], mask_ref[MASK_INCL]

    def cum_decay(lw):
        h1 = lw.astype(BF16)
        r1 = lw - h1.astype(F32)
        h2 = r1.astype(BF16)
        h3 = (r1 - h2.astype(F32)).astype(BF16)
        return _dot(tri, h1) + _dot(tri, h2) + _dot(tri, h3)

    lw = [lw_ref[rows, sl] for rows, sl in chains]
    kka = [kka_ref[rows, sl] for rows, sl in chains]
    k = [k_ref[rows, sl] for rows, sl in chains]
    cum = each(cum_decay, lw)
    cum_last = each(lambda c: c[CHUNK - 1:CHUNK, :], cum)
    p_inv = each(lambda c: jnp.exp(-c), cum)
    p_tail = each(lambda c, cl: jnp.exp(cl - c), cum, cum_last)
    a_bd = [block_diag(-kk_ref[rows, sl] * jnp.exp(c - l)) for (rows, sl), c, l in zip(chains, cum, lw)]
    r_f = [tile_rows(r_ref[rows, sl] * jnp.exp(c)) * same for (rows, sl), c in zip(chains, cum)]
    r_bd = each(lambda x: x.astype(BF16), r_f)
    v_bd = [block_diag(v_ref[rows, sl]) for rows, sl in chains]
    b_rep = each(lambda x, p: tile_rows((x * p).astype(BF16)), kka, p_inv)
    k_rep = each(lambda x, p: tile_rows((x * p).astype(BF16)), k, p_inv)
    bh_rep = each(lambda x, p: tile_rows(x * p), kka, p_tail)
    kh_rep = each(lambda x, p: tile_rows(x * p), k, p_tail)

    l_ab_f = each(lambda a, b: _dot_nt(a, b) * strict, a_bd, b_rep)
    l_ab = each(lambda x: x.astype(BF16), l_ab_f)
    l_ak = each(lambda a, b: (_dot_nt(a, b) * strict).astype(BF16), a_bd, k_rep)
    m_rb = each(lambda a, b: (_dot_nt(a, b) * incl).astype(BF16), r_bd, b_rep)
    m_rk = each(lambda a, b: (_dot_nt(a, b) * incl).astype(BF16), r_bd, k_rep)
    bh_t = each(lambda x: (x.T * same).astype(BF16), bh_rep)
    kh_t = each(lambda x: (x.T * same).astype(BF16), kh_rep)

    pending = list(range(tm // piece))

    def next_features():
        if pending:
            features(pending.pop(0))

    d = each(lambda l: eye + l * mask_ref[MASK_LEVEL0], l_ab_f)
    for lvl in range(1, N_LEVELS):
        d_b = each(lambda x: x.astype(BF16), d)
        x = each(lambda l, db: (_dot(l, db) * mask_ref[MASK_LEVEL0 + lvl]).astype(BF16), l_ab, d_b)
        next_features()
        d = each(lambda dd, db, xx: dd + _dot(db, xx), d, d_b, x)
    t_b = each(lambda x: x.astype(BF16), d)
    while pending:
        next_features()

    wm = each(lambda a, b, vv: _dot(jnp.concatenate([a, b], axis=0), vv), l_ak, m_rk, v_bd)
    twa = each(lambda t, w, a: _dot(t, jnp.concatenate([w[:GROUP_W].astype(BF16), a], axis=1)).astype(BF16),
               t_b, wm, a_bd)
    ry = each(_dot, m_rb, twa)
    mn = each(_dot, bh_t, twa)
    khv = each(_dot, kh_t, v_bd)
    y0 = each(lambda a, w: a[:, :GROUP_W] + w[GROUP_W:], ry, wm)
    n_x = each(lambda a, b: a[:, :GROUP_W] + b, mn, khv)
    mr = each(lambda a, cl, rf, b: jnp.concatenate(
        [(eye * jnp.exp(cl) + a[:, GROUP_W:]).astype(BF16), (rf + b[:, GROUP_W:]).astype(BF16)], axis=0),
        mn, cum_last, r_f, ry)

    st = [st_scr[g] for g in range(N_GROUPS)]
    for i, (rows, sl) in enumerate(chains):
        g = i % N_GROUPS
        ys = _dot(mr[i], st[g].astype(BF16))
        st[g] = ys[:GROUP_W] + n_x[i]
        y_bd = ys[GROUP_W:] + y0[i]
        y = y_bd[0:CHUNK]
        for h in range(1, GROUP_HEADS):
            y = y + y_bd[h * CHUNK:(h + 1) * CHUNK]
        y_ref[rows, sl] = y
    for g in range(N_GROUPS):
        st_scr[g] = st[g]
    o_ref[...] = _rwkv_output(y_ref[...], bonus_set[cur], g_set[cur], lnw_ref, lnb_ref, ones_ref)

    @pl.when(step == pl.num_programs(0) - 1)
    def _():
        sto_ref[...] = st_scr[...]


def _rwkv_prompt(x, shift0, s0, p, ones, *, riders=(), chunks_per_step=4):
    t = x.shape[0]
    d = RWKV_DIM
    n = RWKV_HEAD_DIM
    tt = CHUNK * chunks_per_step
    assert CHUNK == n and t % tt == 0
    st0 = jnp.einsum("ghvk,hj->ghkjv", s0.reshape(N_GROUPS, GROUP_HEADS, n, n), jnp.eye(GROUP_HEADS, dtype=F32))
    st0 = st0.reshape(N_GROUPS, GROUP_W, GROUP_W)
    tri = jnp.asarray(np.tril(np.ones((CHUNK, CHUNK), np.float32)), BF16)
    eye = jnp.eye(GROUP_W, dtype=F32)
    masks = jnp.asarray(_chunk_masks())
    const = lambda c: (0, 0)
    vec = lambda width: pl.BlockSpec((1, width), const)
    st_spec = pl.BlockSpec((N_GROUPS, GROUP_W, GROUP_W), lambda c: (0, 0, 0))
    feature_set = pltpu.VMEM((2, tt, d), F32)
    pre_blocks = tt // SUBLANES
    n_tiles = t // tt
    fill = lambda c: jnp.minimum(c, n_tiles - 1)
    rider_specs = _rider_specs(riders, n_tiles + 1, lambda c: c)
    o, st, *rounded = pl.pallas_call(
        functools.partial(_rwkv_prompt_kernel, seq=t, tm=tt, n_tiles=n_tiles, n_riders=len(riders)),
        out_shape=[jax.ShapeDtypeStruct((t, d), BF16), jax.ShapeDtypeStruct((N_GROUPS, GROUP_W, GROUP_W), F32)]
        + [jax.ShapeDtypeStruct(arr.shape, BF16) for arr in riders],
        grid=(n_tiles + 1,),
        in_specs=[
            pl.BlockSpec((tt, RWKV_IN), lambda c: (fill(c), 0)),
            pl.BlockSpec((SUBLANES, RWKV_IN), lambda c: (jnp.maximum(fill(c) * pre_blocks - 1, 0), 0)),
            pl.BlockSpec((1, 1, RWKV_IN), lambda c: (0, 0, 0)),
            vec(RWKV_IN), vec(d),
            pl.BlockSpec((LORA_W, d), const),
            vec(d),
            pl.BlockSpec((LORA_A, d), const),
            pl.BlockSpec((LORA_G, d), const),
            vec(d), vec(d), vec(d),
            pl.BlockSpec(ones.shape, const),
            vec(d), vec(d),
            st_spec,
            pl.BlockSpec((CHUNK, CHUNK), const),
            pl.BlockSpec((GROUP_W, GROUP_W), const),
            pl.BlockSpec(masks.shape, lambda c: (0, 0, 0)),
        ] + rider_specs,
        out_specs=[pl.BlockSpec((tt, d), lambda c: (jnp.maximum(c - 1, 0), 0)), st_spec] + rider_specs,
        scratch_shapes=[pltpu.VMEM((N_GROUPS, GROUP_W, GROUP_W), F32), pltpu.VMEM((tt, RWKV_IN), F32),
                        pltpu.VMEM((tt, d), F32)] + [feature_set] * 8,
        compiler_params=_cparams("arbitrary"),
        name="rwkv_prompt",
    )(x, x, shift0, p["mu"], p["w0"], p["w_w2"], p["a0"], p["a_w2"], p["g_w2"], p["k_k"], p["k_a"], p["r_k"], ones,
      p["ln_w"], p["ln_b"], st0, tri, eye, masks, *riders)
    st5 = st.reshape(N_GROUPS, GROUP_HEADS, n, GROUP_HEADS, n)
    s_new = jnp.einsum("ghkjv,hj->ghvk", st5, jnp.eye(GROUP_HEADS, dtype=F32)).reshape(RWKV_HEADS, n, n)
    return o, s_new, rounded


def _rwkv_output(y, bonus, gate, lnw_ref, lnb_ref, ones_ref):
    inv_n = 1.0 / RWKV_HEAD_DIM
    mu = _head_sum(y, ones_ref) * inv_n
    dlt = y - mu
    var = _head_sum(dlt * dlt, ones_ref) * inv_n
    yn = dlt * lax.rsqrt(var + GN_EPS) * lnw_ref[...] + lnb_ref[...]
    return ((yn + bonus) * gate).astype(BF16)


def _rwkv_post_kernel(y_ref, bonus_ref, g_ref, lnw_ref, lnb_ref, ones_ref, o_ref):
    o_ref[...] = _rwkv_output(y_ref[...], bonus_ref[...], g_ref[...], lnw_ref, lnb_ref, ones_ref)


def _rwkv_post(y, bonus, g, lnw, lnb, ones, *, tm=256):
    m = y.shape[0]
    row = lambda i: (i, 0)
    const = lambda i: (0, 0)
    tile = pl.BlockSpec((tm, RWKV_DIM), row)
    return pl.pallas_call(
        _rwkv_post_kernel,
        out_shape=jax.ShapeDtypeStruct((m, RWKV_DIM), BF16),
        grid=(m // tm,),
        in_specs=[tile, tile, tile, pl.BlockSpec((1, RWKV_DIM), const), pl.BlockSpec((1, RWKV_DIM), const),
                  pl.BlockSpec(ones.shape, const)],
        out_specs=tile,
        compiler_params=_cparams("parallel"),
        name="rwkv_post",
    )(y, bonus, g, lnw, lnb, ones)


GATE_BLOCK = math.gcd(PROJ_DIM, D_MODEL)


def _merge_kernel(h_ref, u_ref, os_ref, or_ref, om_ref, wos_ref, wor_ref, wom_ref, wout_ref, *rest, parts):
    gate_refs, (o_ref, acc_ref) = rest[:N_BRANCH * parts], rest[N_BRANCH * parts:]
    j = pl.program_id(1)

    @pl.when(j == 0)
    def _():
        acc_ref[...] = jnp.zeros_like(acc_ref)

    u = u_ref[...]
    merged = None
    for b, (x_ref, w_ref) in enumerate(((os_ref, wos_ref), (or_ref, wor_ref), (om_ref, wom_ref))):
        gate = jnp.concatenate([_dot(u, gate_refs[b * parts + c][...]) for c in range(parts)], axis=1)
        term = jax.nn.sigmoid(gate) * _dot(x_ref[...], w_ref[...])
        merged = term if merged is None else merged + term
    acc_ref[...] += _dot(merged.astype(BF16), wout_ref[...])

    @pl.when(j == pl.num_programs(1) - 1)
    def _():
        o_ref[...] = h_ref[...] + acc_ref[...]


def _merge(h, u, o_swa, o_rw, o_mem, w_in, wo_swa, wo_rw, wo_mem, w_out, *, tm=512, tn=512):
    m, d = h.shape
    nt = d // tn
    parts = tn // GATE_BLOCK
    g0 = PROJ_DIM // GATE_BLOCK
    row = lambda i, j: (i, 0)
    col = lambda i, j: (0, j)
    gate_specs = [pl.BlockSpec((d, GATE_BLOCK), functools.partial(
        lambda i, j, off: (0, off + j * parts), off=g0 + b * (d // GATE_BLOCK) + c))
        for b in range(N_BRANCH) for c in range(parts)]
    return pl.pallas_call(
        functools.partial(_merge_kernel, parts=parts),
        out_shape=jax.ShapeDtypeStruct((m, d), F32),
        grid=(m // tm, nt),
        in_specs=[
            pl.BlockSpec((tm, d), row),
            pl.BlockSpec((tm, d), row),
            pl.BlockSpec((tm, SWA_Q_DIM), row),
            pl.BlockSpec((tm, RWKV_DIM), row),
            pl.BlockSpec((tm, MEM_DIM), row),
            pl.BlockSpec((SWA_Q_DIM, tn), col),
            pl.BlockSpec((RWKV_DIM, tn), col),
            pl.BlockSpec((MEM_DIM, tn), col),
            pl.BlockSpec((tn, d), lambda i, j: (j, 0)),
        ] + gate_specs,
        out_specs=pl.BlockSpec((tm, d), row),
        scratch_shapes=[pltpu.VMEM((tm, d), F32)],
        compiler_params=_cparams("parallel", "arbitrary"),
        name="merge",
    )(h, u, o_swa, o_rw, o_mem, wo_swa, wo_rw, wo_mem, w_out, *([w_in] * (N_BRANCH * parts)))


def _t5_bucket(dist):
    max_exact = N_BUCKETS // 2
    d = np.maximum(dist, 0)
    log_ratio = (np.log(np.maximum(d, 1).astype(np.float32) / np.float32(max_exact))
                 / np.float32(math.log(MAX_DISTANCE / max_exact)))
    large = np.minimum(max_exact + (log_ratio * (N_BUCKETS - max_exact)).astype(np.int32), N_BUCKETS - 1)
    return np.where(d < max_exact, d, large).astype(np.int32)


def _rel_bias(table, dist):
    onehot = np.eye(N_BUCKETS, dtype=np.float32)[_t5_bucket(dist).reshape(-1)]
    bias = jnp.einsum("nb,bh->hn", jnp.asarray(onehot), table, precision=lax.Precision.HIGHEST)
    return bias.reshape(SWA_HEADS, *dist.shape)


def _rwkv_branch(xr, shift0, s0, p, ones, riders=()):
    b, t, _ = xr.shape
    flat = lambda z: z.reshape(b * t, z.shape[-1])
    if b == 1:
        o, s_new, rounded = _rwkv_prompt(flat(xr), shift0, s0[0], p, ones, riders=riders)
        return o, s_new[None], rounded
    assert b == LANES and not riders, "short sequences are batched one per lane"
    r, w, k, v, kk, kka, g, bonus = _rwkv_prep(flat(xr), shift0, p, ones, seq=t)
    y, s_new = _rwkv_lanes(r, w, k, v, kk, kka, jnp.transpose(s0, (1, 2, 3, 0)), t=t)
    o = _rwkv_post(flat(jnp.transpose(y, (1, 0, 2))), bonus, g, p["ln_w"], p["ln_b"], ones)
    return o, jnp.transpose(s_new, (3, 0, 1, 2))


def kernel(x_prompt, mem_prompt, x_sample, cache_swa_k, cache_swa_v, state_rwkv, state_rwkv_shift, cache_mem_k, cache_mem_v, ffn1_norm, ffn1_wi, ffn1_wo, mix_norm, w_in, swa_sinks, rel_bias_table, rwkv_mu, rwkv_w0, rwkv_w_w2, rwkv_a0, rwkv_a_w2, rwkv_g_w2, rwkv_k_k, rwkv_k_a, rwkv_r_k, rwkv_ln_w, rwkv_ln_b, mem_norm, w_mem_kv, w_o_swa, w_o_rwkv, w_o_mem, w_out, ffn2_norm, ffn2_wi, ffn2_wo, final_norm):
    assert ffn1_wi.shape[0] == 1, "single-layer trunk"
    bp, tp, d = x_prompt.shape
    bs, ts, _ = x_sample.shape
    assert bp == 1
    row = lambda z: z.reshape(1, -1).astype(F32)

    g1, gm, g2, gf = row(ffn1_norm[0]), row(mix_norm[0]), row(ffn2_norm[0]), row(final_norm)
    rp = {
        "mu": row(rwkv_mu[0]), "w0": row(rwkv_w0[0]), "w_w2": rwkv_w_w2[0], "a0": row(rwkv_a0[0]),
        "a_w2": rwkv_a_w2[0], "g_w2": rwkv_g_w2[0], "k_k": row(rwkv_k_k[0]), "k_a": row(rwkv_k_a[0]),
        "r_k": row(rwkv_r_k[0]), "ln_w": row(rwkv_ln_w[0]), "ln_b": row(rwkv_ln_b[0]),
    }
    seg = np.arange(GROUP_W) // RWKV_HEAD_DIM
    ones = jnp.asarray(seg[:, None] == seg[None, :], dtype=BF16)
    sinks = swa_sinks[0].astype(F32)
    table = rel_bias_table.astype(F32)

    xp = x_prompt.reshape(tp, d)
    xs = x_sample.reshape(bs * ts, d)
    hs, wg1, wu1, wo1 = _ffn(xs, g1, ffn1_wi[0], ffn1_wo[0], gf, final_norm=False, emit_weights=True, tf=256)
    hp, w_in_b, wo_swa, wo_rw, wo_mem, w_out_b = _ffn(
        xp, g1, (wg1, wu1), wo1, gf, final_norm=False,
        riders=(w_in[0], w_o_swa[0], w_o_rwkv[0], w_o_mem[0], w_out[0]))
    wo_swa = wo_swa.reshape(SWA_KV_HEADS, SWA_GROUP, SWA_HEAD_DIM, d).transpose(1, 0, 2, 3).reshape(SWA_Q_DIM, d)
    w_q = w_in_b[:, :SWA_Q_DIM].reshape(d, SWA_KV_HEADS, SWA_GROUP, SWA_HEAD_DIM).transpose(0, 2, 1, 3).reshape(d, SWA_Q_DIM)
    qp, kvp, xrp, qmp, up = _inproj(hp, gm, w_q, w_in_b)
    qs, kvs, xrs, qms, us = _inproj(hs, gm, w_q, w_in_b)

    w = WINDOW
    dist_p = np.arange(w)[:, None] + w - np.arange(2 * w)[None, :]
    bias_p = _rel_bias(table, dist_p).reshape(SWA_KV_HEADS, SWA_GROUP, w, 2 * w).transpose(1, 0, 2, 3)
    bias_p = bias_p.reshape(SWA_GROUP, SWA_KV_HEADS * w, 2 * w)
    sink_p = jnp.repeat(sinks.reshape(SWA_KV_HEADS, SWA_GROUP).T, w, axis=1).reshape(SWA_GROUP, SWA_KV_HEADS * w, 1)
    o_swa_p = _swa_prompt(qp, kvp, bias_p, sink_p)

    wbuf = cache_swa_k.shape[2]
    dist_s = np.arange(ts)[:, None] + wbuf - np.arange(wbuf + ts)[None, :]
    bias_s = _rel_bias(table, dist_s).reshape(SWA_HEADS * ts, wbuf + ts)
    sink_rows = jnp.repeat(sinks, ts).reshape(SWA_HEADS * ts, 1)
    qs_gt = qs.reshape(bs, ts, SWA_GROUP, SWA_KV_DIM).transpose(0, 2, 1, 3).reshape(bs, SWA_GROUP * ts, SWA_KV_DIM)
    kbuf = cache_swa_k[0].reshape(bs, wbuf, SWA_KV_DIM).transpose(0, 2, 1)
    vbuf = cache_swa_v[0].reshape(bs, wbuf, SWA_KV_DIM).transpose(0, 2, 1)
    o_swa_s, knew_t, vnew_t = _swa_sample(qs_gt, kvs.reshape(bs, ts, 2 * SWA_KV_DIM), kbuf, vbuf,
                                          bias_s[:, :wbuf], bias_s[:, wbuf:], sink_rows)
    o_swa_s = o_swa_s.reshape(bs, SWA_GROUP, ts, SWA_KV_DIM).transpose(0, 2, 1, 3).reshape(bs * ts, SWA_Q_DIM)

    zero_shift = jnp.zeros((bp, 1, RWKV_IN), F32)
    zero_state = jnp.zeros((bp, RWKV_HEADS, RWKV_HEAD_DIM, RWKV_HEAD_DIM), F32)
    o_rw_p, state_p, (wi2, wo2) = _rwkv_branch(xrp.reshape(bp, tp, RWKV_IN), zero_shift, zero_state, rp, ones,
                                               riders=(ffn2_wi[0], ffn2_wo[0]))
    o_rw_s, state_s = _rwkv_branch(xrs.reshape(bs, ts, RWKV_IN), state_rwkv_shift[0], state_rwkv[0], rp, ones)

    mkv = _norm_matmul(mem_prompt.reshape(N_MEM, d), row(mem_norm[0]), w_mem_kv[0].astype(BF16))
    o_mem_p = _mem_prompt(qmp, mkv)
    o_mem_s = _mem_sample(qms.reshape(bs, ts, MEM_DIM), cache_mem_k[0].reshape(bs, N_MEM * MEM_HEADS, MEM_HEAD_DIM),
                          cache_mem_v[0].reshape(bs, N_MEM * MEM_HEADS, MEM_HEAD_DIM)).reshape(bs * ts, MEM_DIM)

    hp = _merge(hp, up, o_swa_p, o_rw_p, o_mem_p, w_in_b, wo_swa, wo_rw, wo_mem, w_out_b)
    hs = _merge(hs, us, o_swa_s, o_rw_s, o_mem_s, w_in_b, wo_swa, wo_rw, wo_mem, w_out_b)
    y_prompt = _ffn(hp, g2, wi2, wo2, gf, final_norm=True).reshape(bp, tp, d)
    y_sample = _ffn(hs, g2, wi2, wo2, gf, final_norm=True).reshape(bs, ts, d)

    wp = min(w, tp)
    p_k = kvp[tp - wp:, :SWA_KV_DIM].reshape(1, bp, wp, SWA_KV_HEADS, SWA_HEAD_DIM)
    p_v = kvp[tp - wp:, SWA_KV_DIM:].reshape(1, bp, wp, SWA_KV_HEADS, SWA_HEAD_DIM)
    p_mk = mkv[:, :MEM_DIM].reshape(1, bp, N_MEM, MEM_HEADS, MEM_HEAD_DIM)
    p_mv = mkv[:, MEM_DIM:].reshape(1, bp, N_MEM, MEM_HEADS, MEM_HEAD_DIM)
    s_k = knew_t.transpose(0, 2, 1).reshape(1, bs, wbuf, SWA_KV_HEADS, SWA_HEAD_DIM)
    s_v = vnew_t.transpose(0, 2, 1).reshape(1, bs, wbuf, SWA_KV_HEADS, SWA_HEAD_DIM)
    return (y_prompt, y_sample,
            p_k, p_v, state_p[None], xrp[tp - 1:].reshape(1, bp, 1, RWKV_IN), p_mk, p_mv,
            s_k, s_v, state_s[None], xrs.reshape(bs, ts, RWKV_IN)[:, ts - 1:][None])
```

```python
import functools
import math

import jax
import jax.numpy as jnp
import numpy as np
from jax import lax
from jax.experimental import pallas as pl
from jax.experimental.pallas import tpu as pltpu

F32 = jnp.float32
BF16 = jnp.bfloat16

D_MODEL = 2048
SWA_HEADS = 16
SWA_KV_HEADS = 4
SWA_GROUP = SWA_HEADS // SWA_KV_HEADS
SWA_HEAD_DIM = 64
SWA_Q_DIM = SWA_HEADS * SWA_HEAD_DIM
SWA_KV_DIM = SWA_KV_HEADS * SWA_HEAD_DIM
WINDOW = 128
N_BUCKETS = 32
MAX_DISTANCE = 128
RWKV_HEADS = 8
RWKV_HEAD_DIM = 64
RWKV_DIM = RWKV_HEADS * RWKV_HEAD_DIM
LORA_W = 64
LORA_A = 64
LORA_G = 128
RWKV_IN = 3 * RWKV_DIM + LORA_W + LORA_A + LORA_G
N_MEM = 256
MEM_HEADS = 4
MEM_HEAD_DIM = 128
MEM_DIM = MEM_HEADS * MEM_HEAD_DIM
N_BRANCH = 3
PROJ_DIM = SWA_Q_DIM + 2 * SWA_KV_DIM + RWKV_IN + MEM_DIM
NORM_EPS = 1e-6
GN_EPS = 64e-5
NEG_INF = -1e30

LANES = 128
SUBLANES = 8
VMEM_LIMIT = 56 * 1024 * 1024


def _cparams(*sem):
    return pltpu.CompilerParams(dimension_semantics=sem, vmem_limit_bytes=VMEM_LIMIT)


def _rms(x, g):
    return x * lax.rsqrt(jnp.mean(x * x, axis=-1, keepdims=True) + NORM_EPS) * g


def _dot(a, b):
    return jnp.dot(a, b, preferred_element_type=F32)


def _dot_nt(a, b):
    return lax.dot_general(a, b, (((1,), (1,)), ((), ())), preferred_element_type=F32)


def _dot_hi(a, b):
    a_hi = a.astype(BF16)
    b_hi = b.astype(BF16)
    a_lo = (a - a_hi.astype(F32)).astype(BF16)
    b_lo = (b - b_hi.astype(F32)).astype(BF16)
    return _dot(a_hi, b_hi) + _dot(a_hi, b_lo) + _dot(a_lo, b_hi)


def _ffn_kernel(x_ref, g_ref, wg_ref, wu_ref, wo_ref, gf_ref, *rest, final_norm, n_riders, emit_weights):
    rider_in, (o_ref, *extra_out), (xn_ref, acc_ref) = rest[:n_riders], rest[n_riders:-2], rest[-2:]
    for src, dst in zip(rider_in, extra_out[:n_riders]):
        dst[...] = src[...].astype(BF16)
    j = pl.program_id(1)

    @pl.when(j == 0)
    def _():
        xn_ref[...] = _rms(x_ref[...], g_ref[...]).astype(BF16)
        acc_ref[...] = jnp.zeros_like(acc_ref)

    wg, wu, wo = wg_ref[...], wu_ref[...], wo_ref[...]
    if emit_weights:
        wg, wu, wo = wg.astype(BF16), wu.astype(BF16), wo.astype(BF16)
        for dst, val in zip(extra_out[n_riders:], (wg, wu, wo)):
            dst[...] = val
    xn = xn_ref[...]
    gate = _dot(xn, wg)
    up = _dot(xn, wu)
    act = (gate * jax.nn.sigmoid(gate)) * up
    acc_ref[...] += _dot(act.astype(BF16), wo)

    @pl.when(j == pl.num_programs(1) - 1)
    def _():
        h = x_ref[...] + 0.5 * acc_ref[...]
        if final_norm:
            h = _rms(h, gf_ref[...])
        o_ref[...] = h


def _rider_tiling(shape, steps):
    rows, cols = shape
    best, best_score = (1, 1), (0, 0)
    for nr in range(1, rows // 16 + 1):
        if rows % nr or (rows // nr) % 16:
            continue
        for nc in range(1, cols // LANES + 1):
            if cols % nc or (cols // nc) % LANES or nr * nc > steps:
                continue
            score = (min(cols // nc, 1024), nr * nc)
            if score > best_score:
                best, best_score = (nr, nc), score
    return best


def _rider_specs(riders, steps, step_of):
    specs = []
    for arr in riders:
        nr, nc = _rider_tiling(arr.shape, steps)
        specs.append(pl.BlockSpec((arr.shape[0] // nr, arr.shape[1] // nc), functools.partial(
            lambda *idx, nc, last: (jnp.minimum(step_of(*idx), last) // nc, jnp.minimum(step_of(*idx), last) % nc),
            nc=nc, last=nr * nc - 1)))
    return specs


def _ffn(x, g, wi, wo, gf, *, final_norm, riders=(), emit_weights=False, tm=512, tf=512):
    m, d = x.shape
    dff = wo.shape[0]
    nf = dff // tf
    steps = (m // tm) * nf
    wg, wu = wi if isinstance(wi, tuple) else (wi, wi)
    up_off = 0 if isinstance(wi, tuple) else nf
    assert not emit_weights or m == tm
    weight_specs = [pl.BlockSpec((d, tf), lambda i, j: (0, j)), pl.BlockSpec((d, tf), lambda i, j: (0, j)),
                    pl.BlockSpec((tf, d), lambda i, j: (j, 0))]
    weight_out = [jax.ShapeDtypeStruct((d, dff), BF16)] * 2 + [jax.ShapeDtypeStruct((dff, d), BF16)]
    rider_specs = _rider_specs(riders, steps, lambda i, j: i * nf + j)
    out = pl.pallas_call(
        functools.partial(_ffn_kernel, final_norm=final_norm, n_riders=len(riders), emit_weights=emit_weights),
        out_shape=([jax.ShapeDtypeStruct((m, d), F32)] + [jax.ShapeDtypeStruct(arr.shape, BF16) for arr in riders]
                   + (weight_out if emit_weights else [])),
        grid=(m // tm, nf),
        in_specs=[
            pl.BlockSpec((tm, d), lambda i, j: (i, 0)),
            pl.BlockSpec((1, d), lambda i, j: (0, 0)),
            pl.BlockSpec((d, tf), lambda i, j: (0, j)),
            pl.BlockSpec((d, tf), lambda i, j: (0, j + up_off)),
            pl.BlockSpec((tf, d), lambda i, j: (j, 0)),
            pl.BlockSpec((1, d), lambda i, j: (0, 0)),
        ] + rider_specs,
        out_specs=([pl.BlockSpec((tm, d), lambda i, j: (i, 0))] + rider_specs
                   + (weight_specs if emit_weights else [])),
        scratch_shapes=[pltpu.VMEM((tm, d), BF16), pltpu.VMEM((tm, d), F32)],
        compiler_params=_cparams("arbitrary", "arbitrary"),
        name="ffn_final" if final_norm else "ffn",
    )(x, g, wg, wu, wo, gf, *riders)
    return out if (riders or emit_weights) else out[0]


def _inproj_kernel(h_ref, g_ref, wq_ref, w_ref, q_ref, kv_ref, xr_ref, qm_ref, u_ref):
    u = _rms(h_ref[...], g_ref[...]).astype(BF16)
    u_ref[...] = u
    c0, c1, c2 = SWA_Q_DIM, SWA_Q_DIM + 2 * SWA_KV_DIM, SWA_Q_DIM + 2 * SWA_KV_DIM + RWKV_IN
    q_ref[...] = _dot(u, wq_ref[...]).astype(BF16)
    kv_ref[...] = _dot(u, w_ref[:, c0:c1])
    xr_ref[...] = _dot(u, w_ref[:, c1:c2])
    qm_ref[...] = _dot(u, w_ref[:, c2:PROJ_DIM]).astype(BF16)


def _inproj(h, g, wq, w, *, tm=512):
    m, d = h.shape
    row = lambda i: (i, 0)
    return pl.pallas_call(
        _inproj_kernel,
        out_shape=(
            jax.ShapeDtypeStruct((m, SWA_Q_DIM), BF16),
            jax.ShapeDtypeStruct((m, 2 * SWA_KV_DIM), F32),
            jax.ShapeDtypeStruct((m, RWKV_IN), F32),
            jax.ShapeDtypeStruct((m, MEM_DIM), BF16),
            jax.ShapeDtypeStruct((m, d), BF16),
        ),
        grid=(m // tm,),
        in_specs=[
            pl.BlockSpec((tm, d), row),
            pl.BlockSpec((1, d), lambda i: (0, 0)),
            pl.BlockSpec((d, SWA_Q_DIM), lambda i: (0, 0), pipeline_mode=pl.Buffered(1)),
            pl.BlockSpec((d, PROJ_DIM), lambda i: (0, 0), pipeline_mode=pl.Buffered(1)),
        ],
        out_specs=(
            pl.BlockSpec((tm, SWA_Q_DIM), row),
            pl.BlockSpec((tm, 2 * SWA_KV_DIM), row),
            pl.BlockSpec((tm, RWKV_IN), row),
            pl.BlockSpec((tm, MEM_DIM), row),
            pl.BlockSpec((tm, d), row),
        ),
        compiler_params=_cparams("parallel"),
        name="inproj",
    )(h, g, wq, w)


def _norm_matmul_kernel(x_ref, g_ref, w_ref, o_ref):
    o_ref[...] = _dot(_rms(x_ref[...], g_ref[...]).astype(BF16), w_ref[...])


def _norm_matmul(x, g, w, *, tn=512):
    m, d = x.shape
    n = w.shape[1]
    return pl.pallas_call(
        _norm_matmul_kernel,
        out_shape=jax.ShapeDtypeStruct((m, n), F32),
        grid=(n // tn,),
        in_specs=[
            pl.BlockSpec((m, d), lambda j: (0, 0)),
            pl.BlockSpec((1, d), lambda j: (0, 0)),
            pl.BlockSpec((d, tn), lambda j: (0, j)),
        ],
        out_specs=pl.BlockSpec((m, tn), lambda j: (0, j)),
        compiler_params=_cparams("parallel"),
        name="norm_matmul",
    )(x, g, w)


def _swa_prompt_kernel(q_ref, kvc_ref, kvp_ref, bias_ref, sink_ref, o_ref, *, nq):
    i = pl.program_id(0)
    w = WINDOW
    rows = SWA_KV_HEADS * w
    kv_blocks = [kvp_ref[...]] + [kvc_ref[s * w:(s + 1) * w, :] for s in range(nq)]
    k_blocks = [x[:, 0:SWA_KV_DIM].astype(BF16) for x in kv_blocks]
    v_blocks = [x[:, SWA_KV_DIM:].astype(BF16) for x in kv_blocks]
    qpos = lax.broadcasted_iota(jnp.int32, (rows, 2 * w), 0) % w
    col = lax.broadcasted_iota(jnp.int32, (rows, 2 * w), 1)
    dist = qpos + w - col
    in_window = (dist >= 0) & (dist < w)
    first = in_window & ((col >= w) | (i > 0))
    lane_head = lax.broadcasted_iota(jnp.int32, (w, SWA_KV_DIM), 1) // SWA_HEAD_DIM
    scale = SWA_HEAD_DIM ** -0.5
    pairs = [(s, g) for s in range(nq) for g in range(SWA_GROUP)]
    k = [jnp.concatenate(k_blocks[s:s + 2], axis=0) for s in range(nq)]
    v = [jnp.concatenate(v_blocks[s:s + 2], axis=0) for s in range(nq)]
    logits = []
    for s, g in pairs:
        qg = q_ref[s * w:(s + 1) * w, g * SWA_KV_DIM:(g + 1) * SWA_KV_DIM].astype(F32) * scale
        qs = jnp.concatenate([jnp.where(lane_head == kvh, qg, 0.0) for kvh in range(SWA_KV_HEADS)], axis=0)
        lg = _dot_nt(qs.astype(BF16), k[s])
        logits.append(jnp.where(first if s == 0 else in_window, lg + bias_ref[g], NEG_INF))
    sink = [sink_ref[g] for _, g in pairs]
    m = [jnp.maximum(jnp.max(x, axis=-1, keepdims=True), sk) for x, sk in zip(logits, sink)]
    p = [jnp.exp(x - mi) for x, mi in zip(logits, m)]
    inv = [1.0 / (jnp.sum(pi, axis=-1, keepdims=True) + jnp.exp(sk - mi)) for pi, sk, mi in zip(p, sink, m)]
    ov = [_dot((pi * ii).astype(BF16), v[s]) for (s, _), pi, ii in zip(pairs, p, inv)]
    for (s, g), o in zip(pairs, ov):
        og = jnp.zeros((w, SWA_KV_DIM), F32)
        for kvh in range(SWA_KV_HEADS):
            og = jnp.where(lane_head == kvh, o[kvh * w:(kvh + 1) * w], og)
        o_ref[s * w:(s + 1) * w, g * SWA_KV_DIM:(g + 1) * SWA_KV_DIM] = og.astype(BF16)


def _swa_prompt(q, kv, bias, sink_rows, *, nq=8):
    t = q.shape[0]
    w = WINDOW
    rows = SWA_KV_HEADS * w
    return pl.pallas_call(
        functools.partial(_swa_prompt_kernel, nq=nq),
        out_shape=jax.ShapeDtypeStruct((t, SWA_Q_DIM), BF16),
        grid=(t // (nq * w),),
        in_specs=[
            pl.BlockSpec((nq * w, SWA_Q_DIM), lambda i: (i, 0)),
            pl.BlockSpec((nq * w, 2 * SWA_KV_DIM), lambda i: (i, 0)),
            pl.BlockSpec((w, 2 * SWA_KV_DIM), lambda i: (jnp.maximum(i * nq - 1, 0), 0)),
            pl.BlockSpec((SWA_GROUP, rows, 2 * w), lambda i: (0, 0, 0)),
            pl.BlockSpec((SWA_GROUP, rows, 1), lambda i: (0, 0, 0)),
        ],
        out_specs=pl.BlockSpec((nq * w, SWA_Q_DIM), lambda i: (i, 0)),
        compiler_params=_cparams("parallel"),
        name="swa_prompt",
    )(q, kv, kv, bias, sink_rows)


def _swa_sample_kernel(q_ref, kvn_ref, kb_ref, vb_ref, bias_b_ref, bias_n_ref, sink_ref, o_ref, ko_ref, vo_ref,
                       *, bb, t):
    gt = SWA_GROUP * t
    rows = SWA_KV_HEADS * gt
    w = kb_ref.shape[2]
    scale = SWA_HEAD_DIM ** -0.5
    lane_head = lax.broadcasted_iota(jnp.int32, (gt, SWA_KV_DIM), 1) // SWA_HEAD_DIM
    tok = lax.broadcasted_iota(jnp.int32, (rows, w), 0) % t
    keyj = lax.broadcasted_iota(jnp.int32, (rows, w), 1)
    valid_b = (tok + w - keyj) < WINDOW
    tok_n = lax.broadcasted_iota(jnp.int32, (rows, 1), 0) % t
    sink = sink_ref[...]
    bs = range(bb)
    toks = range(t)
    qall = [jnp.concatenate([jnp.where(lane_head == kvh, q_ref[b].astype(F32), 0.0) for kvh in range(SWA_KV_HEADS)],
                            axis=0) for b in bs]
    kvn = [kvn_ref[b] for b in bs]
    lb = [_dot(qall[b].astype(BF16), kb_ref[b].astype(BF16)) for b in bs]
    lb = [jnp.where(valid_b, lb[b] * scale + bias_b_ref[...], NEG_INF) for b in bs]
    ln = [[jnp.sum(qall[b] * kvn[b][j:j + 1, 0:SWA_KV_DIM], axis=-1, keepdims=True) for j in toks] for b in bs]
    ln = [[jnp.where(tok_n >= j, ln[b][j] * scale + bias_n_ref[:, j:j + 1], NEG_INF) for j in toks] for b in bs]
    m = [jnp.maximum(jnp.max(lb[b], axis=-1, keepdims=True), sink) for b in bs]
    m = [functools.reduce(jnp.maximum, ln[b], m[b]) for b in bs]
    pb = [jnp.exp(lb[b] - m[b]) for b in bs]
    pn = [[jnp.exp(ln[b][j] - m[b]) for j in toks] for b in bs]
    denom = [jnp.sum(pb[b], axis=-1, keepdims=True) + jnp.exp(sink - m[b]) for b in bs]
    inv = [1.0 / functools.reduce(jnp.add, pn[b], denom[b]) for b in bs]
    oall = [_dot_nt((pb[b] * inv[b]).astype(BF16), vb_ref[b].astype(BF16)) for b in bs]
    for b in bs:
        ob = oall[b]
        for j in toks:
            ob = ob + (pn[b][j] * inv[b]) * kvn[b][j:j + 1, SWA_KV_DIM:]
        og = jnp.zeros((gt, SWA_KV_DIM), F32)
        for kvh in range(SWA_KV_HEADS):
            og = jnp.where(lane_head == kvh, ob[kvh * gt:(kvh + 1) * gt], og)
        o_ref[b] = og.astype(BF16)
    pos = lax.broadcasted_iota(jnp.int32, (SWA_KV_DIM, w), 1)
    pad = jnp.zeros((w - SUBLANES, 2 * SWA_KV_DIM), F32)
    row8 = lax.broadcasted_iota(jnp.int32, (SUBLANES, 2 * SWA_KV_DIM), 0)
    for b in bs:
        last8 = jnp.zeros((SUBLANES, 2 * SWA_KV_DIM), F32)
        for j in toks:
            last8 = jnp.where(row8 == SUBLANES - t + j, kvn[b][j:j + 1], last8)
        tail_t = jnp.concatenate([pad, last8], axis=0).T
        ko_ref[b] = jnp.where(pos >= w - t, tail_t[:SWA_KV_DIM], pltpu.roll(kb_ref[b], w - t, axis=1))
        vo_ref[b] = jnp.where(pos >= w - t, tail_t[SWA_KV_DIM:], pltpu.roll(vb_ref[b], w - t, axis=1))


def _swa_sample(q, kvn, kbuf, vbuf, bias_b, bias_n, sink_rows, *, bb=8):
    b, gt, _ = q.shape
    t = kvn.shape[1]
    w = kbuf.shape[2]
    rows = SWA_KV_HEADS * gt
    blk = lambda i: (i, 0, 0)
    const = lambda i: (0, 0)
    cache = pl.BlockSpec((bb, SWA_KV_DIM, w), blk)
    return pl.pallas_call(
        functools.partial(_swa_sample_kernel, bb=bb, t=t),
        out_shape=(jax.ShapeDtypeStruct((b, gt, SWA_KV_DIM), BF16),
                   jax.ShapeDtypeStruct(kbuf.shape, F32), jax.ShapeDtypeStruct(vbuf.shape, F32)),
        grid=(b // bb,),
        in_specs=[
            pl.BlockSpec((bb, gt, SWA_KV_DIM), blk),
            pl.BlockSpec((bb, t, 2 * SWA_KV_DIM), blk),
            cache, cache,
            pl.BlockSpec((rows, w), const),
            pl.BlockSpec((rows, t), const),
            pl.BlockSpec((rows, 1), const),
        ],
        out_specs=(pl.BlockSpec((bb, gt, SWA_KV_DIM), blk), cache, cache),
        compiler_params=_cparams("parallel"),
        name="swa_sample",
    )(q, kvn, kbuf, vbuf, bias_b, bias_n, sink_rows)


def _mem_heads(q, mk, mv):
    scale = MEM_HEAD_DIM ** -0.5
    cols = [slice(h * MEM_HEAD_DIM, (h + 1) * MEM_HEAD_DIM) for h in range(MEM_HEADS)]
    x = [_dot_nt(q[:, sl], mk[:, sl]) * scale for sl in cols]
    m = [jnp.max(xi, axis=-1, keepdims=True) for xi in x]
    p = [jnp.exp(xi - mi) for xi, mi in zip(x, m)]
    inv = [1.0 / jnp.sum(pi, axis=-1, keepdims=True) for pi in p]
    return jnp.concatenate([_dot((pi * ii).astype(BF16), mv[:, sl]) for pi, ii, sl in zip(p, inv, cols)], axis=-1)


def _mem_prompt_kernel(q_ref, mk_ref, mv_ref, o_ref):
    o_ref[...] = _mem_heads(q_ref[...], mk_ref[...].astype(BF16), mv_ref[...].astype(BF16)).astype(BF16)


def _mem_prompt(q, mkv, *, tm=512):
    m = q.shape[0]
    return pl.pallas_call(
        _mem_prompt_kernel,
        out_shape=jax.ShapeDtypeStruct((m, MEM_DIM), BF16),
        grid=(m // tm,),
        in_specs=[
            pl.BlockSpec((tm, MEM_DIM), lambda i: (i, 0)),
            pl.BlockSpec((N_MEM, MEM_DIM), lambda i: (0, 0)),
            pl.BlockSpec((N_MEM, MEM_DIM), lambda i: (0, 1)),
        ],
        out_specs=pl.BlockSpec((tm, MEM_DIM), lambda i: (i, 0)),
        compiler_params=_cparams("parallel"),
        name="mem_prompt",
    )(q, mkv, mkv)


def _mem_sample_kernel(q_ref, mk_ref, mv_ref, o_ref, *, bb):
    scale = MEM_HEAD_DIM ** -0.5
    pairs = [(b, h) for b in range(bb) for h in range(MEM_HEADS)]
    rows = lambda h: pl.ds(h, N_MEM, stride=MEM_HEADS)
    cols = lambda h: slice(h * MEM_HEAD_DIM, (h + 1) * MEM_HEAD_DIM)
    q = [q_ref[b] for b in range(bb)]
    x = [_dot_nt(q[b][:, cols(h)], mk_ref[b, rows(h), :].astype(BF16)) * scale for b, h in pairs]
    m = [jnp.max(xi, axis=-1, keepdims=True) for xi in x]
    p = [jnp.exp(xi - mi) for xi, mi in zip(x, m)]
    inv = [1.0 / jnp.sum(pi, axis=-1, keepdims=True) for pi in p]
    o = [_dot((pi * ii).astype(BF16), mv_ref[b, rows(h), :].astype(BF16)) for (b, h), pi, ii in zip(pairs, p, inv)]
    for b in range(bb):
        o_ref[b] = jnp.concatenate(o[b * MEM_HEADS:(b + 1) * MEM_HEADS], axis=-1).astype(BF16)


def _mem_sample(q, mk, mv, *, bb=8):
    b, t, _ = q.shape
    blk = lambda i: (i, 0, 0)
    return pl.pallas_call(
        functools.partial(_mem_sample_kernel, bb=bb),
        out_shape=jax.ShapeDtypeStruct((b, t, MEM_DIM), BF16),
        grid=(b // bb,),
        in_specs=[
            pl.BlockSpec((bb, t, MEM_DIM), blk),
            pl.BlockSpec((bb, N_MEM * MEM_HEADS, MEM_HEAD_DIM), blk),
            pl.BlockSpec((bb, N_MEM * MEM_HEADS, MEM_HEAD_DIM), blk),
        ],
        out_specs=pl.BlockSpec((bb, t, MEM_DIM), blk),
        compiler_params=_cparams("parallel"),
        name="mem_sample",
    )(q, mk, mv)


def _head_sum(x, ones_ref):
    hi = x.astype(BF16)
    lo = (x - hi.astype(F32)).astype(BF16)
    ones = ones_ref[...]
    w = ones.shape[0]
    return jnp.concatenate([_dot(hi[:, c:c + w], ones) + _dot(lo[:, c:c + w], ones) for c in range(0, x.shape[1], w)],
                           axis=1)


def _token_shift(x_ref, pre_ref, start_ref, mu_ref, *, seq, tm, tile):
    x = x_ref[...]
    row = lax.broadcasted_iota(jnp.int32, x.shape, 0)
    shifted = pltpu.roll(x, 1, axis=0)
    if seq >= tm:
        is_start = (tile * tm) % seq == 0
        first_prev = jnp.where(is_start, start_ref[0], pre_ref[SUBLANES - 1:SUBLANES, :])
        prev = jnp.where(row == 0, first_prev, shifted)
    else:
        prev = jnp.where(row % seq == 0, start_ref[...], shifted)
    return x + mu_ref[...] * (prev - x)


def _rwkv_features(xs, w0_ref, ww2_ref, a0_ref, aw2_ref, gw2_ref, kk_ref, ka_ref, rk_ref, ones_ref):
    d = RWKV_DIM
    r = xs[:, 0:d]
    k = xs[:, d:2 * d]
    v = xs[:, 2 * d:3 * d]
    lw = xs[:, 3 * d:3 * d + LORA_W]
    la = xs[:, 3 * d + LORA_W:3 * d + LORA_W + LORA_A]
    lg = xs[:, 3 * d + LORA_W + LORA_A:]
    wpre = w0_ref[...] + _dot_hi(jnp.tanh(lw), ww2_ref[...])
    w = -jax.nn.softplus(-wpre) - 0.5
    log_decay = -jnp.exp(w)
    a = jax.nn.sigmoid(a0_ref[...] + _dot_hi(la, aw2_ref[...]))
    g = _dot_hi(jax.nn.sigmoid(lg), gw2_ref[...])
    kk = k * kk_ref[...]
    kk = kk / jnp.maximum(jnp.sqrt(_head_sum(kk * kk, ones_ref)), 1e-12)
    kh = k * (1.0 + (a - 1.0) * ka_ref[...])
    bonus = _head_sum(r * kh * rk_ref[...], ones_ref) * v
    return r, log_decay, kh, v, kk, kk * a, g, bonus


def _rwkv_prep_kernel(x_ref, pre_ref, start_ref, mu_ref, w0_ref, ww2_ref, a0_ref, aw2_ref, gw2_ref, kk_ref, ka_ref,
                      rk_ref, ones_ref, *outs, seq, tm):
    xs = _token_shift(x_ref, pre_ref, start_ref, mu_ref, seq=seq, tm=tm, tile=pl.program_id(0))
    feats = _rwkv_features(xs, w0_ref, ww2_ref, a0_ref, aw2_ref, gw2_ref, kk_ref, ka_ref, rk_ref, ones_ref)
    for o_ref, val in zip(outs, feats):
        o_ref[...] = val


def _rwkv_prep(x, shift0, p, ones, *, seq, tm=256):
    m = x.shape[0]
    row = lambda i: (i, 0)
    const = lambda i: (0, 0)
    vec = lambda n: pl.BlockSpec((1, n), const)
    out = jax.ShapeDtypeStruct((m, RWKV_DIM), F32)
    if seq >= tm:
        assert seq % tm == 0
        start = shift0
        start_spec = pl.BlockSpec((1, 1, RWKV_IN), lambda i: ((i * tm) // seq, 0, 0))
    else:
        assert tm % seq == 0
        start = jnp.repeat(shift0[:, 0], seq, axis=0)
        start_spec = pl.BlockSpec((tm, RWKV_IN), row)
    pre_blocks = tm // SUBLANES
    return pl.pallas_call(
        functools.partial(_rwkv_prep_kernel, seq=seq, tm=tm),
        out_shape=(out,) * 8,
        grid=(m // tm,),
        in_specs=[
            pl.BlockSpec((tm, RWKV_IN), row),
            pl.BlockSpec((SUBLANES, RWKV_IN), lambda i: (jnp.maximum(i * pre_blocks - 1, 0), 0)),
            start_spec,
            vec(RWKV_IN), vec(RWKV_DIM),
            pl.BlockSpec((LORA_W, RWKV_DIM), const),
            vec(RWKV_DIM),
            pl.BlockSpec((LORA_A, RWKV_DIM), const),
            pl.BlockSpec((LORA_G, RWKV_DIM), const),
            vec(RWKV_DIM), vec(RWKV_DIM), vec(RWKV_DIM),
            pl.BlockSpec(ones.shape, const),
        ],
        out_specs=(pl.BlockSpec((tm, RWKV_DIM), row),) * 8,
        compiler_params=_cparams("parallel"),
        name="rwkv_prep",
    )(x, x, start, p["mu"], p["w0"], p["w_w2"], p["a0"], p["a_w2"], p["g_w2"], p["k_k"], p["k_a"], p["r_k"], ones)


def _rwkv_lanes_kernel(r_ref, lw_ref, k_ref, v_ref, kk_ref, kka_ref, s_ref, y_ref, so_ref, v_scr, y_scr, *, t):
    n = RWKV_HEAD_DIM
    nb = s_ref.shape[-1]
    heads = range(2)

    def token_major(ref, j):
        return ref[pl.ds(j, nb, stride=t), :].T

    for j in range(t):
        v_scr[...] = token_major(v_ref, j)
        r_t, k_t, kka_t = (token_major(ref, j) for ref in (r_ref, k_ref, kka_ref))
        neg_kk_t = -token_major(kk_ref, j)
        w_t = jnp.exp(token_major(lw_ref, j))
        src = s_ref if j == 0 else so_ref

        def value_group(g, carry):
            rows = pl.multiple_of(g * SUBLANES, SUBLANES)
            ys = [[] for _ in heads]
            vg = [v_scr[pl.ds(h * n + rows, SUBLANES), :] for h in heads]
            for i in range(SUBLANES):
                for h in heads:
                    f = slice(h * n, (h + 1) * n)
                    s = src[h, rows + i]
                    sa = jnp.sum(s * neg_kk_t[f], axis=0, keepdims=True)
                    s = s * w_t[f] + sa * kka_t[f] + vg[h][i:i + 1] * k_t[f]
                    so_ref[h, rows + i] = s
                    ys[h].append(jnp.sum(s * r_t[f], axis=0, keepdims=True))
            for h in heads:
                y_scr[pl.ds(h * n + rows, SUBLANES), :] = jnp.concatenate(ys[h], axis=0)
            return carry

        lax.fori_loop(0, n // SUBLANES, value_group, 0)
        y_ref[j] = y_scr[...].T


def _rwkv_lanes(r, lw, k, v, kk, kka, s0, *, t):
    m, d = r.shape
    nb = m // t
    n = RWKV_HEAD_DIM
    assert nb == LANES, "one batch per lane"
    tok = pl.BlockSpec((m, LANES), lambda p: (0, p))
    st = pl.BlockSpec((2, n, n, nb), lambda p: (p, 0, 0, 0))
    return pl.pallas_call(
        functools.partial(_rwkv_lanes_kernel, t=t),
        out_shape=(jax.ShapeDtypeStruct((t, nb, d), F32), jax.ShapeDtypeStruct((RWKV_HEADS, n, n, nb), F32)),
        grid=(RWKV_HEADS // 2,),
        in_specs=[tok] * 6 + [st],
        out_specs=(pl.BlockSpec((t, nb, LANES), lambda p: (0, 0, p)), st),
        scratch_shapes=[pltpu.VMEM((LANES, nb), F32), pltpu.VMEM((LANES, nb), F32)],
        compiler_params=_cparams("parallel"),
        name="rwkv_lanes",
    )(r, lw, k, v, kk, kka, s0)


CHUNK = 64
GROUP_HEADS = 4
GROUP_W = GROUP_HEADS * RWKV_HEAD_DIM
N_GROUPS = RWKV_HEADS // GROUP_HEADS
(MASK_SAME, MASK_STRICT, MASK_INCL, MASK_LEVEL0) = (0, 1, 2, 3)
N_LEVELS = int(math.log2(CHUNK))


def _chunk_masks():
    i = np.arange(GROUP_W)
    same = (i[:, None] // CHUNK) == (i[None, :] // CHUNK)
    masks = [same, same & (i[None, :] < i[:, None]), same & (i[None, :] <= i[:, None])]
    for lvl in range(N_LEVELS):
        m = 1 << lvl
        masks.append(((i[:, None] // (2 * m)) == (i[None, :] // (2 * m))) & ((i[:, None] // m) != (i[None, :] // m))
                     & (i[None, :] < i[:, None]))
    return np.stack(masks).astype(np.float32)


def _rwkv_prompt_kernel(x_ref, pre_ref, start_ref, mu_ref, w0_ref, ww2_ref, a0_ref, aw2_ref, gw2_ref, kk_p_ref,
                        ka_ref, rk_ref, ones_ref, lnw_ref, lnb_ref, st0_ref, tri_ref, eye_ref, mask_ref,
                        *rest, seq, tm, n_tiles, n_riders):
    rider_in, (o_ref, sto_ref), rider_out = rest[:n_riders], rest[n_riders:n_riders + 2], rest[n_riders + 2:-11]
    st_scr, xs_ref, y_ref, r_set, lw_set, k_set, v_set, kk_set, kka_set, g_set, bonus_set = rest[-11:]
    for src, dst in zip(rider_in, rider_out):
        dst[...] = src[...].astype(BF16)
    step = pl.program_id(0)
    cur = (step + 1) % 2
    nxt = step % 2
    sets = (r_set, lw_set, k_set, v_set, kk_set, kka_set, g_set, bonus_set)
    r_ref, lw_ref, k_ref, v_ref, kk_ref, kka_ref = (s.at[cur] for s in sets[:6])

    @pl.when(step == 0)
    def _():
        st_scr[...] = st0_ref[...]
        for s in sets:
            s[1] = jnp.zeros(s.shape[1:], F32)

    xs_ref[...] = _token_shift(x_ref, pre_ref, start_ref, mu_ref, seq=seq, tm=tm, tile=jnp.minimum(step, n_tiles - 1))
    n_chunks = tm // CHUNK

    piece = 2 * CHUNK
    def features(c):
        rows = slice(c * piece, (c + 1) * piece)
        feats = _rwkv_features(xs_ref[rows, :], w0_ref, ww2_ref, a0_ref, aw2_ref, gw2_ref, kk_p_ref, ka_ref, rk_ref,
                               ones_ref)
        for s, val in zip(sets, feats):
            s[nxt, rows, :] = val

    eye = eye_ref[...]
    tri = tri_ref[...]
    tile_rows = lambda x: jnp.concatenate([x] * GROUP_HEADS, axis=0)
    block_diag = lambda x: (tile_rows(x) * mask_ref[MASK_SAME]).astype(BF16)

    chains = [(slice(c * CHUNK, (c + 1) * CHUNK), slice(g * GROUP_W, (g + 1) * GROUP_W))
              for c in range(n_chunks) for g in range(N_GROUPS)]
    each = lambda f, *cols: [f(*args) for args in zip(*cols)]
    same, strict, incl = mask_ref[MASK_SAME], mask_ref[MASK_STRICT], mask_ref[MASK_INCL]

    def cum_decay(lw):
        h1 = lw.astype(BF16)
        r1 = lw - h1.astype(F32)
        h2 = r1.astype(BF16)
        h3 = (r1 - h2.astype(F32)).astype(BF16)
        return _dot(tri, h1) + _dot(tri, h2) + _dot(tri, h3)

    lw = [lw_ref[rows, sl] for rows, sl in chains]
    kka = [kka_ref[rows, sl] for rows, sl in chains]
    k = [k_ref[rows, sl] for rows, sl in chains]
    cum = each(cum_decay, lw)
    cum_last = each(lambda c: c[CHUNK - 1:CHUNK, :], cum)
    p_inv = each(lambda c: jnp.exp(-c), cum)
    p_tail = each(lambda c, cl: jnp.exp(cl - c), cum, cum_last)
    a_bd = [block_diag(-kk_ref[rows, sl] * jnp.exp(c - l)) for (rows, sl), c, l in zip(chains, cum, lw)]
    r_f = [tile_rows(r_ref[rows, sl] * jnp.exp(c)) * same for (rows, sl), c in zip(chains, cum)]
    r_bd = each(lambda x: x.astype(BF16), r_f)
    v_bd = [block_diag(v_ref[rows, sl]) for rows, sl in chains]
    b_rep = each(lambda x, p: tile_rows((x * p).astype(BF16)), kka, p_inv)
    k_rep = each(lambda x, p: tile_rows((x * p).astype(BF16)), k, p_inv)
    bh_rep = each(lambda x, p: tile_rows(x * p), kka, p_tail)
    kh_rep = each(lambda x, p: tile_rows(x * p), k, p_tail)

    l_ab_f = each(lambda a, b: _dot_nt(a, b) * strict, a_bd, b_rep)
    l_ab = each(lambda x: x.astype(BF16), l_ab_f)
    l_ak = each(lambda a, b: (_dot_nt(a, b) * strict).astype(BF16), a_bd, k_rep)
    m_rb = each(lambda a, b: (_dot_nt(a, b) * incl).astype(BF16), r_bd, b_rep)
    m_rk = each(lambda a, b: (_dot_nt(a, b) * incl).astype(BF16), r_bd, k_rep)
    bh_t = each(lambda x: (x.T * same).astype(BF16), bh_rep)
    kh_t = each(lambda x: (x.T * same).astype(BF16), kh_rep)

    pending = list(range(tm // piece))

    def next_features():
        if pending:
            features(pending.pop(0))

    d = each(lambda l: eye + l * mask_ref[MASK_LEVEL0], l_ab_f)
    for lvl in range(1, N_LEVELS):
        d_b = each(lambda x: x.astype(BF16), d)
        x = each(lambda l, db: (_dot(l, db) * mask_ref[MASK_LEVEL0 + lvl]).astype(BF16), l_ab, d_b)
        next_features()
        d = each(lambda dd, db, xx: dd + _dot(db, xx), d, d_b, x)
    t_b = each(lambda x: x.astype(BF16), d)
    while pending:
        next_features()

    wm = each(lambda a, b, vv: _dot(jnp.concatenate([a, b], axis=0), vv), l_ak, m_rk, v_bd)
    twa = each(lambda t, w, a: _dot(t, jnp.concatenate([w[:GROUP_W].astype(BF16), a], axis=1)).astype(BF16),
               t_b, wm, a_bd)
    ry = each(_dot, m_rb, twa)
    mn = each(_dot, bh_t, twa)
    khv = each(_dot, kh_t, v_bd)
    y0 = each(lambda a, w: a[:, :GROUP_W] + w[GROUP_W:], ry, wm)
    n_x = each(lambda a, b: a[:, :GROUP_W] + b, mn, khv)
    mr = each(lambda a, cl, rf, b: jnp.concatenate(
        [(eye * jnp.exp(cl) + a[:, GROUP_W:]).astype(BF16), (rf + b[:, GROUP_W:]).astype(BF16)], axis=0),
        mn, cum_last, r_f, ry)

    st = [st_scr[g] for g in range(N_GROUPS)]
    for i, (rows, sl) in enumerate(chains):
        g = i % N_GROUPS
        ys = _dot(mr[i], st[g].astype(BF16))
        st[g] = ys[:GROUP_W] + n_x[i]
        y_bd = ys[GROUP_W:] + y0[i]
        y = y_bd[0:CHUNK]
        for h in range(1, GROUP_HEADS):
            y = y + y_bd[h * CHUNK:(h + 1) * CHUNK]
        y_ref[rows, sl] = y
    for g in range(N_GROUPS):
        st_scr[g] = st[g]
    o_ref[...] = _rwkv_output(y_ref[...], bonus_set[cur], g_set[cur], lnw_ref, lnb_ref, ones_ref)

    @pl.when(step == pl.num_programs(0) - 1)
    def _():
        sto_ref[...] = st_scr[...]


def _rwkv_prompt(x, shift0, s0, p, ones, *, riders=(), chunks_per_step=4):
    t = x.shape[0]
    d = RWKV_DIM
    n = RWKV_HEAD_DIM
    tt = CHUNK * chunks_per_step
    assert CHUNK == n and t % tt == 0
    st0 = jnp.einsum("ghvk,hj->ghkjv", s0.reshape(N_GROUPS, GROUP_HEADS, n, n), jnp.eye(GROUP_HEADS, dtype=F32))
    st0 = st0.reshape(N_GROUPS, GROUP_W, GROUP_W)
    tri = jnp.asarray(np.tril(np.ones((CHUNK, CHUNK), np.float32)), BF16)
    eye = jnp.eye(GROUP_W, dtype=F32)
    masks = jnp.asarray(_chunk_masks())
    const = lambda c: (0, 0)
    vec = lambda width: pl.BlockSpec((1, width), const)
    st_spec = pl.BlockSpec((N_GROUPS, GROUP_W, GROUP_W), lambda c: (0, 0, 0))
    feature_set = pltpu.VMEM((2, tt, d), F32)
    pre_blocks = tt // SUBLANES
    n_tiles = t // tt
    fill = lambda c: jnp.minimum(c, n_tiles - 1)
    rider_specs = _rider_specs(riders, n_tiles + 1, lambda c: c)
    o, st, *rounded = pl.pallas_call(
        functools.partial(_rwkv_prompt_kernel, seq=t, tm=tt, n_tiles=n_tiles, n_riders=len(riders)),
        out_shape=[jax.ShapeDtypeStruct((t, d), BF16), jax.ShapeDtypeStruct((N_GROUPS, GROUP_W, GROUP_W), F32)]
        + [jax.ShapeDtypeStruct(arr.shape, BF16) for arr in riders],
        grid=(n_tiles + 1,),
        in_specs=[
            pl.BlockSpec((tt, RWKV_IN), lambda c: (fill(c), 0)),
            pl.BlockSpec((SUBLANES, RWKV_IN), lambda c: (jnp.maximum(fill(c) * pre_blocks - 1, 0), 0)),
            pl.BlockSpec((1, 1, RWKV_IN), lambda c: (0, 0, 0)),
            vec(RWKV_IN), vec(d),
            pl.BlockSpec((LORA_W, d), const),
            vec(d),
            pl.BlockSpec((LORA_A, d), const),
            pl.BlockSpec((LORA_G, d), const),
            vec(d), vec(d), vec(d),
            pl.BlockSpec(ones.shape, const),
            vec(d), vec(d),
            st_spec,
            pl.BlockSpec((CHUNK, CHUNK), const),
            pl.BlockSpec((GROUP_W, GROUP_W), const),
            pl.BlockSpec(masks.shape, lambda c: (0, 0, 0)),
        ] + rider_specs,
        out_specs=[pl.BlockSpec((tt, d), lambda c: (jnp.maximum(c - 1, 0), 0)), st_spec] + rider_specs,
        scratch_shapes=[pltpu.VMEM((N_GROUPS, GROUP_W, GROUP_W), F32), pltpu.VMEM((tt, RWKV_IN), F32),
                        pltpu.VMEM((tt, d), F32)] + [feature_set] * 8,
        compiler_params=_cparams("arbitrary"),
        name="rwkv_prompt",
    )(x, x, shift0, p["mu"], p["w0"], p["w_w2"], p["a0"], p["a_w2"], p["g_w2"], p["k_k"], p["k_a"], p["r_k"], ones,
      p["ln_w"], p["ln_b"], st0, tri, eye, masks, *riders)
    st5 = st.reshape(N_GROUPS, GROUP_HEADS, n, GROUP_HEADS, n)
    s_new = jnp.einsum("ghkjv,hj->ghvk", st5, jnp.eye(GROUP_HEADS, dtype=F32)).reshape(RWKV_HEADS, n, n)
    return o, s_new, rounded


def _rwkv_output(y, bonus, gate, lnw_ref, lnb_ref, ones_ref):
    inv_n = 1.0 / RWKV_HEAD_DIM
    mu = _head_sum(y, ones_ref) * inv_n
    dlt = y - mu
    var = _head_sum(dlt * dlt, ones_ref) * inv_n
    yn = dlt * lax.rsqrt(var + GN_EPS) * lnw_ref[...] + lnb_ref[...]
    return ((yn + bonus) * gate).astype(BF16)


def _rwkv_post_kernel(y_ref, bonus_ref, g_ref, lnw_ref, lnb_ref, ones_ref, o_ref):
    o_ref[...] = _rwkv_output(y_ref[...], bonus_ref[...], g_ref[...], lnw_ref, lnb_ref, ones_ref)


def _rwkv_post(y, bonus, g, lnw, lnb, ones, *, tm=256):
    m = y.shape[0]
    row = lambda i: (i, 0)
    const = lambda i: (0, 0)
    tile = pl.BlockSpec((tm, RWKV_DIM), row)
    return pl.pallas_call(
        _rwkv_post_kernel,
        out_shape=jax.ShapeDtypeStruct((m, RWKV_DIM), BF16),
        grid=(m // tm,),
        in_specs=[tile, tile, tile, pl.BlockSpec((1, RWKV_DIM), const), pl.BlockSpec((1, RWKV_DIM), const),
                  pl.BlockSpec(ones.shape, const)],
        out_specs=tile,
        compiler_params=_cparams("parallel"),
        name="rwkv_post",
    )(y, bonus, g, lnw, lnb, ones)


GATE_BLOCK = math.gcd(PROJ_DIM, D_MODEL)


def _merge_kernel(h_ref, u_ref, os_ref, or_ref, om_ref, wos_ref, wor_ref, wom_ref, wout_ref, *rest, parts):
    gate_refs, (o_ref, acc_ref) = rest[:N_BRANCH * parts], rest[N_BRANCH * parts:]
    j = pl.program_id(1)

    @pl.when(j == 0)
    def _():
        acc_ref[...] = jnp.zeros_like(acc_ref)

    u = u_ref[...]
    merged = None
    for b, (x_ref, w_ref) in enumerate(((os_ref, wos_ref), (or_ref, wor_ref), (om_ref, wom_ref))):
        gate = jnp.concatenate([_dot(u, gate_refs[b * parts + c][...]) for c in range(parts)], axis=1)
        term = jax.nn.sigmoid(gate) * _dot(x_ref[...], w_ref[...])
        merged = term if merged is None else merged + term
    acc_ref[...] += _dot(merged.astype(BF16), wout_ref[...])

    @pl.when(j == pl.num_programs(1) - 1)
    def _():
        o_ref[...] = h_ref[...] + acc_ref[...]


def _merge(h, u, o_swa, o_rw, o_mem, w_in, wo_swa, wo_rw, wo_mem, w_out, *, tm=512, tn=512):
    m, d = h.shape
    nt = d // tn
    parts = tn // GATE_BLOCK
    g0 = PROJ_DIM // GATE_BLOCK
    row = lambda i, j: (i, 0)
    col = lambda i, j: (0, j)
    gate_specs = [pl.BlockSpec((d, GATE_BLOCK), functools.partial(
        lambda i, j, off: (0, off + j * parts), off=g0 + b * (d // GATE_BLOCK) + c))
        for b in range(N_BRANCH) for c in range(parts)]
    return pl.pallas_call(
        functools.partial(_merge_kernel, parts=parts),
        out_shape=jax.ShapeDtypeStruct((m, d), F32),
        grid=(m // tm, nt),
        in_specs=[
            pl.BlockSpec((tm, d), row),
            pl.BlockSpec((tm, d), row),
            pl.BlockSpec((tm, SWA_Q_DIM), row),
            pl.BlockSpec((tm, RWKV_DIM), row),
            pl.BlockSpec((tm, MEM_DIM), row),
            pl.BlockSpec((SWA_Q_DIM, tn), col),
            pl.BlockSpec((RWKV_DIM, tn), col),
            pl.BlockSpec((MEM_DIM, tn), col),
            pl.BlockSpec((tn, d), lambda i, j: (j, 0)),
        ] + gate_specs,
        out_specs=pl.BlockSpec((tm, d), row),
        scratch_shapes=[pltpu.VMEM((tm, d), F32)],
        compiler_params=_cparams("parallel", "arbitrary"),
        name="merge",
    )(h, u, o_swa, o_rw, o_mem, wo_swa, wo_rw, wo_mem, w_out, *([w_in] * (N_BRANCH * parts)))


def _t5_bucket(dist):
    max_exact = N_BUCKETS // 2
    d = np.maximum(dist, 0)
    log_ratio = (np.log(np.maximum(d, 1).astype(np.float32) / np.float32(max_exact))
                 / np.float32(math.log(MAX_DISTANCE / max_exact)))
    large = np.minimum(max_exact + (log_ratio * (N_BUCKETS - max_exact)).astype(np.int32), N_BUCKETS - 1)
    return np.where(d < max_exact, d, large).astype(np.int32)


def _rel_bias(table, dist):
    onehot = np.eye(N_BUCKETS, dtype=np.float32)[_t5_bucket(dist).reshape(-1)]
    bias = jnp.einsum("nb,bh->hn", jnp.asarray(onehot), table, precision=lax.Precision.HIGHEST)
    return bias.reshape(SWA_HEADS, *dist.shape)


def _rwkv_branch(xr, shift0, s0, p, ones, riders=()):
    b, t, _ = xr.shape
    flat = lambda z: z.reshape(b * t, z.shape[-1])
    if b == 1:
        o, s_new, rounded = _rwkv_prompt(flat(xr), shift0, s0[0], p, ones, riders=riders)
        return o, s_new[None], rounded
    assert b == LANES and not riders, "short sequences are batched one per lane"
    r, w, k, v, kk, kka, g, bonus = _rwkv_prep(flat(xr), shift0, p, ones, seq=t)
    y, s_new = _rwkv_lanes(r, w, k, v, kk, kka, jnp.transpose(s0, (1, 2, 3, 0)), t=t)
    o = _rwkv_post(flat(jnp.transpose(y, (1, 0, 2))), bonus, g, p["ln_w"], p["ln_b"], ones)
    return o, jnp.transpose(s_new, (3, 0, 1, 2))


def kernel(x_prompt, mem_prompt, x_sample, cache_swa_k, cache_swa_v, state_rwkv, state_rwkv_shift, cache_mem_k, cache_mem_v, ffn1_norm, ffn1_wi, ffn1_wo, mix_norm, w_in, swa_sinks, rel_bias_table, rwkv_mu, rwkv_w0, rwkv_w_w2, rwkv_a0, rwkv_a_w2, rwkv_g_w2, rwkv_k_k, rwkv_k_a, rwkv_r_k, rwkv_ln_w, rwkv_ln_b, mem_norm, w_mem_kv, w_o_swa, w_o_rwkv, w_o_mem, w_out, ffn2_norm, ffn2_wi, ffn2_wo, final_norm):
    assert ffn1_wi.shape[0] == 1, "single-layer trunk"
    bp, tp, d = x_prompt.shape
    bs, ts, _ = x_sample.shape
    assert bp == 1
    row = lambda z: z.reshape(1, -1).astype(F32)

    g1, gm, g2, gf = row(ffn1_norm[0]), row(mix_norm[0]), row(ffn2_norm[0]), row(final_norm)
    rp = {
        "mu": row(rwkv_mu[0]), "w0": row(rwkv_w0[0]), "w_w2": rwkv_w_w2[0], "a0": row(rwkv_a0[0]),
        "a_w2": rwkv_a_w2[0], "g_w2": rwkv_g_w2[0], "k_k": row(rwkv_k_k[0]), "k_a": row(rwkv_k_a[0]),
        "r_k": row(rwkv_r_k[0]), "ln_w": row(rwkv_ln_w[0]), "ln_b": row(rwkv_ln_b[0]),
    }
    seg = np.arange(GROUP_W) // RWKV_HEAD_DIM
    ones = jnp.asarray(seg[:, None] == seg[None, :], dtype=BF16)
    sinks = swa_sinks[0].astype(F32)
    table = rel_bias_table.astype(F32)

    xp = x_prompt.reshape(tp, d)
    xs = x_sample.reshape(bs * ts, d)
    hs, wg1, wu1, wo1 = _ffn(xs, g1, ffn1_wi[0], ffn1_wo[0], gf, final_norm=False, emit_weights=True, tf=256)
    hp, w_in_b = _ffn(xp, g1, (wg1, wu1), wo1, gf, final_norm=False, riders=(w_in[0],))
    w_q = w_in_b[:, :SWA_Q_DIM].reshape(d, SWA_KV_HEADS, SWA_GROUP, SWA_HEAD_DIM).transpose(0, 2, 1, 3).reshape(d, SWA_Q_DIM)
    qp, kvp, xrp, qmp, up = _inproj(hp, gm, w_q, w_in_b)
    qs, kvs, xrs, qms, us = _inproj(hs, gm, w_q, w_in_b)

    w = WINDOW
    dist_p = np.arange(w)[:, None] + w - np.arange(2 * w)[None, :]
    bias_p = _rel_bias(table, dist_p).reshape(SWA_KV_HEADS, SWA_GROUP, w, 2 * w).transpose(1, 0, 2, 3)
    bias_p = bias_p.reshape(SWA_GROUP, SWA_KV_HEADS * w, 2 * w)
    sink_p = jnp.repeat(sinks.reshape(SWA_KV_HEADS, SWA_GROUP).T, w, axis=1).reshape(SWA_GROUP, SWA_KV_HEADS * w, 1)
    o_swa_p = _swa_prompt(qp, kvp, bias_p, sink_p)

    wbuf = cache_swa_k.shape[2]
    dist_s = np.arange(ts)[:, None] + wbuf - np.arange(wbuf + ts)[None, :]
    bias_s = _rel_bias(table, dist_s).reshape(SWA_HEADS * ts, wbuf + ts)
    sink_rows = jnp.repeat(sinks, ts).reshape(SWA_HEADS * ts, 1)
    qs_gt = qs.reshape(bs, ts, SWA_GROUP, SWA_KV_DIM).transpose(0, 2, 1, 3).reshape(bs, SWA_GROUP * ts, SWA_KV_DIM)
    kbuf = cache_swa_k[0].reshape(bs, wbuf, SWA_KV_DIM).transpose(0, 2, 1)
    vbuf = cache_swa_v[0].reshape(bs, wbuf, SWA_KV_DIM).transpose(0, 2, 1)
    o_swa_s, knew_t, vnew_t = _swa_sample(qs_gt, kvs.reshape(bs, ts, 2 * SWA_KV_DIM), kbuf, vbuf,
                                          bias_s[:, :wbuf], bias_s[:, wbuf:], sink_rows)
    o_swa_s = o_swa_s.reshape(bs, SWA_GROUP, ts, SWA_KV_DIM).transpose(0, 2, 1, 3).reshape(bs * ts, SWA_Q_DIM)

    zero_shift = jnp.zeros((bp, 1, RWKV_IN), F32)
    zero_state = jnp.zeros((bp, RWKV_HEADS, RWKV_HEAD_DIM, RWKV_HEAD_DIM), F32)
    o_rw_p, state_p, (wi2, wo2, wo_swa, wo_rw, wo_mem, w_out_b) = _rwkv_branch(
        xrp.reshape(bp, tp, RWKV_IN), zero_shift, zero_state, rp, ones,
        riders=(ffn2_wi[0], ffn2_wo[0], w_o_swa[0], w_o_rwkv[0], w_o_mem[0], w_out[0]))
    wo_swa = wo_swa.reshape(SWA_KV_HEADS, SWA_GROUP, SWA_HEAD_DIM, d).transpose(1, 0, 2, 3).reshape(SWA_Q_DIM, d)
    o_rw_s, state_s = _rwkv_branch(xrs.reshape(bs, ts, RWKV_IN), state_rwkv_shift[0], state_rwkv[0], rp, ones)

    mkv = _norm_matmul(mem_prompt.reshape(N_MEM, d), row(mem_norm[0]), w_mem_kv[0].astype(BF16))
    o_mem_p = _mem_prompt(qmp, mkv)
    o_mem_s = _mem_sample(qms.reshape(bs, ts, MEM_DIM), cache_mem_k[0].reshape(bs, N_MEM * MEM_HEADS, MEM_HEAD_DIM),
                          cache_mem_v[0].reshape(bs, N_MEM * MEM_HEADS, MEM_HEAD_DIM)).reshape(bs * ts, MEM_DIM)

    hp = _merge(hp, up, o_swa_p, o_rw_p, o_mem_p, w_in_b, wo_swa, wo_rw, wo_mem, w_out_b)
    hs = _merge(hs, us, o_swa_s, o_rw_s, o_mem_s, w_in_b, wo_swa, wo_rw, wo_mem, w_out_b)
    y_prompt = _ffn(hp, g2, wi2, wo2, gf, final_norm=True).reshape(bp, tp, d)
    y_sample = _ffn(hs, g2, wi2, wo2, gf, final_norm=True).reshape(bs, ts, d)

    wp = min(w, tp)
    p_k = kvp[tp - wp:, :SWA_KV_DIM].reshape(1, bp, wp, SWA_KV_HEADS, SWA_HEAD_DIM)
    p_v = kvp[tp - wp:, SWA_KV_DIM:].reshape(1, bp, wp, SWA_KV_HEADS, SWA_HEAD_DIM)
    p_mk = mkv[:, :MEM_DIM].reshape(1, bp, N_MEM, MEM_HEADS, MEM_HEAD_DIM)
    p_mv = mkv[:, MEM_DIM:].reshape(1, bp, N_MEM, MEM_HEADS, MEM_HEAD_DIM)
    s_k = knew_t.transpose(0, 2, 1).reshape(1, bs, wbuf, SWA_KV_HEADS, SWA_HEAD_DIM)
    s_v = vnew_t.transpose(0, 2, 1).reshape(1, bs, wbuf, SWA_KV_HEADS, SWA_HEAD_DIM)
    return (y_prompt, y_sample,
            p_k, p_v, state_p[None], xrp[tp - 1:].reshape(1, bp, 1, RWKV_IN), p_mk, p_mv,
            s_k, s_v, state_s[None], xrs.reshape(bs, ts, RWKV_IN)[:, ts - 1:][None])
```

```python
import functools
import math

import jax
import jax.numpy as jnp
import numpy as np
from jax import lax
from jax.experimental import pallas as pl
from jax.experimental.pallas import tpu as pltpu

F32 = jnp.float32
BF16 = jnp.bfloat16

D_MODEL = 2048
SWA_HEADS = 16
SWA_KV_HEADS = 4
SWA_GROUP = SWA_HEADS // SWA_KV_HEADS
SWA_HEAD_DIM = 64
SWA_Q_DIM = SWA_HEADS * SWA_HEAD_DIM
SWA_KV_DIM = SWA_KV_HEADS * SWA_HEAD_DIM
WINDOW = 128
N_BUCKETS = 32
MAX_DISTANCE = 128
RWKV_HEADS = 8
RWKV_HEAD_DIM = 64
RWKV_DIM = RWKV_HEADS * RWKV_HEAD_DIM
LORA_W = 64
LORA_A = 64
LORA_G = 128
RWKV_IN = 3 * RWKV_DIM + LORA_W + LORA_A + LORA_G
N_MEM = 256
MEM_HEADS = 4
MEM_HEAD_DIM = 128
MEM_DIM = MEM_HEADS * MEM_HEAD_DIM
N_BRANCH = 3
PROJ_DIM = SWA_Q_DIM + 2 * SWA_KV_DIM + RWKV_IN + MEM_DIM
NORM_EPS = 1e-6
GN_EPS = 64e-5
NEG_INF = -1e30

LANES = 128
SUBLANES = 8
VMEM_LIMIT = 56 * 1024 * 1024


def _cparams(*sem):
    return pltpu.CompilerParams(dimension_semantics=sem, vmem_limit_bytes=VMEM_LIMIT)


def _rms(x, g):
    return x * lax.rsqrt(jnp.mean(x * x, axis=-1, keepdims=True) + NORM_EPS) * g


def _dot(a, b):
    return jnp.dot(a, b, preferred_element_type=F32)


def _dot_nt(a, b):
    return lax.dot_general(a, b, (((1,), (1,)), ((), ())), preferred_element_type=F32)


def _dot_hi(a, b):
    a_hi = a.astype(BF16)
    b_hi = b.astype(BF16)
    a_lo = (a - a_hi.astype(F32)).astype(BF16)
    b_lo = (b - b_hi.astype(F32)).astype(BF16)
    return _dot(a_hi, b_hi) + _dot(a_hi, b_lo) + _dot(a_lo, b_hi)


def _ffn_kernel(x_ref, g_ref, wg_ref, wu_ref, wo_ref, gf_ref, *rest, final_norm, n_riders, emit_weights):
    rider_in, (o_ref, *extra_out), (xn_ref, acc_ref) = rest[:n_riders], rest[n_riders:-2], rest[-2:]
    for src, dst in zip(rider_in, extra_out[:n_riders]):
        dst[...] = src[...].astype(BF16)
    j = pl.program_id(1)

    @pl.when(j == 0)
    def _():
        xn_ref[...] = _rms(x_ref[...], g_ref[...]).astype(BF16)
        acc_ref[...] = jnp.zeros_like(acc_ref)

    wg, wu, wo = wg_ref[...], wu_ref[...], wo_ref[...]
    if emit_weights:
        wg, wu, wo = wg.astype(BF16), wu.astype(BF16), wo.astype(BF16)
        for dst, val in zip(extra_out[n_riders:], (wg, wu, wo)):
            dst[...] = val
    xn = xn_ref[...]
    gate = _dot(xn, wg)
    up = _dot(xn, wu)
    act = (gate * jax.nn.sigmoid(gate)) * up
    acc_ref[...] += _dot(act.astype(BF16), wo)

    @pl.when(j == pl.num_programs(1) - 1)
    def _():
        h = x_ref[...] + 0.5 * acc_ref[...]
        if final_norm:
            h = _rms(h, gf_ref[...])
        o_ref[...] = h


def _rider_tiling(shape, steps):
    rows, cols = shape
    best, best_score = (1, 1), (0, 0)
    for nr in range(1, rows // 16 + 1):
        if rows % nr or (rows // nr) % 16:
            continue
        for nc in range(1, cols // LANES + 1):
            if cols % nc or (cols // nc) % LANES or nr * nc > steps:
                continue
            score = (min(cols // nc, 1024), nr * nc)
            if score > best_score:
                best, best_score = (nr, nc), score
    return best


def _rider_specs(riders, steps, step_of):
    specs = []
    for arr in riders:
        nr, nc = _rider_tiling(arr.shape, steps)
        specs.append(pl.BlockSpec((arr.shape[0] // nr, arr.shape[1] // nc), functools.partial(
            lambda *idx, nc, last: (jnp.minimum(step_of(*idx), last) // nc, jnp.minimum(step_of(*idx), last) % nc),
            nc=nc, last=nr * nc - 1)))
    return specs


def _ffn(x, g, wi, wo, gf, *, final_norm, riders=(), emit_weights=False, tm=512, tf=512):
    m, d = x.shape
    dff = wo.shape[0]
    nf = dff // tf
    steps = (m // tm) * nf
    wg, wu = wi if isinstance(wi, tuple) else (wi, wi)
    up_off = 0 if isinstance(wi, tuple) else nf
    assert not emit_weights or m == tm
    weight_specs = [pl.BlockSpec((d, tf), lambda i, j: (0, j)), pl.BlockSpec((d, tf), lambda i, j: (0, j)),
                    pl.BlockSpec((tf, d), lambda i, j: (j, 0))]
    weight_out = [jax.ShapeDtypeStruct((d, dff), BF16)] * 2 + [jax.ShapeDtypeStruct((dff, d), BF16)]
    rider_specs = _rider_specs(riders, steps, lambda i, j: i * nf + j)
    out = pl.pallas_call(
        functools.partial(_ffn_kernel, final_norm=final_norm, n_riders=len(riders), emit_weights=emit_weights),
        out_shape=([jax.ShapeDtypeStruct((m, d), F32)] + [jax.ShapeDtypeStruct(arr.shape, BF16) for arr in riders]
                   + (weight_out if emit_weights else [])),
        grid=(m // tm, nf),
        in_specs=[
            pl.BlockSpec((tm, d), lambda i, j: (i, 0)),
            pl.BlockSpec((1, d), lambda i, j: (0, 0)),
            pl.BlockSpec((d, tf), lambda i, j: (0, j)),
            pl.BlockSpec((d, tf), lambda i, j: (0, j + up_off)),
            pl.BlockSpec((tf, d), lambda i, j: (j, 0)),
            pl.BlockSpec((1, d), lambda i, j: (0, 0)),
        ] + rider_specs,
        out_specs=([pl.BlockSpec((tm, d), lambda i, j: (i, 0))] + rider_specs
                   + (weight_specs if emit_weights else [])),
        scratch_shapes=[pltpu.VMEM((tm, d), BF16), pltpu.VMEM((tm, d), F32)],
        compiler_params=_cparams("arbitrary", "arbitrary"),
        name="ffn_final" if final_norm else "ffn",
    )(x, g, wg, wu, wo, gf, *riders)
    return out if (riders or emit_weights) else out[0]


def _inproj_kernel(h_ref, g_ref, wq_ref, w_ref, q_ref, kv_ref, xr_ref, qm_ref, u_ref):
    u = _rms(h_ref[...], g_ref[...]).astype(BF16)
    u_ref[...] = u
    c0, c1, c2 = SWA_Q_DIM, SWA_Q_DIM + 2 * SWA_KV_DIM, SWA_Q_DIM + 2 * SWA_KV_DIM + RWKV_IN
    q_ref[...] = _dot(u, wq_ref[...]).astype(BF16)
    kv_ref[...] = _dot(u, w_ref[:, c0:c1])
    xr_ref[...] = _dot(u, w_ref[:, c1:c2])
    qm_ref[...] = _dot(u, w_ref[:, c2:PROJ_DIM]).astype(BF16)


def _inproj(h, g, wq, w, *, tm=512):
    m, d = h.shape
    row = lambda i: (i, 0)
    return pl.pallas_call(
        _inproj_kernel,
        out_shape=(
            jax.ShapeDtypeStruct((m, SWA_Q_DIM), BF16),
            jax.ShapeDtypeStruct((m, 2 * SWA_KV_DIM), F32),
            jax.ShapeDtypeStruct((m, RWKV_IN), F32),
            jax.ShapeDtypeStruct((m, MEM_DIM), BF16),
            jax.ShapeDtypeStruct((m, d), BF16),
        ),
        grid=(m // tm,),
        in_specs=[
            pl.BlockSpec((tm, d), row),
            pl.BlockSpec((1, d), lambda i: (0, 0)),
            pl.BlockSpec((d, SWA_Q_DIM), lambda i: (0, 0), pipeline_mode=pl.Buffered(1)),
            pl.BlockSpec((d, PROJ_DIM), lambda i: (0, 0), pipeline_mode=pl.Buffered(1)),
        ],
        out_specs=(
            pl.BlockSpec((tm, SWA_Q_DIM), row),
            pl.BlockSpec((tm, 2 * SWA_KV_DIM), row),
            pl.BlockSpec((tm, RWKV_IN), row),
            pl.BlockSpec((tm, MEM_DIM), row),
            pl.BlockSpec((tm, d), row),
        ),
        compiler_params=_cparams("parallel"),
        name="inproj",
    )(h, g, wq, w)


def _norm_matmul_kernel(x_ref, g_ref, w_ref, o_ref):
    o_ref[...] = _dot(_rms(x_ref[...], g_ref[...]).astype(BF16), w_ref[...])


def _norm_matmul(x, g, w, *, tn=512):
    m, d = x.shape
    n = w.shape[1]
    return pl.pallas_call(
        _norm_matmul_kernel,
        out_shape=jax.ShapeDtypeStruct((m, n), F32),
        grid=(n // tn,),
        in_specs=[
            pl.BlockSpec((m, d), lambda j: (0, 0)),
            pl.BlockSpec((1, d), lambda j: (0, 0)),
            pl.BlockSpec((d, tn), lambda j: (0, j)),
        ],
        out_specs=pl.BlockSpec((m, tn), lambda j: (0, j)),
        compiler_params=_cparams("parallel"),
        name="norm_matmul",
    )(x, g, w)


def _swa_prompt_kernel(q_ref, kvc_ref, kvp_ref, bias_ref, sink_ref, o_ref, *, nq):
    i = pl.program_id(0)
    w = WINDOW
    rows = SWA_KV_HEADS * w
    kv_blocks = [kvp_ref[...]] + [kvc_ref[s * w:(s + 1) * w, :] for s in range(nq)]
    k_blocks = [x[:, 0:SWA_KV_DIM].astype(BF16) for x in kv_blocks]
    v_blocks = [x[:, SWA_KV_DIM:].astype(BF16) for x in kv_blocks]
    qpos = lax.broadcasted_iota(jnp.int32, (rows, 2 * w), 0) % w
    col = lax.broadcasted_iota(jnp.int32, (rows, 2 * w), 1)
    dist = qpos + w - col
    in_window = (dist >= 0) & (dist < w)
    first = in_window & ((col >= w) | (i > 0))
    lane_head = lax.broadcasted_iota(jnp.int32, (w, SWA_KV_DIM), 1) // SWA_HEAD_DIM
    scale = SWA_HEAD_DIM ** -0.5
    pairs = [(s, g) for s in range(nq) for g in range(SWA_GROUP)]
    k = [jnp.concatenate(k_blocks[s:s + 2], axis=0) for s in range(nq)]
    v = [jnp.concatenate(v_blocks[s:s + 2], axis=0) for s in range(nq)]
    logits = []
    for s, g in pairs:
        qg = q_ref[s * w:(s + 1) * w, g * SWA_KV_DIM:(g + 1) * SWA_KV_DIM].astype(F32) * scale
        qs = jnp.concatenate([jnp.where(lane_head == kvh, qg, 0.0) for kvh in range(SWA_KV_HEADS)], axis=0)
        lg = _dot_nt(qs.astype(BF16), k[s])
        logits.append(jnp.where(first if s == 0 else in_window, lg + bias_ref[g], NEG_INF))
    sink = [sink_ref[g] for _, g in pairs]
    m = [jnp.maximum(jnp.max(x, axis=-1, keepdims=True), sk) for x, sk in zip(logits, sink)]
    p = [jnp.exp(x - mi) for x, mi in zip(logits, m)]
    inv = [1.0 / (jnp.sum(pi, axis=-1, keepdims=True) + jnp.exp(sk - mi)) for pi, sk, mi in zip(p, sink, m)]
    ov = [_dot((pi * ii).astype(BF16), v[s]) for (s, _), pi, ii in zip(pairs, p, inv)]
    for (s, g), o in zip(pairs, ov):
        og = jnp.zeros((w, SWA_KV_DIM), F32)
        for kvh in range(SWA_KV_HEADS):
            og = jnp.where(lane_head == kvh, o[kvh * w:(kvh + 1) * w], og)
        o_ref[s * w:(s + 1) * w, g * SWA_KV_DIM:(g + 1) * SWA_KV_DIM] = og.astype(BF16)


def _swa_prompt(q, kv, bias, sink_rows, *, nq=8):
    t = q.shape[0]
    w = WINDOW
    rows = SWA_KV_HEADS * w
    return pl.pallas_call(
        functools.partial(_swa_prompt_kernel, nq=nq),
        out_shape=jax.ShapeDtypeStruct((t, SWA_Q_DIM), BF16),
        grid=(t // (nq * w),),
        in_specs=[
            pl.BlockSpec((nq * w, SWA_Q_DIM), lambda i: (i, 0)),
            pl.BlockSpec((nq * w, 2 * SWA_KV_DIM), lambda i: (i, 0)),
            pl.BlockSpec((w, 2 * SWA_KV_DIM), lambda i: (jnp.maximum(i * nq - 1, 0), 0)),
            pl.BlockSpec((SWA_GROUP, rows, 2 * w), lambda i: (0, 0, 0)),
            pl.BlockSpec((SWA_GROUP, rows, 1), lambda i: (0, 0, 0)),
        ],
        out_specs=pl.BlockSpec((nq * w, SWA_Q_DIM), lambda i: (i, 0)),
        compiler_params=_cparams("parallel"),
        name="swa_prompt",
    )(q, kv, kv, bias, sink_rows)


def _swa_sample_kernel(q_ref, kvn_ref, kb_ref, vb_ref, bias_b_ref, bias_n_ref, sink_ref, o_ref, ko_ref, vo_ref,
                       *, bb, t):
    gt = SWA_GROUP * t
    rows = SWA_KV_HEADS * gt
    w = kb_ref.shape[2]
    scale = SWA_HEAD_DIM ** -0.5
    lane_head = lax.broadcasted_iota(jnp.int32, (gt, SWA_KV_DIM), 1) // SWA_HEAD_DIM
    tok = lax.broadcasted_iota(jnp.int32, (rows, w), 0) % t
    keyj = lax.broadcasted_iota(jnp.int32, (rows, w), 1)
    valid_b = (tok + w - keyj) < WINDOW
    tok_n = lax.broadcasted_iota(jnp.int32, (rows, 1), 0) % t
    sink = sink_ref[...]
    bs = range(bb)
    toks = range(t)
    qall = [jnp.concatenate([jnp.where(lane_head == kvh, q_ref[b].astype(F32), 0.0) for kvh in range(SWA_KV_HEADS)],
                            axis=0) for b in bs]
    kvn = [kvn_ref[b] for b in bs]
    lb = [_dot(qall[b].astype(BF16), kb_ref[b].astype(BF16)) for b in bs]
    lb = [jnp.where(valid_b, lb[b] * scale + bias_b_ref[...], NEG_INF) for b in bs]
    ln = [[jnp.sum(qall[b] * kvn[b][j:j + 1, 0:SWA_KV_DIM], axis=-1, keepdims=True) for j in toks] for b in bs]
    ln = [[jnp.where(tok_n >= j, ln[b][j] * scale + bias_n_ref[:, j:j + 1], NEG_INF) for j in toks] for b in bs]
    m = [jnp.maximum(jnp.max(lb[b], axis=-1, keepdims=True), sink) for b in bs]
    m = [functools.reduce(jnp.maximum, ln[b], m[b]) for b in bs]
    pb = [jnp.exp(lb[b] - m[b]) for b in bs]
    pn = [[jnp.exp(ln[b][j] - m[b]) for j in toks] for b in bs]
    denom = [jnp.sum(pb[b], axis=-1, keepdims=True) + jnp.exp(sink - m[b]) for b in bs]
    inv = [1.0 / functools.reduce(jnp.add, pn[b], denom[b]) for b in bs]
    oall = [_dot_nt((pb[b] * inv[b]).astype(BF16), vb_ref[b].astype(BF16)) for b in bs]
    for b in bs:
        ob = oall[b]
        for j in toks:
            ob = ob + (pn[b][j] * inv[b]) * kvn[b][j:j + 1, SWA_KV_DIM:]
        og = jnp.zeros((gt, SWA_KV_DIM), F32)
        for kvh in range(SWA_KV_HEADS):
            og = jnp.where(lane_head == kvh, ob[kvh * gt:(kvh + 1) * gt], og)
        o_ref[b] = og.astype(BF16)
    pos = lax.broadcasted_iota(jnp.int32, (SWA_KV_DIM, w), 1)
    pad = jnp.zeros((w - SUBLANES, 2 * SWA_KV_DIM), F32)
    row8 = lax.broadcasted_iota(jnp.int32, (SUBLANES, 2 * SWA_KV_DIM), 0)
    for b in bs:
        last8 = jnp.zeros((SUBLANES, 2 * SWA_KV_DIM), F32)
        for j in toks:
            last8 = jnp.where(row8 == SUBLANES - t + j, kvn[b][j:j + 1], last8)
        tail_t = jnp.concatenate([pad, last8], axis=0).T
        ko_ref[b] = jnp.where(pos >= w - t, tail_t[:SWA_KV_DIM], pltpu.roll(kb_ref[b], w - t, axis=1))
        vo_ref[b] = jnp.where(pos >= w - t, tail_t[SWA_KV_DIM:], pltpu.roll(vb_ref[b], w - t, axis=1))


def _swa_sample(q, kvn, kbuf, vbuf, bias_b, bias_n, sink_rows, *, bb=8):
    b, gt, _ = q.shape
    t = kvn.shape[1]
    w = kbuf.shape[2]
    rows = SWA_KV_HEADS * gt
    blk = lambda i: (i, 0, 0)
    const = lambda i: (0, 0)
    cache = pl.BlockSpec((bb, SWA_KV_DIM, w), blk)
    return pl.pallas_call(
        functools.partial(_swa_sample_kernel, bb=bb, t=t),
        out_shape=(jax.ShapeDtypeStruct((b, gt, SWA_KV_DIM), BF16),
                   jax.ShapeDtypeStruct(kbuf.shape, F32), jax.ShapeDtypeStruct(vbuf.shape, F32)),
        grid=(b // bb,),
        in_specs=[
            pl.BlockSpec((bb, gt, SWA_KV_DIM), blk),
            pl.BlockSpec((bb, t, 2 * SWA_KV_DIM), blk),
            cache, cache,
            pl.BlockSpec((rows, w), const),
            pl.BlockSpec((rows, t), const),
            pl.BlockSpec((rows, 1), const),
        ],
        out_specs=(pl.BlockSpec((bb, gt, SWA_KV_DIM), blk), cache, cache),
        compiler_params=_cparams("parallel"),
        name="swa_sample",
    )(q, kvn, kbuf, vbuf, bias_b, bias_n, sink_rows)


def _mem_heads(q, mk, mv):
    scale = MEM_HEAD_DIM ** -0.5
    cols = [slice(h * MEM_HEAD_DIM, (h + 1) * MEM_HEAD_DIM) for h in range(MEM_HEADS)]
    x = [_dot_nt(q[:, sl], mk[:, sl]) * scale for sl in cols]
    m = [jnp.max(xi, axis=-1, keepdims=True) for xi in x]
    p = [jnp.exp(xi - mi) for xi, mi in zip(x, m)]
    inv = [1.0 / jnp.sum(pi, axis=-1, keepdims=True) for pi in p]
    return jnp.concatenate([_dot((pi * ii).astype(BF16), mv[:, sl]) for pi, ii, sl in zip(p, inv, cols)], axis=-1)


def _mem_prompt_kernel(q_ref, mk_ref, mv_ref, o_ref):
    o_ref[...] = _mem_heads(q_ref[...], mk_ref[...].astype(BF16), mv_ref[...].astype(BF16)).astype(BF16)


def _mem_prompt(q, mkv, *, tm=512):
    m = q.shape[0]
    return pl.pallas_call(
        _mem_prompt_kernel,
        out_shape=jax.ShapeDtypeStruct((m, MEM_DIM), BF16),
        grid=(m // tm,),
        in_specs=[
            pl.BlockSpec((tm, MEM_DIM), lambda i: (i, 0)),
            pl.BlockSpec((N_MEM, MEM_DIM), lambda i: (0, 0)),
            pl.BlockSpec((N_MEM, MEM_DIM), lambda i: (0, 1)),
        ],
        out_specs=pl.BlockSpec((tm, MEM_DIM), lambda i: (i, 0)),
        compiler_params=_cparams("parallel"),
        name="mem_prompt",
    )(q, mkv, mkv)


def _mem_sample_kernel(q_ref, mk_ref, mv_ref, o_ref, *, bb):
    scale = MEM_HEAD_DIM ** -0.5
    pairs = [(b, h) for b in range(bb) for h in range(MEM_HEADS)]
    rows = lambda h: pl.ds(h, N_MEM, stride=MEM_HEADS)
    cols = lambda h: slice(h * MEM_HEAD_DIM, (h + 1) * MEM_HEAD_DIM)
    q = [q_ref[b] for b in range(bb)]
    x = [_dot_nt(q[b][:, cols(h)], mk_ref[b, rows(h), :].astype(BF16)) * scale for b, h in pairs]
    m = [jnp.max(xi, axis=-1, keepdims=True) for xi in x]
    p = [jnp.exp(xi - mi) for xi, mi in zip(x, m)]
    inv = [1.0 / jnp.sum(pi, axis=-1, keepdims=True) for pi in p]
    o = [_dot((pi * ii).astype(BF16), mv_ref[b, rows(h), :].astype(BF16)) for (b, h), pi, ii in zip(pairs, p, inv)]
    for b in range(bb):
        o_ref[b] = jnp.concatenate(o[b * MEM_HEADS:(b + 1) * MEM_HEADS], axis=-1).astype(BF16)


def _mem_sample(q, mk, mv, *, bb=8):
    b, t, _ = q.shape
    blk = lambda i: (i, 0, 0)
    return pl.pallas_call(
        functools.partial(_mem_sample_kernel, bb=bb),
        out_shape=jax.ShapeDtypeStruct((b, t, MEM_DIM), BF16),
        grid=(b // bb,),
        in_specs=[
            pl.BlockSpec((bb, t, MEM_DIM), blk),
            pl.BlockSpec((bb, N_MEM * MEM_HEADS, MEM_HEAD_DIM), blk),
            pl.BlockSpec((bb, N_MEM * MEM_HEADS, MEM_HEAD_DIM), blk),
        ],
        out_specs=pl.BlockSpec((bb, t, MEM_DIM), blk),
        compiler_params=_cparams("parallel"),
        name="mem_sample",
    )(q, mk, mv)


def _head_sum(x, ones_ref):
    hi = x.astype(BF16)
    lo = (x - hi.astype(F32)).astype(BF16)
    ones = ones_ref[...]
    w = ones.shape[0]
    return jnp.concatenate([_dot(hi[:, c:c + w], ones) + _dot(lo[:, c:c + w], ones) for c in range(0, x.shape[1], w)],
                           axis=1)


def _token_shift(x_ref, pre_ref, start_ref, mu_ref, *, seq, tm, tile):
    x = x_ref[...]
    row = lax.broadcasted_iota(jnp.int32, x.shape, 0)
    shifted = pltpu.roll(x, 1, axis=0)
    if seq >= tm:
        is_start = (tile * tm) % seq == 0
        first_prev = jnp.where(is_start, start_ref[0], pre_ref[SUBLANES - 1:SUBLANES, :])
        prev = jnp.where(row == 0, first_prev, shifted)
    else:
        prev = jnp.where(row % seq == 0, start_ref[...], shifted)
    return x + mu_ref[...] * (prev - x)


def _rwkv_features(xs, w0_ref, ww2_ref, a0_ref, aw2_ref, gw2_ref, kk_ref, ka_ref, rk_ref, ones_ref):
    d = RWKV_DIM
    r = xs[:, 0:d]
    k = xs[:, d:2 * d]
    v = xs[:, 2 * d:3 * d]
    lw = xs[:, 3 * d:3 * d + LORA_W]
    la = xs[:, 3 * d + LORA_W:3 * d + LORA_W + LORA_A]
    lg = xs[:, 3 * d + LORA_W + LORA_A:]
    wpre = w0_ref[...] + _dot_hi(jnp.tanh(lw), ww2_ref[...])
    w = -jax.nn.softplus(-wpre) - 0.5
    log_decay = -jnp.exp(w)
    a = jax.nn.sigmoid(a0_ref[...] + _dot_hi(la, aw2_ref[...]))
    g = _dot_hi(jax.nn.sigmoid(lg), gw2_ref[...])
    kk = k * kk_ref[...]
    kk = kk / jnp.maximum(jnp.sqrt(_head_sum(kk * kk, ones_ref)), 1e-12)
    kh = k * (1.0 + (a - 1.0) * ka_ref[...])
    bonus = _head_sum(r * kh * rk_ref[...], ones_ref) * v
    return r, log_decay, kh, v, kk, kk * a, g, bonus


def _rwkv_prep_kernel(x_ref, pre_ref, start_ref, mu_ref, w0_ref, ww2_ref, a0_ref, aw2_ref, gw2_ref, kk_ref, ka_ref,
                      rk_ref, ones_ref, *outs, seq, tm):
    xs = _token_shift(x_ref, pre_ref, start_ref, mu_ref, seq=seq, tm=tm, tile=pl.program_id(0))
    feats = _rwkv_features(xs, w0_ref, ww2_ref, a0_ref, aw2_ref, gw2_ref, kk_ref, ka_ref, rk_ref, ones_ref)
    for o_ref, val in zip(outs, feats):
        o_ref[...] = val


def _rwkv_prep(x, shift0, p, ones, *, seq, tm=256):
    m = x.shape[0]
    row = lambda i: (i, 0)
    const = lambda i: (0, 0)
    vec = lambda n: pl.BlockSpec((1, n), const)
    out = jax.ShapeDtypeStruct((m, RWKV_DIM), F32)
    if seq >= tm:
        assert seq % tm == 0
        start = shift0
        start_spec = pl.BlockSpec((1, 1, RWKV_IN), lambda i: ((i * tm) // seq, 0, 0))
    else:
        assert tm % seq == 0
        start = jnp.repeat(shift0[:, 0], seq, axis=0)
        start_spec = pl.BlockSpec((tm, RWKV_IN), row)
    pre_blocks = tm // SUBLANES
    return pl.pallas_call(
        functools.partial(_rwkv_prep_kernel, seq=seq, tm=tm),
        out_shape=(out,) * 8,
        grid=(m // tm,),
        in_specs=[
            pl.BlockSpec((tm, RWKV_IN), row),
            pl.BlockSpec((SUBLANES, RWKV_IN), lambda i: (jnp.maximum(i * pre_blocks - 1, 0), 0)),
            start_spec,
            vec(RWKV_IN), vec(RWKV_DIM),
            pl.BlockSpec((LORA_W, RWKV_DIM), const),
            vec(RWKV_DIM),
            pl.BlockSpec((LORA_A, RWKV_DIM), const),
            pl.BlockSpec((LORA_G, RWKV_DIM), const),
            vec(RWKV_DIM), vec(RWKV_DIM), vec(RWKV_DIM),
            pl.BlockSpec(ones.shape, const),
        ],
        out_specs=(pl.BlockSpec((tm, RWKV_DIM), row),) * 8,
        compiler_params=_cparams("parallel"),
        name="rwkv_prep",
    )(x, x, start, p["mu"], p["w0"], p["w_w2"], p["a0"], p["a_w2"], p["g_w2"], p["k_k"], p["k_a"], p["r_k"], ones)


def _rwkv_lanes_kernel(r_ref, lw_ref, k_ref, v_ref, kk_ref, kka_ref, s_ref, y_ref, so_ref, v_scr, y_scr, *, t):
    n = RWKV_HEAD_DIM
    nb = s_ref.shape[-1]
    heads = range(2)

    def token_major(ref, j):
        return ref[pl.ds(j, nb, stride=t), :].T

    for j in range(t):
        v_scr[...] = token_major(v_ref, j)
        r_t, k_t, kka_t = (token_major(ref, j) for ref in (r_ref, k_ref, kka_ref))
        neg_kk_t = -token_major(kk_ref, j)
        w_t = jnp.exp(token_major(lw_ref, j))
        src = s_ref if j == 0 else so_ref

        def value_group(g, carry):
            rows = pl.multiple_of(g * SUBLANES, SUBLANES)
            ys = [[] for _ in heads]
            vg = [v_scr[pl.ds(h * n + rows, SUBLANES), :] for h in heads]
            for i in range(SUBLANES):
                for h in heads:
                    f = slice(h * n, (h + 1) * n)
                    s = src[h, rows + i]
                    sa = jnp.sum(s * neg_kk_t[f], axis=0, keepdims=True)
                    s = s * w_t[f] + sa * kka_t[f] + vg[h][i:i + 1] * k_t[f]
                    so_ref[h, rows + i] = s
                    ys[h].append(jnp.sum(s * r_t[f], axis=0, keepdims=True))
            for h in heads:
                y_scr[pl.ds(h * n + rows, SUBLANES), :] = jnp.concatenate(ys[h], axis=0)
            return carry

        lax.fori_loop(0, n // SUBLANES, value_group, 0)
        y_ref[j] = y_scr[...].T


def _rwkv_lanes(r, lw, k, v, kk, kka, s0, *, t):
    m, d = r.shape
    nb = m // t
    n = RWKV_HEAD_DIM
    assert nb == LANES, "one batch per lane"
    tok = pl.BlockSpec((m, LANES), lambda p: (0, p))
    st = pl.BlockSpec((2, n, n, nb), lambda p: (p, 0, 0, 0))
    return pl.pallas_call(
        functools.partial(_rwkv_lanes_kernel, t=t),
        out_shape=(jax.ShapeDtypeStruct((t, nb, d), F32), jax.ShapeDtypeStruct((RWKV_HEADS, n, n, nb), F32)),
        grid=(RWKV_HEADS // 2,),
        in_specs=[tok] * 6 + [st],
        out_specs=(pl.BlockSpec((t, nb, LANES), lambda p: (0, 0, p)), st),
        scratch_shapes=[pltpu.VMEM((LANES, nb), F32), pltpu.VMEM((LANES, nb), F32)],
        compiler_params=_cparams("parallel"),
        name="rwkv_lanes",
    )(r, lw, k, v, kk, kka, s0)


CHUNK = 64
GROUP_HEADS = 4
GROUP_W = GROUP_HEADS * RWKV_HEAD_DIM
N_GROUPS = RWKV_HEADS // GROUP_HEADS
(MASK_SAME, MASK_STRICT, MASK_INCL, MASK_LEVEL0) = (0, 1, 2, 3)
N_LEVELS = int(math.log2(CHUNK))


def _chunk_masks():
    i = np.arange(GROUP_W)
    same = (i[:, None] // CHUNK) == (i[None, :] // CHUNK)
    masks = [same, same & (i[None, :] < i[:, None]), same & (i[None, :] <= i[:, None])]
    for lvl in range(N_LEVELS):
        m = 1 << lvl
        masks.append(((i[:, None] // (2 * m)) == (i[None, :] // (2 * m))) & ((i[:, None] // m) != (i[None, :] // m))
                     & (i[None, :] < i[:, None]))
    return np.stack(masks).astype(np.float32)


def _rwkv_prompt_kernel(x_ref, pre_ref, start_ref, mu_ref, w0_ref, ww2_ref, a0_ref, aw2_ref, gw2_ref, kk_p_ref,
                        ka_ref, rk_ref, ones_ref, lnw_ref, lnb_ref, st0_ref, tri_ref, eye_ref, mask_ref,
                        *rest, seq, tm, n_tiles, n_riders):
    rider_in, (o_ref, sto_ref), rider_out = rest[:n_riders], rest[n_riders:n_riders + 2], rest[n_riders + 2:-11]
    st_scr, xs_ref, y_ref, r_set, lw_set, k_set, v_set, kk_set, kka_set, g_set, bonus_set = rest[-11:]
    for src, dst in zip(rider_in, rider_out):
        dst[...] = src[...].astype(BF16)
    step = pl.program_id(0)
    cur = (step + 1) % 2
    nxt = step % 2
    sets = (r_set, lw_set, k_set, v_set, kk_set, kka_set, g_set, bonus_set)
    r_ref, lw_ref, k_ref, v_ref, kk_ref, kka_ref = (s.at[cur] for s in sets[:6])

    @pl.when(step == 0)
    def _():
        st_scr[...] = st0_ref[...]
        for s in sets:
            s[1] = jnp.zeros(s.shape[1:], F32)

    xs_ref[...] = _token_shift(x_ref, pre_ref, start_ref, mu_ref, seq=seq, tm=tm, tile=jnp.minimum(step, n_tiles - 1))
    n_chunks = tm // CHUNK

    piece = 2 * CHUNK
    def features(c):
        rows = slice(c * piece, (c + 1) * piece)
        feats = _rwkv_features(xs_ref[rows, :], w0_ref, ww2_ref, a0_ref, aw2_ref, gw2_ref, kk_p_ref, ka_ref, rk_ref,
                               ones_ref)
        for s, val in zip(sets, feats):
            s[nxt, rows, :] = val

    eye = eye_ref[...]
    tri = tri_ref[...]
    tile_rows = lambda x: jnp.concatenate([x] * GROUP_HEADS, axis=0)
    block_diag = lambda x: (tile_rows(x) * mask_ref[MASK_SAME]).astype(BF16)

    chains = [(slice(c * CHUNK, (c + 1) * CHUNK), slice(g * GROUP_W, (g + 1) * GROUP_W))
              for c in range(n_chunks) for g in range(N_GROUPS)]
    each = lambda f, *cols: [f(*args) for args in zip(*cols)]
    same, strict, incl = mask_ref[MASK_SAME], mask_ref[MASK_STRICT], mask_ref[MASK_INCL]

    def cum_decay(lw):
        h1 = lw.astype(BF16)
        r1 = lw - h1.astype(F32)
        h2 = r1.astype(BF16)
        h3 = (r1 - h2.astype(F32)).astype(BF16)
        return _dot(tri, h1) + _dot(tri, h2) + _dot(tri, h3)

    pending = list(range(tm // piece))

    def next_features():
        if pending:
            features(pending.pop(0))

    all_chains = chains
    st = [st_scr[g] for g in range(N_GROUPS)]
    for first in range(0, len(all_chains), len(all_chains) // 2):
        chains = all_chains[first:first + len(all_chains) // 2]
        lw = [lw_ref[rows, sl] for rows, sl in chains]
        kka = [kka_ref[rows, sl] for rows, sl in chains]
        k = [k_ref[rows, sl] for rows, sl in chains]
        cum = each(cum_decay, lw)
        cum_last = each(lambda c: c[CHUNK - 1:CHUNK, :], cum)
        p_inv = each(lambda c: jnp.exp(-c), cum)
        p_tail = each(lambda c, cl: jnp.exp(cl - c), cum, cum_last)
        a_bd = [block_diag(-kk_ref[rows, sl] * jnp.exp(c - l)) for (rows, sl), c, l in zip(chains, cum, lw)]
        r_f = [tile_rows(r_ref[rows, sl] * jnp.exp(c)) * same for (rows, sl), c in zip(chains, cum)]
        r_bd = each(lambda x: x.astype(BF16), r_f)
        v_bd = [block_diag(v_ref[rows, sl]) for rows, sl in chains]
        b_rep = each(lambda x, p: tile_rows((x * p).astype(BF16)), kka, p_inv)
        k_rep = each(lambda x, p: tile_rows((x * p).astype(BF16)), k, p_inv)
        bh_rep = each(lambda x, p: tile_rows(x * p), kka, p_tail)
        kh_rep = each(lambda x, p: tile_rows(x * p), k, p_tail)

        l_ab_f = each(lambda a, b: _dot_nt(a, b) * strict, a_bd, b_rep)
        l_ab = each(lambda x: x.astype(BF16), l_ab_f)
        l_ak = each(lambda a, b: (_dot_nt(a, b) * strict).astype(BF16), a_bd, k_rep)
        m_rb = each(lambda a, b: (_dot_nt(a, b) * incl).astype(BF16), r_bd, b_rep)
        m_rk = each(lambda a, b: (_dot_nt(a, b) * incl).astype(BF16), r_bd, k_rep)
        bh_t = each(lambda x: (x.T * same).astype(BF16), bh_rep)
        kh_t = each(lambda x: (x.T * same).astype(BF16), kh_rep)

        d = each(lambda l: eye + l * mask_ref[MASK_LEVEL0], l_ab_f)
        for lvl in range(1, N_LEVELS):
            d_b = each(lambda x: x.astype(BF16), d)
            x = each(lambda l, db: (_dot(l, db) * mask_ref[MASK_LEVEL0 + lvl]).astype(BF16), l_ab, d_b)
            next_features()
            d = each(lambda dd, db, xx: dd + _dot(db, xx), d, d_b, x)
        t_b = each(lambda x: x.astype(BF16), d)
        while pending:
            next_features()

        wm = each(lambda a, b, vv: _dot(jnp.concatenate([a, b], axis=0), vv), l_ak, m_rk, v_bd)
        twa = each(lambda t, w, a: _dot(t, jnp.concatenate([w[:GROUP_W].astype(BF16), a], axis=1)).astype(BF16),
                   t_b, wm, a_bd)
        ry = each(_dot, m_rb, twa)
        mn = each(_dot, bh_t, twa)
        khv = each(_dot, kh_t, v_bd)
        y0 = each(lambda a, w: a[:, :GROUP_W] + w[GROUP_W:], ry, wm)
        n_x = each(lambda a, b: a[:, :GROUP_W] + b, mn, khv)
        mr = each(lambda a, cl, rf, b: jnp.concatenate(
            [(eye * jnp.exp(cl) + a[:, GROUP_W:]).astype(BF16), (rf + b[:, GROUP_W:]).astype(BF16)], axis=0),
            mn, cum_last, r_f, ry)

        for i, (rows, sl) in enumerate(chains):
            g = (first + i) % N_GROUPS
            ys = _dot(mr[i], st[g].astype(BF16))
            st[g] = ys[:GROUP_W] + n_x[i]
            y_bd = ys[GROUP_W:] + y0[i]
            y = y_bd[0:CHUNK]
            for h in range(1, GROUP_HEADS):
                y = y + y_bd[h * CHUNK:(h + 1) * CHUNK]
            y_ref[rows, sl] = y
    for g in range(N_GROUPS):
        st_scr[g] = st[g]
    o_ref[...] = _rwkv_output(y_ref[...], bonus_set[cur], g_set[cur], lnw_ref, lnb_ref, ones_ref)

    @pl.when(step == pl.num_programs(0) - 1)
    def _():
        sto_ref[...] = st_scr[...]


def _rwkv_prompt(x, shift0, s0, p, ones, *, riders=(), chunks_per_step=4):
    t = x.shape[0]
    d = RWKV_DIM
    n = RWKV_HEAD_DIM
    tt = CHUNK * chunks_per_step
    assert CHUNK == n and t % tt == 0
    st0 = jnp.einsum("ghvk,hj->ghkjv", s0.reshape(N_GROUPS, GROUP_HEADS, n, n), jnp.eye(GROUP_HEADS, dtype=F32))
    st0 = st0.reshape(N_GROUPS, GROUP_W, GROUP_W)
    tri = jnp.asarray(np.tril(np.ones((CHUNK, CHUNK), np.float32)), BF16)
    eye = jnp.eye(GROUP_W, dtype=F32)
    masks = jnp.asarray(_chunk_masks())
    const = lambda c: (0, 0)
    vec = lambda width: pl.BlockSpec((1, width), const)
    st_spec = pl.BlockSpec((N_GROUPS, GROUP_W, GROUP_W), lambda c: (0, 0, 0))
    feature_set = pltpu.VMEM((2, tt, d), F32)
    pre_blocks = tt // SUBLANES
    n_tiles = t // tt
    fill = lambda c: jnp.minimum(c, n_tiles - 1)
    rider_specs = _rider_specs(riders, n_tiles + 1, lambda c: c)
    o, st, *rounded = pl.pallas_call(
        functools.partial(_rwkv_prompt_kernel, seq=t, tm=tt, n_tiles=n_tiles, n_riders=len(riders)),
        out_shape=[jax.ShapeDtypeStruct((t, d), BF16), jax.ShapeDtypeStruct((N_GROUPS, GROUP_W, GROUP_W), F32)]
        + [jax.ShapeDtypeStruct(arr.shape, BF16) for arr in riders],
        grid=(n_tiles + 1,),
        in_specs=[
            pl.BlockSpec((tt, RWKV_IN), lambda c: (fill(c), 0)),
            pl.BlockSpec((SUBLANES, RWKV_IN), lambda c: (jnp.maximum(fill(c) * pre_blocks - 1, 0), 0)),
            pl.BlockSpec((1, 1, RWKV_IN), lambda c: (0, 0, 0)),
            vec(RWKV_IN), vec(d),
            pl.BlockSpec((LORA_W, d), const),
            vec(d),
            pl.BlockSpec((LORA_A, d), const),
            pl.BlockSpec((LORA_G, d), const),
            vec(d), vec(d), vec(d),
            pl.BlockSpec(ones.shape, const),
            vec(d), vec(d),
            st_spec,
            pl.BlockSpec((CHUNK, CHUNK), const),
            pl.BlockSpec((GROUP_W, GROUP_W), const),
            pl.BlockSpec(masks.shape, lambda c: (0, 0, 0)),
        ] + rider_specs,
        out_specs=[pl.BlockSpec((tt, d), lambda c: (jnp.maximum(c - 1, 0), 0)), st_spec] + rider_specs,
        scratch_shapes=[pltpu.VMEM((N_GROUPS, GROUP_W, GROUP_W), F32), pltpu.VMEM((tt, RWKV_IN), F32),
                        pltpu.VMEM((tt, d), F32)] + [feature_set] * 8,
        compiler_params=_cparams("arbitrary"),
        name="rwkv_prompt",
    )(x, x, shift0, p["mu"], p["w0"], p["w_w2"], p["a0"], p["a_w2"], p["g_w2"], p["k_k"], p["k_a"], p["r_k"], ones,
      p["ln_w"], p["ln_b"], st0, tri, eye, masks, *riders)
    st5 = st.reshape(N_GROUPS, GROUP_HEADS, n, GROUP_HEADS, n)
    s_new = jnp.einsum("ghkjv,hj->ghvk", st5, jnp.eye(GROUP_HEADS, dtype=F32)).reshape(RWKV_HEADS, n, n)
    return o, s_new, rounded


def _rwkv_output(y, bonus, gate, lnw_ref, lnb_ref, ones_ref):
    inv_n = 1.0 / RWKV_HEAD_DIM
    mu = _head_sum(y, ones_ref) * inv_n
    dlt = y - mu
    var = _head_sum(dlt * dlt, ones_ref) * inv_n
    yn = dlt * lax.rsqrt(var + GN_EPS) * lnw_ref[...] + lnb_ref[...]
    return ((yn + bonus) * gate).astype(BF16)


def _rwkv_post_kernel(y_ref, bonus_ref, g_ref, lnw_ref, lnb_ref, ones_ref, o_ref):
    o_ref[...] = _rwkv_output(y_ref[...], bonus_ref[...], g_ref[...], lnw_ref, lnb_ref, ones_ref)


def _rwkv_post(y, bonus, g, lnw, lnb, ones, *, tm=256):
    m = y.shape[0]
    row = lambda i: (i, 0)
    const = lambda i: (0, 0)
    tile = pl.BlockSpec((tm, RWKV_DIM), row)
    return pl.pallas_call(
        _rwkv_post_kernel,
        out_shape=jax.ShapeDtypeStruct((m, RWKV_DIM), BF16),
        grid=(m // tm,),
        in_specs=[tile, tile, tile, pl.BlockSpec((1, RWKV_DIM), const), pl.BlockSpec((1, RWKV_DIM), const),
                  pl.BlockSpec(ones.shape, const)],
        out_specs=tile,
        compiler_params=_cparams("parallel"),
        name="rwkv_post",
    )(y, bonus, g, lnw, lnb, ones)


GATE_BLOCK = math.gcd(PROJ_DIM, D_MODEL)


def _merge_kernel(h_ref, u_ref, os_ref, or_ref, om_ref, wos_ref, wor_ref, wom_ref, wout_ref, *rest, parts):
    gate_refs, (o_ref, acc_ref) = rest[:N_BRANCH * parts], rest[N_BRANCH * parts:]
    j = pl.program_id(1)

    @pl.when(j == 0)
    def _():
        acc_ref[...] = jnp.zeros_like(acc_ref)

    u = u_ref[...]
    merged = None
    for b, (x_ref, w_ref) in enumerate(((os_ref, wos_ref), (or_ref, wor_ref), (om_ref, wom_ref))):
        gate = jnp.concatenate([_dot(u, gate_refs[b * parts + c][...]) for c in range(parts)], axis=1)
        term = jax.nn.sigmoid(gate) * _dot(x_ref[...], w_ref[...])
        merged = term if merged is None else merged + term
    acc_ref[...] += _dot(merged.astype(BF16), wout_ref[...])

    @pl.when(j == pl.num_programs(1) - 1)
    def _():
        o_ref[...] = h_ref[...] + acc_ref[...]


def _merge(h, u, o_swa, o_rw, o_mem, w_in, wo_swa, wo_rw, wo_mem, w_out, *, tm=512, tn=512):
    m, d = h.shape
    nt = d // tn
    parts = tn // GATE_BLOCK
    g0 = PROJ_DIM // GATE_BLOCK
    row = lambda i, j: (i, 0)
    col = lambda i, j: (0, j)
    gate_specs = [pl.BlockSpec((d, GATE_BLOCK), functools.partial(
        lambda i, j, off: (0, off + j * parts), off=g0 + b * (d // GATE_BLOCK) + c))
        for b in range(N_BRANCH) for c in range(parts)]
    return pl.pallas_call(
        functools.partial(_merge_kernel, parts=parts),
        out_shape=jax.ShapeDtypeStruct((m, d), F32),
        grid=(m // tm, nt),
        in_specs=[
            pl.BlockSpec((tm, d), row),
            pl.BlockSpec((tm, d), row),
            pl.BlockSpec((tm, SWA_Q_DIM), row),
            pl.BlockSpec((tm, RWKV_DIM), row),
            pl.BlockSpec((tm, MEM_DIM), row),
            pl.BlockSpec((SWA_Q_DIM, tn), col),
            pl.BlockSpec((RWKV_DIM, tn), col),
            pl.BlockSpec((MEM_DIM, tn), col),
            pl.BlockSpec((tn, d), lambda i, j: (j, 0)),
        ] + gate_specs,
        out_specs=pl.BlockSpec((tm, d), row),
        scratch_shapes=[pltpu.VMEM((tm, d), F32)],
        compiler_params=_cparams("parallel", "arbitrary"),
        name="merge",
    )(h, u, o_swa, o_rw, o_mem, wo_swa, wo_rw, wo_mem, w_out, *([w_in] * (N_BRANCH * parts)))


def _t5_bucket(dist):
    max_exact = N_BUCKETS // 2
    d = np.maximum(dist, 0)
    log_ratio = (np.log(np.maximum(d, 1).astype(np.float32) / np.float32(max_exact))
                 / np.float32(math.log(MAX_DISTANCE / max_exact)))
    large = np.minimum(max_exact + (log_ratio * (N_BUCKETS - max_exact)).astype(np.int32), N_BUCKETS - 1)
    return np.where(d < max_exact, d, large).astype(np.int32)


def _rel_bias(table, dist):
    onehot = np.eye(N_BUCKETS, dtype=np.float32)[_t5_bucket(dist).reshape(-1)]
    bias = jnp.einsum("nb,bh->hn", jnp.asarray(onehot), table, precision=lax.Precision.HIGHEST)
    return bias.reshape(SWA_HEADS, *dist.shape)


def _rwkv_branch(xr, shift0, s0, p, ones, riders=()):
    b, t, _ = xr.shape
    flat = lambda z: z.reshape(b * t, z.shape[-1])
    if b == 1:
        o, s_new, rounded = _rwkv_prompt(flat(xr), shift0, s0[0], p, ones, riders=riders)
        return o, s_new[None], rounded
    assert b == LANES and not riders, "short sequences are batched one per lane"
    r, w, k, v, kk, kka, g, bonus = _rwkv_prep(flat(xr), shift0, p, ones, seq=t)
    y, s_new = _rwkv_lanes(r, w, k, v, kk, kka, jnp.transpose(s0, (1, 2, 3, 0)), t=t)
    o = _rwkv_post(flat(jnp.transpose(y, (1, 0, 2))), bonus, g, p["ln_w"], p["ln_b"], ones)
    return o, jnp.transpose(s_new, (3, 0, 1, 2))


def kernel(x_prompt, mem_prompt, x_sample, cache_swa_k, cache_swa_v, state_rwkv, state_rwkv_shift, cache_mem_k, cache_mem_v, ffn1_norm, ffn1_wi, ffn1_wo, mix_norm, w_in, swa_sinks, rel_bias_table, rwkv_mu, rwkv_w0, rwkv_w_w2, rwkv_a0, rwkv_a_w2, rwkv_g_w2, rwkv_k_k, rwkv_k_a, rwkv_r_k, rwkv_ln_w, rwkv_ln_b, mem_norm, w_mem_kv, w_o_swa, w_o_rwkv, w_o_mem, w_out, ffn2_norm, ffn2_wi, ffn2_wo, final_norm):
    assert ffn1_wi.shape[0] == 1, "single-layer trunk"
    bp, tp, d = x_prompt.shape
    bs, ts, _ = x_sample.shape
    assert bp == 1
    row = lambda z: z.reshape(1, -1).astype(F32)

    g1, gm, g2, gf = row(ffn1_norm[0]), row(mix_norm[0]), row(ffn2_norm[0]), row(final_norm)
    rp = {
        "mu": row(rwkv_mu[0]), "w0": row(rwkv_w0[0]), "w_w2": rwkv_w_w2[0], "a0": row(rwkv_a0[0]),
        "a_w2": rwkv_a_w2[0], "g_w2": rwkv_g_w2[0], "k_k": row(rwkv_k_k[0]), "k_a": row(rwkv_k_a[0]),
        "r_k": row(rwkv_r_k[0]), "ln_w": row(rwkv_ln_w[0]), "ln_b": row(rwkv_ln_b[0]),
    }
    seg = np.arange(GROUP_W) // RWKV_HEAD_DIM
    ones = jnp.asarray(seg[:, None] == seg[None, :], dtype=BF16)
    sinks = swa_sinks[0].astype(F32)
    table = rel_bias_table.astype(F32)

    xp = x_prompt.reshape(tp, d)
    xs = x_sample.reshape(bs * ts, d)
    hs, wg1, wu1, wo1 = _ffn(xs, g1, ffn1_wi[0], ffn1_wo[0], gf, final_norm=False, emit_weights=True, tf=256)
    hp, w_in_b = _ffn(xp, g1, (wg1, wu1), wo1, gf, final_norm=False, riders=(w_in[0],))
    w_q = w_in_b[:, :SWA_Q_DIM].reshape(d, SWA_KV_HEADS, SWA_GROUP, SWA_HEAD_DIM).transpose(0, 2, 1, 3).reshape(d, SWA_Q_DIM)
    qp, kvp, xrp, qmp, up = _inproj(hp, gm, w_q, w_in_b)
    qs, kvs, xrs, qms, us = _inproj(hs, gm, w_q, w_in_b)

    w = WINDOW
    dist_p = np.arange(w)[:, None] + w - np.arange(2 * w)[None, :]
    bias_p = _rel_bias(table, dist_p).reshape(SWA_KV_HEADS, SWA_GROUP, w, 2 * w).transpose(1, 0, 2, 3)
    bias_p = bias_p.reshape(SWA_GROUP, SWA_KV_HEADS * w, 2 * w)
    sink_p = jnp.repeat(sinks.reshape(SWA_KV_HEADS, SWA_GROUP).T, w, axis=1).reshape(SWA_GROUP, SWA_KV_HEADS * w, 1)
    o_swa_p = _swa_prompt(qp, kvp, bias_p, sink_p)

    wbuf = cache_swa_k.shape[2]
    dist_s = np.arange(ts)[:, None] + wbuf - np.arange(wbuf + ts)[None, :]
    bias_s = _rel_bias(table, dist_s).reshape(SWA_HEADS * ts, wbuf + ts)
    sink_rows = jnp.repeat(sinks, ts).reshape(SWA_HEADS * ts, 1)
    qs_gt = qs.reshape(bs, ts, SWA_GROUP, SWA_KV_DIM).transpose(0, 2, 1, 3).reshape(bs, SWA_GROUP * ts, SWA_KV_DIM)
    kbuf = cache_swa_k[0].reshape(bs, wbuf, SWA_KV_DIM).transpose(0, 2, 1)
    vbuf = cache_swa_v[0].reshape(bs, wbuf, SWA_KV_DIM).transpose(0, 2, 1)
    o_swa_s, knew_t, vnew_t = _swa_sample(qs_gt, kvs.reshape(bs, ts, 2 * SWA_KV_DIM), kbuf, vbuf,
                                          bias_s[:, :wbuf], bias_s[:, wbuf:], sink_rows)
    o_swa_s = o_swa_s.reshape(bs, SWA_GROUP, ts, SWA_KV_DIM).transpose(0, 2, 1, 3).reshape(bs * ts, SWA_Q_DIM)

    zero_shift = jnp.zeros((bp, 1, RWKV_IN), F32)
    zero_state = jnp.zeros((bp, RWKV_HEADS, RWKV_HEAD_DIM, RWKV_HEAD_DIM), F32)
    o_rw_p, state_p, (wi2, wo2, wo_swa, wo_rw, wo_mem, w_out_b) = _rwkv_branch(
        xrp.reshape(bp, tp, RWKV_IN), zero_shift, zero_state, rp, ones,
        riders=(ffn2_wi[0], ffn2_wo[0], w_o_swa[0], w_o_rwkv[0], w_o_mem[0], w_out[0]))
    wo_swa = wo_swa.reshape(SWA_KV_HEADS, SWA_GROUP, SWA_HEAD_DIM, d).transpose(1, 0, 2, 3).reshape(SWA_Q_DIM, d)
    o_rw_s, state_s = _rwkv_branch(xrs.reshape(bs, ts, RWKV_IN), state_rwkv_shift[0], state_rwkv[0], rp, ones)

    mkv = _norm_matmul(mem_prompt.reshape(N_MEM, d), row(mem_norm[0]), w_mem_kv[0].astype(BF16))
    o_mem_p = _mem_prompt(qmp, mkv)
    o_mem_s = _mem_sample(qms.reshape(bs, ts, MEM_DIM), cache_mem_k[0].reshape(bs, N_MEM * MEM_HEADS, MEM_HEAD_DIM),
                          cache_mem_v[0].reshape(bs, N_MEM * MEM_HEADS, MEM_HEAD_DIM)).reshape(bs * ts, MEM_DIM)

    hp = _merge(hp, up, o_swa_p, o_rw_p, o_mem_p, w_in_b, wo_swa, wo_rw, wo_mem, w_out_b)
    hs = _merge(hs, us, o_swa_s, o_rw_s, o_mem_s, w_in_b, wo_swa, wo_rw, wo_mem, w_out_b)
    y_prompt = _ffn(hp, g2, wi2, wo2, gf, final_norm=True).reshape(bp, tp, d)
    y_sample = _ffn(hs, g2, wi2, wo2, gf, final_norm=True).reshape(bs, ts, d)

    wp = min(w, tp)
    p_k = kvp[tp - wp:, :SWA_KV_DIM].reshape(1, bp, wp, SWA_KV_HEADS, SWA_HEAD_DIM)
    p_v = kvp[tp - wp:, SWA_KV_DIM:].reshape(1, bp, wp, SWA_KV_HEADS, SWA_HEAD_DIM)
    p_mk = mkv[:, :MEM_DIM].reshape(1, bp, N_MEM, MEM_HEADS, MEM_HEAD_DIM)
    p_mv = mkv[:, MEM_DIM:].reshape(1, bp, N_MEM, MEM_HEADS, MEM_HEAD_DIM)
    s_k = knew_t.transpose(0, 2, 1).reshape(1, bs, wbuf, SWA_KV_HEADS, SWA_HEAD_DIM)
    s_v = vnew_t.transpose(0, 2, 1).reshape(1, bs, wbuf, SWA_KV_HEADS, SWA_HEAD_DIM)
    return (y_prompt, y_sample,
            p_k, p_v, state_p[None], xrp[tp - 1:].reshape(1, bp, 1, RWKV_IN), p_mk, p_mv,
            s_k, s_v, state_s[None], xrs.reshape(bs, ts, RWKV_IN)[:, ts - 1:][None])
```

```python
import functools
import math

import jax
import jax.numpy as jnp
import numpy as np
from jax import lax
from jax.experimental import pallas as pl
from jax.experimental.pallas import tpu as pltpu

F32 = jnp.float32
BF16 = jnp.bfloat16

D_MODEL = 2048
SWA_HEADS = 16
SWA_KV_HEADS = 4
SWA_GROUP = SWA_HEADS // SWA_KV_HEADS
SWA_HEAD_DIM = 64
SWA_Q_DIM = SWA_HEADS * SWA_HEAD_DIM
SWA_KV_DIM = SWA_KV_HEADS * SWA_HEAD_DIM
WINDOW = 128
N_BUCKETS = 32
MAX_DISTANCE = 128
RWKV_HEADS = 8
RWKV_HEAD_DIM = 64
RWKV_DIM = RWKV_HEADS * RWKV_HEAD_DIM
LORA_W = 64
LORA_A = 64
LORA_G = 128
RWKV_IN = 3 * RWKV_DIM + LORA_W + LORA_A + LORA_G
N_MEM = 256
MEM_HEADS = 4
MEM_HEAD_DIM = 128
MEM_DIM = MEM_HEADS * MEM_HEAD_DIM
N_BRANCH = 3
PROJ_DIM = SWA_Q_DIM + 2 * SWA_KV_DIM + RWKV_IN + MEM_DIM
NORM_EPS = 1e-6
GN_EPS = 64e-5
NEG_INF = -1e30

LANES = 128
SUBLANES = 8
VMEM_LIMIT = 56 * 1024 * 1024


def _cparams(*sem):
    return pltpu.CompilerParams(dimension_semantics=sem, vmem_limit_bytes=VMEM_LIMIT)


def _rms(x, g):
    return x * lax.rsqrt(jnp.mean(x * x, axis=-1, keepdims=True) + NORM_EPS) * g


def _dot(a, b):
    return jnp.dot(a, b, preferred_element_type=F32)


def _dot_nt(a, b):
    return lax.dot_general(a, b, (((1,), (1,)), ((), ())), preferred_element_type=F32)


def _dot_hi(a, b):
    a_hi = a.astype(BF16)
    b_hi = b.astype(BF16)
    a_lo = (a - a_hi.astype(F32)).astype(BF16)
    b_lo = (b - b_hi.astype(F32)).astype(BF16)
    return _dot(a_hi, b_hi) + _dot(a_hi, b_lo) + _dot(a_lo, b_hi)


def _ffn_kernel(x_ref, g_ref, wg_ref, wu_ref, wo_ref, gf_ref, *rest, final_norm, n_riders, emit_weights):
    rider_in, (o_ref, *extra_out), (xn_ref, acc_ref) = rest[:n_riders], rest[n_riders:-2], rest[-2:]
    for src, dst in zip(rider_in, extra_out[:n_riders]):
        dst[...] = src[...].astype(BF16)
    j = pl.program_id(1)

    @pl.when(j == 0)
    def _():
        xn_ref[...] = _rms(x_ref[...], g_ref[...]).astype(BF16)
        acc_ref[...] = jnp.zeros_like(acc_ref)

    wg, wu, wo = wg_ref[...], wu_ref[...], wo_ref[...]
    if emit_weights:
        wg, wu, wo = wg.astype(BF16), wu.astype(BF16), wo.astype(BF16)
        for dst, val in zip(extra_out[n_riders:], (wg, wu, wo)):
            dst[...] = val
    xn = xn_ref[...]
    gate = _dot(xn, wg)
    up = _dot(xn, wu)
    act = (gate * jax.nn.sigmoid(gate)) * up
    acc_ref[...] += _dot(act.astype(BF16), wo)

    @pl.when(j == pl.num_programs(1) - 1)
    def _():
        h = x_ref[...] + 0.5 * acc_ref[...]
        if final_norm:
            h = _rms(h, gf_ref[...])
        o_ref[...] = h


def _rider_tiling(shape, steps):
    rows, cols = shape
    best, best_score = (1, 1), (0, 0)
    for nr in range(1, rows // 16 + 1):
        if rows % nr or (rows // nr) % 16:
            continue
        for nc in range(1, cols // LANES + 1):
            if cols % nc or (cols // nc) % LANES or nr * nc > steps:
                continue
            score = (min(cols // nc, 1024), nr * nc)
            if score > best_score:
                best, best_score = (nr, nc), score
    return best


def _rider_specs(riders, steps, step_of):
    specs = []
    for arr in riders:
        nr, nc = _rider_tiling(arr.shape, steps)
        specs.append(pl.BlockSpec((arr.shape[0] // nr, arr.shape[1] // nc), functools.partial(
            lambda *idx, nc, last: (jnp.minimum(step_of(*idx), last) // nc, jnp.minimum(step_of(*idx), last) % nc),
            nc=nc, last=nr * nc - 1)))
    return specs


def _ffn(x, g, wi, wo, gf, *, final_norm, riders=(), emit_weights=False, tm=512, tf=512):
    m, d = x.shape
    dff = wo.shape[0]
    nf = dff // tf
    steps = (m // tm) * nf
    wg, wu = wi if isinstance(wi, tuple) else (wi, wi)
    up_off = 0 if isinstance(wi, tuple) else nf
    assert not emit_weights or m == tm
    weight_specs = [pl.BlockSpec((d, tf), lambda i, j: (0, j)), pl.BlockSpec((d, tf), lambda i, j: (0, j)),
                    pl.BlockSpec((tf, d), lambda i, j: (j, 0))]
    weight_out = [jax.ShapeDtypeStruct((d, dff), BF16)] * 2 + [jax.ShapeDtypeStruct((dff, d), BF16)]
    rider_specs = _rider_specs(riders, steps, lambda i, j: i * nf + j)
    out = pl.pallas_call(
        functools.partial(_ffn_kernel, final_norm=final_norm, n_riders=len(riders), emit_weights=emit_weights),
        out_shape=([jax.ShapeDtypeStruct((m, d), F32)] + [jax.ShapeDtypeStruct(arr.shape, BF16) for arr in riders]
                   + (weight_out if emit_weights else [])),
        grid=(m // tm, nf),
        in_specs=[
            pl.BlockSpec((tm, d), lambda i, j: (i, 0)),
            pl.BlockSpec((1, d), lambda i, j: (0, 0)),
            pl.BlockSpec((d, tf), lambda i, j: (0, j)),
            pl.BlockSpec((d, tf), lambda i, j: (0, j + up_off)),
            pl.BlockSpec((tf, d), lambda i, j: (j, 0)),
            pl.BlockSpec((1, d), lambda i, j: (0, 0)),
        ] + rider_specs,
        out_specs=([pl.BlockSpec((tm, d), lambda i, j: (i, 0))] + rider_specs
                   + (weight_specs if emit_weights else [])),
        scratch_shapes=[pltpu.VMEM((tm, d), BF16), pltpu.VMEM((tm, d), F32)],
        compiler_params=_cparams("arbitrary", "arbitrary"),
        name="ffn_final" if final_norm else "ffn",
    )(x, g, wg, wu, wo, gf, *riders)
    return out if (riders or emit_weights) else out[0]


def _inproj_kernel(h_ref, g_ref, wq_ref, w_ref, q_ref, kv_ref, xr_ref, qm_ref, u_ref):
    u = _rms(h_ref[...], g_ref[...]).astype(BF16)
    u_ref[...] = u
    c0, c1, c2 = SWA_Q_DIM, SWA_Q_DIM + 2 * SWA_KV_DIM, SWA_Q_DIM + 2 * SWA_KV_DIM + RWKV_IN
    q_ref[...] = _dot(u, wq_ref[...]).astype(BF16)
    kv_ref[...] = _dot(u, w_ref[:, c0:c1])
    xr_ref[...] = _dot(u, w_ref[:, c1:c2])
    qm_ref[...] = _dot(u, w_ref[:, c2:PROJ_DIM]).astype(BF16)


def _inproj(h, g, wq, w, *, tm=512):
    m, d = h.shape
    row = lambda i: (i, 0)
    return pl.pallas_call(
        _inproj_kernel,
        out_shape=(
            jax.ShapeDtypeStruct((m, SWA_Q_DIM), BF16),
            jax.ShapeDtypeStruct((m, 2 * SWA_KV_DIM), F32),
            jax.ShapeDtypeStruct((m, RWKV_IN), F32),
            jax.ShapeDtypeStruct((m, MEM_DIM), BF16),
            jax.ShapeDtypeStruct((m, d), BF16),
        ),
        grid=(m // tm,),
        in_specs=[
            pl.BlockSpec((tm, d), row),
            pl.BlockSpec((1, d), lambda i: (0, 0)),
            pl.BlockSpec((d, SWA_Q_DIM), lambda i: (0, 0), pipeline_mode=pl.Buffered(1)),
            pl.BlockSpec((d, PROJ_DIM), lambda i: (0, 0), pipeline_mode=pl.Buffered(1)),
        ],
        out_specs=(
            pl.BlockSpec((tm, SWA_Q_DIM), row),
            pl.BlockSpec((tm, 2 * SWA_KV_DIM), row),
            pl.BlockSpec((tm, RWKV_IN), row),
            pl.BlockSpec((tm, MEM_DIM), row),
            pl.BlockSpec((tm, d), row),
        ),
        compiler_params=_cparams("parallel"),
        name="inproj",
    )(h, g, wq, w)


def _norm_matmul_kernel(x_ref, g_ref, w_ref, o_ref):
    o_ref[...] = _dot(_rms(x_ref[...], g_ref[...]).astype(BF16), w_ref[...])


def _norm_matmul(x, g, w, *, tn=512):
    m, d = x.shape
    n = w.shape[1]
    return pl.pallas_call(
        _norm_matmul_kernel,
        out_shape=jax.ShapeDtypeStruct((m, n), F32),
        grid=(n // tn,),
        in_specs=[
            pl.BlockSpec((m, d), lambda j: (0, 0)),
            pl.BlockSpec((1, d), lambda j: (0, 0)),
            pl.BlockSpec((d, tn), lambda j: (0, j)),
        ],
        out_specs=pl.BlockSpec((m, tn), lambda j: (0, j)),
        compiler_params=_cparams("parallel"),
        name="norm_matmul",
    )(x, g, w)


def _swa_prompt_kernel(q_ref, kvc_ref, kvp_ref, bias_ref, sink_ref, o_ref, *, nq):
    i = pl.program_id(0)
    w = WINDOW
    rows = SWA_KV_HEADS * w
    kv_blocks = [kvp_ref[...]] + [kvc_ref[s * w:(s + 1) * w, :] for s in range(nq)]
    k_blocks = [x[:, 0:SWA_KV_DIM].astype(BF16) for x in kv_blocks]
    v_blocks = [x[:, SWA_KV_DIM:].astype(BF16) for x in kv_blocks]
    qpos = lax.broadcasted_iota(jnp.int32, (rows, 2 * w), 0) % w
    col = lax.broadcasted_iota(jnp.int32, (rows, 2 * w), 1)
    dist = qpos + w - col
    in_window = (dist >= 0) & (dist < w)
    first = in_window & ((col >= w) | (i > 0))
    lane_head = lax.broadcasted_iota(jnp.int32, (w, SWA_KV_DIM), 1) // SWA_HEAD_DIM
    scale = SWA_HEAD_DIM ** -0.5
    pairs = [(s, g) for s in range(nq) for g in range(SWA_GROUP)]
    k = [jnp.concatenate(k_blocks[s:s + 2], axis=0) for s in range(nq)]
    v = [jnp.concatenate(v_blocks[s:s + 2], axis=0) for s in range(nq)]
    logits = []
    for s, g in pairs:
        qg = q_ref[s * w:(s + 1) * w, g * SWA_KV_DIM:(g + 1) * SWA_KV_DIM].astype(F32) * scale
        qs = jnp.concatenate([jnp.where(lane_head == kvh, qg, 0.0) for kvh in range(SWA_KV_HEADS)], axis=0)
        lg = _dot_nt(qs.astype(BF16), k[s])
        logits.append(jnp.where(first if s == 0 else in_window, lg + bias_ref[g], NEG_INF))
    sink = [sink_ref[g] for _, g in pairs]
    m = [jnp.maximum(jnp.max(x, axis=-1, keepdims=True), sk) for x, sk in zip(logits, sink)]
    p = [jnp.exp(x - mi) for x, mi in zip(logits, m)]
    inv = [1.0 / (jnp.sum(pi, axis=-1, keepdims=True) + jnp.exp(sk - mi)) for pi, sk, mi in zip(p, sink, m)]
    ov = [_dot((pi * ii).astype(BF16), v[s]) for (s, _), pi, ii in zip(pairs, p, inv)]
    for (s, g), o in zip(pairs, ov):
        og = jnp.zeros((w, SWA_KV_DIM), F32)
        for kvh in range(SWA_KV_HEADS):
            og = jnp.where(lane_head == kvh, o[kvh * w:(kvh + 1) * w], og)
        o_ref[s * w:(s + 1) * w, g * SWA_KV_DIM:(g + 1) * SWA_KV_DIM] = og.astype(BF16)


def _swa_prompt(q, kv, bias, sink_rows, *, nq=8):
    t = q.shape[0]
    w = WINDOW
    rows = SWA_KV_HEADS * w
    return pl.pallas_call(
        functools.partial(_swa_prompt_kernel, nq=nq),
        out_shape=jax.ShapeDtypeStruct((t, SWA_Q_DIM), BF16),
        grid=(t // (nq * w),),
        in_specs=[
            pl.BlockSpec((nq * w, SWA_Q_DIM), lambda i: (i, 0)),
            pl.BlockSpec((nq * w, 2 * SWA_KV_DIM), lambda i: (i, 0)),
            pl.BlockSpec((w, 2 * SWA_KV_DIM), lambda i: (jnp.maximum(i * nq - 1, 0), 0)),
            pl.BlockSpec((SWA_GROUP, rows, 2 * w), lambda i: (0, 0, 0)),
            pl.BlockSpec((SWA_GROUP, rows, 1), lambda i: (0, 0, 0)),
        ],
        out_specs=pl.BlockSpec((nq * w, SWA_Q_DIM), lambda i: (i, 0)),
        compiler_params=_cparams("parallel"),
        name="swa_prompt",
    )(q, kv, kv, bias, sink_rows)


def _swa_sample_kernel(q_ref, kvn_ref, kb_ref, vb_ref, bias_b_ref, bias_n_ref, sink_ref, o_ref, ko_ref, vo_ref,
                       *, bb, t):
    gt = SWA_GROUP * t
    rows = SWA_KV_HEADS * gt
    w = kb_ref.shape[2]
    scale = SWA_HEAD_DIM ** -0.5
    lane_head = lax.broadcasted_iota(jnp.int32, (gt, SWA_KV_DIM), 1) // SWA_HEAD_DIM
    tok = lax.broadcasted_iota(jnp.int32, (rows, w), 0) % t
    keyj = lax.broadcasted_iota(jnp.int32, (rows, w), 1)
    valid_b = (tok + w - keyj) < WINDOW
    tok_n = lax.broadcasted_iota(jnp.int32, (rows, 1), 0) % t
    sink = sink_ref[...]
    bs = range(bb)
    toks = range(t)
    qall = [jnp.concatenate([jnp.where(lane_head == kvh, q_ref[b].astype(F32), 0.0) for kvh in range(SWA_KV_HEADS)],
                            axis=0) for b in bs]
    kvn = [kvn_ref[b] for b in bs]
    lb = [_dot(qall[b].astype(BF16), kb_ref[b].astype(BF16)) for b in bs]
    lb = [jnp.where(valid_b, lb[b] * scale + bias_b_ref[...], NEG_INF) for b in bs]
    ln = [[jnp.sum(qall[b] * kvn[b][j:j + 1, 0:SWA_KV_DIM], axis=-1, keepdims=True) for j in toks] for b in bs]
    ln = [[jnp.where(tok_n >= j, ln[b][j] * scale + bias_n_ref[:, j:j + 1], NEG_INF) for j in toks] for b in bs]
    m = [jnp.maximum(jnp.max(lb[b], axis=-1, keepdims=True), sink) for b in bs]
    m = [functools.reduce(jnp.maximum, ln[b], m[b]) for b in bs]
    pb = [jnp.exp(lb[b] - m[b]) for b in bs]
    pn = [[jnp.exp(ln[b][j] - m[b]) for j in toks] for b in bs]
    denom = [jnp.sum(pb[b], axis=-1, keepdims=True) + jnp.exp(sink - m[b]) for b in bs]
    inv = [1.0 / functools.reduce(jnp.add, pn[b], denom[b]) for b in bs]
    oall = [_dot_nt((pb[b] * inv[b]).astype(BF16), vb_ref[b].astype(BF16)) for b in bs]
    for b in bs:
        ob = oall[b]
        for j in toks:
            ob = ob + (pn[b][j] * inv[b]) * kvn[b][j:j + 1, SWA_KV_DIM:]
        og = jnp.zeros((gt, SWA_KV_DIM), F32)
        for kvh in range(SWA_KV_HEADS):
            og = jnp.where(lane_head == kvh, ob[kvh * gt:(kvh + 1) * gt], og)
        o_ref[b] = og.astype(BF16)
    pos = lax.broadcasted_iota(jnp.int32, (SWA_KV_DIM, w), 1)
    pad = jnp.zeros((w - SUBLANES, 2 * SWA_KV_DIM), F32)
    row8 = lax.broadcasted_iota(jnp.int32, (SUBLANES, 2 * SWA_KV_DIM), 0)
    for b in bs:
        last8 = jnp.zeros((SUBLANES, 2 * SWA_KV_DIM), F32)
        for j in toks:
            last8 = jnp.where(row8 == SUBLANES - t + j, kvn[b][j:j + 1], last8)
        tail_t = jnp.concatenate([pad, last8], axis=0).T
        ko_ref[b] = jnp.where(pos >= w - t, tail_t[:SWA_KV_DIM], pltpu.roll(kb_ref[b], w - t, axis=1))
        vo_ref[b] = jnp.where(pos >= w - t, tail_t[SWA_KV_DIM:], pltpu.roll(vb_ref[b], w - t, axis=1))


def _swa_sample(q, kvn, kbuf, vbuf, bias_b, bias_n, sink_rows, *, bb=8):
    b, gt, _ = q.shape
    t = kvn.shape[1]
    w = kbuf.shape[2]
    rows = SWA_KV_HEADS * gt
    blk = lambda i: (i, 0, 0)
    const = lambda i: (0, 0)
    cache = pl.BlockSpec((bb, SWA_KV_DIM, w), blk)
    return pl.pallas_call(
        functools.partial(_swa_sample_kernel, bb=bb, t=t),
        out_shape=(jax.ShapeDtypeStruct((b, gt, SWA_KV_DIM), BF16),
                   jax.ShapeDtypeStruct(kbuf.shape, F32), jax.ShapeDtypeStruct(vbuf.shape, F32)),
        grid=(b // bb,),
        in_specs=[
            pl.BlockSpec((bb, gt, SWA_KV_DIM), blk),
            pl.BlockSpec((bb, t, 2 * SWA_KV_DIM), blk),
            cache, cache,
            pl.BlockSpec((rows, w), const),
            pl.BlockSpec((rows, t), const),
            pl.BlockSpec((rows, 1), const),
        ],
        out_specs=(pl.BlockSpec((bb, gt, SWA_KV_DIM), blk), cache, cache),
        compiler_params=_cparams("parallel"),
        name="swa_sample",
    )(q, kvn, kbuf, vbuf, bias_b, bias_n, sink_rows)


def _mem_heads(q, mk, mv):
    scale = MEM_HEAD_DIM ** -0.5
    cols = [slice(h * MEM_HEAD_DIM, (h + 1) * MEM_HEAD_DIM) for h in range(MEM_HEADS)]
    x = [_dot_nt(q[:, sl], mk[:, sl]) * scale for sl in cols]
    m = [jnp.max(xi, axis=-1, keepdims=True) for xi in x]
    p = [jnp.exp(xi - mi) for xi, mi in zip(x, m)]
    inv = [1.0 / jnp.sum(pi, axis=-1, keepdims=True) for pi in p]
    return jnp.concatenate([_dot((pi * ii).astype(BF16), mv[:, sl]) for pi, ii, sl in zip(p, inv, cols)], axis=-1)


def _mem_prompt_kernel(q_ref, mk_ref, mv_ref, o_ref):
    o_ref[...] = _mem_heads(q_ref[...], mk_ref[...].astype(BF16), mv_ref[...].astype(BF16)).astype(BF16)


def _mem_prompt(q, mkv, *, tm=512):
    m = q.shape[0]
    return pl.pallas_call(
        _mem_prompt_kernel,
        out_shape=jax.ShapeDtypeStruct((m, MEM_DIM), BF16),
        grid=(m // tm,),
        in_specs=[
            pl.BlockSpec((tm, MEM_DIM), lambda i: (i, 0)),
            pl.BlockSpec((N_MEM, MEM_DIM), lambda i: (0, 0)),
            pl.BlockSpec((N_MEM, MEM_DIM), lambda i: (0, 1)),
        ],
        out_specs=pl.BlockSpec((tm, MEM_DIM), lambda i: (i, 0)),
        compiler_params=_cparams("parallel"),
        name="mem_prompt",
    )(q, mkv, mkv)


def _mem_sample_kernel(q_ref, mk_ref, mv_ref, o_ref, *, bb):
    scale = MEM_HEAD_DIM ** -0.5
    pairs = [(b, h) for b in range(bb) for h in range(MEM_HEADS)]
    rows = lambda h: pl.ds(h, N_MEM, stride=MEM_HEADS)
    cols = lambda h: slice(h * MEM_HEAD_DIM, (h + 1) * MEM_HEAD_DIM)
    q = [q_ref[b] for b in range(bb)]
    x = [_dot_nt(q[b][:, cols(h)], mk_ref[b, rows(h), :].astype(BF16)) * scale for b, h in pairs]
    m = [jnp.max(xi, axis=-1, keepdims=True) for xi in x]
    p = [jnp.exp(xi - mi) for xi, mi in zip(x, m)]
    inv = [1.0 / jnp.sum(pi, axis=-1, keepdims=True) for pi in p]
    o = [_dot((pi * ii).astype(BF16), mv_ref[b, rows(h), :].astype(BF16)) for (b, h), pi, ii in zip(pairs, p, inv)]
    for b in range(bb):
        o_ref[b] = jnp.concatenate(o[b * MEM_HEADS:(b + 1) * MEM_HEADS], axis=-1).astype(BF16)


def _mem_sample(q, mk, mv, *, bb=8):
    b, t, _ = q.shape
    blk = lambda i: (i, 0, 0)
    return pl.pallas_call(
        functools.partial(_mem_sample_kernel, bb=bb),
        out_shape=jax.ShapeDtypeStruct((b, t, MEM_DIM), BF16),
        grid=(b // bb,),
        in_specs=[
            pl.BlockSpec((bb, t, MEM_DIM), blk),
            pl.BlockSpec((bb, N_MEM * MEM_HEADS, MEM_HEAD_DIM), blk),
            pl.BlockSpec((bb, N_MEM * MEM_HEADS, MEM_HEAD_DIM), blk),
        ],
        out_specs=pl.BlockSpec((bb, t, MEM_DIM), blk),
        compiler_params=_cparams("parallel"),
        name="mem_sample",
    )(q, mk, mv)


def _head_sum(x, ones_ref):
    hi = x.astype(BF16)
    lo = (x - hi.astype(F32)).astype(BF16)
    ones = ones_ref[...]
    w = ones.shape[0]
    return jnp.concatenate([_dot(hi[:, c:c + w], ones) + _dot(lo[:, c:c + w], ones) for c in range(0, x.shape[1], w)],
                           axis=1)


def _token_shift(x_ref, pre_ref, start_ref, mu_ref, *, seq, tm, tile):
    x = x_ref[...]
    row = lax.broadcasted_iota(jnp.int32, x.shape, 0)
    shifted = pltpu.roll(x, 1, axis=0)
    if seq >= tm:
        is_start = (tile * tm) % seq == 0
        first_prev = jnp.where(is_start, start_ref[0], pre_ref[SUBLANES - 1:SUBLANES, :])
        prev = jnp.where(row == 0, first_prev, shifted)
    else:
        prev = jnp.where(row % seq == 0, start_ref[...], shifted)
    return x + mu_ref[...] * (prev - x)


def _rwkv_features(xs, w0_ref, ww2_ref, a0_ref, aw2_ref, gw2_ref, kk_ref, ka_ref, rk_ref, ones_ref):
    d = RWKV_DIM
    r = xs[:, 0:d]
    k = xs[:, d:2 * d]
    v = xs[:, 2 * d:3 * d]
    lw = xs[:, 3 * d:3 * d + LORA_W]
    la = xs[:, 3 * d + LORA_W:3 * d + LORA_W + LORA_A]
    lg = xs[:, 3 * d + LORA_W + LORA_A:]
    wpre = w0_ref[...] + _dot_hi(jnp.tanh(lw), ww2_ref[...])
    w = -jax.nn.softplus(-wpre) - 0.5
    log_decay = -jnp.exp(w)
    a = jax.nn.sigmoid(a0_ref[...] + _dot_hi(la, aw2_ref[...]))
    g = _dot_hi(jax.nn.sigmoid(lg), gw2_ref[...])
    kk = k * kk_ref[...]
    kk = kk / jnp.maximum(jnp.sqrt(_head_sum(kk * kk, ones_ref)), 1e-12)
    kh = k * (1.0 + (a - 1.0) * ka_ref[...])
    bonus = _head_sum(r * kh * rk_ref[...], ones_ref) * v
    return r, log_decay, kh, v, kk, kk * a, g, bonus


def _rwkv_prep_kernel(x_ref, pre_ref, start_ref, mu_ref, w0_ref, ww2_ref, a0_ref, aw2_ref, gw2_ref, kk_ref, ka_ref,
                      rk_ref, ones_ref, *outs, seq, tm):
    xs = _token_shift(x_ref, pre_ref, start_ref, mu_ref, seq=seq, tm=tm, tile=pl.program_id(0))
    feats = _rwkv_features(xs, w0_ref, ww2_ref, a0_ref, aw2_ref, gw2_ref, kk_ref, ka_ref, rk_ref, ones_ref)
    for o_ref, val in zip(outs, feats):
        o_ref[...] = val


def _rwkv_prep(x, shift0, p, ones, *, seq, tm=256):
    m = x.shape[0]
    row = lambda i: (i, 0)
    const = lambda i: (0, 0)
    vec = lambda n: pl.BlockSpec((1, n), const)
    out = jax.ShapeDtypeStruct((m, RWKV_DIM), F32)
    if seq >= tm:
        assert seq % tm == 0
        start = shift0
        start_spec = pl.BlockSpec((1, 1, RWKV_IN), lambda i: ((i * tm) // seq, 0, 0))
    else:
        assert tm % seq == 0
        start = jnp.repeat(shift0[:, 0], seq, axis=0)
        start_spec = pl.BlockSpec((tm, RWKV_IN), row)
    pre_blocks = tm // SUBLANES
    return pl.pallas_call(
        functools.partial(_rwkv_prep_kernel, seq=seq, tm=tm),
        out_shape=(out,) * 8,
        grid=(m // tm,),
        in_specs=[
            pl.BlockSpec((tm, RWKV_IN), row),
            pl.BlockSpec((SUBLANES, RWKV_IN), lambda i: (jnp.maximum(i * pre_blocks - 1, 0), 0)),
            start_spec,
            vec(RWKV_IN), vec(RWKV_DIM),
            pl.BlockSpec((LORA_W, RWKV_DIM), const),
            vec(RWKV_DIM),
            pl.BlockSpec((LORA_A, RWKV_DIM), const),
            pl.BlockSpec((LORA_G, RWKV_DIM), const),
            vec(RWKV_DIM), vec(RWKV_DIM), vec(RWKV_DIM),
            pl.BlockSpec(ones.shape, const),
        ],
        out_specs=(pl.BlockSpec((tm, RWKV_DIM), row),) * 8,
        compiler_params=_cparams("parallel"),
        name="rwkv_prep",
    )(x, x, start, p["mu"], p["w0"], p["w_w2"], p["a0"], p["a_w2"], p["g_w2"], p["k_k"], p["k_a"], p["r_k"], ones)


def _rwkv_lanes_kernel(r_ref, lw_ref, k_ref, v_ref, kk_ref, kka_ref, s_ref, y_ref, so_ref, v_scr, y_scr, *, t):
    n = RWKV_HEAD_DIM
    nb = s_ref.shape[-1]
    heads = range(2)

    def token_major(ref, j):
        return ref[pl.ds(j, nb, stride=t), :].T

    for j in range(t):
        v_scr[...] = token_major(v_ref, j)
        r_t, k_t, kka_t = (token_major(ref, j) for ref in (r_ref, k_ref, kka_ref))
        neg_kk_t = -token_major(kk_ref, j)
        w_t = jnp.exp(token_major(lw_ref, j))
        src = s_ref if j == 0 else so_ref

        def value_group(g, carry):
            rows = pl.multiple_of(g * SUBLANES, SUBLANES)
            ys = [[] for _ in heads]
            vg = [v_scr[pl.ds(h * n + rows, SUBLANES), :] for h in heads]
            for i in range(SUBLANES):
                for h in heads:
                    f = slice(h * n, (h + 1) * n)
                    s = src[h, rows + i]
                    sa = jnp.sum(s * neg_kk_t[f], axis=0, keepdims=True)
                    s = s * w_t[f] + sa * kka_t[f] + vg[h][i:i + 1] * k_t[f]
                    so_ref[h, rows + i] = s
                    ys[h].append(jnp.sum(s * r_t[f], axis=0, keepdims=True))
            for h in heads:
                y_scr[pl.ds(h * n + rows, SUBLANES), :] = jnp.concatenate(ys[h], axis=0)
            return carry

        lax.fori_loop(0, n // SUBLANES, value_group, 0)
        y_ref[j] = y_scr[...].T


def _rwkv_lanes(r, lw, k, v, kk, kka, s0, *, t):
    m, d = r.shape
    nb = m // t
    n = RWKV_HEAD_DIM
    assert nb == LANES, "one batch per lane"
    tok = pl.BlockSpec((m, LANES), lambda p: (0, p))
    st = pl.BlockSpec((2, n, n, nb), lambda p: (p, 0, 0, 0))
    return pl.pallas_call(
        functools.partial(_rwkv_lanes_kernel, t=t),
        out_shape=(jax.ShapeDtypeStruct((t, nb, d), F32), jax.ShapeDtypeStruct((RWKV_HEADS, n, n, nb), F32)),
        grid=(RWKV_HEADS // 2,),
        in_specs=[tok] * 6 + [st],
        out_specs=(pl.BlockSpec((t, nb, LANES), lambda p: (0, 0, p)), st),
        scratch_shapes=[pltpu.VMEM((LANES, nb), F32), pltpu.VMEM((LANES, nb), F32)],
        compiler_params=_cparams("parallel"),
        name="rwkv_lanes",
    )(r, lw, k, v, kk, kka, s0)


CHUNK = 64
GROUP_HEADS = 4
GROUP_W = GROUP_HEADS * RWKV_HEAD_DIM
N_GROUPS = RWKV_HEADS // GROUP_HEADS
(MASK_SAME, MASK_STRICT, MASK_INCL, MASK_LEVEL0) = (0, 1, 2, 3)
N_LEVELS = int(math.log2(CHUNK))


def _chunk_masks():
    i = np.arange(GROUP_W)
    same = (i[:, None] // CHUNK) == (i[None, :] // CHUNK)
    masks = [same, same & (i[None, :] < i[:, None]), same & (i[None, :] <= i[:, None])]
    for lvl in range(N_LEVELS):
        m = 1 << lvl
        masks.append(((i[:, None] // (2 * m)) == (i[None, :] // (2 * m))) & ((i[:, None] // m) != (i[None, :] // m))
                     & (i[None, :] < i[:, None]))
    return np.stack(masks).astype(np.float32)


def _rwkv_prompt_kernel(x_ref, pre_ref, start_ref, mu_ref, w0_ref, ww2_ref, a0_ref, aw2_ref, gw2_ref, kk_p_ref,
                        ka_ref, rk_ref, ones_ref, lnw_ref, lnb_ref, st0_ref, tri_ref, eye_ref, mask_ref,
                        *rest, seq, tm, n_tiles, n_riders):
    rider_in, (o_ref, sto_ref), rider_out = rest[:n_riders], rest[n_riders:n_riders + 2], rest[n_riders + 2:-11]
    st_scr, xs_ref, y_ref, r_set, lw_set, k_set, v_set, kk_set, kka_set, g_set, bonus_set = rest[-11:]
    for src, dst in zip(rider_in, rider_out):
        dst[...] = src[...].astype(BF16)
    step = pl.program_id(0)
    cur = (step + 1) % 2
    nxt = step % 2
    sets = (r_set, lw_set, k_set, v_set, kk_set, kka_set, g_set, bonus_set)
    r_ref, lw_ref, k_ref, v_ref, kk_ref, kka_ref = (s.at[cur] for s in sets[:6])

    @pl.when(step == 0)
    def _():
        st_scr[...] = st0_ref[...]
        for s in sets:
            s[1] = jnp.zeros(s.shape[1:], F32)

    xs_ref[...] = _token_shift(x_ref, pre_ref, start_ref, mu_ref, seq=seq, tm=tm, tile=jnp.minimum(step, n_tiles - 1))
    n_chunks = tm // CHUNK

    piece = 2 * CHUNK
    def features(c):
        rows = slice(c * piece, (c + 1) * piece)
        feats = _rwkv_features(xs_ref[rows, :], w0_ref, ww2_ref, a0_ref, aw2_ref, gw2_ref, kk_p_ref, ka_ref, rk_ref,
                               ones_ref)
        for s, val in zip(sets, feats):
            s[nxt, rows, :] = val

    eye = eye_ref[...]
    tri = tri_ref[...]
    tile_rows = lambda x: jnp.concatenate([x] * GROUP_HEADS, axis=0)
    block_diag = lambda x: (tile_rows(x) * mask_ref[MASK_SAME]).astype(BF16)

    chains = [(slice(c * CHUNK, (c + 1) * CHUNK), slice(g * GROUP_W, (g + 1) * GROUP_W))
              for c in range(n_chunks) for g in range(N_GROUPS)]
    each = lambda f, *cols: [f(*args) for args in zip(*cols)]
    same, strict, incl = mask_ref[MASK_SAME], mask_ref[MASK_STRICT], mask_ref[MASK_INCL]

    def cum_decay(lw):
        h1 = lw.astype(BF16)
        r1 = lw - h1.astype(F32)
        h2 = r1.astype(BF16)
        h3 = (r1 - h2.astype(F32)).astype(BF16)
        return _dot(tri, h1) + _dot(tri, h2) + _dot(tri, h3)

    pending = list(range(tm // piece))

    def next_features():
        if pending:
            features(pending.pop(0))

    all_chains = chains
    part = max(len(all_chains) // 4, N_GROUPS)
    st = [st_scr[g] for g in range(N_GROUPS)]
    for first in range(0, len(all_chains), part):
        chains = all_chains[first:first + part]
        lw = [lw_ref[rows, sl] for rows, sl in chains]
        kka = [kka_ref[rows, sl] for rows, sl in chains]
        k = [k_ref[rows, sl] for rows, sl in chains]
        cum = each(cum_decay, lw)
        cum_last = each(lambda c: c[CHUNK - 1:CHUNK, :], cum)
        p_inv = each(lambda c: jnp.exp(-c), cum)
        p_tail = each(lambda c, cl: jnp.exp(cl - c), cum, cum_last)
        a_bd = [block_diag(-kk_ref[rows, sl] * jnp.exp(c - l)) for (rows, sl), c, l in zip(chains, cum, lw)]
        r_f = [tile_rows(r_ref[rows, sl] * jnp.exp(c)) * same for (rows, sl), c in zip(chains, cum)]
        r_bd = each(lambda x: x.astype(BF16), r_f)
        v_bd = [block_diag(v_ref[rows, sl]) for rows, sl in chains]
        b_rep = each(lambda x, p: tile_rows((x * p).astype(BF16)), kka, p_inv)
        k_rep = each(lambda x, p: tile_rows((x * p).astype(BF16)), k, p_inv)
        bh_rep = each(lambda x, p: tile_rows(x * p), kka, p_tail)
        kh_rep = each(lambda x, p: tile_rows(x * p), k, p_tail)

        l_ab_f = each(lambda a, b: _dot_nt(a, b) * strict, a_bd, b_rep)
        l_ab = each(lambda x: x.astype(BF16), l_ab_f)
        l_ak = each(lambda a, b: (_dot_nt(a, b) * strict).astype(BF16), a_bd, k_rep)
        m_rb = each(lambda a, b: (_dot_nt(a, b) * incl).astype(BF16), r_bd, b_rep)
        m_rk = each(lambda a, b: (_dot_nt(a, b) * incl).astype(BF16), r_bd, k_rep)
        bh_t = each(lambda x: (x.T * same).astype(BF16), bh_rep)
        kh_t = each(lambda x: (x.T * same).astype(BF16), kh_rep)

        d = each(lambda l: eye + l * mask_ref[MASK_LEVEL0], l_ab_f)
        for lvl in range(1, N_LEVELS):
            d_b = each(lambda x: x.astype(BF16), d)
            x = each(lambda l, db: (_dot(l, db) * mask_ref[MASK_LEVEL0 + lvl]).astype(BF16), l_ab, d_b)
            next_features()
            d = each(lambda dd, db, xx: dd + _dot(db, xx), d, d_b, x)
        t_b = each(lambda x: x.astype(BF16), d)
        while pending:
            next_features()

        wm = each(lambda a, b, vv: _dot(jnp.concatenate([a, b], axis=0), vv), l_ak, m_rk, v_bd)
        twa = each(lambda t, w, a: _dot(t, jnp.concatenate([w[:GROUP_W].astype(BF16), a], axis=1)).astype(BF16),
                   t_b, wm, a_bd)
        ry = each(_dot, m_rb, twa)
        mn = each(_dot, bh_t, twa)
        khv = each(_dot, kh_t, v_bd)
        y0 = each(lambda a, w: a[:, :GROUP_W] + w[GROUP_W:], ry, wm)
        n_x = each(lambda a, b: a[:, :GROUP_W] + b, mn, khv)
        mr = each(lambda a, cl, rf, b: jnp.concatenate(
            [(eye * jnp.exp(cl) + a[:, GROUP_W:]).astype(BF16), (rf + b[:, GROUP_W:]).astype(BF16)], axis=0),
            mn, cum_last, r_f, ry)

        for i, (rows, sl) in enumerate(chains):
            g = (first + i) % N_GROUPS
            ys = _dot(mr[i], st[g].astype(BF16))
            st[g] = ys[:GROUP_W] + n_x[i]
            y_bd = ys[GROUP_W:] + y0[i]
            y = y_bd[0:CHUNK]
            for h in range(1, GROUP_HEADS):
                y = y + y_bd[h * CHUNK:(h + 1) * CHUNK]
            y_ref[rows, sl] = y
    for g in range(N_GROUPS):
        st_scr[g] = st[g]
    o_ref[...] = _rwkv_output(y_ref[...], bonus_set[cur], g_set[cur], lnw_ref, lnb_ref, ones_ref)

    @pl.when(step == pl.num_programs(0) - 1)
    def _():
        sto_ref[...] = st_scr[...]


def _rwkv_prompt(x, shift0, s0, p, ones, *, riders=(), chunks_per_step=4):
    t = x.shape[0]
    d = RWKV_DIM
    n = RWKV_HEAD_DIM
    tt = CHUNK * chunks_per_step
    assert CHUNK == n and t % tt == 0
    st0 = jnp.einsum("ghvk,hj->ghkjv", s0.reshape(N_GROUPS, GROUP_HEADS, n, n), jnp.eye(GROUP_HEADS, dtype=F32))
    st0 = st0.reshape(N_GROUPS, GROUP_W, GROUP_W)
    tri = jnp.asarray(np.tril(np.ones((CHUNK, CHUNK), np.float32)), BF16)
    eye = jnp.eye(GROUP_W, dtype=F32)
    masks = jnp.asarray(_chunk_masks())
    const = lambda c: (0, 0)
    vec = lambda width: pl.BlockSpec((1, width), const)
    st_spec = pl.BlockSpec((N_GROUPS, GROUP_W, GROUP_W), lambda c: (0, 0, 0))
    feature_set = pltpu.VMEM((2, tt, d), F32)
    pre_blocks = tt // SUBLANES
    n_tiles = t // tt
    fill = lambda c: jnp.minimum(c, n_tiles - 1)
    rider_specs = _rider_specs(riders, n_tiles + 1, lambda c: c)
    o, st, *rounded = pl.pallas_call(
        functools.partial(_rwkv_prompt_kernel, seq=t, tm=tt, n_tiles=n_tiles, n_riders=len(riders)),
        out_shape=[jax.ShapeDtypeStruct((t, d), BF16), jax.ShapeDtypeStruct((N_GROUPS, GROUP_W, GROUP_W), F32)]
        + [jax.ShapeDtypeStruct(arr.shape, BF16) for arr in riders],
        grid=(n_tiles + 1,),
        in_specs=[
            pl.BlockSpec((tt, RWKV_IN), lambda c: (fill(c), 0)),
            pl.BlockSpec((SUBLANES, RWKV_IN), lambda c: (jnp.maximum(fill(c) * pre_blocks - 1, 0), 0)),
            pl.BlockSpec((1, 1, RWKV_IN), lambda c: (0, 0, 0)),
            vec(RWKV_IN), vec(d),
            pl.BlockSpec((LORA_W, d), const),
            vec(d),
            pl.BlockSpec((LORA_A, d), const),
            pl.BlockSpec((LORA_G, d), const),
            vec(d), vec(d), vec(d),
            pl.BlockSpec(ones.shape, const),
            vec(d), vec(d),
            st_spec,
            pl.BlockSpec((CHUNK, CHUNK), const),
            pl.BlockSpec((GROUP_W, GROUP_W), const),
            pl.BlockSpec(masks.shape, lambda c: (0, 0, 0)),
        ] + rider_specs,
        out_specs=[pl.BlockSpec((tt, d), lambda c: (jnp.maximum(c - 1, 0), 0)), st_spec] + rider_specs,
        scratch_shapes=[pltpu.VMEM((N_GROUPS, GROUP_W, GROUP_W), F32), pltpu.VMEM((tt, RWKV_IN), F32),
                        pltpu.VMEM((tt, d), F32)] + [feature_set] * 8,
        compiler_params=_cparams("arbitrary"),
        name="rwkv_prompt",
    )(x, x, shift0, p["mu"], p["w0"], p["w_w2"], p["a0"], p["a_w2"], p["g_w2"], p["k_k"], p["k_a"], p["r_k"], ones,
      p["ln_w"], p["ln_b"], st0, tri, eye, masks, *riders)
    st5 = st.reshape(N_GROUPS, GROUP_HEADS, n, GROUP_HEADS, n)
    s_new = jnp.einsum("ghkjv,hj->ghvk", st5, jnp.eye(GROUP_HEADS, dtype=F32)).reshape(RWKV_HEADS, n, n)
    return o, s_new, rounded


def _rwkv_output(y, bonus, gate, lnw_ref, lnb_ref, ones_ref):
    inv_n = 1.0 / RWKV_HEAD_DIM
    mu = _head_sum(y, ones_ref) * inv_n
    dlt = y - mu
    var = _head_sum(dlt * dlt, ones_ref) * inv_n
    yn = dlt * lax.rsqrt(var + GN_EPS) * lnw_ref[...] + lnb_ref[...]
    return ((yn + bonus) * gate).astype(BF16)


def _rwkv_post_kernel(y_ref, bonus_ref, g_ref, lnw_ref, lnb_ref, ones_ref, o_ref):
    o_ref[...] = _rwkv_output(y_ref[...], bonus_ref[...], g_ref[...], lnw_ref, lnb_ref, ones_ref)


def _rwkv_post(y, bonus, g, lnw, lnb, ones, *, tm=256):
    m = y.shape[0]
    row = lambda i: (i, 0)
    const = lambda i: (0, 0)
    tile = pl.BlockSpec((tm, RWKV_DIM), row)
    return pl.pallas_call(
        _rwkv_post_kernel,
        out_shape=jax.ShapeDtypeStruct((m, RWKV_DIM), BF16),
        grid=(m // tm,),
        in_specs=[tile, tile, tile, pl.BlockSpec((1, RWKV_DIM), const), pl.BlockSpec((1, RWKV_DIM), const),
                  pl.BlockSpec(ones.shape, const)],
        out_specs=tile,
        compiler_params=_cparams("parallel"),
        name="rwkv_post",
    )(y, bonus, g, lnw, lnb, ones)


GATE_BLOCK = math.gcd(PROJ_DIM, D_MODEL)


def _merge_kernel(h_ref, u_ref, os_ref, or_ref, om_ref, wos_ref, wor_ref, wom_ref, wout_ref, *rest, parts):
    gate_refs, (o_ref, acc_ref) = rest[:N_BRANCH * parts], rest[N_BRANCH * parts:]
    j = pl.program_id(1)

    @pl.when(j == 0)
    def _():
        acc_ref[...] = jnp.zeros_like(acc_ref)

    u = u_ref[...]
    merged = None
    for b, (x_ref, w_ref) in enumerate(((os_ref, wos_ref), (or_ref, wor_ref), (om_ref, wom_ref))):
        gate = jnp.concatenate([_dot(u, gate_refs[b * parts + c][...]) for c in range(parts)], axis=1)
        term = jax.nn.sigmoid(gate) * _dot(x_ref[...], w_ref[...])
        merged = term if merged is None else merged + term
    acc_ref[...] += _dot(merged.astype(BF16), wout_ref[...])

    @pl.when(j == pl.num_programs(1) - 1)
    def _():
        o_ref[...] = h_ref[...] + acc_ref[...]


def _merge(h, u, o_swa, o_rw, o_mem, w_in, wo_swa, wo_rw, wo_mem, w_out, *, tm=512, tn=512):
    m, d = h.shape
    nt = d // tn
    parts = tn // GATE_BLOCK
    g0 = PROJ_DIM // GATE_BLOCK
    row = lambda i, j: (i, 0)
    col = lambda i, j: (0, j)
    gate_specs = [pl.BlockSpec((d, GATE_BLOCK), functools.partial(
        lambda i, j, off: (0, off + j * parts), off=g0 + b * (d // GATE_BLOCK) + c))
        for b in range(N_BRANCH) for c in range(parts)]
    return pl.pallas_call(
        functools.partial(_merge_kernel, parts=parts),
        out_shape=jax.ShapeDtypeStruct((m, d), F32),
        grid=(m // tm, nt),
        in_specs=[
            pl.BlockSpec((tm, d), row),
            pl.BlockSpec((tm, d), row),
            pl.BlockSpec((tm, SWA_Q_DIM), row),
            pl.BlockSpec((tm, RWKV_DIM), row),
            pl.BlockSpec((tm, MEM_DIM), row),
            pl.BlockSpec((SWA_Q_DIM, tn), col),
            pl.BlockSpec((RWKV_DIM, tn), col),
            pl.BlockSpec((MEM_DIM, tn), col),
            pl.BlockSpec((tn, d), lambda i, j: (j, 0)),
        ] + gate_specs,
        out_specs=pl.BlockSpec((tm, d), row),
        scratch_shapes=[pltpu.VMEM((tm, d), F32)],
        compiler_params=_cparams("parallel", "arbitrary"),
        name="merge",
    )(h, u, o_swa, o_rw, o_mem, wo_swa, wo_rw, wo_mem, w_out, *([w_in] * (N_BRANCH * parts)))


def _t5_bucket(dist):
    max_exact = N_BUCKETS // 2
    d = np.maximum(dist, 0)
    log_ratio = (np.log(np.maximum(d, 1).astype(np.float32) / np.float32(max_exact))
                 / np.float32(math.log(MAX_DISTANCE / max_exact)))
    large = np.minimum(max_exact + (log_ratio * (N_BUCKETS - max_exact)).astype(np.int32), N_BUCKETS - 1)
    return np.where(d < max_exact, d, large).astype(np.int32)


def _rel_bias(table, dist):
    onehot = np.eye(N_BUCKETS, dtype=np.float32)[_t5_bucket(dist).reshape(-1)]
    bias = jnp.einsum("nb,bh->hn", jnp.asarray(onehot), table, precision=lax.Precision.HIGHEST)
    return bias.reshape(SWA_HEADS, *dist.shape)


def _rwkv_branch(xr, shift0, s0, p, ones, riders=()):
    b, t, _ = xr.shape
    flat = lambda z: z.reshape(b * t, z.shape[-1])
    if b == 1:
        o, s_new, rounded = _rwkv_prompt(flat(xr), shift0, s0[0], p, ones, riders=riders)
        return o, s_new[None], rounded
    assert b == LANES and not riders, "short sequences are batched one per lane"
    r, w, k, v, kk, kka, g, bonus = _rwkv_prep(flat(xr), shift0, p, ones, seq=t)
    y, s_new = _rwkv_lanes(r, w, k, v, kk, kka, jnp.transpose(s0, (1, 2, 3, 0)), t=t)
    o = _rwkv_post(flat(jnp.transpose(y, (1, 0, 2))), bonus, g, p["ln_w"], p["ln_b"], ones)
    return o, jnp.transpose(s_new, (3, 0, 1, 2))


def kernel(x_prompt, mem_prompt, x_sample, cache_swa_k, cache_swa_v, state_rwkv, state_rwkv_shift, cache_mem_k, cache_mem_v, ffn1_norm, ffn1_wi, ffn1_wo, mix_norm, w_in, swa_sinks, rel_bias_table, rwkv_mu, rwkv_w0, rwkv_w_w2, rwkv_a0, rwkv_a_w2, rwkv_g_w2, rwkv_k_k, rwkv_k_a, rwkv_r_k, rwkv_ln_w, rwkv_ln_b, mem_norm, w_mem_kv, w_o_swa, w_o_rwkv, w_o_mem, w_out, ffn2_norm, ffn2_wi, ffn2_wo, final_norm):
    assert ffn1_wi.shape[0] == 1, "single-layer trunk"
    bp, tp, d = x_prompt.shape
    bs, ts, _ = x_sample.shape
    assert bp == 1
    row = lambda z: z.reshape(1, -1).astype(F32)

    g1, gm, g2, gf = row(ffn1_norm[0]), row(mix_norm[0]), row(ffn2_norm[0]), row(final_norm)
    rp = {
        "mu": row(rwkv_mu[0]), "w0": row(rwkv_w0[0]), "w_w2": rwkv_w_w2[0], "a0": row(rwkv_a0[0]),
        "a_w2": rwkv_a_w2[0], "g_w2": rwkv_g_w2[0], "k_k": row(rwkv_k_k[0]), "k_a": row(rwkv_k_a[0]),
        "r_k": row(rwkv_r_k[0]), "ln_w": row(rwkv_ln_w[0]), "ln_b": row(rwkv_ln_b[0]),
    }
    seg = np.arange(GROUP_W) // RWKV_HEAD_DIM
    ones = jnp.asarray(seg[:, None] == seg[None, :], dtype=BF16)
    sinks = swa_sinks[0].astype(F32)
    table = rel_bias_table.astype(F32)

    xp = x_prompt.reshape(tp, d)
    xs = x_sample.reshape(bs * ts, d)
    hs, wg1, wu1, wo1 = _ffn(xs, g1, ffn1_wi[0], ffn1_wo[0], gf, final_norm=False, emit_weights=True, tf=256)
    hp, w_in_b = _ffn(xp, g1, (wg1, wu1), wo1, gf, final_norm=False, riders=(w_in[0],))
    w_q = w_in_b[:, :SWA_Q_DIM].reshape(d, SWA_KV_HEADS, SWA_GROUP, SWA_HEAD_DIM).transpose(0, 2, 1, 3).reshape(d, SWA_Q_DIM)
    qp, kvp, xrp, qmp, up = _inproj(hp, gm, w_q, w_in_b)
    qs, kvs, xrs, qms, us = _inproj(hs, gm, w_q, w_in_b)

    w = WINDOW
    dist_p = np.arange(w)[:, None] + w - np.arange(2 * w)[None, :]
    bias_p = _rel_bias(table, dist_p).reshape(SWA_KV_HEADS, SWA_GROUP, w, 2 * w).transpose(1, 0, 2, 3)
    bias_p = bias_p.reshape(SWA_GROUP, SWA_KV_HEADS * w, 2 * w)
    sink_p = jnp.repeat(sinks.reshape(SWA_KV_HEADS, SWA_GROUP).T, w, axis=1).reshape(SWA_GROUP, SWA_KV_HEADS * w, 1)
    o_swa_p = _swa_prompt(qp, kvp, bias_p, sink_p)

    wbuf = cache_swa_k.shape[2]
    dist_s = np.arange(ts)[:, None] + wbuf - np.arange(wbuf + ts)[None, :]
    bias_s = _rel_bias(table, dist_s).reshape(SWA_HEADS * ts, wbuf + ts)
    sink_rows = jnp.repeat(sinks, ts).reshape(SWA_HEADS * ts, 1)
    qs_gt = qs.reshape(bs, ts, SWA_GROUP, SWA_KV_DIM).transpose(0, 2, 1, 3).reshape(bs, SWA_GROUP * ts, SWA_KV_DIM)
    kbuf = cache_swa_k[0].reshape(bs, wbuf, SWA_KV_DIM).transpose(0, 2, 1)
    vbuf = cache_swa_v[0].reshape(bs, wbuf, SWA_KV_DIM).transpose(0, 2, 1)
    o_swa_s, knew_t, vnew_t = _swa_sample(qs_gt, kvs.reshape(bs, ts, 2 * SWA_KV_DIM), kbuf, vbuf,
                                          bias_s[:, :wbuf], bias_s[:, wbuf:], sink_rows)
    o_swa_s = o_swa_s.reshape(bs, SWA_GROUP, ts, SWA_KV_DIM).transpose(0, 2, 1, 3).reshape(bs * ts, SWA_Q_DIM)

    zero_shift = jnp.zeros((bp, 1, RWKV_IN), F32)
    zero_state = jnp.zeros((bp, RWKV_HEADS, RWKV_HEAD_DIM, RWKV_HEAD_DIM), F32)
    o_rw_p, state_p, (wi2, wo2, wo_swa, wo_rw, wo_mem, w_out_b) = _rwkv_branch(
        xrp.reshape(bp, tp, RWKV_IN), zero_shift, zero_state, rp, ones,
        riders=(ffn2_wi[0], ffn2_wo[0], w_o_swa[0], w_o_rwkv[0], w_o_mem[0], w_out[0]))
    wo_swa = wo_swa.reshape(SWA_KV_HEADS, SWA_GROUP, SWA_HEAD_DIM, d).transpose(1, 0, 2, 3).reshape(SWA_Q_DIM, d)
    o_rw_s, state_s = _rwkv_branch(xrs.reshape(bs, ts, RWKV_IN), state_rwkv_shift[0], state_rwkv[0], rp, ones)

    mkv = _norm_matmul(mem_prompt.reshape(N_MEM, d), row(mem_norm[0]), w_mem_kv[0].astype(BF16))
    o_mem_p = _mem_prompt(qmp, mkv)
    o_mem_s = _mem_sample(qms.reshape(bs, ts, MEM_DIM), cache_mem_k[0].reshape(bs, N_MEM * MEM_HEADS, MEM_HEAD_DIM),
                          cache_mem_v[0].reshape(bs, N_MEM * MEM_HEADS, MEM_HEAD_DIM)).reshape(bs * ts, MEM_DIM)

    hp = _merge(hp, up, o_swa_p, o_rw_p, o_mem_p, w_in_b, wo_swa, wo_rw, wo_mem, w_out_b)
    hs = _merge(hs, us, o_swa_s, o_rw_s, o_mem_s, w_in_b, wo_swa, wo_rw, wo_mem, w_out_b)
    y_prompt = _ffn(hp, g2, wi2, wo2, gf, final_norm=True).reshape(bp, tp, d)
    y_sample = _ffn(hs, g2, wi2, wo2, gf, final_norm=True).reshape(bs, ts, d)

    wp = min(w, tp)
    p_k = kvp[tp - wp:, :SWA_KV_DIM].reshape(1, bp, wp, SWA_KV_HEADS, SWA_HEAD_DIM)
    p_v = kvp[tp - wp:, SWA_KV_DIM:].reshape(1, bp, wp, SWA_KV_HEADS, SWA_HEAD_DIM)
    p_mk = mkv[:, :MEM_DIM].reshape(1, bp, N_MEM, MEM_HEADS, MEM_HEAD_DIM)
    p_mv = mkv[:, MEM_DIM:].reshape(1, bp, N_MEM, MEM_HEADS, MEM_HEAD_DIM)
    s_k = knew_t.transpose(0, 2, 1).reshape(1, bs, wbuf, SWA_KV_HEADS, SWA_HEAD_DIM)
    s_v = vnew_t.transpose(0, 2, 1).reshape(1, bs, wbuf, SWA_KV_HEADS, SWA_HEAD_DIM)
    return (y_prompt, y_sample,
            p_k, p_v, state_p[None], xrp[tp - 1:].reshape(1, bp, 1, RWKV_IN), p_mk, p_mv,
            s_k, s_v, state_s[None], xrs.reshape(bs, ts, RWKV_IN)[:, ts - 1:][None])
```
